```python
import math
import jax, jax.numpy as jnp
from jax import lax
import numpy as np

D_MODEL = 4096
BATCH = 8
SEQ = 4096
DEPTH = 2

D_MIX = D_MODEL
D_POOL = D_MIX // 4
D_ATTN = D_MIX // 2
D_SSM = D_MIX - D_POOL - D_ATTN
POOL_WINDOWS = (2, 4, 8, 16)
N_POOL_GROUPS = len(POOL_WINDOWS)
POOL_GROUP = D_POOL // N_POOL_GROUPS
HEAD_DIM = 128
N_HEADS = D_ATTN // HEAD_DIM
Q_BLOCK = 128
SSM_GROUP = 16
N_SSM_GROUPS = D_SSM // SSM_GROUP
SSM_STATE = 64
D_IN = 2 * D_POOL + 4 * D_ATTN + 2 * D_SSM
EPS = 1e-6

kernel_name = "hybrid_pool_stickbreak_s5_parallel"

_SPLIT_SIZES = (D_POOL, D_POOL, D_ATTN, D_ATTN, D_ATTN, D_ATTN, D_SSM, D_SSM)
_SPLIT_OFFSETS = tuple(int(v) for v in np.cumsum(_SPLIT_SIZES)[:-1])


def rmsnorm(x, g):
    xf = x.astype(jnp.float32)
    y = xf * lax.rsqrt(jnp.mean(xf * xf, axis=-1, keepdims=True) + EPS)
    return y * g.astype(jnp.float32)


def pool_mixer(xp, w_pool, pool_scale):
    bsz, L, _ = xp.shape
    xg = xp.astype(jnp.float32).reshape(bsz, L, N_POOL_GROUPS, POOL_GROUP)
    csum = jnp.cumsum(xg, axis=1)
    pos = jnp.arange(1, L + 1, dtype=jnp.float32)[None, :, None]
    outs = []
    for g, w in enumerate(POOL_WINDOWS):
        cg = csum[:, :, g]
        c_lag = jnp.pad(cg, ((0, 0), (w, 0), (0, 0)))[:, :L]
        mean = (cg - c_lag) / jnp.minimum(pos, float(w))
        outs.append(mean - xg[:, :, g])
    pooled = jnp.stack(outs, axis=2)
    mixed = jnp.einsum('blgc,gcd->blgd', pooled, w_pool.astype(jnp.float32))
    return mixed.reshape(bsz, L, D_POOL) * pool_scale.astype(jnp.float32)


def stick_breaking_attention(q, k, v):
    bsz, L = q.shape[:2]
    nb = L // Q_BLOCK
    qf = q.astype(jnp.float32) * (HEAD_DIM ** -0.5)
    kh = k.astype(jnp.float32).transpose(0, 2, 1, 3)
    vh = v.astype(jnp.float32).transpose(0, 2, 1, 3)
    q_blocks = qf.reshape(bsz, nb, Q_BLOCK, N_HEADS, HEAD_DIM).transpose(1, 0, 3, 2, 4)
    k_pos = jnp.arange(L)

    def one_block(args):
        qb, start = args
        z = jnp.einsum('bhqd,bhkd->bhqk', qb, kh)
        q_pos = start + jnp.arange(Q_BLOCK)
        causal = k_pos[None, :] < q_pos[:, None]
        log_1m = jnp.where(causal, jax.nn.log_sigmoid(-z), 0.0)
        suffix = lax.cumsum(log_1m, axis=3, reverse=True) - log_1m
        wts = jnp.where(causal, jnp.exp(jax.nn.log_sigmoid(z) + suffix), 0.0)
        return jnp.einsum('bhqk,bhkd->bhqd', wts, vh)

    starts = jnp.arange(nb, dtype=jnp.int32) * Q_BLOCK
    out = lax.map(one_block, (q_blocks, starts))
    return out.transpose(1, 0, 3, 2, 4).reshape(bsz, L, D_ATTN)


def s5_mixer(u, lam_re, lam_im, log_dt, b_re, b_im, c_re, c_im, d_skip, w_glu, b_glu):
    bsz, L, _ = u.shape
    uf = u.astype(jnp.float32)
    lam = lax.complex(lam_re.astype(jnp.float32), lam_im.astype(jnp.float32))
    dt = jnp.exp(log_dt.astype(jnp.float32))[:, None]
    lam_bar = jnp.exp(lam * dt)
    b_mat = lax.complex(b_re.astype(jnp.float32), b_im.astype(jnp.float32))
    b_bar = ((lam_bar - 1.0) / lam)[..., None] * b_mat
    c_mat = lax.complex(c_re.astype(jnp.float32), c_im.astype(jnp.float32))
    ug = uf.reshape(bsz, L, N_SSM_GROUPS, SSM_GROUP)
    bu = jnp.einsum('blgc,gpc->blgp', ug.astype(jnp.complex64), b_bar)
    a = jnp.broadcast_to(lam_bar, bu.shape)

    def combine(e_prev, e_next):
        a1, x1 = e_prev
        a2, x2 = e_next
        return a2 * a1, a2 * x1 + x2

    _, states = lax.associative_scan(combine, (a, bu), axis=1)
    y = jnp.einsum('blgp,gcp->blgc', states, c_mat).real.reshape(bsz, L, D_SSM)
    y = y + d_skip.astype(jnp.float32) * uf
    h = jax.nn.gelu(y)
    val, gate = jnp.split(h @ w_glu.astype(jnp.float32) + b_glu.astype(jnp.float32), 2, axis=-1)
    return val * jax.nn.sigmoid(gate)


def hybrid_layer(x, ln_g, w_in, w_pool, pool_scale, lam_re, lam_im, log_dt,
                 b_re, b_im, c_re, c_im, d_skip, w_glu, b_glu, branch_g, w_out):
    bsz, L, _ = x.shape
    h = rmsnorm(x, ln_g).astype(x.dtype)
    proj = h @ w_in
    p_x, p_gate, q, k, v, a_gate, s_u, s_gate = jnp.split(proj, _SPLIT_OFFSETS, axis=-1)

    y_pool = pool_mixer(p_x, w_pool, pool_scale)
    shp = (bsz, L, N_HEADS, HEAD_DIM)
    y_attn = stick_breaking_attention(q.reshape(shp), k.reshape(shp), v.reshape(shp))
    y_ssm = s5_mixer(s_u, lam_re, lam_im, log_dt, b_re, b_im, c_re, c_im, d_skip, w_glu, b_glu)

    g_pool, g_attn, g_ssm = jnp.split(branch_g, [D_POOL, D_POOL + D_ATTN])
    y_pool = rmsnorm(y_pool, g_pool) * jax.nn.silu(p_gate.astype(jnp.float32))
    y_attn = rmsnorm(y_attn, g_attn) * jax.nn.silu(a_gate.astype(jnp.float32))
    y_ssm = rmsnorm(y_ssm, g_ssm) * jax.nn.silu(s_gate.astype(jnp.float32))
    y = jnp.concatenate([y_pool, y_attn, y_ssm], axis=-1).astype(x.dtype)
    return x + y @ w_out


def _fwd_setup_inputs(seed: int = 0) -> dict:
    key = jax.random.key(seed)
    ks = jax.random.split(key, 20)
    f32 = jnp.float32
    nrm = lambda k, shape, s: jax.random.normal(k, shape, f32) * s
    n_idx = jnp.arange(SSM_STATE, dtype=f32)
    return {
        "x": jax.random.normal(ks[0], (BATCH, SEQ, D_MODEL), f32),
        "ln_g": 1.0 + nrm(ks[1], (DEPTH, D_MODEL), 0.02),
        "w_in": nrm(ks[2], (DEPTH, D_MODEL, D_IN), D_MODEL ** -0.5),
        "w_pool": nrm(ks[3], (DEPTH, N_POOL_GROUPS, POOL_GROUP, POOL_GROUP), POOL_GROUP ** -0.5),
        "pool_scale": 1.0 + nrm(ks[4], (DEPTH, D_POOL), 0.1),
        "lam_re": -0.5 + nrm(ks[5], (DEPTH, N_SSM_GROUPS, SSM_STATE), 0.01),
        "lam_im": math.pi * n_idx + nrm(ks[6], (DEPTH, N_SSM_GROUPS, SSM_STATE), 0.01),
        "log_dt": jax.random.uniform(ks[7], (DEPTH, N_SSM_GROUPS), f32,
                                     math.log(1e-3), math.log(1e-1)),
        "b_re": nrm(ks[8], (DEPTH, N_SSM_GROUPS, SSM_STATE, SSM_GROUP), (2 * SSM_GROUP) ** -0.5),
        "b_im": nrm(ks[9], (DEPTH, N_SSM_GROUPS, SSM_STATE, SSM_GROUP), (2 * SSM_GROUP) ** -0.5),
        "c_re": nrm(ks[10], (DEPTH, N_SSM_GROUPS, SSM_GROUP, SSM_STATE), (2 * SSM_STATE) ** -0.5),
        "c_im": nrm(ks[11], (DEPTH, N_SSM_GROUPS, SSM_GROUP, SSM_STATE), (2 * SSM_STATE) ** -0.5),
        "d_skip": nrm(ks[12], (DEPTH, D_SSM), 1.0),
        "w_glu": nrm(ks[13], (DEPTH, D_SSM, 2 * D_SSM), D_SSM ** -0.5),
        "b_glu": nrm(ks[14], (DEPTH, 2 * D_SSM), 0.01),
        "branch_g": 1.0 + nrm(ks[15], (DEPTH, D_MIX), 0.02),
        "w_out": nrm(ks[16], (DEPTH, D_MIX, D_MODEL), (2 * DEPTH * D_MIX) ** -0.5),
        "final_g": 1.0 + nrm(ks[17], (D_MODEL,), 0.02),
    }


def _fwd_reference(x, ln_g, w_in, w_pool, pool_scale, lam_re, lam_im, log_dt,
              b_re, b_im, c_re, c_im, d_skip, w_glu, b_glu, branch_g, w_out, final_g):
    h = x
    for l in range(DEPTH):
        h = hybrid_layer(h, ln_g[l], w_in[l], w_pool[l], pool_scale[l], lam_re[l], lam_im[l],
                         log_dt[l], b_re[l], b_im[l], c_re[l], c_im[l], d_skip[l],
                         w_glu[l], b_glu[l], branch_g[l], w_out[l])
    return rmsnorm(h, final_g).astype(x.dtype)


import jax as _jax
import jax.numpy as _jnp

TWIN_FORMAT = 'train_step'
FWD_PARAMS = ['x', 'ln_g', 'w_in', 'w_pool', 'pool_scale', 'lam_re', 'lam_im', 'log_dt', 'b_re', 'b_im', 'c_re', 'c_im', 'd_skip', 'w_glu', 'b_glu', 'branch_g', 'w_out', 'final_g']
TWIN_WEIGHTS = ['ln_g', 'w_in', 'w_pool', 'pool_scale', 'lam_re', 'lam_im', 'log_dt', 'b_re', 'b_im', 'c_re', 'c_im', 'd_skip', 'w_glu', 'b_glu', 'branch_g', 'w_out', 'final_g']
TWIN_DIFF_INPUT = 'x'
TWIN_INPUTS = ['x', 'ln_g', 'w_in', 'w_pool', 'pool_scale', 'lam_re', 'lam_im', 'log_dt', 'b_re', 'b_im', 'c_re', 'c_im', 'd_skip', 'w_glu', 'b_glu', 'branch_g', 'w_out', 'final_g', 'loss_target', 'm_ln_g', 'm_w_in', 'm_w_pool', 'm_pool_scale', 'm_lam_re', 'm_lam_im', 'm_log_dt', 'm_b_re', 'm_b_im', 'm_c_re', 'm_c_im', 'm_d_skip', 'm_w_glu', 'm_b_glu', 'm_branch_g', 'm_w_out', 'm_final_g', 'v_ln_g', 'v_w_in', 'v_w_pool', 'v_pool_scale', 'v_lam_re', 'v_lam_im', 'v_log_dt', 'v_b_re', 'v_b_im', 'v_c_re', 'v_c_im', 'v_d_skip', 'v_w_glu', 'v_b_glu', 'v_branch_g', 'v_w_out', 'v_final_g']
TWIN_OUTPUTS = ['loss', 'grad_x', 'grad_ln_g', 'grad_w_in', 'grad_w_pool', 'grad_pool_scale', 'grad_lam_re', 'grad_lam_im', 'grad_log_dt', 'grad_b_re', 'grad_b_im', 'grad_c_re', 'grad_c_im', 'grad_d_skip', 'grad_w_glu', 'grad_b_glu', 'grad_branch_g', 'grad_w_out', 'grad_final_g', 'delta_ln_g', 'delta_w_in', 'delta_w_pool', 'delta_pool_scale', 'delta_lam_re', 'delta_lam_im', 'delta_log_dt', 'delta_b_re', 'delta_b_im', 'delta_c_re', 'delta_c_im', 'delta_d_skip', 'delta_w_glu', 'delta_b_glu', 'delta_branch_g', 'delta_w_out', 'delta_final_g', 'new_m_ln_g', 'new_m_w_in', 'new_m_w_pool', 'new_m_pool_scale', 'new_m_lam_re', 'new_m_lam_im', 'new_m_log_dt', 'new_m_b_re', 'new_m_b_im', 'new_m_c_re', 'new_m_c_im', 'new_m_d_skip', 'new_m_w_glu', 'new_m_b_glu', 'new_m_branch_g', 'new_m_w_out', 'new_m_final_g', 'new_v_ln_g', 'new_v_w_in', 'new_v_w_pool', 'new_v_pool_scale', 'new_v_lam_re', 'new_v_lam_im', 'new_v_log_dt', 'new_v_b_re', 'new_v_b_im', 'new_v_c_re', 'new_v_c_im', 'new_v_d_skip', 'new_v_w_glu', 'new_v_b_glu', 'new_v_branch_g', 'new_v_w_out', 'new_v_final_g']
TWIN_LEAF_KINDS = {'loss': 'loss', 'grad_x': 'grad_x', 'grad_ln_g': 'grad_w', 'grad_w_in': 'grad_w', 'grad_w_pool': 'grad_w', 'grad_pool_scale': 'grad_w', 'grad_lam_re': 'grad_w', 'grad_lam_im': 'grad_w', 'grad_log_dt': 'grad_w', 'grad_b_re': 'grad_w', 'grad_b_im': 'grad_w', 'grad_c_re': 'grad_w', 'grad_c_im': 'grad_w', 'grad_d_skip': 'grad_w', 'grad_w_glu': 'grad_w', 'grad_b_glu': 'grad_w', 'grad_branch_g': 'grad_w', 'grad_w_out': 'grad_w', 'grad_final_g': 'grad_w', 'delta_ln_g': 'delta_w', 'delta_w_in': 'delta_w', 'delta_w_pool': 'delta_w', 'delta_pool_scale': 'delta_w', 'delta_lam_re': 'delta_w', 'delta_lam_im': 'delta_w', 'delta_log_dt': 'delta_w', 'delta_b_re': 'delta_w', 'delta_b_im': 'delta_w', 'delta_c_re': 'delta_w', 'delta_c_im': 'delta_w', 'delta_d_skip': 'delta_w', 'delta_w_glu': 'delta_w', 'delta_b_glu': 'delta_w', 'delta_branch_g': 'delta_w', 'delta_w_out': 'delta_w', 'delta_final_g': 'delta_w', 'new_m_ln_g': 'new_m', 'new_m_w_in': 'new_m', 'new_m_w_pool': 'new_m', 'new_m_pool_scale': 'new_m', 'new_m_lam_re': 'new_m', 'new_m_lam_im': 'new_m', 'new_m_log_dt': 'new_m', 'new_m_b_re': 'new_m', 'new_m_b_im': 'new_m', 'new_m_c_re': 'new_m', 'new_m_c_im': 'new_m', 'new_m_d_skip': 'new_m', 'new_m_w_glu': 'new_m', 'new_m_b_glu': 'new_m', 'new_m_branch_g': 'new_m', 'new_m_w_out': 'new_m', 'new_m_final_g': 'new_m', 'new_v_ln_g': 'new_v', 'new_v_w_in': 'new_v', 'new_v_w_pool': 'new_v', 'new_v_pool_scale': 'new_v', 'new_v_lam_re': 'new_v', 'new_v_lam_im': 'new_v', 'new_v_log_dt': 'new_v', 'new_v_b_re': 'new_v', 'new_v_b_im': 'new_v', 'new_v_c_re': 'new_v', 'new_v_c_im': 'new_v', 'new_v_d_skip': 'new_v', 'new_v_w_glu': 'new_v', 'new_v_b_glu': 'new_v', 'new_v_branch_g': 'new_v', 'new_v_w_out': 'new_v', 'new_v_final_g': 'new_v'}


def _forward(args):
    return _fwd_reference(*[args[k] for k in FWD_PARAMS])


def _output_shape():
    out = _jax.eval_shape(lambda: _forward(_fwd_setup_inputs(0)))
    return out.shape, out.dtype

N_MICROBATCH = 1
ADAM_LR = 0.001
ADAM_B1 = 0.9
ADAM_B2 = 0.999
ADAM_EPS = 1e-08
ADAM_WD = 0.01
ADAM_STEP = 10
PER_EXAMPLE_BATCH_AXIS = {'x': 0, 'loss_target': 0}
SHARED_INPUTS = []
_WEIGHT_DTYPES = {'ln_g': _jnp.float32, 'w_in': _jnp.float32, 'w_pool': _jnp.float32, 'pool_scale': _jnp.float32, 'lam_re': _jnp.float32, 'lam_im': _jnp.float32, 'log_dt': _jnp.float32, 'b_re': _jnp.float32, 'b_im': _jnp.float32, 'c_re': _jnp.float32, 'c_im': _jnp.float32, 'd_skip': _jnp.float32, 'w_glu': _jnp.float32, 'b_glu': _jnp.float32, 'branch_g': _jnp.float32, 'w_out': _jnp.float32, 'final_g': _jnp.float32}
MOMENT_SCALE = {'ln_g': 1.961213e-02, 'w_in': 1.124101e-02, 'w_pool': 1.276336e-02, 'pool_scale': 1.293630e-02, 'lam_re': 7.300952e-04, 'lam_im': 6.989320e-04, 'log_dt': 4.115697e-01, 'b_re': 4.530064e-04, 'b_im': 4.472355e-04, 'c_re': 8.995335e-04, 'c_im': 9.028503e-04, 'd_skip': 1.397664e-02, 'w_glu': 9.445352e-03, 'b_glu': 1.478175e-02, 'branch_g': 1.285865e-02, 'w_out': 2.554297e-02, 'final_g': 7.982240e+00}


def _to_microbatches(a, axis):
    t = _jnp.moveaxis(a, axis, 0)
    t = t.reshape((N_MICROBATCH, t.shape[0] // N_MICROBATCH) + t.shape[1:])
    return _jnp.moveaxis(t, 1, axis + 1)


def setup_inputs(seed: int = 0) -> dict:
    inp = _fwd_setup_inputs(seed)
    key = _jax.random.fold_in(_jax.random.key(seed), 7919)
    shape, _ = _output_shape()
    out = dict(inp)
    out["loss_target"] = _jax.random.normal(_jax.random.fold_in(key, 0), shape, _jnp.float32)
    for i, name in enumerate(TWIN_WEIGHTS):
        w = inp[name].astype(_jnp.float32)
        if MOMENT_SCALE is None:
            s = _jnp.sqrt(_jnp.mean(_jnp.square(w)) + 1e-30)
        else:
            s = MOMENT_SCALE[name]
        km, kv = _jax.random.split(_jax.random.fold_in(key, i + 1))
        out[name] = w
        out["m_" + name] = s * _jax.random.normal(km, w.shape, _jnp.float32)
        out["v_" + name] = (s * s) * _jax.random.uniform(kv, w.shape, _jnp.float32, 0.5, 1.5)
    if N_MICROBATCH > 1:
        for name, axis in PER_EXAMPLE_BATCH_AXIS.items():
            out[name] = _to_microbatches(out[name], axis)
    return {'x': out['x'], 'ln_g': out['ln_g'], 'w_in': out['w_in'], 'w_pool': out['w_pool'], 'pool_scale': out['pool_scale'], 'lam_re': out['lam_re'], 'lam_im': out['lam_im'], 'log_dt': out['log_dt'], 'b_re': out['b_re'], 'b_im': out['b_im'], 'c_re': out['c_re'], 'c_im': out['c_im'], 'd_skip': out['d_skip'], 'w_glu': out['w_glu'], 'b_glu': out['b_glu'], 'branch_g': out['branch_g'], 'w_out': out['w_out'], 'final_g': out['final_g'], 'loss_target': out['loss_target'], 'm_ln_g': out['m_ln_g'], 'm_w_in': out['m_w_in'], 'm_w_pool': out['m_w_pool'], 'm_pool_scale': out['m_pool_scale'], 'm_lam_re': out['m_lam_re'], 'm_lam_im': out['m_lam_im'], 'm_log_dt': out['m_log_dt'], 'm_b_re': out['m_b_re'], 'm_b_im': out['m_b_im'], 'm_c_re': out['m_c_re'], 'm_c_im': out['m_c_im'], 'm_d_skip': out['m_d_skip'], 'm_w_glu': out['m_w_glu'], 'm_b_glu': out['m_b_glu'], 'm_branch_g': out['m_branch_g'], 'm_w_out': out['m_w_out'], 'm_final_g': out['m_final_g'], 'v_ln_g': out['v_ln_g'], 'v_w_in': out['v_w_in'], 'v_w_pool': out['v_w_pool'], 'v_pool_scale': out['v_pool_scale'], 'v_lam_re': out['v_lam_re'], 'v_lam_im': out['v_lam_im'], 'v_log_dt': out['v_log_dt'], 'v_b_re': out['v_b_re'], 'v_b_im': out['v_b_im'], 'v_c_re': out['v_c_re'], 'v_c_im': out['v_c_im'], 'v_d_skip': out['v_d_skip'], 'v_w_glu': out['v_w_glu'], 'v_b_glu': out['v_b_glu'], 'v_branch_g': out['v_branch_g'], 'v_w_out': out['v_w_out'], 'v_final_g': out['v_final_g']}


def _loss(weights, diff, rest, loss_target):
    with _jax.named_scope("forward"):
        args = {**rest, TWIN_DIFF_INPUT: diff, **{k: w.astype(_WEIGHT_DTYPES[k]) for k, w in weights.items()}}
        y = _forward(args)
    with _jax.named_scope("loss_head"):
        err = _jnp.square(y.astype(_jnp.float32) - loss_target)
        return 0.5 * _jnp.sum(_jnp.mean(err, axis=-1)) if err.ndim else 0.5 * err


def _adamw(w, g, m, v):
    m = ADAM_B1 * m + (1.0 - ADAM_B1) * g
    v = ADAM_B2 * v + (1.0 - ADAM_B2) * _jnp.square(g)
    m_hat = m / (1.0 - ADAM_B1 ** ADAM_STEP)
    v_hat = v / (1.0 - ADAM_B2 ** ADAM_STEP)
    delta = -ADAM_LR * (m_hat / (_jnp.sqrt(v_hat) + ADAM_EPS) + ADAM_WD * w)
    return delta, m, v


def reference(x, ln_g, w_in, w_pool, pool_scale, lam_re, lam_im, log_dt, b_re, b_im, c_re, c_im, d_skip, w_glu, b_glu, branch_g, w_out, final_g, loss_target, m_ln_g, m_w_in, m_w_pool, m_pool_scale, m_lam_re, m_lam_im, m_log_dt, m_b_re, m_b_im, m_c_re, m_c_im, m_d_skip, m_w_glu, m_b_glu, m_branch_g, m_w_out, m_final_g, v_ln_g, v_w_in, v_w_pool, v_pool_scale, v_lam_re, v_lam_im, v_log_dt, v_b_re, v_b_im, v_c_re, v_c_im, v_d_skip, v_w_glu, v_b_glu, v_branch_g, v_w_out, v_final_g):
    given = dict(x=x, ln_g=ln_g, w_in=w_in, w_pool=w_pool, pool_scale=pool_scale, lam_re=lam_re, lam_im=lam_im, log_dt=log_dt, b_re=b_re, b_im=b_im, c_re=c_re, c_im=c_im, d_skip=d_skip, w_glu=w_glu, b_glu=b_glu, branch_g=branch_g, w_out=w_out, final_g=final_g, loss_target=loss_target, m_ln_g=m_ln_g, m_w_in=m_w_in, m_w_pool=m_w_pool, m_pool_scale=m_pool_scale, m_lam_re=m_lam_re, m_lam_im=m_lam_im, m_log_dt=m_log_dt, m_b_re=m_b_re, m_b_im=m_b_im, m_c_re=m_c_re, m_c_im=m_c_im, m_d_skip=m_d_skip, m_w_glu=m_w_glu, m_b_glu=m_b_glu, m_branch_g=m_branch_g, m_w_out=m_w_out, m_final_g=m_final_g, v_ln_g=v_ln_g, v_w_in=v_w_in, v_w_pool=v_w_pool, v_pool_scale=v_pool_scale, v_lam_re=v_lam_re, v_lam_im=v_lam_im, v_log_dt=v_log_dt, v_b_re=v_b_re, v_b_im=v_b_im, v_c_re=v_c_re, v_c_im=v_c_im, v_d_skip=v_d_skip, v_w_glu=v_w_glu, v_b_glu=v_b_glu, v_branch_g=v_branch_g, v_w_out=v_w_out, v_final_g=v_final_g)
    weights = {n: given[n] for n in TWIN_WEIGHTS}
    shared = {n: given[n] for n in SHARED_INPUTS}
    per_example = {n: given[n] for n in ['x']}
    grad_fn = _jax.value_and_grad(_loss, argnums=(0, 1))

    def one_microbatch(ex, loss_target):
        ex = dict(ex)
        diff = ex.pop(TWIN_DIFF_INPUT)
        return grad_fn(weights, diff, {**shared, **ex}, loss_target)

    if N_MICROBATCH == 1:
        loss, (grad_w, grad_x) = one_microbatch(per_example, given["loss_target"])
    else:
        def body(carry, xs):
            loss_sum, grad_sum = carry
            l_k, (gw_k, gx_k) = one_microbatch(xs[0], xs[1])
            with _jax.named_scope("update"):
                return (loss_sum + l_k, _jax.tree.map(_jnp.add, grad_sum, gw_k)), gx_k

        init = (_jnp.zeros((), _jnp.float32), _jax.tree.map(_jnp.zeros_like, weights))
        (loss, grad_w), grad_x = _jax.lax.scan(body, init, (per_example, given["loss_target"]))
    with _jax.named_scope("update"):
        delta_w, new_m, new_v = {}, {}, {}
        for n in TWIN_WEIGHTS:
            delta_w[n], new_m[n], new_v[n] = _adamw(weights[n], grad_w[n], given["m_" + n], given["v_" + n])
    return (loss, grad_x, *[grad_w[n] for n in TWIN_WEIGHTS], *[delta_w[n] for n in TWIN_WEIGHTS],
            *[new_m[n] for n in TWIN_WEIGHTS], *[new_v[n] for n in TWIN_WEIGHTS])
```

```python
import functools
import math

import jax
import jax.numpy as jnp
from jax import lax
from jax.experimental import pallas as pl
from jax.experimental.pallas import tpu as pltpu

F32 = jnp.float32
BF16 = jnp.bfloat16
MESH = pl.DeviceIdType.MESH

EPS = 1e-6
HEAD_DIM = 128
SSM_GROUP = 16
SSM_STATE = 64
GROUPS_PER_CHUNK = 8
CHUNK_U = GROUPS_PER_CHUNK * SSM_GROUP
CHUNK_X = GROUPS_PER_CHUNK * SSM_STATE
N_POOL_GROUPS = 4
POOL_HALO = 16
N_DEV = 8
LANE = 128
ATTN_TQ = 128
ATTN_TK = 128
ATTN_DECAY_CUTOFF = 100.0
ROW_TILE = 128
VMEM_BIG = 58 * 1024 * 1024
VMEM_MID = 40 * 1024 * 1024

ADAM_LR = 0.001
ADAM_B1 = 0.9
ADAM_B2 = 0.999
ADAM_EPS = 1e-08
ADAM_WD = 0.01
ADAM_STEP = 10
ADAM_C1 = 1.0 / (1.0 - ADAM_B1 ** ADAM_STEP)
ADAM_C2 = 1.0 / (1.0 - ADAM_B2 ** ADAM_STEP)

NN = (((1,), (0,)), ((), ()))
NT = (((1,), (1,)), ((), ()))
TN = (((0,), (0,)), ((), ()))


def _pick(n, cap):
    if n <= cap:
        return n
    step = LANE if cap >= LANE else 8
    t = (cap // step) * step
    while t > step and n % t:
        t -= step
    assert n % t == 0, (n, cap)
    return t


def _cparams(sem, vmem=None):
    return pltpu.CompilerParams(dimension_semantics=sem, vmem_limit_bytes=vmem)


def _dot(a, b, dn=NN):
    return lax.dot_general(a, b, dn, preferred_element_type=F32)


def _sigmoid(x):
    e = jnp.exp(-jnp.abs(x))
    r = 1.0 / (1.0 + e)
    return jnp.where(x >= 0, r, e * r)


def _matmul(name, a, b, *, grid, a_spec, b_spec, o_spec, out_shape, dn, acc_shape,
            res=None, res_spec=None):
    nk = grid[2]

    def body(*refs):
        if res is None:
            a_ref, b_ref, o_ref, acc = refs
            r_ref = None
        else:
            a_ref, b_ref, r_ref, o_ref, acc = refs
        k = pl.program_id(2)

        @pl.when(k == 0)
        def _():
            acc[...] = jnp.zeros_like(acc)

        acc[...] += _dot(a_ref[...].astype(BF16), b_ref[...].astype(BF16), dn)

        @pl.when(k == nk - 1)
        def _():
            r = acc[...]
            if r_ref is not None:
                r = r + r_ref[...]
            o_ref[...] = r.astype(o_ref.dtype)

    in_specs = [a_spec, b_spec] + ([res_spec] if res is not None else [])
    args = (a, b) + ((res,) if res is not None else ())
    return pl.pallas_call(
        body, name=name, grid=grid, in_specs=in_specs, out_specs=o_spec, out_shape=out_shape,
        scratch_shapes=[pltpu.VMEM(acc_shape, F32)],
        compiler_params=_cparams(("parallel", "parallel", "arbitrary"), VMEM_BIG),
    )(*args)


def mm_nn_gathered(name, a, wg, out_dtype=F32):
    M, K = a.shape
    _, _, nper = wg.shape
    tm, tk, tn = _pick(M, 1024), _pick(K, 512), _pick(nper, 1536)
    r = nper // tn
    return _matmul(
        name, a, wg, grid=(M // tm, N_DEV * r, K // tk),
        a_spec=pl.BlockSpec((tm, tk), lambda i, j, k: (i, k)),
        b_spec=pl.BlockSpec((None, tk, tn), lambda i, j, k: (j // r, k, j % r)),
        o_spec=pl.BlockSpec((tm, tn), lambda i, j, k: (i, j)),
        out_shape=jax.ShapeDtypeStruct((M, N_DEV * nper), out_dtype), dn=NN, acc_shape=(tm, tn))


def mm_nt_gathered(name, a, wg, out_dtype=F32):
    M, _ = a.shape
    _, N, nper = wg.shape
    tm, tn, tk = _pick(M, 1024), _pick(N, 1024), _pick(nper, 768)
    r = nper // tk
    return _matmul(
        name, a, wg, grid=(M // tm, N // tn, N_DEV * r),
        a_spec=pl.BlockSpec((tm, tk), lambda i, j, k: (i, k)),
        b_spec=pl.BlockSpec((None, tn, tk), lambda i, j, k: (k // r, j, k % r)),
        o_spec=pl.BlockSpec((tm, tn), lambda i, j, k: (i, j)),
        out_shape=jax.ShapeDtypeStruct((M, N), out_dtype), dn=NT, acc_shape=(tm, tn))


def mm_tn_scattered(name, a, b):
    L, M = a.shape
    nper = b.shape[1] // N_DEV
    tm, tn, tk = _pick(M, 1024), _pick(nper, 1536), _pick(L, 512)
    r = nper // tn
    return _matmul(
        name, a, b, grid=(M // tm, N_DEV * r, L // tk),
        a_spec=pl.BlockSpec((tk, tm), lambda i, j, k: (k, i)),
        b_spec=pl.BlockSpec((tk, tn), lambda i, j, k: (k, j)),
        o_spec=pl.BlockSpec((None, tm, tn), lambda i, j, k: (j // r, i, j % r)),
        out_shape=jax.ShapeDtypeStruct((N_DEV, M, nper), F32), dn=TN, acc_shape=(tm, tn))


def mm_plain(name, a, b, dn, out_dtype=F32, res=None):
    if dn == NN:
        (M, K), N = a.shape, b.shape[1]
    elif dn == NT:
        (M, K), N = a.shape, b.shape[0]
    else:
        (K, M), N = a.shape, b.shape[1]
    tm, tn, tk = _pick(M, 1024), _pick(N, 1024), _pick(K, 512)
    a_spec = (pl.BlockSpec((tk, tm), lambda i, j, k: (k, i)) if dn == TN
              else pl.BlockSpec((tm, tk), lambda i, j, k: (i, k)))
    b_spec = (pl.BlockSpec((tn, tk), lambda i, j, k: (j, k)) if dn == NT
              else pl.BlockSpec((tk, tn), lambda i, j, k: (k, j)))
    o_spec = pl.BlockSpec((tm, tn), lambda i, j, k: (i, j))
    return _matmul(
        name, a, b, grid=(M // tm, N // tn, K // tk), a_spec=a_spec, b_spec=b_spec, o_spec=o_spec,
        out_shape=jax.ShapeDtypeStruct((M, N), out_dtype), dn=dn, acc_shape=(tm, tn),
        res=res, res_spec=o_spec if res is not None else None)


def rms_fwd(name, x, g):
    L, D = x.shape
    tr = _pick(L, ROW_TILE)

    def body(x_ref, g_ref, h_ref):
        xv = x_ref[...]
        r = lax.rsqrt(jnp.mean(xv * xv, axis=-1, keepdims=True) + EPS)
        h_ref[...] = (xv * r * g_ref[...]).astype(BF16)

    return pl.pallas_call(
        body, name=name, grid=(L // tr,),
        in_specs=[pl.BlockSpec((tr, D), lambda i: (i, 0)), pl.BlockSpec((1, D), lambda i: (0, 0))],
        out_specs=pl.BlockSpec((tr, D), lambda i: (i, 0)),
        out_shape=jax.ShapeDtypeStruct((L, D), BF16),
        compiler_params=_cparams(("parallel",), VMEM_MID))(x, g)


def rms_bwd(name, x, dh, dres, g):
    L, D = x.shape
    tr = _pick(L, ROW_TILE)

    def body(x_ref, dh_ref, dr_ref, g_ref, dx_ref, dg_ref):
        xv = x_ref[...]
        r = lax.rsqrt(jnp.mean(xv * xv, axis=-1, keepdims=True) + EPS)
        xh = xv * r
        dhv = dh_ref[...]
        dn = dhv * g_ref[...]
        dx_ref[...] = dr_ref[...] + r * (dn - xh * jnp.mean(dn * xh, axis=-1, keepdims=True))

        @pl.when(pl.program_id(0) == 0)
        def _():
            dg_ref[...] = jnp.zeros_like(dg_ref)

        dg_ref[...] += jnp.sum(dhv * xh, axis=0, keepdims=True)

    row = pl.BlockSpec((tr, D), lambda i: (i, 0))
    vec = pl.BlockSpec((1, D), lambda i: (0, 0))
    return pl.pallas_call(
        body, name=name, grid=(L // tr,), in_specs=[row, row, row, vec], out_specs=[row, vec],
        out_shape=[jax.ShapeDtypeStruct((L, D), F32), jax.ShapeDtypeStruct((1, D), F32)],
        compiler_params=_cparams(("arbitrary",), VMEM_MID))(x, dh, dres, g)


def loss_head(name, x, g, target):
    L, D = x.shape
    tr = _pick(L, ROW_TILE)

    def body(x_ref, g_ref, t_ref, loss_ref, dx_ref, dg_ref):
        xv = x_ref[...]
        gv = g_ref[...]
        r = lax.rsqrt(jnp.mean(xv * xv, axis=-1, keepdims=True) + EPS)
        xh = xv * r
        err = xh * gv - t_ref[...]
        dy = err * (1.0 / D)
        dn = dy * gv
        dx_ref[...] = r * (dn - xh * jnp.mean(dn * xh, axis=-1, keepdims=True))

        @pl.when(pl.program_id(0) == 0)
        def _():
            dg_ref[...] = jnp.zeros_like(dg_ref)
            loss_ref[...] = jnp.zeros_like(loss_ref)

        dg_ref[...] += jnp.sum(dy * xh, axis=0, keepdims=True)
        row_loss = jnp.sum(err * err, axis=-1, keepdims=True) * (0.5 / D)
        loss_ref[...] += jnp.sum(row_loss, axis=0, keepdims=True)

    row = pl.BlockSpec((tr, D), lambda i: (i, 0))
    vec = pl.BlockSpec((1, D), lambda i: (0, 0))
    one = pl.BlockSpec((1, 1), lambda i: (0, 0))
    return pl.pallas_call(
        body, name=name, grid=(L // tr,), in_specs=[row, vec, row], out_specs=[one, row, vec],
        out_shape=[jax.ShapeDtypeStruct((1, 1), F32), jax.ShapeDtypeStruct((L, D), F32),
                   jax.ShapeDtypeStruct((1, D), F32)],
        compiler_params=_cparams(("arbitrary",), VMEM_MID))(x, g, target)


def _branch_specs(D, tr):
    DP, DA, DS = D // 4, D // 2, D // 4
    return dict(
        pool=pl.BlockSpec((tr, DP), lambda i: (i, 0)),
        attn=pl.BlockSpec((tr, DA), lambda i: (i, 0)),
        glu=pl.BlockSpec((tr, 2 * DS), lambda i: (i, 0)),
        p_gate=pl.BlockSpec((tr, DP), lambda i: (i, 1)),
        a_gate=pl.BlockSpec((tr, DA), lambda i: (i, 4)),
        s_gate=pl.BlockSpec((tr, DS), lambda i: (i, 11)),
        bglu=pl.BlockSpec((1, 2 * DS), lambda i: (0, 0)),
        bg=pl.BlockSpec((1, D), lambda i: (0, 0)),
        row=pl.BlockSpec((tr, D), lambda i: (i, 0)),
    )


def branch_fwd(name, ypool, yattn, glu_pre, proj, b_glu, branch_g):
    L, DP = ypool.shape
    D = 4 * DP
    DA, DS = D // 2, D // 4
    tr = _pick(L, ROW_TILE)
    s = _branch_specs(D, tr)

    def body(yp_ref, ya_ref, gl_ref, pg_ref, ag_ref, sg_ref, bgl_ref, bg_ref, y_ref):
        pre = gl_ref[...] + bgl_ref[...]
        ys = pre[:, :DS] * _sigmoid(pre[:, DS:])
        bg = bg_ref[...]

        def one(raw, gate, g):
            r = lax.rsqrt(jnp.mean(raw * raw, axis=-1, keepdims=True) + EPS)
            return raw * r * g * (gate * _sigmoid(gate))

        y_ref[:, :DP] = one(yp_ref[...], pg_ref[...], bg[:, :DP]).astype(BF16)
        y_ref[:, DP:DP + DA] = one(ya_ref[...], ag_ref[...], bg[:, DP:DP + DA]).astype(BF16)
        y_ref[:, DP + DA:] = one(ys, sg_ref[...], bg[:, DP + DA:]).astype(BF16)

    return pl.pallas_call(
        body, name=name, grid=(L // tr,),
        in_specs=[s["pool"], s["attn"], s["glu"], s["p_gate"], s["a_gate"], s["s_gate"], s["bglu"], s["bg"]],
        out_specs=s["row"], out_shape=jax.ShapeDtypeStruct((L, D), BF16),
        compiler_params=_cparams(("parallel",), VMEM_MID))(ypool, yattn, glu_pre, proj, proj, proj, b_glu, branch_g)


def branch_bwd(name, dy, ypool, yattn, glu_pre, proj, b_glu, branch_g):
    L, DP = ypool.shape
    D = 4 * DP
    DA, DS = D // 2, D // 4
    tr = _pick(L, ROW_TILE // 2)
    s = _branch_specs(D, tr)

    def body(dy_ref, yp_ref, ya_ref, gl_ref, pg_ref, ag_ref, sg_ref, bgl_ref, bg_ref,
             dyp_ref, dya_ref, dgl_ref, dpg_ref, dag_ref, dsg_ref, dbg_ref, dbgl_ref):
        @pl.when(pl.program_id(0) == 0)
        def _():
            dbg_ref[...] = jnp.zeros_like(dbg_ref)
            dbgl_ref[...] = jnp.zeros_like(dbgl_ref)

        bg = bg_ref[...]

        def one(raw, gate, g, dyb):
            r = lax.rsqrt(jnp.mean(raw * raw, axis=-1, keepdims=True) + EPS)
            n = raw * r
            sg = _sigmoid(gate)
            sl = gate * sg
            dgate = dyb * n * g * (sg * (1.0 + gate * (1.0 - sg)))
            dbg = jnp.sum(dyb * n * sl, axis=0, keepdims=True)
            dn = dyb * g * sl
            draw = r * (dn - n * jnp.mean(dn * n, axis=-1, keepdims=True))
            return draw, dgate, dbg

        draw, dgate, dbg = one(yp_ref[...], pg_ref[...], bg[:, :DP], dy_ref[:, :DP])
        dyp_ref[...] = draw
        dpg_ref[...] = dgate.astype(BF16)
        dbg_ref[:, :DP] += dbg

        draw, dgate, dbg = one(ya_ref[...], ag_ref[...], bg[:, DP:DP + DA], dy_ref[:, DP:DP + DA])
        dya_ref[...] = draw
        dag_ref[...] = dgate.astype(BF16)
        dbg_ref[:, DP:DP + DA] += dbg

        pre = gl_ref[...] + bgl_ref[...]
        val = pre[:, :DS]
        sgt = _sigmoid(pre[:, DS:])
        draw, dgate, dbg = one(val * sgt, sg_ref[...], bg[:, DP + DA:], dy_ref[:, DP + DA:])
        dsg_ref[...] = dgate.astype(BF16)
        dbg_ref[:, DP + DA:] += dbg
        dval = draw * sgt
        dgt = draw * val * sgt * (1.0 - sgt)
        dgl_ref[:, :DS] = dval.astype(BF16)
        dgl_ref[:, DS:] = dgt.astype(BF16)
        dbgl_ref[:, :DS] += jnp.sum(dval, axis=0, keepdims=True)
        dbgl_ref[:, DS:] += jnp.sum(dgt, axis=0, keepdims=True)

    loc = lambda w: pl.BlockSpec((tr, w), lambda i: (i, 0))
    return pl.pallas_call(
        body, name=name, grid=(L // tr,),
        in_specs=[s["row"], s["pool"], s["attn"], s["glu"], s["p_gate"], s["a_gate"], s["s_gate"], s["bglu"], s["bg"]],
        out_specs=[loc(DP), loc(DA), loc(2 * DS), loc(DP), loc(DA), loc(DS), s["bg"], s["bglu"]],
        out_shape=[jax.ShapeDtypeStruct((L, DP), F32), jax.ShapeDtypeStruct((L, DA), F32),
                   jax.ShapeDtypeStruct((L, 2 * DS), BF16), jax.ShapeDtypeStruct((L, DP), BF16),
                   jax.ShapeDtypeStruct((L, DA), BF16), jax.ShapeDtypeStruct((L, DS), BF16),
                   jax.ShapeDtypeStruct((1, D), F32), jax.ShapeDtypeStruct((1, 2 * DS), F32)],
        compiler_params=_cparams(("arbitrary",), VMEM_BIG),
    )(dy, ypool, yattn, glu_pre, proj, proj, proj, b_glu, branch_g)


def _pool_select(g, s2, s4, s8, s16):
    return jnp.where(g == 0, s2, jnp.where(g == 1, s4, jnp.where(g == 2, s8, s16)))


def _pool_window(g):
    return jnp.where(g == 0, 2.0, jnp.where(g == 1, 4.0, jnp.where(g == 2, 8.0, 16.0))).astype(F32)


def _pooled_chunk(pad, g, r0, ch):
    xh = pad[pl.ds(r0, ch + POOL_HALO), :]
    s2 = xh + pltpu.roll(xh, 1, 0)
    s4 = s2 + pltpu.roll(s2, 2, 0)
    s8 = s4 + pltpu.roll(s4, 4, 0)
    s16 = s8 + pltpu.roll(s8, 8, 0)
    win = _pool_select(g, s2, s4, s8, s16)[POOL_HALO:]
    pos = (r0 + 1 + lax.broadcasted_iota(jnp.int32, (ch, 1), 0)).astype(F32)
    return win / jnp.minimum(pos, _pool_window(g)) - xh[POOL_HALO:]


def pool_fwd(name, proj, wp, scale):
    L = proj.shape[0]
    DP = scale.shape[1]
    PG = DP // N_POOL_GROUPS
    ch = _pick(L, 256)

    def body(x_ref, w_ref, s_ref, o_ref, pad):
        g = pl.program_id(0)
        pad[0:POOL_HALO, :] = jnp.zeros((POOL_HALO, PG), F32)
        pad[POOL_HALO:, :] = x_ref[...]

        def chunk(ci, carry):
            r0 = pl.multiple_of(ci * ch, ch)
            pooled = _pooled_chunk(pad, g, r0, ch)
            o_ref[pl.ds(r0, ch), :] = _dot(pooled.astype(BF16), w_ref[...]) * s_ref[...]
            return carry

        lax.fori_loop(0, L // ch, chunk, 0)

    return pl.pallas_call(
        body, name=name, grid=(N_POOL_GROUPS,),
        in_specs=[pl.BlockSpec((L, PG), lambda g: (0, g)), pl.BlockSpec((None, PG, PG), lambda g: (g, 0, 0)),
                  pl.BlockSpec((1, PG), lambda g: (0, g))],
        out_specs=pl.BlockSpec((L, PG), lambda g: (0, g)),
        out_shape=jax.ShapeDtypeStruct((L, DP), F32),
        scratch_shapes=[pltpu.VMEM((L + POOL_HALO, PG), F32)],
        compiler_params=_cparams(("parallel",), VMEM_MID))(proj, wp, scale)


def pool_bwd(name, dyraw, proj, wp, scale):
    L = proj.shape[0]
    DP = scale.shape[1]
    PG = DP // N_POOL_GROUPS
    ch = _pick(L, 256)

    def body(dy_ref, x_ref, w_ref, s_ref, dx_ref, dw_ref, ds_ref, pad, dpad, dpo):
        g = pl.program_id(0)
        pad[0:POOL_HALO, :] = jnp.zeros((POOL_HALO, PG), F32)
        pad[POOL_HALO:, :] = x_ref[...]
        dpad[L:, :] = jnp.zeros((POOL_HALO, PG), F32)
        dw_ref[...] = jnp.zeros_like(dw_ref)
        ds_ref[...] = jnp.zeros_like(ds_ref)
        wv = w_ref[...]
        win_f = _pool_window(g)

        def chunk(ci, carry):
            r0 = pl.multiple_of(ci * ch, ch)
            pooled = _pooled_chunk(pad, g, r0, ch).astype(BF16)
            dyv = dy_ref[pl.ds(r0, ch), :]
            ds_ref[...] += jnp.sum(dyv * _dot(pooled, wv), axis=0, keepdims=True)
            dmixed = (dyv * s_ref[...]).astype(BF16)
            dw_ref[...] += _dot(pooled, dmixed, TN)
            dpooled = _dot(dmixed, wv, NT)
            pos = (r0 + 1 + lax.broadcasted_iota(jnp.int32, (ch, 1), 0)).astype(F32)
            dpad[pl.ds(r0, ch), :] = dpooled / jnp.minimum(pos, win_f)
            dpo[pl.ds(r0, ch), :] = dpooled
            return carry

        lax.fori_loop(0, L // ch, chunk, 0)

        def chunk2(ci, carry):
            r0 = pl.multiple_of(ci * ch, ch)
            n = ch + POOL_HALO
            dm = dpad[pl.ds(r0, n), :]
            s2 = dm + pltpu.roll(dm, n - 1, 0)
            s4 = s2 + pltpu.roll(s2, n - 2, 0)
            s8 = s4 + pltpu.roll(s4, n - 4, 0)
            s16 = s8 + pltpu.roll(s8, n - 8, 0)
            win = _pool_select(g, s2, s4, s8, s16)[:ch]
            dx_ref[pl.ds(r0, ch), :] = (win - dpo[pl.ds(r0, ch), :]).astype(BF16)
            return carry

        lax.fori_loop(0, L // ch, chunk2, 0)

    col = pl.BlockSpec((L, PG), lambda g: (0, g))
    return pl.pallas_call(
        body, name=name, grid=(N_POOL_GROUPS,),
        in_specs=[col, col, pl.BlockSpec((None, PG, PG), lambda g: (g, 0, 0)), pl.BlockSpec((1, PG), lambda g: (0, g))],
        out_specs=[col, pl.BlockSpec((None, PG, PG), lambda g: (g, 0, 0)), pl.BlockSpec((1, PG), lambda g: (0, g))],
        out_shape=[jax.ShapeDtypeStruct((L, DP), BF16), jax.ShapeDtypeStruct((N_POOL_GROUPS, PG, PG), F32),
                   jax.ShapeDtypeStruct((1, DP), F32)],
        scratch_shapes=[pltpu.VMEM((L + POOL_HALO, PG), F32), pltpu.VMEM((L + POOL_HALO, PG), F32),
                        pltpu.VMEM((L, PG), F32)],
        compiler_params=_cparams(("parallel",), VMEM_MID))(dyraw, proj, wp, scale)


def _attn_block(q, kt, i, k0, rb):
    tq, tk = ATTN_TQ, ATTN_TK
    row = lax.broadcasted_iota(jnp.int32, (tq, tk), 0)
    col = lax.broadcasted_iota(jnp.int32, (tq, tk), 1)
    causal = (k0 + col) < (i * tq + row)
    z = _dot(q, kt, NT)
    e = jnp.exp(-jnp.abs(z))
    l1p = jnp.log(1.0 + e)
    log_sig = jnp.minimum(z, 0.0) - l1p
    log_1m = -jnp.maximum(z, 0.0) - l1p
    b = jnp.where(causal, log_1m, 0.0)
    b_hi = b.astype(BF16)
    b_lo = (b - b_hi.astype(F32)).astype(BF16)
    jr = lax.broadcasted_iota(jnp.int32, (tk, tk), 0)
    sc = lax.broadcasted_iota(jnp.int32, (tk, tk), 1)
    after = (jr > sc).astype(BF16)
    suffix = _dot(b_hi, after) + _dot(b_lo, after) + rb
    w = jnp.where(causal, jnp.exp(log_sig + suffix), 0.0)
    return z, e, causal, b, w


def attn_fwd(name, proj, D):
    L = proj.shape[0]
    DA = D // 2
    H = DA // HEAD_DIM
    tq, tk = ATTN_TQ, ATTN_TK
    qo, ko, vo = (D // 2) // HEAD_DIM, D // HEAD_DIM, (3 * D // 2) // HEAD_DIM
    scale = HEAD_DIM ** -0.5

    def body(q_ref, k_ref, v_ref, o_ref, kb_s, vb_s, acc, rb):
        i = pl.program_id(1)

        @pl.when(i == 0)
        def _():
            kb_s[...] = k_ref[...].astype(BF16)
            vb_s[...] = v_ref[...].astype(BF16)

        q = (q_ref[...] * scale).astype(BF16)
        acc[...] = jnp.zeros_like(acc)
        rb[...] = jnp.zeros_like(rb)

        def cond(c):
            return jnp.logical_and(c[0] >= 0, c[1])

        def step(c):
            kb = c[0]
            k0 = pl.multiple_of(kb * tk, tk)
            kt = kb_s[pl.ds(k0, tk), :]
            vt = vb_s[pl.ds(k0, tk), :]
            _, _, _, b, w = _attn_block(q, kt, i, k0, rb[...])
            acc[...] += _dot(w.astype(BF16), vt)
            rbn = rb[...] + jnp.sum(b, axis=1, keepdims=True)
            rb[...] = rbn
            return kb - 1, jnp.max(rbn) > -ATTN_DECAY_CUTOFF

        lax.while_loop(cond, step, (i, jnp.bool_(True)))
        o_ref[...] = acc[...]

    return pl.pallas_call(
        body, name=name, grid=(H, L // tq),
        in_specs=[pl.BlockSpec((tq, HEAD_DIM), lambda h, i: (i, qo + h)),
                  pl.BlockSpec((L, HEAD_DIM), lambda h, i: (0, ko + h)),
                  pl.BlockSpec((L, HEAD_DIM), lambda h, i: (0, vo + h))],
        out_specs=pl.BlockSpec((tq, HEAD_DIM), lambda h, i: (i, h)),
        out_shape=jax.ShapeDtypeStruct((L, DA), F32),
        scratch_shapes=[pltpu.VMEM((L, HEAD_DIM), BF16), pltpu.VMEM((L, HEAD_DIM), BF16),
                        pltpu.VMEM((tq, HEAD_DIM), F32), pltpu.VMEM((tq, 1), F32)],
        compiler_params=_cparams(("arbitrary", "arbitrary"), VMEM_MID))(proj, proj, proj)


def attn_bwd(name, proj, o, do, D):
    L = proj.shape[0]
    DA = D // 2
    H = DA // HEAD_DIM
    tq, tk = ATTN_TQ, ATTN_TK
    qo, ko, vo = (D // 2) // HEAD_DIM, D // HEAD_DIM, (3 * D // 2) // HEAD_DIM
    scale = HEAD_DIM ** -0.5

    def body(q_ref, k_ref, v_ref, o_ref, do_ref, dq_ref, dk_ref, dv_ref,
             kb_s, vb_s, dk_s, dv_s, dq_acc, rb, rg):
        i = pl.program_id(1)
        nq = pl.num_programs(1)

        @pl.when(i == 0)
        def _():
            kb_s[...] = k_ref[...].astype(BF16)
            vb_s[...] = v_ref[...].astype(BF16)
            dk_s[...] = jnp.zeros_like(dk_s)
            dv_s[...] = jnp.zeros_like(dv_s)

        q = (q_ref[...] * scale).astype(BF16)
        dob = do_ref[...].astype(BF16)
        delta = jnp.sum(dob.astype(F32) * o_ref[...], axis=1, keepdims=True)
        dq_acc[...] = jnp.zeros_like(dq_acc)
        rb[...] = jnp.zeros_like(rb)
        rg[...] = jnp.zeros_like(rg)

        def cond(c):
            return jnp.logical_and(c[0] >= 0, c[1])

        def step(c):
            kb = c[0]
            k0 = pl.multiple_of(kb * tk, tk)
            kt = kb_s[pl.ds(k0, tk), :]
            vt = vb_s[pl.ds(k0, tk), :]
            z, e, causal, b, w = _attn_block(q, kt, i, k0, rb[...])
            wq = w.astype(BF16)
            dw = _dot(dob, vt, NT)
            g = wq.astype(F32) * dw
            g_hi = g.astype(BF16)
            g_lo = (g - g_hi.astype(F32)).astype(BF16)
            jr = lax.broadcasted_iota(jnp.int32, (tk, tk), 0)
            sc = lax.broadcasted_iota(jnp.int32, (tk, tk), 1)
            from_s = (jr >= sc).astype(BF16)
            suffix_g = _dot(g_hi, from_s) + _dot(g_lo, from_s) + rg[...]
            before = delta - suffix_g
            r = 1.0 / (1.0 + e)
            sig = jnp.where(z >= 0, r, e * r)
            sig_neg = jnp.where(z >= 0, e * r, r)
            dz = jnp.where(causal, g * sig_neg - before * sig, 0.0).astype(BF16)
            dq_acc[...] += _dot(dz, kt)
            dk_s[pl.ds(k0, tk), :] += _dot(dz, q, TN)
            dv_s[pl.ds(k0, tk), :] += _dot(wq, dob, TN)
            rbn = rb[...] + jnp.sum(b, axis=1, keepdims=True)
            rb[...] = rbn
            rg[...] += jnp.sum(g, axis=1, keepdims=True)
            return kb - 1, jnp.max(rbn) > -ATTN_DECAY_CUTOFF

        lax.while_loop(cond, step, (i, jnp.bool_(True)))
        dq_ref[...] = (dq_acc[...] * scale).astype(BF16)

        @pl.when(i == nq - 1)
        def _():
            dk_ref[...] = dk_s[...].astype(BF16)
            dv_ref[...] = dv_s[...].astype(BF16)

    blk = pl.BlockSpec((tq, HEAD_DIM), lambda h, i: (i, h))
    full = pl.BlockSpec((L, HEAD_DIM), lambda h, i: (0, h))
    return pl.pallas_call(
        body, name=name, grid=(H, L // tq),
        in_specs=[pl.BlockSpec((tq, HEAD_DIM), lambda h, i: (i, qo + h)),
                  pl.BlockSpec((L, HEAD_DIM), lambda h, i: (0, ko + h)),
                  pl.BlockSpec((L, HEAD_DIM), lambda h, i: (0, vo + h)), blk, blk],
        out_specs=[blk, full, full],
        out_shape=[jax.ShapeDtypeStruct((L, DA), BF16)] * 3,
        scratch_shapes=[pltpu.VMEM((L, HEAD_DIM), BF16), pltpu.VMEM((L, HEAD_DIM), BF16),
                        pltpu.VMEM((L, HEAD_DIM), F32), pltpu.VMEM((L, HEAD_DIM), F32),
                        pltpu.VMEM((tq, HEAD_DIM), F32), pltpu.VMEM((tq, 1), F32), pltpu.VMEM((tq, 1), F32)],
        compiler_params=_cparams(("arbitrary", "arbitrary"), VMEM_MID))(proj, proj, proj, o, do)


def _cmul(ar, ai, br, bi):
    return ar * br - ai * bi, ar * bi + ai * br


def _cmul_conj(ar, ai, br, bi):
    return ar * br + ai * bi, ar * bi - ai * br


def _ssm_disc(lr, li, ld):
    dt = jnp.exp(ld)
    m = jnp.exp(lr * dt)
    ar, ai = m * jnp.cos(li * dt), m * jnp.sin(li * dt)
    inv = 1.0 / (lr * lr + li * li)
    fr, fi = _cmul(ar - 1.0, ai, lr * inv, -li * inv)
    return dt, ar, ai, fr, fi, inv


def ssm_prep(name, lr, li, ld, br, bi):
    def body(lr_ref, li_ref, ld_ref, br_ref, bi_ref, ar_ref, ai_ref, bbr_ref, bbi_ref):
        _, ar, ai, fr, fi, _ = _ssm_disc(lr_ref[...], li_ref[...], ld_ref[...])
        ar_ref[...] = ar
        ai_ref[...] = ai
        bbr, bbi = _cmul(fr, fi, br_ref[...], bi_ref[...])
        bbr_ref[...] = bbr
        bbi_ref[...] = bbi

    sd = jax.ShapeDtypeStruct
    return pl.pallas_call(
        body, name=name,
        out_shape=[sd(lr.shape, F32), sd(lr.shape, F32), sd(br.shape, F32), sd(br.shape, F32)],
    )(lr, li, ld, br, bi)


def ssm_prep_bwd(name, lr, li, ld, br, bi, gar, gai, gbr, gbi):
    def body(lr_ref, li_ref, ld_ref, br_ref, bi_ref, gar_ref, gai_ref, gbr_ref, gbi_ref,
             dlr_ref, dli_ref, dld_ref, dbr_ref, dbi_ref):
        lr_, li_ = lr_ref[...], li_ref[...]
        dt, ar, ai, fr, fi, inv = _ssm_disc(lr_, li_, ld_ref[...])
        gbr_, gbi_ = gbr_ref[...], gbi_ref[...]
        dbr, dbi = _cmul_conj(fr, fi, gbr_, gbi_)
        dbr_ref[...] = dbr
        dbi_ref[...] = dbi
        pr, pi = _cmul_conj(br_ref[...], bi_ref[...], gbr_, gbi_)
        gfr = jnp.sum(pr, axis=1, keepdims=True)
        gfi = jnp.sum(pi, axis=1, keepdims=True)
        ilr, ili = lr_ * inv, -li_ * inv
        tr_, ti_ = _cmul_conj(ilr, ili, gfr, gfi)
        gatr, gati = gar_ref[...] + tr_, gai_ref[...] + ti_
        hr, hi = _cmul(fr, fi, ilr, ili)
        t1r, t1i = _cmul_conj(ar * dt, ai * dt, gatr, gati)
        t2r, t2i = _cmul_conj(hr, hi, gfr, gfi)
        dlr_ref[...] = t1r - t2r
        dli_ref[...] = t1i - t2i
        lar, lai = _cmul(lr_, li_, ar, ai)
        gdt, _ = _cmul_conj(lar, lai, gatr, gati)
        dld_ref[...] = jnp.sum(gdt, axis=2, keepdims=True) * dt

    sd = jax.ShapeDtypeStruct
    return pl.pallas_call(
        body, name=name,
        out_shape=[sd(lr.shape, F32), sd(lr.shape, F32), sd(ld.shape, F32), sd(br.shape, F32), sd(br.shape, F32)],
    )(lr, li, ld, br, bi, gar, gai, gbr, gbi)


SCAN_ROWS = 64


def _scan(xr, xi, ar, ai, L, reverse):
    R = min(SCAN_ROWS, L)
    nt = L // R
    npass = int(round(math.log2(L)))
    assert (1 << npass) == L and L % R == 0
    ns = CHUNK_X // LANE

    def update(rows_cur, load_shift, pr, pi):
        for c in range(ns):
            cs = pl.ds(c * LANE, LANE)
            sr, si = load_shift(xr, cs), load_shift(xi, cs)
            cr, ci = pr[:, c * LANE:(c + 1) * LANE], pi[:, c * LANE:(c + 1) * LANE]
            xr[rows_cur, cs] = xr[rows_cur, cs] + cr * sr - ci * si
            xi[rows_cur, cs] = xi[rows_cur, cs] + cr * si + ci * sr

    pr, pi = ar, ai
    for p in range(npass):
        d = 1 << p
        if d >= R:
            skip = d // R

            def tile(n, carry, d=d, pr=pr, pi=pi, skip=skip):
                t = (n if reverse else nt - 1 - n)
                r0 = pl.multiple_of(t * R, R)
                src = pl.multiple_of(r0 + d if reverse else r0 - d, R)
                update(pl.ds(r0, R), lambda ref, cs: ref[pl.ds(src, R), cs], pr, pi)
                return carry

            lax.fori_loop(0, nt - skip, tile, 0)
        elif d >= 8:
            def tile(n, carry, d=d, pr=pr, pi=pi):
                t = (n if reverse else nt - 1 - n)
                r0 = pl.multiple_of(t * R, 8)
                src = pl.multiple_of(r0 + d if reverse else r0 - d, 8)
                update(pl.ds(r0, R), lambda ref, cs: ref[pl.ds(src, R), cs], pr, pi)
                return carry

            if nt > 1:
                lax.fori_loop(0, nt - 1, tile, 0)
            if reverse:
                update(pl.ds(L - R, R - d), lambda ref, cs: ref[pl.ds(L - R + d, R - d), cs], pr, pi)
            else:
                update(pl.ds(d, R - d), lambda ref, cs: ref[pl.ds(0, R - d), cs], pr, pi)
        else:
            def tile(n, carry, d=d, pr=pr, pi=pi):
                t = (n if reverse else nt - 1 - n)
                r0 = pl.multiple_of(t * R, 8)
                if reverse:
                    load = lambda ref, cs: pltpu.roll(ref[pl.ds(r0, R + 8), cs], R + 8 - d, 0)[:R]
                else:
                    load = lambda ref, cs: pltpu.roll(ref[pl.ds(pl.multiple_of(r0 - 8, 8), R + 8), cs], d, 0)[8:]
                update(pl.ds(r0, R), load, pr, pi)
                return carry

            if nt > 1:
                lax.fori_loop(0, nt - 1, tile, 0)
            ridx = lax.broadcasted_iota(jnp.int32, (R, LANE), 0)
            if reverse:
                load = lambda ref, cs: jnp.where(
                    ridx < R - d, pltpu.roll(ref[pl.ds(L - R, R), cs], R - d, 0), 0.0)
                update(pl.ds(L - R, R), load, pr, pi)
            else:
                load = lambda ref, cs: jnp.where(ridx >= d, pltpu.roll(ref[pl.ds(0, R), cs], d, 0), 0.0)
                update(pl.ds(0, R), load, pr, pi)
        pr, pi = _cmul(pr, pi, pr, pi)


def _gelu(x):
    t = jnp.tanh(0.7978845608028654 * (x + 0.044715 * x * x * x))
    return 0.5 * x * (1.0 + t)


def _gelu_grad(x):
    t = jnp.tanh(0.7978845608028654 * (x + 0.044715 * x * x * x))
    return 0.5 * (1.0 + t) + 0.5 * x * (1.0 - t * t) * 0.7978845608028654 * (1.0 + 0.134145 * x * x)


def ssm_fwd(name, proj, wbr, wbi, ar, ai, wcr, wci, dskip, D):
    L = proj.shape[0]
    DS = D // 4
    NC = DS // CHUNK_U
    uo = (5 * D // 2) // CHUNK_U
    ch = _pick(L, 256)

    def body(u_ref, wbr_ref, wbi_ref, ar_ref, ai_ref, wcr_ref, wci_ref, ds_ref,
             y_ref, hg_ref, xr_ref, xi_ref, sr, si):
        def fill(ci, carry):
            rows = pl.ds(pl.multiple_of(ci * ch, ch), ch)
            ub = u_ref[rows, :].astype(BF16)
            sr[rows, :] = _dot(ub, wbr_ref[...])
            si[rows, :] = _dot(ub, wbi_ref[...])
            return carry

        lax.fori_loop(0, L // ch, fill, 0)
        _scan(sr, si, ar_ref[...], ai_ref[...], L, reverse=False)

        def emit(ci, carry):
            rows = pl.ds(pl.multiple_of(ci * ch, ch), ch)
            xrb, xib = sr[rows, :].astype(BF16), si[rows, :].astype(BF16)
            xr_ref[rows, :] = xrb
            xi_ref[rows, :] = xib
            y = _dot(xrb, wcr_ref[...]) - _dot(xib, wci_ref[...]) + ds_ref[...] * u_ref[rows, :]
            y_ref[rows, :] = y
            hg_ref[rows, :] = _gelu(y).astype(BF16)
            return carry

        lax.fori_loop(0, L // ch, emit, 0)

    ucol = pl.BlockSpec((L, CHUNK_U), lambda k: (0, k))
    xcol = pl.BlockSpec((L, CHUNK_X), lambda k: (0, k))
    sd = jax.ShapeDtypeStruct
    return pl.pallas_call(
        body, name=name, grid=(NC,),
        in_specs=[pl.BlockSpec((L, CHUNK_U), lambda k: (0, uo + k)),
                  pl.BlockSpec((None, CHUNK_U, CHUNK_X), lambda k: (k, 0, 0)),
                  pl.BlockSpec((None, CHUNK_U, CHUNK_X), lambda k: (k, 0, 0)),
                  pl.BlockSpec((1, CHUNK_X), lambda k: (0, k)), pl.BlockSpec((1, CHUNK_X), lambda k: (0, k)),
                  pl.BlockSpec((None, CHUNK_X, CHUNK_U), lambda k: (k, 0, 0)),
                  pl.BlockSpec((None, CHUNK_X, CHUNK_U), lambda k: (k, 0, 0)),
                  pl.BlockSpec((1, CHUNK_U), lambda k: (0, k))],
        out_specs=[ucol, ucol, xcol, xcol],
        out_shape=[sd((L, DS), F32), sd((L, DS), BF16), sd((L, 4 * DS), BF16), sd((L, 4 * DS), BF16)],
        scratch_shapes=[pltpu.VMEM((L, CHUNK_X), F32), pltpu.VMEM((L, CHUNK_X), F32)],
        compiler_params=_cparams(("parallel",), VMEM_BIG))(proj, wbr, wbi, ar, ai, wcr, wci, dskip)


def ssm_bwd(name, dhg, ypre, proj, xr, xi, wbr, wbi, ar, ai, wcr, wci, dskip, D):
    L = proj.shape[0]
    DS = D // 4
    NC = DS // CHUNK_U
    uo = (5 * D // 2) // CHUNK_U
    ch = _pick(L, 256)
    nch = L // ch
    halo = 16

    def body(dhg_ref, y_ref, u_ref, xr_ref, xi_ref, wbr_ref, wbi_ref, ar_ref, ai_ref, wcr_ref, wci_ref,
             ds_ref, du_ref, dwcr_ref, dwci_ref, dwbr_ref, dwbi_ref, dar_ref, dai_ref, dds_ref, gr, gi, duf):
        dwcr_ref[...] = jnp.zeros_like(dwcr_ref)
        dwci_ref[...] = jnp.zeros_like(dwci_ref)
        dwbr_ref[...] = jnp.zeros_like(dwbr_ref)
        dwbi_ref[...] = jnp.zeros_like(dwbi_ref)
        dar_ref[...] = jnp.zeros_like(dar_ref)
        dai_ref[...] = jnp.zeros_like(dai_ref)
        dds_ref[...] = jnp.zeros_like(dds_ref)

        def first(ci, carry):
            rows = pl.ds(pl.multiple_of(ci * ch, ch), ch)
            dy = dhg_ref[rows, :] * _gelu_grad(y_ref[rows, :])
            dyb = dy.astype(BF16)
            dds_ref[...] += jnp.sum(dy * u_ref[rows, :], axis=0, keepdims=True)
            duf[rows, :] = ds_ref[...] * dy
            gr[rows, :] = _dot(dyb, wcr_ref[...], NT)
            gi[rows, :] = -_dot(dyb, wci_ref[...], NT)
            dwcr_ref[...] += _dot(xr_ref[rows, :], dyb, TN)
            dwci_ref[...] -= _dot(xi_ref[rows, :], dyb, TN)
            return carry

        lax.fori_loop(0, nch, first, 0)
        _scan(gr, gi, ar_ref[...], -ai_ref[...], L, reverse=True)

        def lam_grad(gxr, gxi, xpr, xpi):
            pr, pi = _cmul_conj(xpr, xpi, gxr, gxi)
            dar_ref[...] += jnp.sum(pr, axis=0, keepdims=True)
            dai_ref[...] += jnp.sum(pi, axis=0, keepdims=True)

        def second(ci, carry):
            r0 = pl.multiple_of(ci * ch, ch)
            rows = pl.ds(r0, ch)
            gxr, gxi = gr[rows, :], gi[rows, :]
            gxrb, gxib = gxr.astype(BF16), gxi.astype(BF16)
            du_ref[rows, :] = (duf[rows, :] + _dot(gxrb, wbr_ref[...], NT) + _dot(gxib, wbi_ref[...], NT)).astype(BF16)
            ub = u_ref[rows, :].astype(BF16)
            dwbr_ref[...] += _dot(ub, gxrb, TN)
            dwbi_ref[...] += _dot(ub, gxib, TN)
            return carry

        lax.fori_loop(0, nch, second, 0)

        ridx = lax.broadcasted_iota(jnp.int32, (ch, CHUNK_X), 0)
        xpr = jnp.where(ridx >= 1, pltpu.roll(xr_ref[0:ch, :].astype(F32), 1, 0), 0.0)
        xpi = jnp.where(ridx >= 1, pltpu.roll(xi_ref[0:ch, :].astype(F32), 1, 0), 0.0)
        lam_grad(gr[0:ch, :], gi[0:ch, :], xpr, xpi)

        def third(ci, carry):
            r0 = pl.multiple_of(ci * ch, ch)
            ext = pl.ds(pl.multiple_of(r0 - halo, halo), ch + halo)
            xpr = pltpu.roll(xr_ref[ext, :].astype(F32), 1, 0)[halo:]
            xpi = pltpu.roll(xi_ref[ext, :].astype(F32), 1, 0)[halo:]
            lam_grad(gr[pl.ds(r0, ch), :], gi[pl.ds(r0, ch), :], xpr, xpi)
            return carry

        if nch > 1:
            lax.fori_loop(1, nch, third, 0)

    ucol = pl.BlockSpec((L, CHUNK_U), lambda k: (0, k))
    xcol = pl.BlockSpec((L, CHUNK_X), lambda k: (0, k))
    wb_spec = pl.BlockSpec((None, CHUNK_U, CHUNK_X), lambda k: (k, 0, 0))
    wc_spec = pl.BlockSpec((None, CHUNK_X, CHUNK_U), lambda k: (k, 0, 0))
    avec = pl.BlockSpec((1, CHUNK_X), lambda k: (0, k))
    uvec = pl.BlockSpec((1, CHUNK_U), lambda k: (0, k))
    sd = jax.ShapeDtypeStruct
    return pl.pallas_call(
        body, name=name, grid=(NC,),
        in_specs=[ucol, ucol, pl.BlockSpec((L, CHUNK_U), lambda k: (0, uo + k)), xcol, xcol,
                  wb_spec, wb_spec, avec, avec, wc_spec, wc_spec, uvec],
        out_specs=[ucol, wc_spec, wc_spec, wb_spec, wb_spec, avec, avec, uvec],
        out_shape=[sd((L, DS), BF16), sd((NC, CHUNK_X, CHUNK_U), F32), sd((NC, CHUNK_X, CHUNK_U), F32),
                   sd((NC, CHUNK_U, CHUNK_X), F32), sd((NC, CHUNK_U, CHUNK_X), F32),
                   sd((1, 4 * DS), F32), sd((1, 4 * DS), F32), sd((1, DS), F32)],
        scratch_shapes=[pltpu.VMEM((L, CHUNK_X), F32), pltpu.VMEM((L, CHUNK_X), F32), pltpu.VMEM((L, CHUNK_U), F32)],
        compiler_params=_cparams(("parallel",), VMEM_BIG),
    )(dhg, ypre, proj, xr, xi, wbr, wbi, ar, ai, wcr, wci, dskip)


def _block_diag(w, transpose):
    G = w.shape[0]
    nc = G // GROUPS_PER_CHUNK
    w4 = w.reshape(nc, GROUPS_PER_CHUNK, SSM_GROUP, SSM_STATE)
    eye = jnp.eye(GROUPS_PER_CHUNK, dtype=w.dtype)
    if transpose:
        return (w4[:, None, :, :, :].transpose(0, 1, 4, 2, 3) * eye[None, :, None, :, None]).reshape(
            nc, CHUNK_X, CHUNK_U).astype(BF16)
    return (w4[:, :, :, None, :] * eye[None, :, None, :, None]).reshape(nc, CHUNK_U, CHUNK_X).astype(BF16)


def _diag_blocks(dw, transpose):
    nc = dw.shape[0]
    gpc = GROUPS_PER_CHUNK
    if transpose:
        d5 = dw.reshape(nc, gpc, SSM_STATE, gpc, SSM_GROUP)
        blocks = [d5[:, g, :, g, :] for g in range(gpc)]
        return jnp.stack(blocks, axis=1).transpose(0, 1, 3, 2).reshape(nc * gpc, SSM_GROUP, SSM_STATE)
    d5 = dw.reshape(nc, gpc, SSM_GROUP, gpc, SSM_STATE)
    blocks = [d5[:, g, :, g, :] for g in range(gpc)]
    return jnp.stack(blocks, axis=1).reshape(nc * gpc, SSM_GROUP, SSM_STATE)


SHARD_BLOCK_ELEMS = 128 * 1024


def _shard_rows(R, C, scale):
    return _pick(R, max(8, scale * SHARD_BLOCK_ELEMS // C))


def cast_bf16(name, w, layer):
    shape = w.shape[1:]
    w3 = w.reshape(w.shape[0], -1, shape[-1])
    _, R, C = w3.shape
    tr = _shard_rows(R, C, 4)

    def body(w_ref, o_ref):
        o_ref[...] = w_ref[...].astype(BF16)

    out = pl.pallas_call(body, name=name, grid=(R // tr,),
                         in_specs=[pl.BlockSpec((None, tr, C), lambda i: (layer, i, 0))],
                         out_specs=pl.BlockSpec((tr, C), lambda i: (i, 0)),
                         out_shape=jax.ShapeDtypeStruct((R, C), BF16),
                         compiler_params=_cparams(("parallel",), VMEM_MID))(w3)
    return out.reshape(shape)


def _adamw(w, g, m, v):
    m = ADAM_B1 * m + (1.0 - ADAM_B1) * g
    v = ADAM_B2 * v + (1.0 - ADAM_B2) * (g * g)
    delta = -ADAM_LR * ((m * ADAM_C1) / (jnp.sqrt(v * ADAM_C2) + ADAM_EPS) + ADAM_WD * w)
    return delta, m, v


def chip_partial(name, pos, g, recv_a):
    _, R, C = g.shape
    g4 = g.reshape(4, 2, R, C)
    tr = _shard_rows(R, C, 4)

    def body(pos_ref, g_ref, a_ref, o_ref):
        o_ref[...] = (g_ref[...] + a_ref[...]).astype(BF16)

    return pl.pallas_call(
        body, name=name,
        grid_spec=pltpu.PrefetchScalarGridSpec(
            num_scalar_prefetch=1, grid=(4, R // tr),
            in_specs=[pl.BlockSpec((None, None, tr, C), lambda q, i, p: (q, p[0], i, 0)),
                      pl.BlockSpec((None, tr, C), lambda q, i, p: (q, i, 0))],
            out_specs=pl.BlockSpec((None, tr, C), lambda q, i, p: (q, i, 0))),
        out_shape=jax.ShapeDtypeStruct((4, R, C), BF16),
        compiler_params=_cparams(("parallel", "parallel"), VMEM_MID))(pos, g4, recv_a)


def adamw_shard(name, pos, layer, g, recv_a, recv_b, w, m, v, prev):
    _, R, C = g.shape
    tr = _shard_rows(R, C, 1)
    n_prev = 0 if prev is None else 4

    def body(pos_ref, g_ref, a_ref, b_ref, w_ref, m_ref, v_ref, *rest):
        go_ref, d_ref, mo_ref, vo_ref = rest[n_prev:]
        gs = g_ref[...] + a_ref[...]
        for j in range(3):
            gs = gs + b_ref[j].astype(F32)
        delta, mn, vn = _adamw(w_ref[...], gs, m_ref[...], v_ref[...])
        go_ref[...] = gs
        d_ref[...] = delta
        mo_ref[...] = mn
        vo_ref[...] = vn

    lay = pl.BlockSpec((None, tr, C), lambda i, p: (layer, i, 0))
    in_specs = [pl.BlockSpec((None, tr, C), lambda i, p: (p[2], i, 0)),
                pl.BlockSpec((None, tr, C), lambda i, p: (p[1], i, 0)),
                pl.BlockSpec((3, tr, C), lambda i, p: (0, i, 0)), lay, lay, lay]
    args = [g, recv_a, recv_b, w, m, v]
    aliases = {}
    if prev is not None:
        in_specs += [pl.BlockSpec(memory_space=pl.ANY)] * 4
        args += list(prev)
        aliases = {7 + j: j for j in range(4)}
    return pl.pallas_call(
        body, name=name,
        grid_spec=pltpu.PrefetchScalarGridSpec(
            num_scalar_prefetch=1, grid=(R // tr,), in_specs=in_specs, out_specs=[lay] * 4),
        out_shape=[jax.ShapeDtypeStruct(w.shape, F32)] * 4,
        input_output_aliases=aliases,
        compiler_params=_cparams(("parallel",), VMEM_MID))(pos, *args)


def adamw_small(name, gathered, w, m, v):
    _, R, C = gathered.shape
    tr = _pick(R, 512)

    def body(g_ref, w_ref, m_ref, v_ref, go_ref, d_ref, mo_ref, vo_ref):
        gs = g_ref[0]
        for j in range(1, N_DEV):
            gs = gs + g_ref[j]
        delta, mn, vn = _adamw(w_ref[...], gs, m_ref[...], v_ref[...])
        go_ref[...] = gs
        d_ref[...] = delta
        mo_ref[...] = mn
        vo_ref[...] = vn

    spec = pl.BlockSpec((tr, C), lambda i: (i, 0))
    return pl.pallas_call(
        body, name=name, grid=(R // tr,),
        in_specs=[pl.BlockSpec((N_DEV, tr, C), lambda i: (0, i, 0)), spec, spec, spec], out_specs=[spec] * 4,
        out_shape=[jax.ShapeDtypeStruct((R, C), F32)] * 4,
        compiler_params=_cparams(("parallel",), VMEM_MID))(gathered, w, m, v)


def _position():
    return lax.axis_index("x"), lax.axis_index("y"), lax.axis_index("c")


def all_gather(name, shards):
    n = len(shards)

    def body(*refs):
        ins, outs = refs[:n], refs[n:2 * n]
        send_sems, recv_sems, local_sems = refs[2 * n:]
        x, y, c = _position()
        me, sibling = (x, y, c), (x, y, 1 - c)
        chips = [(1 - x, y), (x, 1 - y), (1 - x, 1 - y)]

        def copy(a, k, block, to, src=None):
            blk = outs[a].at[4 * block[0] + 2 * block[1] + block[2]]
            return pltpu.make_async_remote_copy(
                src_ref=blk if src is None else src, dst_ref=blk,
                send_sem=send_sems.at[a, k], recv_sem=recv_sems.at[a, k], device_id=to, device_id_type=MESH)

        started = []
        mine = []
        for a in range(n):
            cp = pltpu.make_async_copy(ins[a], outs[a].at[4 * x + 2 * y + c], local_sems.at[a])
            cp.start()
            mine.append(cp)
            first = [copy(a, 0, me, sibling, src=ins[a])]
            first += [copy(a, 1 + j, me, (*chip, c), src=ins[a]) for j, chip in enumerate(chips)]
            for cp in first:
                cp.start()
            started += first
        for a in range(n):
            for j, chip in enumerate(chips):
                copy(a, 1 + j, (*chip, c), me).wait_recv()
                passed = copy(a, 4 + j, (*chip, c), sibling)
                passed.start()
                started.append(passed)
        for a in range(n):
            copy(a, 0, sibling, me).wait_recv()
            for j, chip in enumerate(chips):
                copy(a, 4 + j, (*chip, 1 - c), me).wait_recv()
        for cp in started:
            cp.wait_send()
        for cp in mine:
            cp.wait()

    hbm = pl.BlockSpec(memory_space=pl.ANY)
    return pl.pallas_call(
        body, name=name, in_specs=[hbm] * n, out_specs=[hbm] * n,
        out_shape=[jax.ShapeDtypeStruct((N_DEV,) + s.shape, s.dtype) for s in shards],
        scratch_shapes=[pltpu.SemaphoreType.DMA((n, 7)), pltpu.SemaphoreType.DMA((n, 7)),
                        pltpu.SemaphoreType.DMA((n,))],
        compiler_params=pltpu.CompilerParams(has_side_effects=True),
    )(*shards)


def exchange_sibling(name, grads):
    n = len(grads)

    def body(*refs):
        ins, outs = refs[:n], refs[n:2 * n]
        send_sems, recv_sems = refs[2 * n:]
        x, y, c = _position()
        copies = []
        for a in range(n):
            for q in range(4):
                cp = pltpu.make_async_remote_copy(
                    src_ref=ins[a].at[2 * q + (1 - c)], dst_ref=outs[a].at[q],
                    send_sem=send_sems.at[a, q], recv_sem=recv_sems.at[a, q],
                    device_id=(x, y, 1 - c), device_id_type=MESH)
                cp.start()
                copies.append(cp)
        for cp in copies:
            cp.wait()

    hbm = pl.BlockSpec(memory_space=pl.ANY)
    return pl.pallas_call(
        body, name=name, in_specs=[hbm] * n, out_specs=[hbm] * n,
        out_shape=[jax.ShapeDtypeStruct((4,) + g.shape[1:], g.dtype) for g in grads],
        scratch_shapes=[pltpu.SemaphoreType.DMA((n, 4)), pltpu.SemaphoreType.DMA((n, 4))],
        compiler_params=pltpu.CompilerParams(has_side_effects=True),
    )(*grads)


def exchange_chips(name, parts):
    n = len(parts)

    def body(*refs):
        ins, outs = refs[:n], refs[n:2 * n]
        send_sems, recv_sems = refs[2 * n:]
        x, y, c = _position()
        chips = [(1 - x, y), (x, 1 - y), (1 - x, 1 - y)]
        copies = []
        for a in range(n):
            for j, chip in enumerate(chips):
                cp = pltpu.make_async_remote_copy(
                    src_ref=ins[a].at[2 * chip[0] + chip[1]], dst_ref=outs[a].at[j],
                    send_sem=send_sems.at[a, j], recv_sem=recv_sems.at[a, j],
                    device_id=(*chip, c), device_id_type=MESH)
                cp.start()
                copies.append(cp)
        for cp in copies:
            cp.wait()

    hbm = pl.BlockSpec(memory_space=pl.ANY)
    return pl.pallas_call(
        body, name=name, in_specs=[hbm] * n, out_specs=[hbm] * n,
        out_shape=[jax.ShapeDtypeStruct((3,) + p.shape[1:], p.dtype) for p in parts],
        scratch_shapes=[pltpu.SemaphoreType.DMA((n, 3)), pltpu.SemaphoreType.DMA((n, 3))],
        compiler_params=pltpu.CompilerParams(has_side_effects=True),
    )(*parts)


def _layer_params(l, ln_g, wp_full, pool_scale, lam_re, lam_im, log_dt, b_re, b_im, c_re, c_im,
                  d_skip, b_glu, branch_g):
    G, P = lam_re.shape[1:]
    p = dict(
        ln_g=ln_g[l][None, :], pool_scale=pool_scale[l][None, :], d_skip=d_skip[l][None, :],
        b_glu=b_glu[l][None, :], branch_g=branch_g[l][None, :], wp=wp_full[l],
        lr=lam_re[l].reshape(G, 1, P), li=lam_im[l].reshape(G, 1, P), ld=log_dt[l].reshape(G, 1, 1),
        br=b_re[l].transpose(0, 2, 1), bi=b_im[l].transpose(0, 2, 1), cr=c_re[l], ci=c_im[l])
    return p


def layer_fwd(l, x, p, w_in_g, w_out_g, w_glu_g, D):
    t = f"l{l}_"
    h = rms_fwd(t + "rms_fwd", x, p["ln_g"])
    proj = mm_nn_gathered(t + "proj", h, w_in_g)
    ypool = pool_fwd(t + "pool_fwd", proj, p["wp"], p["pool_scale"])
    yattn = attn_fwd(t + "attn_fwd", proj, D)
    ar, ai, bbr, bbi = ssm_prep(t + "ssm_prep", p["lr"], p["li"], p["ld"], p["br"], p["bi"])
    ssm_w = dict(wbr=_block_diag(bbr, False), wbi=_block_diag(bbi, False),
                 ar=ar.reshape(1, -1), ai=ai.reshape(1, -1),
                 wcr=_block_diag(p["cr"], True), wci=_block_diag(p["ci"], True))
    ypre, hg, xr, xi = ssm_fwd(t + "ssm_fwd", proj, ssm_w["wbr"], ssm_w["wbi"], ssm_w["ar"], ssm_w["ai"],
                               ssm_w["wcr"], ssm_w["wci"], p["d_skip"], D)
    glu_pre = mm_nn_gathered(t + "glu", hg, w_glu_g)
    y = branch_fwd(t + "branch_fwd", ypool, yattn, glu_pre, proj, p["b_glu"], p["branch_g"])
    out = mm_plain(t + "out", y, w_out_g, NN, res=x)
    saved = dict(x=x, h=h, proj=proj, ypool=ypool, yattn=yattn, ypre=ypre, hg=hg, xr=xr, xi=xi,
                 glu_pre=glu_pre, y=y, ssm_w=ssm_w)
    return out, saved


def layer_bwd(l, dres, s, p, w_in_g, w_out_g, w_glu_g, D):
    t = f"l{l}_"
    proj = s["proj"]
    dy = mm_plain(t + "dy", dres, w_out_g, NT)
    dw_out = mm_plain(t + "dw_out", s["y"], dres, TN)
    dypool, dyattn, dglu, dpg, dag, dsg, dbg, dbglu = branch_bwd(
        t + "branch_bwd", dy, s["ypool"], s["yattn"], s["glu_pre"], proj, p["b_glu"], p["branch_g"])
    dhg = mm_nt_gathered(t + "dhg", dglu, w_glu_g)
    dw_glu = mm_tn_scattered(t + "dw_glu", s["hg"], dglu)
    w = s["ssm_w"]
    du, dwcr, dwci, dwbr, dwbi, dar, dai, dds = ssm_bwd(
        t + "ssm_bwd", dhg, s["ypre"], proj, s["xr"], s["xi"], w["wbr"], w["wbi"], w["ar"], w["ai"],
        w["wcr"], w["wci"], p["d_skip"], D)
    G, _, P = p["lr"].shape
    dlr, dli, dld, dbr, dbi = ssm_prep_bwd(
        t + "ssm_prep_bwd", p["lr"], p["li"], p["ld"], p["br"], p["bi"],
        dar.reshape(G, 1, P), dai.reshape(G, 1, P), _diag_blocks(dwbr, False), _diag_blocks(dwbi, False))
    dq, dk, dv = attn_bwd(t + "attn_bwd", proj, s["yattn"], dyattn, D)
    dxp, dwp, dps = pool_bwd(t + "pool_bwd", dypool, proj, p["wp"], p["pool_scale"])
    dproj = jnp.concatenate([dxp, dpg, dq, dk, dv, dag, du, dsg], axis=1)
    dh = mm_nt_gathered(t + "dh", dproj, w_in_g)
    dw_in = mm_tn_scattered(t + "dw_in", s["h"], dproj)
    dx, dlng = rms_bwd(t + "rms_bwd", s["x"], dh, dres, p["ln_g"])
    PG = dwp.shape[1]
    dwp_s = dwp.reshape(N_POOL_GROUPS, N_DEV, PG // N_DEV, PG).transpose(1, 0, 2, 3).reshape(N_DEV, -1, PG)
    big = dict(w_in=dw_in, w_out=dw_out.reshape(N_DEV, D // N_DEV, D), w_glu=dw_glu, w_pool=dwp_s)
    small = dict(ln_g=dlng[0], pool_scale=dps[0], lam_re=dlr.reshape(G, P), lam_im=dli.reshape(G, P),
                 log_dt=dld.reshape(G), b_re=dbr.transpose(0, 2, 1), b_im=dbi.transpose(0, 2, 1),
                 c_re=_diag_blocks(dwcr, True), c_im=_diag_blocks(dwci, True),
                 d_skip=dds[0], b_glu=dbglu[0], branch_g=dbg[0])
    return dx, big, small


SMALL_NAMES = ("ln_g", "pool_scale", "lam_re", "lam_im", "log_dt", "b_re", "b_im", "c_re", "c_im",
               "d_skip", "b_glu", "branch_g", "final_g")
BIG_NAMES = ("w_in", "w_pool", "w_glu", "w_out")
WEIGHT_ORDER = ("ln_g", "w_in", "w_pool", "pool_scale", "lam_re", "lam_im", "log_dt", "b_re", "b_im",
                "c_re", "c_im", "d_skip", "w_glu", "b_glu", "branch_g", "w_out", "final_g")


PACK_ROWS = 512


def _pack(arrs):
    flat = jnp.concatenate([a.reshape(-1) for a in arrs])
    pad = (-flat.shape[0]) % (PACK_ROWS * LANE)
    return jnp.pad(flat, (0, pad)).reshape(-1, LANE)


def _unpack(packed, like):
    flat = packed.reshape(-1)
    out, off = [], 0
    for a in like:
        out.append(flat[off:off + a.size].reshape(a.shape))
        off += a.size
    return out


def kernel(x, ln_g, w_in, w_pool, pool_scale, lam_re, lam_im, log_dt, b_re, b_im, c_re, c_im, d_skip, w_glu, b_glu, branch_g, w_out, final_g, loss_target, m_ln_g, m_w_in, m_w_pool, m_pool_scale, m_lam_re, m_lam_im, m_log_dt, m_b_re, m_b_im, m_c_re, m_c_im, m_d_skip, m_w_glu, m_b_glu, m_branch_g, m_w_out, m_final_g, v_ln_g, v_w_in, v_w_pool, v_pool_scale, v_lam_re, v_lam_im, v_log_dt, v_b_re, v_b_im, v_c_re, v_c_im, v_d_skip, v_w_glu, v_b_glu, v_branch_g, v_w_out, v_final_g):
    W = dict(ln_g=ln_g, w_in=w_in, w_pool=w_pool, pool_scale=pool_scale, lam_re=lam_re, lam_im=lam_im,
             log_dt=log_dt, b_re=b_re, b_im=b_im, c_re=c_re, c_im=c_im, d_skip=d_skip, w_glu=w_glu,
             b_glu=b_glu, branch_g=branch_g, w_out=w_out, final_g=final_g)
    Mo = dict(ln_g=m_ln_g, w_in=m_w_in, w_pool=m_w_pool, pool_scale=m_pool_scale, lam_re=m_lam_re,
              lam_im=m_lam_im, log_dt=m_log_dt, b_re=m_b_re, b_im=m_b_im, c_re=m_c_re, c_im=m_c_im,
              d_skip=m_d_skip, w_glu=m_w_glu, b_glu=m_b_glu, branch_g=m_branch_g, w_out=m_w_out,
              final_g=m_final_g)
    Vo = dict(ln_g=v_ln_g, w_in=v_w_in, w_pool=v_w_pool, pool_scale=v_pool_scale, lam_re=v_lam_re,
              lam_im=v_lam_im, log_dt=v_log_dt, b_re=v_b_re, b_im=v_b_im, c_re=v_c_re, c_im=v_c_im,
              d_skip=v_d_skip, w_glu=v_w_glu, b_glu=v_b_glu, branch_g=v_branch_g, w_out=v_w_out,
              final_g=v_final_g)
    depth = ln_g.shape[0]
    _, L, D = x.shape
    xc, yc, cc = _position()
    pos = jnp.stack([cc, 2 * xc + yc, 4 * xc + 2 * yc + cc]).astype(jnp.int32)

    shards = [cast_bf16(f"cast_{n}_{l}", W[n], l) for n in BIG_NAMES for l in range(depth)]
    gathered_w = all_gather("gather_weights", shards)
    w_g = {n: gathered_w[i * depth:(i + 1) * depth] for i, n in enumerate(BIG_NAMES)}
    PG = w_pool.shape[-1]
    wp_full = [t.transpose(1, 0, 2, 3).reshape(N_POOL_GROUPS, PG, PG) for t in w_g["w_pool"]]

    params = [_layer_params(l, ln_g, wp_full, pool_scale, lam_re, lam_im, log_dt, b_re, b_im, c_re, c_im,
                            d_skip, b_glu, branch_g) for l in range(depth)]
    layer_w = [(w_g["w_in"][l], w_g["w_out"][l].reshape(D, D), w_g["w_glu"][l]) for l in range(depth)]

    h = x[0]
    saved = []
    for l in range(depth):
        h, s = layer_fwd(l, h, params[l], *layer_w[l], D)
        saved.append(s)
    loss_part, dres, dfinal = loss_head("loss_head", h, final_g[None, :], loss_target[0])
    loss = lax.psum(loss_part[0, 0], ("x", "y", "c"))

    big = [None] * depth
    small = [None] * depth
    for l in reversed(range(depth)):
        dres, big[l], small[l] = layer_bwd(l, dres, saved[l], params[l], *layer_w[l], D)
    grad_x = dres[None]

    names = [(n, l) for n in BIG_NAMES for l in range(depth)]
    g_list = [big[l][n] for n, l in names]
    recv_a = exchange_sibling("reduce_sibling", g_list)
    parts = [chip_partial(f"chip_partial_{n}_{l}", pos, g, a) for (n, l), g, a in zip(names, g_list, recv_a)]
    recv_b = exchange_chips("reduce_chips", parts)
    results = {}
    for n in BIG_NAMES:
        shape = W[n].shape
        R, C = int(math.prod(shape[1:-1])), shape[-1]
        w3, m3, v3 = (t.reshape(depth, R, C) for t in (W[n], Mo[n], Vo[n]))
        prev = None
        for l in range(depth):
            k = names.index((n, l))
            prev = adamw_shard(f"adamw_{n}_{l}", pos, l, g_list[k], recv_a[k], recv_b[k], w3, m3, v3, prev)
        results[n] = [t.reshape(shape) for t in prev]

    small_like = [W[n] for n in SMALL_NAMES]
    small_grads = [jnp.stack([small[l][n] for l in range(depth)]) for n in SMALL_NAMES[:-1]] + [dfinal[0]]
    gathered = all_gather("gather_small_grads", [_pack(small_grads)])[0]
    packed = adamw_small("adamw_small", gathered, _pack(small_like), _pack([Mo[n] for n in SMALL_NAMES]),
                         _pack([Vo[n] for n in SMALL_NAMES]))
    unpacked = [_unpack(t, small_like) for t in packed]
    for i, n in enumerate(SMALL_NAMES):
        results[n] = [unpacked[j][i] for j in range(4)]

    out = [loss, grad_x]
    for j in range(4):
        out += [results[n][j] for n in WEIGHT_ORDER]
    return tuple(out)
```

```python
import functools
import math

import jax
import jax.numpy as jnp
from jax import lax
from jax.experimental import pallas as pl
from jax.experimental.pallas import tpu as pltpu

F32 = jnp.float32
BF16 = jnp.bfloat16
MESH = pl.DeviceIdType.MESH

EPS = 1e-6
HEAD_DIM = 128
SSM_GROUP = 16
SSM_STATE = 64
GROUPS_PER_CHUNK = 8
CHUNK_U = GROUPS_PER_CHUNK * SSM_GROUP
CHUNK_X = GROUPS_PER_CHUNK * SSM_STATE
N_POOL_GROUPS = 4
POOL_HALO = 16
N_DEV = 8
LANE = 128
ATTN_TILE = 256
ATTN_DECAY_CUTOFF = 100.0
ROW_TILE = 128
VMEM_BIG = 58 * 1024 * 1024
VMEM_MID = 40 * 1024 * 1024

ADAM_LR = 0.001
ADAM_B1 = 0.9
ADAM_B2 = 0.999
ADAM_EPS = 1e-08
ADAM_WD = 0.01
ADAM_STEP = 10
ADAM_C1 = 1.0 / (1.0 - ADAM_B1 ** ADAM_STEP)
ADAM_C2 = 1.0 / (1.0 - ADAM_B2 ** ADAM_STEP)

NN = (((1,), (0,)), ((), ()))
NT = (((1,), (1,)), ((), ()))
TN = (((0,), (0,)), ((), ()))


def _pick(n, cap):
    if n <= cap:
        return n
    step = LANE if cap >= LANE else 8
    t = (cap // step) * step
    while t > step and n % t:
        t -= step
    assert n % t == 0, (n, cap)
    return t


def _cparams(sem, vmem=None):
    return pltpu.CompilerParams(dimension_semantics=sem, vmem_limit_bytes=vmem)


def _dot(a, b, dn=NN):
    return lax.dot_general(a, b, dn, preferred_element_type=F32)


def _sigmoid(x):
    e = jnp.exp(-jnp.abs(x))
    r = 1.0 / (1.0 + e)
    return jnp.where(x >= 0, r, e * r)


HBM = pl.BlockSpec(memory_space=pl.ANY)


class HostedCopies:
    def __init__(self, inputs, out_shape, scratch, phases):
        self.inputs, self.out_shape, self.scratch, self.phases = inputs, out_shape, scratch, phases

    def emit(self, ins, outs, sems, step, total):
        plan = {}
        for frac, fn in self.phases:
            plan.setdefault(min(total - 1, int(frac * total)), []).append(fn)
        for s in sorted(plan):
            def run(fns=plan[s]):
                for fn in fns:
                    fn(ins, outs, sems)
            if total == 1:
                run()
            else:
                pl.when(step == s)(run)


def copies_call(name, copies):
    n_i, n_o = len(copies.inputs), len(copies.out_shape)

    def body(*refs):
        copies.emit(refs[:n_i], refs[n_i:n_i + n_o], refs[n_i + n_o:], 0, 1)

    return pl.pallas_call(
        body, name=name, in_specs=[HBM] * n_i, out_specs=[HBM] * n_o, out_shape=copies.out_shape,
        scratch_shapes=copies.scratch, compiler_params=pltpu.CompilerParams(has_side_effects=True),
    )(*copies.inputs)


def _matmul(name, a, b, *, grid, a_spec, b_spec, o_spec, out_shape, dn, acc_shape,
            res=None, res_spec=None, pos=None, copies=None):
    ni, nj, nk = grid
    n_pos = 0 if pos is None else 1
    n_res = 0 if res is None else 1
    n_ci = 0 if copies is None else len(copies.inputs)
    n_co = 0 if copies is None else len(copies.out_shape)

    def body(*refs):
        refs = refs[n_pos:]
        a_ref, b_ref = refs[:2]
        r_ref = refs[2] if n_res else None
        base = 2 + n_res
        cin = refs[base:base + n_ci]
        o_ref = refs[base + n_ci]
        cout = refs[base + n_ci + 1:base + n_ci + 1 + n_co]
        acc = refs[base + n_ci + 1 + n_co]
        sems = refs[base + n_ci + 2 + n_co:]
        k = pl.program_id(2)
        if copies is not None:
            step = (pl.program_id(0) * nj + pl.program_id(1)) * nk + k
            copies.emit(cin, cout, sems, step, ni * nj * nk)

        @pl.when(k == 0)
        def _():
            acc[...] = jnp.zeros_like(acc)

        acc[...] += _dot(a_ref[...].astype(BF16), b_ref[...].astype(BF16), dn)

        @pl.when(k == nk - 1)
        def _():
            r = acc[...]
            if r_ref is not None:
                r = r + r_ref[...]
            o_ref[...] = r.astype(o_ref.dtype)

    in_specs = [a_spec, b_spec] + ([res_spec] if n_res else []) + [HBM] * n_ci
    args = ((pos,) if n_pos else ()) + (a, b) + ((res,) if n_res else ()) + tuple(copies.inputs if copies else ())
    out_specs = [o_spec] + [HBM] * n_co
    out_shapes = [out_shape] + list(copies.out_shape if copies else [])
    scratch = [pltpu.VMEM(acc_shape, F32)] + list(copies.scratch if copies else [])
    params = pltpu.CompilerParams(
        dimension_semantics=("arbitrary",) * 3 if copies else ("parallel", "parallel", "arbitrary"),
        vmem_limit_bytes=VMEM_BIG, has_side_effects=copies is not None)
    out = pl.pallas_call(
        body, name=name,
        grid_spec=pltpu.PrefetchScalarGridSpec(
            num_scalar_prefetch=n_pos, grid=grid, in_specs=in_specs, out_specs=out_specs, scratch_shapes=scratch),
        out_shape=out_shapes, compiler_params=params)(*args)
    return out[0] if copies is None else (out[0], list(out[1:]))


def mm_nn_gathered(name, a, wg, out_dtype=F32, copies=None):
    M, K = a.shape
    _, _, nper = wg.shape
    tm, tk, tn = _pick(M, 1024), _pick(K, 512), _pick(nper, 1536)
    r = nper // tn
    return _matmul(
        name, a, wg, grid=(M // tm, N_DEV * r, K // tk),
        a_spec=pl.BlockSpec((tm, tk), lambda i, j, k, *_: (i, k)),
        b_spec=pl.BlockSpec((None, tk, tn), lambda i, j, k, *_: (j // r, k, j % r)),
        o_spec=pl.BlockSpec((tm, tn), lambda i, j, k, *_: (i, j)),
        out_shape=jax.ShapeDtypeStruct((M, N_DEV * nper), out_dtype), dn=NN, acc_shape=(tm, tn), copies=copies)


def mm_nt_gathered(name, a, wg, out_dtype=F32, copies=None):
    M, _ = a.shape
    _, N, nper = wg.shape
    tm, tn, tk = _pick(M, 1024), _pick(N, 1024), _pick(nper, 768)
    r = nper // tk
    return _matmul(
        name, a, wg, grid=(M // tm, N // tn, N_DEV * r),
        a_spec=pl.BlockSpec((tm, tk), lambda i, j, k, *_: (i, k)),
        b_spec=pl.BlockSpec((None, tn, tk), lambda i, j, k, *_: (k // r, j, k % r)),
        o_spec=pl.BlockSpec((tm, tn), lambda i, j, k, *_: (i, j)),
        out_shape=jax.ShapeDtypeStruct((M, N), out_dtype), dn=NT, acc_shape=(tm, tn), copies=copies)


def mm_tn_scattered(name, a, b, copies=None):
    L, M = a.shape
    nper = b.shape[1] // N_DEV
    tm, tn, tk = _pick(M, 1024), _pick(nper, 1536), _pick(L, 512)
    r = nper // tn
    return _matmul(
        name, a, b, grid=(M // tm, N_DEV * r, L // tk),
        a_spec=pl.BlockSpec((tk, tm), lambda i, j, k, *_: (k, i)),
        b_spec=pl.BlockSpec((tk, tn), lambda i, j, k, *_: (k, j)),
        o_spec=pl.BlockSpec((None, tm, tn), lambda i, j, k, *_: (j // r, i, j % r)),
        out_shape=jax.ShapeDtypeStruct((N_DEV, M, nper), F32), dn=TN, acc_shape=(tm, tn), copies=copies)


def mm_tn_half(name, a, b, pos, own, copies=None):
    L, M = a.shape
    nper = b.shape[1] // N_DEV
    tm, tn, tk = _pick(M, 1024), _pick(nper, 1536), _pick(L, 512)
    r = nper // tn

    def b_map(i, j, k, p):
        core = p[0] if own else 1 - p[0]
        return (k, (2 * (j // r) + core) * r + j % r)

    return _matmul(
        name, a, b, grid=(M // tm, 4 * r, L // tk),
        a_spec=pl.BlockSpec((tk, tm), lambda i, j, k, *_: (k, i)),
        b_spec=pl.BlockSpec((tk, tn), b_map),
        o_spec=pl.BlockSpec((None, tm, tn), lambda i, j, k, *_: (j // r, i, j % r)),
        out_shape=jax.ShapeDtypeStruct((4, M, nper), F32), dn=TN, acc_shape=(tm, tn), pos=pos, copies=copies)


def mm_plain(name, a, b, dn, out_dtype=F32, res=None, copies=None):
    if dn == NN:
        (M, K), N = a.shape, b.shape[1]
    elif dn == NT:
        (M, K), N = a.shape, b.shape[0]
    else:
        (K, M), N = a.shape, b.shape[1]
    tm, tn, tk = _pick(M, 1024), _pick(N, 1024), _pick(K, 512)
    a_spec = (pl.BlockSpec((tk, tm), lambda i, j, k, *_: (k, i)) if dn == TN
              else pl.BlockSpec((tm, tk), lambda i, j, k, *_: (i, k)))
    b_spec = (pl.BlockSpec((tn, tk), lambda i, j, k, *_: (j, k)) if dn == NT
              else pl.BlockSpec((tk, tn), lambda i, j, k, *_: (k, j)))
    o_spec = pl.BlockSpec((tm, tn), lambda i, j, k, *_: (i, j))
    return _matmul(
        name, a, b, grid=(M // tm, N // tn, K // tk), a_spec=a_spec, b_spec=b_spec, o_spec=o_spec,
        out_shape=jax.ShapeDtypeStruct((M, N), out_dtype), dn=dn, acc_shape=(tm, tn),
        res=res, res_spec=o_spec if res is not None else None, copies=copies)


def rms_fwd(name, x, g):
    L, D = x.shape
    tr = _pick(L, ROW_TILE)

    def body(x_ref, g_ref, h_ref):
        xv = x_ref[...]
        r = lax.rsqrt(jnp.mean(xv * xv, axis=-1, keepdims=True) + EPS)
        h_ref[...] = (xv * r * g_ref[...]).astype(BF16)

    return pl.pallas_call(
        body, name=name, grid=(L // tr,),
        in_specs=[pl.BlockSpec((tr, D), lambda i: (i, 0)), pl.BlockSpec((1, D), lambda i: (0, 0))],
        out_specs=pl.BlockSpec((tr, D), lambda i: (i, 0)),
        out_shape=jax.ShapeDtypeStruct((L, D), BF16),
        compiler_params=_cparams(("parallel",), VMEM_MID))(x, g)


def rms_bwd(name, x, dh, dres, g):
    L, D = x.shape
    tr = _pick(L, ROW_TILE)

    def body(x_ref, dh_ref, dr_ref, g_ref, dx_ref, dg_ref):
        xv = x_ref[...]
        r = lax.rsqrt(jnp.mean(xv * xv, axis=-1, keepdims=True) + EPS)
        xh = xv * r
        dhv = dh_ref[...]
        dn = dhv * g_ref[...]
        dx_ref[...] = dr_ref[...] + r * (dn - xh * jnp.mean(dn * xh, axis=-1, keepdims=True))

        @pl.when(pl.program_id(0) == 0)
        def _():
            dg_ref[...] = jnp.zeros_like(dg_ref)

        dg_ref[...] += jnp.sum(dhv * xh, axis=0, keepdims=True)

    row = pl.BlockSpec((tr, D), lambda i: (i, 0))
    vec = pl.BlockSpec((1, D), lambda i: (0, 0))
    return pl.pallas_call(
        body, name=name, grid=(L // tr,), in_specs=[row, row, row, vec], out_specs=[row, vec],
        out_shape=[jax.ShapeDtypeStruct((L, D), F32), jax.ShapeDtypeStruct((1, D), F32)],
        compiler_params=_cparams(("arbitrary",), VMEM_MID))(x, dh, dres, g)


def loss_head(name, x, g, target):
    L, D = x.shape
    tr = _pick(L, ROW_TILE)

    def body(x_ref, g_ref, t_ref, loss_ref, dx_ref, dg_ref):
        xv = x_ref[...]
        gv = g_ref[...]
        r = lax.rsqrt(jnp.mean(xv * xv, axis=-1, keepdims=True) + EPS)
        xh = xv * r
        err = xh * gv - t_ref[...]
        dy = err * (1.0 / D)
        dn = dy * gv
        dx_ref[...] = r * (dn - xh * jnp.mean(dn * xh, axis=-1, keepdims=True))

        @pl.when(pl.program_id(0) == 0)
        def _():
            dg_ref[...] = jnp.zeros_like(dg_ref)
            loss_ref[...] = jnp.zeros_like(loss_ref)

        dg_ref[...] += jnp.sum(dy * xh, axis=0, keepdims=True)
        row_loss = jnp.sum(err * err, axis=-1, keepdims=True) * (0.5 / D)
        loss_ref[...] += jnp.sum(row_loss, axis=0, keepdims=True)

    row = pl.BlockSpec((tr, D), lambda i: (i, 0))
    vec = pl.BlockSpec((1, D), lambda i: (0, 0))
    one = pl.BlockSpec((1, 1), lambda i: (0, 0))
    return pl.pallas_call(
        body, name=name, grid=(L // tr,), in_specs=[row, vec, row], out_specs=[one, row, vec],
        out_shape=[jax.ShapeDtypeStruct((1, 1), F32), jax.ShapeDtypeStruct((L, D), F32),
                   jax.ShapeDtypeStruct((1, D), F32)],
        compiler_params=_cparams(("arbitrary",), VMEM_MID))(x, g, target)


def _branch_specs(D, tr):
    DP, DA, DS = D // 4, D // 2, D // 4
    return dict(
        pool=pl.BlockSpec((tr, DP), lambda i: (i, 0)),
        attn=pl.BlockSpec((tr, DA), lambda i: (i, 0)),
        glu=pl.BlockSpec((tr, 2 * DS), lambda i: (i, 0)),
        p_gate=pl.BlockSpec((tr, DP), lambda i: (i, 1)),
        a_gate=pl.BlockSpec((tr, DA), lambda i: (i, 4)),
        s_gate=pl.BlockSpec((tr, DS), lambda i: (i, 11)),
        bglu=pl.BlockSpec((1, 2 * DS), lambda i: (0, 0)),
        bg=pl.BlockSpec((1, D), lambda i: (0, 0)),
        row=pl.BlockSpec((tr, D), lambda i: (i, 0)),
    )


def branch_fwd(name, ypool, yattn, glu_pre, proj, b_glu, branch_g):
    L, DP = ypool.shape
    D = 4 * DP
    DA, DS = D // 2, D // 4
    tr = _pick(L, ROW_TILE)
    s = _branch_specs(D, tr)

    def body(yp_ref, ya_ref, gl_ref, pg_ref, ag_ref, sg_ref, bgl_ref, bg_ref, y_ref):
        pre = gl_ref[...] + bgl_ref[...]
        ys = pre[:, :DS] * _sigmoid(pre[:, DS:])
        bg = bg_ref[...]

        def one(raw, gate, g):
            r = lax.rsqrt(jnp.mean(raw * raw, axis=-1, keepdims=True) + EPS)
            return raw * r * g * (gate * _sigmoid(gate))

        y_ref[:, :DP] = one(yp_ref[...], pg_ref[...], bg[:, :DP]).astype(BF16)
        y_ref[:, DP:DP + DA] = one(ya_ref[...], ag_ref[...], bg[:, DP:DP + DA]).astype(BF16)
        y_ref[:, DP + DA:] = one(ys, sg_ref[...], bg[:, DP + DA:]).astype(BF16)

    return pl.pallas_call(
        body, name=name, grid=(L // tr,),
        in_specs=[s["pool"], s["attn"], s["glu"], s["p_gate"], s["a_gate"], s["s_gate"], s["bglu"], s["bg"]],
        out_specs=s["row"], out_shape=jax.ShapeDtypeStruct((L, D), BF16),
        compiler_params=_cparams(("parallel",), VMEM_MID))(ypool, yattn, glu_pre, proj, proj, proj, b_glu, branch_g)


def branch_bwd(name, dy, ypool, yattn, glu_pre, proj, b_glu, branch_g):
    L, DP = ypool.shape
    D = 4 * DP
    DA, DS = D // 2, D // 4
    tr = _pick(L, ROW_TILE // 2)
    s = _branch_specs(D, tr)

    def body(dy_ref, yp_ref, ya_ref, gl_ref, pg_ref, ag_ref, sg_ref, bgl_ref, bg_ref,
             dyp_ref, dya_ref, dgl_ref, dpg_ref, dag_ref, dsg_ref, dbg_ref, dbgl_ref):
        @pl.when(pl.program_id(0) == 0)
        def _():
            dbg_ref[...] = jnp.zeros_like(dbg_ref)
            dbgl_ref[...] = jnp.zeros_like(dbgl_ref)

        bg = bg_ref[...]

        def one(raw, gate, g, dyb):
            r = lax.rsqrt(jnp.mean(raw * raw, axis=-1, keepdims=True) + EPS)
            n = raw * r
            sg = _sigmoid(gate)
            sl = gate * sg
            dgate = dyb * n * g * (sg * (1.0 + gate * (1.0 - sg)))
            dbg = jnp.sum(dyb * n * sl, axis=0, keepdims=True)
            dn = dyb * g * sl
            draw = r * (dn - n * jnp.mean(dn * n, axis=-1, keepdims=True))
            return draw, dgate, dbg

        draw, dgate, dbg = one(yp_ref[...], pg_ref[...], bg[:, :DP], dy_ref[:, :DP])
        dyp_ref[...] = draw
        dpg_ref[...] = dgate.astype(BF16)
        dbg_ref[:, :DP] += dbg

        draw, dgate, dbg = one(ya_ref[...], ag_ref[...], bg[:, DP:DP + DA], dy_ref[:, DP:DP + DA])
        dya_ref[...] = draw
        dag_ref[...] = dgate.astype(BF16)
        dbg_ref[:, DP:DP + DA] += dbg

        pre = gl_ref[...] + bgl_ref[...]
        val = pre[:, :DS]
        sgt = _sigmoid(pre[:, DS:])
        draw, dgate, dbg = one(val * sgt, sg_ref[...], bg[:, DP + DA:], dy_ref[:, DP + DA:])
        dsg_ref[...] = dgate.astype(BF16)
        dbg_ref[:, DP + DA:] += dbg
        dval = draw * sgt
        dgt = draw * val * sgt * (1.0 - sgt)
        dgl_ref[:, :DS] = dval.astype(BF16)
        dgl_ref[:, DS:] = dgt.astype(BF16)
        dbgl_ref[:, :DS] += jnp.sum(dval, axis=0, keepdims=True)
        dbgl_ref[:, DS:] += jnp.sum(dgt, axis=0, keepdims=True)

    loc = lambda w: pl.BlockSpec((tr, w), lambda i: (i, 0))
    return pl.pallas_call(
        body, name=name, grid=(L // tr,),
        in_specs=[s["row"], s["pool"], s["attn"], s["glu"], s["p_gate"], s["a_gate"], s["s_gate"], s["bglu"], s["bg"]],
        out_specs=[loc(DP), loc(DA), loc(2 * DS), loc(DP), loc(DA), loc(DS), s["bg"], s["bglu"]],
        out_shape=[jax.ShapeDtypeStruct((L, DP), F32), jax.ShapeDtypeStruct((L, DA), F32),
                   jax.ShapeDtypeStruct((L, 2 * DS), BF16), jax.ShapeDtypeStruct((L, DP), BF16),
                   jax.ShapeDtypeStruct((L, DA), BF16), jax.ShapeDtypeStruct((L, DS), BF16),
                   jax.ShapeDtypeStruct((1, D), F32), jax.ShapeDtypeStruct((1, 2 * DS), F32)],
        compiler_params=_cparams(("arbitrary",), VMEM_BIG),
    )(dy, ypool, yattn, glu_pre, proj, proj, proj, b_glu, branch_g)


def _pool_select(g, s2, s4, s8, s16):
    return jnp.where(g == 0, s2, jnp.where(g == 1, s4, jnp.where(g == 2, s8, s16)))


def _pool_window(g):
    return jnp.where(g == 0, 2.0, jnp.where(g == 1, 4.0, jnp.where(g == 2, 8.0, 16.0))).astype(F32)


def _pooled_chunk(pad, g, r0, ch):
    xh = pad[pl.ds(r0, ch + POOL_HALO), :]
    s2 = xh + pltpu.roll(xh, 1, 0)
    s4 = s2 + pltpu.roll(s2, 2, 0)
    s8 = s4 + pltpu.roll(s4, 4, 0)
    s16 = s8 + pltpu.roll(s8, 8, 0)
    win = _pool_select(g, s2, s4, s8, s16)[POOL_HALO:]
    pos = (r0 + 1 + lax.broadcasted_iota(jnp.int32, (ch, 1), 0)).astype(F32)
    return win / jnp.minimum(pos, _pool_window(g)) - xh[POOL_HALO:]


def pool_fwd(name, proj, wp, scale):
    L = proj.shape[0]
    DP = scale.shape[1]
    PG = DP // N_POOL_GROUPS
    ch = _pick(L, 256)

    def body(x_ref, w_ref, s_ref, o_ref, pad):
        g = pl.program_id(0)
        pad[0:POOL_HALO, :] = jnp.zeros((POOL_HALO, PG), F32)
        pad[POOL_HALO:, :] = x_ref[...]

        def chunk(ci, carry):
            r0 = pl.multiple_of(ci * ch, ch)
            pooled = _pooled_chunk(pad, g, r0, ch)
            o_ref[pl.ds(r0, ch), :] = _dot(pooled.astype(BF16), w_ref[...]) * s_ref[...]
            return carry

        lax.fori_loop(0, L // ch, chunk, 0)

    return pl.pallas_call(
        body, name=name, grid=(N_POOL_GROUPS,),
        in_specs=[pl.BlockSpec((L, PG), lambda g: (0, g)), pl.BlockSpec((None, PG, PG), lambda g: (g, 0, 0)),
                  pl.BlockSpec((1, PG), lambda g: (0, g))],
        out_specs=pl.BlockSpec((L, PG), lambda g: (0, g)),
        out_shape=jax.ShapeDtypeStruct((L, DP), F32),
        scratch_shapes=[pltpu.VMEM((L + POOL_HALO, PG), F32)],
        compiler_params=_cparams(("parallel",), VMEM_MID))(proj, wp, scale)


def pool_bwd(name, dyraw, proj, wp, scale):
    L = proj.shape[0]
    DP = scale.shape[1]
    PG = DP // N_POOL_GROUPS
    ch = _pick(L, 256)

    def body(dy_ref, x_ref, w_ref, s_ref, dx_ref, dw_ref, ds_ref, pad, dpad, dpo):
        g = pl.program_id(0)
        pad[0:POOL_HALO, :] = jnp.zeros((POOL_HALO, PG), F32)
        pad[POOL_HALO:, :] = x_ref[...]
        dpad[L:, :] = jnp.zeros((POOL_HALO, PG), F32)
        dw_ref[...] = jnp.zeros_like(dw_ref)
        ds_ref[...] = jnp.zeros_like(ds_ref)
        wv = w_ref[...]
        win_f = _pool_window(g)

        def chunk(ci, carry):
            r0 = pl.multiple_of(ci * ch, ch)
            pooled = _pooled_chunk(pad, g, r0, ch).astype(BF16)
            dyv = dy_ref[pl.ds(r0, ch), :]
            ds_ref[...] += jnp.sum(dyv * _dot(pooled, wv), axis=0, keepdims=True)
            dmixed = (dyv * s_ref[...]).astype(BF16)
            dw_ref[...] += _dot(pooled, dmixed, TN)
            dpooled = _dot(dmixed, wv, NT)
            pos = (r0 + 1 + lax.broadcasted_iota(jnp.int32, (ch, 1), 0)).astype(F32)
            dpad[pl.ds(r0, ch), :] = dpooled / jnp.minimum(pos, win_f)
            dpo[pl.ds(r0, ch), :] = dpooled
            return carry

        lax.fori_loop(0, L // ch, chunk, 0)

        def chunk2(ci, carry):
            r0 = pl.multiple_of(ci * ch, ch)
            n = ch + POOL_HALO
            dm = dpad[pl.ds(r0, n), :]
            s2 = dm + pltpu.roll(dm, n - 1, 0)
            s4 = s2 + pltpu.roll(s2, n - 2, 0)
            s8 = s4 + pltpu.roll(s4, n - 4, 0)
            s16 = s8 + pltpu.roll(s8, n - 8, 0)
            win = _pool_select(g, s2, s4, s8, s16)[:ch]
            dx_ref[pl.ds(r0, ch), :] = (win - dpo[pl.ds(r0, ch), :]).astype(BF16)
            return carry

        lax.fori_loop(0, L // ch, chunk2, 0)

    col = pl.BlockSpec((L, PG), lambda g: (0, g))
    return pl.pallas_call(
        body, name=name, grid=(N_POOL_GROUPS,),
        in_specs=[col, col, pl.BlockSpec((None, PG, PG), lambda g: (g, 0, 0)), pl.BlockSpec((1, PG), lambda g: (0, g))],
        out_specs=[col, pl.BlockSpec((None, PG, PG), lambda g: (g, 0, 0)), pl.BlockSpec((1, PG), lambda g: (0, g))],
        out_shape=[jax.ShapeDtypeStruct((L, DP), BF16), jax.ShapeDtypeStruct((N_POOL_GROUPS, PG, PG), F32),
                   jax.ShapeDtypeStruct((1, DP), F32)],
        scratch_shapes=[pltpu.VMEM((L + POOL_HALO, PG), F32), pltpu.VMEM((L + POOL_HALO, PG), F32),
                        pltpu.VMEM((L, PG), F32)],
        compiler_params=_cparams(("parallel",), VMEM_MID))(dyraw, proj, wp, scale)


def _attn_tile(L):
    return _pick(L, ATTN_TILE)


def _tri(t, strict):
    j = lax.broadcasted_iota(jnp.int32, (t, t), 0)
    s = lax.broadcasted_iota(jnp.int32, (t, t), 1)
    return ((j > s) if strict else (j >= s)).astype(BF16)


def _attn_block(q, kt, i, k0, rb, after):
    tq, tk = q.shape[0], kt.shape[0]
    row = lax.broadcasted_iota(jnp.int32, (tq, tk), 0)
    col = lax.broadcasted_iota(jnp.int32, (tq, tk), 1)
    causal = (k0 + col) < (i * tq + row)
    z = _dot(q, kt, NT)
    e = jnp.exp(-jnp.abs(z))
    l1p = jnp.log(1.0 + e)
    log_sig = jnp.minimum(z, 0.0) - l1p
    log_1m = -jnp.maximum(z, 0.0) - l1p
    b = jnp.where(causal, log_1m, 0.0)
    b_hi = b.astype(BF16)
    b_lo = (b - b_hi.astype(F32)).astype(BF16)
    suffix = _dot(b_hi, after) + _dot(b_lo, after) + rb
    w = jnp.where(causal, jnp.exp(log_sig + suffix), 0.0)
    return z, e, causal, b, w


def attn_fwd(name, proj, D):
    L = proj.shape[0]
    DA = D // 2
    H = DA // HEAD_DIM
    tq = tk = _attn_tile(L)
    qo, ko, vo = (D // 2) // HEAD_DIM, D // HEAD_DIM, (3 * D // 2) // HEAD_DIM
    scale = HEAD_DIM ** -0.5

    def body(q_ref, k_ref, v_ref, tri_ref, o_ref, kb_s, vb_s, acc, rb):
        i = pl.program_id(1)

        @pl.when(i == 0)
        def _():
            kb_s[...] = k_ref[...].astype(BF16)
            vb_s[...] = v_ref[...].astype(BF16)

        q = (q_ref[...] * scale).astype(BF16)
        acc[...] = jnp.zeros_like(acc)
        rb[...] = jnp.zeros_like(rb)

        def cond(c):
            return jnp.logical_and(c[0] >= 0, c[1])

        def step(c):
            kb = c[0]
            k0 = pl.multiple_of(kb * tk, tk)
            kt = kb_s[pl.ds(k0, tk), :]
            vt = vb_s[pl.ds(k0, tk), :]
            _, _, _, b, w = _attn_block(q, kt, i, k0, rb[...], tri_ref[...])
            acc[...] += _dot(w.astype(BF16), vt)
            rbn = rb[...] + jnp.sum(b, axis=1, keepdims=True)
            rb[...] = rbn
            return kb - 1, jnp.max(rbn) > -ATTN_DECAY_CUTOFF

        lax.while_loop(cond, step, (i, jnp.bool_(True)))
        o_ref[...] = acc[...]

    return pl.pallas_call(
        body, name=name, grid=(H, L // tq),
        in_specs=[pl.BlockSpec((tq, HEAD_DIM), lambda h, i: (i, qo + h)),
                  pl.BlockSpec((L, HEAD_DIM), lambda h, i: (0, ko + h)),
                  pl.BlockSpec((L, HEAD_DIM), lambda h, i: (0, vo + h)),
                  pl.BlockSpec((tk, tk), lambda h, i: (0, 0))],
        out_specs=pl.BlockSpec((tq, HEAD_DIM), lambda h, i: (i, h)),
        out_shape=jax.ShapeDtypeStruct((L, DA), F32),
        scratch_shapes=[pltpu.VMEM((L, HEAD_DIM), BF16), pltpu.VMEM((L, HEAD_DIM), BF16),
                        pltpu.VMEM((tq, HEAD_DIM), F32), pltpu.VMEM((tq, 1), F32)],
        compiler_params=_cparams(("arbitrary", "arbitrary"), VMEM_MID))(proj, proj, proj, _tri(tk, True))


def attn_bwd(name, proj, o, do, D):
    L = proj.shape[0]
    DA = D // 2
    H = DA // HEAD_DIM
    tq = tk = _attn_tile(L)
    qo, ko, vo = (D // 2) // HEAD_DIM, D // HEAD_DIM, (3 * D // 2) // HEAD_DIM
    scale = HEAD_DIM ** -0.5

    def body(q_ref, k_ref, v_ref, o_ref, do_ref, after_ref, from_ref, dq_ref, dk_ref, dv_ref,
             kb_s, vb_s, dk_s, dv_s, dq_acc, rb, rg):
        i = pl.program_id(1)
        nq = pl.num_programs(1)

        @pl.when(i == 0)
        def _():
            kb_s[...] = k_ref[...].astype(BF16)
            vb_s[...] = v_ref[...].astype(BF16)
            dk_s[...] = jnp.zeros_like(dk_s)
            dv_s[...] = jnp.zeros_like(dv_s)

        q = (q_ref[...] * scale).astype(BF16)
        dob = do_ref[...].astype(BF16)
        delta = jnp.sum(dob.astype(F32) * o_ref[...], axis=1, keepdims=True)
        dq_acc[...] = jnp.zeros_like(dq_acc)
        rb[...] = jnp.zeros_like(rb)
        rg[...] = jnp.zeros_like(rg)

        def cond(c):
            return jnp.logical_and(c[0] >= 0, c[1])

        def step(c):
            kb = c[0]
            k0 = pl.multiple_of(kb * tk, tk)
            kt = kb_s[pl.ds(k0, tk), :]
            vt = vb_s[pl.ds(k0, tk), :]
            z, e, causal, b, w = _attn_block(q, kt, i, k0, rb[...], after_ref[...])
            wq = w.astype(BF16)
            dw = _dot(dob, vt, NT)
            g = wq.astype(F32) * dw
            g_hi = g.astype(BF16)
            g_lo = (g - g_hi.astype(F32)).astype(BF16)
            from_s = from_ref[...]
            suffix_g = _dot(g_hi, from_s) + _dot(g_lo, from_s) + rg[...]
            before = delta - suffix_g
            r = 1.0 / (1.0 + e)
            sig = jnp.where(z >= 0, r, e * r)
            sig_neg = jnp.where(z >= 0, e * r, r)
            dz = jnp.where(causal, g * sig_neg - before * sig, 0.0).astype(BF16)
            dq_acc[...] += _dot(dz, kt)
            dk_s[pl.ds(k0, tk), :] += _dot(dz, q, TN)
            dv_s[pl.ds(k0, tk), :] += _dot(wq, dob, TN)
            rbn = rb[...] + jnp.sum(b, axis=1, keepdims=True)
            rb[...] = rbn
            rg[...] += jnp.sum(g, axis=1, keepdims=True)
            return kb - 1, jnp.max(rbn) > -ATTN_DECAY_CUTOFF

        lax.while_loop(cond, step, (i, jnp.bool_(True)))
        dq_ref[...] = (dq_acc[...] * scale).astype(BF16)

        @pl.when(i == nq - 1)
        def _():
            dk_ref[...] = dk_s[...].astype(BF16)
            dv_ref[...] = dv_s[...].astype(BF16)

    blk = pl.BlockSpec((tq, HEAD_DIM), lambda h, i: (i, h))
    full = pl.BlockSpec((L, HEAD_DIM), lambda h, i: (0, h))
    return pl.pallas_call(
        body, name=name, grid=(H, L // tq),
        in_specs=[pl.BlockSpec((tq, HEAD_DIM), lambda h, i: (i, qo + h)),
                  pl.BlockSpec((L, HEAD_DIM), lambda h, i: (0, ko + h)),
                  pl.BlockSpec((L, HEAD_DIM), lambda h, i: (0, vo + h)), blk, blk,
                  pl.BlockSpec((tk, tk), lambda h, i: (0, 0)), pl.BlockSpec((tk, tk), lambda h, i: (0, 0))],
        out_specs=[blk, full, full],
        out_shape=[jax.ShapeDtypeStruct((L, DA), BF16)] * 3,
        scratch_shapes=[pltpu.VMEM((L, HEAD_DIM), BF16), pltpu.VMEM((L, HEAD_DIM), BF16),
                        pltpu.VMEM((L, HEAD_DIM), F32), pltpu.VMEM((L, HEAD_DIM), F32),
                        pltpu.VMEM((tq, HEAD_DIM), F32), pltpu.VMEM((tq, 1), F32), pltpu.VMEM((tq, 1), F32)],
        compiler_params=_cparams(("arbitrary", "arbitrary"), VMEM_MID),
    )(proj, proj, proj, o, do, _tri(tk, True), _tri(tk, False))


def _cmul(ar, ai, br, bi):
    return ar * br - ai * bi, ar * bi + ai * br


def _cmul_conj(ar, ai, br, bi):
    return ar * br + ai * bi, ar * bi - ai * br


def _ssm_disc(lr, li, ld):
    dt = jnp.exp(ld)
    m = jnp.exp(lr * dt)
    ar, ai = m * jnp.cos(li * dt), m * jnp.sin(li * dt)
    inv = 1.0 / (lr * lr + li * li)
    fr, fi = _cmul(ar - 1.0, ai, lr * inv, -li * inv)
    return dt, ar, ai, fr, fi, inv


def ssm_prep(name, lr, li, ld, br, bi):
    def body(lr_ref, li_ref, ld_ref, br_ref, bi_ref, ar_ref, ai_ref, bbr_ref, bbi_ref):
        _, ar, ai, fr, fi, _ = _ssm_disc(lr_ref[...], li_ref[...], ld_ref[...])
        ar_ref[...] = ar
        ai_ref[...] = ai
        bbr, bbi = _cmul(fr, fi, br_ref[...], bi_ref[...])
        bbr_ref[...] = bbr
        bbi_ref[...] = bbi

    sd = jax.ShapeDtypeStruct
    return pl.pallas_call(
        body, name=name,
        out_shape=[sd(lr.shape, F32), sd(lr.shape, F32), sd(br.shape, F32), sd(br.shape, F32)],
    )(lr, li, ld, br, bi)


def ssm_prep_bwd(name, lr, li, ld, br, bi, gar, gai, gbr, gbi):
    def body(lr_ref, li_ref, ld_ref, br_ref, bi_ref, gar_ref, gai_ref, gbr_ref, gbi_ref,
             dlr_ref, dli_ref, dld_ref, dbr_ref, dbi_ref):
        lr_, li_ = lr_ref[...], li_ref[...]
        dt, ar, ai, fr, fi, inv = _ssm_disc(lr_, li_, ld_ref[...])
        gbr_, gbi_ = gbr_ref[...], gbi_ref[...]
        dbr, dbi = _cmul_conj(fr, fi, gbr_, gbi_)
        dbr_ref[...] = dbr
        dbi_ref[...] = dbi
        pr, pi = _cmul_conj(br_ref[...], bi_ref[...], gbr_, gbi_)
        gfr = jnp.sum(pr, axis=1, keepdims=True)
        gfi = jnp.sum(pi, axis=1, keepdims=True)
        ilr, ili = lr_ * inv, -li_ * inv
        tr_, ti_ = _cmul_conj(ilr, ili, gfr, gfi)
        gatr, gati = gar_ref[...] + tr_, gai_ref[...] + ti_
        hr, hi = _cmul(fr, fi, ilr, ili)
        t1r, t1i = _cmul_conj(ar * dt, ai * dt, gatr, gati)
        t2r, t2i = _cmul_conj(hr, hi, gfr, gfi)
        dlr_ref[...] = t1r - t2r
        dli_ref[...] = t1i - t2i
        lar, lai = _cmul(lr_, li_, ar, ai)
        gdt, _ = _cmul_conj(lar, lai, gatr, gati)
        dld_ref[...] = jnp.sum(gdt, axis=2, keepdims=True) * dt

    sd = jax.ShapeDtypeStruct
    return pl.pallas_call(
        body, name=name,
        out_shape=[sd(lr.shape, F32), sd(lr.shape, F32), sd(ld.shape, F32), sd(br.shape, F32), sd(br.shape, F32)],
    )(lr, li, ld, br, bi, gar, gai, gbr, gbi)


SCAN_ROWS = 64


def _scan(xr, xi, ar, ai, L, reverse):
    R = min(SCAN_ROWS, L)
    nt = L // R
    npass = int(round(math.log2(L)))
    assert (1 << npass) == L and L % R == 0
    ns = CHUNK_X // LANE

    def update(rows_cur, load_shift, pr, pi):
        for c in range(ns):
            cs = pl.ds(c * LANE, LANE)
            sr, si = load_shift(xr, cs), load_shift(xi, cs)
            cr, ci = pr[:, c * LANE:(c + 1) * LANE], pi[:, c * LANE:(c + 1) * LANE]
            xr[rows_cur, cs] = xr[rows_cur, cs] + cr * sr - ci * si
            xi[rows_cur, cs] = xi[rows_cur, cs] + cr * si + ci * sr

    pr, pi = ar, ai
    for p in range(npass):
        d = 1 << p
        if d >= R:
            skip = d // R

            def tile(n, carry, d=d, pr=pr, pi=pi, skip=skip):
                t = (n if reverse else nt - 1 - n)
                r0 = pl.multiple_of(t * R, R)
                src = pl.multiple_of(r0 + d if reverse else r0 - d, R)
                update(pl.ds(r0, R), lambda ref, cs: ref[pl.ds(src, R), cs], pr, pi)
                return carry

            lax.fori_loop(0, nt - skip, tile, 0)
        elif d >= 8:
            def tile(n, carry, d=d, pr=pr, pi=pi):
                t = (n if reverse else nt - 1 - n)
                r0 = pl.multiple_of(t * R, 8)
                src = pl.multiple_of(r0 + d if reverse else r0 - d, 8)
                update(pl.ds(r0, R), lambda ref, cs: ref[pl.ds(src, R), cs], pr, pi)
                return carry

            if nt > 1:
                lax.fori_loop(0, nt - 1, tile, 0)
            if reverse:
                update(pl.ds(L - R, R - d), lambda ref, cs: ref[pl.ds(L - R + d, R - d), cs], pr, pi)
            else:
                update(pl.ds(d, R - d), lambda ref, cs: ref[pl.ds(0, R - d), cs], pr, pi)
        else:
            def tile(n, carry, d=d, pr=pr, pi=pi):
                t = (n if reverse else nt - 1 - n)
                r0 = pl.multiple_of(t * R, 8)
                if reverse:
                    load = lambda ref, cs: pltpu.roll(ref[pl.ds(r0, R + 8), cs], R + 8 - d, 0)[:R]
                else:
                    load = lambda ref, cs: pltpu.roll(ref[pl.ds(pl.multiple_of(r0 - 8, 8), R + 8), cs], d, 0)[8:]
                update(pl.ds(r0, R), load, pr, pi)
                return carry

            if nt > 1:
                lax.fori_loop(0, nt - 1, tile, 0)
            ridx = lax.broadcasted_iota(jnp.int32, (R, LANE), 0)
            if reverse:
                load = lambda ref, cs: jnp.where(
                    ridx < R - d, pltpu.roll(ref[pl.ds(L - R, R), cs], R - d, 0), 0.0)
                update(pl.ds(L - R, R), load, pr, pi)
            else:
                load = lambda ref, cs: jnp.where(ridx >= d, pltpu.roll(ref[pl.ds(0, R), cs], d, 0), 0.0)
                update(pl.ds(0, R), load, pr, pi)
        pr, pi = _cmul(pr, pi, pr, pi)


def _gelu(x):
    t = jnp.tanh(0.7978845608028654 * (x + 0.044715 * x * x * x))
    return 0.5 * x * (1.0 + t)


def _gelu_grad(x):
    t = jnp.tanh(0.7978845608028654 * (x + 0.044715 * x * x * x))
    return 0.5 * (1.0 + t) + 0.5 * x * (1.0 - t * t) * 0.7978845608028654 * (1.0 + 0.134145 * x * x)


def _call_1d(body, *, name, grid, in_specs, out_specs, out_shape, scratch_shapes, vmem, args, copies=None):
    n_i, n_o, n_s = len(in_specs), len(out_specs), len(scratch_shapes)
    if copies is None:
        out = pl.pallas_call(
            body, name=name, grid=grid, in_specs=in_specs, out_specs=out_specs, out_shape=out_shape,
            scratch_shapes=scratch_shapes, compiler_params=_cparams(("parallel",), vmem))(*args)
        return list(out), []
    n_ci, n_co = len(copies.inputs), len(copies.out_shape)

    def hosted(*refs):
        ins, cin = refs[:n_i], refs[n_i:n_i + n_ci]
        outs = refs[n_i + n_ci:n_i + n_ci + n_o]
        cout = refs[n_i + n_ci + n_o:n_i + n_ci + n_o + n_co]
        scr = refs[n_i + n_ci + n_o + n_co:n_i + n_ci + n_o + n_co + n_s]
        sems = refs[n_i + n_ci + n_o + n_co + n_s:]
        copies.emit(cin, cout, sems, pl.program_id(0), grid[0])
        body(*ins, *outs, *scr)

    out = pl.pallas_call(
        hosted, name=name, grid=grid, in_specs=list(in_specs) + [HBM] * n_ci,
        out_specs=list(out_specs) + [HBM] * n_co, out_shape=list(out_shape) + list(copies.out_shape),
        scratch_shapes=list(scratch_shapes) + list(copies.scratch),
        compiler_params=pltpu.CompilerParams(dimension_semantics=("arbitrary",), vmem_limit_bytes=vmem,
                                             has_side_effects=True))(*args, *copies.inputs)
    return list(out[:n_o]), list(out[n_o:])


def ssm_fwd(name, proj, wbr, wbi, ar, ai, wcr, wci, dskip, D, copies=None):
    L = proj.shape[0]
    DS = D // 4
    NC = DS // CHUNK_U
    uo = (5 * D // 2) // CHUNK_U
    ch = _pick(L, 256)

    def body(u_ref, wbr_ref, wbi_ref, ar_ref, ai_ref, wcr_ref, wci_ref, ds_ref,
             y_ref, hg_ref, xr_ref, xi_ref, sr, si):
        def fill(ci, carry):
            rows = pl.ds(pl.multiple_of(ci * ch, ch), ch)
            ub = u_ref[rows, :].astype(BF16)
            sr[rows, :] = _dot(ub, wbr_ref[...])
            si[rows, :] = _dot(ub, wbi_ref[...])
            return carry

        lax.fori_loop(0, L // ch, fill, 0)
        _scan(sr, si, ar_ref[...], ai_ref[...], L, reverse=False)

        def emit(ci, carry):
            rows = pl.ds(pl.multiple_of(ci * ch, ch), ch)
            xrb, xib = sr[rows, :].astype(BF16), si[rows, :].astype(BF16)
            xr_ref[rows, :] = xrb
            xi_ref[rows, :] = xib
            y = _dot(xrb, wcr_ref[...]) - _dot(xib, wci_ref[...]) + ds_ref[...] * u_ref[rows, :]
            y_ref[rows, :] = y
            hg_ref[rows, :] = _gelu(y).astype(BF16)
            return carry

        lax.fori_loop(0, L // ch, emit, 0)

    ucol = pl.BlockSpec((L, CHUNK_U), lambda k: (0, k))
    xcol = pl.BlockSpec((L, CHUNK_X), lambda k: (0, k))
    sd = jax.ShapeDtypeStruct
    return _call_1d(
        body, name=name, grid=(NC,),
        in_specs=[pl.BlockSpec((L, CHUNK_U), lambda k: (0, uo + k)),
                  pl.BlockSpec((None, CHUNK_U, CHUNK_X), lambda k: (k, 0, 0)),
                  pl.BlockSpec((None, CHUNK_U, CHUNK_X), lambda k: (k, 0, 0)),
                  pl.BlockSpec((1, CHUNK_X), lambda k: (0, k)), pl.BlockSpec((1, CHUNK_X), lambda k: (0, k)),
                  pl.BlockSpec((None, CHUNK_X, CHUNK_U), lambda k: (k, 0, 0)),
                  pl.BlockSpec((None, CHUNK_X, CHUNK_U), lambda k: (k, 0, 0)),
                  pl.BlockSpec((1, CHUNK_U), lambda k: (0, k))],
        out_specs=[ucol, ucol, xcol, xcol],
        out_shape=[sd((L, DS), F32), sd((L, DS), BF16), sd((L, 4 * DS), BF16), sd((L, 4 * DS), BF16)],
        scratch_shapes=[pltpu.VMEM((L, CHUNK_X), F32), pltpu.VMEM((L, CHUNK_X), F32)],
        vmem=VMEM_BIG, args=(proj, wbr, wbi, ar, ai, wcr, wci, dskip), copies=copies)


def ssm_bwd(name, dhg, ypre, proj, xr, xi, wbr, wbi, ar, ai, wcr, wci, dskip, D, copies=None):
    L = proj.shape[0]
    DS = D // 4
    NC = DS // CHUNK_U
    uo = (5 * D // 2) // CHUNK_U
    ch = _pick(L, 256)
    nch = L // ch
    halo = 16

    def body(dhg_ref, y_ref, u_ref, xr_ref, xi_ref, wbr_ref, wbi_ref, ar_ref, ai_ref, wcr_ref, wci_ref,
             ds_ref, du_ref, dwcr_ref, dwci_ref, dwbr_ref, dwbi_ref, dar_ref, dai_ref, dds_ref, gr, gi, duf):
        dwcr_ref[...] = jnp.zeros_like(dwcr_ref)
        dwci_ref[...] = jnp.zeros_like(dwci_ref)
        dwbr_ref[...] = jnp.zeros_like(dwbr_ref)
        dwbi_ref[...] = jnp.zeros_like(dwbi_ref)
        dar_ref[...] = jnp.zeros_like(dar_ref)
        dai_ref[...] = jnp.zeros_like(dai_ref)
        dds_ref[...] = jnp.zeros_like(dds_ref)

        def first(ci, carry):
            rows = pl.ds(pl.multiple_of(ci * ch, ch), ch)
            dy = dhg_ref[rows, :] * _gelu_grad(y_ref[rows, :])
            dyb = dy.astype(BF16)
            dds_ref[...] += jnp.sum(dy * u_ref[rows, :], axis=0, keepdims=True)
            duf[rows, :] = ds_ref[...] * dy
            gr[rows, :] = _dot(dyb, wcr_ref[...], NT)
            gi[rows, :] = -_dot(dyb, wci_ref[...], NT)
            dwcr_ref[...] += _dot(xr_ref[rows, :], dyb, TN)
            dwci_ref[...] -= _dot(xi_ref[rows, :], dyb, TN)
            return carry

        lax.fori_loop(0, nch, first, 0)
        _scan(gr, gi, ar_ref[...], -ai_ref[...], L, reverse=True)

        def lam_grad(gxr, gxi, xpr, xpi):
            pr, pi = _cmul_conj(xpr, xpi, gxr, gxi)
            dar_ref[...] += jnp.sum(pr, axis=0, keepdims=True)
            dai_ref[...] += jnp.sum(pi, axis=0, keepdims=True)

        def second(ci, carry):
            r0 = pl.multiple_of(ci * ch, ch)
            rows = pl.ds(r0, ch)
            gxr, gxi = gr[rows, :], gi[rows, :]
            gxrb, gxib = gxr.astype(BF16), gxi.astype(BF16)
            du_ref[rows, :] = (duf[rows, :] + _dot(gxrb, wbr_ref[...], NT) + _dot(gxib, wbi_ref[...], NT)).astype(BF16)
            ub = u_ref[rows, :].astype(BF16)
            dwbr_ref[...] += _dot(ub, gxrb, TN)
            dwbi_ref[...] += _dot(ub, gxib, TN)
            return carry

        lax.fori_loop(0, nch, second, 0)

        ridx = lax.broadcasted_iota(jnp.int32, (ch, CHUNK_X), 0)
        xpr = jnp.where(ridx >= 1, pltpu.roll(xr_ref[0:ch, :].astype(F32), 1, 0), 0.0)
        xpi = jnp.where(ridx >= 1, pltpu.roll(xi_ref[0:ch, :].astype(F32), 1, 0), 0.0)
        lam_grad(gr[0:ch, :], gi[0:ch, :], xpr, xpi)

        def third(ci, carry):
            r0 = pl.multiple_of(ci * ch, ch)
            ext = pl.ds(pl.multiple_of(r0 - halo, halo), ch + halo)
            xpr = pltpu.roll(xr_ref[ext, :].astype(F32), 1, 0)[halo:]
            xpi = pltpu.roll(xi_ref[ext, :].astype(F32), 1, 0)[halo:]
            lam_grad(gr[pl.ds(r0, ch), :], gi[pl.ds(r0, ch), :], xpr, xpi)
            return carry

        if nch > 1:
            lax.fori_loop(1, nch, third, 0)

    ucol = pl.BlockSpec((L, CHUNK_U), lambda k: (0, k))
    xcol = pl.BlockSpec((L, CHUNK_X), lambda k: (0, k))
    wb_spec = pl.BlockSpec((None, CHUNK_U, CHUNK_X), lambda k: (k, 0, 0))
    wc_spec = pl.BlockSpec((None, CHUNK_X, CHUNK_U), lambda k: (k, 0, 0))
    avec = pl.BlockSpec((1, CHUNK_X), lambda k: (0, k))
    uvec = pl.BlockSpec((1, CHUNK_U), lambda k: (0, k))
    sd = jax.ShapeDtypeStruct
    return _call_1d(
        body, name=name, grid=(NC,),
        in_specs=[ucol, ucol, pl.BlockSpec((L, CHUNK_U), lambda k: (0, uo + k)), xcol, xcol,
                  wb_spec, wb_spec, avec, avec, wc_spec, wc_spec, uvec],
        out_specs=[ucol, wc_spec, wc_spec, wb_spec, wb_spec, avec, avec, uvec],
        out_shape=[sd((L, DS), BF16), sd((NC, CHUNK_X, CHUNK_U), F32), sd((NC, CHUNK_X, CHUNK_U), F32),
                   sd((NC, CHUNK_U, CHUNK_X), F32), sd((NC, CHUNK_U, CHUNK_X), F32),
                   sd((1, 4 * DS), F32), sd((1, 4 * DS), F32), sd((1, DS), F32)],
        scratch_shapes=[pltpu.VMEM((L, CHUNK_X), F32), pltpu.VMEM((L, CHUNK_X), F32), pltpu.VMEM((L, CHUNK_U), F32)],
        vmem=VMEM_BIG, args=(dhg, ypre, proj, xr, xi, wbr, wbi, ar, ai, wcr, wci, dskip), copies=copies)


def _block_diag(w, transpose):
    G = w.shape[0]
    nc = G // GROUPS_PER_CHUNK
    w4 = w.reshape(nc, GROUPS_PER_CHUNK, SSM_GROUP, SSM_STATE)
    eye = jnp.eye(GROUPS_PER_CHUNK, dtype=w.dtype)
    if transpose:
        return (w4[:, None, :, :, :].transpose(0, 1, 4, 2, 3) * eye[None, :, None, :, None]).reshape(
            nc, CHUNK_X, CHUNK_U).astype(BF16)
    return (w4[:, :, :, None, :] * eye[None, :, None, :, None]).reshape(nc, CHUNK_U, CHUNK_X).astype(BF16)


def _diag_blocks(dw, transpose):
    nc = dw.shape[0]
    gpc = GROUPS_PER_CHUNK
    if transpose:
        d5 = dw.reshape(nc, gpc, SSM_STATE, gpc, SSM_GROUP)
        blocks = [d5[:, g, :, g, :] for g in range(gpc)]
        return jnp.stack(blocks, axis=1).transpose(0, 1, 3, 2).reshape(nc * gpc, SSM_GROUP, SSM_STATE)
    d5 = dw.reshape(nc, gpc, SSM_GROUP, gpc, SSM_STATE)
    blocks = [d5[:, g, :, g, :] for g in range(gpc)]
    return jnp.stack(blocks, axis=1).reshape(nc * gpc, SSM_GROUP, SSM_STATE)


SHARD_BLOCK_ELEMS = 128 * 1024


def _shard_rows(R, C, scale):
    return _pick(R, max(8, scale * SHARD_BLOCK_ELEMS // C))


def cast_bf16(name, w, layer):
    shape = w.shape[1:]
    w3 = w.reshape(w.shape[0], -1, shape[-1])
    _, R, C = w3.shape
    tr = _shard_rows(R, C, 4)

    def body(w_ref, o_ref):
        o_ref[...] = w_ref[...].astype(BF16)

    out = pl.pallas_call(body, name=name, grid=(R // tr,),
                         in_specs=[pl.BlockSpec((None, tr, C), lambda i: (layer, i, 0))],
                         out_specs=pl.BlockSpec((tr, C), lambda i: (i, 0)),
                         out_shape=jax.ShapeDtypeStruct((R, C), BF16),
                         compiler_params=_cparams(("parallel",), VMEM_MID))(w3)
    return out.reshape(shape)


def _adamw(w, g, m, v):
    m = ADAM_B1 * m + (1.0 - ADAM_B1) * g
    v = ADAM_B2 * v + (1.0 - ADAM_B2) * (g * g)
    delta = -ADAM_LR * ((m * ADAM_C1) / (jnp.sqrt(v * ADAM_C2) + ADAM_EPS) + ADAM_WD * w)
    return delta, m, v


def _own_core(g4):
    return (lambda p: p[0]) if g4.shape[1] == 2 else (lambda p: 0)


def chip_partial(name, pos, g4, recv_a):
    _, _, R, C = g4.shape
    tr = _shard_rows(R, C, 4)
    core = _own_core(g4)

    def body(pos_ref, g_ref, a_ref, o_ref):
        o_ref[...] = (g_ref[...] + a_ref[...]).astype(BF16)

    return pl.pallas_call(
        body, name=name,
        grid_spec=pltpu.PrefetchScalarGridSpec(
            num_scalar_prefetch=1, grid=(4, R // tr),
            in_specs=[pl.BlockSpec((None, None, tr, C), lambda q, i, p: (q, core(p), i, 0)),
                      pl.BlockSpec((None, tr, C), lambda q, i, p: (q, i, 0))],
            out_specs=pl.BlockSpec((None, tr, C), lambda q, i, p: (q, i, 0))),
        out_shape=jax.ShapeDtypeStruct((4, R, C), BF16),
        compiler_params=_cparams(("parallel", "parallel"), VMEM_MID))(pos, g4, recv_a)


def adamw_shard(name, pos, layer, g4, recv_a, recv_b, w, m, v, prev):
    _, _, R, C = g4.shape
    tr = _shard_rows(R, C, 1)
    n_prev = 0 if prev is None else 4
    core = _own_core(g4)

    def body(pos_ref, g_ref, a_ref, b_ref, w_ref, m_ref, v_ref, *rest):
        go_ref, d_ref, mo_ref, vo_ref = rest[n_prev:]
        gs = g_ref[...] + a_ref[...]
        for j in range(3):
            gs = gs + b_ref[j].astype(F32)
        delta, mn, vn = _adamw(w_ref[...], gs, m_ref[...], v_ref[...])
        go_ref[...] = gs
        d_ref[...] = delta
        mo_ref[...] = mn
        vo_ref[...] = vn

    lay = pl.BlockSpec((None, tr, C), lambda i, p: (layer, i, 0))
    in_specs = [pl.BlockSpec((None, None, tr, C), lambda i, p: (p[1], core(p), i, 0)),
                pl.BlockSpec((None, tr, C), lambda i, p: (p[1], i, 0)),
                pl.BlockSpec((3, tr, C), lambda i, p: (0, i, 0)), lay, lay, lay]
    args = [g4, recv_a, recv_b, w, m, v]
    aliases = {}
    if prev is not None:
        in_specs += [pl.BlockSpec(memory_space=pl.ANY)] * 4
        args += list(prev)
        aliases = {7 + j: j for j in range(4)}
    return pl.pallas_call(
        body, name=name,
        grid_spec=pltpu.PrefetchScalarGridSpec(
            num_scalar_prefetch=1, grid=(R // tr,), in_specs=in_specs, out_specs=[lay] * 4),
        out_shape=[jax.ShapeDtypeStruct(w.shape, F32)] * 4,
        input_output_aliases=aliases,
        compiler_params=_cparams(("parallel",), VMEM_MID))(pos, *args)


def adamw_small(name, gathered, w, m, v):
    _, R, C = gathered.shape
    tr = _pick(R, 512)

    def body(g_ref, w_ref, m_ref, v_ref, go_ref, d_ref, mo_ref, vo_ref):
        gs = g_ref[0]
        for j in range(1, N_DEV):
            gs = gs + g_ref[j]
        delta, mn, vn = _adamw(w_ref[...], gs, m_ref[...], v_ref[...])
        go_ref[...] = gs
        d_ref[...] = delta
        mo_ref[...] = mn
        vo_ref[...] = vn

    spec = pl.BlockSpec((tr, C), lambda i: (i, 0))
    return pl.pallas_call(
        body, name=name, grid=(R // tr,),
        in_specs=[pl.BlockSpec((N_DEV, tr, C), lambda i: (0, i, 0)), spec, spec, spec], out_specs=[spec] * 4,
        out_shape=[jax.ShapeDtypeStruct((R, C), F32)] * 4,
        compiler_params=_cparams(("parallel",), VMEM_MID))(gathered, w, m, v)


def _position():
    return lax.axis_index("x"), lax.axis_index("y"), lax.axis_index("c")


FORWARD_AT = 0.7


def gather_copies(shards):
    n = len(shards)

    def parts(ins, outs, sems):
        send_sems, recv_sems, local_sems = sems
        x, y, c = _position()
        me, sibling = (x, y, c), (x, y, 1 - c)
        chips = [(1 - x, y), (x, 1 - y), (1 - x, 1 - y)]

        def copy(a, k, block, to, src=None):
            blk = outs[a].at[4 * block[0] + 2 * block[1] + block[2]]
            return pltpu.make_async_remote_copy(
                src_ref=blk if src is None else src, dst_ref=blk,
                send_sem=send_sems.at[a, k], recv_sem=recv_sems.at[a, k], device_id=to, device_id_type=MESH)

        mine = [pltpu.make_async_copy(ins[a], outs[a].at[4 * x + 2 * y + c], local_sems.at[a]) for a in range(n)]
        first = [[copy(a, 0, me, sibling, src=ins[a])] +
                 [copy(a, 1 + j, me, (*chip, c), src=ins[a]) for j, chip in enumerate(chips)] for a in range(n)]
        landed = [[copy(a, 1 + j, (*chip, c), me) for j, chip in enumerate(chips)] for a in range(n)]
        passed = [[copy(a, 4 + j, (*chip, c), sibling) for j, chip in enumerate(chips)] for a in range(n)]
        from_sibling = [[copy(a, 0, sibling, me)] +
                        [copy(a, 4 + j, (*chip, 1 - c), me) for j, chip in enumerate(chips)] for a in range(n)]
        return mine, first, landed, passed, from_sibling

    def start(ins, outs, sems):
        mine, first, _, _, _ = parts(ins, outs, sems)
        for a in range(n):
            mine[a].start()
            for cp in first[a]:
                cp.start()

    def forward(ins, outs, sems):
        _, _, landed, passed, _ = parts(ins, outs, sems)
        for a in range(n):
            for j in range(3):
                landed[a][j].wait_recv()
                passed[a][j].start()

    def finish(ins, outs, sems):
        mine, first, _, passed, from_sibling = parts(ins, outs, sems)
        for a in range(n):
            for cp in from_sibling[a]:
                cp.wait_recv()
        for a in range(n):
            for cp in first[a] + passed[a]:
                cp.wait_send()
            mine[a].wait()

    return HostedCopies(
        list(shards), [jax.ShapeDtypeStruct((N_DEV,) + s.shape, s.dtype) for s in shards],
        [pltpu.SemaphoreType.DMA((n, 7)), pltpu.SemaphoreType.DMA((n, 7)), pltpu.SemaphoreType.DMA((n,))],
        [(0.0, start), (FORWARD_AT, forward), (1.0, finish)])


def _exchange_copies(arrays, out_lead, make):
    n = len(arrays)

    def all_copies(ins, outs, sems):
        send_sems, recv_sems = sems
        return [make(ins[a], outs[a], send_sems.at[a, k], recv_sems.at[a, k], k)
                for a in range(n) for k in range(out_lead)]

    def start(ins, outs, sems):
        for cp in all_copies(ins, outs, sems):
            cp.start()

    def finish(ins, outs, sems):
        for cp in all_copies(ins, outs, sems):
            cp.wait()

    return HostedCopies(
        list(arrays), [jax.ShapeDtypeStruct((out_lead,) + a.shape[2:], a.dtype) for a in arrays],
        [pltpu.SemaphoreType.DMA((n, out_lead)), pltpu.SemaphoreType.DMA((n, out_lead))],
        [(0.0, start), (1.0, finish)])


def sibling_copies(grads):
    def make(src, dst, send_sem, recv_sem, q):
        x, y, c = _position()
        core = 1 - c if src.shape[1] == 2 else 0
        return pltpu.make_async_remote_copy(
            src_ref=src.at[q, core], dst_ref=dst.at[q], send_sem=send_sem, recv_sem=recv_sem,
            device_id=(x, y, 1 - c), device_id_type=MESH)

    return _exchange_copies(grads, 4, make)


def chip_copies(parts):
    def make(src, dst, send_sem, recv_sem, j):
        x, y, c = _position()
        chip = [(1 - x, y), (x, 1 - y), (1 - x, 1 - y)][j]
        return pltpu.make_async_remote_copy(
            src_ref=src.at[2 * chip[0] + chip[1], 0], dst_ref=dst.at[j], send_sem=send_sem, recv_sem=recv_sem,
            device_id=(*chip, c), device_id_type=MESH)

    return _exchange_copies(parts, 3, make)


class Carrier:
    def __init__(self):
        self.plan = {}

    def ride(self, site, make, store):
        self.plan[site] = (make, store)

    def make(self, site, ctx=None):
        return self.plan[site][0](ctx) if site in self.plan else None

    def store(self, site, results):
        if site in self.plan:
            self.plan[site][1](results)

    def split(self, site, out):
        if site not in self.plan:
            return out
        self.plan[site][1](out[1])
        return out[0]


def _pool_weight(gathered):
    PG = gathered.shape[-1]
    return gathered.transpose(1, 0, 2, 3).reshape(N_POOL_GROUPS, PG, PG)


def _layer_params(l, ln_g, pool_scale, lam_re, lam_im, log_dt, b_re, b_im, c_re, c_im,
                  d_skip, b_glu, branch_g):
    G, P = lam_re.shape[1:]
    p = dict(
        ln_g=ln_g[l][None, :], pool_scale=pool_scale[l][None, :], d_skip=d_skip[l][None, :],
        b_glu=b_glu[l][None, :], branch_g=branch_g[l][None, :],
        lr=lam_re[l].reshape(G, 1, P), li=lam_im[l].reshape(G, 1, P), ld=log_dt[l].reshape(G, 1, 1),
        br=b_re[l].transpose(0, 2, 1), bi=b_im[l].transpose(0, 2, 1), cr=c_re[l], ci=c_im[l])
    return p


def layer_fwd(l, x, p, gw, D, carrier):
    t = f"l{l}_"
    h = rms_fwd(t + "rms_fwd", x, p["ln_g"])
    site = (l, "proj")
    proj = carrier.split(site, mm_nn_gathered(t + "proj", h, gw["w_in", l], copies=carrier.make(site)))
    wp = _pool_weight(gw["w_pool", l])
    ypool = pool_fwd(t + "pool_fwd", proj, wp, p["pool_scale"])
    yattn = attn_fwd(t + "attn_fwd", proj, D)
    ar, ai, bbr, bbi = ssm_prep(t + "ssm_prep", p["lr"], p["li"], p["ld"], p["br"], p["bi"])
    ssm_w = dict(wbr=_block_diag(bbr, False), wbi=_block_diag(bbi, False),
                 ar=ar.reshape(1, -1), ai=ai.reshape(1, -1),
                 wcr=_block_diag(p["cr"], True), wci=_block_diag(p["ci"], True))
    site = (l, "ssm_fwd")
    (ypre, hg, xr, xi), landed = ssm_fwd(
        t + "ssm_fwd", proj, ssm_w["wbr"], ssm_w["wbi"], ssm_w["ar"], ssm_w["ai"],
        ssm_w["wcr"], ssm_w["wci"], p["d_skip"], D, copies=carrier.make(site))
    carrier.store(site, landed)
    glu_pre = mm_nn_gathered(t + "glu", hg, gw["w_glu", l])
    y = branch_fwd(t + "branch_fwd", ypool, yattn, glu_pre, proj, p["b_glu"], p["branch_g"])
    out = mm_plain(t + "out", y, gw["w_out", l].reshape(D, D), NN, res=x)
    saved = dict(x=x, h=h, proj=proj, ypool=ypool, yattn=yattn, ypre=ypre, hg=hg, xr=xr, xi=xi,
                 glu_pre=glu_pre, y=y, ssm_w=ssm_w, wp=wp)
    return out, saved


def layer_bwd(l, dres, s, p, gw, D, carrier, pos, split_w_in):
    t = f"l{l}_"
    proj = s["proj"]
    w_out_g = gw["w_out", l].reshape(D, D)
    site = (l, "dy")
    dy = carrier.split(site, mm_plain(t + "dy", dres, w_out_g, NT, copies=carrier.make(site)))
    dw_out = mm_plain(t + "dw_out", s["y"], dres, TN)
    dypool, dyattn, dglu, dpg, dag, dsg, dbg, dbglu = branch_bwd(
        t + "branch_bwd", dy, s["ypool"], s["yattn"], s["glu_pre"], proj, p["b_glu"], p["branch_g"])
    dhg = mm_nt_gathered(t + "dhg", dglu, gw["w_glu", l])
    dw_glu = mm_tn_scattered(t + "dw_glu", s["hg"], dglu)
    w = s["ssm_w"]
    site = (l, "ssm_bwd")
    (du, dwcr, dwci, dwbr, dwbi, dar, dai, dds), landed = ssm_bwd(
        t + "ssm_bwd", dhg, s["ypre"], proj, s["xr"], s["xi"], w["wbr"], w["wbi"], w["ar"], w["ai"],
        w["wcr"], w["wci"], p["d_skip"], D, copies=carrier.make(site))
    carrier.store(site, landed)
    G, _, P = p["lr"].shape
    dlr, dli, dld, dbr, dbi = ssm_prep_bwd(
        t + "ssm_prep_bwd", p["lr"], p["li"], p["ld"], p["br"], p["bi"],
        dar.reshape(G, 1, P), dai.reshape(G, 1, P), _diag_blocks(dwbr, False), _diag_blocks(dwbi, False))
    dq, dk, dv = attn_bwd(t + "attn_bwd", proj, s["yattn"], dyattn, D)
    dxp, dwp, dps = pool_bwd(t + "pool_bwd", dypool, proj, s["wp"], p["pool_scale"])
    dproj = jnp.concatenate([dxp, dpg, dq, dk, dv, dag, du, dsg], axis=1)
    PG = dwp.shape[1]
    dwp_s = dwp.reshape(N_POOL_GROUPS, N_DEV, PG // N_DEV, PG).transpose(1, 0, 2, 3)

    def by_target(g):
        return g.reshape(4, 2, -1, g.shape[-1])

    big = dict(w_out=by_target(dw_out.reshape(N_DEV, D // N_DEV, D)), w_glu=by_target(dw_glu),
               w_pool=by_target(dwp_s))
    if split_w_in:
        site = (l, "dw_in_a")
        to_sibling = carrier.split(site, mm_tn_half(t + "dw_in_a", s["h"], dproj, pos, False,
                                                    copies=carrier.make(site, big)))
        site = (l, "dw_in_b")
        mine = carrier.split(site, mm_tn_half(t + "dw_in_b", s["h"], dproj, pos, True,
                                              copies=carrier.make(site, to_sibling[:, None])))
        big["w_in"] = mine[:, None]
    else:
        big["w_in"] = by_target(mm_tn_scattered(t + "dw_in", s["h"], dproj))
    site = (l, "dh")
    dh = carrier.split(site, mm_nt_gathered(t + "dh", dproj, gw["w_in", l], copies=carrier.make(site, big)))
    dx, dlng = rms_bwd(t + "rms_bwd", s["x"], dh, dres, p["ln_g"])
    small = dict(ln_g=dlng[0], pool_scale=dps[0], lam_re=dlr.reshape(G, P), lam_im=dli.reshape(G, P),
                 log_dt=dld.reshape(G), b_re=dbr.transpose(0, 2, 1), b_im=dbi.transpose(0, 2, 1),
                 c_re=_diag_blocks(dwcr, True), c_im=_diag_blocks(dwci, True),
                 d_skip=dds[0], b_glu=dbglu[0], branch_g=dbg[0])
    return dx, big, small


SMALL_NAMES = ("ln_g", "pool_scale", "lam_re", "lam_im", "log_dt", "b_re", "b_im", "c_re", "c_im",
               "d_skip", "b_glu", "branch_g", "final_g")
BIG_NAMES = ("w_in", "w_pool", "w_glu", "w_out")
WEIGHT_ORDER = ("ln_g", "w_in", "w_pool", "pool_scale", "lam_re", "lam_im", "log_dt", "b_re", "b_im",
                "c_re", "c_im", "d_skip", "w_glu", "b_glu", "branch_g", "w_out", "final_g")


PACK_ROWS = 512


def _pack(arrs):
    flat = jnp.concatenate([a.reshape(-1) for a in arrs])
    pad = (-flat.shape[0]) % (PACK_ROWS * LANE)
    return jnp.pad(flat, (0, pad)).reshape(-1, LANE)


def _unpack(packed, like):
    flat = packed.reshape(-1)
    out, off = [], 0
    for a in like:
        out.append(flat[off:off + a.size].reshape(a.shape))
        off += a.size
    return out


def kernel(x, ln_g, w_in, w_pool, pool_scale, lam_re, lam_im, log_dt, b_re, b_im, c_re, c_im, d_skip, w_glu, b_glu, branch_g, w_out, final_g, loss_target, m_ln_g, m_w_in, m_w_pool, m_pool_scale, m_lam_re, m_lam_im, m_log_dt, m_b_re, m_b_im, m_c_re, m_c_im, m_d_skip, m_w_glu, m_b_glu, m_branch_g, m_w_out, m_final_g, v_ln_g, v_w_in, v_w_pool, v_pool_scale, v_lam_re, v_lam_im, v_log_dt, v_b_re, v_b_im, v_c_re, v_c_im, v_d_skip, v_w_glu, v_b_glu, v_branch_g, v_w_out, v_final_g):
    W = dict(ln_g=ln_g, w_in=w_in, w_pool=w_pool, pool_scale=pool_scale, lam_re=lam_re, lam_im=lam_im,
             log_dt=log_dt, b_re=b_re, b_im=b_im, c_re=c_re, c_im=c_im, d_skip=d_skip, w_glu=w_glu,
             b_glu=b_glu, branch_g=branch_g, w_out=w_out, final_g=final_g)
    Mo = dict(ln_g=m_ln_g, w_in=m_w_in, w_pool=m_w_pool, pool_scale=m_pool_scale, lam_re=m_lam_re,
              lam_im=m_lam_im, log_dt=m_log_dt, b_re=m_b_re, b_im=m_b_im, c_re=m_c_re, c_im=m_c_im,
              d_skip=m_d_skip, w_glu=m_w_glu, b_glu=m_b_glu, branch_g=m_branch_g, w_out=m_w_out,
              final_g=m_final_g)
    Vo = dict(ln_g=v_ln_g, w_in=v_w_in, w_pool=v_w_pool, pool_scale=v_pool_scale, lam_re=v_lam_re,
              lam_im=v_lam_im, log_dt=v_log_dt, b_re=v_b_re, b_im=v_b_im, c_re=v_c_re, c_im=v_c_im,
              d_skip=v_d_skip, w_glu=v_w_glu, b_glu=v_b_glu, branch_g=v_branch_g, w_out=v_w_out,
              final_g=v_final_g)
    depth = ln_g.shape[0]
    _, L, D = x.shape
    xc, yc, cc = _position()
    pos = jnp.stack([cc, 2 * xc + yc, 4 * xc + 2 * yc + cc]).astype(jnp.int32)

    shards = {(n, l): cast_bf16(f"cast_{n}_{l}", W[n], l) for n in BIG_NAMES for l in range(depth)}
    gw = {}
    carrier = Carrier()

    def gather_plan(keys):
        return (lambda ctx: gather_copies([shards[k] for k in keys])), (lambda outs: gw.update(zip(keys, outs)))

    first = [("w_in", 0)] + [("w_pool", l) for l in range(depth)]
    gw.update(zip(first, copies_call("gather_first", gather_copies([shards[k] for k in first]))))
    carrier.ride((0, "proj"), *gather_plan([(n, l) for l in range(depth) for n in ("w_out", "w_glu")]))
    if depth > 1:
        carrier.ride((0, "ssm_fwd"), *gather_plan([("w_in", l) for l in range(1, depth)]))

    own, recv_a, recv_b = {}, {}, {}

    def sibling_plan(l, names, pick):
        def make(ctx):
            if isinstance(ctx, dict):
                own.update({(n, l): ctx[n] for n in names})
            return sibling_copies(pick(ctx))
        return make, (lambda outs: recv_a.update(zip([(n, l) for n in names], outs)))

    def chip_plan(l, names):
        def make(ctx):
            parts = [chip_partial(f"chip_partial_{n}_{l}", pos, own[n, l], recv_a[n, l])[:, None] for n in names]
            return chip_copies(parts)
        return make, (lambda outs: recv_b.update(zip([(n, l) for n in names], outs)))

    light = ("w_out", "w_glu", "w_pool")
    for l in range(1, depth):
        carrier.ride((l, "dh"), *sibling_plan(l, BIG_NAMES, lambda big: [big[n] for n in BIG_NAMES]))
        carrier.ride((l - 1, "dy"), *chip_plan(l, light))
        carrier.ride((l - 1, "ssm_bwd"), *chip_plan(l, ("w_in",)))
    carrier.ride((0, "dw_in_a"), *sibling_plan(0, light, lambda big: [big[n] for n in light]))
    carrier.ride((0, "dw_in_b"), *sibling_plan(0, ("w_in",), lambda to_sibling: [to_sibling]))

    def last_chip_make(big):
        own["w_in", 0] = big["w_in"]
        return chip_plan(0, BIG_NAMES)[0](big)

    carrier.ride((0, "dh"), last_chip_make, chip_plan(0, BIG_NAMES)[1])

    params = [_layer_params(l, ln_g, pool_scale, lam_re, lam_im, log_dt, b_re, b_im, c_re, c_im,
                            d_skip, b_glu, branch_g) for l in range(depth)]
    h = x[0]
    saved = []
    for l in range(depth):
        h, s = layer_fwd(l, h, params[l], gw, D, carrier)
        saved.append(s)
    loss_part, dres, dfinal = loss_head("loss_head", h, final_g[None, :], loss_target[0])
    loss = lax.psum(loss_part[0, 0], ("x", "y", "c"))

    small = [None] * depth
    for l in reversed(range(depth)):
        dres, _, small[l] = layer_bwd(l, dres, saved[l], params[l], gw, D, carrier, pos, split_w_in=(l == 0))
    grad_x = dres[None]

    results = {}
    for n in BIG_NAMES:
        shape = W[n].shape
        R, C = int(math.prod(shape[1:-1])), shape[-1]
        w3, m3, v3 = (t.reshape(depth, R, C) for t in (W[n], Mo[n], Vo[n]))
        prev = None
        for l in range(depth):
            prev = adamw_shard(f"adamw_{n}_{l}", pos, l, own[n, l], recv_a[n, l], recv_b[n, l], w3, m3, v3, prev)
        results[n] = [t.reshape(shape) for t in prev]

    small_like = [W[n] for n in SMALL_NAMES]
    small_grads = [jnp.stack([small[l][n] for l in range(depth)]) for n in SMALL_NAMES[:-1]] + [dfinal[0]]
    gathered = copies_call("gather_small_grads", gather_copies([_pack(small_grads)]))[0]
    packed = adamw_small("adamw_small", gathered, _pack(small_like), _pack([Mo[n] for n in SMALL_NAMES]),
                         _pack([Vo[n] for n in SMALL_NAMES]))
    unpacked = [_unpack(t, small_like) for t in packed]
    for i, n in enumerate(SMALL_NAMES):
        results[n] = [unpacked[j][i] for j in range(4)]

    out = [loss, grad_x]
    for j in range(4):
        out += [results[n][j] for n in WEIGHT_ORDER]
    return tuple(out)
```

```python
import functools
import math

import jax
import jax.numpy as jnp
from jax import lax
from jax.experimental import pallas as pl
from jax.experimental.pallas import tpu as pltpu

F32 = jnp.float32
BF16 = jnp.bfloat16
MESH = pl.DeviceIdType.MESH

EPS = 1e-6
HEAD_DIM = 128
SSM_GROUP = 16
SSM_STATE = 64
GROUPS_PER_CHUNK = 8
CHUNK_U = GROUPS_PER_CHUNK * SSM_GROUP
CHUNK_X = GROUPS_PER_CHUNK * SSM_STATE
N_POOL_GROUPS = 4
POOL_HALO = 16
N_DEV = 8
LANE = 128
FULL_K = 4096
ATTN_TILE = 256
ATTN_DECAY_CUTOFF = 100.0
ROW_TILE = 128
VMEM_BIG = 58 * 1024 * 1024
VMEM_MID = 40 * 1024 * 1024

ADAM_LR = 0.001
ADAM_B1 = 0.9
ADAM_B2 = 0.999
ADAM_EPS = 1e-08
ADAM_WD = 0.01
ADAM_STEP = 10
ADAM_C1 = 1.0 / (1.0 - ADAM_B1 ** ADAM_STEP)
ADAM_C2 = 1.0 / (1.0 - ADAM_B2 ** ADAM_STEP)

NN = (((1,), (0,)), ((), ()))
NT = (((1,), (1,)), ((), ()))
TN = (((0,), (0,)), ((), ()))


def _pick(n, cap):
    if n <= cap:
        return n
    step = LANE if cap >= LANE else 8
    t = (cap // step) * step
    while t > step and n % t:
        t -= step
    assert n % t == 0, (n, cap)
    return t


def _cparams(sem, vmem=None):
    return pltpu.CompilerParams(dimension_semantics=sem, vmem_limit_bytes=vmem)


def _dot(a, b, dn=NN):
    return lax.dot_general(a, b, dn, preferred_element_type=F32)


def _sigmoid(x):
    e = jnp.exp(-jnp.abs(x))
    r = 1.0 / (1.0 + e)
    return jnp.where(x >= 0, r, e * r)


HBM = pl.BlockSpec(memory_space=pl.ANY)


class HostedCopies:
    def __init__(self, inputs, out_shape, scratch, phases):
        self.inputs, self.out_shape, self.scratch, self.phases = inputs, out_shape, scratch, phases

    def emit(self, ins, outs, sems, step, total):
        plan = {}
        for frac, fn in self.phases:
            plan.setdefault(min(total - 1, int(frac * total)), []).append(fn)
        for s in sorted(plan):
            def run(fns=plan[s]):
                for fn in fns:
                    fn(ins, outs, sems)
            if total == 1:
                run()
            else:
                pl.when(step == s)(run)


def copies_call(name, copies):
    n_i, n_o = len(copies.inputs), len(copies.out_shape)

    def body(*refs):
        copies.emit(refs[:n_i], refs[n_i:n_i + n_o], refs[n_i + n_o:], 0, 1)

    return pl.pallas_call(
        body, name=name, in_specs=[HBM] * n_i, out_specs=[HBM] * n_o, out_shape=copies.out_shape,
        scratch_shapes=copies.scratch, compiler_params=pltpu.CompilerParams(has_side_effects=True),
    )(*copies.inputs)


def _matmul(name, a, b, *, grid, a_spec, b_spec, o_spec, out_shape, dn,
            res=None, res_spec=None, pos=None, copies=None):
    ni, nj, nk = grid
    n_pos = 0 if pos is None else 1
    n_res = 0 if res is None else 1
    n_ci = 0 if copies is None else len(copies.inputs)
    n_co = 0 if copies is None else len(copies.out_shape)

    def body(*refs):
        refs = refs[n_pos:]
        a_ref, b_ref = refs[:2]
        r_ref = refs[2] if n_res else None
        base = 2 + n_res
        cin = refs[base:base + n_ci]
        o_ref = refs[base + n_ci]
        cout = refs[base + n_ci + 1:base + n_ci + 1 + n_co]
        sems = refs[base + n_ci + 1 + n_co:]
        k = pl.program_id(2)
        if copies is not None:
            step = (pl.program_id(0) * nj + pl.program_id(1)) * nk + k
            copies.emit(cin, cout, sems, step, ni * nj * nk)

        part = _dot(a_ref[...].astype(BF16), b_ref[...].astype(BF16), dn)
        if nk == 1:
            if r_ref is not None:
                part = part + r_ref[...]
            o_ref[...] = part.astype(o_ref.dtype)
        else:
            @pl.when(k == 0)
            def _():
                o_ref[...] = part if r_ref is None else part + r_ref[...]

            @pl.when(k > 0)
            def _():
                o_ref[...] += part

    assert nk == 1 or out_shape.dtype == F32
    in_specs = [a_spec, b_spec] + ([res_spec] if n_res else []) + [HBM] * n_ci
    args = ((pos,) if n_pos else ()) + (a, b) + ((res,) if n_res else ()) + tuple(copies.inputs if copies else ())
    out_specs = [o_spec] + [HBM] * n_co
    out_shapes = [out_shape] + list(copies.out_shape if copies else [])
    scratch = list(copies.scratch if copies else [])
    params = pltpu.CompilerParams(
        dimension_semantics=("arbitrary",) * 3 if copies else ("parallel", "parallel", "arbitrary"),
        vmem_limit_bytes=VMEM_BIG, has_side_effects=copies is not None)
    out = pl.pallas_call(
        body, name=name,
        grid_spec=pltpu.PrefetchScalarGridSpec(
            num_scalar_prefetch=n_pos, grid=grid, in_specs=in_specs, out_specs=out_specs, scratch_shapes=scratch),
        out_shape=out_shapes, compiler_params=params)(*args)
    return out[0] if copies is None else (out[0], list(out[1:]))


def mm_nn_gathered(name, a, wg, out_dtype=F32, copies=None):
    M, K = a.shape
    _, _, nper = wg.shape
    tm, tk, tn = _pick(M, 1024), _pick(K, FULL_K), _pick(nper, 768)
    r = nper // tn
    return _matmul(
        name, a, wg, grid=(M // tm, N_DEV * r, K // tk),
        a_spec=pl.BlockSpec((tm, tk), lambda i, j, k, *_: (i, k)),
        b_spec=pl.BlockSpec((None, tk, tn), lambda i, j, k, *_: (j // r, k, j % r)),
        o_spec=pl.BlockSpec((tm, tn), lambda i, j, k, *_: (i, j)),
        out_shape=jax.ShapeDtypeStruct((M, N_DEV * nper), out_dtype), dn=NN, copies=copies)


def mm_nt_gathered(name, a, wg, out_dtype=F32, copies=None):
    M, _ = a.shape
    _, N, nper = wg.shape
    tm, tn, tk = _pick(M, 1024), _pick(N, 1024), _pick(nper, 1536)
    r = nper // tk
    return _matmul(
        name, a, wg, grid=(M // tm, N // tn, N_DEV * r),
        a_spec=pl.BlockSpec((tm, tk), lambda i, j, k, *_: (i, k)),
        b_spec=pl.BlockSpec((None, tn, tk), lambda i, j, k, *_: (k // r, j, k % r)),
        o_spec=pl.BlockSpec((tm, tn), lambda i, j, k, *_: (i, j)),
        out_shape=jax.ShapeDtypeStruct((M, N), out_dtype), dn=NT, copies=copies)


def mm_tn_scattered(name, a, b, copies=None):
    L, M = a.shape
    nper = b.shape[1] // N_DEV
    tm, tn, tk = _pick(M, 1024), _pick(nper, 768), _pick(L, FULL_K)
    r = nper // tn
    return _matmul(
        name, a, b, grid=(M // tm, N_DEV * r, L // tk),
        a_spec=pl.BlockSpec((tk, tm), lambda i, j, k, *_: (k, i)),
        b_spec=pl.BlockSpec((tk, tn), lambda i, j, k, *_: (k, j)),
        o_spec=pl.BlockSpec((None, tm, tn), lambda i, j, k, *_: (j // r, i, j % r)),
        out_shape=jax.ShapeDtypeStruct((N_DEV, M, nper), F32), dn=TN, copies=copies)


def mm_tn_half(name, a, b, pos, own, copies=None):
    L, M = a.shape
    nper = b.shape[1] // N_DEV
    tm, tn, tk = _pick(M, 1024), _pick(nper, 768), _pick(L, FULL_K)
    r = nper // tn

    def b_map(i, j, k, p):
        core = p[0] if own else 1 - p[0]
        return (k, (2 * (j // r) + core) * r + j % r)

    return _matmul(
        name, a, b, grid=(M // tm, 4 * r, L // tk),
        a_spec=pl.BlockSpec((tk, tm), lambda i, j, k, *_: (k, i)),
        b_spec=pl.BlockSpec((tk, tn), b_map),
        o_spec=pl.BlockSpec((None, tm, tn), lambda i, j, k, *_: (j // r, i, j % r)),
        out_shape=jax.ShapeDtypeStruct((4, M, nper), F32), dn=TN, pos=pos, copies=copies)


def mm_plain(name, a, b, dn, out_dtype=F32, res=None, copies=None):
    if dn == NN:
        (M, K), N = a.shape, b.shape[1]
    elif dn == NT:
        (M, K), N = a.shape, b.shape[0]
    else:
        (K, M), N = a.shape, b.shape[1]
    tm, tn, tk = _pick(M, 1024), _pick(N, 512), _pick(K, FULL_K)
    a_spec =(pl.BlockSpec((tk, tm), lambda i, j, k, *_: (k, i)) if dn == TN
              else pl.BlockSpec((tm, tk), lambda i, j, k, *_: (i, k)))
    b_spec = (pl.BlockSpec((tn, tk), lambda i, j, k, *_: (j, k)) if dn == NT
              else pl.BlockSpec((tk, tn), lambda i, j, k, *_: (k, j)))
    o_spec = pl.BlockSpec((tm, tn), lambda i, j, k, *_: (i, j))
    return _matmul(
        name, a, b, grid=(M // tm, N // tn, K // tk), a_spec=a_spec, b_spec=b_spec, o_spec=o_spec,
        out_shape=jax.ShapeDtypeStruct((M, N), out_dtype), dn=dn,
        res=res, res_spec=o_spec if res is not None else None, copies=copies)


def rms_fwd(name, x, g):
    L, D = x.shape
    tr = _pick(L, ROW_TILE)

    def body(x_ref, g_ref, h_ref):
        xv = x_ref[...]
        r = lax.rsqrt(jnp.mean(xv * xv, axis=-1, keepdims=True) + EPS)
        h_ref[...] = (xv * r * g_ref[...]).astype(BF16)

    return pl.pallas_call(
        body, name=name, grid=(L // tr,),
        in_specs=[pl.BlockSpec((tr, D), lambda i: (i, 0)), pl.BlockSpec((1, D), lambda i: (0, 0))],
        out_specs=pl.BlockSpec((tr, D), lambda i: (i, 0)),
        out_shape=jax.ShapeDtypeStruct((L, D), BF16),
        compiler_params=_cparams(("parallel",), VMEM_MID))(x, g)


def rms_bwd(name, x, dh, dres, g):
    L, D = x.shape
    tr = _pick(L, ROW_TILE)

    def body(x_ref, dh_ref, dr_ref, g_ref, dx_ref, dxb_ref, dg_ref):
        xv = x_ref[...]
        r = lax.rsqrt(jnp.mean(xv * xv, axis=-1, keepdims=True) + EPS)
        xh = xv * r
        dhv = dh_ref[...]
        dn = dhv * g_ref[...]
        dxv = dr_ref[...] + r * (dn - xh * jnp.mean(dn * xh, axis=-1, keepdims=True))
        dx_ref[...] = dxv
        dxb_ref[...] = dxv.astype(BF16)

        @pl.when(pl.program_id(0) == 0)
        def _():
            dg_ref[...] = jnp.zeros_like(dg_ref)

        dg_ref[...] += jnp.sum(dhv * xh, axis=0, keepdims=True)

    row = pl.BlockSpec((tr, D), lambda i: (i, 0))
    vec = pl.BlockSpec((1, D), lambda i: (0, 0))
    return pl.pallas_call(
        body, name=name, grid=(L // tr,), in_specs=[row, row, row, vec], out_specs=[row, row, vec],
        out_shape=[jax.ShapeDtypeStruct((L, D), F32), jax.ShapeDtypeStruct((L, D), BF16),
                   jax.ShapeDtypeStruct((1, D), F32)],
        compiler_params=_cparams(("arbitrary",), VMEM_MID))(x, dh, dres, g)


def loss_head(name, x, g, target):
    L, D = x.shape
    tr = _pick(L, ROW_TILE)

    def body(x_ref, g_ref, t_ref, loss_ref, dx_ref, dxb_ref, dg_ref):
        xv = x_ref[...]
        gv = g_ref[...]
        r = lax.rsqrt(jnp.mean(xv * xv, axis=-1, keepdims=True) + EPS)
        xh = xv * r
        err = xh * gv - t_ref[...]
        dy = err * (1.0 / D)
        dn = dy * gv
        dxv = r * (dn - xh * jnp.mean(dn * xh, axis=-1, keepdims=True))
        dx_ref[...] = dxv
        dxb_ref[...] = dxv.astype(BF16)

        @pl.when(pl.program_id(0) == 0)
        def _():
            dg_ref[...] = jnp.zeros_like(dg_ref)
            loss_ref[...] = jnp.zeros_like(loss_ref)

        dg_ref[...] += jnp.sum(dy * xh, axis=0, keepdims=True)
        row_loss = jnp.sum(err * err, axis=-1, keepdims=True) * (0.5 / D)
        loss_ref[...] += jnp.sum(row_loss, axis=0, keepdims=True)

    row = pl.BlockSpec((tr, D), lambda i: (i, 0))
    vec = pl.BlockSpec((1, D), lambda i: (0, 0))
    one = pl.BlockSpec((1, 1), lambda i: (0, 0))
    return pl.pallas_call(
        body, name=name, grid=(L // tr,), in_specs=[row, vec, row], out_specs=[one, row, row, vec],
        out_shape=[jax.ShapeDtypeStruct((1, 1), F32), jax.ShapeDtypeStruct((L, D), F32),
                   jax.ShapeDtypeStruct((L, D), BF16), jax.ShapeDtypeStruct((1, D), F32)],
        compiler_params=_cparams(("arbitrary",), VMEM_MID))(x, g, target)


def _branch_specs(D, tr):
    DP, DA, DS = D // 4, D // 2, D // 4
    return dict(
        pool=pl.BlockSpec((tr, DP), lambda i: (i, 0)),
        attn=pl.BlockSpec((tr, DA), lambda i: (i, 0)),
        glu=pl.BlockSpec((tr, 2 * DS), lambda i: (i, 0)),
        p_gate=pl.BlockSpec((tr, DP), lambda i: (i, 1)),
        a_gate=pl.BlockSpec((tr, DA), lambda i: (i, 4)),
        s_gate=pl.BlockSpec((tr, DS), lambda i: (i, 11)),
        bglu=pl.BlockSpec((1, 2 * DS), lambda i: (0, 0)),
        bg=pl.BlockSpec((1, D), lambda i: (0, 0)),
        row=pl.BlockSpec((tr, D), lambda i: (i, 0)),
    )


def branch_fwd(name, ypool, yattn, glu_pre, proj, b_glu, branch_g):
    L, DP = ypool.shape
    D = 4 * DP
    DA, DS = D // 2, D // 4
    tr = _pick(L, ROW_TILE)
    s = _branch_specs(D, tr)

    def body(yp_ref, ya_ref, gl_ref, pg_ref, ag_ref, sg_ref, bgl_ref, bg_ref, y_ref):
        pre = gl_ref[...] + bgl_ref[...]
        ys = pre[:, :DS] * _sigmoid(pre[:, DS:])
        bg = bg_ref[...]

        def one(raw, gate, g):
            r = lax.rsqrt(jnp.mean(raw * raw, axis=-1, keepdims=True) + EPS)
            return raw * r * g * (gate * _sigmoid(gate))

        y_ref[:, :DP] = one(yp_ref[...], pg_ref[...], bg[:, :DP]).astype(BF16)
        y_ref[:, DP:DP + DA] = one(ya_ref[...], ag_ref[...], bg[:, DP:DP + DA]).astype(BF16)
        y_ref[:, DP + DA:] = one(ys, sg_ref[...], bg[:, DP + DA:]).astype(BF16)

    return pl.pallas_call(
        body, name=name, grid=(L // tr,),
        in_specs=[s["pool"], s["attn"], s["glu"], s["p_gate"], s["a_gate"], s["s_gate"], s["bglu"], s["bg"]],
        out_specs=s["row"], out_shape=jax.ShapeDtypeStruct((L, D), BF16),
        compiler_params=_cparams(("parallel",), VMEM_MID))(ypool, yattn, glu_pre, proj, proj, proj, b_glu, branch_g)


def branch_bwd(name, dy, ypool, yattn, glu_pre, proj, b_glu, branch_g):
    L, DP = ypool.shape
    D = 4 * DP
    DA, DS = D // 2, D // 4
    tr = _pick(L, ROW_TILE // 2)
    s = _branch_specs(D, tr)

    def body(dy_ref, yp_ref, ya_ref, gl_ref, pg_ref, ag_ref, sg_ref, bgl_ref, bg_ref,
             dyp_ref, dya_ref, dgl_ref, dpg_ref, dag_ref, dsg_ref, dbg_ref, dbgl_ref):
        @pl.when(pl.program_id(0) == 0)
        def _():
            dbg_ref[...] = jnp.zeros_like(dbg_ref)
            dbgl_ref[...] = jnp.zeros_like(dbgl_ref)

        bg = bg_ref[...]

        def one(raw, gate, g, dyb):
            r = lax.rsqrt(jnp.mean(raw * raw, axis=-1, keepdims=True) + EPS)
            n = raw * r
            sg = _sigmoid(gate)
            sl = gate * sg
            dgate = dyb * n * g * (sg * (1.0 + gate * (1.0 - sg)))
            dbg = jnp.sum(dyb * n * sl, axis=0, keepdims=True)
            dn = dyb * g * sl
            draw = r * (dn - n * jnp.mean(dn * n, axis=-1, keepdims=True))
            return draw, dgate, dbg

        draw, dgate, dbg = one(yp_ref[...], pg_ref[...], bg[:, :DP], dy_ref[:, :DP])
        dyp_ref[...] = draw
        dpg_ref[...] = dgate.astype(BF16)
        dbg_ref[:, :DP] += dbg

        draw, dgate, dbg = one(ya_ref[...], ag_ref[...], bg[:, DP:DP + DA], dy_ref[:, DP:DP + DA])
        dya_ref[...] = draw
        dag_ref[...] = dgate.astype(BF16)
        dbg_ref[:, DP:DP + DA] += dbg

        pre = gl_ref[...] + bgl_ref[...]
        val = pre[:, :DS]
        sgt = _sigmoid(pre[:, DS:])
        draw, dgate, dbg = one(val * sgt, sg_ref[...], bg[:, DP + DA:], dy_ref[:, DP + DA:])
        dsg_ref[...] = dgate.astype(BF16)
        dbg_ref[:, DP + DA:] += dbg
        dval = draw * sgt
        dgt = draw * val * sgt * (1.0 - sgt)
        dgl_ref[:, :DS] = dval.astype(BF16)
        dgl_ref[:, DS:] = dgt.astype(BF16)
        dbgl_ref[:, :DS] += jnp.sum(dval, axis=0, keepdims=True)
        dbgl_ref[:, DS:] += jnp.sum(dgt, axis=0, keepdims=True)

    loc = lambda w: pl.BlockSpec((tr, w), lambda i: (i, 0))
    return pl.pallas_call(
        body, name=name, grid=(L // tr,),
        in_specs=[s["row"], s["pool"], s["attn"], s["glu"], s["p_gate"], s["a_gate"], s["s_gate"], s["bglu"], s["bg"]],
        out_specs=[loc(DP), loc(DA), loc(2 * DS), loc(DP), loc(DA), loc(DS), s["bg"], s["bglu"]],
        out_shape=[jax.ShapeDtypeStruct((L, DP), F32), jax.ShapeDtypeStruct((L, DA), F32),
                   jax.ShapeDtypeStruct((L, 2 * DS), BF16), jax.ShapeDtypeStruct((L, DP), BF16),
                   jax.ShapeDtypeStruct((L, DA), BF16), jax.ShapeDtypeStruct((L, DS), BF16),
                   jax.ShapeDtypeStruct((1, D), F32), jax.ShapeDtypeStruct((1, 2 * DS), F32)],
        compiler_params=_cparams(("arbitrary",), VMEM_BIG),
    )(dy, ypool, yattn, glu_pre, proj, proj, proj, b_glu, branch_g)


def _pool_select(g, s2, s4, s8, s16):
    return jnp.where(g == 0, s2, jnp.where(g == 1, s4, jnp.where(g == 2, s8, s16)))


def _pool_window(g):
    return jnp.where(g == 0, 2.0, jnp.where(g == 1, 4.0, jnp.where(g == 2, 8.0, 16.0))).astype(F32)


def _pooled_chunk(pad, g, r0, ch):
    xh = pad[pl.ds(r0, ch + POOL_HALO), :]
    s2 = xh + pltpu.roll(xh, 1, 0)
    s4 = s2 + pltpu.roll(s2, 2, 0)
    s8 = s4 + pltpu.roll(s4, 4, 0)
    s16 = s8 + pltpu.roll(s8, 8, 0)
    win = _pool_select(g, s2, s4, s8, s16)[POOL_HALO:]
    pos = (r0 + 1 + lax.broadcasted_iota(jnp.int32, (ch, 1), 0)).astype(F32)
    return win / jnp.minimum(pos, _pool_window(g)) - xh[POOL_HALO:]


def pool_fwd(name, proj, wp, scale):
    L = proj.shape[0]
    DP = scale.shape[1]
    PG = DP // N_POOL_GROUPS
    ch = _pick(L, 256)

    def body(x_ref, w_ref, s_ref, o_ref, pad):
        g = pl.program_id(0)
        pad[0:POOL_HALO, :] = jnp.zeros((POOL_HALO, PG), F32)
        pad[POOL_HALO:, :] = x_ref[...]

        def chunk(ci, carry):
            r0 = pl.multiple_of(ci * ch, ch)
            pooled = _pooled_chunk(pad, g, r0, ch)
            o_ref[pl.ds(r0, ch), :] = _dot(pooled.astype(BF16), w_ref[...]) * s_ref[...]
            return carry

        lax.fori_loop(0, L // ch, chunk, 0)

    return pl.pallas_call(
        body, name=name, grid=(N_POOL_GROUPS,),
        in_specs=[pl.BlockSpec((L, PG), lambda g: (0, g)), pl.BlockSpec((None, PG, PG), lambda g: (g, 0, 0)),
                  pl.BlockSpec((1, PG), lambda g: (0, g))],
        out_specs=pl.BlockSpec((L, PG), lambda g: (0, g)),
        out_shape=jax.ShapeDtypeStruct((L, DP), F32),
        scratch_shapes=[pltpu.VMEM((L + POOL_HALO, PG), F32)],
        compiler_params=_cparams(("parallel",), VMEM_MID))(proj, wp, scale)


def pool_bwd(name, dyraw, proj, wp, scale):
    L = proj.shape[0]
    DP = scale.shape[1]
    PG = DP // N_POOL_GROUPS
    ch = _pick(L, 256)

    def body(dy_ref, x_ref, w_ref, s_ref, dx_ref, dw_ref, ds_ref, pad, dpad, dpo):
        g = pl.program_id(0)
        pad[0:POOL_HALO, :] = jnp.zeros((POOL_HALO, PG), F32)
        pad[POOL_HALO:, :] = x_ref[...]
        dpad[L:, :] = jnp.zeros((POOL_HALO, PG), F32)
        dw_ref[...] = jnp.zeros_like(dw_ref)
        ds_ref[...] = jnp.zeros_like(ds_ref)
        wv = w_ref[...]
        win_f = _pool_window(g)

        def chunk(ci, carry):
            r0 = pl.multiple_of(ci * ch, ch)
            pooled = _pooled_chunk(pad, g, r0, ch).astype(BF16)
            dyv = dy_ref[pl.ds(r0, ch), :]
            ds_ref[...] += jnp.sum(dyv * _dot(pooled, wv), axis=0, keepdims=True)
            dmixed = (dyv * s_ref[...]).astype(BF16)
            dw_ref[...] += _dot(pooled, dmixed, TN)
            dpooled = _dot(dmixed, wv, NT)
            pos = (r0 + 1 + lax.broadcasted_iota(jnp.int32, (ch, 1), 0)).astype(F32)
            dpad[pl.ds(r0, ch), :] = dpooled / jnp.minimum(pos, win_f)
            dpo[pl.ds(r0, ch), :] = dpooled
            return carry

        lax.fori_loop(0, L // ch, chunk, 0)

        def chunk2(ci, carry):
            r0 = pl.multiple_of(ci * ch, ch)
            n = ch + POOL_HALO
            dm = dpad[pl.ds(r0, n), :]
            s2 = dm + pltpu.roll(dm, n - 1, 0)
            s4 = s2 + pltpu.roll(s2, n - 2, 0)
            s8 = s4 + pltpu.roll(s4, n - 4, 0)
            s16 = s8 + pltpu.roll(s8, n - 8, 0)
            win = _pool_select(g, s2, s4, s8, s16)[:ch]
            dx_ref[pl.ds(r0, ch), :] = (win - dpo[pl.ds(r0, ch), :]).astype(BF16)
            return carry

        lax.fori_loop(0, L // ch, chunk2, 0)

    col = pl.BlockSpec((L, PG), lambda g: (0, g))
    return pl.pallas_call(
        body, name=name, grid=(N_POOL_GROUPS,),
        in_specs=[col, col, pl.BlockSpec((None, PG, PG), lambda g: (g, 0, 0)), pl.BlockSpec((1, PG), lambda g: (0, g))],
        out_specs=[col, pl.BlockSpec((None, PG, PG), lambda g: (g, 0, 0)), pl.BlockSpec((1, PG), lambda g: (0, g))],
        out_shape=[jax.ShapeDtypeStruct((L, DP), BF16), jax.ShapeDtypeStruct((N_POOL_GROUPS, PG, PG), F32),
                   jax.ShapeDtypeStruct((1, DP), F32)],
        scratch_shapes=[pltpu.VMEM((L + POOL_HALO, PG), F32), pltpu.VMEM((L + POOL_HALO, PG), F32),
                        pltpu.VMEM((L, PG), F32)],
        compiler_params=_cparams(("parallel",), VMEM_MID))(dyraw, proj, wp, scale)


def _attn_tile(L):
    return _pick(L, ATTN_TILE)


def _tri(t, strict):
    j = lax.broadcasted_iota(jnp.int32, (t, t), 0)
    s = lax.broadcasted_iota(jnp.int32, (t, t), 1)
    return ((j > s) if strict else (j >= s)).astype(BF16)


def _attn_block(q, kt, i, k0, rb, after):
    tq, tk = q.shape[0], kt.shape[0]
    row = lax.broadcasted_iota(jnp.int32, (tq, tk), 0)
    col = lax.broadcasted_iota(jnp.int32, (tq, tk), 1)
    causal = (k0 + col) < (i * tq + row)
    z = _dot(q, kt, NT)
    e = jnp.exp(-jnp.abs(z))
    l1p = jnp.log(1.0 + e)
    log_sig = jnp.minimum(z, 0.0) - l1p
    log_1m = -jnp.maximum(z, 0.0) - l1p
    b = jnp.where(causal, log_1m, 0.0)
    b_hi = b.astype(BF16)
    b_lo = (b - b_hi.astype(F32)).astype(BF16)
    suffix = _dot(b_hi, after) + _dot(b_lo, after) + rb
    w = jnp.where(causal, jnp.exp(log_sig + suffix), 0.0)
    return z, e, causal, b, w


def attn_fwd(name, proj, D):
    L = proj.shape[0]
    DA = D // 2
    H = DA // HEAD_DIM
    tq = tk = _attn_tile(L)
    qo, ko, vo = (D // 2) // HEAD_DIM, D // HEAD_DIM, (3 * D // 2) // HEAD_DIM
    scale = HEAD_DIM ** -0.5

    def body(q_ref, k_ref, v_ref, tri_ref, o_ref, kb_s, vb_s, acc, rb):
        i = pl.program_id(1)

        @pl.when(i == 0)
        def _():
            kb_s[...] = k_ref[...].astype(BF16)
            vb_s[...] = v_ref[...].astype(BF16)

        q = (q_ref[...] * scale).astype(BF16)
        acc[...] = jnp.zeros_like(acc)
        rb[...] = jnp.zeros_like(rb)

        def cond(c):
            return jnp.logical_and(c[0] >= 0, c[1])

        def step(c):
            kb = c[0]
            k0 = pl.multiple_of(kb * tk, tk)
            kt = kb_s[pl.ds(k0, tk), :]
            vt = vb_s[pl.ds(k0, tk), :]
            _, _, _, b, w = _attn_block(q, kt, i, k0, rb[...], tri_ref[...])
            acc[...] += _dot(w.astype(BF16), vt)
            rbn = rb[...] + jnp.sum(b, axis=1, keepdims=True)
            rb[...] = rbn
            return kb - 1, jnp.max(rbn) > -ATTN_DECAY_CUTOFF

        lax.while_loop(cond, step, (i, jnp.bool_(True)))
        o_ref[...] = acc[...]

    return pl.pallas_call(
        body, name=name, grid=(H, L // tq),
        in_specs=[pl.BlockSpec((tq, HEAD_DIM), lambda h, i: (i, qo + h)),
                  pl.BlockSpec((L, HEAD_DIM), lambda h, i: (0, ko + h)),
                  pl.BlockSpec((L, HEAD_DIM), lambda h, i: (0, vo + h)),
                  pl.BlockSpec((tk, tk), lambda h, i: (0, 0))],
        out_specs=pl.BlockSpec((tq, HEAD_DIM), lambda h, i: (i, h)),
        out_shape=jax.ShapeDtypeStruct((L, DA), F32),
        scratch_shapes=[pltpu.VMEM((L, HEAD_DIM), BF16), pltpu.VMEM((L, HEAD_DIM), BF16),
                        pltpu.VMEM((tq, HEAD_DIM), F32), pltpu.VMEM((tq, 1), F32)],
        compiler_params=_cparams(("arbitrary", "arbitrary"), VMEM_MID))(proj, proj, proj, _tri(tk, True))


def attn_bwd(name, proj, o, do, D):
    L = proj.shape[0]
    DA = D // 2
    H = DA // HEAD_DIM
    tq = tk = _attn_tile(L)
    qo, ko, vo = (D // 2) // HEAD_DIM, D // HEAD_DIM, (3 * D // 2) // HEAD_DIM
    scale = HEAD_DIM ** -0.5

    def body(q_ref, k_ref, v_ref, o_ref, do_ref, after_ref, from_ref, dq_ref, dk_ref, dv_ref,
             kb_s, vb_s, dk_s, dv_s, dq_acc, rb, rg):
        i = pl.program_id(1)
        nq = pl.num_programs(1)

        @pl.when(i == 0)
        def _():
            kb_s[...] = k_ref[...].astype(BF16)
            vb_s[...] = v_ref[...].astype(BF16)
            dk_s[...] = jnp.zeros_like(dk_s)
            dv_s[...] = jnp.zeros_like(dv_s)

        q = (q_ref[...] * scale).astype(BF16)
        dob = do_ref[...].astype(BF16)
        delta = jnp.sum(dob.astype(F32) * o_ref[...], axis=1, keepdims=True)
        dq_acc[...] = jnp.zeros_like(dq_acc)
        rb[...] = jnp.zeros_like(rb)
        rg[...] = jnp.zeros_like(rg)

        def cond(c):
            return jnp.logical_and(c[0] >= 0, c[1])

        def step(c):
            kb = c[0]
            k0 = pl.multiple_of(kb * tk, tk)
            kt = kb_s[pl.ds(k0, tk), :]
            vt = vb_s[pl.ds(k0, tk), :]
            z, e, causal, b, w = _attn_block(q, kt, i, k0, rb[...], after_ref[...])
            wq = w.astype(BF16)
            dw = _dot(dob, vt, NT)
            g = wq.astype(F32) * dw
            g_hi = g.astype(BF16)
            g_lo = (g - g_hi.astype(F32)).astype(BF16)
            from_s = from_ref[...]
            suffix_g = _dot(g_hi, from_s) + _dot(g_lo, from_s) + rg[...]
            before = delta - suffix_g
            r = 1.0 / (1.0 + e)
            sig = jnp.where(z >= 0, r, e * r)
            sig_neg = jnp.where(z >= 0, e * r, r)
            dz = jnp.where(causal, g * sig_neg - before * sig, 0.0).astype(BF16)
            dq_acc[...] += _dot(dz, kt)
            dk_s[pl.ds(k0, tk), :] += _dot(dz, q, TN)
            dv_s[pl.ds(k0, tk), :] += _dot(wq, dob, TN)
            rbn = rb[...] + jnp.sum(b, axis=1, keepdims=True)
            rb[...] = rbn
            rg[...] += jnp.sum(g, axis=1, keepdims=True)
            return kb - 1, jnp.max(rbn) > -ATTN_DECAY_CUTOFF

        lax.while_loop(cond, step, (i, jnp.bool_(True)))
        dq_ref[...] = (dq_acc[...] * scale).astype(BF16)

        @pl.when(i == nq - 1)
        def _():
            dk_ref[...] = dk_s[...].astype(BF16)
            dv_ref[...] = dv_s[...].astype(BF16)

    blk = pl.BlockSpec((tq, HEAD_DIM), lambda h, i: (i, h))
    full = pl.BlockSpec((L, HEAD_DIM), lambda h, i: (0, h))
    return pl.pallas_call(
        body, name=name, grid=(H, L // tq),
        in_specs=[pl.BlockSpec((tq, HEAD_DIM), lambda h, i: (i, qo + h)),
                  pl.BlockSpec((L, HEAD_DIM), lambda h, i: (0, ko + h)),
                  pl.BlockSpec((L, HEAD_DIM), lambda h, i: (0, vo + h)), blk, blk,
                  pl.BlockSpec((tk, tk), lambda h, i: (0, 0)), pl.BlockSpec((tk, tk), lambda h, i: (0, 0))],
        out_specs=[blk, full, full],
        out_shape=[jax.ShapeDtypeStruct((L, DA), BF16)] * 3,
        scratch_shapes=[pltpu.VMEM((L, HEAD_DIM), BF16), pltpu.VMEM((L, HEAD_DIM), BF16),
                        pltpu.VMEM((L, HEAD_DIM), F32), pltpu.VMEM((L, HEAD_DIM), F32),
                        pltpu.VMEM((tq, HEAD_DIM), F32), pltpu.VMEM((tq, 1), F32), pltpu.VMEM((tq, 1), F32)],
        compiler_params=_cparams(("arbitrary", "arbitrary"), VMEM_MID),
    )(proj, proj, proj, o, do, _tri(tk, True), _tri(tk, False))


def _cmul(ar, ai, br, bi):
    return ar * br - ai * bi, ar * bi + ai * br


def _cmul_conj(ar, ai, br, bi):
    return ar * br + ai * bi, ar * bi - ai * br


def _ssm_disc(lr, li, ld):
    dt = jnp.exp(ld)
    m = jnp.exp(lr * dt)
    ar, ai = m * jnp.cos(li * dt), m * jnp.sin(li * dt)
    inv = 1.0 / (lr * lr + li * li)
    fr, fi = _cmul(ar - 1.0, ai, lr * inv, -li * inv)
    return dt, ar, ai, fr, fi, inv


def ssm_prep(name, lr, li, ld, br, bi):
    def body(lr_ref, li_ref, ld_ref, br_ref, bi_ref, zr_ref, zi_ref, bbr_ref, bbi_ref):
        dt, _, _, fr, fi, _ = _ssm_disc(lr_ref[...], li_ref[...], ld_ref[...])
        zr_ref[...] = lr_ref[...] * dt
        zi_ref[...] = li_ref[...] * dt
        bbr, bbi = _cmul(fr, fi, br_ref[...], bi_ref[...])
        bbr_ref[...] = bbr
        bbi_ref[...] = bbi

    sd = jax.ShapeDtypeStruct
    return pl.pallas_call(
        body, name=name,
        out_shape=[sd(lr.shape, F32), sd(lr.shape, F32), sd(br.shape, F32), sd(br.shape, F32)],
    )(lr, li, ld, br, bi)


def ssm_prep_bwd(name, lr, li, ld, br, bi, gar, gai, gbr, gbi):
    def body(lr_ref, li_ref, ld_ref, br_ref, bi_ref, gar_ref, gai_ref, gbr_ref, gbi_ref,
             dlr_ref, dli_ref, dld_ref, dbr_ref, dbi_ref):
        lr_, li_ = lr_ref[...], li_ref[...]
        dt, ar, ai, fr, fi, inv = _ssm_disc(lr_, li_, ld_ref[...])
        gbr_, gbi_ = gbr_ref[...], gbi_ref[...]
        dbr, dbi = _cmul_conj(fr, fi, gbr_, gbi_)
        dbr_ref[...] = dbr
        dbi_ref[...] = dbi
        pr, pi = _cmul_conj(br_ref[...], bi_ref[...], gbr_, gbi_)
        gfr = jnp.sum(pr, axis=1, keepdims=True)
        gfi = jnp.sum(pi, axis=1, keepdims=True)
        ilr, ili = lr_ * inv, -li_ * inv
        tr_, ti_ = _cmul_conj(ilr, ili, gfr, gfi)
        gatr, gati = gar_ref[...] + tr_, gai_ref[...] + ti_
        hr, hi = _cmul(fr, fi, ilr, ili)
        t1r, t1i = _cmul_conj(ar * dt, ai * dt, gatr, gati)
        t2r, t2i = _cmul_conj(hr, hi, gfr, gfi)
        dlr_ref[...] = t1r - t2r
        dli_ref[...] = t1i - t2i
        lar, lai = _cmul(lr_, li_, ar, ai)
        gdt, _ = _cmul_conj(lar, lai, gatr, gati)
        dld_ref[...] = jnp.sum(gdt, axis=2, keepdims=True) * dt

    sd = jax.ShapeDtypeStruct
    return pl.pallas_call(
        body, name=name,
        out_shape=[sd(lr.shape, F32), sd(lr.shape, F32), sd(ld.shape, F32), sd(br.shape, F32), sd(br.shape, F32)],
    )(lr, li, ld, br, bi, gar, gai, gbr, gbi)


SCAN_ROWS = 64


def _scan_rows(L):
    return min(SCAN_ROWS, L)


def _power_table(pr_s, pi_s, zr, zi, L, reverse):
    R = _scan_rows(L)
    row = lax.broadcasted_iota(jnp.int32, (R, 1), 0).astype(F32)
    dist = (R - row) if reverse else (row + 1.0)
    mag = jnp.exp(dist * zr)
    pr_s[...] = mag * jnp.cos(dist * zi)
    pi_s[...] = mag * jnp.sin(dist * zi)


def _scan(xr, xi, pr_s, pi_s, L, reverse):
    R = _scan_rows(L)
    nt = L // R
    assert L % R == 0 and R & (R - 1) == 0
    ns = CHUNK_X // LANE
    ridx = lax.broadcasted_iota(jnp.int32, (R, LANE), 0)

    def power(ref, d, cs):
        at = R - d if reverse else d - 1
        return ref[at:at + 1, cs]

    def shift(v, d):
        if d < 8:
            if reverse:
                return jnp.where(ridx < R - d, pltpu.roll(v, R - d, 0), 0.0)
            return jnp.where(ridx >= d, pltpu.roll(v, d, 0), 0.0)
        zeros = jnp.zeros((d, LANE), F32)
        return jnp.concatenate([v[d:], zeros], 0) if reverse else jnp.concatenate([zeros, v[:R - d]], 0)

    def tile(n, carry):
        t = nt - 1 - n if reverse else n
        rows = pl.ds(pl.multiple_of(t * R, R), R)
        edges = []
        for c in range(ns):
            cs = slice(c * LANE, (c + 1) * LANE)
            vr, vi = xr[rows, cs], xi[rows, cs]
            d = 1
            while d < R:
                ar, ai = power(pr_s, d, cs), power(pi_s, d, cs)
                sr, si = shift(vr, d), shift(vi, d)
                vr, vi = vr + ar * sr - ai * si, vi + ar * si + ai * sr
                d *= 2
            cr, ci = carry[2 * c], carry[2 * c + 1]
            pr, pi = pr_s[:, cs], pi_s[:, cs]
            vr, vi = vr + pr * cr - pi * ci, vi + pr * ci + pi * cr
            xr[rows, cs] = vr
            xi[rows, cs] = vi
            edge = slice(0, 1) if reverse else slice(R - 1, R)
            edges += [vr[edge], vi[edge]]
        return tuple(edges)

    lax.fori_loop(0, nt, tile, tuple(jnp.zeros((1, LANE), F32) for _ in range(2 * ns)))


def _gelu(x):
    t = jnp.tanh(0.7978845608028654 * (x + 0.044715 * x * x * x))
    return 0.5 * x * (1.0 + t)


def _gelu_grad(x):
    t = jnp.tanh(0.7978845608028654 * (x + 0.044715 * x * x * x))
    return 0.5 * (1.0 + t) + 0.5 * x * (1.0 - t * t) * 0.7978845608028654 * (1.0 + 0.134145 * x * x)


def _call_1d(body, *, name, grid, in_specs, out_specs, out_shape, scratch_shapes, vmem, args, copies=None):
    n_i, n_o, n_s = len(in_specs), len(out_specs), len(scratch_shapes)
    if copies is None:
        out = pl.pallas_call(
            body, name=name, grid=grid, in_specs=in_specs, out_specs=out_specs, out_shape=out_shape,
            scratch_shapes=scratch_shapes, compiler_params=_cparams(("parallel",), vmem))(*args)
        return list(out), []
    n_ci, n_co = len(copies.inputs), len(copies.out_shape)

    def hosted(*refs):
        ins, cin = refs[:n_i], refs[n_i:n_i + n_ci]
        outs = refs[n_i + n_ci:n_i + n_ci + n_o]
        cout = refs[n_i + n_ci + n_o:n_i + n_ci + n_o + n_co]
        scr = refs[n_i + n_ci + n_o + n_co:n_i + n_ci + n_o + n_co + n_s]
        sems = refs[n_i + n_ci + n_o + n_co + n_s:]
        copies.emit(cin, cout, sems, pl.program_id(0), grid[0])
        body(*ins, *outs, *scr)

    out = pl.pallas_call(
        hosted, name=name, grid=grid, in_specs=list(in_specs) + [HBM] * n_ci,
        out_specs=list(out_specs) + [HBM] * n_co, out_shape=list(out_shape) + list(copies.out_shape),
        scratch_shapes=list(scratch_shapes) + list(copies.scratch),
        compiler_params=pltpu.CompilerParams(dimension_semantics=("arbitrary",), vmem_limit_bytes=vmem,
                                             has_side_effects=True))(*args, *copies.inputs)
    return list(out[:n_o]), list(out[n_o:])


def ssm_fwd(name, proj, wbr, wbi, zr, zi, wcr, wci, dskip, D, copies=None):
    L = proj.shape[0]
    DS = D // 4
    NC = DS // CHUNK_U
    uo = (5 * D // 2) // CHUNK_U
    ch = _pick(L, 256)

    def body(u_ref, wbr_ref, wbi_ref, zr_ref, zi_ref, wcr_ref, wci_ref, ds_ref,
             y_ref, hg_ref, xr_ref, xi_ref, sr, si, pr_s, pi_s):
        def fill(ci, carry):
            rows = pl.ds(pl.multiple_of(ci * ch, ch), ch)
            ub = u_ref[rows, :].astype(BF16)
            sr[rows, :] = _dot(ub, wbr_ref[...])
            si[rows, :] = _dot(ub, wbi_ref[...])
            return carry

        lax.fori_loop(0, L // ch, fill, 0)
        _power_table(pr_s, pi_s, zr_ref[...], zi_ref[...], L, reverse=False)
        _scan(sr, si, pr_s, pi_s, L, reverse=False)

        def emit(ci, carry):
            rows = pl.ds(pl.multiple_of(ci * ch, ch), ch)
            xrb, xib = sr[rows, :].astype(BF16), si[rows, :].astype(BF16)
            xr_ref[rows, :] = xrb
            xi_ref[rows, :] = xib
            y = _dot(xrb, wcr_ref[...]) - _dot(xib, wci_ref[...]) + ds_ref[...] * u_ref[rows, :]
            y_ref[rows, :] = y
            hg_ref[rows, :] = _gelu(y).astype(BF16)
            return carry

        lax.fori_loop(0, L // ch, emit, 0)

    ucol = pl.BlockSpec((L, CHUNK_U), lambda k: (0, k))
    xcol = pl.BlockSpec((L, CHUNK_X), lambda k: (0, k))
    sd = jax.ShapeDtypeStruct
    return _call_1d(
        body, name=name, grid=(NC,),
        in_specs=[pl.BlockSpec((L, CHUNK_U), lambda k: (0, uo + k)),
                  pl.BlockSpec((None, CHUNK_U, CHUNK_X), lambda k: (k, 0, 0)),
                  pl.BlockSpec((None, CHUNK_U, CHUNK_X), lambda k: (k, 0, 0)),
                  pl.BlockSpec((1, CHUNK_X), lambda k: (0, k)), pl.BlockSpec((1, CHUNK_X), lambda k: (0, k)),
                  pl.BlockSpec((None, CHUNK_X, CHUNK_U), lambda k: (k, 0, 0)),
                  pl.BlockSpec((None, CHUNK_X, CHUNK_U), lambda k: (k, 0, 0)),
                  pl.BlockSpec((1, CHUNK_U), lambda k: (0, k))],
        out_specs=[ucol, ucol, xcol, xcol],
        out_shape=[sd((L, DS), F32), sd((L, DS), BF16), sd((L, 4 * DS), BF16), sd((L, 4 * DS), BF16)],
        scratch_shapes=[pltpu.VMEM((L, CHUNK_X), F32), pltpu.VMEM((L, CHUNK_X), F32),
                        pltpu.VMEM((_scan_rows(L), CHUNK_X), F32), pltpu.VMEM((_scan_rows(L), CHUNK_X), F32)],
        vmem=VMEM_BIG, args=(proj, wbr, wbi, zr, zi, wcr, wci, dskip), copies=copies)


def ssm_bwd(name, dhg, ypre, proj, xr, xi, wbr, wbi, zr, zi, wcr, wci, dskip, D, copies=None):
    L = proj.shape[0]
    DS = D // 4
    NC = DS // CHUNK_U
    uo = (5 * D // 2) // CHUNK_U
    ch = _pick(L, 256)
    nch = L // ch
    halo = 16

    def body(dhg_ref, y_ref, u_ref, xr_ref, xi_ref, wbr_ref, wbi_ref, zr_ref, zi_ref, wcr_ref, wci_ref,
             ds_ref, du_ref, dwcr_ref, dwci_ref, dwbr_ref, dwbi_ref, dar_ref, dai_ref, dds_ref,
             gr, gi, duf, pr_s, pi_s):
        dwcr_ref[...] = jnp.zeros_like(dwcr_ref)
        dwci_ref[...] = jnp.zeros_like(dwci_ref)
        dwbr_ref[...] = jnp.zeros_like(dwbr_ref)
        dwbi_ref[...] = jnp.zeros_like(dwbi_ref)
        dar_ref[...] = jnp.zeros_like(dar_ref)
        dai_ref[...] = jnp.zeros_like(dai_ref)
        dds_ref[...] = jnp.zeros_like(dds_ref)

        def first(ci, carry):
            rows = pl.ds(pl.multiple_of(ci * ch, ch), ch)
            dy = dhg_ref[rows, :] * _gelu_grad(y_ref[rows, :])
            dyb = dy.astype(BF16)
            dds_ref[...] += jnp.sum(dy * u_ref[rows, :], axis=0, keepdims=True)
            duf[rows, :] = ds_ref[...] * dy
            gr[rows, :] = _dot(dyb, wcr_ref[...], NT)
            gi[rows, :] = -_dot(dyb, wci_ref[...], NT)
            dwcr_ref[...] += _dot(xr_ref[rows, :], dyb, TN)
            dwci_ref[...] -= _dot(xi_ref[rows, :], dyb, TN)
            return carry

        lax.fori_loop(0, nch, first, 0)
        _power_table(pr_s, pi_s, zr_ref[...], -zi_ref[...], L, reverse=True)
        _scan(gr, gi, pr_s, pi_s, L, reverse=True)

        def lam_grad(gxr, gxi, xpr, xpi):
            pr, pi = _cmul_conj(xpr, xpi, gxr, gxi)
            dar_ref[...] += jnp.sum(pr, axis=0, keepdims=True)
            dai_ref[...] += jnp.sum(pi, axis=0, keepdims=True)

        def second(ci, carry):
            r0 = pl.multiple_of(ci * ch, ch)
            rows = pl.ds(r0, ch)
            gxr, gxi = gr[rows, :], gi[rows, :]
            gxrb, gxib = gxr.astype(BF16), gxi.astype(BF16)
            du_ref[rows, :] = (duf[rows, :] + _dot(gxrb, wbr_ref[...], NT) + _dot(gxib, wbi_ref[...], NT)).astype(BF16)
            ub = u_ref[rows, :].astype(BF16)
            dwbr_ref[...] += _dot(ub, gxrb, TN)
            dwbi_ref[...] += _dot(ub, gxib, TN)
            return carry

        lax.fori_loop(0, nch, second, 0)

        ridx = lax.broadcasted_iota(jnp.int32, (ch, CHUNK_X), 0)
        xpr = jnp.where(ridx >= 1, pltpu.roll(xr_ref[0:ch, :].astype(F32), 1, 0), 0.0)
        xpi = jnp.where(ridx >= 1, pltpu.roll(xi_ref[0:ch, :].astype(F32), 1, 0), 0.0)
        lam_grad(gr[0:ch, :], gi[0:ch, :], xpr, xpi)

        def third(ci, carry):
            r0 = pl.multiple_of(ci * ch, ch)
            ext = pl.ds(pl.multiple_of(r0 - halo, halo), ch + halo)
            xpr = pltpu.roll(xr_ref[ext, :].astype(F32), 1, 0)[halo:]
            xpi = pltpu.roll(xi_ref[ext, :].astype(F32), 1, 0)[halo:]
            lam_grad(gr[pl.ds(r0, ch), :], gi[pl.ds(r0, ch), :], xpr, xpi)
            return carry

        if nch > 1:
            lax.fori_loop(1, nch, third, 0)

    ucol = pl.BlockSpec((L, CHUNK_U), lambda k: (0, k))
    xcol = pl.BlockSpec((L, CHUNK_X), lambda k: (0, k))
    wb_spec = pl.BlockSpec((None, CHUNK_U, CHUNK_X), lambda k: (k, 0, 0))
    wc_spec = pl.BlockSpec((None, CHUNK_X, CHUNK_U), lambda k: (k, 0, 0))
    avec = pl.BlockSpec((1, CHUNK_X), lambda k: (0, k))
    uvec = pl.BlockSpec((1, CHUNK_U), lambda k: (0, k))
    sd = jax.ShapeDtypeStruct
    return _call_1d(
        body, name=name, grid=(NC,),
        in_specs=[ucol, ucol, pl.BlockSpec((L, CHUNK_U), lambda k: (0, uo + k)), xcol, xcol,
                  wb_spec, wb_spec, avec, avec, wc_spec, wc_spec, uvec],
        out_specs=[ucol, wc_spec, wc_spec, wb_spec, wb_spec, avec, avec, uvec],
        out_shape=[sd((L, DS), BF16), sd((NC, CHUNK_X, CHUNK_U), F32), sd((NC, CHUNK_X, CHUNK_U), F32),
                   sd((NC, CHUNK_U, CHUNK_X), F32), sd((NC, CHUNK_U, CHUNK_X), F32),
                   sd((1, 4 * DS), F32), sd((1, 4 * DS), F32), sd((1, DS), F32)],
        scratch_shapes=[pltpu.VMEM((L, CHUNK_X), F32), pltpu.VMEM((L, CHUNK_X), F32), pltpu.VMEM((L, CHUNK_U), F32),
                        pltpu.VMEM((_scan_rows(L), CHUNK_X), F32), pltpu.VMEM((_scan_rows(L), CHUNK_X), F32)],
        vmem=VMEM_BIG, args=(dhg, ypre, proj, xr, xi, wbr, wbi, zr, zi, wcr, wci, dskip), copies=copies)


def _block_diag(w, transpose):
    G = w.shape[0]
    nc = G // GROUPS_PER_CHUNK
    w4 = w.reshape(nc, GROUPS_PER_CHUNK, SSM_GROUP, SSM_STATE)
    eye = jnp.eye(GROUPS_PER_CHUNK, dtype=w.dtype)
    if transpose:
        return (w4[:, None, :, :, :].transpose(0, 1, 4, 2, 3) * eye[None, :, None, :, None]).reshape(
            nc, CHUNK_X, CHUNK_U).astype(BF16)
    return (w4[:, :, :, None, :] * eye[None, :, None, :, None]).reshape(nc, CHUNK_U, CHUNK_X).astype(BF16)


def _diag_blocks(dw, transpose):
    nc = dw.shape[0]
    gpc = GROUPS_PER_CHUNK
    if transpose:
        d5 = dw.reshape(nc, gpc, SSM_STATE, gpc, SSM_GROUP)
        blocks = [d5[:, g, :, g, :] for g in range(gpc)]
        return jnp.stack(blocks, axis=1).transpose(0, 1, 3, 2).reshape(nc * gpc, SSM_GROUP, SSM_STATE)
    d5 = dw.reshape(nc, gpc, SSM_GROUP, gpc, SSM_STATE)
    blocks = [d5[:, g, :, g, :] for g in range(gpc)]
    return jnp.stack(blocks, axis=1).reshape(nc * gpc, SSM_GROUP, SSM_STATE)


SHARD_BLOCK_ELEMS = 128 * 1024


def _shard_rows(R, C, scale):
    return _pick(R, max(8, scale * SHARD_BLOCK_ELEMS // C))


def cast_bf16(name, w, layer):
    shape = w.shape[1:]
    w3 = w.reshape(w.shape[0], -1, shape[-1])
    _, R, C = w3.shape
    tr = _shard_rows(R, C, 4)

    def body(w_ref, o_ref):
        o_ref[...] = w_ref[...].astype(BF16)

    out = pl.pallas_call(body, name=name, grid=(R // tr,),
                         in_specs=[pl.BlockSpec((None, tr, C), lambda i: (layer, i, 0))],
                         out_specs=pl.BlockSpec((tr, C), lambda i: (i, 0)),
                         out_shape=jax.ShapeDtypeStruct((R, C), BF16),
                         compiler_params=_cparams(("parallel",), VMEM_MID))(w3)
    return out.reshape(shape)


def _adamw(w, g, m, v):
    m = ADAM_B1 * m + (1.0 - ADAM_B1) * g
    v = ADAM_B2 * v + (1.0 - ADAM_B2) * (g * g)
    delta = -ADAM_LR * ((m * ADAM_C1) / (jnp.sqrt(v * ADAM_C2) + ADAM_EPS) + ADAM_WD * w)
    return delta, m, v


def _own_core(g4):
    return (lambda p: p[0]) if g4.shape[1] == 2 else (lambda p: 0)


def chip_partial(name, pos, g4, recv_a):
    _, _, R, C = g4.shape
    tr = _shard_rows(R, C, 4)
    core = _own_core(g4)

    def body(pos_ref, g_ref, a_ref, o_ref):
        o_ref[...] = (g_ref[...] + a_ref[...]).astype(BF16)

    return pl.pallas_call(
        body, name=name,
        grid_spec=pltpu.PrefetchScalarGridSpec(
            num_scalar_prefetch=1, grid=(4, R // tr),
            in_specs=[pl.BlockSpec((None, None, tr, C), lambda q, i, p: (q, core(p), i, 0)),
                      pl.BlockSpec((None, tr, C), lambda q, i, p: (q, i, 0))],
            out_specs=pl.BlockSpec((None, tr, C), lambda q, i, p: (q, i, 0))),
        out_shape=jax.ShapeDtypeStruct((4, R, C), BF16),
        compiler_params=_cparams(("parallel", "parallel"), VMEM_MID))(pos, g4, recv_a)


def adamw_shard(name, pos, layer, g4, recv_a, recv_b, w, m, v, prev):
    _, _, R, C = g4.shape
    tr = _shard_rows(R, C, 1)
    n_prev = 0 if prev is None else 4
    core = _own_core(g4)

    def body(pos_ref, g_ref, a_ref, b_ref, w_ref, m_ref, v_ref, *rest):
        go_ref, d_ref, mo_ref, vo_ref = rest[n_prev:]
        gs = g_ref[...] + a_ref[...]
        for j in range(3):
            gs = gs + b_ref[j].astype(F32)
        delta, mn, vn = _adamw(w_ref[...], gs, m_ref[...], v_ref[...])
        go_ref[...] = gs
        d_ref[...] = delta
        mo_ref[...] = mn
        vo_ref[...] = vn

    lay = pl.BlockSpec((None, tr, C), lambda i, p: (layer, i, 0))
    in_specs = [pl.BlockSpec((None, None, tr, C), lambda i, p: (p[1], core(p), i, 0)),
                pl.BlockSpec((None, tr, C), lambda i, p: (p[1], i, 0)),
                pl.BlockSpec((3, tr, C), lambda i, p: (0, i, 0)), lay, lay, lay]
    args = [g4, recv_a, recv_b, w, m, v]
    aliases = {}
    if prev is not None:
        in_specs += [pl.BlockSpec(memory_space=pl.ANY)] * 4
        args += list(prev)
        aliases = {7 + j: j for j in range(4)}
    return pl.pallas_call(
        body, name=name,
        grid_spec=pltpu.PrefetchScalarGridSpec(
            num_scalar_prefetch=1, grid=(R // tr,), in_specs=in_specs, out_specs=[lay] * 4),
        out_shape=[jax.ShapeDtypeStruct(w.shape, F32)] * 4,
        input_output_aliases=aliases,
        compiler_params=_cparams(("parallel",), VMEM_MID))(pos, *args)


def adamw_small(name, gathered, w, m, v):
    _, R, C = gathered.shape
    tr = _pick(R, 512)

    def body(g_ref, w_ref, m_ref, v_ref, go_ref, d_ref, mo_ref, vo_ref):
        gs = g_ref[0]
        for j in range(1, N_DEV):
            gs = gs + g_ref[j]
        delta, mn, vn = _adamw(w_ref[...], gs, m_ref[...], v_ref[...])
        go_ref[...] = gs
        d_ref[...] = delta
        mo_ref[...] = mn
        vo_ref[...] = vn

    spec = pl.BlockSpec((tr, C), lambda i: (i, 0))
    return pl.pallas_call(
        body, name=name, grid=(R // tr,),
        in_specs=[pl.BlockSpec((N_DEV, tr, C), lambda i: (0, i, 0)), spec, spec, spec], out_specs=[spec] * 4,
        out_shape=[jax.ShapeDtypeStruct((R, C), F32)] * 4,
        compiler_params=_cparams(("parallel",), VMEM_MID))(gathered, w, m, v)


def _position():
    return lax.axis_index("x"), lax.axis_index("y"), lax.axis_index("c")


FORWARD_AT = 0.7


def gather_copies(shards):
    n = len(shards)

    def parts(ins, outs, sems):
        send_sems, recv_sems, local_sems = sems
        x, y, c = _position()
        me, sibling = (x, y, c), (x, y, 1 - c)
        chips = [(1 - x, y), (x, 1 - y), (1 - x, 1 - y)]

        def copy(a, k, block, to, src=None):
            blk = outs[a].at[4 * block[0] + 2 * block[1] + block[2]]
            return pltpu.make_async_remote_copy(
                src_ref=blk if src is None else src, dst_ref=blk,
                send_sem=send_sems.at[a, k], recv_sem=recv_sems.at[a, k], device_id=to, device_id_type=MESH)

        mine = [pltpu.make_async_copy(ins[a], outs[a].at[4 * x + 2 * y + c], local_sems.at[a]) for a in range(n)]
        first = [[copy(a, 0, me, sibling, src=ins[a])] +
                 [copy(a, 1 + j, me, (*chip, c), src=ins[a]) for j, chip in enumerate(chips)] for a in range(n)]
        landed = [[copy(a, 1 + j, (*chip, c), me) for j, chip in enumerate(chips)] for a in range(n)]
        passed = [[copy(a, 4 + j, (*chip, c), sibling) for j, chip in enumerate(chips)] for a in range(n)]
        from_sibling = [[copy(a, 0, sibling, me)] +
                        [copy(a, 4 + j, (*chip, 1 - c), me) for j, chip in enumerate(chips)] for a in range(n)]
        return mine, first, landed, passed, from_sibling

    def start(ins, outs, sems):
        mine, first, _, _, _ = parts(ins, outs, sems)
        for a in range(n):
            mine[a].start()
            for cp in first[a]:
                cp.start()

    def forward(ins, outs, sems):
        _, _, landed, passed, _ = parts(ins, outs, sems)
        for a in range(n):
            for j in range(3):
                landed[a][j].wait_recv()
                passed[a][j].start()

    def finish(ins, outs, sems):
        mine, first, _, passed, from_sibling = parts(ins, outs, sems)
        for a in range(n):
            for cp in from_sibling[a]:
                cp.wait_recv()
        for a in range(n):
            for cp in first[a] + passed[a]:
                cp.wait_send()
            mine[a].wait()

    return HostedCopies(
        list(shards), [jax.ShapeDtypeStruct((N_DEV,) + s.shape, s.dtype) for s in shards],
        [pltpu.SemaphoreType.DMA((n, 7)), pltpu.SemaphoreType.DMA((n, 7)), pltpu.SemaphoreType.DMA((n,))],
        [(0.0, start), (FORWARD_AT, forward), (1.0, finish)])


def _exchange_copies(arrays, out_lead, make):
    n = len(arrays)

    def all_copies(ins, outs, sems):
        send_sems, recv_sems = sems
        return [make(ins[a], outs[a], send_sems.at[a, k], recv_sems.at[a, k], k)
                for a in range(n) for k in range(out_lead)]

    def start(ins, outs, sems):
        for cp in all_copies(ins, outs, sems):
            cp.start()

    def finish(ins, outs, sems):
        for cp in all_copies(ins, outs, sems):
            cp.wait()

    return HostedCopies(
        list(arrays), [jax.ShapeDtypeStruct((out_lead,) + a.shape[2:], a.dtype) for a in arrays],
        [pltpu.SemaphoreType.DMA((n, out_lead)), pltpu.SemaphoreType.DMA((n, out_lead))],
        [(0.0, start), (1.0, finish)])


def sibling_copies(grads):
    def make(src, dst, send_sem, recv_sem, q):
        x, y, c = _position()
        core = 1 - c if src.shape[1] == 2 else 0
        return pltpu.make_async_remote_copy(
            src_ref=src.at[q, core], dst_ref=dst.at[q], send_sem=send_sem, recv_sem=recv_sem,
            device_id=(x, y, 1 - c), device_id_type=MESH)

    return _exchange_copies(grads, 4, make)


def chip_copies(parts):
    def make(src, dst, send_sem, recv_sem, j):
        x, y, c = _position()
        chip = [(1 - x, y), (x, 1 - y), (1 - x, 1 - y)][j]
        return pltpu.make_async_remote_copy(
            src_ref=src.at[2 * chip[0] + chip[1], 0], dst_ref=dst.at[j], send_sem=send_sem, recv_sem=recv_sem,
            device_id=(*chip, c), device_id_type=MESH)

    return _exchange_copies(parts, 3, make)


class Carrier:
    def __init__(self):
        self.plan = {}

    def ride(self, site, make, store):
        self.plan[site] = (make, store)

    def make(self, site, ctx=None):
        return self.plan[site][0](ctx) if site in self.plan else None

    def store(self, site, results):
        if site in self.plan:
            self.plan[site][1](results)

    def split(self, site, out):
        if site not in self.plan:
            return out
        self.plan[site][1](out[1])
        return out[0]


def _pool_weight(gathered):
    PG = gathered.shape[-1]
    return gathered.transpose(1, 0, 2, 3).reshape(N_POOL_GROUPS, PG, PG)


def _layer_params(l, ln_g, pool_scale, lam_re, lam_im, log_dt, b_re, b_im, c_re, c_im,
                  d_skip, b_glu, branch_g):
    G, P = lam_re.shape[1:]
    p = dict(
        ln_g=ln_g[l][None, :], pool_scale=pool_scale[l][None, :], d_skip=d_skip[l][None, :],
        b_glu=b_glu[l][None, :], branch_g=branch_g[l][None, :],
        lr=lam_re[l].reshape(G, 1, P), li=lam_im[l].reshape(G, 1, P), ld=log_dt[l].reshape(G, 1, 1),
        br=b_re[l].transpose(0, 2, 1), bi=b_im[l].transpose(0, 2, 1), cr=c_re[l], ci=c_im[l])
    return p


def layer_fwd(l, x, p, gw, D, carrier):
    t = f"l{l}_"
    h = rms_fwd(t + "rms_fwd", x, p["ln_g"])
    site = (l, "proj")
    proj = carrier.split(site, mm_nn_gathered(t + "proj", h, gw["w_in", l], copies=carrier.make(site)))
    wp = _pool_weight(gw["w_pool", l])
    ypool = pool_fwd(t + "pool_fwd", proj, wp, p["pool_scale"])
    yattn = attn_fwd(t + "attn_fwd", proj, D)
    zr, zi, bbr, bbi = ssm_prep(t + "ssm_prep", p["lr"], p["li"], p["ld"], p["br"], p["bi"])
    ssm_w = dict(wbr=_block_diag(bbr, False), wbi=_block_diag(bbi, False),
                 zr=zr.reshape(1, -1), zi=zi.reshape(1, -1),
                 wcr=_block_diag(p["cr"], True), wci=_block_diag(p["ci"], True))
    site = (l, "ssm_fwd")
    (ypre, hg, xr, xi), landed = ssm_fwd(
        t + "ssm_fwd", proj, ssm_w["wbr"], ssm_w["wbi"], ssm_w["zr"], ssm_w["zi"],
        ssm_w["wcr"], ssm_w["wci"], p["d_skip"], D, copies=carrier.make(site))
    carrier.store(site, landed)
    glu_pre = mm_nn_gathered(t + "glu", hg, gw["w_glu", l])
    y = branch_fwd(t + "branch_fwd", ypool, yattn, glu_pre, proj, p["b_glu"], p["branch_g"])
    out = mm_plain(t + "out", y, gw["w_out", l].reshape(D, D), NN, res=x)
    saved = dict(x=x, h=h, proj=proj, ypool=ypool, yattn=yattn, ypre=ypre, hg=hg, xr=xr, xi=xi,
                 glu_pre=glu_pre, y=y, ssm_w=ssm_w, wp=wp)
    return out, saved


def layer_bwd(l, dres, dres_b, s, p, gw, D, carrier, pos, split_w_in):
    t = f"l{l}_"
    proj = s["proj"]
    w_out_g = gw["w_out", l].reshape(D, D)
    site = (l, "dy")
    dy = carrier.split(site, mm_plain(t + "dy", dres_b, w_out_g, NT, copies=carrier.make(site)))
    dw_out = mm_plain(t + "dw_out", s["y"], dres_b, TN)
    dypool, dyattn, dglu, dpg, dag, dsg, dbg, dbglu = branch_bwd(
        t + "branch_bwd", dy, s["ypool"], s["yattn"], s["glu_pre"], proj, p["b_glu"], p["branch_g"])
    dhg = mm_nt_gathered(t + "dhg", dglu, gw["w_glu", l])
    dw_glu = mm_tn_scattered(t + "dw_glu", s["hg"], dglu)
    w = s["ssm_w"]
    site = (l, "ssm_bwd")
    (du, dwcr, dwci, dwbr, dwbi, dar, dai, dds), landed = ssm_bwd(
        t + "ssm_bwd", dhg, s["ypre"], proj, s["xr"], s["xi"], w["wbr"], w["wbi"], w["zr"], w["zi"],
        w["wcr"], w["wci"], p["d_skip"], D, copies=carrier.make(site))
    carrier.store(site, landed)
    G, _, P = p["lr"].shape
    dlr, dli, dld, dbr, dbi = ssm_prep_bwd(
        t + "ssm_prep_bwd", p["lr"], p["li"], p["ld"], p["br"], p["bi"],
        dar.reshape(G, 1, P), dai.reshape(G, 1, P), _diag_blocks(dwbr, False), _diag_blocks(dwbi, False))
    dq, dk, dv = attn_bwd(t + "attn_bwd", proj, s["yattn"], dyattn, D)
    dxp, dwp, dps = pool_bwd(t + "pool_bwd", dypool, proj, s["wp"], p["pool_scale"])
    dproj = jnp.concatenate([dxp, dpg, dq, dk, dv, dag, du, dsg], axis=1)
    PG = dwp.shape[1]
    dwp_s = dwp.reshape(N_POOL_GROUPS, N_DEV, PG // N_DEV, PG).transpose(1, 0, 2, 3)

    def by_target(g):
        return g.reshape(4, 2, -1, g.shape[-1])

    big = dict(w_out=by_target(dw_out.reshape(N_DEV, D // N_DEV, D)), w_glu=by_target(dw_glu),
               w_pool=by_target(dwp_s))
    if split_w_in:
        site = (l, "dw_in_a")
        to_sibling = carrier.split(site, mm_tn_half(t + "dw_in_a", s["h"], dproj, pos, False,
                                                    copies=carrier.make(site, big)))
        site = (l, "dw_in_b")
        mine = carrier.split(site, mm_tn_half(t + "dw_in_b", s["h"], dproj, pos, True,
                                              copies=carrier.make(site, to_sibling[:, None])))
        big["w_in"] = mine[:, None]
    else:
        big["w_in"] = by_target(mm_tn_scattered(t + "dw_in", s["h"], dproj))
    site = (l, "dh")
    dh = carrier.split(site, mm_nt_gathered(t + "dh", dproj, gw["w_in", l], copies=carrier.make(site, big)))
    dx, dx_b, dlng = rms_bwd(t + "rms_bwd", s["x"], dh, dres, p["ln_g"])
    small = dict(ln_g=dlng[0], pool_scale=dps[0], lam_re=dlr.reshape(G, P), lam_im=dli.reshape(G, P),
                 log_dt=dld.reshape(G), b_re=dbr.transpose(0, 2, 1), b_im=dbi.transpose(0, 2, 1),
                 c_re=_diag_blocks(dwcr, True), c_im=_diag_blocks(dwci, True),
                 d_skip=dds[0], b_glu=dbglu[0], branch_g=dbg[0])
    return dx, dx_b, small


SMALL_NAMES = ("ln_g", "pool_scale", "lam_re", "lam_im", "log_dt", "b_re", "b_im", "c_re", "c_im",
               "d_skip", "b_glu", "branch_g", "final_g")
BIG_NAMES = ("w_in", "w_pool", "w_glu", "w_out")
WEIGHT_ORDER = ("ln_g", "w_in", "w_pool", "pool_scale", "lam_re", "lam_im", "log_dt", "b_re", "b_im",
                "c_re", "c_im", "d_skip", "w_glu", "b_glu", "branch_g", "w_out", "final_g")


PACK_ROWS = 512


def _pack(arrs):
    flat = jnp.concatenate([a.reshape(-1) for a in arrs])
    pad = (-flat.shape[0]) % (PACK_ROWS * LANE)
    return jnp.pad(flat, (0, pad)).reshape(-1, LANE)


def _unpack(packed, like):
    flat = packed.reshape(-1)
    out, off = [], 0
    for a in like:
        out.append(flat[off:off + a.size].reshape(a.shape))
        off += a.size
    return out


def kernel(x, ln_g, w_in, w_pool, pool_scale, lam_re, lam_im, log_dt, b_re, b_im, c_re, c_im, d_skip, w_glu, b_glu, branch_g, w_out, final_g, loss_target, m_ln_g, m_w_in, m_w_pool, m_pool_scale, m_lam_re, m_lam_im, m_log_dt, m_b_re, m_b_im, m_c_re, m_c_im, m_d_skip, m_w_glu, m_b_glu, m_branch_g, m_w_out, m_final_g, v_ln_g, v_w_in, v_w_pool, v_pool_scale, v_lam_re, v_lam_im, v_log_dt, v_b_re, v_b_im, v_c_re, v_c_im, v_d_skip, v_w_glu, v_b_glu, v_branch_g, v_w_out, v_final_g):
    W = dict(ln_g=ln_g, w_in=w_in, w_pool=w_pool, pool_scale=pool_scale, lam_re=lam_re, lam_im=lam_im,
             log_dt=log_dt, b_re=b_re, b_im=b_im, c_re=c_re, c_im=c_im, d_skip=d_skip, w_glu=w_glu,
             b_glu=b_glu, branch_g=branch_g, w_out=w_out, final_g=final_g)
    Mo = dict(ln_g=m_ln_g, w_in=m_w_in, w_pool=m_w_pool, pool_scale=m_pool_scale, lam_re=m_lam_re,
              lam_im=m_lam_im, log_dt=m_log_dt, b_re=m_b_re, b_im=m_b_im, c_re=m_c_re, c_im=m_c_im,
              d_skip=m_d_skip, w_glu=m_w_glu, b_glu=m_b_glu, branch_g=m_branch_g, w_out=m_w_out,
              final_g=m_final_g)
    Vo = dict(ln_g=v_ln_g, w_in=v_w_in, w_pool=v_w_pool, pool_scale=v_pool_scale, lam_re=v_lam_re,
              lam_im=v_lam_im, log_dt=v_log_dt, b_re=v_b_re, b_im=v_b_im, c_re=v_c_re, c_im=v_c_im,
              d_skip=v_d_skip, w_glu=v_w_glu, b_glu=v_b_glu, branch_g=v_branch_g, w_out=v_w_out,
              final_g=v_final_g)
    depth = ln_g.shape[0]
    _, L, D = x.shape
    xc, yc, cc = _position()
    pos = jnp.stack([cc, 2 * xc + yc, 4 * xc + 2 * yc + cc]).astype(jnp.int32)

    shards = {(n, l): cast_bf16(f"cast_{n}_{l}", W[n], l) for n in BIG_NAMES for l in range(depth)}
    gw = {}
    carrier = Carrier()

    def gather_plan(keys):
        return (lambda ctx: gather_copies([shards[k] for k in keys])), (lambda outs: gw.update(zip(keys, outs)))

    first = [("w_in", 0)] + [("w_pool", l) for l in range(depth)]
    gw.update(zip(first, copies_call("gather_first", gather_copies([shards[k] for k in first]))))
    carrier.ride((0, "proj"), *gather_plan([(n, l) for l in range(depth) for n in ("w_out", "w_glu")]))
    if depth > 1:
        carrier.ride((0, "ssm_fwd"), *gather_plan([("w_in", l) for l in range(1, depth)]))

    own, recv_a, recv_b = {}, {}, {}

    def sibling_plan(l, names, pick):
        def make(ctx):
            if isinstance(ctx, dict):
                own.update({(n, l): ctx[n] for n in names})
            return sibling_copies(pick(ctx))
        return make, (lambda outs: recv_a.update(zip([(n, l) for n in names], outs)))

    def chip_plan(l, names):
        def make(ctx):
            parts = [chip_partial(f"chip_partial_{n}_{l}", pos, own[n, l], recv_a[n, l])[:, None] for n in names]
            return chip_copies(parts)
        return make, (lambda outs: recv_b.update(zip([(n, l) for n in names], outs)))

    light = ("w_out", "w_glu", "w_pool")
    for l in range(1, depth):
        carrier.ride((l, "dh"), *sibling_plan(l, BIG_NAMES, lambda big: [big[n] for n in BIG_NAMES]))
        carrier.ride((l - 1, "dy"), *chip_plan(l, light))
        carrier.ride((l - 1, "ssm_bwd"), *chip_plan(l, ("w_in",)))
    carrier.ride((0, "dw_in_a"), *sibling_plan(0, light, lambda big: [big[n] for n in light]))
    carrier.ride((0, "dw_in_b"), *sibling_plan(0, ("w_in",), lambda to_sibling: [to_sibling]))

    def last_chip_make(big):
        own["w_in", 0] = big["w_in"]
        return chip_plan(0, BIG_NAMES)[0](big)

    carrier.ride((0, "dh"), last_chip_make, chip_plan(0, BIG_NAMES)[1])

    params = [_layer_params(l, ln_g, pool_scale, lam_re, lam_im, log_dt, b_re, b_im, c_re, c_im,
                            d_skip, b_glu, branch_g) for l in range(depth)]
    h = x[0]
    saved = []
    for l in range(depth):
        h, s = layer_fwd(l, h, params[l], gw, D, carrier)
        saved.append(s)
    loss_part, dres, dres_b, dfinal = loss_head("loss_head", h, final_g[None, :], loss_target[0])
    loss = lax.psum(loss_part[0, 0], ("x", "y", "c"))

    small = [None] * depth
    for l in reversed(range(depth)):
        dres, dres_b, small[l] = layer_bwd(l, dres, dres_b, saved[l], params[l], gw, D, carrier, pos,
                                           split_w_in=(l == 0))
    grad_x = dres[None]

    results = {}
    for n in BIG_NAMES:
        shape = W[n].shape
        R, C = int(math.prod(shape[1:-1])), shape[-1]
        w3, m3, v3 = (t.reshape(depth, R, C) for t in (W[n], Mo[n], Vo[n]))
        prev = None
        for l in range(depth):
            prev = adamw_shard(f"adamw_{n}_{l}", pos, l, own[n, l], recv_a[n, l], recv_b[n, l], w3, m3, v3, prev)
        results[n] = [t.reshape(shape) for t in prev]

    small_like = [W[n] for n in SMALL_NAMES]
    small_grads = [jnp.stack([small[l][n] for l in range(depth)]) for n in SMALL_NAMES[:-1]] + [dfinal[0]]
    gathered = copies_call("gather_small_grads", gather_copies([_pack(small_grads)]))[0]
    packed = adamw_small("adamw_small", gathered, _pack(small_like), _pack([Mo[n] for n in SMALL_NAMES]),
                         _pack([Vo[n] for n in SMALL_NAMES]))
    unpacked = [_unpack(t, small_like) for t in packed]
    for i, n in enumerate(SMALL_NAMES):
        results[n] = [unpacked[j][i] for j in range(4)]

    out = [loss, grad_x]
    for j in range(4):
        out += [results[n][j] for n in WEIGHT_ORDER]
    return tuple(out)
```

```python
import functools
import math

import jax
import jax.numpy as jnp
from jax import lax
from jax.experimental import pallas as pl
from jax.experimental.pallas import tpu as pltpu

F32 = jnp.float32
BF16 = jnp.bfloat16
MESH = pl.DeviceIdType.MESH

EPS = 1e-6
HEAD_DIM = 128
SSM_GROUP = 16
SSM_STATE = 64
GROUPS_PER_CHUNK = 8
CHUNK_U = GROUPS_PER_CHUNK * SSM_GROUP
CHUNK_X = GROUPS_PER_CHUNK * SSM_STATE
N_POOL_GROUPS = 4
POOL_HALO = 16
N_DEV = 8
LANE = 128
FULL_K = 4096
ATTN_TILE = 256
ATTN_DECAY_CUTOFF = 100.0
ROW_TILE = 128
VMEM_BIG = 58 * 1024 * 1024
VMEM_MID = 40 * 1024 * 1024

ADAM_LR = 0.001
ADAM_B1 = 0.9
ADAM_B2 = 0.999
ADAM_EPS = 1e-08
ADAM_WD = 0.01
ADAM_STEP = 10
ADAM_C1 = 1.0 / (1.0 - ADAM_B1 ** ADAM_STEP)
ADAM_C2 = 1.0 / (1.0 - ADAM_B2 ** ADAM_STEP)

NN = (((1,), (0,)), ((), ()))
NT = (((1,), (1,)), ((), ()))
TN = (((0,), (0,)), ((), ()))


def _pick(n, cap):
    if n <= cap:
        return n
    step = LANE if cap >= LANE else 8
    t = (cap // step) * step
    while t > step and n % t:
        t -= step
    assert n % t == 0, (n, cap)
    return t


def _cparams(sem, vmem=None):
    return pltpu.CompilerParams(dimension_semantics=sem, vmem_limit_bytes=vmem)


def _dot(a, b, dn=NN):
    return lax.dot_general(a, b, dn, preferred_element_type=F32)


def _sigmoid(x):
    e = jnp.exp(-jnp.abs(x))
    r = 1.0 / (1.0 + e)
    return jnp.where(x >= 0, r, e * r)


HBM = pl.BlockSpec(memory_space=pl.ANY)


class HostedCopies:
    def __init__(self, inputs, out_shape, scratch, phases):
        self.inputs, self.out_shape, self.scratch, self.phases = inputs, out_shape, scratch, phases

    def emit(self, ins, outs, sems, step, total):
        plan = {}
        for frac, fn in self.phases:
            plan.setdefault(min(total - 1, int(frac * total)), []).append(fn)
        for s in sorted(plan):
            def run(fns=plan[s]):
                for fn in fns:
                    fn(ins, outs, sems)
            if total == 1:
                run()
            else:
                pl.when(step == s)(run)


def copies_call(name, copies):
    n_i, n_o = len(copies.inputs), len(copies.out_shape)

    def body(*refs):
        copies.emit(refs[:n_i], refs[n_i:n_i + n_o], refs[n_i + n_o:], 0, 1)

    return pl.pallas_call(
        body, name=name, in_specs=[HBM] * n_i, out_specs=[HBM] * n_o, out_shape=copies.out_shape,
        scratch_shapes=copies.scratch, compiler_params=pltpu.CompilerParams(has_side_effects=True),
    )(*copies.inputs)


def _matmul(name, a, b, *, grid, a_spec, b_spec, o_spec, out_shape, dn,
            res=None, res_spec=None, pos=None, copies=None):
    ni, nj, nk = grid
    n_pos = 0 if pos is None else 1
    n_res = 0 if res is None else 1
    n_ci = 0 if copies is None else len(copies.inputs)
    n_co = 0 if copies is None else len(copies.out_shape)

    def body(*refs):
        refs = refs[n_pos:]
        a_ref, b_ref = refs[:2]
        r_ref = refs[2] if n_res else None
        base = 2 + n_res
        cin = refs[base:base + n_ci]
        o_ref = refs[base + n_ci]
        cout = refs[base + n_ci + 1:base + n_ci + 1 + n_co]
        sems = refs[base + n_ci + 1 + n_co:]
        k = pl.program_id(2)
        if copies is not None:
            step = (pl.program_id(0) * nj + pl.program_id(1)) * nk + k
            copies.emit(cin, cout, sems, step, ni * nj * nk)

        part = _dot(a_ref[...].astype(BF16), b_ref[...].astype(BF16), dn)
        if nk == 1:
            if r_ref is not None:
                part = part + r_ref[...]
            o_ref[...] = part.astype(o_ref.dtype)
        else:
            @pl.when(k == 0)
            def _():
                o_ref[...] = part if r_ref is None else part + r_ref[...]

            @pl.when(k > 0)
            def _():
                o_ref[...] += part

    assert nk == 1 or out_shape.dtype == F32
    in_specs = [a_spec, b_spec] + ([res_spec] if n_res else []) + [HBM] * n_ci
    args = ((pos,) if n_pos else ()) + (a, b) + ((res,) if n_res else ()) + tuple(copies.inputs if copies else ())
    out_specs = [o_spec] + [HBM] * n_co
    out_shapes = [out_shape] + list(copies.out_shape if copies else [])
    scratch = list(copies.scratch if copies else [])
    params = pltpu.CompilerParams(
        dimension_semantics=("arbitrary",) * 3 if copies else ("parallel", "parallel", "arbitrary"),
        vmem_limit_bytes=VMEM_BIG, has_side_effects=copies is not None)
    out = pl.pallas_call(
        body, name=name,
        grid_spec=pltpu.PrefetchScalarGridSpec(
            num_scalar_prefetch=n_pos, grid=grid, in_specs=in_specs, out_specs=out_specs, scratch_shapes=scratch),
        out_shape=out_shapes, compiler_params=params)(*args)
    return out[0] if copies is None else (out[0], list(out[1:]))


def mm_nn_gathered(name, a, wg, out_dtype=F32, copies=None):
    M, K = a.shape
    _, _, nper = wg.shape
    tm, tk, tn = _pick(M, 1024), _pick(K, FULL_K), _pick(nper, 768)
    r = nper // tn
    return _matmul(
        name, a, wg, grid=(M // tm, N_DEV * r, K // tk),
        a_spec=pl.BlockSpec((tm, tk), lambda i, j, k, *_: (i, k)),
        b_spec=pl.BlockSpec((None, tk, tn), lambda i, j, k, *_: (j // r, k, j % r)),
        o_spec=pl.BlockSpec((tm, tn), lambda i, j, k, *_: (i, j)),
        out_shape=jax.ShapeDtypeStruct((M, N_DEV * nper), out_dtype), dn=NN, copies=copies)


def mm_nt_gathered(name, a, wg, out_dtype=F32, copies=None):
    M, _ = a.shape
    _, N, nper = wg.shape
    tm, tn, tk = _pick(M, 1024), _pick(N, 1024), _pick(nper, 1536)
    r = nper // tk
    return _matmul(
        name, a, wg, grid=(M // tm, N // tn, N_DEV * r),
        a_spec=pl.BlockSpec((tm, tk), lambda i, j, k, *_: (i, k)),
        b_spec=pl.BlockSpec((None, tn, tk), lambda i, j, k, *_: (k // r, j, k % r)),
        o_spec=pl.BlockSpec((tm, tn), lambda i, j, k, *_: (i, j)),
        out_shape=jax.ShapeDtypeStruct((M, N), out_dtype), dn=NT, copies=copies)


def mm_tn_scattered(name, a, b, copies=None):
    L, M = a.shape
    nper = b.shape[1] // N_DEV
    tm, tn, tk = _pick(M, 1024), _pick(nper, 768), _pick(L, FULL_K)
    r = nper // tn
    return _matmul(
        name, a, b, grid=(M // tm, N_DEV * r, L // tk),
        a_spec=pl.BlockSpec((tk, tm), lambda i, j, k, *_: (k, i)),
        b_spec=pl.BlockSpec((tk, tn), lambda i, j, k, *_: (k, j)),
        o_spec=pl.BlockSpec((None, tm, tn), lambda i, j, k, *_: (j // r, i, j % r)),
        out_shape=jax.ShapeDtypeStruct((N_DEV, M, nper), F32), dn=TN, copies=copies)


def mm_tn_half(name, a, b, pos, own, copies=None):
    L, M = a.shape
    nper = b.shape[1] // N_DEV
    tm, tn, tk = _pick(M, 1024), _pick(nper, 768), _pick(L, FULL_K)
    r = nper // tn

    def b_map(i, j, k, p):
        core = p[0] if own else 1 - p[0]
        return (k, (2 * (j // r) + core) * r + j % r)

    return _matmul(
        name, a, b, grid=(M // tm, 4 * r, L // tk),
        a_spec=pl.BlockSpec((tk, tm), lambda i, j, k, *_: (k, i)),
        b_spec=pl.BlockSpec((tk, tn), b_map),
        o_spec=pl.BlockSpec((None, tm, tn), lambda i, j, k, *_: (j // r, i, j % r)),
        out_shape=jax.ShapeDtypeStruct((4, M, nper), F32), dn=TN, pos=pos, copies=copies)


def mm_plain(name, a, b, dn, out_dtype=F32, res=None, copies=None):
    if dn == NN:
        (M, K), N = a.shape, b.shape[1]
    elif dn == NT:
        (M, K), N = a.shape, b.shape[0]
    else:
        (K, M), N = a.shape, b.shape[1]
    tm, tn, tk = _pick(M, 1024), _pick(N, 512), _pick(K, FULL_K)
    a_spec =(pl.BlockSpec((tk, tm), lambda i, j, k, *_: (k, i)) if dn == TN
              else pl.BlockSpec((tm, tk), lambda i, j, k, *_: (i, k)))
    b_spec = (pl.BlockSpec((tn, tk), lambda i, j, k, *_: (j, k)) if dn == NT
              else pl.BlockSpec((tk, tn), lambda i, j, k, *_: (k, j)))
    o_spec = pl.BlockSpec((tm, tn), lambda i, j, k, *_: (i, j))
    return _matmul(
        name, a, b, grid=(M // tm, N // tn, K // tk), a_spec=a_spec, b_spec=b_spec, o_spec=o_spec,
        out_shape=jax.ShapeDtypeStruct((M, N), out_dtype), dn=dn,
        res=res, res_spec=o_spec if res is not None else None, copies=copies)


def rms_fwd(name, x, g):
    L, D = x.shape
    tr = _pick(L, ROW_TILE)

    def body(x_ref, g_ref, h_ref):
        xv = x_ref[...]
        r = lax.rsqrt(jnp.mean(xv * xv, axis=-1, keepdims=True) + EPS)
        h_ref[...] = (xv * r * g_ref[...]).astype(BF16)

    return pl.pallas_call(
        body, name=name, grid=(L // tr,),
        in_specs=[pl.BlockSpec((tr, D), lambda i: (i, 0)), pl.BlockSpec((1, D), lambda i: (0, 0))],
        out_specs=pl.BlockSpec((tr, D), lambda i: (i, 0)),
        out_shape=jax.ShapeDtypeStruct((L, D), BF16),
        compiler_params=_cparams(("parallel",), VMEM_MID))(x, g)


def rms_bwd(name, x, dh, dres, g):
    L, D = x.shape
    tr = _pick(L, ROW_TILE)

    def body(x_ref, dh_ref, dr_ref, g_ref, dx_ref, dxb_ref, dg_ref):
        xv = x_ref[...]
        r = lax.rsqrt(jnp.mean(xv * xv, axis=-1, keepdims=True) + EPS)
        xh = xv * r
        dhv = dh_ref[...]
        dn = dhv * g_ref[...]
        dxv = dr_ref[...] + r * (dn - xh * jnp.mean(dn * xh, axis=-1, keepdims=True))
        dx_ref[...] = dxv
        dxb_ref[...] = dxv.astype(BF16)

        @pl.when(pl.program_id(0) == 0)
        def _():
            dg_ref[...] = jnp.zeros_like(dg_ref)

        dg_ref[...] += jnp.sum(dhv * xh, axis=0, keepdims=True)

    row = pl.BlockSpec((tr, D), lambda i: (i, 0))
    vec = pl.BlockSpec((1, D), lambda i: (0, 0))
    return pl.pallas_call(
        body, name=name, grid=(L // tr,), in_specs=[row, row, row, vec], out_specs=[row, row, vec],
        out_shape=[jax.ShapeDtypeStruct((L, D), F32), jax.ShapeDtypeStruct((L, D), BF16),
                   jax.ShapeDtypeStruct((1, D), F32)],
        compiler_params=_cparams(("arbitrary",), VMEM_MID))(x, dh, dres, g)


def loss_head(name, x, g, target):
    L, D = x.shape
    tr = _pick(L, ROW_TILE)

    def body(x_ref, g_ref, t_ref, loss_ref, dx_ref, dxb_ref, dg_ref):
        xv = x_ref[...]
        gv = g_ref[...]
        r = lax.rsqrt(jnp.mean(xv * xv, axis=-1, keepdims=True) + EPS)
        xh = xv * r
        err = xh * gv - t_ref[...]
        dy = err * (1.0 / D)
        dn = dy * gv
        dxv = r * (dn - xh * jnp.mean(dn * xh, axis=-1, keepdims=True))
        dx_ref[...] = dxv
        dxb_ref[...] = dxv.astype(BF16)

        @pl.when(pl.program_id(0) == 0)
        def _():
            dg_ref[...] = jnp.zeros_like(dg_ref)
            loss_ref[...] = jnp.zeros_like(loss_ref)

        dg_ref[...] += jnp.sum(dy * xh, axis=0, keepdims=True)
        row_loss = jnp.sum(err * err, axis=-1, keepdims=True) * (0.5 / D)
        loss_ref[...] += jnp.sum(row_loss, axis=0, keepdims=True)

    row = pl.BlockSpec((tr, D), lambda i: (i, 0))
    vec = pl.BlockSpec((1, D), lambda i: (0, 0))
    one = pl.BlockSpec((1, 1), lambda i: (0, 0))
    return pl.pallas_call(
        body, name=name, grid=(L // tr,), in_specs=[row, vec, row], out_specs=[one, row, row, vec],
        out_shape=[jax.ShapeDtypeStruct((1, 1), F32), jax.ShapeDtypeStruct((L, D), F32),
                   jax.ShapeDtypeStruct((L, D), BF16), jax.ShapeDtypeStruct((1, D), F32)],
        compiler_params=_cparams(("arbitrary",), VMEM_MID))(x, g, target)


def _branch_specs(D, tr):
    DP, DA, DS = D // 4, D // 2, D // 4
    return dict(
        pool=pl.BlockSpec((tr, DP), lambda i: (i, 0)),
        attn=pl.BlockSpec((tr, DA), lambda i: (i, 0)),
        glu=pl.BlockSpec((tr, 2 * DS), lambda i: (i, 0)),
        p_gate=pl.BlockSpec((tr, DP), lambda i: (i, 1)),
        a_gate=pl.BlockSpec((tr, DA), lambda i: (i, 4)),
        s_gate=pl.BlockSpec((tr, DS), lambda i: (i, 11)),
        bglu=pl.BlockSpec((1, 2 * DS), lambda i: (0, 0)),
        bg=pl.BlockSpec((1, D), lambda i: (0, 0)),
        row=pl.BlockSpec((tr, D), lambda i: (i, 0)),
    )


def branch_fwd(name, ypool, yattn, glu_pre, proj, b_glu, branch_g):
    L, DP = ypool.shape
    D = 4 * DP
    DA, DS = D // 2, D // 4
    tr = _pick(L, ROW_TILE)
    s = _branch_specs(D, tr)

    def body(yp_ref, ya_ref, gl_ref, pg_ref, ag_ref, sg_ref, bgl_ref, bg_ref, y_ref):
        pre = gl_ref[...] + bgl_ref[...]
        ys = pre[:, :DS] * _sigmoid(pre[:, DS:])
        bg = bg_ref[...]

        def one(raw, gate, g):
            r = lax.rsqrt(jnp.mean(raw * raw, axis=-1, keepdims=True) + EPS)
            return raw * r * g * (gate * _sigmoid(gate))

        y_ref[:, :DP] = one(yp_ref[...], pg_ref[...], bg[:, :DP]).astype(BF16)
        y_ref[:, DP:DP + DA] = one(ya_ref[...], ag_ref[...], bg[:, DP:DP + DA]).astype(BF16)
        y_ref[:, DP + DA:] = one(ys, sg_ref[...], bg[:, DP + DA:]).astype(BF16)

    return pl.pallas_call(
        body, name=name, grid=(L // tr,),
        in_specs=[s["pool"], s["attn"], s["glu"], s["p_gate"], s["a_gate"], s["s_gate"], s["bglu"], s["bg"]],
        out_specs=s["row"], out_shape=jax.ShapeDtypeStruct((L, D), BF16),
        compiler_params=_cparams(("parallel",), VMEM_MID))(ypool, yattn, glu_pre, proj, proj, proj, b_glu, branch_g)


def branch_bwd(name, dy, ypool, yattn, glu_pre, proj, b_glu, branch_g):
    L, DP = ypool.shape
    D = 4 * DP
    DA, DS = D // 2, D // 4
    tr = _pick(L, ROW_TILE // 2)
    s = _branch_specs(D, tr)

    def body(dy_ref, yp_ref, ya_ref, gl_ref, pg_ref, ag_ref, sg_ref, bgl_ref, bg_ref,
             dyp_ref, dya_ref, dgl_ref, dpg_ref, dag_ref, dsg_ref, dbg_ref, dbgl_ref):
        @pl.when(pl.program_id(0) == 0)
        def _():
            dbg_ref[...] = jnp.zeros_like(dbg_ref)
            dbgl_ref[...] = jnp.zeros_like(dbgl_ref)

        bg = bg_ref[...]

        def one(raw, gate, g, dyb):
            r = lax.rsqrt(jnp.mean(raw * raw, axis=-1, keepdims=True) + EPS)
            n = raw * r
            sg = _sigmoid(gate)
            sl = gate * sg
            dgate = dyb * n * g * (sg * (1.0 + gate * (1.0 - sg)))
            dbg = jnp.sum(dyb * n * sl, axis=0, keepdims=True)
            dn = dyb * g * sl
            draw = r * (dn - n * jnp.mean(dn * n, axis=-1, keepdims=True))
            return draw, dgate, dbg

        draw, dgate, dbg = one(yp_ref[...], pg_ref[...], bg[:, :DP], dy_ref[:, :DP])
        dyp_ref[...] = draw
        dpg_ref[...] = dgate.astype(BF16)
        dbg_ref[:, :DP] += dbg

        draw, dgate, dbg = one(ya_ref[...], ag_ref[...], bg[:, DP:DP + DA], dy_ref[:, DP:DP + DA])
        dya_ref[...] = draw
        dag_ref[...] = dgate.astype(BF16)
        dbg_ref[:, DP:DP + DA] += dbg

        pre = gl_ref[...] + bgl_ref[...]
        val = pre[:, :DS]
        sgt = _sigmoid(pre[:, DS:])
        draw, dgate, dbg = one(val * sgt, sg_ref[...], bg[:, DP + DA:], dy_ref[:, DP + DA:])
        dsg_ref[...] = dgate.astype(BF16)
        dbg_ref[:, DP + DA:] += dbg
        dval = draw * sgt
        dgt = draw * val * sgt * (1.0 - sgt)
        dgl_ref[:, :DS] = dval.astype(BF16)
        dgl_ref[:, DS:] = dgt.astype(BF16)
        dbgl_ref[:, :DS] += jnp.sum(dval, axis=0, keepdims=True)
        dbgl_ref[:, DS:] += jnp.sum(dgt, axis=0, keepdims=True)

    loc = lambda w: pl.BlockSpec((tr, w), lambda i: (i, 0))
    return pl.pallas_call(
        body, name=name, grid=(L // tr,),
        in_specs=[s["row"], s["pool"], s["attn"], s["glu"], s["p_gate"], s["a_gate"], s["s_gate"], s["bglu"], s["bg"]],
        out_specs=[loc(DP), loc(DA), loc(2 * DS), loc(DP), loc(DA), loc(DS), s["bg"], s["bglu"]],
        out_shape=[jax.ShapeDtypeStruct((L, DP), F32), jax.ShapeDtypeStruct((L, DA), F32),
                   jax.ShapeDtypeStruct((L, 2 * DS), BF16), jax.ShapeDtypeStruct((L, DP), BF16),
                   jax.ShapeDtypeStruct((L, DA), BF16), jax.ShapeDtypeStruct((L, DS), BF16),
                   jax.ShapeDtypeStruct((1, D), F32), jax.ShapeDtypeStruct((1, 2 * DS), F32)],
        compiler_params=_cparams(("arbitrary",), VMEM_BIG),
    )(dy, ypool, yattn, glu_pre, proj, proj, proj, b_glu, branch_g)


def _pool_select(g, s2, s4, s8, s16):
    return jnp.where(g == 0, s2, jnp.where(g == 1, s4, jnp.where(g == 2, s8, s16)))


def _pool_window(g):
    return jnp.where(g == 0, 2.0, jnp.where(g == 1, 4.0, jnp.where(g == 2, 8.0, 16.0))).astype(F32)


def _pooled_chunk(pad, g, r0, ch):
    xh = pad[pl.ds(r0, ch + POOL_HALO), :]
    s2 = xh + pltpu.roll(xh, 1, 0)
    s4 = s2 + pltpu.roll(s2, 2, 0)
    s8 = s4 + pltpu.roll(s4, 4, 0)
    s16 = s8 + pltpu.roll(s8, 8, 0)
    win = _pool_select(g, s2, s4, s8, s16)[POOL_HALO:]
    pos = (r0 + 1 + lax.broadcasted_iota(jnp.int32, (ch, 1), 0)).astype(F32)
    return win / jnp.minimum(pos, _pool_window(g)) - xh[POOL_HALO:]


def pool_fwd(name, proj, wp, scale):
    L = proj.shape[0]
    DP = scale.shape[1]
    PG = DP // N_POOL_GROUPS
    ch = _pick(L, 256)

    def body(x_ref, w_ref, s_ref, o_ref, pad):
        g = pl.program_id(0)
        pad[0:POOL_HALO, :] = jnp.zeros((POOL_HALO, PG), F32)
        pad[POOL_HALO:, :] = x_ref[...]

        def chunk(ci, carry):
            r0 = pl.multiple_of(ci * ch, ch)
            pooled = _pooled_chunk(pad, g, r0, ch)
            o_ref[pl.ds(r0, ch), :] = _dot(pooled.astype(BF16), w_ref[...]) * s_ref[...]
            return carry

        lax.fori_loop(0, L // ch, chunk, 0)

    return pl.pallas_call(
        body, name=name, grid=(N_POOL_GROUPS,),
        in_specs=[pl.BlockSpec((L, PG), lambda g: (0, g)), pl.BlockSpec((None, PG, PG), lambda g: (g, 0, 0)),
                  pl.BlockSpec((1, PG), lambda g: (0, g))],
        out_specs=pl.BlockSpec((L, PG), lambda g: (0, g)),
        out_shape=jax.ShapeDtypeStruct((L, DP), F32),
        scratch_shapes=[pltpu.VMEM((L + POOL_HALO, PG), F32)],
        compiler_params=_cparams(("parallel",), VMEM_MID))(proj, wp, scale)


def pool_bwd(name, dyraw, proj, wp, scale):
    L = proj.shape[0]
    DP = scale.shape[1]
    PG = DP // N_POOL_GROUPS
    ch = _pick(L, 256)

    def body(dy_ref, x_ref, w_ref, s_ref, dx_ref, dw_ref, ds_ref, pad, dpad, dpo):
        g = pl.program_id(0)
        pad[0:POOL_HALO, :] = jnp.zeros((POOL_HALO, PG), F32)
        pad[POOL_HALO:, :] = x_ref[...]
        dpad[L:, :] = jnp.zeros((POOL_HALO, PG), F32)
        dw_ref[...] = jnp.zeros_like(dw_ref)
        ds_ref[...] = jnp.zeros_like(ds_ref)
        wv = w_ref[...]
        win_f = _pool_window(g)

        def chunk(ci, carry):
            r0 = pl.multiple_of(ci * ch, ch)
            pooled = _pooled_chunk(pad, g, r0, ch).astype(BF16)
            dyv = dy_ref[pl.ds(r0, ch), :]
            ds_ref[...] += jnp.sum(dyv * _dot(pooled, wv), axis=0, keepdims=True)
            dmixed = (dyv * s_ref[...]).astype(BF16)
            dw_ref[...] += _dot(pooled, dmixed, TN)
            dpooled = _dot(dmixed, wv, NT)
            pos = (r0 + 1 + lax.broadcasted_iota(jnp.int32, (ch, 1), 0)).astype(F32)
            dpad[pl.ds(r0, ch), :] = dpooled / jnp.minimum(pos, win_f)
            dpo[pl.ds(r0, ch), :] = dpooled
            return carry

        lax.fori_loop(0, L // ch, chunk, 0)

        def chunk2(ci, carry):
            r0 = pl.multiple_of(ci * ch, ch)
            n = ch + POOL_HALO
            dm = dpad[pl.ds(r0, n), :]
            s2 = dm + pltpu.roll(dm, n - 1, 0)
            s4 = s2 + pltpu.roll(s2, n - 2, 0)
            s8 = s4 + pltpu.roll(s4, n - 4, 0)
            s16 = s8 + pltpu.roll(s8, n - 8, 0)
            win = _pool_select(g, s2, s4, s8, s16)[:ch]
            dx_ref[pl.ds(r0, ch), :] = (win - dpo[pl.ds(r0, ch), :]).astype(BF16)
            return carry

        lax.fori_loop(0, L // ch, chunk2, 0)

    col = pl.BlockSpec((L, PG), lambda g: (0, g))
    return pl.pallas_call(
        body, name=name, grid=(N_POOL_GROUPS,),
        in_specs=[col, col, pl.BlockSpec((None, PG, PG), lambda g: (g, 0, 0)), pl.BlockSpec((1, PG), lambda g: (0, g))],
        out_specs=[col, pl.BlockSpec((None, PG, PG), lambda g: (g, 0, 0)), pl.BlockSpec((1, PG), lambda g: (0, g))],
        out_shape=[jax.ShapeDtypeStruct((L, DP), BF16), jax.ShapeDtypeStruct((N_POOL_GROUPS, PG, PG), F32),
                   jax.ShapeDtypeStruct((1, DP), F32)],
        scratch_shapes=[pltpu.VMEM((L + POOL_HALO, PG), F32), pltpu.VMEM((L + POOL_HALO, PG), F32),
                        pltpu.VMEM((L, PG), F32)],
        compiler_params=_cparams(("parallel",), VMEM_MID))(dyraw, proj, wp, scale)


def _attn_tile(L):
    return _pick(L, ATTN_TILE)


def _tri(t, strict):
    j = lax.broadcasted_iota(jnp.int32, (t, t), 0)
    s = lax.broadcasted_iota(jnp.int32, (t, t), 1)
    return ((j > s) if strict else (j >= s)).astype(BF16)


def _attn_block(q, kt, i, k0, rb, after):
    tq, tk = q.shape[0], kt.shape[0]
    row = lax.broadcasted_iota(jnp.int32, (tq, tk), 0)
    col = lax.broadcasted_iota(jnp.int32, (tq, tk), 1)
    causal = (k0 + col) < (i * tq + row)
    z = _dot(q, kt, NT)
    e = jnp.exp(-jnp.abs(z))
    l1p = jnp.log(1.0 + e)
    log_sig = jnp.minimum(z, 0.0) - l1p
    log_1m = -jnp.maximum(z, 0.0) - l1p
    b = jnp.where(causal, log_1m, 0.0)
    b_hi = b.astype(BF16)
    b_lo = (b - b_hi.astype(F32)).astype(BF16)
    suffix = _dot(b_hi, after) + _dot(b_lo, after) + rb
    w = jnp.where(causal, jnp.exp(log_sig + suffix), 0.0)
    return z, e, causal, b, w


def attn_fwd(name, proj, D, copies=None):
    L = proj.shape[0]
    DA = D // 2
    H = DA // HEAD_DIM
    tq = tk = _attn_tile(L)
    qo, ko, vo = (D // 2) // HEAD_DIM, D // HEAD_DIM, (3 * D // 2) // HEAD_DIM
    scale = HEAD_DIM ** -0.5

    def body(q_ref, k_ref, v_ref, tri_ref, o_ref, kb_s, vb_s, acc, rb):
        i = pl.program_id(1)

        @pl.when(i == 0)
        def _():
            kb_s[...] = k_ref[...].astype(BF16)
            vb_s[...] = v_ref[...].astype(BF16)

        q = (q_ref[...] * scale).astype(BF16)
        acc[...] = jnp.zeros_like(acc)
        rb[...] = jnp.zeros_like(rb)

        def cond(c):
            return jnp.logical_and(c[0] >= 0, c[1])

        def step(c):
            kb = c[0]
            k0 = pl.multiple_of(kb * tk, tk)
            kt = kb_s[pl.ds(k0, tk), :]
            vt = vb_s[pl.ds(k0, tk), :]
            _, _, _, b, w = _attn_block(q, kt, i, k0, rb[...], tri_ref[...])
            acc[...] += _dot(w.astype(BF16), vt)
            rbn = rb[...] + jnp.sum(b, axis=1, keepdims=True)
            rb[...] = rbn
            return kb - 1, jnp.max(rbn) > -ATTN_DECAY_CUTOFF

        lax.while_loop(cond, step, (i, jnp.bool_(True)))
        o_ref[...] = acc[...]

    (out,), landed = _call(
        body, name=name, grid=(H, L // tq),
        in_specs=[pl.BlockSpec((tq, HEAD_DIM), lambda h, i: (i, qo + h)),
                  pl.BlockSpec((L, HEAD_DIM), lambda h, i: (0, ko + h)),
                  pl.BlockSpec((L, HEAD_DIM), lambda h, i: (0, vo + h)),
                  pl.BlockSpec((tk, tk), lambda h, i: (0, 0))],
        out_specs=[pl.BlockSpec((tq, HEAD_DIM), lambda h, i: (i, h))],
        out_shape=[jax.ShapeDtypeStruct((L, DA), F32)],
        scratch_shapes=[pltpu.VMEM((L, HEAD_DIM), BF16), pltpu.VMEM((L, HEAD_DIM), BF16),
                        pltpu.VMEM((tq, HEAD_DIM), F32), pltpu.VMEM((tq, 1), F32)],
        vmem=VMEM_MID, args=(proj, proj, proj, _tri(tk, True)), semantics=("arbitrary", "arbitrary"),
        copies=copies)
    return out, landed


def attn_bwd(name, proj, o, do, D, copies=None):
    L = proj.shape[0]
    DA = D // 2
    H = DA // HEAD_DIM
    tq = tk = _attn_tile(L)
    qo, ko, vo = (D // 2) // HEAD_DIM, D // HEAD_DIM, (3 * D // 2) // HEAD_DIM
    scale = HEAD_DIM ** -0.5

    def body(q_ref, k_ref, v_ref, o_ref, do_ref, after_ref, from_ref, dq_ref, dk_ref, dv_ref,
             kb_s, vb_s, dk_s, dv_s, dq_acc, rb, rg):
        i = pl.program_id(1)
        nq = pl.num_programs(1)

        @pl.when(i == 0)
        def _():
            kb_s[...] = k_ref[...].astype(BF16)
            vb_s[...] = v_ref[...].astype(BF16)
            dk_s[...] = jnp.zeros_like(dk_s)
            dv_s[...] = jnp.zeros_like(dv_s)

        q = (q_ref[...] * scale).astype(BF16)
        dob = do_ref[...].astype(BF16)
        delta = jnp.sum(dob.astype(F32) * o_ref[...], axis=1, keepdims=True)
        dq_acc[...] = jnp.zeros_like(dq_acc)
        rb[...] = jnp.zeros_like(rb)
        rg[...] = jnp.zeros_like(rg)

        def cond(c):
            return jnp.logical_and(c[0] >= 0, c[1])

        def step(c):
            kb = c[0]
            k0 = pl.multiple_of(kb * tk, tk)
            kt = kb_s[pl.ds(k0, tk), :]
            vt = vb_s[pl.ds(k0, tk), :]
            z, e, causal, b, w = _attn_block(q, kt, i, k0, rb[...], after_ref[...])
            wq = w.astype(BF16)
            dw = _dot(dob, vt, NT)
            g = wq.astype(F32) * dw
            g_hi = g.astype(BF16)
            g_lo = (g - g_hi.astype(F32)).astype(BF16)
            from_s = from_ref[...]
            suffix_g = _dot(g_hi, from_s) + _dot(g_lo, from_s) + rg[...]
            before = delta - suffix_g
            r = 1.0 / (1.0 + e)
            sig = jnp.where(z >= 0, r, e * r)
            sig_neg = jnp.where(z >= 0, e * r, r)
            dz = jnp.where(causal, g * sig_neg - before * sig, 0.0).astype(BF16)
            dq_acc[...] += _dot(dz, kt)
            dk_s[pl.ds(k0, tk), :] += _dot(dz, q, TN)
            dv_s[pl.ds(k0, tk), :] += _dot(wq, dob, TN)
            rbn = rb[...] + jnp.sum(b, axis=1, keepdims=True)
            rb[...] = rbn
            rg[...] += jnp.sum(g, axis=1, keepdims=True)
            return kb - 1, jnp.max(rbn) > -ATTN_DECAY_CUTOFF

        lax.while_loop(cond, step, (i, jnp.bool_(True)))
        dq_ref[...] = (dq_acc[...] * scale).astype(BF16)

        @pl.when(i == nq - 1)
        def _():
            dk_ref[...] = dk_s[...].astype(BF16)
            dv_ref[...] = dv_s[...].astype(BF16)

    blk = pl.BlockSpec((tq, HEAD_DIM), lambda h, i: (i, h))
    full = pl.BlockSpec((L, HEAD_DIM), lambda h, i: (0, h))
    return _call(
        body, name=name, grid=(H, L // tq),
        in_specs=[pl.BlockSpec((tq, HEAD_DIM), lambda h, i: (i, qo + h)),
                  pl.BlockSpec((L, HEAD_DIM), lambda h, i: (0, ko + h)),
                  pl.BlockSpec((L, HEAD_DIM), lambda h, i: (0, vo + h)), blk, blk,
                  pl.BlockSpec((tk, tk), lambda h, i: (0, 0)), pl.BlockSpec((tk, tk), lambda h, i: (0, 0))],
        out_specs=[blk, full, full],
        out_shape=[jax.ShapeDtypeStruct((L, DA), BF16)] * 3,
        scratch_shapes=[pltpu.VMEM((L, HEAD_DIM), BF16), pltpu.VMEM((L, HEAD_DIM), BF16),
                        pltpu.VMEM((L, HEAD_DIM), F32), pltpu.VMEM((L, HEAD_DIM), F32),
                        pltpu.VMEM((tq, HEAD_DIM), F32), pltpu.VMEM((tq, 1), F32), pltpu.VMEM((tq, 1), F32)],
        vmem=VMEM_MID, args=(proj, proj, proj, o, do, _tri(tk, True), _tri(tk, False)),
        semantics=("arbitrary", "arbitrary"), copies=copies)


def _cmul(ar, ai, br, bi):
    return ar * br - ai * bi, ar * bi + ai * br


def _cmul_conj(ar, ai, br, bi):
    return ar * br + ai * bi, ar * bi - ai * br


def _ssm_disc(lr, li, ld):
    dt = jnp.exp(ld)
    m = jnp.exp(lr * dt)
    ar, ai = m * jnp.cos(li * dt), m * jnp.sin(li * dt)
    inv = 1.0 / (lr * lr + li * li)
    fr, fi = _cmul(ar - 1.0, ai, lr * inv, -li * inv)
    return dt, ar, ai, fr, fi, inv


def ssm_prep(name, lr, li, ld, br, bi):
    def body(lr_ref, li_ref, ld_ref, br_ref, bi_ref, zr_ref, zi_ref, bbr_ref, bbi_ref):
        dt, _, _, fr, fi, _ = _ssm_disc(lr_ref[...], li_ref[...], ld_ref[...])
        zr_ref[...] = lr_ref[...] * dt
        zi_ref[...] = li_ref[...] * dt
        bbr, bbi = _cmul(fr, fi, br_ref[...], bi_ref[...])
        bbr_ref[...] = bbr
        bbi_ref[...] = bbi

    sd = jax.ShapeDtypeStruct
    return pl.pallas_call(
        body, name=name,
        out_shape=[sd(lr.shape, F32), sd(lr.shape, F32), sd(br.shape, F32), sd(br.shape, F32)],
    )(lr, li, ld, br, bi)


def ssm_prep_bwd(name, lr, li, ld, br, bi, gar, gai, gbr, gbi):
    def body(lr_ref, li_ref, ld_ref, br_ref, bi_ref, gar_ref, gai_ref, gbr_ref, gbi_ref,
             dlr_ref, dli_ref, dld_ref, dbr_ref, dbi_ref):
        lr_, li_ = lr_ref[...], li_ref[...]
        dt, ar, ai, fr, fi, inv = _ssm_disc(lr_, li_, ld_ref[...])
        gbr_, gbi_ = gbr_ref[...], gbi_ref[...]
        dbr, dbi = _cmul_conj(fr, fi, gbr_, gbi_)
        dbr_ref[...] = dbr
        dbi_ref[...] = dbi
        pr, pi = _cmul_conj(br_ref[...], bi_ref[...], gbr_, gbi_)
        gfr = jnp.sum(pr, axis=1, keepdims=True)
        gfi = jnp.sum(pi, axis=1, keepdims=True)
        ilr, ili = lr_ * inv, -li_ * inv
        tr_, ti_ = _cmul_conj(ilr, ili, gfr, gfi)
        gatr, gati = gar_ref[...] + tr_, gai_ref[...] + ti_
        hr, hi = _cmul(fr, fi, ilr, ili)
        t1r, t1i = _cmul_conj(ar * dt, ai * dt, gatr, gati)
        t2r, t2i = _cmul_conj(hr, hi, gfr, gfi)
        dlr_ref[...] = t1r - t2r
        dli_ref[...] = t1i - t2i
        lar, lai = _cmul(lr_, li_, ar, ai)
        gdt, _ = _cmul_conj(lar, lai, gatr, gati)
        dld_ref[...] = jnp.sum(gdt, axis=2, keepdims=True) * dt

    sd = jax.ShapeDtypeStruct
    return pl.pallas_call(
        body, name=name,
        out_shape=[sd(lr.shape, F32), sd(lr.shape, F32), sd(ld.shape, F32), sd(br.shape, F32), sd(br.shape, F32)],
    )(lr, li, ld, br, bi, gar, gai, gbr, gbi)


SCAN_ROWS = 64


def _scan_rows(L):
    return min(SCAN_ROWS, L)


def _power_table(pr_s, pi_s, zr, zi, L, reverse):
    R = _scan_rows(L)
    row = lax.broadcasted_iota(jnp.int32, (R, 1), 0).astype(F32)
    dist = (R - row) if reverse else (row + 1.0)
    mag = jnp.exp(dist * zr)
    pr_s[...] = mag * jnp.cos(dist * zi)
    pi_s[...] = mag * jnp.sin(dist * zi)


def _scan(xr, xi, pr_s, pi_s, L, reverse):
    R = _scan_rows(L)
    nt = L // R
    assert L % R == 0 and R & (R - 1) == 0
    ns = CHUNK_X // LANE
    ridx = lax.broadcasted_iota(jnp.int32, (R, LANE), 0)

    def power(ref, d, cs):
        at = R - d if reverse else d - 1
        return ref[at:at + 1, cs]

    def shift(v, d):
        if d < 8:
            if reverse:
                return jnp.where(ridx < R - d, pltpu.roll(v, R - d, 0), 0.0)
            return jnp.where(ridx >= d, pltpu.roll(v, d, 0), 0.0)
        zeros = jnp.zeros((d, LANE), F32)
        return jnp.concatenate([v[d:], zeros], 0) if reverse else jnp.concatenate([zeros, v[:R - d]], 0)

    def tile(n, carry):
        t = nt - 1 - n if reverse else n
        rows = pl.ds(pl.multiple_of(t * R, R), R)
        edges = []
        for c in range(ns):
            cs = slice(c * LANE, (c + 1) * LANE)
            vr, vi = xr[rows, cs], xi[rows, cs]
            d = 1
            while d < R:
                ar, ai = power(pr_s, d, cs), power(pi_s, d, cs)
                sr, si = shift(vr, d), shift(vi, d)
                vr, vi = vr + ar * sr - ai * si, vi + ar * si + ai * sr
                d *= 2
            cr, ci = carry[2 * c], carry[2 * c + 1]
            pr, pi = pr_s[:, cs], pi_s[:, cs]
            vr, vi = vr + pr * cr - pi * ci, vi + pr * ci + pi * cr
            xr[rows, cs] = vr
            xi[rows, cs] = vi
            edge = slice(0, 1) if reverse else slice(R - 1, R)
            edges += [vr[edge], vi[edge]]
        return tuple(edges)

    lax.fori_loop(0, nt, tile, tuple(jnp.zeros((1, LANE), F32) for _ in range(2 * ns)))


def _gelu(x):
    t = jnp.tanh(0.7978845608028654 * (x + 0.044715 * x * x * x))
    return 0.5 * x * (1.0 + t)


def _gelu_grad(x):
    t = jnp.tanh(0.7978845608028654 * (x + 0.044715 * x * x * x))
    return 0.5 * (1.0 + t) + 0.5 * x * (1.0 - t * t) * 0.7978845608028654 * (1.0 + 0.134145 * x * x)


def _call(body, *, name, grid, in_specs, out_specs, out_shape, scratch_shapes, vmem, args, semantics,
          copies=None):
    n_i, n_o, n_s = len(in_specs), len(out_specs), len(scratch_shapes)
    if copies is None:
        out = pl.pallas_call(
            body, name=name, grid=grid, in_specs=in_specs, out_specs=out_specs, out_shape=out_shape,
            scratch_shapes=scratch_shapes, compiler_params=_cparams(semantics, vmem))(*args)
        return list(out), []
    n_ci, n_co = len(copies.inputs), len(copies.out_shape)

    def hosted(*refs):
        ins, cin = refs[:n_i], refs[n_i:n_i + n_ci]
        outs = refs[n_i + n_ci:n_i + n_ci + n_o]
        cout = refs[n_i + n_ci + n_o:n_i + n_ci + n_o + n_co]
        scr = refs[n_i + n_ci + n_o + n_co:n_i + n_ci + n_o + n_co + n_s]
        sems = refs[n_i + n_ci + n_o + n_co + n_s:]
        step = pl.program_id(0)
        for axis in range(1, len(grid)):
            step = step * grid[axis] + pl.program_id(axis)
        copies.emit(cin, cout, sems, step, math.prod(grid))
        body(*ins, *outs, *scr)

    out = pl.pallas_call(
        hosted, name=name, grid=grid, in_specs=list(in_specs) + [HBM] * n_ci,
        out_specs=list(out_specs) + [HBM] * n_co, out_shape=list(out_shape) + list(copies.out_shape),
        scratch_shapes=list(scratch_shapes) + list(copies.scratch),
        compiler_params=pltpu.CompilerParams(dimension_semantics=("arbitrary",) * len(grid),
                                             vmem_limit_bytes=vmem, has_side_effects=True))(*args, *copies.inputs)
    return list(out[:n_o]), list(out[n_o:])


def merge_copies(group):
    group = [c for c in group if c is not None]
    if len(group) <= 1:
        return group[0] if group else None
    bounds, i0, o0, s0 = [], 0, 0, 0
    for c in group:
        bounds.append((i0, o0, s0))
        i0, o0, s0 = i0 + len(c.inputs), o0 + len(c.out_shape), s0 + len(c.scratch)
    phases = []
    for c, (i, o, s) in zip(group, bounds):
        for frac, fn in c.phases:
            def shifted(ins, outs, sems, fn=fn, c=c, i=i, o=o, s=s):
                fn(ins[i:i + len(c.inputs)], outs[o:o + len(c.out_shape)], sems[s:s + len(c.scratch)])
            phases.append((frac, shifted))
    return HostedCopies([a for c in group for a in c.inputs], [a for c in group for a in c.out_shape],
                        [a for c in group for a in c.scratch], phases)


def ssm_fwd(name, proj, wbr, wbi, zr, zi, wcr, wci, dskip, D, copies=None):
    L = proj.shape[0]
    DS = D // 4
    NC = DS // CHUNK_U
    uo = (5 * D // 2) // CHUNK_U
    ch = _pick(L, 256)

    def body(u_ref, wbr_ref, wbi_ref, zr_ref, zi_ref, wcr_ref, wci_ref, ds_ref,
             y_ref, hg_ref, xr_ref, xi_ref, sr, si, pr_s, pi_s):
        def fill(ci, carry):
            rows = pl.ds(pl.multiple_of(ci * ch, ch), ch)
            ub = u_ref[rows, :].astype(BF16)
            sr[rows, :] = _dot(ub, wbr_ref[...])
            si[rows, :] = _dot(ub, wbi_ref[...])
            return carry

        lax.fori_loop(0, L // ch, fill, 0)
        _power_table(pr_s, pi_s, zr_ref[...], zi_ref[...], L, reverse=False)
        _scan(sr, si, pr_s, pi_s, L, reverse=False)

        def emit(ci, carry):
            rows = pl.ds(pl.multiple_of(ci * ch, ch), ch)
            xrb, xib = sr[rows, :].astype(BF16), si[rows, :].astype(BF16)
            xr_ref[rows, :] = xrb
            xi_ref[rows, :] = xib
            y = _dot(xrb, wcr_ref[...]) - _dot(xib, wci_ref[...]) + ds_ref[...] * u_ref[rows, :]
            y_ref[rows, :] = y
            hg_ref[rows, :] = _gelu(y).astype(BF16)
            return carry

        lax.fori_loop(0, L // ch, emit, 0)

    ucol = pl.BlockSpec((L, CHUNK_U), lambda k: (0, k))
    xcol = pl.BlockSpec((L, CHUNK_X), lambda k: (0, k))
    sd = jax.ShapeDtypeStruct
    return _call(
        body, name=name, grid=(NC,),
        in_specs=[pl.BlockSpec((L, CHUNK_U), lambda k: (0, uo + k)),
                  pl.BlockSpec((None, CHUNK_U, CHUNK_X), lambda k: (k, 0, 0)),
                  pl.BlockSpec((None, CHUNK_U, CHUNK_X), lambda k: (k, 0, 0)),
                  pl.BlockSpec((1, CHUNK_X), lambda k: (0, k)), pl.BlockSpec((1, CHUNK_X), lambda k: (0, k)),
                  pl.BlockSpec((None, CHUNK_X, CHUNK_U), lambda k: (k, 0, 0)),
                  pl.BlockSpec((None, CHUNK_X, CHUNK_U), lambda k: (k, 0, 0)),
                  pl.BlockSpec((1, CHUNK_U), lambda k: (0, k))],
        out_specs=[ucol, ucol, xcol, xcol],
        out_shape=[sd((L, DS), F32), sd((L, DS), BF16), sd((L, 4 * DS), BF16), sd((L, 4 * DS), BF16)],
        scratch_shapes=[pltpu.VMEM((L, CHUNK_X), F32), pltpu.VMEM((L, CHUNK_X), F32),
                        pltpu.VMEM((_scan_rows(L), CHUNK_X), F32), pltpu.VMEM((_scan_rows(L), CHUNK_X), F32)],
        vmem=VMEM_BIG, args=(proj, wbr, wbi, zr, zi, wcr, wci, dskip), semantics=("parallel",), copies=copies)


def ssm_bwd(name, dhg, ypre, proj, xr, xi, wbr, wbi, zr, zi, wcr, wci, dskip, D, copies=None):
    L = proj.shape[0]
    DS = D // 4
    NC = DS // CHUNK_U
    uo = (5 * D // 2) // CHUNK_U
    ch = _pick(L, 256)
    nch = L // ch
    halo = 16

    def body(dhg_ref, y_ref, u_ref, xr_ref, xi_ref, wbr_ref, wbi_ref, zr_ref, zi_ref, wcr_ref, wci_ref,
             ds_ref, du_ref, dwcr_ref, dwci_ref, dwbr_ref, dwbi_ref, dar_ref, dai_ref, dds_ref,
             gr, gi, duf, pr_s, pi_s):
        dwcr_ref[...] = jnp.zeros_like(dwcr_ref)
        dwci_ref[...] = jnp.zeros_like(dwci_ref)
        dwbr_ref[...] = jnp.zeros_like(dwbr_ref)
        dwbi_ref[...] = jnp.zeros_like(dwbi_ref)
        dar_ref[...] = jnp.zeros_like(dar_ref)
        dai_ref[...] = jnp.zeros_like(dai_ref)
        dds_ref[...] = jnp.zeros_like(dds_ref)

        def first(ci, carry):
            rows = pl.ds(pl.multiple_of(ci * ch, ch), ch)
            dy = dhg_ref[rows, :] * _gelu_grad(y_ref[rows, :])
            dyb = dy.astype(BF16)
            dds_ref[...] += jnp.sum(dy * u_ref[rows, :], axis=0, keepdims=True)
            duf[rows, :] = ds_ref[...] * dy
            gr[rows, :] = _dot(dyb, wcr_ref[...], NT)
            gi[rows, :] = -_dot(dyb, wci_ref[...], NT)
            dwcr_ref[...] += _dot(xr_ref[rows, :], dyb, TN)
            dwci_ref[...] -= _dot(xi_ref[rows, :], dyb, TN)
            return carry

        lax.fori_loop(0, nch, first, 0)
        _power_table(pr_s, pi_s, zr_ref[...], -zi_ref[...], L, reverse=True)
        _scan(gr, gi, pr_s, pi_s, L, reverse=True)

        def lam_grad(gxr, gxi, xpr, xpi):
            pr, pi = _cmul_conj(xpr, xpi, gxr, gxi)
            dar_ref[...] += jnp.sum(pr, axis=0, keepdims=True)
            dai_ref[...] += jnp.sum(pi, axis=0, keepdims=True)

        def second(ci, carry):
            r0 = pl.multiple_of(ci * ch, ch)
            rows = pl.ds(r0, ch)
            gxr, gxi = gr[rows, :], gi[rows, :]
            gxrb, gxib = gxr.astype(BF16), gxi.astype(BF16)
            du_ref[rows, :] = (duf[rows, :] + _dot(gxrb, wbr_ref[...], NT) + _dot(gxib, wbi_ref[...], NT)).astype(BF16)
            ub = u_ref[rows, :].astype(BF16)
            dwbr_ref[...] += _dot(ub, gxrb, TN)
            dwbi_ref[...] += _dot(ub, gxib, TN)
            return carry

        lax.fori_loop(0, nch, second, 0)

        ridx = lax.broadcasted_iota(jnp.int32, (ch, CHUNK_X), 0)
        xpr = jnp.where(ridx >= 1, pltpu.roll(xr_ref[0:ch, :].astype(F32), 1, 0), 0.0)
        xpi = jnp.where(ridx >= 1, pltpu.roll(xi_ref[0:ch, :].astype(F32), 1, 0), 0.0)
        lam_grad(gr[0:ch, :], gi[0:ch, :], xpr, xpi)

        def third(ci, carry):
            r0 = pl.multiple_of(ci * ch, ch)
            ext = pl.ds(pl.multiple_of(r0 - halo, halo), ch + halo)
            xpr = pltpu.roll(xr_ref[ext, :].astype(F32), 1, 0)[halo:]
            xpi = pltpu.roll(xi_ref[ext, :].astype(F32), 1, 0)[halo:]
            lam_grad(gr[pl.ds(r0, ch), :], gi[pl.ds(r0, ch), :], xpr, xpi)
            return carry

        if nch > 1:
            lax.fori_loop(1, nch, third, 0)

    ucol = pl.BlockSpec((L, CHUNK_U), lambda k: (0, k))
    xcol = pl.BlockSpec((L, CHUNK_X), lambda k: (0, k))
    wb_spec = pl.BlockSpec((None, CHUNK_U, CHUNK_X), lambda k: (k, 0, 0))
    wc_spec = pl.BlockSpec((None, CHUNK_X, CHUNK_U), lambda k: (k, 0, 0))
    avec = pl.BlockSpec((1, CHUNK_X), lambda k: (0, k))
    uvec = pl.BlockSpec((1, CHUNK_U), lambda k: (0, k))
    sd = jax.ShapeDtypeStruct
    return _call(
        body, name=name, grid=(NC,),
        in_specs=[ucol, ucol, pl.BlockSpec((L, CHUNK_U), lambda k: (0, uo + k)), xcol, xcol,
                  wb_spec, wb_spec, avec, avec, wc_spec, wc_spec, uvec],
        out_specs=[ucol, wc_spec, wc_spec, wb_spec, wb_spec, avec, avec, uvec],
        out_shape=[sd((L, DS), BF16), sd((NC, CHUNK_X, CHUNK_U), F32), sd((NC, CHUNK_X, CHUNK_U), F32),
                   sd((NC, CHUNK_U, CHUNK_X), F32), sd((NC, CHUNK_U, CHUNK_X), F32),
                   sd((1, 4 * DS), F32), sd((1, 4 * DS), F32), sd((1, DS), F32)],
        scratch_shapes=[pltpu.VMEM((L, CHUNK_X), F32), pltpu.VMEM((L, CHUNK_X), F32), pltpu.VMEM((L, CHUNK_U), F32),
                        pltpu.VMEM((_scan_rows(L), CHUNK_X), F32), pltpu.VMEM((_scan_rows(L), CHUNK_X), F32)],
        vmem=VMEM_BIG, args=(dhg, ypre, proj, xr, xi, wbr, wbi, zr, zi, wcr, wci, dskip),
        semantics=("parallel",), copies=copies)


def _block_diag(w, transpose):
    G = w.shape[0]
    nc = G // GROUPS_PER_CHUNK
    w4 = w.reshape(nc, GROUPS_PER_CHUNK, SSM_GROUP, SSM_STATE)
    eye = jnp.eye(GROUPS_PER_CHUNK, dtype=w.dtype)
    if transpose:
        return (w4[:, None, :, :, :].transpose(0, 1, 4, 2, 3) * eye[None, :, None, :, None]).reshape(
            nc, CHUNK_X, CHUNK_U).astype(BF16)
    return (w4[:, :, :, None, :] * eye[None, :, None, :, None]).reshape(nc, CHUNK_U, CHUNK_X).astype(BF16)


def _diag_blocks(dw, transpose):
    nc = dw.shape[0]
    gpc = GROUPS_PER_CHUNK
    if transpose:
        d5 = dw.reshape(nc, gpc, SSM_STATE, gpc, SSM_GROUP)
        blocks = [d5[:, g, :, g, :] for g in range(gpc)]
        return jnp.stack(blocks, axis=1).transpose(0, 1, 3, 2).reshape(nc * gpc, SSM_GROUP, SSM_STATE)
    d5 = dw.reshape(nc, gpc, SSM_GROUP, gpc, SSM_STATE)
    blocks = [d5[:, g, :, g, :] for g in range(gpc)]
    return jnp.stack(blocks, axis=1).reshape(nc * gpc, SSM_GROUP, SSM_STATE)


SHARD_BLOCK_ELEMS = 128 * 1024


def _shard_rows(R, C, scale):
    return _pick(R, max(8, scale * SHARD_BLOCK_ELEMS // C))


def cast_bf16(name, w, layer):
    shape = w.shape[1:]
    w3 = w.reshape(w.shape[0], -1, shape[-1])
    _, R, C = w3.shape
    tr = _shard_rows(R, C, 4)

    def body(w_ref, o_ref):
        o_ref[...] = w_ref[...].astype(BF16)

    out = pl.pallas_call(body, name=name, grid=(R // tr,),
                         in_specs=[pl.BlockSpec((None, tr, C), lambda i: (layer, i, 0))],
                         out_specs=pl.BlockSpec((tr, C), lambda i: (i, 0)),
                         out_shape=jax.ShapeDtypeStruct((R, C), BF16),
                         compiler_params=_cparams(("parallel",), VMEM_MID))(w3)
    return out.reshape(shape)


def _adamw(w, g, m, v):
    m = ADAM_B1 * m + (1.0 - ADAM_B1) * g
    v = ADAM_B2 * v + (1.0 - ADAM_B2) * (g * g)
    delta = -ADAM_LR * ((m * ADAM_C1) / (jnp.sqrt(v * ADAM_C2) + ADAM_EPS) + ADAM_WD * w)
    return delta, m, v


def _own_core(g4):
    return (lambda p: p[0]) if g4.shape[1] == 2 else (lambda p: 0)


def chip_partial(name, pos, g4, recv_a):
    _, _, R, C = g4.shape
    tr = _shard_rows(R, C, 4)
    core = _own_core(g4)

    def body(pos_ref, g_ref, a_ref, o_ref):
        o_ref[...] = (g_ref[...] + a_ref[...]).astype(BF16)

    return pl.pallas_call(
        body, name=name,
        grid_spec=pltpu.PrefetchScalarGridSpec(
            num_scalar_prefetch=1, grid=(4, R // tr),
            in_specs=[pl.BlockSpec((None, None, tr, C), lambda q, i, p: (q, core(p), i, 0)),
                      pl.BlockSpec((None, tr, C), lambda q, i, p: (q, i, 0))],
            out_specs=pl.BlockSpec((None, tr, C), lambda q, i, p: (q, i, 0))),
        out_shape=jax.ShapeDtypeStruct((4, R, C), BF16),
        compiler_params=_cparams(("parallel", "parallel"), VMEM_MID))(pos, g4, recv_a)


def adamw_shard(name, pos, layer, g4, recv_a, recv_b, w, m, v, prev):
    _, _, R, C = g4.shape
    tr = _shard_rows(R, C, 1)
    n_prev = 0 if prev is None else 4
    core = _own_core(g4)

    def body(pos_ref, g_ref, a_ref, b_ref, w_ref, m_ref, v_ref, *rest):
        go_ref, d_ref, mo_ref, vo_ref = rest[n_prev:]
        gs = g_ref[...] + a_ref[...]
        for j in range(3):
            gs = gs + b_ref[j].astype(F32)
        delta, mn, vn = _adamw(w_ref[...], gs, m_ref[...], v_ref[...])
        go_ref[...] = gs
        d_ref[...] = delta
        mo_ref[...] = mn
        vo_ref[...] = vn

    lay = pl.BlockSpec((None, tr, C), lambda i, p: (layer, i, 0))
    in_specs = [pl.BlockSpec((None, None, tr, C), lambda i, p: (p[1], core(p), i, 0)),
                pl.BlockSpec((None, tr, C), lambda i, p: (p[1], i, 0)),
                pl.BlockSpec((3, tr, C), lambda i, p: (0, i, 0)), lay, lay, lay]
    args = [g4, recv_a, recv_b, w, m, v]
    aliases = {}
    if prev is not None:
        in_specs += [pl.BlockSpec(memory_space=pl.ANY)] * 4
        args += list(prev)
        aliases = {7 + j: j for j in range(4)}
    return pl.pallas_call(
        body, name=name,
        grid_spec=pltpu.PrefetchScalarGridSpec(
            num_scalar_prefetch=1, grid=(R // tr,), in_specs=in_specs, out_specs=[lay] * 4),
        out_shape=[jax.ShapeDtypeStruct(w.shape, F32)] * 4,
        input_output_aliases=aliases,
        compiler_params=_cparams(("parallel",), VMEM_MID))(pos, *args)


def adamw_small(name, gathered, w, m, v):
    _, R, C = gathered.shape
    tr = _pick(R, 512)

    def body(g_ref, w_ref, m_ref, v_ref, go_ref, d_ref, mo_ref, vo_ref):
        gs = g_ref[0]
        for j in range(1, N_DEV):
            gs = gs + g_ref[j]
        delta, mn, vn = _adamw(w_ref[...], gs, m_ref[...], v_ref[...])
        go_ref[...] = gs
        d_ref[...] = delta
        mo_ref[...] = mn
        vo_ref[...] = vn

    spec = pl.BlockSpec((tr, C), lambda i: (i, 0))
    return pl.pallas_call(
        body, name=name, grid=(R // tr,),
        in_specs=[pl.BlockSpec((N_DEV, tr, C), lambda i: (0, i, 0)), spec, spec, spec], out_specs=[spec] * 4,
        out_shape=[jax.ShapeDtypeStruct((R, C), F32)] * 4,
        compiler_params=_cparams(("parallel",), VMEM_MID))(gathered, w, m, v)


def _position():
    return lax.axis_index("x"), lax.axis_index("y"), lax.axis_index("c")


FORWARD_AT = 0.7


def gather_copies(shards):
    n = len(shards)

    def parts(ins, outs, sems):
        send_sems, recv_sems, local_sems = sems
        x, y, c = _position()
        me, sibling = (x, y, c), (x, y, 1 - c)
        chips = [(1 - x, y), (x, 1 - y), (1 - x, 1 - y)]

        def copy(a, k, block, to, src=None):
            blk = outs[a].at[4 * block[0] + 2 * block[1] + block[2]]
            return pltpu.make_async_remote_copy(
                src_ref=blk if src is None else src, dst_ref=blk,
                send_sem=send_sems.at[a, k], recv_sem=recv_sems.at[a, k], device_id=to, device_id_type=MESH)

        mine = [pltpu.make_async_copy(ins[a], outs[a].at[4 * x + 2 * y + c], local_sems.at[a]) for a in range(n)]
        first = [[copy(a, 0, me, sibling, src=ins[a])] +
                 [copy(a, 1 + j, me, (*chip, c), src=ins[a]) for j, chip in enumerate(chips)] for a in range(n)]
        landed = [[copy(a, 1 + j, (*chip, c), me) for j, chip in enumerate(chips)] for a in range(n)]
        passed = [[copy(a, 4 + j, (*chip, c), sibling) for j, chip in enumerate(chips)] for a in range(n)]
        from_sibling = [[copy(a, 0, sibling, me)] +
                        [copy(a, 4 + j, (*chip, 1 - c), me) for j, chip in enumerate(chips)] for a in range(n)]
        return mine, first, landed, passed, from_sibling

    def start(ins, outs, sems):
        mine, first, _, _, _ = parts(ins, outs, sems)
        for a in range(n):
            mine[a].start()
            for cp in first[a]:
                cp.start()

    def forward(ins, outs, sems):
        _, _, landed, passed, _ = parts(ins, outs, sems)
        for a in range(n):
            for j in range(3):
                landed[a][j].wait_recv()
                passed[a][j].start()

    def finish(ins, outs, sems):
        mine, first, _, passed, from_sibling = parts(ins, outs, sems)
        for a in range(n):
            for cp in from_sibling[a]:
                cp.wait_recv()
        for a in range(n):
            for cp in first[a] + passed[a]:
                cp.wait_send()
            mine[a].wait()

    return HostedCopies(
        list(shards), [jax.ShapeDtypeStruct((N_DEV,) + s.shape, s.dtype) for s in shards],
        [pltpu.SemaphoreType.DMA((n, 7)), pltpu.SemaphoreType.DMA((n, 7)), pltpu.SemaphoreType.DMA((n,))],
        [(0.0, start), (FORWARD_AT, forward), (1.0, finish)])


def _exchange_copies(arrays, out_lead, make):
    n = len(arrays)

    def all_copies(ins, outs, sems):
        send_sems, recv_sems = sems
        return [make(ins[a], outs[a], send_sems.at[a, k], recv_sems.at[a, k], k)
                for a in range(n) for k in range(out_lead)]

    def start(ins, outs, sems):
        for cp in all_copies(ins, outs, sems):
            cp.start()

    def finish(ins, outs, sems):
        for cp in all_copies(ins, outs, sems):
            cp.wait()

    return HostedCopies(
        list(arrays), [jax.ShapeDtypeStruct((out_lead,) + a.shape[2:], a.dtype) for a in arrays],
        [pltpu.SemaphoreType.DMA((n, out_lead)), pltpu.SemaphoreType.DMA((n, out_lead))],
        [(0.0, start), (1.0, finish)])


def sibling_copies(grads):
    def make(src, dst, send_sem, recv_sem, q):
        x, y, c = _position()
        core = 1 - c if src.shape[1] == 2 else 0
        return pltpu.make_async_remote_copy(
            src_ref=src.at[q, core], dst_ref=dst.at[q], send_sem=send_sem, recv_sem=recv_sem,
            device_id=(x, y, 1 - c), device_id_type=MESH)

    return _exchange_copies(grads, 4, make)


def chip_copies(parts):
    def make(src, dst, send_sem, recv_sem, j):
        x, y, c = _position()
        chip = [(1 - x, y), (x, 1 - y), (1 - x, 1 - y)][j]
        return pltpu.make_async_remote_copy(
            src_ref=src.at[2 * chip[0] + chip[1], 0], dst_ref=dst.at[j], send_sem=send_sem, recv_sem=recv_sem,
            device_id=(*chip, c), device_id_type=MESH)

    return _exchange_copies(parts, 3, make)


class Carrier:
    def __init__(self):
        self.plan = {}
        self.counts = {}

    def ride(self, site, make, store):
        self.plan.setdefault(site, []).append((make, store))

    def make(self, site, ctx=None):
        if site not in self.plan:
            return None
        group = [make(ctx) for make, _ in self.plan[site]]
        self.counts[site] = [len(c.out_shape) for c in group]
        return merge_copies(group)

    def store(self, site, results):
        if site in self.plan:
            at = 0
            for (_, store), n in zip(self.plan[site], self.counts[site]):
                store(results[at:at + n])
                at += n

    def split(self, site, out):
        if site not in self.plan:
            return out
        self.store(site, out[1])
        return out[0]


def _pool_weight(gathered):
    PG = gathered.shape[-1]
    return gathered.transpose(1, 0, 2, 3).reshape(N_POOL_GROUPS, PG, PG)


def _layer_params(l, ln_g, pool_scale, lam_re, lam_im, log_dt, b_re, b_im, c_re, c_im,
                  d_skip, b_glu, branch_g):
    G, P = lam_re.shape[1:]
    p = dict(
        ln_g=ln_g[l][None, :], pool_scale=pool_scale[l][None, :], d_skip=d_skip[l][None, :],
        b_glu=b_glu[l][None, :], branch_g=branch_g[l][None, :],
        lr=lam_re[l].reshape(G, 1, P), li=lam_im[l].reshape(G, 1, P), ld=log_dt[l].reshape(G, 1, 1),
        br=b_re[l].transpose(0, 2, 1), bi=b_im[l].transpose(0, 2, 1), cr=c_re[l], ci=c_im[l])
    return p


def layer_fwd(l, x, p, gw, D, carrier):
    t = f"l{l}_"
    h = rms_fwd(t + "rms_fwd", x, p["ln_g"])
    site = (l, "proj")
    proj = carrier.split(site, mm_nn_gathered(t + "proj", h, gw["w_in", l], copies=carrier.make(site)))
    wp = _pool_weight(gw["w_pool", l])
    ypool = pool_fwd(t + "pool_fwd", proj, wp, p["pool_scale"])
    site = (l, "attn_fwd")
    yattn, landed = attn_fwd(t + "attn_fwd", proj, D, copies=carrier.make(site))
    carrier.store(site, landed)
    zr, zi, bbr, bbi = ssm_prep(t + "ssm_prep", p["lr"], p["li"], p["ld"], p["br"], p["bi"])
    ssm_w = dict(wbr=_block_diag(bbr, False), wbi=_block_diag(bbi, False),
                 zr=zr.reshape(1, -1), zi=zi.reshape(1, -1),
                 wcr=_block_diag(p["cr"], True), wci=_block_diag(p["ci"], True))
    site = (l, "ssm_fwd")
    (ypre, hg, xr, xi), landed = ssm_fwd(
        t + "ssm_fwd", proj, ssm_w["wbr"], ssm_w["wbi"], ssm_w["zr"], ssm_w["zi"],
        ssm_w["wcr"], ssm_w["wci"], p["d_skip"], D, copies=carrier.make(site))
    carrier.store(site, landed)
    glu_pre = mm_nn_gathered(t + "glu", hg, gw["w_glu", l])
    y = branch_fwd(t + "branch_fwd", ypool, yattn, glu_pre, proj, p["b_glu"], p["branch_g"])
    out = mm_plain(t + "out", y, gw["w_out", l].reshape(D, D), NN, res=x)
    saved = dict(x=x, h=h, proj=proj, ypool=ypool, yattn=yattn, ypre=ypre, hg=hg, xr=xr, xi=xi,
                 glu_pre=glu_pre, y=y, ssm_w=ssm_w, wp=wp)
    return out, saved


def layer_bwd(l, dres, dres_b, s, p, gw, D, carrier, pos, split_w_in):
    t = f"l{l}_"
    proj = s["proj"]

    def by_target(g):
        return g.reshape(4, 2, -1, g.shape[-1])

    big = {}
    w_out_g = gw["w_out", l].reshape(D, D)
    site = (l, "dy")
    dy = carrier.split(site, mm_plain(t + "dy", dres_b, w_out_g, NT, copies=carrier.make(site)))
    big["w_out"] = by_target(mm_plain(t + "dw_out", s["y"], dres_b, TN).reshape(N_DEV, D // N_DEV, D))
    dypool, dyattn, dglu, dpg, dag, dsg, dbg, dbglu = branch_bwd(
        t + "branch_bwd", dy, s["ypool"], s["yattn"], s["glu_pre"], proj, p["b_glu"], p["branch_g"])
    dhg = mm_nt_gathered(t + "dhg", dglu, gw["w_glu", l])
    big["w_glu"] = by_target(mm_tn_scattered(t + "dw_glu", s["hg"], dglu))
    w = s["ssm_w"]
    site = (l, "ssm_bwd")
    (du, dwcr, dwci, dwbr, dwbi, dar, dai, dds), landed = ssm_bwd(
        t + "ssm_bwd", dhg, s["ypre"], proj, s["xr"], s["xi"], w["wbr"], w["wbi"], w["zr"], w["zi"],
        w["wcr"], w["wci"], p["d_skip"], D, copies=carrier.make(site, big))
    carrier.store(site, landed)
    G, _, P = p["lr"].shape
    dlr, dli, dld, dbr, dbi = ssm_prep_bwd(
        t + "ssm_prep_bwd", p["lr"], p["li"], p["ld"], p["br"], p["bi"],
        dar.reshape(G, 1, P), dai.reshape(G, 1, P), _diag_blocks(dwbr, False), _diag_blocks(dwbi, False))
    site = (l, "attn_bwd")
    (dq, dk, dv), landed = attn_bwd(t + "attn_bwd", proj, s["yattn"], dyattn, D, copies=carrier.make(site, big))
    carrier.store(site, landed)
    dxp, dwp, dps = pool_bwd(t + "pool_bwd", dypool, proj, s["wp"], p["pool_scale"])
    dproj = jnp.concatenate([dxp, dpg, dq, dk, dv, dag, du, dsg], axis=1)
    PG = dwp.shape[1]
    big["w_pool"] = by_target(dwp.reshape(N_POOL_GROUPS, N_DEV, PG // N_DEV, PG).transpose(1, 0, 2, 3))
    if split_w_in:
        site = (l, "dw_in_a")
        to_sibling = carrier.split(site, mm_tn_half(t + "dw_in_a", s["h"], dproj, pos, False,
                                                    copies=carrier.make(site, big)))
        site = (l, "dw_in_b")
        mine = carrier.split(site, mm_tn_half(t + "dw_in_b", s["h"], dproj, pos, True,
                                              copies=carrier.make(site, dict(big, to_sibling=to_sibling[:, None]))))
        big["w_in"] = mine[:, None]
    else:
        big["w_in"] = by_target(mm_tn_scattered(t + "dw_in", s["h"], dproj))
    site = (l, "dh")
    dh = carrier.split(site, mm_nt_gathered(t + "dh", dproj, gw["w_in", l], copies=carrier.make(site, big)))
    dx, dx_b, dlng = rms_bwd(t + "rms_bwd", s["x"], dh, dres, p["ln_g"])
    small = dict(ln_g=dlng[0], pool_scale=dps[0], lam_re=dlr.reshape(G, P), lam_im=dli.reshape(G, P),
                 log_dt=dld.reshape(G), b_re=dbr.transpose(0, 2, 1), b_im=dbi.transpose(0, 2, 1),
                 c_re=_diag_blocks(dwcr, True), c_im=_diag_blocks(dwci, True),
                 d_skip=dds[0], b_glu=dbglu[0], branch_g=dbg[0])
    return dx, dx_b, small


SMALL_NAMES = ("ln_g", "pool_scale", "lam_re", "lam_im", "log_dt", "b_re", "b_im", "c_re", "c_im",
               "d_skip", "b_glu", "branch_g", "final_g")
BIG_NAMES = ("w_in", "w_pool", "w_glu", "w_out")
WEIGHT_ORDER = ("ln_g", "w_in", "w_pool", "pool_scale", "lam_re", "lam_im", "log_dt", "b_re", "b_im",
                "c_re", "c_im", "d_skip", "w_glu", "b_glu", "branch_g", "w_out", "final_g")


PACK_ROWS = 512


def _pack(arrs):
    flat = jnp.concatenate([a.reshape(-1) for a in arrs])
    pad = (-flat.shape[0]) % (PACK_ROWS * LANE)
    return jnp.pad(flat, (0, pad)).reshape(-1, LANE)


def _unpack(packed, like):
    flat = packed.reshape(-1)
    out, off = [], 0
    for a in like:
        out.append(flat[off:off + a.size].reshape(a.shape))
        off += a.size
    return out


def kernel(x, ln_g, w_in, w_pool, pool_scale, lam_re, lam_im, log_dt, b_re, b_im, c_re, c_im, d_skip, w_glu, b_glu, branch_g, w_out, final_g, loss_target, m_ln_g, m_w_in, m_w_pool, m_pool_scale, m_lam_re, m_lam_im, m_log_dt, m_b_re, m_b_im, m_c_re, m_c_im, m_d_skip, m_w_glu, m_b_glu, m_branch_g, m_w_out, m_final_g, v_ln_g, v_w_in, v_w_pool, v_pool_scale, v_lam_re, v_lam_im, v_log_dt, v_b_re, v_b_im, v_c_re, v_c_im, v_d_skip, v_w_glu, v_b_glu, v_branch_g, v_w_out, v_final_g):
    W = dict(ln_g=ln_g, w_in=w_in, w_pool=w_pool, pool_scale=pool_scale, lam_re=lam_re, lam_im=lam_im,
             log_dt=log_dt, b_re=b_re, b_im=b_im, c_re=c_re, c_im=c_im, d_skip=d_skip, w_glu=w_glu,
             b_glu=b_glu, branch_g=branch_g, w_out=w_out, final_g=final_g)
    Mo = dict(ln_g=m_ln_g, w_in=m_w_in, w_pool=m_w_pool, pool_scale=m_pool_scale, lam_re=m_lam_re,
              lam_im=m_lam_im, log_dt=m_log_dt, b_re=m_b_re, b_im=m_b_im, c_re=m_c_re, c_im=m_c_im,
              d_skip=m_d_skip, w_glu=m_w_glu, b_glu=m_b_glu, branch_g=m_branch_g, w_out=m_w_out,
              final_g=m_final_g)
    Vo = dict(ln_g=v_ln_g, w_in=v_w_in, w_pool=v_w_pool, pool_scale=v_pool_scale, lam_re=v_lam_re,
              lam_im=v_lam_im, log_dt=v_log_dt, b_re=v_b_re, b_im=v_b_im, c_re=v_c_re, c_im=v_c_im,
              d_skip=v_d_skip, w_glu=v_w_glu, b_glu=v_b_glu, branch_g=v_branch_g, w_out=v_w_out,
              final_g=v_final_g)
    depth = ln_g.shape[0]
    _, L, D = x.shape
    xc, yc, cc = _position()
    pos = jnp.stack([cc, 2 * xc + yc, 4 * xc + 2 * yc + cc]).astype(jnp.int32)

    shards = {(n, l): cast_bf16(f"cast_{n}_{l}", W[n], l) for n in BIG_NAMES for l in range(depth)}
    gw = {}
    carrier = Carrier()

    def gather_plan(keys):
        return (lambda ctx: gather_copies([shards[k] for k in keys])), (lambda outs: gw.update(zip(keys, outs)))

    first = [("w_in", 0)] + [("w_pool", l) for l in range(depth)]
    gw.update(zip(first, copies_call("gather_first", gather_copies([shards[k] for k in first]))))
    carrier.ride((0, "attn_fwd"), *gather_plan([("w_out", 0), ("w_glu", 0)]))
    for l in range(1, depth):
        carrier.ride((l - 1, "proj"), *gather_plan([("w_in", l)]))
        carrier.ride((l, "proj"), *gather_plan([("w_out", l), ("w_glu", l)]))

    own, recv_a, recv_b = {}, {}, {}

    def sibling_plan(l, names, keep, pick):
        def make(ctx):
            own.update({(n, l): ctx[n] for n in keep})
            return sibling_copies(pick(ctx))
        return make, (lambda outs: recv_a.update(zip([(n, l) for n in names], outs)))

    def chip_plan(l, names):
        def make(ctx):
            parts = [chip_partial(f"chip_partial_{n}_{l}", pos, own[n, l], recv_a[n, l])[:, None] for n in names]
            return chip_copies(parts)
        return make, (lambda outs: recv_b.update(zip([(n, l) for n in names], outs)))

    early, late = ("w_out", "w_glu"), ("w_in", "w_pool")
    for l in range(1, depth):
        carrier.ride((l, "dh"), *sibling_plan(l, BIG_NAMES, BIG_NAMES, lambda big: [big[n] for n in BIG_NAMES]))
        carrier.ride((l - 1, "ssm_bwd"), *chip_plan(l, ("w_out", "w_glu", "w_pool")))
        carrier.ride((l - 1, "attn_bwd"), *chip_plan(l, ("w_in",)))
    carrier.ride((0, "ssm_bwd"), *sibling_plan(0, early, early, lambda big: [big[n] for n in early]))
    carrier.ride((0, "dw_in_a"), *chip_plan(0, early))
    carrier.ride((0, "dw_in_b"), *sibling_plan(0, late, ("w_pool",), lambda ctx: [ctx["to_sibling"], ctx["w_pool"]]))

    def last_chip_make(big):
        own["w_in", 0] = big["w_in"]
        return chip_plan(0, late)[0](big)

    carrier.ride((0, "dh"), last_chip_make, chip_plan(0, late)[1])

    params = [_layer_params(l, ln_g, pool_scale, lam_re, lam_im, log_dt, b_re, b_im, c_re, c_im,
                            d_skip, b_glu, branch_g) for l in range(depth)]
    h = x[0]
    saved = []
    for l in range(depth):
        h, s = layer_fwd(l, h, params[l], gw, D, carrier)
        saved.append(s)
    loss_part, dres, dres_b, dfinal = loss_head("loss_head", h, final_g[None, :], loss_target[0])
    loss = lax.psum(loss_part[0, 0], ("x", "y", "c"))

    small = [None] * depth
    for l in reversed(range(depth)):
        dres, dres_b, small[l] = layer_bwd(l, dres, dres_b, saved[l], params[l], gw, D, carrier, pos,
                                           split_w_in=(l == 0))
    grad_x = dres[None]

    results = {}
    for n in BIG_NAMES:
        shape = W[n].shape
        R, C = int(math.prod(shape[1:-1])), shape[-1]
        w3, m3, v3 = (t.reshape(depth, R, C) for t in (W[n], Mo[n], Vo[n]))
        prev = None
        for l in range(depth):
            prev = adamw_shard(f"adamw_{n}_{l}", pos, l, own[n, l], recv_a[n, l], recv_b[n, l], w3, m3, v3, prev)
        results[n] = [t.reshape(shape) for t in prev]

    small_like = [W[n] for n in SMALL_NAMES]
    small_grads = [jnp.stack([small[l][n] for l in range(depth)]) for n in SMALL_NAMES[:-1]] + [dfinal[0]]
    gathered = copies_call("gather_small_grads", gather_copies([_pack(small_grads)]))[0]
    packed = adamw_small("adamw_small", gathered, _pack(small_like), _pack([Mo[n] for n in SMALL_NAMES]),
                         _pack([Vo[n] for n in SMALL_NAMES]))
    unpacked = [_unpack(t, small_like) for t in packed]
    for i, n in enumerate(SMALL_NAMES):
        results[n] = [unpacked[j][i] for j in range(4)]

    out = [loss, grad_x]
    for j in range(4):
        out += [results[n][j] for n in WEIGHT_ORDER]
    return tuple(out)
```

```python
import functools
import math

import jax
import jax.numpy as jnp
from jax import lax
from jax.experimental import pallas as pl
from jax.experimental.pallas import tpu as pltpu

F32 = jnp.float32
BF16 = jnp.bfloat16
MESH = pl.DeviceIdType.MESH

EPS = 1e-6
HEAD_DIM = 128
SSM_GROUP = 16
SSM_STATE = 64
GROUPS_PER_CHUNK = 8
CHUNK_U = GROUPS_PER_CHUNK * SSM_GROUP
CHUNK_X = GROUPS_PER_CHUNK * SSM_STATE
N_POOL_GROUPS = 4
POOL_HALO = 16
N_DEV = 8
LANE = 128
FULL_K = 4096
ATTN_TILE = 256
ATTN_DECAY_CUTOFF = 100.0
ROW_TILE = 128
VMEM_BIG = 58 * 1024 * 1024
VMEM_MID = 40 * 1024 * 1024

ADAM_LR = 0.001
ADAM_B1 = 0.9
ADAM_B2 = 0.999
ADAM_EPS = 1e-08
ADAM_WD = 0.01
ADAM_STEP = 10
ADAM_C1 = 1.0 / (1.0 - ADAM_B1 ** ADAM_STEP)
ADAM_C2 = 1.0 / (1.0 - ADAM_B2 ** ADAM_STEP)

NN = (((1,), (0,)), ((), ()))
NT = (((1,), (1,)), ((), ()))
TN = (((0,), (0,)), ((), ()))


def _pick(n, cap):
    if n <= cap:
        return n
    step = LANE if cap >= LANE else 8
    t = (cap // step) * step
    while t > step and n % t:
        t -= step
    assert n % t == 0, (n, cap)
    return t


def _cparams(sem, vmem=None):
    return pltpu.CompilerParams(dimension_semantics=sem, vmem_limit_bytes=vmem)


def _dot(a, b, dn=NN):
    return lax.dot_general(a, b, dn, preferred_element_type=F32)


def _sigmoid(x):
    e = jnp.exp(-jnp.abs(x))
    r = 1.0 / (1.0 + e)
    return jnp.where(x >= 0, r, e * r)


HBM = pl.BlockSpec(memory_space=pl.ANY)


class HostedCopies:
    def __init__(self, inputs, out_shape, scratch, phases):
        self.inputs, self.out_shape, self.scratch, self.phases = inputs, out_shape, scratch, phases

    def emit(self, ins, outs, sems, step, total):
        plan = {}
        for frac, fn in self.phases:
            plan.setdefault(min(total - 1, int(frac * total)), []).append(fn)
        for s in sorted(plan):
            def run(fns=plan[s]):
                for fn in fns:
                    fn(ins, outs, sems)
            if total == 1:
                run()
            else:
                pl.when(step == s)(run)


def copies_call(name, copies):
    n_i, n_o = len(copies.inputs), len(copies.out_shape)

    def body(*refs):
        copies.emit(refs[:n_i], refs[n_i:n_i + n_o], refs[n_i + n_o:], 0, 1)

    return pl.pallas_call(
        body, name=name, in_specs=[HBM] * n_i, out_specs=[HBM] * n_o, out_shape=copies.out_shape,
        scratch_shapes=copies.scratch, compiler_params=pltpu.CompilerParams(has_side_effects=True),
    )(*copies.inputs)


def _matmul(name, a, b, *, grid, a_spec, b_spec, o_spec, out_shape, dn,
            res=None, res_spec=None, pos=None, copies=None, product=None):
    ni, nj, nk = grid
    bs, b_specs = (list(b), list(b_spec)) if isinstance(b, (list, tuple)) else ([b], [b_spec])
    n_b = len(bs)
    n_pos = 0 if pos is None else 1
    n_res = 0 if res is None else 1
    n_ci = 0 if copies is None else len(copies.inputs)
    n_co = 0 if copies is None else len(copies.out_shape)

    def body(*refs):
        refs = refs[n_pos:]
        a_ref, b_refs = refs[0], refs[1:1 + n_b]
        r_ref = refs[1 + n_b] if n_res else None
        base = 1 + n_b + n_res
        cin = refs[base:base + n_ci]
        o_ref = refs[base + n_ci]
        cout = refs[base + n_ci + 1:base + n_ci + 1 + n_co]
        sems = refs[base + n_ci + 1 + n_co:]
        k = pl.program_id(2)
        if copies is not None:
            step = (pl.program_id(0) * nj + pl.program_id(1)) * nk + k
            copies.emit(cin, cout, sems, step, ni * nj * nk)

        if product is None:
            part = _dot(a_ref[...].astype(BF16), b_refs[0][...].astype(BF16), dn)
        else:
            part = product(a_ref, b_refs)
        if nk == 1:
            if r_ref is not None:
                part = part + r_ref[...]
            o_ref[...] = part.astype(o_ref.dtype)
        else:
            @pl.when(k == 0)
            def _():
                o_ref[...] = part if r_ref is None else part + r_ref[...]

            @pl.when(k > 0)
            def _():
                o_ref[...] += part

    assert nk == 1 or out_shape.dtype == F32
    in_specs = [a_spec] + b_specs + ([res_spec] if n_res else []) + [HBM] * n_ci
    args = ((pos,) if n_pos else ()) + (a, *bs) + ((res,) if n_res else ()) + tuple(copies.inputs if copies else ())
    out_specs = [o_spec] + [HBM] * n_co
    out_shapes = [out_shape] + list(copies.out_shape if copies else [])
    scratch = list(copies.scratch if copies else [])
    params = pltpu.CompilerParams(
        dimension_semantics=("arbitrary",) * 3 if copies else ("parallel", "parallel", "arbitrary"),
        vmem_limit_bytes=VMEM_BIG, has_side_effects=copies is not None)
    out = pl.pallas_call(
        body, name=name,
        grid_spec=pltpu.PrefetchScalarGridSpec(
            num_scalar_prefetch=n_pos, grid=grid, in_specs=in_specs, out_specs=out_specs, scratch_shapes=scratch),
        out_shape=out_shapes, compiler_params=params)(*args)
    return out[0] if copies is None else (out[0], list(out[1:]))


def mm_nn_gathered(name, a, parts, out_dtype=F32, copies=None):
    M, K = a.shape
    P, w = len(parts), parts[0].shape[2]
    nper = P * w
    tm, tk, tn = _pick(M, 1024), _pick(K, FULL_K), _pick(w, 768)
    r = w // tn
    per_part = N_DEV * r

    def b_spec(g):
        def index(i, j, k, *_):
            t = jnp.clip(j - g * per_part, 0, per_part - 1)
            return (t // r, k, t % r)
        return pl.BlockSpec((None, tk, tn), index)

    def o_index(i, j, k, *_):
        t = j % per_part
        return (i, (t // r) * (nper // tn) + (j // per_part) * r + t % r)

    def product(a_ref, b_refs):
        j = pl.program_id(1)
        b = b_refs[0][...]
        for g in range(1, P):
            b = jnp.where(j >= g * per_part, b_refs[g][...], b)
        return _dot(a_ref[...].astype(BF16), b.astype(BF16))

    return _matmul(
        name, a, list(parts), grid=(M // tm, P * per_part, K // tk),
        a_spec=pl.BlockSpec((tm, tk), lambda i, j, k, *_: (i, k)),
        b_spec=[b_spec(g) for g in range(P)], o_spec=pl.BlockSpec((tm, tn), o_index),
        out_shape=jax.ShapeDtypeStruct((M, N_DEV * nper), out_dtype), dn=NN, copies=copies,
        product=product if P > 1 else None)


NT_SLICES = 2
W_IN_PARTS = 2


def mm_nt_gathered(name, a, parts, out_dtype=F32, copies=None):
    M, _ = a.shape
    P, (_, N, w) = len(parts), parts[0].shape
    nper = P * w
    tm, tn = _pick(M, 1024), _pick(N, 1024)
    S = NT_SLICES

    def product(a_ref, b_refs):
        total = None
        for s in range(S):
            for g in range(P):
                off = (s * P + g) * w
                term = _dot(a_ref[:, off:off + w].astype(BF16), b_refs[g][s], NT)
                total = term if total is None else total + term
        return total

    return _matmul(
        name, a, list(parts), grid=(M // tm, N // tn, N_DEV // S),
        a_spec=pl.BlockSpec((tm, S * nper), lambda i, j, k, *_: (i, k)),
        b_spec=[pl.BlockSpec((S, tn, w), lambda i, j, k, *_: (k, j, 0)) for _ in range(P)],
        o_spec=pl.BlockSpec((tm, tn), lambda i, j, k, *_: (i, j)),
        out_shape=jax.ShapeDtypeStruct((M, N), out_dtype), dn=NT, copies=copies, product=product)


def mm_tn_scattered(name, a, b, copies=None):
    L, M = a.shape
    nper = b.shape[1] // N_DEV
    tm, tn, tk = _pick(M, 1024), _pick(nper, 768), _pick(L, FULL_K)
    r = nper // tn
    return _matmul(
        name, a, b, grid=(M // tm, N_DEV * r, L // tk),
        a_spec=pl.BlockSpec((tk, tm), lambda i, j, k, *_: (k, i)),
        b_spec=pl.BlockSpec((tk, tn), lambda i, j, k, *_: (k, j)),
        o_spec=pl.BlockSpec((None, tm, tn), lambda i, j, k, *_: (j // r, i, j % r)),
        out_shape=jax.ShapeDtypeStruct((N_DEV, M, nper), F32), dn=TN, copies=copies)


def mm_tn_half(name, a, b, pos, own, copies=None):
    L, M = a.shape
    nper = b.shape[1] // N_DEV
    tm, tn, tk = _pick(M, 1024), _pick(nper, 768), _pick(L, FULL_K)
    r = nper // tn

    def b_map(i, j, k, p):
        core = p[0] if own else 1 - p[0]
        return (k, (2 * (j // r) + core) * r + j % r)

    return _matmul(
        name, a, b, grid=(M // tm, 4 * r, L // tk),
        a_spec=pl.BlockSpec((tk, tm), lambda i, j, k, *_: (k, i)),
        b_spec=pl.BlockSpec((tk, tn), b_map),
        o_spec=pl.BlockSpec((None, tm, tn), lambda i, j, k, *_: (j // r, i, j % r)),
        out_shape=jax.ShapeDtypeStruct((4, M, nper), F32), dn=TN, pos=pos, copies=copies)


def mm_plain(name, a, b, dn, out_dtype=F32, res=None, copies=None):
    if dn == NN:
        (M, K), N = a.shape, b.shape[1]
    elif dn == NT:
        (M, K), N = a.shape, b.shape[0]
    else:
        (K, M), N = a.shape, b.shape[1]
    tm, tn, tk = _pick(M, 1024), _pick(N, 512), _pick(K, FULL_K)
    a_spec =(pl.BlockSpec((tk, tm), lambda i, j, k, *_: (k, i)) if dn == TN
              else pl.BlockSpec((tm, tk), lambda i, j, k, *_: (i, k)))
    b_spec = (pl.BlockSpec((tn, tk), lambda i, j, k, *_: (j, k)) if dn == NT
              else pl.BlockSpec((tk, tn), lambda i, j, k, *_: (k, j)))
    o_spec = pl.BlockSpec((tm, tn), lambda i, j, k, *_: (i, j))
    return _matmul(
        name, a, b, grid=(M // tm, N // tn, K // tk), a_spec=a_spec, b_spec=b_spec, o_spec=o_spec,
        out_shape=jax.ShapeDtypeStruct((M, N), out_dtype), dn=dn,
        res=res, res_spec=o_spec if res is not None else None, copies=copies)


def rms_fwd(name, x, g):
    L, D = x.shape
    tr = _pick(L, ROW_TILE)

    def body(x_ref, g_ref, h_ref):
        xv = x_ref[...]
        r = lax.rsqrt(jnp.mean(xv * xv, axis=-1, keepdims=True) + EPS)
        h_ref[...] = (xv * r * g_ref[...]).astype(BF16)

    return pl.pallas_call(
        body, name=name, grid=(L // tr,),
        in_specs=[pl.BlockSpec((tr, D), lambda i: (i, 0)), pl.BlockSpec((1, D), lambda i: (0, 0))],
        out_specs=pl.BlockSpec((tr, D), lambda i: (i, 0)),
        out_shape=jax.ShapeDtypeStruct((L, D), BF16),
        compiler_params=_cparams(("parallel",), VMEM_MID))(x, g)


def rms_bwd(name, x, dh, dres, g):
    L, D = x.shape
    tr = _pick(L, ROW_TILE)

    def body(x_ref, dh_ref, dr_ref, g_ref, dx_ref, dxb_ref, dg_ref):
        xv = x_ref[...]
        r = lax.rsqrt(jnp.mean(xv * xv, axis=-1, keepdims=True) + EPS)
        xh = xv * r
        dhv = dh_ref[...]
        dn = dhv * g_ref[...]
        dxv = dr_ref[...] + r * (dn - xh * jnp.mean(dn * xh, axis=-1, keepdims=True))
        dx_ref[...] = dxv
        dxb_ref[...] = dxv.astype(BF16)

        @pl.when(pl.program_id(0) == 0)
        def _():
            dg_ref[...] = jnp.zeros_like(dg_ref)

        dg_ref[...] += jnp.sum(dhv * xh, axis=0, keepdims=True)

    row = pl.BlockSpec((tr, D), lambda i: (i, 0))
    vec = pl.BlockSpec((1, D), lambda i: (0, 0))
    return pl.pallas_call(
        body, name=name, grid=(L // tr,), in_specs=[row, row, row, vec], out_specs=[row, row, vec],
        out_shape=[jax.ShapeDtypeStruct((L, D), F32), jax.ShapeDtypeStruct((L, D), BF16),
                   jax.ShapeDtypeStruct((1, D), F32)],
        compiler_params=_cparams(("arbitrary",), VMEM_MID))(x, dh, dres, g)


def loss_head(name, x, g, target):
    L, D = x.shape
    tr = _pick(L, ROW_TILE)

    def body(x_ref, g_ref, t_ref, loss_ref, dx_ref, dxb_ref, dg_ref):
        xv = x_ref[...]
        gv = g_ref[...]
        r = lax.rsqrt(jnp.mean(xv * xv, axis=-1, keepdims=True) + EPS)
        xh = xv * r
        err = xh * gv - t_ref[...]
        dy = err * (1.0 / D)
        dn = dy * gv
        dxv = r * (dn - xh * jnp.mean(dn * xh, axis=-1, keepdims=True))
        dx_ref[...] = dxv
        dxb_ref[...] = dxv.astype(BF16)

        @pl.when(pl.program_id(0) == 0)
        def _():
            dg_ref[...] = jnp.zeros_like(dg_ref)
            loss_ref[...] = jnp.zeros_like(loss_ref)

        dg_ref[...] += jnp.sum(dy * xh, axis=0, keepdims=True)
        row_loss = jnp.sum(err * err, axis=-1, keepdims=True) * (0.5 / D)
        loss_ref[...] += jnp.sum(row_loss, axis=0, keepdims=True)

    row = pl.BlockSpec((tr, D), lambda i: (i, 0))
    vec = pl.BlockSpec((1, D), lambda i: (0, 0))
    one = pl.BlockSpec((1, 1), lambda i: (0, 0))
    return pl.pallas_call(
        body, name=name, grid=(L // tr,), in_specs=[row, vec, row], out_specs=[one, row, row, vec],
        out_shape=[jax.ShapeDtypeStruct((1, 1), F32), jax.ShapeDtypeStruct((L, D), F32),
                   jax.ShapeDtypeStruct((L, D), BF16), jax.ShapeDtypeStruct((1, D), F32)],
        compiler_params=_cparams(("arbitrary",), VMEM_MID))(x, g, target)


def _branch_specs(D, tr):
    DP, DA, DS = D // 4, D // 2, D // 4
    return dict(
        pool=pl.BlockSpec((tr, DP), lambda i: (i, 0)),
        attn=pl.BlockSpec((tr, DA), lambda i: (i, 0)),
        glu=pl.BlockSpec((tr, 2 * DS), lambda i: (i, 0)),
        p_gate=pl.BlockSpec((tr, DP), lambda i: (i, 1)),
        a_gate=pl.BlockSpec((tr, DA), lambda i: (i, 4)),
        s_gate=pl.BlockSpec((tr, DS), lambda i: (i, 11)),
        bglu=pl.BlockSpec((1, 2 * DS), lambda i: (0, 0)),
        bg=pl.BlockSpec((1, D), lambda i: (0, 0)),
        row=pl.BlockSpec((tr, D), lambda i: (i, 0)),
    )


def branch_fwd(name, ypool, yattn, glu_pre, proj, b_glu, branch_g):
    L, DP = ypool.shape
    D = 4 * DP
    DA, DS = D // 2, D // 4
    tr = _pick(L, ROW_TILE)
    s = _branch_specs(D, tr)

    def body(yp_ref, ya_ref, gl_ref, pg_ref, ag_ref, sg_ref, bgl_ref, bg_ref, y_ref):
        pre = gl_ref[...] + bgl_ref[...]
        ys = pre[:, :DS] * _sigmoid(pre[:, DS:])
        bg = bg_ref[...]

        def one(raw, gate, g):
            r = lax.rsqrt(jnp.mean(raw * raw, axis=-1, keepdims=True) + EPS)
            return raw * r * g * (gate * _sigmoid(gate))

        y_ref[:, :DP] = one(yp_ref[...], pg_ref[...], bg[:, :DP]).astype(BF16)
        y_ref[:, DP:DP + DA] = one(ya_ref[...], ag_ref[...], bg[:, DP:DP + DA]).astype(BF16)
        y_ref[:, DP + DA:] = one(ys, sg_ref[...], bg[:, DP + DA:]).astype(BF16)

    return pl.pallas_call(
        body, name=name, grid=(L // tr,),
        in_specs=[s["pool"], s["attn"], s["glu"], s["p_gate"], s["a_gate"], s["s_gate"], s["bglu"], s["bg"]],
        out_specs=s["row"], out_shape=jax.ShapeDtypeStruct((L, D), BF16),
        compiler_params=_cparams(("parallel",), VMEM_MID))(ypool, yattn, glu_pre, proj, proj, proj, b_glu, branch_g)


def branch_bwd(name, dy, ypool, yattn, glu_pre, proj, b_glu, branch_g):
    L, DP = ypool.shape
    D = 4 * DP
    DA, DS = D // 2, D // 4
    tr = _pick(L, ROW_TILE // 2)
    s = _branch_specs(D, tr)

    def body(dy_ref, yp_ref, ya_ref, gl_ref, pg_ref, ag_ref, sg_ref, bgl_ref, bg_ref,
             dyp_ref, dya_ref, dgl_ref, dpg_ref, dag_ref, dsg_ref, dbg_ref, dbgl_ref):
        @pl.when(pl.program_id(0) == 0)
        def _():
            dbg_ref[...] = jnp.zeros_like(dbg_ref)
            dbgl_ref[...] = jnp.zeros_like(dbgl_ref)

        bg = bg_ref[...]

        def one(raw, gate, g, dyb):
            r = lax.rsqrt(jnp.mean(raw * raw, axis=-1, keepdims=True) + EPS)
            n = raw * r
            sg = _sigmoid(gate)
            sl = gate * sg
            dgate = dyb * n * g * (sg * (1.0 + gate * (1.0 - sg)))
            dbg = jnp.sum(dyb * n * sl, axis=0, keepdims=True)
            dn = dyb * g * sl
            draw = r * (dn - n * jnp.mean(dn * n, axis=-1, keepdims=True))
            return draw, dgate, dbg

        draw, dgate, dbg = one(yp_ref[...], pg_ref[...], bg[:, :DP], dy_ref[:, :DP])
        dyp_ref[...] = draw
        dpg_ref[...] = dgate.astype(BF16)
        dbg_ref[:, :DP] += dbg

        draw, dgate, dbg = one(ya_ref[...], ag_ref[...], bg[:, DP:DP + DA], dy_ref[:, DP:DP + DA])
        dya_ref[...] = draw
        dag_ref[...] = dgate.astype(BF16)
        dbg_ref[:, DP:DP + DA] += dbg

        pre = gl_ref[...] + bgl_ref[...]
        val = pre[:, :DS]
        sgt = _sigmoid(pre[:, DS:])
        draw, dgate, dbg = one(val * sgt, sg_ref[...], bg[:, DP + DA:], dy_ref[:, DP + DA:])
        dsg_ref[...] = dgate.astype(BF16)
        dbg_ref[:, DP + DA:] += dbg
        dval = draw * sgt
        dgt = draw * val * sgt * (1.0 - sgt)
        dgl_ref[:, :DS] = dval.astype(BF16)
        dgl_ref[:, DS:] = dgt.astype(BF16)
        dbgl_ref[:, :DS] += jnp.sum(dval, axis=0, keepdims=True)
        dbgl_ref[:, DS:] += jnp.sum(dgt, axis=0, keepdims=True)

    loc = lambda w: pl.BlockSpec((tr, w), lambda i: (i, 0))
    return pl.pallas_call(
        body, name=name, grid=(L // tr,),
        in_specs=[s["row"], s["pool"], s["attn"], s["glu"], s["p_gate"], s["a_gate"], s["s_gate"], s["bglu"], s["bg"]],
        out_specs=[loc(DP), loc(DA), loc(2 * DS), loc(DP), loc(DA), loc(DS), s["bg"], s["bglu"]],
        out_shape=[jax.ShapeDtypeStruct((L, DP), F32), jax.ShapeDtypeStruct((L, DA), F32),
                   jax.ShapeDtypeStruct((L, 2 * DS), BF16), jax.ShapeDtypeStruct((L, DP), BF16),
                   jax.ShapeDtypeStruct((L, DA), BF16), jax.ShapeDtypeStruct((L, DS), BF16),
                   jax.ShapeDtypeStruct((1, D), F32), jax.ShapeDtypeStruct((1, 2 * DS), F32)],
        compiler_params=_cparams(("arbitrary",), VMEM_BIG),
    )(dy, ypool, yattn, glu_pre, proj, proj, proj, b_glu, branch_g)


def _pool_select(g, s2, s4, s8, s16):
    return jnp.where(g == 0, s2, jnp.where(g == 1, s4, jnp.where(g == 2, s8, s16)))


def _pool_window(g):
    return jnp.where(g == 0, 2.0, jnp.where(g == 1, 4.0, jnp.where(g == 2, 8.0, 16.0))).astype(F32)


def _pooled_chunk(pad, g, r0, ch):
    xh = pad[pl.ds(r0, ch + POOL_HALO), :]
    s2 = xh + pltpu.roll(xh, 1, 0)
    s4 = s2 + pltpu.roll(s2, 2, 0)
    s8 = s4 + pltpu.roll(s4, 4, 0)
    s16 = s8 + pltpu.roll(s8, 8, 0)
    win = _pool_select(g, s2, s4, s8, s16)[POOL_HALO:]
    pos = (r0 + 1 + lax.broadcasted_iota(jnp.int32, (ch, 1), 0)).astype(F32)
    return win / jnp.minimum(pos, _pool_window(g)) - xh[POOL_HALO:]


def pool_fwd(name, proj, wp, scale):
    L = proj.shape[0]
    DP = scale.shape[1]
    PG = DP // N_POOL_GROUPS
    ch = _pick(L, 256)

    def body(x_ref, w_ref, s_ref, o_ref, pad):
        g = pl.program_id(0)
        pad[0:POOL_HALO, :] = jnp.zeros((POOL_HALO, PG), F32)
        pad[POOL_HALO:, :] = x_ref[...]

        def chunk(ci, carry):
            r0 = pl.multiple_of(ci * ch, ch)
            pooled = _pooled_chunk(pad, g, r0, ch)
            o_ref[pl.ds(r0, ch), :] = _dot(pooled.astype(BF16), w_ref[...]) * s_ref[...]
            return carry

        lax.fori_loop(0, L // ch, chunk, 0)

    return pl.pallas_call(
        body, name=name, grid=(N_POOL_GROUPS,),
        in_specs=[pl.BlockSpec((L, PG), lambda g: (0, g)), pl.BlockSpec((None, PG, PG), lambda g: (g, 0, 0)),
                  pl.BlockSpec((1, PG), lambda g: (0, g))],
        out_specs=pl.BlockSpec((L, PG), lambda g: (0, g)),
        out_shape=jax.ShapeDtypeStruct((L, DP), F32),
        scratch_shapes=[pltpu.VMEM((L + POOL_HALO, PG), F32)],
        compiler_params=_cparams(("parallel",), VMEM_MID))(proj, wp, scale)


def pool_bwd(name, dyraw, proj, wp, scale):
    L = proj.shape[0]
    DP = scale.shape[1]
    PG = DP // N_POOL_GROUPS
    ch = _pick(L, 256)

    def body(dy_ref, x_ref, w_ref, s_ref, dx_ref, dw_ref, ds_ref, pad, dpad, dpo):
        g = pl.program_id(0)
        pad[0:POOL_HALO, :] = jnp.zeros((POOL_HALO, PG), F32)
        pad[POOL_HALO:, :] = x_ref[...]
        dpad[L:, :] = jnp.zeros((POOL_HALO, PG), F32)
        dw_ref[...] = jnp.zeros_like(dw_ref)
        ds_ref[...] = jnp.zeros_like(ds_ref)
        wv = w_ref[...]
        win_f = _pool_window(g)

        def chunk(ci, carry):
            r0 = pl.multiple_of(ci * ch, ch)
            pooled = _pooled_chunk(pad, g, r0, ch).astype(BF16)
            dyv = dy_ref[pl.ds(r0, ch), :]
            ds_ref[...] += jnp.sum(dyv * _dot(pooled, wv), axis=0, keepdims=True)
            dmixed = (dyv * s_ref[...]).astype(BF16)
            dw_ref[...] += _dot(pooled, dmixed, TN)
            dpooled = _dot(dmixed, wv, NT)
            pos = (r0 + 1 + lax.broadcasted_iota(jnp.int32, (ch, 1), 0)).astype(F32)
            dpad[pl.ds(r0, ch), :] = dpooled / jnp.minimum(pos, win_f)
            dpo[pl.ds(r0, ch), :] = dpooled
            return carry

        lax.fori_loop(0, L // ch, chunk, 0)

        def chunk2(ci, carry):
            r0 = pl.multiple_of(ci * ch, ch)
            n = ch + POOL_HALO
            dm = dpad[pl.ds(r0, n), :]
            s2 = dm + pltpu.roll(dm, n - 1, 0)
            s4 = s2 + pltpu.roll(s2, n - 2, 0)
            s8 = s4 + pltpu.roll(s4, n - 4, 0)
            s16 = s8 + pltpu.roll(s8, n - 8, 0)
            win = _pool_select(g, s2, s4, s8, s16)[:ch]
            dx_ref[pl.ds(r0, ch), :] = (win - dpo[pl.ds(r0, ch), :]).astype(BF16)
            return carry

        lax.fori_loop(0, L // ch, chunk2, 0)

    col = pl.BlockSpec((L, PG), lambda g: (0, g))
    return pl.pallas_call(
        body, name=name, grid=(N_POOL_GROUPS,),
        in_specs=[col, col, pl.BlockSpec((None, PG, PG), lambda g: (g, 0, 0)), pl.BlockSpec((1, PG), lambda g: (0, g))],
        out_specs=[col, pl.BlockSpec((None, PG, PG), lambda g: (g, 0, 0)), pl.BlockSpec((1, PG), lambda g: (0, g))],
        out_shape=[jax.ShapeDtypeStruct((L, DP), BF16), jax.ShapeDtypeStruct((N_POOL_GROUPS, PG, PG), F32),
                   jax.ShapeDtypeStruct((1, DP), F32)],
        scratch_shapes=[pltpu.VMEM((L + POOL_HALO, PG), F32), pltpu.VMEM((L + POOL_HALO, PG), F32),
                        pltpu.VMEM((L, PG), F32)],
        compiler_params=_cparams(("parallel",), VMEM_MID))(dyraw, proj, wp, scale)


def _attn_tile(L):
    return _pick(L, ATTN_TILE)


def _tri(t, strict):
    j = lax.broadcasted_iota(jnp.int32, (t, t), 0)
    s = lax.broadcasted_iota(jnp.int32, (t, t), 1)
    return ((j > s) if strict else (j >= s)).astype(BF16)


def _attn_block(q, kt, i, k0, rb, after):
    tq, tk = q.shape[0], kt.shape[0]
    row = lax.broadcasted_iota(jnp.int32, (tq, tk), 0)
    col = lax.broadcasted_iota(jnp.int32, (tq, tk), 1)
    causal = (k0 + col) < (i * tq + row)
    z = _dot(q, kt, NT)
    e = jnp.exp(-jnp.abs(z))
    l1p = jnp.log(1.0 + e)
    log_sig = jnp.minimum(z, 0.0) - l1p
    log_1m = -jnp.maximum(z, 0.0) - l1p
    b = jnp.where(causal, log_1m, 0.0)
    b_hi = b.astype(BF16)
    b_lo = (b - b_hi.astype(F32)).astype(BF16)
    suffix = _dot(b_hi, after) + _dot(b_lo, after) + rb
    w = jnp.where(causal, jnp.exp(log_sig + suffix), 0.0)
    return z, e, causal, b, w


def attn_fwd(name, proj, D, copies=None):
    L = proj.shape[0]
    DA = D // 2
    H = DA // HEAD_DIM
    tq = tk = _attn_tile(L)
    qo, ko, vo = (D // 2) // HEAD_DIM, D // HEAD_DIM, (3 * D // 2) // HEAD_DIM
    scale = HEAD_DIM ** -0.5

    def body(q_ref, k_ref, v_ref, tri_ref, o_ref, kb_s, vb_s, acc, rb):
        i = pl.program_id(1)

        @pl.when(i == 0)
        def _():
            kb_s[...] = k_ref[...].astype(BF16)
            vb_s[...] = v_ref[...].astype(BF16)

        q = (q_ref[...] * scale).astype(BF16)
        acc[...] = jnp.zeros_like(acc)
        rb[...] = jnp.zeros_like(rb)

        def cond(c):
            return jnp.logical_and(c[0] >= 0, c[1])

        def step(c):
            kb = c[0]
            k0 = pl.multiple_of(kb * tk, tk)
            kt = kb_s[pl.ds(k0, tk), :]
            vt = vb_s[pl.ds(k0, tk), :]
            _, _, _, b, w = _attn_block(q, kt, i, k0, rb[...], tri_ref[...])
            acc[...] += _dot(w.astype(BF16), vt)
            rbn = rb[...] + jnp.sum(b, axis=1, keepdims=True)
            rb[...] = rbn
            return kb - 1, jnp.max(rbn) > -ATTN_DECAY_CUTOFF

        lax.while_loop(cond, step, (i, jnp.bool_(True)))
        o_ref[...] = acc[...]

    (out,), landed = _call(
        body, name=name, grid=(H, L // tq),
        in_specs=[pl.BlockSpec((tq, HEAD_DIM), lambda h, i: (i, qo + h)),
                  pl.BlockSpec((L, HEAD_DIM), lambda h, i: (0, ko + h)),
                  pl.BlockSpec((L, HEAD_DIM), lambda h, i: (0, vo + h)),
                  pl.BlockSpec((tk, tk), lambda h, i: (0, 0))],
        out_specs=[pl.BlockSpec((tq, HEAD_DIM), lambda h, i: (i, h))],
        out_shape=[jax.ShapeDtypeStruct((L, DA), F32)],
        scratch_shapes=[pltpu.VMEM((L, HEAD_DIM), BF16), pltpu.VMEM((L, HEAD_DIM), BF16),
                        pltpu.VMEM((tq, HEAD_DIM), F32), pltpu.VMEM((tq, 1), F32)],
        vmem=VMEM_MID, args=(proj, proj, proj, _tri(tk, True)), semantics=("arbitrary", "arbitrary"),
        copies=copies)
    return out, landed


def attn_bwd(name, proj, o, do, D, copies=None):
    L = proj.shape[0]
    DA = D // 2
    H = DA // HEAD_DIM
    tq = tk = _attn_tile(L)
    qo, ko, vo = (D // 2) // HEAD_DIM, D // HEAD_DIM, (3 * D // 2) // HEAD_DIM
    scale = HEAD_DIM ** -0.5

    def body(q_ref, k_ref, v_ref, o_ref, do_ref, after_ref, from_ref, dq_ref, dk_ref, dv_ref,
             kb_s, vb_s, dk_s, dv_s, dq_acc, rb, rg):
        i = pl.program_id(1)
        nq = pl.num_programs(1)

        @pl.when(i == 0)
        def _():
            kb_s[...] = k_ref[...].astype(BF16)
            vb_s[...] = v_ref[...].astype(BF16)
            dk_s[...] = jnp.zeros_like(dk_s)
            dv_s[...] = jnp.zeros_like(dv_s)

        q = (q_ref[...] * scale).astype(BF16)
        dob = do_ref[...].astype(BF16)
        delta = jnp.sum(dob.astype(F32) * o_ref[...], axis=1, keepdims=True)
        dq_acc[...] = jnp.zeros_like(dq_acc)
        rb[...] = jnp.zeros_like(rb)
        rg[...] = jnp.zeros_like(rg)

        def cond(c):
            return jnp.logical_and(c[0] >= 0, c[1])

        def step(c):
            kb = c[0]
            k0 = pl.multiple_of(kb * tk, tk)
            kt = kb_s[pl.ds(k0, tk), :]
            vt = vb_s[pl.ds(k0, tk), :]
            z, e, causal, b, w = _attn_block(q, kt, i, k0, rb[...], after_ref[...])
            wq = w.astype(BF16)
            dw = _dot(dob, vt, NT)
            g = wq.astype(F32) * dw
            g_hi = g.astype(BF16)
            g_lo = (g - g_hi.astype(F32)).astype(BF16)
            from_s = from_ref[...]
            suffix_g = _dot(g_hi, from_s) + _dot(g_lo, from_s) + rg[...]
            before = delta - suffix_g
            r = 1.0 / (1.0 + e)
            sig = jnp.where(z >= 0, r, e * r)
            sig_neg = jnp.where(z >= 0, e * r, r)
            dz = jnp.where(causal, g * sig_neg - before * sig, 0.0).astype(BF16)
            dq_acc[...] += _dot(dz, kt)
            dk_s[pl.ds(k0, tk), :] += _dot(dz, q, TN)
            dv_s[pl.ds(k0, tk), :] += _dot(wq, dob, TN)
            rbn = rb[...] + jnp.sum(b, axis=1, keepdims=True)
            rb[...] = rbn
            rg[...] += jnp.sum(g, axis=1, keepdims=True)
            return kb - 1, jnp.max(rbn) > -ATTN_DECAY_CUTOFF

        lax.while_loop(cond, step, (i, jnp.bool_(True)))
        dq_ref[...] = (dq_acc[...] * scale).astype(BF16)

        @pl.when(i == nq - 1)
        def _():
            dk_ref[...] = dk_s[...].astype(BF16)
            dv_ref[...] = dv_s[...].astype(BF16)

    blk = pl.BlockSpec((tq, HEAD_DIM), lambda h, i: (i, h))
    full = pl.BlockSpec((L, HEAD_DIM), lambda h, i: (0, h))
    return _call(
        body, name=name, grid=(H, L // tq),
        in_specs=[pl.BlockSpec((tq, HEAD_DIM), lambda h, i: (i, qo + h)),
                  pl.BlockSpec((L, HEAD_DIM), lambda h, i: (0, ko + h)),
                  pl.BlockSpec((L, HEAD_DIM), lambda h, i: (0, vo + h)), blk, blk,
                  pl.BlockSpec((tk, tk), lambda h, i: (0, 0)), pl.BlockSpec((tk, tk), lambda h, i: (0, 0))],
        out_specs=[blk, full, full],
        out_shape=[jax.ShapeDtypeStruct((L, DA), BF16)] * 3,
        scratch_shapes=[pltpu.VMEM((L, HEAD_DIM), BF16), pltpu.VMEM((L, HEAD_DIM), BF16),
                        pltpu.VMEM((L, HEAD_DIM), F32), pltpu.VMEM((L, HEAD_DIM), F32),
                        pltpu.VMEM((tq, HEAD_DIM), F32), pltpu.VMEM((tq, 1), F32), pltpu.VMEM((tq, 1), F32)],
        vmem=VMEM_MID, args=(proj, proj, proj, o, do, _tri(tk, True), _tri(tk, False)),
        semantics=("arbitrary", "arbitrary"), copies=copies)


def _cmul(ar, ai, br, bi):
    return ar * br - ai * bi, ar * bi + ai * br


def _cmul_conj(ar, ai, br, bi):
    return ar * br + ai * bi, ar * bi - ai * br


def _ssm_disc(lr, li, ld):
    dt = jnp.exp(ld)
    m = jnp.exp(lr * dt)
    ar, ai = m * jnp.cos(li * dt), m * jnp.sin(li * dt)
    inv = 1.0 / (lr * lr + li * li)
    fr, fi = _cmul(ar - 1.0, ai, lr * inv, -li * inv)
    return dt, ar, ai, fr, fi, inv


def ssm_prep(name, lr, li, ld, br, bi):
    def body(lr_ref, li_ref, ld_ref, br_ref, bi_ref, zr_ref, zi_ref, bbr_ref, bbi_ref):
        dt, _, _, fr, fi, _ = _ssm_disc(lr_ref[...], li_ref[...], ld_ref[...])
        zr_ref[...] = lr_ref[...] * dt
        zi_ref[...] = li_ref[...] * dt
        bbr, bbi = _cmul(fr, fi, br_ref[...], bi_ref[...])
        bbr_ref[...] = bbr
        bbi_ref[...] = bbi

    sd = jax.ShapeDtypeStruct
    return pl.pallas_call(
        body, name=name,
        out_shape=[sd(lr.shape, F32), sd(lr.shape, F32), sd(br.shape, F32), sd(br.shape, F32)],
    )(lr, li, ld, br, bi)


def ssm_prep_bwd(name, lr, li, ld, br, bi, gar, gai, gbr, gbi):
    def body(lr_ref, li_ref, ld_ref, br_ref, bi_ref, gar_ref, gai_ref, gbr_ref, gbi_ref,
             dlr_ref, dli_ref, dld_ref, dbr_ref, dbi_ref):
        lr_, li_ = lr_ref[...], li_ref[...]
        dt, ar, ai, fr, fi, inv = _ssm_disc(lr_, li_, ld_ref[...])
        gbr_, gbi_ = gbr_ref[...], gbi_ref[...]
        dbr, dbi = _cmul_conj(fr, fi, gbr_, gbi_)
        dbr_ref[...] = dbr
        dbi_ref[...] = dbi
        pr, pi = _cmul_conj(br_ref[...], bi_ref[...], gbr_, gbi_)
        gfr = jnp.sum(pr, axis=1, keepdims=True)
        gfi = jnp.sum(pi, axis=1, keepdims=True)
        ilr, ili = lr_ * inv, -li_ * inv
        tr_, ti_ = _cmul_conj(ilr, ili, gfr, gfi)
        gatr, gati = gar_ref[...] + tr_, gai_ref[...] + ti_
        hr, hi = _cmul(fr, fi, ilr, ili)
        t1r, t1i = _cmul_conj(ar * dt, ai * dt, gatr, gati)
        t2r, t2i = _cmul_conj(hr, hi, gfr, gfi)
        dlr_ref[...] = t1r - t2r
        dli_ref[...] = t1i - t2i
        lar, lai = _cmul(lr_, li_, ar, ai)
        gdt, _ = _cmul_conj(lar, lai, gatr, gati)
        dld_ref[...] = jnp.sum(gdt, axis=2, keepdims=True) * dt

    sd = jax.ShapeDtypeStruct
    return pl.pallas_call(
        body, name=name,
        out_shape=[sd(lr.shape, F32), sd(lr.shape, F32), sd(ld.shape, F32), sd(br.shape, F32), sd(br.shape, F32)],
    )(lr, li, ld, br, bi, gar, gai, gbr, gbi)


SCAN_ROWS = 64


def _scan_rows(L):
    return min(SCAN_ROWS, L)


def _power_table(pr_s, pi_s, zr, zi, L, reverse):
    R = _scan_rows(L)
    row = lax.broadcasted_iota(jnp.int32, (R, 1), 0).astype(F32)
    dist = (R - row) if reverse else (row + 1.0)
    mag = jnp.exp(dist * zr)
    pr_s[...] = mag * jnp.cos(dist * zi)
    pi_s[...] = mag * jnp.sin(dist * zi)


def _scan(xr, xi, pr_s, pi_s, L, reverse):
    R = _scan_rows(L)
    nt = L // R
    assert L % R == 0 and R & (R - 1) == 0
    ns = CHUNK_X // LANE
    ridx = lax.broadcasted_iota(jnp.int32, (R, LANE), 0)

    def power(ref, d, cs):
        at = R - d if reverse else d - 1
        return ref[at:at + 1, cs]

    def shift(v, d):
        if d < 8:
            if reverse:
                return jnp.where(ridx < R - d, pltpu.roll(v, R - d, 0), 0.0)
            return jnp.where(ridx >= d, pltpu.roll(v, d, 0), 0.0)
        zeros = jnp.zeros((d, LANE), F32)
        return jnp.concatenate([v[d:], zeros], 0) if reverse else jnp.concatenate([zeros, v[:R - d]], 0)

    def tile(n, carry):
        t = nt - 1 - n if reverse else n
        rows = pl.ds(pl.multiple_of(t * R, R), R)
        edges = []
        for c in range(ns):
            cs = slice(c * LANE, (c + 1) * LANE)
            vr, vi = xr[rows, cs], xi[rows, cs]
            d = 1
            while d < R:
                ar, ai = power(pr_s, d, cs), power(pi_s, d, cs)
                sr, si = shift(vr, d), shift(vi, d)
                vr, vi = vr + ar * sr - ai * si, vi + ar * si + ai * sr
                d *= 2
            cr, ci = carry[2 * c], carry[2 * c + 1]
            pr, pi = pr_s[:, cs], pi_s[:, cs]
            vr, vi = vr + pr * cr - pi * ci, vi + pr * ci + pi * cr
            xr[rows, cs] = vr
            xi[rows, cs] = vi
            edge = slice(0, 1) if reverse else slice(R - 1, R)
            edges += [vr[edge], vi[edge]]
        return tuple(edges)

    lax.fori_loop(0, nt, tile, tuple(jnp.zeros((1, LANE), F32) for _ in range(2 * ns)))


def _gelu(x):
    t = jnp.tanh(0.7978845608028654 * (x + 0.044715 * x * x * x))
    return 0.5 * x * (1.0 + t)


def _gelu_grad(x):
    t = jnp.tanh(0.7978845608028654 * (x + 0.044715 * x * x * x))
    return 0.5 * (1.0 + t) + 0.5 * x * (1.0 - t * t) * 0.7978845608028654 * (1.0 + 0.134145 * x * x)


def _call(body, *, name, grid, in_specs, out_specs, out_shape, scratch_shapes, vmem, args, semantics,
          copies=None):
    n_i, n_o, n_s = len(in_specs), len(out_specs), len(scratch_shapes)
    if copies is None:
        out = pl.pallas_call(
            body, name=name, grid=grid, in_specs=in_specs, out_specs=out_specs, out_shape=out_shape,
            scratch_shapes=scratch_shapes, compiler_params=_cparams(semantics, vmem))(*args)
        return list(out), []
    n_ci, n_co = len(copies.inputs), len(copies.out_shape)

    def hosted(*refs):
        ins, cin = refs[:n_i], refs[n_i:n_i + n_ci]
        outs = refs[n_i + n_ci:n_i + n_ci + n_o]
        cout = refs[n_i + n_ci + n_o:n_i + n_ci + n_o + n_co]
        scr = refs[n_i + n_ci + n_o + n_co:n_i + n_ci + n_o + n_co + n_s]
        sems = refs[n_i + n_ci + n_o + n_co + n_s:]
        step = pl.program_id(0)
        for axis in range(1, len(grid)):
            step = step * grid[axis] + pl.program_id(axis)
        copies.emit(cin, cout, sems, step, math.prod(grid))
        body(*ins, *outs, *scr)

    out = pl.pallas_call(
        hosted, name=name, grid=grid, in_specs=list(in_specs) + [HBM] * n_ci,
        out_specs=list(out_specs) + [HBM] * n_co, out_shape=list(out_shape) + list(copies.out_shape),
        scratch_shapes=list(scratch_shapes) + list(copies.scratch),
        compiler_params=pltpu.CompilerParams(dimension_semantics=("arbitrary",) * len(grid),
                                             vmem_limit_bytes=vmem, has_side_effects=True))(*args, *copies.inputs)
    return list(out[:n_o]), list(out[n_o:])


def merge_copies(group):
    group = [c for c in group if c is not None]
    if len(group) <= 1:
        return group[0] if group else None
    bounds, i0, o0, s0 = [], 0, 0, 0
    for c in group:
        bounds.append((i0, o0, s0))
        i0, o0, s0 = i0 + len(c.inputs), o0 + len(c.out_shape), s0 + len(c.scratch)
    phases = []
    for c, (i, o, s) in zip(group, bounds):
        for frac, fn in c.phases:
            def shifted(ins, outs, sems, fn=fn, c=c, i=i, o=o, s=s):
                fn(ins[i:i + len(c.inputs)], outs[o:o + len(c.out_shape)], sems[s:s + len(c.scratch)])
            phases.append((frac, shifted))
    return HostedCopies([a for c in group for a in c.inputs], [a for c in group for a in c.out_shape],
                        [a for c in group for a in c.scratch], phases)


def ssm_fwd(name, proj, wbr, wbi, zr, zi, wcr, wci, dskip, D, copies=None):
    L = proj.shape[0]
    DS = D // 4
    NC = DS // CHUNK_U
    uo = (5 * D // 2) // CHUNK_U
    ch = _pick(L, 256)

    def body(u_ref, wbr_ref, wbi_ref, zr_ref, zi_ref, wcr_ref, wci_ref, ds_ref,
             y_ref, hg_ref, xr_ref, xi_ref, sr, si, pr_s, pi_s):
        def fill(ci, carry):
            rows = pl.ds(pl.multiple_of(ci * ch, ch), ch)
            ub = u_ref[rows, :].astype(BF16)
            sr[rows, :] = _dot(ub, wbr_ref[...])
            si[rows, :] = _dot(ub, wbi_ref[...])
            return carry

        lax.fori_loop(0, L // ch, fill, 0)
        _power_table(pr_s, pi_s, zr_ref[...], zi_ref[...], L, reverse=False)
        _scan(sr, si, pr_s, pi_s, L, reverse=False)

        def emit(ci, carry):
            rows = pl.ds(pl.multiple_of(ci * ch, ch), ch)
            xrb, xib = sr[rows, :].astype(BF16), si[rows, :].astype(BF16)
            xr_ref[rows, :] = xrb
            xi_ref[rows, :] = xib
            y = _dot(xrb, wcr_ref[...]) - _dot(xib, wci_ref[...]) + ds_ref[...] * u_ref[rows, :]
            y_ref[rows, :] = y
            hg_ref[rows, :] = _gelu(y).astype(BF16)
            return carry

        lax.fori_loop(0, L // ch, emit, 0)

    ucol = pl.BlockSpec((L, CHUNK_U), lambda k: (0, k))
    xcol = pl.BlockSpec((L, CHUNK_X), lambda k: (0, k))
    sd = jax.ShapeDtypeStruct
    return _call(
        body, name=name, grid=(NC,),
        in_specs=[pl.BlockSpec((L, CHUNK_U), lambda k: (0, uo + k)),
                  pl.BlockSpec((None, CHUNK_U, CHUNK_X), lambda k: (k, 0, 0)),
                  pl.BlockSpec((None, CHUNK_U, CHUNK_X), lambda k: (k, 0, 0)),
                  pl.BlockSpec((1, CHUNK_X), lambda k: (0, k)), pl.BlockSpec((1, CHUNK_X), lambda k: (0, k)),
                  pl.BlockSpec((None, CHUNK_X, CHUNK_U), lambda k: (k, 0, 0)),
                  pl.BlockSpec((None, CHUNK_X, CHUNK_U), lambda k: (k, 0, 0)),
                  pl.BlockSpec((1, CHUNK_U), lambda k: (0, k))],
        out_specs=[ucol, ucol, xcol, xcol],
        out_shape=[sd((L, DS), F32), sd((L, DS), BF16), sd((L, 4 * DS), BF16), sd((L, 4 * DS), BF16)],
        scratch_shapes=[pltpu.VMEM((L, CHUNK_X), F32), pltpu.VMEM((L, CHUNK_X), F32),
                        pltpu.VMEM((_scan_rows(L), CHUNK_X), F32), pltpu.VMEM((_scan_rows(L), CHUNK_X), F32)],
        vmem=VMEM_BIG, args=(proj, wbr, wbi, zr, zi, wcr, wci, dskip), semantics=("parallel",), copies=copies)


def ssm_bwd(name, dhg, ypre, proj, xr, xi, wbr, wbi, zr, zi, wcr, wci, dskip, D, copies=None):
    L = proj.shape[0]
    DS = D // 4
    NC = DS // CHUNK_U
    uo = (5 * D // 2) // CHUNK_U
    ch = _pick(L, 256)
    nch = L // ch
    halo = 16

    def body(dhg_ref, y_ref, u_ref, xr_ref, xi_ref, wbr_ref, wbi_ref, zr_ref, zi_ref, wcr_ref, wci_ref,
             ds_ref, du_ref, dwcr_ref, dwci_ref, dwbr_ref, dwbi_ref, dar_ref, dai_ref, dds_ref,
             gr, gi, duf, pr_s, pi_s):
        dwcr_ref[...] = jnp.zeros_like(dwcr_ref)
        dwci_ref[...] = jnp.zeros_like(dwci_ref)
        dwbr_ref[...] = jnp.zeros_like(dwbr_ref)
        dwbi_ref[...] = jnp.zeros_like(dwbi_ref)
        dar_ref[...] = jnp.zeros_like(dar_ref)
        dai_ref[...] = jnp.zeros_like(dai_ref)
        dds_ref[...] = jnp.zeros_like(dds_ref)

        def first(ci, carry):
            rows = pl.ds(pl.multiple_of(ci * ch, ch), ch)
            dy = dhg_ref[rows, :] * _gelu_grad(y_ref[rows, :])
            dyb = dy.astype(BF16)
            dds_ref[...] += jnp.sum(dy * u_ref[rows, :], axis=0, keepdims=True)
            duf[rows, :] = ds_ref[...] * dy
            gr[rows, :] = _dot(dyb, wcr_ref[...], NT)
            gi[rows, :] = -_dot(dyb, wci_ref[...], NT)
            dwcr_ref[...] += _dot(xr_ref[rows, :], dyb, TN)
            dwci_ref[...] -= _dot(xi_ref[rows, :], dyb, TN)
            return carry

        lax.fori_loop(0, nch, first, 0)
        _power_table(pr_s, pi_s, zr_ref[...], -zi_ref[...], L, reverse=True)
        _scan(gr, gi, pr_s, pi_s, L, reverse=True)

        def lam_grad(gxr, gxi, xpr, xpi):
            pr, pi = _cmul_conj(xpr, xpi, gxr, gxi)
            dar_ref[...] += jnp.sum(pr, axis=0, keepdims=True)
            dai_ref[...] += jnp.sum(pi, axis=0, keepdims=True)

        def second(ci, carry):
            r0 = pl.multiple_of(ci * ch, ch)
            rows = pl.ds(r0, ch)
            gxr, gxi = gr[rows, :], gi[rows, :]
            gxrb, gxib = gxr.astype(BF16), gxi.astype(BF16)
            du_ref[rows, :] = (duf[rows, :] + _dot(gxrb, wbr_ref[...], NT) + _dot(gxib, wbi_ref[...], NT)).astype(BF16)
            ub = u_ref[rows, :].astype(BF16)
            dwbr_ref[...] += _dot(ub, gxrb, TN)
            dwbi_ref[...] += _dot(ub, gxib, TN)
            return carry

        lax.fori_loop(0, nch, second, 0)

        ridx = lax.broadcasted_iota(jnp.int32, (ch, CHUNK_X), 0)
        xpr = jnp.where(ridx >= 1, pltpu.roll(xr_ref[0:ch, :].astype(F32), 1, 0), 0.0)
        xpi = jnp.where(ridx >= 1, pltpu.roll(xi_ref[0:ch, :].astype(F32), 1, 0), 0.0)
        lam_grad(gr[0:ch, :], gi[0:ch, :], xpr, xpi)

        def third(ci, carry):
            r0 = pl.multiple_of(ci * ch, ch)
            ext = pl.ds(pl.multiple_of(r0 - halo, halo), ch + halo)
            xpr = pltpu.roll(xr_ref[ext, :].astype(F32), 1, 0)[halo:]
            xpi = pltpu.roll(xi_ref[ext, :].astype(F32), 1, 0)[halo:]
            lam_grad(gr[pl.ds(r0, ch), :], gi[pl.ds(r0, ch), :], xpr, xpi)
            return carry

        if nch > 1:
            lax.fori_loop(1, nch, third, 0)

    ucol = pl.BlockSpec((L, CHUNK_U), lambda k: (0, k))
    xcol = pl.BlockSpec((L, CHUNK_X), lambda k: (0, k))
    wb_spec = pl.BlockSpec((None, CHUNK_U, CHUNK_X), lambda k: (k, 0, 0))
    wc_spec = pl.BlockSpec((None, CHUNK_X, CHUNK_U), lambda k: (k, 0, 0))
    avec = pl.BlockSpec((1, CHUNK_X), lambda k: (0, k))
    uvec = pl.BlockSpec((1, CHUNK_U), lambda k: (0, k))
    sd = jax.ShapeDtypeStruct
    return _call(
        body, name=name, grid=(NC,),
        in_specs=[ucol, ucol, pl.BlockSpec((L, CHUNK_U), lambda k: (0, uo + k)), xcol, xcol,
                  wb_spec, wb_spec, avec, avec, wc_spec, wc_spec, uvec],
        out_specs=[ucol, wc_spec, wc_spec, wb_spec, wb_spec, avec, avec, uvec],
        out_shape=[sd((L, DS), BF16), sd((NC, CHUNK_X, CHUNK_U), F32), sd((NC, CHUNK_X, CHUNK_U), F32),
                   sd((NC, CHUNK_U, CHUNK_X), F32), sd((NC, CHUNK_U, CHUNK_X), F32),
                   sd((1, 4 * DS), F32), sd((1, 4 * DS), F32), sd((1, DS), F32)],
        scratch_shapes=[pltpu.VMEM((L, CHUNK_X), F32), pltpu.VMEM((L, CHUNK_X), F32), pltpu.VMEM((L, CHUNK_U), F32),
                        pltpu.VMEM((_scan_rows(L), CHUNK_X), F32), pltpu.VMEM((_scan_rows(L), CHUNK_X), F32)],
        vmem=VMEM_BIG, args=(dhg, ypre, proj, xr, xi, wbr, wbi, zr, zi, wcr, wci, dskip),
        semantics=("parallel",), copies=copies)


def _block_diag(w, transpose):
    G = w.shape[0]
    nc = G // GROUPS_PER_CHUNK
    w4 = w.reshape(nc, GROUPS_PER_CHUNK, SSM_GROUP, SSM_STATE)
    eye = jnp.eye(GROUPS_PER_CHUNK, dtype=w.dtype)
    if transpose:
        return (w4[:, None, :, :, :].transpose(0, 1, 4, 2, 3) * eye[None, :, None, :, None]).reshape(
            nc, CHUNK_X, CHUNK_U).astype(BF16)
    return (w4[:, :, :, None, :] * eye[None, :, None, :, None]).reshape(nc, CHUNK_U, CHUNK_X).astype(BF16)


def _diag_blocks(dw, transpose):
    nc = dw.shape[0]
    gpc = GROUPS_PER_CHUNK
    if transpose:
        d5 = dw.reshape(nc, gpc, SSM_STATE, gpc, SSM_GROUP)
        blocks = [d5[:, g, :, g, :] for g in range(gpc)]
        return jnp.stack(blocks, axis=1).transpose(0, 1, 3, 2).reshape(nc * gpc, SSM_GROUP, SSM_STATE)
    d5 = dw.reshape(nc, gpc, SSM_GROUP, gpc, SSM_STATE)
    blocks = [d5[:, g, :, g, :] for g in range(gpc)]
    return jnp.stack(blocks, axis=1).reshape(nc * gpc, SSM_GROUP, SSM_STATE)


SHARD_BLOCK_ELEMS = 128 * 1024


def _shard_rows(R, C, scale):
    return _pick(R, max(8, scale * SHARD_BLOCK_ELEMS // C))


def cast_bf16(name, w, layer, parts=1):
    shape = w.shape[1:]
    w3 = w.reshape(w.shape[0], -1, shape[-1])
    _, R, C = w3.shape
    tr = _shard_rows(R, C, 4)
    cw = C // parts

    def body(w_ref, *o_refs):
        for g, o_ref in enumerate(o_refs):
            o_ref[...] = w_ref[:, g * cw:(g + 1) * cw].astype(BF16)

    out = pl.pallas_call(body, name=name, grid=(R // tr,),
                         in_specs=[pl.BlockSpec((None, tr, C), lambda i: (layer, i, 0))],
                         out_specs=[pl.BlockSpec((tr, cw), lambda i: (i, 0))] * parts,
                         out_shape=[jax.ShapeDtypeStruct((R, cw), BF16)] * parts,
                         compiler_params=_cparams(("parallel",), VMEM_MID))(w3)
    return [o.reshape(shape[:-1] + (cw,)) for o in out]


def _adamw(w, g, m, v):
    m = ADAM_B1 * m + (1.0 - ADAM_B1) * g
    v = ADAM_B2 * v + (1.0 - ADAM_B2) * (g * g)
    delta = -ADAM_LR * ((m * ADAM_C1) / (jnp.sqrt(v * ADAM_C2) + ADAM_EPS) + ADAM_WD * w)
    return delta, m, v


def _own_core(g4):
    return (lambda p: p[0]) if g4.shape[1] == 2 else (lambda p: 0)


def chip_partial(name, pos, g4, recv_a):
    _, _, R, C = g4.shape
    tr = _shard_rows(R, C, 4)
    core = _own_core(g4)

    def body(pos_ref, g_ref, a_ref, o_ref):
        o_ref[...] = (g_ref[...] + a_ref[...]).astype(BF16)

    return pl.pallas_call(
        body, name=name,
        grid_spec=pltpu.PrefetchScalarGridSpec(
            num_scalar_prefetch=1, grid=(4, R // tr),
            in_specs=[pl.BlockSpec((None, None, tr, C), lambda q, i, p: (q, core(p), i, 0)),
                      pl.BlockSpec((None, tr, C), lambda q, i, p: (q, i, 0))],
            out_specs=pl.BlockSpec((None, tr, C), lambda q, i, p: (q, i, 0))),
        out_shape=jax.ShapeDtypeStruct((4, R, C), BF16),
        compiler_params=_cparams(("parallel", "parallel"), VMEM_MID))(pos, g4, recv_a)


def adamw_shard(name, pos, layer, g4, recv_a, recv_b, w, m, v, prev):
    _, _, R, C = g4.shape
    tr = _shard_rows(R, C, 1)
    n_prev = 0 if prev is None else 4
    core = _own_core(g4)

    def body(pos_ref, g_ref, a_ref, b_ref, w_ref, m_ref, v_ref, *rest):
        go_ref, d_ref, mo_ref, vo_ref = rest[n_prev:]
        gs = g_ref[...] + a_ref[...]
        for j in range(3):
            gs = gs + b_ref[j].astype(F32)
        delta, mn, vn = _adamw(w_ref[...], gs, m_ref[...], v_ref[...])
        go_ref[...] = gs
        d_ref[...] = delta
        mo_ref[...] = mn
        vo_ref[...] = vn

    lay = pl.BlockSpec((None, tr, C), lambda i, p: (layer, i, 0))
    in_specs = [pl.BlockSpec((None, None, tr, C), lambda i, p: (p[1], core(p), i, 0)),
                pl.BlockSpec((None, tr, C), lambda i, p: (p[1], i, 0)),
                pl.BlockSpec((3, tr, C), lambda i, p: (0, i, 0)), lay, lay, lay]
    args = [g4, recv_a, recv_b, w, m, v]
    aliases = {}
    if prev is not None:
        in_specs += [pl.BlockSpec(memory_space=pl.ANY)] * 4
        args += list(prev)
        aliases = {7 + j: j for j in range(4)}
    return pl.pallas_call(
        body, name=name,
        grid_spec=pltpu.PrefetchScalarGridSpec(
            num_scalar_prefetch=1, grid=(R // tr,), in_specs=in_specs, out_specs=[lay] * 4),
        out_shape=[jax.ShapeDtypeStruct(w.shape, F32)] * 4,
        input_output_aliases=aliases,
        compiler_params=_cparams(("parallel",), VMEM_MID))(pos, *args)


def adamw_small(name, gathered, w, m, v):
    _, R, C = gathered.shape
    tr = _pick(R, 512)

    def body(g_ref, w_ref, m_ref, v_ref, go_ref, d_ref, mo_ref, vo_ref):
        gs = g_ref[0]
        for j in range(1, N_DEV):
            gs = gs + g_ref[j]
        delta, mn, vn = _adamw(w_ref[...], gs, m_ref[...], v_ref[...])
        go_ref[...] = gs
        d_ref[...] = delta
        mo_ref[...] = mn
        vo_ref[...] = vn

    spec = pl.BlockSpec((tr, C), lambda i: (i, 0))
    return pl.pallas_call(
        body, name=name, grid=(R // tr,),
        in_specs=[pl.BlockSpec((N_DEV, tr, C), lambda i: (0, i, 0)), spec, spec, spec], out_specs=[spec] * 4,
        out_shape=[jax.ShapeDtypeStruct((R, C), F32)] * 4,
        compiler_params=_cparams(("parallel",), VMEM_MID))(gathered, w, m, v)


def _position():
    return lax.axis_index("x"), lax.axis_index("y"), lax.axis_index("c")


FORWARD_AT = 0.7


def gather_copies(shards):
    n = len(shards)

    def parts(ins, outs, sems):
        send_sems, recv_sems, local_sems = sems
        x, y, c = _position()
        me, sibling = (x, y, c), (x, y, 1 - c)
        chips = [(1 - x, y), (x, 1 - y), (1 - x, 1 - y)]

        def copy(a, k, block, to, src=None):
            blk = outs[a].at[4 * block[0] + 2 * block[1] + block[2]]
            return pltpu.make_async_remote_copy(
                src_ref=blk if src is None else src, dst_ref=blk,
                send_sem=send_sems.at[a, k], recv_sem=recv_sems.at[a, k], device_id=to, device_id_type=MESH)

        mine = [pltpu.make_async_copy(ins[a], outs[a].at[4 * x + 2 * y + c], local_sems.at[a]) for a in range(n)]
        first = [[copy(a, 0, me, sibling, src=ins[a])] +
                 [copy(a, 1 + j, me, (*chip, c), src=ins[a]) for j, chip in enumerate(chips)] for a in range(n)]
        landed = [[copy(a, 1 + j, (*chip, c), me) for j, chip in enumerate(chips)] for a in range(n)]
        passed = [[copy(a, 4 + j, (*chip, c), sibling) for j, chip in enumerate(chips)] for a in range(n)]
        from_sibling = [[copy(a, 0, sibling, me)] +
                        [copy(a, 4 + j, (*chip, 1 - c), me) for j, chip in enumerate(chips)] for a in range(n)]
        return mine, first, landed, passed, from_sibling

    def start(ins, outs, sems):
        mine, first, _, _, _ = parts(ins, outs, sems)
        for a in range(n):
            mine[a].start()
            for cp in first[a]:
                cp.start()

    def forward(ins, outs, sems):
        _, _, landed, passed, _ = parts(ins, outs, sems)
        for a in range(n):
            for j in range(3):
                landed[a][j].wait_recv()
                passed[a][j].start()

    def finish(ins, outs, sems):
        mine, first, _, passed, from_sibling = parts(ins, outs, sems)
        for a in range(n):
            for cp in from_sibling[a]:
                cp.wait_recv()
        for a in range(n):
            for cp in first[a] + passed[a]:
                cp.wait_send()
            mine[a].wait()

    return HostedCopies(
        list(shards), [jax.ShapeDtypeStruct((N_DEV,) + s.shape, s.dtype) for s in shards],
        [pltpu.SemaphoreType.DMA((n, 7)), pltpu.SemaphoreType.DMA((n, 7)), pltpu.SemaphoreType.DMA((n,))],
        [(0.0, start), (FORWARD_AT, forward), (1.0, finish)])


def _exchange_copies(arrays, out_lead, make):
    n = len(arrays)

    def all_copies(ins, outs, sems):
        send_sems, recv_sems = sems
        return [make(ins[a], outs[a], send_sems.at[a, k], recv_sems.at[a, k], k)
                for a in range(n) for k in range(out_lead)]

    def start(ins, outs, sems):
        for cp in all_copies(ins, outs, sems):
            cp.start()

    def finish(ins, outs, sems):
        for cp in all_copies(ins, outs, sems):
            cp.wait()

    return HostedCopies(
        list(arrays), [jax.ShapeDtypeStruct((out_lead,) + a.shape[2:], a.dtype) for a in arrays],
        [pltpu.SemaphoreType.DMA((n, out_lead)), pltpu.SemaphoreType.DMA((n, out_lead))],
        [(0.0, start), (1.0, finish)])


def sibling_copies(grads):
    def make(src, dst, send_sem, recv_sem, q):
        x, y, c = _position()
        core = 1 - c if src.shape[1] == 2 else 0
        return pltpu.make_async_remote_copy(
            src_ref=src.at[q, core], dst_ref=dst.at[q], send_sem=send_sem, recv_sem=recv_sem,
            device_id=(x, y, 1 - c), device_id_type=MESH)

    return _exchange_copies(grads, 4, make)


def chip_copies(parts):
    def make(src, dst, send_sem, recv_sem, j):
        x, y, c = _position()
        chip = [(1 - x, y), (x, 1 - y), (1 - x, 1 - y)][j]
        return pltpu.make_async_remote_copy(
            src_ref=src.at[2 * chip[0] + chip[1], 0], dst_ref=dst.at[j], send_sem=send_sem, recv_sem=recv_sem,
            device_id=(*chip, c), device_id_type=MESH)

    return _exchange_copies(parts, 3, make)


class Carrier:
    def __init__(self):
        self.plan = {}
        self.counts = {}

    def ride(self, site, make, store):
        self.plan.setdefault(site, []).append((make, store))

    def make(self, site, ctx=None):
        if site not in self.plan:
            return None
        group = [make(ctx) for make, _ in self.plan[site]]
        self.counts[site] = [len(c.out_shape) for c in group]
        return merge_copies(group)

    def store(self, site, results):
        if site in self.plan:
            at = 0
            for (_, store), n in zip(self.plan[site], self.counts[site]):
                store(results[at:at + n])
                at += n

    def split(self, site, out):
        if site not in self.plan:
            return out
        self.store(site, out[1])
        return out[0]


def _pool_weight(gathered):
    PG = gathered.shape[-1]
    return gathered.transpose(1, 0, 2, 3).reshape(N_POOL_GROUPS, PG, PG)


def _layer_params(l, ln_g, pool_scale, lam_re, lam_im, log_dt, b_re, b_im, c_re, c_im,
                  d_skip, b_glu, branch_g):
    G, P = lam_re.shape[1:]
    p = dict(
        ln_g=ln_g[l][None, :], pool_scale=pool_scale[l][None, :], d_skip=d_skip[l][None, :],
        b_glu=b_glu[l][None, :], branch_g=branch_g[l][None, :],
        lr=lam_re[l].reshape(G, 1, P), li=lam_im[l].reshape(G, 1, P), ld=log_dt[l].reshape(G, 1, 1),
        br=b_re[l].transpose(0, 2, 1), bi=b_im[l].transpose(0, 2, 1), cr=c_re[l], ci=c_im[l])
    return p


def layer_fwd(l, x, p, gw, D, carrier):
    t = f"l{l}_"
    h = rms_fwd(t + "rms_fwd", x, p["ln_g"])
    site = (l, "proj")
    proj = carrier.split(site, mm_nn_gathered(t + "proj", h, gw("w_in", l), copies=carrier.make(site)))
    wp = _pool_weight(gw("w_pool", l)[0])
    ypool = pool_fwd(t + "pool_fwd", proj, wp, p["pool_scale"])
    site = (l, "attn_fwd")
    yattn, landed = attn_fwd(t + "attn_fwd", proj, D, copies=carrier.make(site))
    carrier.store(site, landed)
    zr, zi, bbr, bbi = ssm_prep(t + "ssm_prep", p["lr"], p["li"], p["ld"], p["br"], p["bi"])
    ssm_w = dict(wbr=_block_diag(bbr, False), wbi=_block_diag(bbi, False),
                 zr=zr.reshape(1, -1), zi=zi.reshape(1, -1),
                 wcr=_block_diag(p["cr"], True), wci=_block_diag(p["ci"], True))
    site = (l, "ssm_fwd")
    (ypre, hg, xr, xi), landed = ssm_fwd(
        t + "ssm_fwd", proj, ssm_w["wbr"], ssm_w["wbi"], ssm_w["zr"], ssm_w["zi"],
        ssm_w["wcr"], ssm_w["wci"], p["d_skip"], D, copies=carrier.make(site))
    carrier.store(site, landed)
    glu_pre = mm_nn_gathered(t + "glu", hg, gw("w_glu", l))
    y = branch_fwd(t + "branch_fwd", ypool, yattn, glu_pre, proj, p["b_glu"], p["branch_g"])
    out = mm_plain(t + "out", y, gw("w_out", l)[0].reshape(D, D), NN, res=x)
    saved = dict(x=x, h=h, proj=proj, ypool=ypool, yattn=yattn, ypre=ypre, hg=hg, xr=xr, xi=xi,
                 glu_pre=glu_pre, y=y, ssm_w=ssm_w, wp=wp)
    return out, saved


def layer_bwd(l, dres, dres_b, s, p, gw, D, carrier, pos, split_w_in):
    t = f"l{l}_"
    proj = s["proj"]

    def by_target(g):
        return g.reshape(4, 2, -1, g.shape[-1])

    big = {}
    w_out_g = gw("w_out", l)[0].reshape(D, D)
    site = (l, "dy")
    dy = carrier.split(site, mm_plain(t + "dy", dres_b, w_out_g, NT, copies=carrier.make(site)))
    big["w_out"] = by_target(mm_plain(t + "dw_out", s["y"], dres_b, TN).reshape(N_DEV, D // N_DEV, D))
    dypool, dyattn, dglu, dpg, dag, dsg, dbg, dbglu = branch_bwd(
        t + "branch_bwd", dy, s["ypool"], s["yattn"], s["glu_pre"], proj, p["b_glu"], p["branch_g"])
    dhg = mm_nt_gathered(t + "dhg", dglu, gw("w_glu", l))
    big["w_glu"] = by_target(mm_tn_scattered(t + "dw_glu", s["hg"], dglu))
    w = s["ssm_w"]
    site = (l, "ssm_bwd")
    (du, dwcr, dwci, dwbr, dwbi, dar, dai, dds), landed = ssm_bwd(
        t + "ssm_bwd", dhg, s["ypre"], proj, s["xr"], s["xi"], w["wbr"], w["wbi"], w["zr"], w["zi"],
        w["wcr"], w["wci"], p["d_skip"], D, copies=carrier.make(site, big))
    carrier.store(site, landed)
    G, _, P = p["lr"].shape
    dlr, dli, dld, dbr, dbi = ssm_prep_bwd(
        t + "ssm_prep_bwd", p["lr"], p["li"], p["ld"], p["br"], p["bi"],
        dar.reshape(G, 1, P), dai.reshape(G, 1, P), _diag_blocks(dwbr, False), _diag_blocks(dwbi, False))
    site = (l, "attn_bwd")
    (dq, dk, dv), landed = attn_bwd(t + "attn_bwd", proj, s["yattn"], dyattn, D, copies=carrier.make(site, big))
    carrier.store(site, landed)
    dxp, dwp, dps = pool_bwd(t + "pool_bwd", dypool, proj, s["wp"], p["pool_scale"])
    dproj = jnp.concatenate([dxp, dpg, dq, dk, dv, dag, du, dsg], axis=1)
    PG = dwp.shape[1]
    big["w_pool"] = by_target(dwp.reshape(N_POOL_GROUPS, N_DEV, PG // N_DEV, PG).transpose(1, 0, 2, 3))
    if split_w_in:
        site = (l, "dw_in_a")
        to_sibling = carrier.split(site, mm_tn_half(t + "dw_in_a", s["h"], dproj, pos, False,
                                                    copies=carrier.make(site, big)))
        site = (l, "dw_in_b")
        mine = carrier.split(site, mm_tn_half(t + "dw_in_b", s["h"], dproj, pos, True,
                                              copies=carrier.make(site, dict(big, to_sibling=to_sibling[:, None]))))
        big["w_in"] = mine[:, None]
    else:
        big["w_in"] = by_target(mm_tn_scattered(t + "dw_in", s["h"], dproj))
    site = (l, "dh")
    dh = carrier.split(site, mm_nt_gathered(t + "dh", dproj, gw("w_in", l), copies=carrier.make(site, big)))
    dx, dx_b, dlng = rms_bwd(t + "rms_bwd", s["x"], dh, dres, p["ln_g"])
    small = dict(ln_g=dlng[0], pool_scale=dps[0], lam_re=dlr.reshape(G, P), lam_im=dli.reshape(G, P),
                 log_dt=dld.reshape(G), b_re=dbr.transpose(0, 2, 1), b_im=dbi.transpose(0, 2, 1),
                 c_re=_diag_blocks(dwcr, True), c_im=_diag_blocks(dwci, True),
                 d_skip=dds[0], b_glu=dbglu[0], branch_g=dbg[0])
    return dx, dx_b, small


SMALL_NAMES = ("ln_g", "pool_scale", "lam_re", "lam_im", "log_dt", "b_re", "b_im", "c_re", "c_im",
               "d_skip", "b_glu", "branch_g", "final_g")
BIG_NAMES = ("w_in", "w_pool", "w_glu", "w_out")
WEIGHT_ORDER = ("ln_g", "w_in", "w_pool", "pool_scale", "lam_re", "lam_im", "log_dt", "b_re", "b_im",
                "c_re", "c_im", "d_skip", "w_glu", "b_glu", "branch_g", "w_out", "final_g")


PACK_ROWS = 512


def _pack(arrs):
    flat = jnp.concatenate([a.reshape(-1) for a in arrs])
    pad = (-flat.shape[0]) % (PACK_ROWS * LANE)
    return jnp.pad(flat, (0, pad)).reshape(-1, LANE)


def _unpack(packed, like):
    flat = packed.reshape(-1)
    out, off = [], 0
    for a in like:
        out.append(flat[off:off + a.size].reshape(a.shape))
        off += a.size
    return out


def kernel(x, ln_g, w_in, w_pool, pool_scale, lam_re, lam_im, log_dt, b_re, b_im, c_re, c_im, d_skip, w_glu, b_glu, branch_g, w_out, final_g, loss_target, m_ln_g, m_w_in, m_w_pool, m_pool_scale, m_lam_re, m_lam_im, m_log_dt, m_b_re, m_b_im, m_c_re, m_c_im, m_d_skip, m_w_glu, m_b_glu, m_branch_g, m_w_out, m_final_g, v_ln_g, v_w_in, v_w_pool, v_pool_scale, v_lam_re, v_lam_im, v_log_dt, v_b_re, v_b_im, v_c_re, v_c_im, v_d_skip, v_w_glu, v_b_glu, v_branch_g, v_w_out, v_final_g):
    W = dict(ln_g=ln_g, w_in=w_in, w_pool=w_pool, pool_scale=pool_scale, lam_re=lam_re, lam_im=lam_im,
             log_dt=log_dt, b_re=b_re, b_im=b_im, c_re=c_re, c_im=c_im, d_skip=d_skip, w_glu=w_glu,
             b_glu=b_glu, branch_g=branch_g, w_out=w_out, final_g=final_g)
    Mo = dict(ln_g=m_ln_g, w_in=m_w_in, w_pool=m_w_pool, pool_scale=m_pool_scale, lam_re=m_lam_re,
              lam_im=m_lam_im, log_dt=m_log_dt, b_re=m_b_re, b_im=m_b_im, c_re=m_c_re, c_im=m_c_im,
              d_skip=m_d_skip, w_glu=m_w_glu, b_glu=m_b_glu, branch_g=m_branch_g, w_out=m_w_out,
              final_g=m_final_g)
    Vo = dict(ln_g=v_ln_g, w_in=v_w_in, w_pool=v_w_pool, pool_scale=v_pool_scale, lam_re=v_lam_re,
              lam_im=v_lam_im, log_dt=v_log_dt, b_re=v_b_re, b_im=v_b_im, c_re=v_c_re, c_im=v_c_im,
              d_skip=v_d_skip, w_glu=v_w_glu, b_glu=v_b_glu, branch_g=v_branch_g, w_out=v_w_out,
              final_g=v_final_g)
    depth = ln_g.shape[0]
    _, L, D = x.shape
    xc, yc, cc = _position()
    pos = jnp.stack([cc, 2 * xc + yc, 4 * xc + 2 * yc + cc]).astype(jnp.int32)

    def n_parts(n, l):
        return W_IN_PARTS if n == "w_in" and l > 0 else 1

    shards, landed = {}, {}
    for n in BIG_NAMES:
        for l in range(depth):
            for g, part in enumerate(cast_bf16(f"cast_{n}_{l}", W[n], l, n_parts(n, l))):
                shards[n, l, g] = part
    carrier = Carrier()

    def gw(n, l):
        return [landed[n, l, g] for g in range(n_parts(n, l))]

    def gather_plan(keys):
        return (lambda ctx: gather_copies([shards[k] for k in keys])), (lambda outs: landed.update(zip(keys, outs)))

    first = [("w_in", 0, 0)] + [("w_pool", l, 0) for l in range(depth)]
    landed.update(zip(first, copies_call("gather_first", gather_copies([shards[k] for k in first]))))
    carrier.ride((0, "proj"), *gather_plan([("w_out", 0, 0), ("w_glu", 0, 0)]))
    for l in range(1, depth):
        for g, call in enumerate(("attn_fwd", "ssm_fwd")):
            carrier.ride((l - 1, call), *gather_plan([("w_in", l, g)]))
        carrier.ride((l, "proj"), *gather_plan([("w_out", l, 0), ("w_glu", l, 0)]))

    own, recv_a, recv_b = {}, {}, {}

    def sibling_plan(l, names, keep, pick):
        def make(ctx):
            own.update({(n, l): ctx[n] for n in keep})
            return sibling_copies(pick(ctx))
        return make, (lambda outs: recv_a.update(zip([(n, l) for n in names], outs)))

    def chip_plan(l, names):
        def make(ctx):
            parts = [chip_partial(f"chip_partial_{n}_{l}", pos, own[n, l], recv_a[n, l])[:, None] for n in names]
            return chip_copies(parts)
        return make, (lambda outs: recv_b.update(zip([(n, l) for n in names], outs)))

    early, late = ("w_out", "w_glu"), ("w_in", "w_pool")
    for l in range(1, depth):
        carrier.ride((l, "dh"), *sibling_plan(l, BIG_NAMES, BIG_NAMES, lambda big: [big[n] for n in BIG_NAMES]))
        carrier.ride((l - 1, "ssm_bwd"), *chip_plan(l, ("w_out", "w_glu", "w_pool")))
        carrier.ride((l - 1, "attn_bwd"), *chip_plan(l, ("w_in",)))
    carrier.ride((0, "ssm_bwd"), *sibling_plan(0, early, early, lambda big: [big[n] for n in early]))
    carrier.ride((0, "dw_in_a"), *chip_plan(0, early))
    carrier.ride((0, "dw_in_b"), *sibling_plan(0, late, ("w_pool",), lambda ctx: [ctx["to_sibling"], ctx["w_pool"]]))

    def last_chip_make(big):
        own["w_in", 0] = big["w_in"]
        return chip_plan(0, late)[0](big)

    carrier.ride((0, "dh"), last_chip_make, chip_plan(0, late)[1])

    params = [_layer_params(l, ln_g, pool_scale, lam_re, lam_im, log_dt, b_re, b_im, c_re, c_im,
                            d_skip, b_glu, branch_g) for l in range(depth)]
    h = x[0]
    saved = []
    for l in range(depth):
        h, s = layer_fwd(l, h, params[l], gw, D, carrier)
        saved.append(s)
    loss_part, dres, dres_b, dfinal = loss_head("loss_head", h, final_g[None, :], loss_target[0])
    loss = lax.psum(loss_part[0, 0], ("x", "y", "c"))

    small = [None] * depth
    for l in reversed(range(depth)):
        dres, dres_b, small[l] = layer_bwd(l, dres, dres_b, saved[l], params[l], gw, D, carrier, pos,
                                           split_w_in=(l == 0))
    grad_x = dres[None]

    results = {}
    for n in BIG_NAMES:
        shape = W[n].shape
        R, C = int(math.prod(shape[1:-1])), shape[-1]
        w3, m3, v3 = (t.reshape(depth, R, C) for t in (W[n], Mo[n], Vo[n]))
        prev = None
        for l in range(depth):
            prev = adamw_shard(f"adamw_{n}_{l}", pos, l, own[n, l], recv_a[n, l], recv_b[n, l], w3, m3, v3, prev)
        results[n] = [t.reshape(shape) for t in prev]

    small_like = [W[n] for n in SMALL_NAMES]
    small_grads = [jnp.stack([small[l][n] for l in range(depth)]) for n in SMALL_NAMES[:-1]] + [dfinal[0]]
    gathered = copies_call("gather_small_grads", gather_copies([_pack(small_grads)]))[0]
    packed = adamw_small("adamw_small", gathered, _pack(small_like), _pack([Mo[n] for n in SMALL_NAMES]),
                         _pack([Vo[n] for n in SMALL_NAMES]))
    unpacked = [_unpack(t, small_like) for t in packed]
    for i, n in enumerate(SMALL_NAMES):
        results[n] = [unpacked[j][i] for j in range(4)]

    out = [loss, grad_x]
    for j in range(4):
        out += [results[n][j] for n in WEIGHT_ORDER]
    return tuple(out)
```

```python
import functools
import math

import jax
import jax.numpy as jnp
from jax import lax
from jax.experimental import pallas as pl
from jax.experimental.pallas import tpu as pltpu

F32 = jnp.float32
BF16 = jnp.bfloat16
MESH = pl.DeviceIdType.MESH

EPS = 1e-6
HEAD_DIM = 128
SSM_GROUP = 16
SSM_STATE = 64
GROUPS_PER_CHUNK = 8
CHUNK_U = GROUPS_PER_CHUNK * SSM_GROUP
CHUNK_X = GROUPS_PER_CHUNK * SSM_STATE
N_POOL_GROUPS = 4
POOL_HALO = 16
N_DEV = 8
LANE = 128
FULL_K = 4096
ATTN_TILE = 256
ATTN_DECAY_CUTOFF = 100.0
ROW_TILE = 128
VMEM_BIG = 58 * 1024 * 1024
VMEM_MID = 40 * 1024 * 1024

ADAM_LR = 0.001
ADAM_B1 = 0.9
ADAM_B2 = 0.999
ADAM_EPS = 1e-08
ADAM_WD = 0.01
ADAM_STEP = 10
ADAM_C1 = 1.0 / (1.0 - ADAM_B1 ** ADAM_STEP)
ADAM_C2 = 1.0 / (1.0 - ADAM_B2 ** ADAM_STEP)

NN = (((1,), (0,)), ((), ()))
NT = (((1,), (1,)), ((), ()))
TN = (((0,), (0,)), ((), ()))


def _pick(n, cap):
    if n <= cap:
        return n
    step = LANE if cap >= LANE else 8
    t = (cap // step) * step
    while t > step and n % t:
        t -= step
    assert n % t == 0, (n, cap)
    return t


def _cparams(sem, vmem=None):
    return pltpu.CompilerParams(dimension_semantics=sem, vmem_limit_bytes=vmem)


def _dot(a, b, dn=NN):
    return lax.dot_general(a, b, dn, preferred_element_type=F32)


def _sigmoid(x):
    e = jnp.exp(-jnp.abs(x))
    r = 1.0 / (1.0 + e)
    return jnp.where(x >= 0, r, e * r)


HBM = pl.BlockSpec(memory_space=pl.ANY)


class HostedCopies:
    def __init__(self, inputs, out_shape, scratch, phases):
        self.inputs, self.out_shape, self.scratch, self.phases = inputs, out_shape, scratch, phases

    def emit(self, ins, outs, sems, step, total):
        plan = {}
        for frac, fn in self.phases:
            plan.setdefault(min(total - 1, int(frac * total)), []).append(fn)
        for s in sorted(plan):
            def run(fns=plan[s]):
                for fn in fns:
                    fn(ins, outs, sems)
            if total == 1:
                run()
            else:
                pl.when(step == s)(run)


def copies_call(name, copies):
    n_i, n_o = len(copies.inputs), len(copies.out_shape)

    def body(*refs):
        copies.emit(refs[:n_i], refs[n_i:n_i + n_o], refs[n_i + n_o:], 0, 1)

    return pl.pallas_call(
        body, name=name, in_specs=[HBM] * n_i, out_specs=[HBM] * n_o, out_shape=copies.out_shape,
        scratch_shapes=copies.scratch, compiler_params=pltpu.CompilerParams(has_side_effects=True),
    )(*copies.inputs)


def _matmul(name, a, b, *, grid, a_spec, b_spec, o_spec, out_shape, dn,
            res=None, res_spec=None, pos=None, copies=None, product=None, into=None):
    ni, nj, nk = grid
    bs, b_specs = (list(b), list(b_spec)) if isinstance(b, (list, tuple)) else ([b], [b_spec])
    n_b = len(bs)
    n_pos = 0 if pos is None else 1
    n_res = 0 if res is None else 1
    n_into = 0 if into is None else 1
    n_ci = 0 if copies is None else len(copies.inputs)
    n_co = 0 if copies is None else len(copies.out_shape)

    def body(*refs):
        refs = refs[n_pos:]
        a_ref, b_refs = refs[0], refs[1:1 + n_b]
        r_ref = refs[1 + n_b] if n_res else None
        base = 1 + n_b + n_res + n_into
        cin = refs[base:base + n_ci]
        o_ref = refs[base + n_ci]
        cout = refs[base + n_ci + 1:base + n_ci + 1 + n_co]
        sems = refs[base + n_ci + 1 + n_co:]
        k = pl.program_id(2)
        if copies is not None:
            step = (pl.program_id(0) * nj + pl.program_id(1)) * nk + k
            copies.emit(cin, cout, sems, step, ni * nj * nk)

        if product is None:
            part = _dot(a_ref[...].astype(BF16), b_refs[0][...].astype(BF16), dn)
        else:
            part = product(a_ref, b_refs)
        if nk == 1:
            if r_ref is not None:
                part = part + r_ref[...]
            o_ref[...] = part.astype(o_ref.dtype)
        else:
            @pl.when(k == 0)
            def _():
                o_ref[...] = part if r_ref is None else part + r_ref[...]

            @pl.when(k > 0)
            def _():
                o_ref[...] += part

    assert nk == 1 or out_shape.dtype == F32
    in_specs = [a_spec] + b_specs + ([res_spec] if n_res else []) + [HBM] * (n_into + n_ci)
    args = (((pos,) if n_pos else ()) + (a, *bs) + ((res,) if n_res else ()) + ((into,) if n_into else ())
            + tuple(copies.inputs if copies else ()))
    aliases = {n_pos + 1 + n_b + n_res: 0} if n_into else {}
    out_specs = [o_spec] + [HBM] * n_co
    out_shapes = [out_shape] + list(copies.out_shape if copies else [])
    scratch = list(copies.scratch if copies else [])
    params = pltpu.CompilerParams(
        dimension_semantics=("arbitrary",) * 3 if copies else ("parallel", "parallel", "arbitrary"),
        vmem_limit_bytes=VMEM_BIG, has_side_effects=copies is not None)
    out = pl.pallas_call(
        body, name=name,
        grid_spec=pltpu.PrefetchScalarGridSpec(
            num_scalar_prefetch=n_pos, grid=grid, in_specs=in_specs, out_specs=out_specs, scratch_shapes=scratch),
        out_shape=out_shapes, input_output_aliases=aliases, compiler_params=params)(*args)
    return out[0] if copies is None else (out[0], list(out[1:]))


def mm_nn_gathered(name, a, parts, out_dtype=F32, copies=None):
    M, K = a.shape
    P, w = len(parts), parts[0].shape[2]
    nper = P * w
    tm, tk, tn = _pick(M, 1024), _pick(K, FULL_K), _pick(w, 768)
    r = w // tn
    per_part = N_DEV * r

    def b_spec(g):
        def index(i, j, k, *_):
            t = jnp.clip(j - g * per_part, 0, per_part - 1)
            return (t // r, k, t % r)
        return pl.BlockSpec((None, tk, tn), index)

    def o_index(i, j, k, *_):
        t = j % per_part
        return (i, (t // r) * (nper // tn) + (j // per_part) * r + t % r)

    def product(a_ref, b_refs):
        j = pl.program_id(1)
        b = b_refs[0][...]
        for g in range(1, P):
            b = jnp.where(j >= g * per_part, b_refs[g][...], b)
        return _dot(a_ref[...].astype(BF16), b.astype(BF16))

    return _matmul(
        name, a, list(parts), grid=(M // tm, P * per_part, K // tk),
        a_spec=pl.BlockSpec((tm, tk), lambda i, j, k, *_: (i, k)),
        b_spec=[b_spec(g) for g in range(P)], o_spec=pl.BlockSpec((tm, tn), o_index),
        out_shape=jax.ShapeDtypeStruct((M, N_DEV * nper), out_dtype), dn=NN, copies=copies,
        product=product if P > 1 else None)


def mm_nn_part(name, a, part, g, P, into=None, out_dtype=F32, copies=None):
    M, K = a.shape
    w = part.shape[2]
    tm, tk, tn = _pick(M, 1024), _pick(K, FULL_K), _pick(w, 768)
    r = w // tn
    return _matmul(
        name, a, part, grid=(M // tm, N_DEV * r, K // tk),
        a_spec=pl.BlockSpec((tm, tk), lambda i, j, k, *_: (i, k)),
        b_spec=pl.BlockSpec((None, tk, tn), lambda i, j, k, *_: (j // r, k, j % r)),
        o_spec=pl.BlockSpec((tm, tn), lambda i, j, k, *_: (i, (j // r) * (P * r) + g * r + j % r)),
        out_shape=jax.ShapeDtypeStruct((M, N_DEV * P * w), out_dtype), dn=NN, copies=copies, into=into)


NT_SLICES = 2
W_IN_PARTS = 2
W_IN0_PARTS = 3


def mm_nt_gathered(name, a, parts, out_dtype=F32, copies=None):
    M, _ = a.shape
    P, (_, N, w) = len(parts), parts[0].shape
    nper = P * w
    tm, tn = _pick(M, 1024), _pick(N, 1024)
    S = NT_SLICES

    def product(a_ref, b_refs):
        total = None
        for s in range(S):
            for g in range(P):
                off = (s * P + g) * w
                term = _dot(a_ref[:, off:off + w].astype(BF16), b_refs[g][s], NT)
                total = term if total is None else total + term
        return total

    return _matmul(
        name, a, list(parts), grid=(M // tm, N // tn, N_DEV // S),
        a_spec=pl.BlockSpec((tm, S * nper), lambda i, j, k, *_: (i, k)),
        b_spec=[pl.BlockSpec((S, tn, w), lambda i, j, k, *_: (k, j, 0)) for _ in range(P)],
        o_spec=pl.BlockSpec((tm, tn), lambda i, j, k, *_: (i, j)),
        out_shape=jax.ShapeDtypeStruct((M, N), out_dtype), dn=NT, copies=copies, product=product)


def mm_tn_scattered(name, a, b, copies=None):
    L, M = a.shape
    nper = b.shape[1] // N_DEV
    tm, tn, tk = _pick(M, 1024), _pick(nper, 768), _pick(L, FULL_K)
    r = nper // tn
    return _matmul(
        name, a, b, grid=(M // tm, N_DEV * r, L // tk),
        a_spec=pl.BlockSpec((tk, tm), lambda i, j, k, *_: (k, i)),
        b_spec=pl.BlockSpec((tk, tn), lambda i, j, k, *_: (k, j)),
        o_spec=pl.BlockSpec((None, tm, tn), lambda i, j, k, *_: (j // r, i, j % r)),
        out_shape=jax.ShapeDtypeStruct((N_DEV, M, nper), F32), dn=TN, copies=copies)


def mm_tn_half(name, a, b, pos, own, copies=None):
    L, M = a.shape
    nper = b.shape[1] // N_DEV
    tm, tn, tk = _pick(M, 1024), _pick(nper, 768), _pick(L, FULL_K)
    r = nper // tn

    def b_map(i, j, k, p):
        core = p[0] if own else 1 - p[0]
        return (k, (2 * (j // r) + core) * r + j % r)

    return _matmul(
        name, a, b, grid=(M // tm, 4 * r, L // tk),
        a_spec=pl.BlockSpec((tk, tm), lambda i, j, k, *_: (k, i)),
        b_spec=pl.BlockSpec((tk, tn), b_map),
        o_spec=pl.BlockSpec((None, tm, tn), lambda i, j, k, *_: (j // r, i, j % r)),
        out_shape=jax.ShapeDtypeStruct((4, M, nper), F32), dn=TN, pos=pos, copies=copies)


def mm_plain(name, a, b, dn, out_dtype=F32, res=None, copies=None):
    if dn == NN:
        (M, K), N = a.shape, b.shape[1]
    elif dn == NT:
        (M, K), N = a.shape, b.shape[0]
    else:
        (K, M), N = a.shape, b.shape[1]
    tm, tn, tk = _pick(M, 1024), _pick(N, 512), _pick(K, FULL_K)
    a_spec =(pl.BlockSpec((tk, tm), lambda i, j, k, *_: (k, i)) if dn == TN
              else pl.BlockSpec((tm, tk), lambda i, j, k, *_: (i, k)))
    b_spec = (pl.BlockSpec((tn, tk), lambda i, j, k, *_: (j, k)) if dn == NT
              else pl.BlockSpec((tk, tn), lambda i, j, k, *_: (k, j)))
    o_spec = pl.BlockSpec((tm, tn), lambda i, j, k, *_: (i, j))
    return _matmul(
        name, a, b, grid=(M // tm, N // tn, K // tk), a_spec=a_spec, b_spec=b_spec, o_spec=o_spec,
        out_shape=jax.ShapeDtypeStruct((M, N), out_dtype), dn=dn,
        res=res, res_spec=o_spec if res is not None else None, copies=copies)


def rms_fwd(name, x, g):
    L, D = x.shape
    tr = _pick(L, ROW_TILE)

    def body(x_ref, g_ref, h_ref):
        xv = x_ref[...]
        r = lax.rsqrt(jnp.mean(xv * xv, axis=-1, keepdims=True) + EPS)
        h_ref[...] = (xv * r * g_ref[...]).astype(BF16)

    return pl.pallas_call(
        body, name=name, grid=(L // tr,),
        in_specs=[pl.BlockSpec((tr, D), lambda i: (i, 0)), pl.BlockSpec((1, D), lambda i: (0, 0))],
        out_specs=pl.BlockSpec((tr, D), lambda i: (i, 0)),
        out_shape=jax.ShapeDtypeStruct((L, D), BF16),
        compiler_params=_cparams(("parallel",), VMEM_MID))(x, g)


def rms_bwd(name, x, dh, dres, g):
    L, D = x.shape
    tr = _pick(L, ROW_TILE)

    def body(x_ref, dh_ref, dr_ref, g_ref, dx_ref, dxb_ref, dg_ref):
        xv = x_ref[...]
        r = lax.rsqrt(jnp.mean(xv * xv, axis=-1, keepdims=True) + EPS)
        xh = xv * r
        dhv = dh_ref[...]
        dn = dhv * g_ref[...]
        dxv = dr_ref[...] + r * (dn - xh * jnp.mean(dn * xh, axis=-1, keepdims=True))
        dx_ref[...] = dxv
        dxb_ref[...] = dxv.astype(BF16)

        @pl.when(pl.program_id(0) == 0)
        def _():
            dg_ref[...] = jnp.zeros_like(dg_ref)

        dg_ref[...] += jnp.sum(dhv * xh, axis=0, keepdims=True)

    row = pl.BlockSpec((tr, D), lambda i: (i, 0))
    vec = pl.BlockSpec((1, D), lambda i: (0, 0))
    return pl.pallas_call(
        body, name=name, grid=(L // tr,), in_specs=[row, row, row, vec], out_specs=[row, row, vec],
        out_shape=[jax.ShapeDtypeStruct((L, D), F32), jax.ShapeDtypeStruct((L, D), BF16),
                   jax.ShapeDtypeStruct((1, D), F32)],
        compiler_params=_cparams(("arbitrary",), VMEM_MID))(x, dh, dres, g)


def loss_head(name, x, g, target):
    L, D = x.shape
    tr = _pick(L, ROW_TILE)

    def body(x_ref, g_ref, t_ref, loss_ref, dx_ref, dxb_ref, dg_ref):
        xv = x_ref[...]
        gv = g_ref[...]
        r = lax.rsqrt(jnp.mean(xv * xv, axis=-1, keepdims=True) + EPS)
        xh = xv * r
        err = xh * gv - t_ref[...]
        dy = err * (1.0 / D)
        dn = dy * gv
        dxv = r * (dn - xh * jnp.mean(dn * xh, axis=-1, keepdims=True))
        dx_ref[...] = dxv
        dxb_ref[...] = dxv.astype(BF16)

        @pl.when(pl.program_id(0) == 0)
        def _():
            dg_ref[...] = jnp.zeros_like(dg_ref)
            loss_ref[...] = jnp.zeros_like(loss_ref)

        dg_ref[...] += jnp.sum(dy * xh, axis=0, keepdims=True)
        row_loss = jnp.sum(err * err, axis=-1, keepdims=True) * (0.5 / D)
        loss_ref[...] += jnp.sum(row_loss, axis=0, keepdims=True)

    row = pl.BlockSpec((tr, D), lambda i: (i, 0))
    vec = pl.BlockSpec((1, D), lambda i: (0, 0))
    one = pl.BlockSpec((1, 1), lambda i: (0, 0))
    return pl.pallas_call(
        body, name=name, grid=(L // tr,), in_specs=[row, vec, row], out_specs=[one, row, row, vec],
        out_shape=[jax.ShapeDtypeStruct((1, 1), F32), jax.ShapeDtypeStruct((L, D), F32),
                   jax.ShapeDtypeStruct((L, D), BF16), jax.ShapeDtypeStruct((1, D), F32)],
        compiler_params=_cparams(("arbitrary",), VMEM_MID))(x, g, target)


def _branch_specs(D, tr):
    DP, DA, DS = D // 4, D // 2, D // 4
    return dict(
        pool=pl.BlockSpec((tr, DP), lambda i: (i, 0)),
        attn=pl.BlockSpec((tr, DA), lambda i: (i, 0)),
        glu=pl.BlockSpec((tr, 2 * DS), lambda i: (i, 0)),
        p_gate=pl.BlockSpec((tr, DP), lambda i: (i, 1)),
        a_gate=pl.BlockSpec((tr, DA), lambda i: (i, 4)),
        s_gate=pl.BlockSpec((tr, DS), lambda i: (i, 11)),
        bglu=pl.BlockSpec((1, 2 * DS), lambda i: (0, 0)),
        bg=pl.BlockSpec((1, D), lambda i: (0, 0)),
        row=pl.BlockSpec((tr, D), lambda i: (i, 0)),
    )


def branch_fwd(name, ypool, yattn, glu_pre, proj, b_glu, branch_g):
    L, DP = ypool.shape
    D = 4 * DP
    DA, DS = D // 2, D // 4
    tr = _pick(L, ROW_TILE)
    s = _branch_specs(D, tr)

    def body(yp_ref, ya_ref, gl_ref, pg_ref, ag_ref, sg_ref, bgl_ref, bg_ref, y_ref):
        pre = gl_ref[...] + bgl_ref[...]
        ys = pre[:, :DS] * _sigmoid(pre[:, DS:])
        bg = bg_ref[...]

        def one(raw, gate, g):
            r = lax.rsqrt(jnp.mean(raw * raw, axis=-1, keepdims=True) + EPS)
            return raw * r * g * (gate * _sigmoid(gate))

        y_ref[:, :DP] = one(yp_ref[...], pg_ref[...], bg[:, :DP]).astype(BF16)
        y_ref[:, DP:DP + DA] = one(ya_ref[...], ag_ref[...], bg[:, DP:DP + DA]).astype(BF16)
        y_ref[:, DP + DA:] = one(ys, sg_ref[...], bg[:, DP + DA:]).astype(BF16)

    return pl.pallas_call(
        body, name=name, grid=(L // tr,),
        in_specs=[s["pool"], s["attn"], s["glu"], s["p_gate"], s["a_gate"], s["s_gate"], s["bglu"], s["bg"]],
        out_specs=s["row"], out_shape=jax.ShapeDtypeStruct((L, D), BF16),
        compiler_params=_cparams(("parallel",), VMEM_MID))(ypool, yattn, glu_pre, proj, proj, proj, b_glu, branch_g)


def branch_bwd(name, dy, ypool, yattn, glu_pre, proj, b_glu, branch_g):
    L, DP = ypool.shape
    D = 4 * DP
    DA, DS = D // 2, D // 4
    tr = _pick(L, ROW_TILE // 2)
    s = _branch_specs(D, tr)

    def body(dy_ref, yp_ref, ya_ref, gl_ref, pg_ref, ag_ref, sg_ref, bgl_ref, bg_ref,
             dyp_ref, dya_ref, dgl_ref, dpg_ref, dag_ref, dsg_ref, dbg_ref, dbgl_ref):
        @pl.when(pl.program_id(0) == 0)
        def _():
            dbg_ref[...] = jnp.zeros_like(dbg_ref)
            dbgl_ref[...] = jnp.zeros_like(dbgl_ref)

        bg = bg_ref[...]

        def one(raw, gate, g, dyb):
            r = lax.rsqrt(jnp.mean(raw * raw, axis=-1, keepdims=True) + EPS)
            n = raw * r
            sg = _sigmoid(gate)
            sl = gate * sg
            dgate = dyb * n * g * (sg * (1.0 + gate * (1.0 - sg)))
            dbg = jnp.sum(dyb * n * sl, axis=0, keepdims=True)
            dn = dyb * g * sl
            draw = r * (dn - n * jnp.mean(dn * n, axis=-1, keepdims=True))
            return draw, dgate, dbg

        draw, dgate, dbg = one(yp_ref[...], pg_ref[...], bg[:, :DP], dy_ref[:, :DP])
        dyp_ref[...] = draw
        dpg_ref[...] = dgate.astype(BF16)
        dbg_ref[:, :DP] += dbg

        draw, dgate, dbg = one(ya_ref[...], ag_ref[...], bg[:, DP:DP + DA], dy_ref[:, DP:DP + DA])
        dya_ref[...] = draw
        dag_ref[...] = dgate.astype(BF16)
        dbg_ref[:, DP:DP + DA] += dbg

        pre = gl_ref[...] + bgl_ref[...]
        val = pre[:, :DS]
        sgt = _sigmoid(pre[:, DS:])
        draw, dgate, dbg = one(val * sgt, sg_ref[...], bg[:, DP + DA:], dy_ref[:, DP + DA:])
        dsg_ref[...] = dgate.astype(BF16)
        dbg_ref[:, DP + DA:] += dbg
        dval = draw * sgt
        dgt = draw * val * sgt * (1.0 - sgt)
        dgl_ref[:, :DS] = dval.astype(BF16)
        dgl_ref[:, DS:] = dgt.astype(BF16)
        dbgl_ref[:, :DS] += jnp.sum(dval, axis=0, keepdims=True)
        dbgl_ref[:, DS:] += jnp.sum(dgt, axis=0, keepdims=True)

    loc = lambda w: pl.BlockSpec((tr, w), lambda i: (i, 0))
    return pl.pallas_call(
        body, name=name, grid=(L // tr,),
        in_specs=[s["row"], s["pool"], s["attn"], s["glu"], s["p_gate"], s["a_gate"], s["s_gate"], s["bglu"], s["bg"]],
        out_specs=[loc(DP), loc(DA), loc(2 * DS), loc(DP), loc(DA), loc(DS), s["bg"], s["bglu"]],
        out_shape=[jax.ShapeDtypeStruct((L, DP), F32), jax.ShapeDtypeStruct((L, DA), F32),
                   jax.ShapeDtypeStruct((L, 2 * DS), BF16), jax.ShapeDtypeStruct((L, DP), BF16),
                   jax.ShapeDtypeStruct((L, DA), BF16), jax.ShapeDtypeStruct((L, DS), BF16),
                   jax.ShapeDtypeStruct((1, D), F32), jax.ShapeDtypeStruct((1, 2 * DS), F32)],
        compiler_params=_cparams(("arbitrary",), VMEM_BIG),
    )(dy, ypool, yattn, glu_pre, proj, proj, proj, b_glu, branch_g)


def _pool_select(g, s2, s4, s8, s16):
    return jnp.where(g == 0, s2, jnp.where(g == 1, s4, jnp.where(g == 2, s8, s16)))


def _pool_window(g):
    return jnp.where(g == 0, 2.0, jnp.where(g == 1, 4.0, jnp.where(g == 2, 8.0, 16.0))).astype(F32)


def _pooled_chunk(pad, g, r0, ch):
    xh = pad[pl.ds(r0, ch + POOL_HALO), :]
    s2 = xh + pltpu.roll(xh, 1, 0)
    s4 = s2 + pltpu.roll(s2, 2, 0)
    s8 = s4 + pltpu.roll(s4, 4, 0)
    s16 = s8 + pltpu.roll(s8, 8, 0)
    win = _pool_select(g, s2, s4, s8, s16)[POOL_HALO:]
    pos = (r0 + 1 + lax.broadcasted_iota(jnp.int32, (ch, 1), 0)).astype(F32)
    return win / jnp.minimum(pos, _pool_window(g)) - xh[POOL_HALO:]


def pool_fwd(name, proj, wp, scale):
    L = proj.shape[0]
    DP = scale.shape[1]
    PG = DP // N_POOL_GROUPS
    ch = _pick(L, 256)

    def body(x_ref, w_ref, s_ref, o_ref, pad):
        g = pl.program_id(0)
        pad[0:POOL_HALO, :] = jnp.zeros((POOL_HALO, PG), F32)
        pad[POOL_HALO:, :] = x_ref[...]

        def chunk(ci, carry):
            r0 = pl.multiple_of(ci * ch, ch)
            pooled = _pooled_chunk(pad, g, r0, ch)
            o_ref[pl.ds(r0, ch), :] = _dot(pooled.astype(BF16), w_ref[...]) * s_ref[...]
            return carry

        lax.fori_loop(0, L // ch, chunk, 0)

    return pl.pallas_call(
        body, name=name, grid=(N_POOL_GROUPS,),
        in_specs=[pl.BlockSpec((L, PG), lambda g: (0, g)), pl.BlockSpec((None, PG, PG), lambda g: (g, 0, 0)),
                  pl.BlockSpec((1, PG), lambda g: (0, g))],
        out_specs=pl.BlockSpec((L, PG), lambda g: (0, g)),
        out_shape=jax.ShapeDtypeStruct((L, DP), F32),
        scratch_shapes=[pltpu.VMEM((L + POOL_HALO, PG), F32)],
        compiler_params=_cparams(("parallel",), VMEM_MID))(proj, wp, scale)


def pool_bwd(name, dyraw, proj, wp, scale):
    L = proj.shape[0]
    DP = scale.shape[1]
    PG = DP // N_POOL_GROUPS
    ch = _pick(L, 256)

    def body(dy_ref, x_ref, w_ref, s_ref, dx_ref, dw_ref, ds_ref, pad, dpad, dpo):
        g = pl.program_id(0)
        pad[0:POOL_HALO, :] = jnp.zeros((POOL_HALO, PG), F32)
        pad[POOL_HALO:, :] = x_ref[...]
        dpad[L:, :] = jnp.zeros((POOL_HALO, PG), F32)
        dw_ref[...] = jnp.zeros_like(dw_ref)
        ds_ref[...] = jnp.zeros_like(ds_ref)
        wv = w_ref[...]
        win_f = _pool_window(g)

        def chunk(ci, carry):
            r0 = pl.multiple_of(ci * ch, ch)
            pooled = _pooled_chunk(pad, g, r0, ch).astype(BF16)
            dyv = dy_ref[pl.ds(r0, ch), :]
            ds_ref[...] += jnp.sum(dyv * _dot(pooled, wv), axis=0, keepdims=True)
            dmixed = (dyv * s_ref[...]).astype(BF16)
            dw_ref[...] += _dot(pooled, dmixed, TN)
            dpooled = _dot(dmixed, wv, NT)
            pos = (r0 + 1 + lax.broadcasted_iota(jnp.int32, (ch, 1), 0)).astype(F32)
            dpad[pl.ds(r0, ch), :] = dpooled / jnp.minimum(pos, win_f)
            dpo[pl.ds(r0, ch), :] = dpooled
            return carry

        lax.fori_loop(0, L // ch, chunk, 0)

        def chunk2(ci, carry):
            r0 = pl.multiple_of(ci * ch, ch)
            n = ch + POOL_HALO
            dm = dpad[pl.ds(r0, n), :]
            s2 = dm + pltpu.roll(dm, n - 1, 0)
            s4 = s2 + pltpu.roll(s2, n - 2, 0)
            s8 = s4 + pltpu.roll(s4, n - 4, 0)
            s16 = s8 + pltpu.roll(s8, n - 8, 0)
            win = _pool_select(g, s2, s4, s8, s16)[:ch]
            dx_ref[pl.ds(r0, ch), :] = (win - dpo[pl.ds(r0, ch), :]).astype(BF16)
            return carry

        lax.fori_loop(0, L // ch, chunk2, 0)

    col = pl.BlockSpec((L, PG), lambda g: (0, g))
    return pl.pallas_call(
        body, name=name, grid=(N_POOL_GROUPS,),
        in_specs=[col, col, pl.BlockSpec((None, PG, PG), lambda g: (g, 0, 0)), pl.BlockSpec((1, PG), lambda g: (0, g))],
        out_specs=[col, pl.BlockSpec((None, PG, PG), lambda g: (g, 0, 0)), pl.BlockSpec((1, PG), lambda g: (0, g))],
        out_shape=[jax.ShapeDtypeStruct((L, DP), BF16), jax.ShapeDtypeStruct((N_POOL_GROUPS, PG, PG), F32),
                   jax.ShapeDtypeStruct((1, DP), F32)],
        scratch_shapes=[pltpu.VMEM((L + POOL_HALO, PG), F32), pltpu.VMEM((L + POOL_HALO, PG), F32),
                        pltpu.VMEM((L, PG), F32)],
        compiler_params=_cparams(("parallel",), VMEM_MID))(dyraw, proj, wp, scale)


def _attn_tile(L):
    return _pick(L, ATTN_TILE)


def _tri(t, strict):
    j = lax.broadcasted_iota(jnp.int32, (t, t), 0)
    s = lax.broadcasted_iota(jnp.int32, (t, t), 1)
    return ((j > s) if strict else (j >= s)).astype(BF16)


def _attn_block(q, kt, i, k0, rb, after):
    tq, tk = q.shape[0], kt.shape[0]
    row = lax.broadcasted_iota(jnp.int32, (tq, tk), 0)
    col = lax.broadcasted_iota(jnp.int32, (tq, tk), 1)
    causal = (k0 + col) < (i * tq + row)
    z = _dot(q, kt, NT)
    e = jnp.exp(-jnp.abs(z))
    l1p = jnp.log(1.0 + e)
    log_sig = jnp.minimum(z, 0.0) - l1p
    log_1m = -jnp.maximum(z, 0.0) - l1p
    b = jnp.where(causal, log_1m, 0.0)
    b_hi = b.astype(BF16)
    b_lo = (b - b_hi.astype(F32)).astype(BF16)
    suffix = _dot(b_hi, after) + _dot(b_lo, after) + rb
    w = jnp.where(causal, jnp.exp(log_sig + suffix), 0.0)
    return z, e, causal, b, w


def attn_fwd(name, proj, D, copies=None):
    L = proj.shape[0]
    DA = D // 2
    H = DA // HEAD_DIM
    tq = tk = _attn_tile(L)
    qo, ko, vo = (D // 2) // HEAD_DIM, D // HEAD_DIM, (3 * D // 2) // HEAD_DIM
    scale = HEAD_DIM ** -0.5

    def body(q_ref, k_ref, v_ref, tri_ref, o_ref, kb_s, vb_s, acc, rb):
        i = pl.program_id(1)

        @pl.when(i == 0)
        def _():
            kb_s[...] = k_ref[...].astype(BF16)
            vb_s[...] = v_ref[...].astype(BF16)

        q = (q_ref[...] * scale).astype(BF16)
        acc[...] = jnp.zeros_like(acc)
        rb[...] = jnp.zeros_like(rb)

        def cond(c):
            return jnp.logical_and(c[0] >= 0, c[1])

        def step(c):
            kb = c[0]
            k0 = pl.multiple_of(kb * tk, tk)
            kt = kb_s[pl.ds(k0, tk), :]
            vt = vb_s[pl.ds(k0, tk), :]
            _, _, _, b, w = _attn_block(q, kt, i, k0, rb[...], tri_ref[...])
            acc[...] += _dot(w.astype(BF16), vt)
            rbn = rb[...] + jnp.sum(b, axis=1, keepdims=True)
            rb[...] = rbn
            return kb - 1, jnp.max(rbn) > -ATTN_DECAY_CUTOFF

        lax.while_loop(cond, step, (i, jnp.bool_(True)))
        o_ref[...] = acc[...]

    (out,), landed = _call(
        body, name=name, grid=(H, L // tq),
        in_specs=[pl.BlockSpec((tq, HEAD_DIM), lambda h, i: (i, qo + h)),
                  pl.BlockSpec((L, HEAD_DIM), lambda h, i: (0, ko + h)),
                  pl.BlockSpec((L, HEAD_DIM), lambda h, i: (0, vo + h)),
                  pl.BlockSpec((tk, tk), lambda h, i: (0, 0))],
        out_specs=[pl.BlockSpec((tq, HEAD_DIM), lambda h, i: (i, h))],
        out_shape=[jax.ShapeDtypeStruct((L, DA), F32)],
        scratch_shapes=[pltpu.VMEM((L, HEAD_DIM), BF16), pltpu.VMEM((L, HEAD_DIM), BF16),
                        pltpu.VMEM((tq, HEAD_DIM), F32), pltpu.VMEM((tq, 1), F32)],
        vmem=VMEM_MID, args=(proj, proj, proj, _tri(tk, True)), semantics=("arbitrary", "arbitrary"),
        copies=copies)
    return out, landed


def attn_bwd(name, proj, o, do, D, copies=None):
    L = proj.shape[0]
    DA = D // 2
    H = DA // HEAD_DIM
    tq = tk = _attn_tile(L)
    qo, ko, vo = (D // 2) // HEAD_DIM, D // HEAD_DIM, (3 * D // 2) // HEAD_DIM
    scale = HEAD_DIM ** -0.5

    def body(q_ref, k_ref, v_ref, o_ref, do_ref, after_ref, from_ref, dq_ref, dk_ref, dv_ref,
             kb_s, vb_s, dk_s, dv_s, dq_acc, rb, rg):
        i = pl.program_id(1)
        nq = pl.num_programs(1)

        @pl.when(i == 0)
        def _():
            kb_s[...] = k_ref[...].astype(BF16)
            vb_s[...] = v_ref[...].astype(BF16)
            dk_s[...] = jnp.zeros_like(dk_s)
            dv_s[...] = jnp.zeros_like(dv_s)

        q = (q_ref[...] * scale).astype(BF16)
        dob = do_ref[...].astype(BF16)
        delta = jnp.sum(dob.astype(F32) * o_ref[...], axis=1, keepdims=True)
        dq_acc[...] = jnp.zeros_like(dq_acc)
        rb[...] = jnp.zeros_like(rb)
        rg[...] = jnp.zeros_like(rg)

        def cond(c):
            return jnp.logical_and(c[0] >= 0, c[1])

        def step(c):
            kb = c[0]
            k0 = pl.multiple_of(kb * tk, tk)
            kt = kb_s[pl.ds(k0, tk), :]
            vt = vb_s[pl.ds(k0, tk), :]
            z, e, causal, b, w = _attn_block(q, kt, i, k0, rb[...], after_ref[...])
            wq = w.astype(BF16)
            dw = _dot(dob, vt, NT)
            g = wq.astype(F32) * dw
            g_hi = g.astype(BF16)
            g_lo = (g - g_hi.astype(F32)).astype(BF16)
            from_s = from_ref[...]
            suffix_g = _dot(g_hi, from_s) + _dot(g_lo, from_s) + rg[...]
            before = delta - suffix_g
            r = 1.0 / (1.0 + e)
            sig = jnp.where(z >= 0, r, e * r)
            sig_neg = jnp.where(z >= 0, e * r, r)
            dz = jnp.where(causal, g * sig_neg - before * sig, 0.0).astype(BF16)
            dq_acc[...] += _dot(dz, kt)
            dk_s[pl.ds(k0, tk), :] += _dot(dz, q, TN)
            dv_s[pl.ds(k0, tk), :] += _dot(wq, dob, TN)
            rbn = rb[...] + jnp.sum(b, axis=1, keepdims=True)
            rb[...] = rbn
            rg[...] += jnp.sum(g, axis=1, keepdims=True)
            return kb - 1, jnp.max(rbn) > -ATTN_DECAY_CUTOFF

        lax.while_loop(cond, step, (i, jnp.bool_(True)))
        dq_ref[...] = (dq_acc[...] * scale).astype(BF16)

        @pl.when(i == nq - 1)
        def _():
            dk_ref[...] = dk_s[...].astype(BF16)
            dv_ref[...] = dv_s[...].astype(BF16)

    blk = pl.BlockSpec((tq, HEAD_DIM), lambda h, i: (i, h))
    full = pl.BlockSpec((L, HEAD_DIM), lambda h, i: (0, h))
    return _call(
        body, name=name, grid=(H, L // tq),
        in_specs=[pl.BlockSpec((tq, HEAD_DIM), lambda h, i: (i, qo + h)),
                  pl.BlockSpec((L, HEAD_DIM), lambda h, i: (0, ko + h)),
                  pl.BlockSpec((L, HEAD_DIM), lambda h, i: (0, vo + h)), blk, blk,
                  pl.BlockSpec((tk, tk), lambda h, i: (0, 0)), pl.BlockSpec((tk, tk), lambda h, i: (0, 0))],
        out_specs=[blk, full, full],
        out_shape=[jax.ShapeDtypeStruct((L, DA), BF16)] * 3,
        scratch_shapes=[pltpu.VMEM((L, HEAD_DIM), BF16), pltpu.VMEM((L, HEAD_DIM), BF16),
                        pltpu.VMEM((L, HEAD_DIM), F32), pltpu.VMEM((L, HEAD_DIM), F32),
                        pltpu.VMEM((tq, HEAD_DIM), F32), pltpu.VMEM((tq, 1), F32), pltpu.VMEM((tq, 1), F32)],
        vmem=VMEM_MID, args=(proj, proj, proj, o, do, _tri(tk, True), _tri(tk, False)),
        semantics=("arbitrary", "arbitrary"), copies=copies)


def _cmul(ar, ai, br, bi):
    return ar * br - ai * bi, ar * bi + ai * br


def _cmul_conj(ar, ai, br, bi):
    return ar * br + ai * bi, ar * bi - ai * br


def _ssm_disc(lr, li, ld):
    dt = jnp.exp(ld)
    m = jnp.exp(lr * dt)
    ar, ai = m * jnp.cos(li * dt), m * jnp.sin(li * dt)
    inv = 1.0 / (lr * lr + li * li)
    fr, fi = _cmul(ar - 1.0, ai, lr * inv, -li * inv)
    return dt, ar, ai, fr, fi, inv


def ssm_prep(name, lr, li, ld, br, bi):
    def body(lr_ref, li_ref, ld_ref, br_ref, bi_ref, zr_ref, zi_ref, bbr_ref, bbi_ref):
        dt, _, _, fr, fi, _ = _ssm_disc(lr_ref[...], li_ref[...], ld_ref[...])
        zr_ref[...] = lr_ref[...] * dt
        zi_ref[...] = li_ref[...] * dt
        bbr, bbi = _cmul(fr, fi, br_ref[...], bi_ref[...])
        bbr_ref[...] = bbr
        bbi_ref[...] = bbi

    sd = jax.ShapeDtypeStruct
    return pl.pallas_call(
        body, name=name,
        out_shape=[sd(lr.shape, F32), sd(lr.shape, F32), sd(br.shape, F32), sd(br.shape, F32)],
    )(lr, li, ld, br, bi)


def ssm_prep_bwd(name, lr, li, ld, br, bi, gar, gai, gbr, gbi):
    def body(lr_ref, li_ref, ld_ref, br_ref, bi_ref, gar_ref, gai_ref, gbr_ref, gbi_ref,
             dlr_ref, dli_ref, dld_ref, dbr_ref, dbi_ref):
        lr_, li_ = lr_ref[...], li_ref[...]
        dt, ar, ai, fr, fi, inv = _ssm_disc(lr_, li_, ld_ref[...])
        gbr_, gbi_ = gbr_ref[...], gbi_ref[...]
        dbr, dbi = _cmul_conj(fr, fi, gbr_, gbi_)
        dbr_ref[...] = dbr
        dbi_ref[...] = dbi
        pr, pi = _cmul_conj(br_ref[...], bi_ref[...], gbr_, gbi_)
        gfr = jnp.sum(pr, axis=1, keepdims=True)
        gfi = jnp.sum(pi, axis=1, keepdims=True)
        ilr, ili = lr_ * inv, -li_ * inv
        tr_, ti_ = _cmul_conj(ilr, ili, gfr, gfi)
        gatr, gati = gar_ref[...] + tr_, gai_ref[...] + ti_
        hr, hi = _cmul(fr, fi, ilr, ili)
        t1r, t1i = _cmul_conj(ar * dt, ai * dt, gatr, gati)
        t2r, t2i = _cmul_conj(hr, hi, gfr, gfi)
        dlr_ref[...] = t1r - t2r
        dli_ref[...] = t1i - t2i
        lar, lai = _cmul(lr_, li_, ar, ai)
        gdt, _ = _cmul_conj(lar, lai, gatr, gati)
        dld_ref[...] = jnp.sum(gdt, axis=2, keepdims=True) * dt

    sd = jax.ShapeDtypeStruct
    return pl.pallas_call(
        body, name=name,
        out_shape=[sd(lr.shape, F32), sd(lr.shape, F32), sd(ld.shape, F32), sd(br.shape, F32), sd(br.shape, F32)],
    )(lr, li, ld, br, bi, gar, gai, gbr, gbi)


SCAN_ROWS = 64


def _scan_rows(L):
    return min(SCAN_ROWS, L)


def _power_table(pr_s, pi_s, zr, zi, L, reverse):
    R = _scan_rows(L)
    row = lax.broadcasted_iota(jnp.int32, (R, 1), 0).astype(F32)
    dist = (R - row) if reverse else (row + 1.0)
    mag = jnp.exp(dist * zr)
    pr_s[...] = mag * jnp.cos(dist * zi)
    pi_s[...] = mag * jnp.sin(dist * zi)


def _scan(xr, xi, pr_s, pi_s, L, reverse):
    R = _scan_rows(L)
    nt = L // R
    assert L % R == 0 and R & (R - 1) == 0
    ns = CHUNK_X // LANE
    ridx = lax.broadcasted_iota(jnp.int32, (R, LANE), 0)

    def power(ref, d, cs):
        at = R - d if reverse else d - 1
        return ref[at:at + 1, cs]

    def shift(v, d):
        if d < 8:
            if reverse:
                return jnp.where(ridx < R - d, pltpu.roll(v, R - d, 0), 0.0)
            return jnp.where(ridx >= d, pltpu.roll(v, d, 0), 0.0)
        zeros = jnp.zeros((d, LANE), F32)
        return jnp.concatenate([v[d:], zeros], 0) if reverse else jnp.concatenate([zeros, v[:R - d]], 0)

    def tile(n, carry):
        t = nt - 1 - n if reverse else n
        rows = pl.ds(pl.multiple_of(t * R, R), R)
        edges = []
        for c in range(ns):
            cs = slice(c * LANE, (c + 1) * LANE)
            vr, vi = xr[rows, cs], xi[rows, cs]
            d = 1
            while d < R:
                ar, ai = power(pr_s, d, cs), power(pi_s, d, cs)
                sr, si = shift(vr, d), shift(vi, d)
                vr, vi = vr + ar * sr - ai * si, vi + ar * si + ai * sr
                d *= 2
            cr, ci = carry[2 * c], carry[2 * c + 1]
            pr, pi = pr_s[:, cs], pi_s[:, cs]
            vr, vi = vr + pr * cr - pi * ci, vi + pr * ci + pi * cr
            xr[rows, cs] = vr
            xi[rows, cs] = vi
            edge = slice(0, 1) if reverse else slice(R - 1, R)
            edges += [vr[edge], vi[edge]]
        return tuple(edges)

    lax.fori_loop(0, nt, tile, tuple(jnp.zeros((1, LANE), F32) for _ in range(2 * ns)))


def _gelu(x):
    t = jnp.tanh(0.7978845608028654 * (x + 0.044715 * x * x * x))
    return 0.5 * x * (1.0 + t)


def _gelu_grad(x):
    t = jnp.tanh(0.7978845608028654 * (x + 0.044715 * x * x * x))
    return 0.5 * (1.0 + t) + 0.5 * x * (1.0 - t * t) * 0.7978845608028654 * (1.0 + 0.134145 * x * x)


def _call(body, *, name, grid, in_specs, out_specs, out_shape, scratch_shapes, vmem, args, semantics,
          copies=None):
    n_i, n_o, n_s = len(in_specs), len(out_specs), len(scratch_shapes)
    if copies is None:
        out = pl.pallas_call(
            body, name=name, grid=grid, in_specs=in_specs, out_specs=out_specs, out_shape=out_shape,
            scratch_shapes=scratch_shapes, compiler_params=_cparams(semantics, vmem))(*args)
        return list(out), []
    n_ci, n_co = len(copies.inputs), len(copies.out_shape)

    def hosted(*refs):
        ins, cin = refs[:n_i], refs[n_i:n_i + n_ci]
        outs = refs[n_i + n_ci:n_i + n_ci + n_o]
        cout = refs[n_i + n_ci + n_o:n_i + n_ci + n_o + n_co]
        scr = refs[n_i + n_ci + n_o + n_co:n_i + n_ci + n_o + n_co + n_s]
        sems = refs[n_i + n_ci + n_o + n_co + n_s:]
        step = pl.program_id(0)
        for axis in range(1, len(grid)):
            step = step * grid[axis] + pl.program_id(axis)
        copies.emit(cin, cout, sems, step, math.prod(grid))
        body(*ins, *outs, *scr)

    out = pl.pallas_call(
        hosted, name=name, grid=grid, in_specs=list(in_specs) + [HBM] * n_ci,
        out_specs=list(out_specs) + [HBM] * n_co, out_shape=list(out_shape) + list(copies.out_shape),
        scratch_shapes=list(scratch_shapes) + list(copies.scratch),
        compiler_params=pltpu.CompilerParams(dimension_semantics=("arbitrary",) * len(grid),
                                             vmem_limit_bytes=vmem, has_side_effects=True))(*args, *copies.inputs)
    return list(out[:n_o]), list(out[n_o:])


def merge_copies(group):
    group = [c for c in group if c is not None]
    if len(group) <= 1:
        return group[0] if group else None
    bounds, i0, o0, s0 = [], 0, 0, 0
    for c in group:
        bounds.append((i0, o0, s0))
        i0, o0, s0 = i0 + len(c.inputs), o0 + len(c.out_shape), s0 + len(c.scratch)
    phases = []
    for c, (i, o, s) in zip(group, bounds):
        for frac, fn in c.phases:
            def shifted(ins, outs, sems, fn=fn, c=c, i=i, o=o, s=s):
                fn(ins[i:i + len(c.inputs)], outs[o:o + len(c.out_shape)], sems[s:s + len(c.scratch)])
            phases.append((frac, shifted))
    return HostedCopies([a for c in group for a in c.inputs], [a for c in group for a in c.out_shape],
                        [a for c in group for a in c.scratch], phases)


def ssm_fwd(name, proj, wbr, wbi, zr, zi, wcr, wci, dskip, D, copies=None):
    L = proj.shape[0]
    DS = D // 4
    NC = DS // CHUNK_U
    uo = (5 * D // 2) // CHUNK_U
    ch = _pick(L, 256)

    def body(u_ref, wbr_ref, wbi_ref, zr_ref, zi_ref, wcr_ref, wci_ref, ds_ref,
             y_ref, hg_ref, xr_ref, xi_ref, sr, si, pr_s, pi_s):
        def fill(ci, carry):
            rows = pl.ds(pl.multiple_of(ci * ch, ch), ch)
            ub = u_ref[rows, :].astype(BF16)
            sr[rows, :] = _dot(ub, wbr_ref[...])
            si[rows, :] = _dot(ub, wbi_ref[...])
            return carry

        lax.fori_loop(0, L // ch, fill, 0)
        _power_table(pr_s, pi_s, zr_ref[...], zi_ref[...], L, reverse=False)
        _scan(sr, si, pr_s, pi_s, L, reverse=False)

        def emit(ci, carry):
            rows = pl.ds(pl.multiple_of(ci * ch, ch), ch)
            xrb, xib = sr[rows, :].astype(BF16), si[rows, :].astype(BF16)
            xr_ref[rows, :] = xrb
            xi_ref[rows, :] = xib
            y = _dot(xrb, wcr_ref[...]) - _dot(xib, wci_ref[...]) + ds_ref[...] * u_ref[rows, :]
            y_ref[rows, :] = y
            hg_ref[rows, :] = _gelu(y).astype(BF16)
            return carry

        lax.fori_loop(0, L // ch, emit, 0)

    ucol = pl.BlockSpec((L, CHUNK_U), lambda k: (0, k))
    xcol = pl.BlockSpec((L, CHUNK_X), lambda k: (0, k))
    sd = jax.ShapeDtypeStruct
    return _call(
        body, name=name, grid=(NC,),
        in_specs=[pl.BlockSpec((L, CHUNK_U), lambda k: (0, uo + k)),
                  pl.BlockSpec((None, CHUNK_U, CHUNK_X), lambda k: (k, 0, 0)),
                  pl.BlockSpec((None, CHUNK_U, CHUNK_X), lambda k: (k, 0, 0)),
                  pl.BlockSpec((1, CHUNK_X), lambda k: (0, k)), pl.BlockSpec((1, CHUNK_X), lambda k: (0, k)),
                  pl.BlockSpec((None, CHUNK_X, CHUNK_U), lambda k: (k, 0, 0)),
                  pl.BlockSpec((None, CHUNK_X, CHUNK_U), lambda k: (k, 0, 0)),
                  pl.BlockSpec((1, CHUNK_U), lambda k: (0, k))],
        out_specs=[ucol, ucol, xcol, xcol],
        out_shape=[sd((L, DS), F32), sd((L, DS), BF16), sd((L, 4 * DS), BF16), sd((L, 4 * DS), BF16)],
        scratch_shapes=[pltpu.VMEM((L, CHUNK_X), F32), pltpu.VMEM((L, CHUNK_X), F32),
                        pltpu.VMEM((_scan_rows(L), CHUNK_X), F32), pltpu.VMEM((_scan_rows(L), CHUNK_X), F32)],
        vmem=VMEM_BIG, args=(proj, wbr, wbi, zr, zi, wcr, wci, dskip), semantics=("parallel",), copies=copies)


def ssm_bwd(name, dhg, ypre, proj, xr, xi, wbr, wbi, zr, zi, wcr, wci, dskip, D, copies=None):
    L = proj.shape[0]
    DS = D // 4
    NC = DS // CHUNK_U
    uo = (5 * D // 2) // CHUNK_U
    ch = _pick(L, 256)
    nch = L // ch
    halo = 16

    def body(dhg_ref, y_ref, u_ref, xr_ref, xi_ref, wbr_ref, wbi_ref, zr_ref, zi_ref, wcr_ref, wci_ref,
             ds_ref, du_ref, dwcr_ref, dwci_ref, dwbr_ref, dwbi_ref, dar_ref, dai_ref, dds_ref,
             gr, gi, duf, pr_s, pi_s):
        dwcr_ref[...] = jnp.zeros_like(dwcr_ref)
        dwci_ref[...] = jnp.zeros_like(dwci_ref)
        dwbr_ref[...] = jnp.zeros_like(dwbr_ref)
        dwbi_ref[...] = jnp.zeros_like(dwbi_ref)
        dar_ref[...] = jnp.zeros_like(dar_ref)
        dai_ref[...] = jnp.zeros_like(dai_ref)
        dds_ref[...] = jnp.zeros_like(dds_ref)

        def first(ci, carry):
            rows = pl.ds(pl.multiple_of(ci * ch, ch), ch)
            dy = dhg_ref[rows, :] * _gelu_grad(y_ref[rows, :])
            dyb = dy.astype(BF16)
            dds_ref[...] += jnp.sum(dy * u_ref[rows, :], axis=0, keepdims=True)
            duf[rows, :] = ds_ref[...] * dy
            gr[rows, :] = _dot(dyb, wcr_ref[...], NT)
            gi[rows, :] = -_dot(dyb, wci_ref[...], NT)
            dwcr_ref[...] += _dot(xr_ref[rows, :], dyb, TN)
            dwci_ref[...] -= _dot(xi_ref[rows, :], dyb, TN)
            return carry

        lax.fori_loop(0, nch, first, 0)
        _power_table(pr_s, pi_s, zr_ref[...], -zi_ref[...], L, reverse=True)
        _scan(gr, gi, pr_s, pi_s, L, reverse=True)

        def lam_grad(gxr, gxi, xpr, xpi):
            pr, pi = _cmul_conj(xpr, xpi, gxr, gxi)
            dar_ref[...] += jnp.sum(pr, axis=0, keepdims=True)
            dai_ref[...] += jnp.sum(pi, axis=0, keepdims=True)

        def second(ci, carry):
            r0 = pl.multiple_of(ci * ch, ch)
            rows = pl.ds(r0, ch)
            gxr, gxi = gr[rows, :], gi[rows, :]
            gxrb, gxib = gxr.astype(BF16), gxi.astype(BF16)
            du_ref[rows, :] = (duf[rows, :] + _dot(gxrb, wbr_ref[...], NT) + _dot(gxib, wbi_ref[...], NT)).astype(BF16)
            ub = u_ref[rows, :].astype(BF16)
            dwbr_ref[...] += _dot(ub, gxrb, TN)
            dwbi_ref[...] += _dot(ub, gxib, TN)
            return carry

        lax.fori_loop(0, nch, second, 0)

        ridx = lax.broadcasted_iota(jnp.int32, (ch, CHUNK_X), 0)
        xpr = jnp.where(ridx >= 1, pltpu.roll(xr_ref[0:ch, :].astype(F32), 1, 0), 0.0)
        xpi = jnp.where(ridx >= 1, pltpu.roll(xi_ref[0:ch, :].astype(F32), 1, 0), 0.0)
        lam_grad(gr[0:ch, :], gi[0:ch, :], xpr, xpi)

        def third(ci, carry):
            r0 = pl.multiple_of(ci * ch, ch)
            ext = pl.ds(pl.multiple_of(r0 - halo, halo), ch + halo)
            xpr = pltpu.roll(xr_ref[ext, :].astype(F32), 1, 0)[halo:]
            xpi = pltpu.roll(xi_ref[ext, :].astype(F32), 1, 0)[halo:]
            lam_grad(gr[pl.ds(r0, ch), :], gi[pl.ds(r0, ch), :], xpr, xpi)
            return carry

        if nch > 1:
            lax.fori_loop(1, nch, third, 0)

    ucol = pl.BlockSpec((L, CHUNK_U), lambda k: (0, k))
    xcol = pl.BlockSpec((L, CHUNK_X), lambda k: (0, k))
    wb_spec = pl.BlockSpec((None, CHUNK_U, CHUNK_X), lambda k: (k, 0, 0))
    wc_spec = pl.BlockSpec((None, CHUNK_X, CHUNK_U), lambda k: (k, 0, 0))
    avec = pl.BlockSpec((1, CHUNK_X), lambda k: (0, k))
    uvec = pl.BlockSpec((1, CHUNK_U), lambda k: (0, k))
    sd = jax.ShapeDtypeStruct
    return _call(
        body, name=name, grid=(NC,),
        in_specs=[ucol, ucol, pl.BlockSpec((L, CHUNK_U), lambda k: (0, uo + k)), xcol, xcol,
                  wb_spec, wb_spec, avec, avec, wc_spec, wc_spec, uvec],
        out_specs=[ucol, wc_spec, wc_spec, wb_spec, wb_spec, avec, avec, uvec],
        out_shape=[sd((L, DS), BF16), sd((NC, CHUNK_X, CHUNK_U), F32), sd((NC, CHUNK_X, CHUNK_U), F32),
                   sd((NC, CHUNK_U, CHUNK_X), F32), sd((NC, CHUNK_U, CHUNK_X), F32),
                   sd((1, 4 * DS), F32), sd((1, 4 * DS), F32), sd((1, DS), F32)],
        scratch_shapes=[pltpu.VMEM((L, CHUNK_X), F32), pltpu.VMEM((L, CHUNK_X), F32), pltpu.VMEM((L, CHUNK_U), F32),
                        pltpu.VMEM((_scan_rows(L), CHUNK_X), F32), pltpu.VMEM((_scan_rows(L), CHUNK_X), F32)],
        vmem=VMEM_BIG, args=(dhg, ypre, proj, xr, xi, wbr, wbi, zr, zi, wcr, wci, dskip),
        semantics=("parallel",), copies=copies)


def _block_diag(w, transpose):
    G = w.shape[0]
    nc = G // GROUPS_PER_CHUNK
    w4 = w.reshape(nc, GROUPS_PER_CHUNK, SSM_GROUP, SSM_STATE)
    eye = jnp.eye(GROUPS_PER_CHUNK, dtype=w.dtype)
    if transpose:
        return (w4[:, None, :, :, :].transpose(0, 1, 4, 2, 3) * eye[None, :, None, :, None]).reshape(
            nc, CHUNK_X, CHUNK_U).astype(BF16)
    return (w4[:, :, :, None, :] * eye[None, :, None, :, None]).reshape(nc, CHUNK_U, CHUNK_X).astype(BF16)


def _diag_blocks(dw, transpose):
    nc = dw.shape[0]
    gpc = GROUPS_PER_CHUNK
    if transpose:
        d5 = dw.reshape(nc, gpc, SSM_STATE, gpc, SSM_GROUP)
        blocks = [d5[:, g, :, g, :] for g in range(gpc)]
        return jnp.stack(blocks, axis=1).transpose(0, 1, 3, 2).reshape(nc * gpc, SSM_GROUP, SSM_STATE)
    d5 = dw.reshape(nc, gpc, SSM_GROUP, gpc, SSM_STATE)
    blocks = [d5[:, g, :, g, :] for g in range(gpc)]
    return jnp.stack(blocks, axis=1).reshape(nc * gpc, SSM_GROUP, SSM_STATE)


SHARD_BLOCK_ELEMS = 128 * 1024


def _shard_rows(R, C, scale):
    return _pick(R, max(8, scale * SHARD_BLOCK_ELEMS // C))


def cast_bf16(name, w, layer, parts=1):
    shape = w.shape[1:]
    w3 = w.reshape(w.shape[0], -1, shape[-1])
    _, R, C = w3.shape
    tr = _shard_rows(R, C, 4)
    cw = C // parts

    def body(w_ref, *o_refs):
        for g, o_ref in enumerate(o_refs):
            o_ref[...] = w_ref[:, g * cw:(g + 1) * cw].astype(BF16)

    out = pl.pallas_call(body, name=name, grid=(R // tr,),
                         in_specs=[pl.BlockSpec((None, tr, C), lambda i: (layer, i, 0))],
                         out_specs=[pl.BlockSpec((tr, cw), lambda i: (i, 0))] * parts,
                         out_shape=[jax.ShapeDtypeStruct((R, cw), BF16)] * parts,
                         compiler_params=_cparams(("parallel",), VMEM_MID))(w3)
    return [o.reshape(shape[:-1] + (cw,)) for o in out]


def _adamw(w, g, m, v):
    m = ADAM_B1 * m + (1.0 - ADAM_B1) * g
    v = ADAM_B2 * v + (1.0 - ADAM_B2) * (g * g)
    delta = -ADAM_LR * ((m * ADAM_C1) / (jnp.sqrt(v * ADAM_C2) + ADAM_EPS) + ADAM_WD * w)
    return delta, m, v


def _own_core(g4):
    return (lambda p: p[0]) if g4.shape[1] == 2 else (lambda p: 0)


def chip_partial(name, pos, g4, recv_a):
    _, _, R, C = g4.shape
    tr = _shard_rows(R, C, 4)
    core = _own_core(g4)

    def body(pos_ref, g_ref, a_ref, o_ref):
        o_ref[...] = (g_ref[...] + a_ref[...]).astype(BF16)

    return pl.pallas_call(
        body, name=name,
        grid_spec=pltpu.PrefetchScalarGridSpec(
            num_scalar_prefetch=1, grid=(4, R // tr),
            in_specs=[pl.BlockSpec((None, None, tr, C), lambda q, i, p: (q, core(p), i, 0)),
                      pl.BlockSpec((None, tr, C), lambda q, i, p: (q, i, 0))],
            out_specs=pl.BlockSpec((None, tr, C), lambda q, i, p: (q, i, 0))),
        out_shape=jax.ShapeDtypeStruct((4, R, C), BF16),
        compiler_params=_cparams(("parallel", "parallel"), VMEM_MID))(pos, g4, recv_a)


def adamw_shard(name, pos, layer, g4, recv_a, recv_b, w, m, v, prev):
    _, _, R, C = g4.shape
    tr = _shard_rows(R, C, 1)
    n_prev = 0 if prev is None else 4
    core = _own_core(g4)

    def body(pos_ref, g_ref, a_ref, b_ref, w_ref, m_ref, v_ref, *rest):
        go_ref, d_ref, mo_ref, vo_ref = rest[n_prev:]
        gs = g_ref[...] + a_ref[...]
        for j in range(3):
            gs = gs + b_ref[j].astype(F32)
        delta, mn, vn = _adamw(w_ref[...], gs, m_ref[...], v_ref[...])
        go_ref[...] = gs
        d_ref[...] = delta
        mo_ref[...] = mn
        vo_ref[...] = vn

    lay = pl.BlockSpec((None, tr, C), lambda i, p: (layer, i, 0))
    in_specs = [pl.BlockSpec((None, None, tr, C), lambda i, p: (p[1], core(p), i, 0)),
                pl.BlockSpec((None, tr, C), lambda i, p: (p[1], i, 0)),
                pl.BlockSpec((3, tr, C), lambda i, p: (0, i, 0)), lay, lay, lay]
    args = [g4, recv_a, recv_b, w, m, v]
    aliases = {}
    if prev is not None:
        in_specs += [pl.BlockSpec(memory_space=pl.ANY)] * 4
        args += list(prev)
        aliases = {7 + j: j for j in range(4)}
    return pl.pallas_call(
        body, name=name,
        grid_spec=pltpu.PrefetchScalarGridSpec(
            num_scalar_prefetch=1, grid=(R // tr,), in_specs=in_specs, out_specs=[lay] * 4),
        out_shape=[jax.ShapeDtypeStruct(w.shape, F32)] * 4,
        input_output_aliases=aliases,
        compiler_params=_cparams(("parallel",), VMEM_MID))(pos, *args)


def adamw_small(name, gathered, w, m, v):
    _, R, C = gathered.shape
    tr = _pick(R, 512)

    def body(g_ref, w_ref, m_ref, v_ref, go_ref, d_ref, mo_ref, vo_ref):
        gs = g_ref[0]
        for j in range(1, N_DEV):
            gs = gs + g_ref[j]
        delta, mn, vn = _adamw(w_ref[...], gs, m_ref[...], v_ref[...])
        go_ref[...] = gs
        d_ref[...] = delta
        mo_ref[...] = mn
        vo_ref[...] = vn

    spec = pl.BlockSpec((tr, C), lambda i: (i, 0))
    return pl.pallas_call(
        body, name=name, grid=(R // tr,),
        in_specs=[pl.BlockSpec((N_DEV, tr, C), lambda i: (0, i, 0)), spec, spec, spec], out_specs=[spec] * 4,
        out_shape=[jax.ShapeDtypeStruct((R, C), F32)] * 4,
        compiler_params=_cparams(("parallel",), VMEM_MID))(gathered, w, m, v)


def _position():
    return lax.axis_index("x"), lax.axis_index("y"), lax.axis_index("c")


FORWARD_AT = 0.88


def gather_copies(shards):
    n = len(shards)

    def parts(ins, outs, sems):
        send_sems, recv_sems, local_sems = sems
        x, y, c = _position()
        me, sibling = (x, y, c), (x, y, 1 - c)
        chips = [(1 - x, y), (x, 1 - y), (1 - x, 1 - y)]

        def copy(a, k, block, to, src=None):
            blk = outs[a].at[4 * block[0] + 2 * block[1] + block[2]]
            return pltpu.make_async_remote_copy(
                src_ref=blk if src is None else src, dst_ref=blk,
                send_sem=send_sems.at[a, k], recv_sem=recv_sems.at[a, k], device_id=to, device_id_type=MESH)

        mine = [pltpu.make_async_copy(ins[a], outs[a].at[4 * x + 2 * y + c], local_sems.at[a]) for a in range(n)]
        first = [[copy(a, 0, me, sibling, src=ins[a])] +
                 [copy(a, 1 + j, me, (*chip, c), src=ins[a]) for j, chip in enumerate(chips)] for a in range(n)]
        landed = [[copy(a, 1 + j, (*chip, c), me) for j, chip in enumerate(chips)] for a in range(n)]
        passed = [[copy(a, 4 + j, (*chip, c), sibling) for j, chip in enumerate(chips)] for a in range(n)]
        from_sibling = [[copy(a, 0, sibling, me)] +
                        [copy(a, 4 + j, (*chip, 1 - c), me) for j, chip in enumerate(chips)] for a in range(n)]
        return mine, first, landed, passed, from_sibling

    def start(ins, outs, sems):
        mine, first, _, _, _ = parts(ins, outs, sems)
        for a in range(n):
            mine[a].start()
            for cp in first[a]:
                cp.start()

    def forward(ins, outs, sems):
        _, _, landed, passed, _ = parts(ins, outs, sems)
        for a in range(n):
            for j in range(3):
                landed[a][j].wait_recv()
                passed[a][j].start()

    def finish(ins, outs, sems):
        mine, first, _, passed, from_sibling = parts(ins, outs, sems)
        for a in range(n):
            for cp in from_sibling[a]:
                cp.wait_recv()
        for a in range(n):
            for cp in first[a] + passed[a]:
                cp.wait_send()
            mine[a].wait()

    return HostedCopies(
        list(shards), [jax.ShapeDtypeStruct((N_DEV,) + s.shape, s.dtype) for s in shards],
        [pltpu.SemaphoreType.DMA((n, 7)), pltpu.SemaphoreType.DMA((n, 7)), pltpu.SemaphoreType.DMA((n,))],
        [(0.0, start), (FORWARD_AT, forward), (1.0, finish)])


def _exchange_copies(arrays, out_lead, make):
    n = len(arrays)

    def all_copies(ins, outs, sems):
        send_sems, recv_sems = sems
        return [make(ins[a], outs[a], send_sems.at[a, k], recv_sems.at[a, k], k)
                for a in range(n) for k in range(out_lead)]

    def start(ins, outs, sems):
        for cp in all_copies(ins, outs, sems):
            cp.start()

    def finish(ins, outs, sems):
        for cp in all_copies(ins, outs, sems):
            cp.wait()

    return HostedCopies(
        list(arrays), [jax.ShapeDtypeStruct((out_lead,) + a.shape[2:], a.dtype) for a in arrays],
        [pltpu.SemaphoreType.DMA((n, out_lead)), pltpu.SemaphoreType.DMA((n, out_lead))],
        [(0.0, start), (1.0, finish)])


def sibling_copies(grads):
    def make(src, dst, send_sem, recv_sem, q):
        x, y, c = _position()
        core = 1 - c if src.shape[1] == 2 else 0
        return pltpu.make_async_remote_copy(
            src_ref=src.at[q, core], dst_ref=dst.at[q], send_sem=send_sem, recv_sem=recv_sem,
            device_id=(x, y, 1 - c), device_id_type=MESH)

    return _exchange_copies(grads, 4, make)


def chip_copies(parts):
    def make(src, dst, send_sem, recv_sem, j):
        x, y, c = _position()
        chip = [(1 - x, y), (x, 1 - y), (1 - x, 1 - y)][j]
        return pltpu.make_async_remote_copy(
            src_ref=src.at[2 * chip[0] + chip[1], 0], dst_ref=dst.at[j], send_sem=send_sem, recv_sem=recv_sem,
            device_id=(*chip, c), device_id_type=MESH)

    return _exchange_copies(parts, 3, make)


class Carrier:
    def __init__(self):
        self.plan = {}
        self.counts = {}

    def ride(self, site, make, store):
        self.plan.setdefault(site, []).append((make, store))

    def make(self, site, ctx=None):
        if site not in self.plan:
            return None
        group = [make(ctx) for make, _ in self.plan[site]]
        self.counts[site] = [len(c.out_shape) for c in group]
        return merge_copies(group)

    def store(self, site, results):
        if site in self.plan:
            at = 0
            for (_, store), n in zip(self.plan[site], self.counts[site]):
                store(results[at:at + n])
                at += n

    def split(self, site, out):
        if site not in self.plan:
            return out
        self.store(site, out[1])
        return out[0]


def _pool_weight(gathered):
    PG = gathered.shape[-1]
    return gathered.transpose(1, 0, 2, 3).reshape(N_POOL_GROUPS, PG, PG)


def _layer_params(l, ln_g, pool_scale, lam_re, lam_im, log_dt, b_re, b_im, c_re, c_im,
                  d_skip, b_glu, branch_g):
    G, P = lam_re.shape[1:]
    p = dict(
        ln_g=ln_g[l][None, :], pool_scale=pool_scale[l][None, :], d_skip=d_skip[l][None, :],
        b_glu=b_glu[l][None, :], branch_g=branch_g[l][None, :],
        lr=lam_re[l].reshape(G, 1, P), li=lam_im[l].reshape(G, 1, P), ld=log_dt[l].reshape(G, 1, 1),
        br=b_re[l].transpose(0, 2, 1), bi=b_im[l].transpose(0, 2, 1), cr=c_re[l], ci=c_im[l])
    return p


def layer_fwd(l, x, p, gw, D, carrier, staged_parts=0):
    t = f"l{l}_"
    h = rms_fwd(t + "rms_fwd", x, p["ln_g"])
    if staged_parts:
        proj = None
        for g in range(staged_parts):
            site = (l, f"proj{g}")
            proj = carrier.split(site, mm_nn_part(t + f"proj{g}", h, gw("w_in", l, g), g, staged_parts,
                                                  into=proj, copies=carrier.make(site)))
    else:
        site = (l, "proj")
        proj = carrier.split(site, mm_nn_gathered(t + "proj", h, gw("w_in", l), copies=carrier.make(site)))
    wp = _pool_weight(gw("w_pool", l)[0])
    ypool = pool_fwd(t + "pool_fwd", proj, wp, p["pool_scale"])
    site = (l, "attn_fwd")
    yattn, landed = attn_fwd(t + "attn_fwd", proj, D, copies=carrier.make(site))
    carrier.store(site, landed)
    zr, zi, bbr, bbi = ssm_prep(t + "ssm_prep", p["lr"], p["li"], p["ld"], p["br"], p["bi"])
    ssm_w = dict(wbr=_block_diag(bbr, False), wbi=_block_diag(bbi, False),
                 zr=zr.reshape(1, -1), zi=zi.reshape(1, -1),
                 wcr=_block_diag(p["cr"], True), wci=_block_diag(p["ci"], True))
    site = (l, "ssm_fwd")
    (ypre, hg, xr, xi), landed = ssm_fwd(
        t + "ssm_fwd", proj, ssm_w["wbr"], ssm_w["wbi"], ssm_w["zr"], ssm_w["zi"],
        ssm_w["wcr"], ssm_w["wci"], p["d_skip"], D, copies=carrier.make(site))
    carrier.store(site, landed)
    glu_pre = mm_nn_gathered(t + "glu", hg, gw("w_glu", l))
    y = branch_fwd(t + "branch_fwd", ypool, yattn, glu_pre, proj, p["b_glu"], p["branch_g"])
    out = mm_plain(t + "out", y, gw("w_out", l)[0].reshape(D, D), NN, res=x)
    saved = dict(x=x, h=h, proj=proj, ypool=ypool, yattn=yattn, ypre=ypre, hg=hg, xr=xr, xi=xi,
                 glu_pre=glu_pre, y=y, ssm_w=ssm_w, wp=wp)
    return out, saved


def layer_bwd(l, dres, dres_b, s, p, gw, D, carrier, pos, split_w_in):
    t = f"l{l}_"
    proj = s["proj"]

    def by_target(g):
        return g.reshape(4, 2, -1, g.shape[-1])

    big = {}
    w_out_g = gw("w_out", l)[0].reshape(D, D)
    site = (l, "dy")
    dy = carrier.split(site, mm_plain(t + "dy", dres_b, w_out_g, NT, copies=carrier.make(site)))
    big["w_out"] = by_target(mm_plain(t + "dw_out", s["y"], dres_b, TN).reshape(N_DEV, D // N_DEV, D))
    dypool, dyattn, dglu, dpg, dag, dsg, dbg, dbglu = branch_bwd(
        t + "branch_bwd", dy, s["ypool"], s["yattn"], s["glu_pre"], proj, p["b_glu"], p["branch_g"])
    dhg = mm_nt_gathered(t + "dhg", dglu, gw("w_glu", l))
    big["w_glu"] = by_target(mm_tn_scattered(t + "dw_glu", s["hg"], dglu))
    w = s["ssm_w"]
    site = (l, "ssm_bwd")
    (du, dwcr, dwci, dwbr, dwbi, dar, dai, dds), landed = ssm_bwd(
        t + "ssm_bwd", dhg, s["ypre"], proj, s["xr"], s["xi"], w["wbr"], w["wbi"], w["zr"], w["zi"],
        w["wcr"], w["wci"], p["d_skip"], D, copies=carrier.make(site, big))
    carrier.store(site, landed)
    G, _, P = p["lr"].shape
    dlr, dli, dld, dbr, dbi = ssm_prep_bwd(
        t + "ssm_prep_bwd", p["lr"], p["li"], p["ld"], p["br"], p["bi"],
        dar.reshape(G, 1, P), dai.reshape(G, 1, P), _diag_blocks(dwbr, False), _diag_blocks(dwbi, False))
    site = (l, "attn_bwd")
    (dq, dk, dv), landed = attn_bwd(t + "attn_bwd", proj, s["yattn"], dyattn, D, copies=carrier.make(site, big))
    carrier.store(site, landed)
    dxp, dwp, dps = pool_bwd(t + "pool_bwd", dypool, proj, s["wp"], p["pool_scale"])
    dproj = jnp.concatenate([dxp, dpg, dq, dk, dv, dag, du, dsg], axis=1)
    PG = dwp.shape[1]
    big["w_pool"] = by_target(dwp.reshape(N_POOL_GROUPS, N_DEV, PG // N_DEV, PG).transpose(1, 0, 2, 3))
    if split_w_in:
        site = (l, "dw_in_a")
        to_sibling = carrier.split(site, mm_tn_half(t + "dw_in_a", s["h"], dproj, pos, False,
                                                    copies=carrier.make(site, big)))
        site = (l, "dw_in_b")
        mine = carrier.split(site, mm_tn_half(t + "dw_in_b", s["h"], dproj, pos, True,
                                              copies=carrier.make(site, dict(big, to_sibling=to_sibling[:, None]))))
        big["w_in"] = mine[:, None]
    else:
        big["w_in"] = by_target(mm_tn_scattered(t + "dw_in", s["h"], dproj))
    site = (l, "dh")
    dh = carrier.split(site, mm_nt_gathered(t + "dh", dproj, gw("w_in", l), copies=carrier.make(site, big)))
    dx, dx_b, dlng = rms_bwd(t + "rms_bwd", s["x"], dh, dres, p["ln_g"])
    small = dict(ln_g=dlng[0], pool_scale=dps[0], lam_re=dlr.reshape(G, P), lam_im=dli.reshape(G, P),
                 log_dt=dld.reshape(G), b_re=dbr.transpose(0, 2, 1), b_im=dbi.transpose(0, 2, 1),
                 c_re=_diag_blocks(dwcr, True), c_im=_diag_blocks(dwci, True),
                 d_skip=dds[0], b_glu=dbglu[0], branch_g=dbg[0])
    return dx, dx_b, small


SMALL_NAMES = ("ln_g", "pool_scale", "lam_re", "lam_im", "log_dt", "b_re", "b_im", "c_re", "c_im",
               "d_skip", "b_glu", "branch_g", "final_g")
BIG_NAMES = ("w_in", "w_pool", "w_glu", "w_out")
WEIGHT_ORDER = ("ln_g", "w_in", "w_pool", "pool_scale", "lam_re", "lam_im", "log_dt", "b_re", "b_im",
                "c_re", "c_im", "d_skip", "w_glu", "b_glu", "branch_g", "w_out", "final_g")


PACK_ROWS = 512


def _pack(arrs):
    flat = jnp.concatenate([a.reshape(-1) for a in arrs])
    pad = (-flat.shape[0]) % (PACK_ROWS * LANE)
    return jnp.pad(flat, (0, pad)).reshape(-1, LANE)


def _unpack(packed, like):
    flat = packed.reshape(-1)
    out, off = [], 0
    for a in like:
        out.append(flat[off:off + a.size].reshape(a.shape))
        off += a.size
    return out


def kernel(x, ln_g, w_in, w_pool, pool_scale, lam_re, lam_im, log_dt, b_re, b_im, c_re, c_im, d_skip, w_glu, b_glu, branch_g, w_out, final_g, loss_target, m_ln_g, m_w_in, m_w_pool, m_pool_scale, m_lam_re, m_lam_im, m_log_dt, m_b_re, m_b_im, m_c_re, m_c_im, m_d_skip, m_w_glu, m_b_glu, m_branch_g, m_w_out, m_final_g, v_ln_g, v_w_in, v_w_pool, v_pool_scale, v_lam_re, v_lam_im, v_log_dt, v_b_re, v_b_im, v_c_re, v_c_im, v_d_skip, v_w_glu, v_b_glu, v_branch_g, v_w_out, v_final_g):
    W = dict(ln_g=ln_g, w_in=w_in, w_pool=w_pool, pool_scale=pool_scale, lam_re=lam_re, lam_im=lam_im,
             log_dt=log_dt, b_re=b_re, b_im=b_im, c_re=c_re, c_im=c_im, d_skip=d_skip, w_glu=w_glu,
             b_glu=b_glu, branch_g=branch_g, w_out=w_out, final_g=final_g)
    Mo = dict(ln_g=m_ln_g, w_in=m_w_in, w_pool=m_w_pool, pool_scale=m_pool_scale, lam_re=m_lam_re,
              lam_im=m_lam_im, log_dt=m_log_dt, b_re=m_b_re, b_im=m_b_im, c_re=m_c_re, c_im=m_c_im,
              d_skip=m_d_skip, w_glu=m_w_glu, b_glu=m_b_glu, branch_g=m_branch_g, w_out=m_w_out,
              final_g=m_final_g)
    Vo = dict(ln_g=v_ln_g, w_in=v_w_in, w_pool=v_w_pool, pool_scale=v_pool_scale, lam_re=v_lam_re,
              lam_im=v_lam_im, log_dt=v_log_dt, b_re=v_b_re, b_im=v_b_im, c_re=v_c_re, c_im=v_c_im,
              d_skip=v_d_skip, w_glu=v_w_glu, b_glu=v_b_glu, branch_g=v_branch_g, w_out=v_w_out,
              final_g=v_final_g)
    depth = ln_g.shape[0]
    _, L, D = x.shape
    xc, yc, cc = _position()
    pos = jnp.stack([cc, 2 * xc + yc, 4 * xc + 2 * yc + cc]).astype(jnp.int32)

    def n_parts(n, l):
        return (W_IN_PARTS if l > 0 else W_IN0_PARTS) if n == "w_in" else 1

    shards, landed = {}, {}
    for n in BIG_NAMES:
        for l in range(depth):
            for g, part in enumerate(cast_bf16(f"cast_{n}_{l}", W[n], l, n_parts(n, l))):
                shards[n, l, g] = part
    carrier = Carrier()

    def gw(n, l, g=None):
        return landed[n, l, g] if g is not None else [landed[n, l, i] for i in range(n_parts(n, l))]

    def gather_plan(keys):
        return (lambda ctx: gather_copies([shards[k] for k in keys])), (lambda outs: landed.update(zip(keys, outs)))

    first = [("w_in", 0, 0)] + [("w_pool", l, 0) for l in range(depth)]
    landed.update(zip(first, copies_call("gather_first", gather_copies([shards[k] for k in first]))))
    for g in range(1, W_IN0_PARTS):
        carrier.ride((0, f"proj{g - 1}"), *gather_plan([("w_in", 0, g)]))
    carrier.ride((0, f"proj{W_IN0_PARTS - 1}"), *gather_plan([("w_out", 0, 0), ("w_glu", 0, 0)]))
    for l in range(1, depth):
        for g, call in enumerate(("attn_fwd", "ssm_fwd")):
            carrier.ride((l - 1, call), *gather_plan([("w_in", l, g)]))
        carrier.ride((l, "proj"), *gather_plan([("w_out", l, 0), ("w_glu", l, 0)]))

    own, recv_a, recv_b = {}, {}, {}

    def sibling_plan(l, names, keep, pick):
        def make(ctx):
            own.update({(n, l): ctx[n] for n in keep})
            return sibling_copies(pick(ctx))
        return make, (lambda outs: recv_a.update(zip([(n, l) for n in names], outs)))

    def chip_plan(l, names):
        def make(ctx):
            parts = [chip_partial(f"chip_partial_{n}_{l}", pos, own[n, l], recv_a[n, l])[:, None] for n in names]
            return chip_copies(parts)
        return make, (lambda outs: recv_b.update(zip([(n, l) for n in names], outs)))

    early, late = ("w_out", "w_glu"), ("w_in", "w_pool")
    for l in range(1, depth):
        carrier.ride((l, "dh"), *sibling_plan(l, BIG_NAMES, BIG_NAMES, lambda big: [big[n] for n in BIG_NAMES]))
        carrier.ride((l - 1, "ssm_bwd"), *chip_plan(l, ("w_out", "w_glu", "w_pool")))
        carrier.ride((l - 1, "attn_bwd"), *chip_plan(l, ("w_in",)))
    carrier.ride((0, "ssm_bwd"), *sibling_plan(0, early, early, lambda big: [big[n] for n in early]))
    carrier.ride((0, "dw_in_a"), *chip_plan(0, early))
    carrier.ride((0, "dw_in_b"), *sibling_plan(0, late, ("w_pool",), lambda ctx: [ctx["to_sibling"], ctx["w_pool"]]))

    def last_chip_make(big):
        own["w_in", 0] = big["w_in"]
        return chip_plan(0, late)[0](big)

    carrier.ride((0, "dh"), last_chip_make, chip_plan(0, late)[1])

    params = [_layer_params(l, ln_g, pool_scale, lam_re, lam_im, log_dt, b_re, b_im, c_re, c_im,
                            d_skip, b_glu, branch_g) for l in range(depth)]
    h = x[0]
    saved = []
    for l in range(depth):
        h, s = layer_fwd(l, h, params[l], gw, D, carrier, staged_parts=W_IN0_PARTS if l == 0 else 0)
        saved.append(s)
    loss_part, dres, dres_b, dfinal = loss_head("loss_head", h, final_g[None, :], loss_target[0])
    loss = lax.psum(loss_part[0, 0], ("x", "y", "c"))

    small = [None] * depth
    for l in reversed(range(depth)):
        dres, dres_b, small[l] = layer_bwd(l, dres, dres_b, saved[l], params[l], gw, D, carrier, pos,
                                           split_w_in=(l == 0))
    grad_x = dres[None]

    results = {}
    for n in BIG_NAMES:
        shape = W[n].shape
        R, C = int(math.prod(shape[1:-1])), shape[-1]
        w3, m3, v3 = (t.reshape(depth, R, C) for t in (W[n], Mo[n], Vo[n]))
        prev = None
        for l in range(depth):
            prev = adamw_shard(f"adamw_{n}_{l}", pos, l, own[n, l], recv_a[n, l], recv_b[n, l], w3, m3, v3, prev)
        results[n] = [t.reshape(shape) for t in prev]

    small_like = [W[n] for n in SMALL_NAMES]
    small_grads = [jnp.stack([small[l][n] for l in range(depth)]) for n in SMALL_NAMES[:-1]] + [dfinal[0]]
    gathered = copies_call("gather_small_grads", gather_copies([_pack(small_grads)]))[0]
    packed = adamw_small("adamw_small", gathered, _pack(small_like), _pack([Mo[n] for n in SMALL_NAMES]),
                         _pack([Vo[n] for n in SMALL_NAMES]))
    unpacked = [_unpack(t, small_like) for t in packed]
    for i, n in enumerate(SMALL_NAMES):
        results[n] = [unpacked[j][i] for j in range(4)]

    out = [loss, grad_x]
    for j in range(4):
        out += [results[n][j] for n in WEIGHT_ORDER]
    return tuple(out)
```

```python
import functools
import math

import jax
import jax.numpy as jnp
from jax import lax
from jax.experimental import pallas as pl
from jax.experimental.pallas import tpu as pltpu

F32 = jnp.float32
BF16 = jnp.bfloat16
MESH = pl.DeviceIdType.MESH

EPS = 1e-6
HEAD_DIM = 128
SSM_GROUP = 16
SSM_STATE = 64
GROUPS_PER_CHUNK = 8
CHUNK_U = GROUPS_PER_CHUNK * SSM_GROUP
CHUNK_X = GROUPS_PER_CHUNK * SSM_STATE
N_POOL_GROUPS = 4
POOL_HALO = 16
N_DEV = 8
LANE = 128
FULL_K = 4096
ATTN_TILE = 256
ATTN_DECAY_CUTOFF = 100.0
ROW_TILE = 128
VMEM_BIG = 58 * 1024 * 1024
VMEM_MID = 40 * 1024 * 1024

ADAM_LR = 0.001
ADAM_B1 = 0.9
ADAM_B2 = 0.999
ADAM_EPS = 1e-08
ADAM_WD = 0.01
ADAM_STEP = 10
ADAM_C1 = 1.0 / (1.0 - ADAM_B1 ** ADAM_STEP)
ADAM_C2 = 1.0 / (1.0 - ADAM_B2 ** ADAM_STEP)

NN = (((1,), (0,)), ((), ()))
NT = (((1,), (1,)), ((), ()))
TN = (((0,), (0,)), ((), ()))


def _pick(n, cap):
    if n <= cap:
        return n
    step = LANE if cap >= LANE else 8
    t = (cap // step) * step
    while t > step and n % t:
        t -= step
    assert n % t == 0, (n, cap)
    return t


def _cparams(sem, vmem=None):
    return pltpu.CompilerParams(dimension_semantics=sem, vmem_limit_bytes=vmem)


def _dot(a, b, dn=NN):
    return lax.dot_general(a, b, dn, preferred_element_type=F32)


def _sigmoid(x):
    e = jnp.exp(-jnp.abs(x))
    r = 1.0 / (1.0 + e)
    return jnp.where(x >= 0, r, e * r)


HBM = pl.BlockSpec(memory_space=pl.ANY)


class HostedCopies:
    def __init__(self, inputs, out_shape, scratch, phases):
        self.inputs, self.out_shape, self.scratch, self.phases = inputs, out_shape, scratch, phases

    def emit(self, ins, outs, sems, step, total):
        plan = {}
        for frac, fn in self.phases:
            plan.setdefault(min(total - 1, int(frac * total)), []).append(fn)
        for s in sorted(plan):
            def run(fns=plan[s]):
                for fn in fns:
                    fn(ins, outs, sems)
            if total == 1:
                run()
            else:
                pl.when(step == s)(run)


def copies_call(name, copies):
    n_i, n_o = len(copies.inputs), len(copies.out_shape)

    def body(*refs):
        copies.emit(refs[:n_i], refs[n_i:n_i + n_o], refs[n_i + n_o:], 0, 1)

    return pl.pallas_call(
        body, name=name, in_specs=[HBM] * n_i, out_specs=[HBM] * n_o, out_shape=copies.out_shape,
        scratch_shapes=copies.scratch, compiler_params=pltpu.CompilerParams(has_side_effects=True),
    )(*copies.inputs)


def _matmul(name, a, b, *, grid, a_spec, b_spec, o_spec, out_shape, dn,
            res=None, res_spec=None, pos=None, copies=None, product=None, into=None):
    ni, nj, nk = grid
    bs, b_specs = (list(b), list(b_spec)) if isinstance(b, (list, tuple)) else ([b], [b_spec])
    n_b = len(bs)
    n_pos = 0 if pos is None else 1
    n_res = 0 if res is None else 1
    n_into = 0 if into is None else 1
    n_ci = 0 if copies is None else len(copies.inputs)
    n_co = 0 if copies is None else len(copies.out_shape)

    def body(*refs):
        refs = refs[n_pos:]
        a_ref, b_refs = refs[0], refs[1:1 + n_b]
        r_ref = refs[1 + n_b] if n_res else None
        base = 1 + n_b + n_res + n_into
        cin = refs[base:base + n_ci]
        o_ref = refs[base + n_ci]
        cout = refs[base + n_ci + 1:base + n_ci + 1 + n_co]
        sems = refs[base + n_ci + 1 + n_co:]
        k = pl.program_id(2)
        if copies is not None:
            step = (pl.program_id(0) * nj + pl.program_id(1)) * nk + k
            copies.emit(cin, cout, sems, step, ni * nj * nk)

        if product is None:
            part = _dot(a_ref[...].astype(BF16), b_refs[0][...].astype(BF16), dn)
        else:
            part = product(a_ref, b_refs)
        if nk == 1:
            if r_ref is not None:
                part = part + r_ref[...]
            o_ref[...] = part.astype(o_ref.dtype)
        else:
            @pl.when(k == 0)
            def _():
                o_ref[...] = part if r_ref is None else part + r_ref[...]

            @pl.when(k > 0)
            def _():
                o_ref[...] += part

    assert nk == 1 or out_shape.dtype == F32
    in_specs = [a_spec] + b_specs + ([res_spec] if n_res else []) + [HBM] * (n_into + n_ci)
    args = (((pos,) if n_pos else ()) + (a, *bs) + ((res,) if n_res else ()) + ((into,) if n_into else ())
            + tuple(copies.inputs if copies else ()))
    aliases = {n_pos + 1 + n_b + n_res: 0} if n_into else {}
    out_specs = [o_spec] + [HBM] * n_co
    out_shapes = [out_shape] + list(copies.out_shape if copies else [])
    scratch = list(copies.scratch if copies else [])
    params = pltpu.CompilerParams(
        dimension_semantics=("arbitrary",) * 3 if copies else ("parallel", "parallel", "arbitrary"),
        vmem_limit_bytes=VMEM_BIG, has_side_effects=copies is not None)
    out = pl.pallas_call(
        body, name=name,
        grid_spec=pltpu.PrefetchScalarGridSpec(
            num_scalar_prefetch=n_pos, grid=grid, in_specs=in_specs, out_specs=out_specs, scratch_shapes=scratch),
        out_shape=out_shapes, input_output_aliases=aliases, compiler_params=params)(*args)
    return out[0] if copies is None else (out[0], list(out[1:]))


def mm_nn_gathered(name, a, parts, out_dtype=F32, copies=None):
    M, K = a.shape
    P, w = len(parts), parts[0].shape[2]
    nper = P * w
    tm, tk, tn = _pick(M, 1024), _pick(K, FULL_K), _pick(w, 768)
    r = w // tn
    per_part = N_DEV * r

    def b_spec(g):
        def index(i, j, k, *_):
            t = jnp.clip(j - g * per_part, 0, per_part - 1)
            return (t // r, k, t % r)
        return pl.BlockSpec((None, tk, tn), index)

    def o_index(i, j, k, *_):
        t = j % per_part
        return (i, (t // r) * (nper // tn) + (j // per_part) * r + t % r)

    def product(a_ref, b_refs):
        j = pl.program_id(1)
        b = b_refs[0][...]
        for g in range(1, P):
            b = jnp.where(j >= g * per_part, b_refs[g][...], b)
        return _dot(a_ref[...].astype(BF16), b.astype(BF16))

    return _matmul(
        name, a, list(parts), grid=(M // tm, P * per_part, K // tk),
        a_spec=pl.BlockSpec((tm, tk), lambda i, j, k, *_: (i, k)),
        b_spec=[b_spec(g) for g in range(P)], o_spec=pl.BlockSpec((tm, tn), o_index),
        out_shape=jax.ShapeDtypeStruct((M, N_DEV * nper), out_dtype), dn=NN, copies=copies,
        product=product if P > 1 else None)


def mm_nn_part(name, a, part, g, P, into=None, out_dtype=F32, copies=None):
    M, K = a.shape
    w = part.shape[2]
    tm, tk, tn = _pick(M, 1024), _pick(K, FULL_K), _pick(w, 768)
    r = w // tn
    return _matmul(
        name, a, part, grid=(M // tm, N_DEV * r, K // tk),
        a_spec=pl.BlockSpec((tm, tk), lambda i, j, k, *_: (i, k)),
        b_spec=pl.BlockSpec((None, tk, tn), lambda i, j, k, *_: (j // r, k, j % r)),
        o_spec=pl.BlockSpec((tm, tn), lambda i, j, k, *_: (i, (j // r) * (P * r) + g * r + j % r)),
        out_shape=jax.ShapeDtypeStruct((M, N_DEV * P * w), out_dtype), dn=NN, copies=copies, into=into)


NT_SLICES = 2
W_IN_PARTS = 2
W_IN0_PARTS = 3


def mm_nt_gathered(name, a, parts, out_dtype=F32, copies=None):
    M, _ = a.shape
    P, (_, N, w) = len(parts), parts[0].shape
    nper = P * w
    tm, tn = _pick(M, 1024), _pick(N, 1024)
    S = NT_SLICES

    def product(a_ref, b_refs):
        total = None
        for s in range(S):
            for g in range(P):
                off = (s * P + g) * w
                term = _dot(a_ref[:, off:off + w].astype(BF16), b_refs[g][s], NT)
                total = term if total is None else total + term
        return total

    return _matmul(
        name, a, list(parts), grid=(M // tm, N // tn, N_DEV // S),
        a_spec=pl.BlockSpec((tm, S * nper), lambda i, j, k, *_: (i, k)),
        b_spec=[pl.BlockSpec((S, tn, w), lambda i, j, k, *_: (k, j, 0)) for _ in range(P)],
        o_spec=pl.BlockSpec((tm, tn), lambda i, j, k, *_: (i, j)),
        out_shape=jax.ShapeDtypeStruct((M, N), out_dtype), dn=NT, copies=copies, product=product)


def mm_tn_scattered(name, a, b, copies=None):
    L, M = a.shape
    nper = b.shape[1] // N_DEV
    tm, tn, tk = _pick(M, 1024), _pick(nper, 768), _pick(L, FULL_K)
    r = nper // tn
    return _matmul(
        name, a, b, grid=(M // tm, N_DEV * r, L // tk),
        a_spec=pl.BlockSpec((tk, tm), lambda i, j, k, *_: (k, i)),
        b_spec=pl.BlockSpec((tk, tn), lambda i, j, k, *_: (k, j)),
        o_spec=pl.BlockSpec((None, tm, tn), lambda i, j, k, *_: (j // r, i, j % r)),
        out_shape=jax.ShapeDtypeStruct((N_DEV, M, nper), F32), dn=TN, copies=copies)


def mm_tn_half(name, a, b, pos, own, copies=None):
    L, M = a.shape
    nper = b.shape[1] // N_DEV
    tm, tn, tk = _pick(M, 1024), _pick(nper, 768), _pick(L, FULL_K)
    r = nper // tn

    def b_map(i, j, k, p):
        core = p[0] if own else 1 - p[0]
        return (k, (2 * (j // r) + core) * r + j % r)

    return _matmul(
        name, a, b, grid=(M // tm, 4 * r, L // tk),
        a_spec=pl.BlockSpec((tk, tm), lambda i, j, k, *_: (k, i)),
        b_spec=pl.BlockSpec((tk, tn), b_map),
        o_spec=pl.BlockSpec((None, tm, tn), lambda i, j, k, *_: (j // r, i, j % r)),
        out_shape=jax.ShapeDtypeStruct((4, M, nper), F32), dn=TN, pos=pos, copies=copies)


def mm_plain(name, a, b, dn, out_dtype=F32, res=None, copies=None):
    if dn == NN:
        (M, K), N = a.shape, b.shape[1]
    elif dn == NT:
        (M, K), N = a.shape, b.shape[0]
    else:
        (K, M), N = a.shape, b.shape[1]
    tm, tn, tk = _pick(M, 1024), _pick(N, 512), _pick(K, FULL_K)
    a_spec =(pl.BlockSpec((tk, tm), lambda i, j, k, *_: (k, i)) if dn == TN
              else pl.BlockSpec((tm, tk), lambda i, j, k, *_: (i, k)))
    b_spec = (pl.BlockSpec((tn, tk), lambda i, j, k, *_: (j, k)) if dn == NT
              else pl.BlockSpec((tk, tn), lambda i, j, k, *_: (k, j)))
    o_spec = pl.BlockSpec((tm, tn), lambda i, j, k, *_: (i, j))
    return _matmul(
        name, a, b, grid=(M // tm, N // tn, K // tk), a_spec=a_spec, b_spec=b_spec, o_spec=o_spec,
        out_shape=jax.ShapeDtypeStruct((M, N), out_dtype), dn=dn,
        res=res, res_spec=o_spec if res is not None else None, copies=copies)


def rms_fwd(name, x, g):
    L, D = x.shape
    tr = _pick(L, ROW_TILE)

    def body(x_ref, g_ref, h_ref):
        xv = x_ref[...]
        r = lax.rsqrt(jnp.mean(xv * xv, axis=-1, keepdims=True) + EPS)
        h_ref[...] = (xv * r * g_ref[...]).astype(BF16)

    return pl.pallas_call(
        body, name=name, grid=(L // tr,),
        in_specs=[pl.BlockSpec((tr, D), lambda i: (i, 0)), pl.BlockSpec((1, D), lambda i: (0, 0))],
        out_specs=pl.BlockSpec((tr, D), lambda i: (i, 0)),
        out_shape=jax.ShapeDtypeStruct((L, D), BF16),
        compiler_params=_cparams(("parallel",), VMEM_MID))(x, g)


def rms_bwd(name, x, dh, dres, g):
    L, D = x.shape
    tr = _pick(L, ROW_TILE)

    def body(x_ref, dh_ref, dr_ref, g_ref, dx_ref, dxb_ref, dg_ref):
        xv = x_ref[...]
        r = lax.rsqrt(jnp.mean(xv * xv, axis=-1, keepdims=True) + EPS)
        xh = xv * r
        dhv = dh_ref[...]
        dn = dhv * g_ref[...]
        dxv = dr_ref[...] + r * (dn - xh * jnp.mean(dn * xh, axis=-1, keepdims=True))
        dx_ref[...] = dxv
        dxb_ref[...] = dxv.astype(BF16)

        @pl.when(pl.program_id(0) == 0)
        def _():
            dg_ref[...] = jnp.zeros_like(dg_ref)

        dg_ref[...] += jnp.sum(dhv * xh, axis=0, keepdims=True)

    row = pl.BlockSpec((tr, D), lambda i: (i, 0))
    vec = pl.BlockSpec((1, D), lambda i: (0, 0))
    return pl.pallas_call(
        body, name=name, grid=(L // tr,), in_specs=[row, row, row, vec], out_specs=[row, row, vec],
        out_shape=[jax.ShapeDtypeStruct((L, D), F32), jax.ShapeDtypeStruct((L, D), BF16),
                   jax.ShapeDtypeStruct((1, D), F32)],
        compiler_params=_cparams(("arbitrary",), VMEM_MID))(x, dh, dres, g)


def loss_head(name, x, g, target):
    L, D = x.shape
    tr = _pick(L, ROW_TILE)

    def body(x_ref, g_ref, t_ref, loss_ref, dx_ref, dxb_ref, dg_ref):
        xv = x_ref[...]
        gv = g_ref[...]
        r = lax.rsqrt(jnp.mean(xv * xv, axis=-1, keepdims=True) + EPS)
        xh = xv * r
        err = xh * gv - t_ref[...]
        dy = err * (1.0 / D)
        dn = dy * gv
        dxv = r * (dn - xh * jnp.mean(dn * xh, axis=-1, keepdims=True))
        dx_ref[...] = dxv
        dxb_ref[...] = dxv.astype(BF16)

        @pl.when(pl.program_id(0) == 0)
        def _():
            dg_ref[...] = jnp.zeros_like(dg_ref)
            loss_ref[...] = jnp.zeros_like(loss_ref)

        dg_ref[...] += jnp.sum(dy * xh, axis=0, keepdims=True)
        row_loss = jnp.sum(err * err, axis=-1, keepdims=True) * (0.5 / D)
        loss_ref[...] += jnp.sum(row_loss, axis=0, keepdims=True)

    row = pl.BlockSpec((tr, D), lambda i: (i, 0))
    vec = pl.BlockSpec((1, D), lambda i: (0, 0))
    one = pl.BlockSpec((1, 1), lambda i: (0, 0))
    return pl.pallas_call(
        body, name=name, grid=(L // tr,), in_specs=[row, vec, row], out_specs=[one, row, row, vec],
        out_shape=[jax.ShapeDtypeStruct((1, 1), F32), jax.ShapeDtypeStruct((L, D), F32),
                   jax.ShapeDtypeStruct((L, D), BF16), jax.ShapeDtypeStruct((1, D), F32)],
        compiler_params=_cparams(("arbitrary",), VMEM_MID))(x, g, target)


def _branch_specs(D, tr):
    DP, DA, DS = D // 4, D // 2, D // 4
    return dict(
        pool=pl.BlockSpec((tr, DP), lambda i: (i, 0)),
        attn=pl.BlockSpec((tr, DA), lambda i: (i, 0)),
        glu=pl.BlockSpec((tr, 2 * DS), lambda i: (i, 0)),
        p_gate=pl.BlockSpec((tr, DP), lambda i: (i, 1)),
        a_gate=pl.BlockSpec((tr, DA), lambda i: (i, 4)),
        s_gate=pl.BlockSpec((tr, DS), lambda i: (i, 11)),
        bglu=pl.BlockSpec((1, 2 * DS), lambda i: (0, 0)),
        bg=pl.BlockSpec((1, D), lambda i: (0, 0)),
        row=pl.BlockSpec((tr, D), lambda i: (i, 0)),
    )


def branch_fwd(name, ypool, yattn, glu_pre, proj, b_glu, branch_g):
    L, DP = ypool.shape
    D = 4 * DP
    DA, DS = D // 2, D // 4
    tr = _pick(L, ROW_TILE)
    s = _branch_specs(D, tr)

    def body(yp_ref, ya_ref, gl_ref, pg_ref, ag_ref, sg_ref, bgl_ref, bg_ref, y_ref):
        pre = gl_ref[...] + bgl_ref[...]
        ys = pre[:, :DS] * _sigmoid(pre[:, DS:])
        bg = bg_ref[...]

        def one(raw, gate, g):
            r = lax.rsqrt(jnp.mean(raw * raw, axis=-1, keepdims=True) + EPS)
            return raw * r * g * (gate * _sigmoid(gate))

        y_ref[:, :DP] = one(yp_ref[...], pg_ref[...], bg[:, :DP]).astype(BF16)
        y_ref[:, DP:DP + DA] = one(ya_ref[...], ag_ref[...], bg[:, DP:DP + DA]).astype(BF16)
        y_ref[:, DP + DA:] = one(ys, sg_ref[...], bg[:, DP + DA:]).astype(BF16)

    return pl.pallas_call(
        body, name=name, grid=(L // tr,),
        in_specs=[s["pool"], s["attn"], s["glu"], s["p_gate"], s["a_gate"], s["s_gate"], s["bglu"], s["bg"]],
        out_specs=s["row"], out_shape=jax.ShapeDtypeStruct((L, D), BF16),
        compiler_params=_cparams(("parallel",), VMEM_MID))(ypool, yattn, glu_pre, proj, proj, proj, b_glu, branch_g)


def branch_bwd(name, dy, ypool, yattn, glu_pre, proj, b_glu, branch_g):
    L, DP = ypool.shape
    D = 4 * DP
    DA, DS = D // 2, D // 4
    tr = _pick(L, ROW_TILE // 2)
    s = _branch_specs(D, tr)

    def body(dy_ref, yp_ref, ya_ref, gl_ref, pg_ref, ag_ref, sg_ref, bgl_ref, bg_ref,
             dyp_ref, dya_ref, dgl_ref, dpg_ref, dag_ref, dsg_ref, dbg_ref, dbgl_ref):
        @pl.when(pl.program_id(0) == 0)
        def _():
            dbg_ref[...] = jnp.zeros_like(dbg_ref)
            dbgl_ref[...] = jnp.zeros_like(dbgl_ref)

        bg = bg_ref[...]

        def one(raw, gate, g, dyb):
            r = lax.rsqrt(jnp.mean(raw * raw, axis=-1, keepdims=True) + EPS)
            n = raw * r
            sg = _sigmoid(gate)
            sl = gate * sg
            dgate = dyb * n * g * (sg * (1.0 + gate * (1.0 - sg)))
            dbg = jnp.sum(dyb * n * sl, axis=0, keepdims=True)
            dn = dyb * g * sl
            draw = r * (dn - n * jnp.mean(dn * n, axis=-1, keepdims=True))
            return draw, dgate, dbg

        draw, dgate, dbg = one(yp_ref[...], pg_ref[...], bg[:, :DP], dy_ref[:, :DP])
        dyp_ref[...] = draw
        dpg_ref[...] = dgate.astype(BF16)
        dbg_ref[:, :DP] += dbg

        draw, dgate, dbg = one(ya_ref[...], ag_ref[...], bg[:, DP:DP + DA], dy_ref[:, DP:DP + DA])
        dya_ref[...] = draw
        dag_ref[...] = dgate.astype(BF16)
        dbg_ref[:, DP:DP + DA] += dbg

        pre = gl_ref[...] + bgl_ref[...]
        val = pre[:, :DS]
        sgt = _sigmoid(pre[:, DS:])
        draw, dgate, dbg = one(val * sgt, sg_ref[...], bg[:, DP + DA:], dy_ref[:, DP + DA:])
        dsg_ref[...] = dgate.astype(BF16)
        dbg_ref[:, DP + DA:] += dbg
        dval = draw * sgt
        dgt = draw * val * sgt * (1.0 - sgt)
        dgl_ref[:, :DS] = dval.astype(BF16)
        dgl_ref[:, DS:] = dgt.astype(BF16)
        dbgl_ref[:, :DS] += jnp.sum(dval, axis=0, keepdims=True)
        dbgl_ref[:, DS:] += jnp.sum(dgt, axis=0, keepdims=True)

    loc = lambda w: pl.BlockSpec((tr, w), lambda i: (i, 0))
    return pl.pallas_call(
        body, name=name, grid=(L // tr,),
        in_specs=[s["row"], s["pool"], s["attn"], s["glu"], s["p_gate"], s["a_gate"], s["s_gate"], s["bglu"], s["bg"]],
        out_specs=[loc(DP), loc(DA), loc(2 * DS), loc(DP), loc(DA), loc(DS), s["bg"], s["bglu"]],
        out_shape=[jax.ShapeDtypeStruct((L, DP), F32), jax.ShapeDtypeStruct((L, DA), F32),
                   jax.ShapeDtypeStruct((L, 2 * DS), BF16), jax.ShapeDtypeStruct((L, DP), BF16),
                   jax.ShapeDtypeStruct((L, DA), BF16), jax.ShapeDtypeStruct((L, DS), BF16),
                   jax.ShapeDtypeStruct((1, D), F32), jax.ShapeDtypeStruct((1, 2 * DS), F32)],
        compiler_params=_cparams(("arbitrary",), VMEM_BIG),
    )(dy, ypool, yattn, glu_pre, proj, proj, proj, b_glu, branch_g)


def _pool_select(g, s2, s4, s8, s16):
    return jnp.where(g == 0, s2, jnp.where(g == 1, s4, jnp.where(g == 2, s8, s16)))


def _pool_window(g):
    return jnp.where(g == 0, 2.0, jnp.where(g == 1, 4.0, jnp.where(g == 2, 8.0, 16.0))).astype(F32)


def _pooled_chunk(pad, g, r0, ch):
    xh = pad[pl.ds(r0, ch + POOL_HALO), :]
    s2 = xh + pltpu.roll(xh, 1, 0)
    s4 = s2 + pltpu.roll(s2, 2, 0)
    s8 = s4 + pltpu.roll(s4, 4, 0)
    s16 = s8 + pltpu.roll(s8, 8, 0)
    win = _pool_select(g, s2, s4, s8, s16)[POOL_HALO:]
    pos = (r0 + 1 + lax.broadcasted_iota(jnp.int32, (ch, 1), 0)).astype(F32)
    return win / jnp.minimum(pos, _pool_window(g)) - xh[POOL_HALO:]


def pool_fwd(name, proj, wp, scale):
    L = proj.shape[0]
    DP = scale.shape[1]
    PG = DP // N_POOL_GROUPS
    ch = _pick(L, 256)

    def body(x_ref, w_ref, s_ref, o_ref, pad):
        g = pl.program_id(0)
        pad[0:POOL_HALO, :] = jnp.zeros((POOL_HALO, PG), F32)
        pad[POOL_HALO:, :] = x_ref[...]

        def chunk(ci, carry):
            r0 = pl.multiple_of(ci * ch, ch)
            pooled = _pooled_chunk(pad, g, r0, ch)
            o_ref[pl.ds(r0, ch), :] = _dot(pooled.astype(BF16), w_ref[...]) * s_ref[...]
            return carry

        lax.fori_loop(0, L // ch, chunk, 0)

    return pl.pallas_call(
        body, name=name, grid=(N_POOL_GROUPS,),
        in_specs=[pl.BlockSpec((L, PG), lambda g: (0, g)), pl.BlockSpec((None, PG, PG), lambda g: (g, 0, 0)),
                  pl.BlockSpec((1, PG), lambda g: (0, g))],
        out_specs=pl.BlockSpec((L, PG), lambda g: (0, g)),
        out_shape=jax.ShapeDtypeStruct((L, DP), F32),
        scratch_shapes=[pltpu.VMEM((L + POOL_HALO, PG), F32)],
        compiler_params=_cparams(("parallel",), VMEM_MID))(proj, wp, scale)


def pool_bwd(name, dyraw, proj, wp, scale):
    L = proj.shape[0]
    DP = scale.shape[1]
    PG = DP // N_POOL_GROUPS
    ch = _pick(L, 256)

    def body(dy_ref, x_ref, w_ref, s_ref, dx_ref, dw_ref, ds_ref, pad, dpad, dpo):
        g = pl.program_id(0)
        pad[0:POOL_HALO, :] = jnp.zeros((POOL_HALO, PG), F32)
        pad[POOL_HALO:, :] = x_ref[...]
        dpad[L:, :] = jnp.zeros((POOL_HALO, PG), F32)
        dw_ref[...] = jnp.zeros_like(dw_ref)
        ds_ref[...] = jnp.zeros_like(ds_ref)
        wv = w_ref[...]
        win_f = _pool_window(g)

        def chunk(ci, carry):
            r0 = pl.multiple_of(ci * ch, ch)
            pooled = _pooled_chunk(pad, g, r0, ch).astype(BF16)
            dyv = dy_ref[pl.ds(r0, ch), :]
            ds_ref[...] += jnp.sum(dyv * _dot(pooled, wv), axis=0, keepdims=True)
            dmixed = (dyv * s_ref[...]).astype(BF16)
            dw_ref[...] += _dot(pooled, dmixed, TN)
            dpooled = _dot(dmixed, wv, NT)
            pos = (r0 + 1 + lax.broadcasted_iota(jnp.int32, (ch, 1), 0)).astype(F32)
            dpad[pl.ds(r0, ch), :] = dpooled / jnp.minimum(pos, win_f)
            dpo[pl.ds(r0, ch), :] = dpooled
            return carry

        lax.fori_loop(0, L // ch, chunk, 0)

        def chunk2(ci, carry):
            r0 = pl.multiple_of(ci * ch, ch)
            n = ch + POOL_HALO
            dm = dpad[pl.ds(r0, n), :]
            s2 = dm + pltpu.roll(dm, n - 1, 0)
            s4 = s2 + pltpu.roll(s2, n - 2, 0)
            s8 = s4 + pltpu.roll(s4, n - 4, 0)
            s16 = s8 + pltpu.roll(s8, n - 8, 0)
            win = _pool_select(g, s2, s4, s8, s16)[:ch]
            dx_ref[pl.ds(r0, ch), :] = (win - dpo[pl.ds(r0, ch), :]).astype(BF16)
            return carry

        lax.fori_loop(0, L // ch, chunk2, 0)

    col = pl.BlockSpec((L, PG), lambda g: (0, g))
    return pl.pallas_call(
        body, name=name, grid=(N_POOL_GROUPS,),
        in_specs=[col, col, pl.BlockSpec((None, PG, PG), lambda g: (g, 0, 0)), pl.BlockSpec((1, PG), lambda g: (0, g))],
        out_specs=[col, pl.BlockSpec((None, PG, PG), lambda g: (g, 0, 0)), pl.BlockSpec((1, PG), lambda g: (0, g))],
        out_shape=[jax.ShapeDtypeStruct((L, DP), BF16), jax.ShapeDtypeStruct((N_POOL_GROUPS, PG, PG), F32),
                   jax.ShapeDtypeStruct((1, DP), F32)],
        scratch_shapes=[pltpu.VMEM((L + POOL_HALO, PG), F32), pltpu.VMEM((L + POOL_HALO, PG), F32),
                        pltpu.VMEM((L, PG), F32)],
        compiler_params=_cparams(("parallel",), VMEM_MID))(dyraw, proj, wp, scale)


def _attn_tile(L):
    return _pick(L, ATTN_TILE)


def _tri(t, strict):
    j = lax.broadcasted_iota(jnp.int32, (t, t), 0)
    s = lax.broadcasted_iota(jnp.int32, (t, t), 1)
    return ((j > s) if strict else (j >= s)).astype(BF16)


def _attn_block(q, kt, rb, after, diagonal):
    tq, tk = q.shape[0], kt.shape[0]
    z = _dot(q, kt, NT)
    e = jnp.exp(-jnp.abs(z))
    l1p = jnp.log(1.0 + e)
    log_sig = jnp.minimum(z, 0.0) - l1p
    b = -jnp.maximum(z, 0.0) - l1p
    causal = None
    if diagonal:
        causal = lax.broadcasted_iota(jnp.int32, (tq, tk), 1) < lax.broadcasted_iota(jnp.int32, (tq, tk), 0)
        b = jnp.where(causal, b, 0.0)
    b_hi = b.astype(BF16)
    b_lo = (b - b_hi.astype(F32)).astype(BF16)
    suffix = _dot(b_hi, after) + _dot(b_lo, after) + rb
    w = jnp.exp(log_sig + suffix)
    if diagonal:
        w = jnp.where(causal, w, 0.0)
    return z, e, causal, b, w


def _attn_sweep(i, visit):
    go = visit(i, True)
    lax.while_loop(lambda c: jnp.logical_and(c[0] >= 0, c[1]),
                   lambda c: (c[0] - 1, visit(c[0], False)), (i - 1, go))


def attn_fwd(name, proj, D, copies=None):
    L = proj.shape[0]
    DA = D // 2
    H = DA // HEAD_DIM
    tq = tk = _attn_tile(L)
    qo, ko, vo = (D // 2) // HEAD_DIM, D // HEAD_DIM, (3 * D // 2) // HEAD_DIM
    scale = HEAD_DIM ** -0.5

    def body(q_ref, k_ref, v_ref, tri_ref, o_ref, kb_s, vb_s, acc, rb):
        i = pl.program_id(1)

        @pl.when(i == 0)
        def _():
            kb_s[...] = k_ref[...].astype(BF16)
            vb_s[...] = v_ref[...].astype(BF16)

        q = (q_ref[...] * scale).astype(BF16)
        acc[...] = jnp.zeros_like(acc)
        rb[...] = jnp.zeros_like(rb)

        def visit(kb, diagonal):
            k0 = pl.multiple_of(kb * tk, tk)
            kt = kb_s[pl.ds(k0, tk), :]
            vt = vb_s[pl.ds(k0, tk), :]
            _, _, _, b, w = _attn_block(q, kt, rb[...], tri_ref[...], diagonal)
            acc[...] += _dot(w.astype(BF16), vt)
            rbn = rb[...] + jnp.sum(b, axis=1, keepdims=True)
            rb[...] = rbn
            return jnp.max(rbn) > -ATTN_DECAY_CUTOFF

        _attn_sweep(i, visit)
        o_ref[...] = acc[...]

    (out,), landed = _call(
        body, name=name, grid=(H, L // tq),
        in_specs=[pl.BlockSpec((tq, HEAD_DIM), lambda h, i: (i, qo + h)),
                  pl.BlockSpec((L, HEAD_DIM), lambda h, i: (0, ko + h)),
                  pl.BlockSpec((L, HEAD_DIM), lambda h, i: (0, vo + h)),
                  pl.BlockSpec((tk, tk), lambda h, i: (0, 0))],
        out_specs=[pl.BlockSpec((tq, HEAD_DIM), lambda h, i: (i, h))],
        out_shape=[jax.ShapeDtypeStruct((L, DA), F32)],
        scratch_shapes=[pltpu.VMEM((L, HEAD_DIM), BF16), pltpu.VMEM((L, HEAD_DIM), BF16),
                        pltpu.VMEM((tq, HEAD_DIM), F32), pltpu.VMEM((tq, 1), F32)],
        vmem=VMEM_MID, args=(proj, proj, proj, _tri(tk, True)), semantics=("arbitrary", "arbitrary"),
        copies=copies)
    return out, landed


def attn_bwd(name, proj, o, do, D, copies=None):
    L = proj.shape[0]
    DA = D // 2
    H = DA // HEAD_DIM
    tq = tk = _attn_tile(L)
    qo, ko, vo = (D // 2) // HEAD_DIM, D // HEAD_DIM, (3 * D // 2) // HEAD_DIM
    scale = HEAD_DIM ** -0.5

    def body(q_ref, k_ref, v_ref, o_ref, do_ref, after_ref, from_ref, dq_ref, dk_ref, dv_ref,
             kb_s, vb_s, dk_s, dv_s, dq_acc, rb, rg):
        i = pl.program_id(1)
        nq = pl.num_programs(1)

        @pl.when(i == 0)
        def _():
            kb_s[...] = k_ref[...].astype(BF16)
            vb_s[...] = v_ref[...].astype(BF16)
            dk_s[...] = jnp.zeros_like(dk_s)
            dv_s[...] = jnp.zeros_like(dv_s)

        q = (q_ref[...] * scale).astype(BF16)
        dob = do_ref[...].astype(BF16)
        delta = jnp.sum(dob.astype(F32) * o_ref[...], axis=1, keepdims=True)
        dq_acc[...] = jnp.zeros_like(dq_acc)
        rb[...] = jnp.zeros_like(rb)
        rg[...] = jnp.zeros_like(rg)

        def visit(kb, diagonal):
            k0 = pl.multiple_of(kb * tk, tk)
            kt = kb_s[pl.ds(k0, tk), :]
            vt = vb_s[pl.ds(k0, tk), :]
            z, e, causal, b, w = _attn_block(q, kt, rb[...], after_ref[...], diagonal)
            wq = w.astype(BF16)
            dw = _dot(dob, vt, NT)
            g = wq.astype(F32) * dw
            g_hi = g.astype(BF16)
            g_lo = (g - g_hi.astype(F32)).astype(BF16)
            from_s = from_ref[...]
            suffix_g = _dot(g_hi, from_s) + _dot(g_lo, from_s) + rg[...]
            before = delta - suffix_g
            r = 1.0 / (1.0 + e)
            sig = jnp.where(z >= 0, r, e * r)
            sig_neg = jnp.where(z >= 0, e * r, r)
            dz = g * sig_neg - before * sig
            if diagonal:
                dz = jnp.where(causal, dz, 0.0)
            dz = dz.astype(BF16)
            dq_acc[...] += _dot(dz, kt)
            dk_s[pl.ds(k0, tk), :] += _dot(dz, q, TN)
            dv_s[pl.ds(k0, tk), :] += _dot(wq, dob, TN)
            rbn = rb[...] + jnp.sum(b, axis=1, keepdims=True)
            rb[...] = rbn
            rg[...] += jnp.sum(g, axis=1, keepdims=True)
            return jnp.max(rbn) > -ATTN_DECAY_CUTOFF

        _attn_sweep(i, visit)
        dq_ref[...] = (dq_acc[...] * scale).astype(BF16)

        @pl.when(i == nq - 1)
        def _():
            dk_ref[...] = dk_s[...].astype(BF16)
            dv_ref[...] = dv_s[...].astype(BF16)

    blk = pl.BlockSpec((tq, HEAD_DIM), lambda h, i: (i, h))
    full = pl.BlockSpec((L, HEAD_DIM), lambda h, i: (0, h))
    return _call(
        body, name=name, grid=(H, L // tq),
        in_specs=[pl.BlockSpec((tq, HEAD_DIM), lambda h, i: (i, qo + h)),
                  pl.BlockSpec((L, HEAD_DIM), lambda h, i: (0, ko + h)),
                  pl.BlockSpec((L, HEAD_DIM), lambda h, i: (0, vo + h)), blk, blk,
                  pl.BlockSpec((tk, tk), lambda h, i: (0, 0)), pl.BlockSpec((tk, tk), lambda h, i: (0, 0))],
        out_specs=[blk, full, full],
        out_shape=[jax.ShapeDtypeStruct((L, DA), BF16)] * 3,
        scratch_shapes=[pltpu.VMEM((L, HEAD_DIM), BF16), pltpu.VMEM((L, HEAD_DIM), BF16),
                        pltpu.VMEM((L, HEAD_DIM), F32), pltpu.VMEM((L, HEAD_DIM), F32),
                        pltpu.VMEM((tq, HEAD_DIM), F32), pltpu.VMEM((tq, 1), F32), pltpu.VMEM((tq, 1), F32)],
        vmem=VMEM_MID, args=(proj, proj, proj, o, do, _tri(tk, True), _tri(tk, False)),
        semantics=("arbitrary", "arbitrary"), copies=copies)


def _cmul(ar, ai, br, bi):
    return ar * br - ai * bi, ar * bi + ai * br


def _cmul_conj(ar, ai, br, bi):
    return ar * br + ai * bi, ar * bi - ai * br


def _ssm_disc(lr, li, ld):
    dt = jnp.exp(ld)
    m = jnp.exp(lr * dt)
    ar, ai = m * jnp.cos(li * dt), m * jnp.sin(li * dt)
    inv = 1.0 / (lr * lr + li * li)
    fr, fi = _cmul(ar - 1.0, ai, lr * inv, -li * inv)
    return dt, ar, ai, fr, fi, inv


def ssm_prep(name, lr, li, ld, br, bi):
    def body(lr_ref, li_ref, ld_ref, br_ref, bi_ref, zr_ref, zi_ref, bbr_ref, bbi_ref):
        dt, _, _, fr, fi, _ = _ssm_disc(lr_ref[...], li_ref[...], ld_ref[...])
        zr_ref[...] = lr_ref[...] * dt
        zi_ref[...] = li_ref[...] * dt
        bbr, bbi = _cmul(fr, fi, br_ref[...], bi_ref[...])
        bbr_ref[...] = bbr
        bbi_ref[...] = bbi

    sd = jax.ShapeDtypeStruct
    return pl.pallas_call(
        body, name=name,
        out_shape=[sd(lr.shape, F32), sd(lr.shape, F32), sd(br.shape, F32), sd(br.shape, F32)],
    )(lr, li, ld, br, bi)


def ssm_prep_bwd(name, lr, li, ld, br, bi, gar, gai, gbr, gbi):
    def body(lr_ref, li_ref, ld_ref, br_ref, bi_ref, gar_ref, gai_ref, gbr_ref, gbi_ref,
             dlr_ref, dli_ref, dld_ref, dbr_ref, dbi_ref):
        lr_, li_ = lr_ref[...], li_ref[...]
        dt, ar, ai, fr, fi, inv = _ssm_disc(lr_, li_, ld_ref[...])
        gbr_, gbi_ = gbr_ref[...], gbi_ref[...]
        dbr, dbi = _cmul_conj(fr, fi, gbr_, gbi_)
        dbr_ref[...] = dbr
        dbi_ref[...] = dbi
        pr, pi = _cmul_conj(br_ref[...], bi_ref[...], gbr_, gbi_)
        gfr = jnp.sum(pr, axis=1, keepdims=True)
        gfi = jnp.sum(pi, axis=1, keepdims=True)
        ilr, ili = lr_ * inv, -li_ * inv
        tr_, ti_ = _cmul_conj(ilr, ili, gfr, gfi)
        gatr, gati = gar_ref[...] + tr_, gai_ref[...] + ti_
        hr, hi = _cmul(fr, fi, ilr, ili)
        t1r, t1i = _cmul_conj(ar * dt, ai * dt, gatr, gati)
        t2r, t2i = _cmul_conj(hr, hi, gfr, gfi)
        dlr_ref[...] = t1r - t2r
        dli_ref[...] = t1i - t2i
        lar, lai = _cmul(lr_, li_, ar, ai)
        gdt, _ = _cmul_conj(lar, lai, gatr, gati)
        dld_ref[...] = jnp.sum(gdt, axis=2, keepdims=True) * dt

    sd = jax.ShapeDtypeStruct
    return pl.pallas_call(
        body, name=name,
        out_shape=[sd(lr.shape, F32), sd(lr.shape, F32), sd(ld.shape, F32), sd(br.shape, F32), sd(br.shape, F32)],
    )(lr, li, ld, br, bi, gar, gai, gbr, gbi)


SCAN_ROWS = 64


def _scan_rows(L):
    return min(SCAN_ROWS, L)


def _power_table(pr_s, pi_s, zr, zi, L, reverse):
    R = _scan_rows(L)
    row = lax.broadcasted_iota(jnp.int32, (R, 1), 0).astype(F32)
    dist = (R - row) if reverse else (row + 1.0)
    mag = jnp.exp(dist * zr)
    pr_s[...] = mag * jnp.cos(dist * zi)
    pi_s[...] = mag * jnp.sin(dist * zi)


def _scan(xr, xi, pr_s, pi_s, L, reverse):
    R = _scan_rows(L)
    nt = L // R
    assert L % R == 0 and R & (R - 1) == 0
    ns = CHUNK_X // LANE
    ridx = lax.broadcasted_iota(jnp.int32, (R, LANE), 0)

    def power(ref, d, cs):
        at = R - d if reverse else d - 1
        return ref[at:at + 1, cs]

    def shift(v, d):
        if d < 8:
            if reverse:
                return jnp.where(ridx < R - d, pltpu.roll(v, R - d, 0), 0.0)
            return jnp.where(ridx >= d, pltpu.roll(v, d, 0), 0.0)
        zeros = jnp.zeros((d, LANE), F32)
        return jnp.concatenate([v[d:], zeros], 0) if reverse else jnp.concatenate([zeros, v[:R - d]], 0)

    def tile(n, carry):
        t = nt - 1 - n if reverse else n
        rows = pl.ds(pl.multiple_of(t * R, R), R)
        edges = []
        for c in range(ns):
            cs = slice(c * LANE, (c + 1) * LANE)
            vr, vi = xr[rows, cs], xi[rows, cs]
            d = 1
            while d < R:
                ar, ai = power(pr_s, d, cs), power(pi_s, d, cs)
                sr, si = shift(vr, d), shift(vi, d)
                vr, vi = vr + ar * sr - ai * si, vi + ar * si + ai * sr
                d *= 2
            cr, ci = carry[2 * c], carry[2 * c + 1]
            pr, pi = pr_s[:, cs], pi_s[:, cs]
            vr, vi = vr + pr * cr - pi * ci, vi + pr * ci + pi * cr
            xr[rows, cs] = vr
            xi[rows, cs] = vi
            edge = slice(0, 1) if reverse else slice(R - 1, R)
            edges += [vr[edge], vi[edge]]
        return tuple(edges)

    lax.fori_loop(0, nt, tile, tuple(jnp.zeros((1, LANE), F32) for _ in range(2 * ns)))


def _gelu(x):
    t = jnp.tanh(0.7978845608028654 * (x + 0.044715 * x * x * x))
    return 0.5 * x * (1.0 + t)


def _gelu_grad(x):
    t = jnp.tanh(0.7978845608028654 * (x + 0.044715 * x * x * x))
    return 0.5 * (1.0 + t) + 0.5 * x * (1.0 - t * t) * 0.7978845608028654 * (1.0 + 0.134145 * x * x)


def _call(body, *, name, grid, in_specs, out_specs, out_shape, scratch_shapes, vmem, args, semantics,
          copies=None):
    n_i, n_o, n_s = len(in_specs), len(out_specs), len(scratch_shapes)
    if copies is None:
        out = pl.pallas_call(
            body, name=name, grid=grid, in_specs=in_specs, out_specs=out_specs, out_shape=out_shape,
            scratch_shapes=scratch_shapes, compiler_params=_cparams(semantics, vmem))(*args)
        return list(out), []
    n_ci, n_co = len(copies.inputs), len(copies.out_shape)

    def hosted(*refs):
        ins, cin = refs[:n_i], refs[n_i:n_i + n_ci]
        outs = refs[n_i + n_ci:n_i + n_ci + n_o]
        cout = refs[n_i + n_ci + n_o:n_i + n_ci + n_o + n_co]
        scr = refs[n_i + n_ci + n_o + n_co:n_i + n_ci + n_o + n_co + n_s]
        sems = refs[n_i + n_ci + n_o + n_co + n_s:]
        step = pl.program_id(0)
        for axis in range(1, len(grid)):
            step = step * grid[axis] + pl.program_id(axis)
        copies.emit(cin, cout, sems, step, math.prod(grid))
        body(*ins, *outs, *scr)

    out = pl.pallas_call(
        hosted, name=name, grid=grid, in_specs=list(in_specs) + [HBM] * n_ci,
        out_specs=list(out_specs) + [HBM] * n_co, out_shape=list(out_shape) + list(copies.out_shape),
        scratch_shapes=list(scratch_shapes) + list(copies.scratch),
        compiler_params=pltpu.CompilerParams(dimension_semantics=("arbitrary",) * len(grid),
                                             vmem_limit_bytes=vmem, has_side_effects=True))(*args, *copies.inputs)
    return list(out[:n_o]), list(out[n_o:])


def merge_copies(group):
    group = [c for c in group if c is not None]
    if len(group) <= 1:
        return group[0] if group else None
    bounds, i0, o0, s0 = [], 0, 0, 0
    for c in group:
        bounds.append((i0, o0, s0))
        i0, o0, s0 = i0 + len(c.inputs), o0 + len(c.out_shape), s0 + len(c.scratch)
    phases = []
    for c, (i, o, s) in zip(group, bounds):
        for frac, fn in c.phases:
            def shifted(ins, outs, sems, fn=fn, c=c, i=i, o=o, s=s):
                fn(ins[i:i + len(c.inputs)], outs[o:o + len(c.out_shape)], sems[s:s + len(c.scratch)])
            phases.append((frac, shifted))
    return HostedCopies([a for c in group for a in c.inputs], [a for c in group for a in c.out_shape],
                        [a for c in group for a in c.scratch], phases)


def ssm_fwd(name, proj, wbr, wbi, zr, zi, wcr, wci, dskip, D, copies=None):
    L = proj.shape[0]
    DS = D // 4
    NC = DS // CHUNK_U
    uo = (5 * D // 2) // CHUNK_U
    ch = _pick(L, 256)

    def body(u_ref, wbr_ref, wbi_ref, zr_ref, zi_ref, wcr_ref, wci_ref, ds_ref,
             y_ref, hg_ref, xr_ref, xi_ref, sr, si, pr_s, pi_s):
        def fill(ci, carry):
            rows = pl.ds(pl.multiple_of(ci * ch, ch), ch)
            ub = u_ref[rows, :].astype(BF16)
            sr[rows, :] = _dot(ub, wbr_ref[...])
            si[rows, :] = _dot(ub, wbi_ref[...])
            return carry

        lax.fori_loop(0, L // ch, fill, 0)
        _power_table(pr_s, pi_s, zr_ref[...], zi_ref[...], L, reverse=False)
        _scan(sr, si, pr_s, pi_s, L, reverse=False)

        def emit(ci, carry):
            rows = pl.ds(pl.multiple_of(ci * ch, ch), ch)
            xrb, xib = sr[rows, :].astype(BF16), si[rows, :].astype(BF16)
            xr_ref[rows, :] = xrb
            xi_ref[rows, :] = xib
            y = _dot(xrb, wcr_ref[...]) - _dot(xib, wci_ref[...]) + ds_ref[...] * u_ref[rows, :]
            y_ref[rows, :] = y
            hg_ref[rows, :] = _gelu(y).astype(BF16)
            return carry

        lax.fori_loop(0, L // ch, emit, 0)

    ucol = pl.BlockSpec((L, CHUNK_U), lambda k: (0, k))
    xcol = pl.BlockSpec((L, CHUNK_X), lambda k: (0, k))
    sd = jax.ShapeDtypeStruct
    return _call(
        body, name=name, grid=(NC,),
        in_specs=[pl.BlockSpec((L, CHUNK_U), lambda k: (0, uo + k)),
                  pl.BlockSpec((None, CHUNK_U, CHUNK_X), lambda k: (k, 0, 0)),
                  pl.BlockSpec((None, CHUNK_U, CHUNK_X), lambda k: (k, 0, 0)),
                  pl.BlockSpec((1, CHUNK_X), lambda k: (0, k)), pl.BlockSpec((1, CHUNK_X), lambda k: (0, k)),
                  pl.BlockSpec((None, CHUNK_X, CHUNK_U), lambda k: (k, 0, 0)),
                  pl.BlockSpec((None, CHUNK_X, CHUNK_U), lambda k: (k, 0, 0)),
                  pl.BlockSpec((1, CHUNK_U), lambda k: (0, k))],
        out_specs=[ucol, ucol, xcol, xcol],
        out_shape=[sd((L, DS), F32), sd((L, DS), BF16), sd((L, 4 * DS), BF16), sd((L, 4 * DS), BF16)],
        scratch_shapes=[pltpu.VMEM((L, CHUNK_X), F32), pltpu.VMEM((L, CHUNK_X), F32),
                        pltpu.VMEM((_scan_rows(L), CHUNK_X), F32), pltpu.VMEM((_scan_rows(L), CHUNK_X), F32)],
        vmem=VMEM_BIG, args=(proj, wbr, wbi, zr, zi, wcr, wci, dskip), semantics=("parallel",), copies=copies)


def ssm_bwd(name, dhg, ypre, proj, xr, xi, wbr, wbi, zr, zi, wcr, wci, dskip, D, copies=None):
    L = proj.shape[0]
    DS = D // 4
    NC = DS // CHUNK_U
    uo = (5 * D // 2) // CHUNK_U
    ch = _pick(L, 256)
    nch = L // ch
    halo = 16

    def body(dhg_ref, y_ref, u_ref, xr_ref, xi_ref, wbr_ref, wbi_ref, zr_ref, zi_ref, wcr_ref, wci_ref,
             ds_ref, du_ref, dwcr_ref, dwci_ref, dwbr_ref, dwbi_ref, dar_ref, dai_ref, dds_ref,
             gr, gi, duf, pr_s, pi_s):
        dwcr_ref[...] = jnp.zeros_like(dwcr_ref)
        dwci_ref[...] = jnp.zeros_like(dwci_ref)
        dwbr_ref[...] = jnp.zeros_like(dwbr_ref)
        dwbi_ref[...] = jnp.zeros_like(dwbi_ref)
        dar_ref[...] = jnp.zeros_like(dar_ref)
        dai_ref[...] = jnp.zeros_like(dai_ref)
        dds_ref[...] = jnp.zeros_like(dds_ref)

        def first(ci, carry):
            rows = pl.ds(pl.multiple_of(ci * ch, ch), ch)
            dy = dhg_ref[rows, :] * _gelu_grad(y_ref[rows, :])
            dyb = dy.astype(BF16)
            dds_ref[...] += jnp.sum(dy * u_ref[rows, :], axis=0, keepdims=True)
            duf[rows, :] = ds_ref[...] * dy
            gr[rows, :] = _dot(dyb, wcr_ref[...], NT)
            gi[rows, :] = -_dot(dyb, wci_ref[...], NT)
            dwcr_ref[...] += _dot(xr_ref[rows, :], dyb, TN)
            dwci_ref[...] -= _dot(xi_ref[rows, :], dyb, TN)
            return carry

        lax.fori_loop(0, nch, first, 0)
        _power_table(pr_s, pi_s, zr_ref[...], -zi_ref[...], L, reverse=True)
        _scan(gr, gi, pr_s, pi_s, L, reverse=True)

        def lam_grad(gxr, gxi, xpr, xpi):
            pr, pi = _cmul_conj(xpr, xpi, gxr, gxi)
            dar_ref[...] += jnp.sum(pr, axis=0, keepdims=True)
            dai_ref[...] += jnp.sum(pi, axis=0, keepdims=True)

        def second(ci, carry):
            r0 = pl.multiple_of(ci * ch, ch)
            rows = pl.ds(r0, ch)
            gxr, gxi = gr[rows, :], gi[rows, :]
            gxrb, gxib = gxr.astype(BF16), gxi.astype(BF16)
            du_ref[rows, :] = (duf[rows, :] + _dot(gxrb, wbr_ref[...], NT) + _dot(gxib, wbi_ref[...], NT)).astype(BF16)
            ub = u_ref[rows, :].astype(BF16)
            dwbr_ref[...] += _dot(ub, gxrb, TN)
            dwbi_ref[...] += _dot(ub, gxib, TN)
            return carry

        lax.fori_loop(0, nch, second, 0)

        ridx = lax.broadcasted_iota(jnp.int32, (ch, CHUNK_X), 0)
        xpr = jnp.where(ridx >= 1, pltpu.roll(xr_ref[0:ch, :].astype(F32), 1, 0), 0.0)
        xpi = jnp.where(ridx >= 1, pltpu.roll(xi_ref[0:ch, :].astype(F32), 1, 0), 0.0)
        lam_grad(gr[0:ch, :], gi[0:ch, :], xpr, xpi)

        def third(ci, carry):
            r0 = pl.multiple_of(ci * ch, ch)
            ext = pl.ds(pl.multiple_of(r0 - halo, halo), ch + halo)
            xpr = pltpu.roll(xr_ref[ext, :].astype(F32), 1, 0)[halo:]
            xpi = pltpu.roll(xi_ref[ext, :].astype(F32), 1, 0)[halo:]
            lam_grad(gr[pl.ds(r0, ch), :], gi[pl.ds(r0, ch), :], xpr, xpi)
            return carry

        if nch > 1:
            lax.fori_loop(1, nch, third, 0)

    ucol = pl.BlockSpec((L, CHUNK_U), lambda k: (0, k))
    xcol = pl.BlockSpec((L, CHUNK_X), lambda k: (0, k))
    wb_spec = pl.BlockSpec((None, CHUNK_U, CHUNK_X), lambda k: (k, 0, 0))
    wc_spec = pl.BlockSpec((None, CHUNK_X, CHUNK_U), lambda k: (k, 0, 0))
    avec = pl.BlockSpec((1, CHUNK_X), lambda k: (0, k))
    uvec = pl.BlockSpec((1, CHUNK_U), lambda k: (0, k))
    sd = jax.ShapeDtypeStruct
    return _call(
        body, name=name, grid=(NC,),
        in_specs=[ucol, ucol, pl.BlockSpec((L, CHUNK_U), lambda k: (0, uo + k)), xcol, xcol,
                  wb_spec, wb_spec, avec, avec, wc_spec, wc_spec, uvec],
        out_specs=[ucol, wc_spec, wc_spec, wb_spec, wb_spec, avec, avec, uvec],
        out_shape=[sd((L, DS), BF16), sd((NC, CHUNK_X, CHUNK_U), F32), sd((NC, CHUNK_X, CHUNK_U), F32),
                   sd((NC, CHUNK_U, CHUNK_X), F32), sd((NC, CHUNK_U, CHUNK_X), F32),
                   sd((1, 4 * DS), F32), sd((1, 4 * DS), F32), sd((1, DS), F32)],
        scratch_shapes=[pltpu.VMEM((L, CHUNK_X), F32), pltpu.VMEM((L, CHUNK_X), F32), pltpu.VMEM((L, CHUNK_U), F32),
                        pltpu.VMEM((_scan_rows(L), CHUNK_X), F32), pltpu.VMEM((_scan_rows(L), CHUNK_X), F32)],
        vmem=VMEM_BIG, args=(dhg, ypre, proj, xr, xi, wbr, wbi, zr, zi, wcr, wci, dskip),
        semantics=("parallel",), copies=copies)


def _block_diag(w, transpose):
    G = w.shape[0]
    nc = G // GROUPS_PER_CHUNK
    w4 = w.reshape(nc, GROUPS_PER_CHUNK, SSM_GROUP, SSM_STATE)
    eye = jnp.eye(GROUPS_PER_CHUNK, dtype=w.dtype)
    if transpose:
        return (w4[:, None, :, :, :].transpose(0, 1, 4, 2, 3) * eye[None, :, None, :, None]).reshape(
            nc, CHUNK_X, CHUNK_U).astype(BF16)
    return (w4[:, :, :, None, :] * eye[None, :, None, :, None]).reshape(nc, CHUNK_U, CHUNK_X).astype(BF16)


def _diag_blocks(dw, transpose):
    nc = dw.shape[0]
    gpc = GROUPS_PER_CHUNK
    eye = jnp.eye(gpc, dtype=dw.dtype)
    if transpose:
        d5 = dw.reshape(nc, gpc, SSM_STATE, gpc, SSM_GROUP)
        kept = jnp.sum(d5 * eye[None, :, None, :, None], axis=1)
        return kept.transpose(0, 2, 3, 1).reshape(nc * gpc, SSM_GROUP, SSM_STATE)
    d5 = dw.reshape(nc, gpc, SSM_GROUP, gpc, SSM_STATE)
    kept = jnp.sum(d5 * eye[None, :, None, :, None], axis=3)
    return kept.reshape(nc * gpc, SSM_GROUP, SSM_STATE)


SHARD_BLOCK_ELEMS = 128 * 1024


def _shard_rows(R, C, scale):
    return _pick(R, max(8, scale * SHARD_BLOCK_ELEMS // C))


def cast_bf16(name, w, layer, parts=1):
    shape = w.shape[1:]
    w3 = w.reshape(w.shape[0], -1, shape[-1])
    _, R, C = w3.shape
    tr = _shard_rows(R, C, 4)
    cw = C // parts

    def body(w_ref, *o_refs):
        for g, o_ref in enumerate(o_refs):
            o_ref[...] = w_ref[:, g * cw:(g + 1) * cw].astype(BF16)

    out = pl.pallas_call(body, name=name, grid=(R // tr,),
                         in_specs=[pl.BlockSpec((None, tr, C), lambda i: (layer, i, 0))],
                         out_specs=[pl.BlockSpec((tr, cw), lambda i: (i, 0))] * parts,
                         out_shape=[jax.ShapeDtypeStruct((R, cw), BF16)] * parts,
                         compiler_params=_cparams(("parallel",), VMEM_MID))(w3)
    return [o.reshape(shape[:-1] + (cw,)) for o in out]


def _adamw(w, g, m, v):
    m = ADAM_B1 * m + (1.0 - ADAM_B1) * g
    v = ADAM_B2 * v + (1.0 - ADAM_B2) * (g * g)
    delta = -ADAM_LR * ((m * ADAM_C1) / (jnp.sqrt(v * ADAM_C2) + ADAM_EPS) + ADAM_WD * w)
    return delta, m, v


def _own_core(g4):
    return (lambda p: p[0]) if g4.shape[1] == 2 else (lambda p: 0)


def chip_partial(name, pos, g4, recv_a):
    _, _, R, C = g4.shape
    tr = _shard_rows(R, C, 4)
    core = _own_core(g4)

    def body(pos_ref, g_ref, a_ref, o_ref):
        o_ref[...] = (g_ref[...] + a_ref[...]).astype(BF16)

    return pl.pallas_call(
        body, name=name,
        grid_spec=pltpu.PrefetchScalarGridSpec(
            num_scalar_prefetch=1, grid=(4, R // tr),
            in_specs=[pl.BlockSpec((None, None, tr, C), lambda q, i, p: (q, core(p), i, 0)),
                      pl.BlockSpec((None, tr, C), lambda q, i, p: (q, i, 0))],
            out_specs=pl.BlockSpec((None, tr, C), lambda q, i, p: (q, i, 0))),
        out_shape=jax.ShapeDtypeStruct((4, R, C), BF16),
        compiler_params=_cparams(("parallel", "parallel"), VMEM_MID))(pos, g4, recv_a)


def adamw_shard(name, pos, layer, g4, recv_a, recv_b, w, m, v, prev):
    _, _, R, C = g4.shape
    tr = _shard_rows(R, C, 1)
    n_prev = 0 if prev is None else 4
    core = _own_core(g4)

    def body(pos_ref, g_ref, a_ref, b_ref, w_ref, m_ref, v_ref, *rest):
        go_ref, d_ref, mo_ref, vo_ref = rest[n_prev:]
        gs = g_ref[...] + a_ref[...]
        for j in range(3):
            gs = gs + b_ref[j].astype(F32)
        delta, mn, vn = _adamw(w_ref[...], gs, m_ref[...], v_ref[...])
        go_ref[...] = gs
        d_ref[...] = delta
        mo_ref[...] = mn
        vo_ref[...] = vn

    lay = pl.BlockSpec((None, tr, C), lambda i, p: (layer, i, 0))
    in_specs = [pl.BlockSpec((None, None, tr, C), lambda i, p: (p[1], core(p), i, 0)),
                pl.BlockSpec((None, tr, C), lambda i, p: (p[1], i, 0)),
                pl.BlockSpec((3, tr, C), lambda i, p: (0, i, 0)), lay, lay, lay]
    args = [g4, recv_a, recv_b, w, m, v]
    aliases = {}
    if prev is not None:
        in_specs += [pl.BlockSpec(memory_space=pl.ANY)] * 4
        args += list(prev)
        aliases = {7 + j: j for j in range(4)}
    return pl.pallas_call(
        body, name=name,
        grid_spec=pltpu.PrefetchScalarGridSpec(
            num_scalar_prefetch=1, grid=(R // tr,), in_specs=in_specs, out_specs=[lay] * 4),
        out_shape=[jax.ShapeDtypeStruct(w.shape, F32)] * 4,
        input_output_aliases=aliases,
        compiler_params=_cparams(("parallel",), VMEM_MID))(pos, *args)


def adamw_small(name, gathered, w, m, v):
    _, R, C = gathered.shape
    tr = _pick(R, 512)

    def body(g_ref, w_ref, m_ref, v_ref, go_ref, d_ref, mo_ref, vo_ref):
        gs = g_ref[0]
        for j in range(1, N_DEV):
            gs = gs + g_ref[j]
        delta, mn, vn = _adamw(w_ref[...], gs, m_ref[...], v_ref[...])
        go_ref[...] = gs
        d_ref[...] = delta
        mo_ref[...] = mn
        vo_ref[...] = vn

    spec = pl.BlockSpec((tr, C), lambda i: (i, 0))
    return pl.pallas_call(
        body, name=name, grid=(R // tr,),
        in_specs=[pl.BlockSpec((N_DEV, tr, C), lambda i: (0, i, 0)), spec, spec, spec], out_specs=[spec] * 4,
        out_shape=[jax.ShapeDtypeStruct((R, C), F32)] * 4,
        compiler_params=_cparams(("parallel",), VMEM_MID))(gathered, w, m, v)


def _position():
    return lax.axis_index("x"), lax.axis_index("y"), lax.axis_index("c")


FORWARD_AT = 0.88


def gather_copies(shards):
    n = len(shards)

    def parts(ins, outs, sems):
        send_sems, recv_sems, local_sems = sems
        x, y, c = _position()
        me, sibling = (x, y, c), (x, y, 1 - c)
        chips = [(1 - x, y), (x, 1 - y), (1 - x, 1 - y)]

        def copy(a, k, block, to, src=None):
            blk = outs[a].at[4 * block[0] + 2 * block[1] + block[2]]
            return pltpu.make_async_remote_copy(
                src_ref=blk if src is None else src, dst_ref=blk,
                send_sem=send_sems.at[a, k], recv_sem=recv_sems.at[a, k], device_id=to, device_id_type=MESH)

        mine = [pltpu.make_async_copy(ins[a], outs[a].at[4 * x + 2 * y + c], local_sems.at[a]) for a in range(n)]
        first = [[copy(a, 0, me, sibling, src=ins[a])] +
                 [copy(a, 1 + j, me, (*chip, c), src=ins[a]) for j, chip in enumerate(chips)] for a in range(n)]
        landed = [[copy(a, 1 + j, (*chip, c), me) for j, chip in enumerate(chips)] for a in range(n)]
        passed = [[copy(a, 4 + j, (*chip, c), sibling) for j, chip in enumerate(chips)] for a in range(n)]
        from_sibling = [[copy(a, 0, sibling, me)] +
                        [copy(a, 4 + j, (*chip, 1 - c), me) for j, chip in enumerate(chips)] for a in range(n)]
        return mine, first, landed, passed, from_sibling

    def start(ins, outs, sems):
        mine, first, _, _, _ = parts(ins, outs, sems)
        for a in range(n):
            mine[a].start()
            for cp in first[a]:
                cp.start()

    def forward(ins, outs, sems):
        _, _, landed, passed, _ = parts(ins, outs, sems)
        for a in range(n):
            for j in range(3):
                landed[a][j].wait_recv()
                passed[a][j].start()

    def finish(ins, outs, sems):
        mine, first, _, passed, from_sibling = parts(ins, outs, sems)
        for a in range(n):
            for cp in from_sibling[a]:
                cp.wait_recv()
        for a in range(n):
            for cp in first[a] + passed[a]:
                cp.wait_send()
            mine[a].wait()

    return HostedCopies(
        list(shards), [jax.ShapeDtypeStruct((N_DEV,) + s.shape, s.dtype) for s in shards],
        [pltpu.SemaphoreType.DMA((n, 7)), pltpu.SemaphoreType.DMA((n, 7)), pltpu.SemaphoreType.DMA((n,))],
        [(0.0, start), (FORWARD_AT, forward), (1.0, finish)])


def _exchange_copies(arrays, out_lead, make):
    n = len(arrays)

    def all_copies(ins, outs, sems):
        send_sems, recv_sems = sems
        return [make(ins[a], outs[a], send_sems.at[a, k], recv_sems.at[a, k], k)
                for a in range(n) for k in range(out_lead)]

    def start(ins, outs, sems):
        for cp in all_copies(ins, outs, sems):
            cp.start()

    def finish(ins, outs, sems):
        for cp in all_copies(ins, outs, sems):
            cp.wait()

    return HostedCopies(
        list(arrays), [jax.ShapeDtypeStruct((out_lead,) + a.shape[2:], a.dtype) for a in arrays],
        [pltpu.SemaphoreType.DMA((n, out_lead)), pltpu.SemaphoreType.DMA((n, out_lead))],
        [(0.0, start), (1.0, finish)])


def sibling_copies(grads):
    def make(src, dst, send_sem, recv_sem, q):
        x, y, c = _position()
        core = 1 - c if src.shape[1] == 2 else 0
        return pltpu.make_async_remote_copy(
            src_ref=src.at[q, core], dst_ref=dst.at[q], send_sem=send_sem, recv_sem=recv_sem,
            device_id=(x, y, 1 - c), device_id_type=MESH)

    return _exchange_copies(grads, 4, make)


def chip_copies(parts):
    def make(src, dst, send_sem, recv_sem, j):
        x, y, c = _position()
        chip = [(1 - x, y), (x, 1 - y), (1 - x, 1 - y)][j]
        return pltpu.make_async_remote_copy(
            src_ref=src.at[2 * chip[0] + chip[1], 0], dst_ref=dst.at[j], send_sem=send_sem, recv_sem=recv_sem,
            device_id=(*chip, c), device_id_type=MESH)

    return _exchange_copies(parts, 3, make)


class Carrier:
    def __init__(self):
        self.plan = {}
        self.counts = {}

    def ride(self, site, make, store):
        self.plan.setdefault(site, []).append((make, store))

    def make(self, site, ctx=None):
        if site not in self.plan:
            return None
        group = [make(ctx) for make, _ in self.plan[site]]
        self.counts[site] = [len(c.out_shape) for c in group]
        return merge_copies(group)

    def store(self, site, results):
        if site in self.plan:
            at = 0
            for (_, store), n in zip(self.plan[site], self.counts[site]):
                store(results[at:at + n])
                at += n

    def split(self, site, out):
        if site not in self.plan:
            return out
        self.store(site, out[1])
        return out[0]


def _pool_weight(gathered):
    PG = gathered.shape[-1]
    return gathered.transpose(1, 0, 2, 3).reshape(N_POOL_GROUPS, PG, PG)


def _layer_params(l, ln_g, pool_scale, lam_re, lam_im, log_dt, b_re, b_im, c_re, c_im,
                  d_skip, b_glu, branch_g):
    G, P = lam_re.shape[1:]
    p = dict(
        ln_g=ln_g[l][None, :], pool_scale=pool_scale[l][None, :], d_skip=d_skip[l][None, :],
        b_glu=b_glu[l][None, :], branch_g=branch_g[l][None, :],
        lr=lam_re[l].reshape(G, 1, P), li=lam_im[l].reshape(G, 1, P), ld=log_dt[l].reshape(G, 1, 1),
        br=b_re[l].transpose(0, 2, 1), bi=b_im[l].transpose(0, 2, 1), cr=c_re[l], ci=c_im[l])
    return p


def layer_fwd(l, x, p, gw, D, carrier, staged_parts=0):
    t = f"l{l}_"
    h = rms_fwd(t + "rms_fwd", x, p["ln_g"])
    if staged_parts:
        proj = None
        for g in range(staged_parts):
            site = (l, f"proj{g}")
            proj = carrier.split(site, mm_nn_part(t + f"proj{g}", h, gw("w_in", l, g), g, staged_parts,
                                                  into=proj, copies=carrier.make(site)))
    else:
        site = (l, "proj")
        proj = carrier.split(site, mm_nn_gathered(t + "proj", h, gw("w_in", l), copies=carrier.make(site)))
    wp = _pool_weight(gw("w_pool", l)[0])
    ypool = pool_fwd(t + "pool_fwd", proj, wp, p["pool_scale"])
    site = (l, "attn_fwd")
    yattn, landed = attn_fwd(t + "attn_fwd", proj, D, copies=carrier.make(site))
    carrier.store(site, landed)
    zr, zi, bbr, bbi = ssm_prep(t + "ssm_prep", p["lr"], p["li"], p["ld"], p["br"], p["bi"])
    ssm_w = dict(wbr=_block_diag(bbr, False), wbi=_block_diag(bbi, False),
                 zr=zr.reshape(1, -1), zi=zi.reshape(1, -1),
                 wcr=_block_diag(p["cr"], True), wci=_block_diag(p["ci"], True))
    site = (l, "ssm_fwd")
    (ypre, hg, xr, xi), landed = ssm_fwd(
        t + "ssm_fwd", proj, ssm_w["wbr"], ssm_w["wbi"], ssm_w["zr"], ssm_w["zi"],
        ssm_w["wcr"], ssm_w["wci"], p["d_skip"], D, copies=carrier.make(site))
    carrier.store(site, landed)
    glu_pre = mm_nn_gathered(t + "glu", hg, gw("w_glu", l))
    y = branch_fwd(t + "branch_fwd", ypool, yattn, glu_pre, proj, p["b_glu"], p["branch_g"])
    out = mm_plain(t + "out", y, gw("w_out", l)[0].reshape(D, D), NN, res=x)
    saved = dict(x=x, h=h, proj=proj, ypool=ypool, yattn=yattn, ypre=ypre, hg=hg, xr=xr, xi=xi,
                 glu_pre=glu_pre, y=y, ssm_w=ssm_w, wp=wp)
    return out, saved


def layer_bwd(l, dres, dres_b, s, p, gw, D, carrier, pos, split_w_in):
    t = f"l{l}_"
    proj = s["proj"]

    def by_target(g):
        return g.reshape(4, 2, -1, g.shape[-1])

    big = {}
    w_out_g = gw("w_out", l)[0].reshape(D, D)
    site = (l, "dy")
    dy = carrier.split(site, mm_plain(t + "dy", dres_b, w_out_g, NT, copies=carrier.make(site)))
    big["w_out"] = by_target(mm_plain(t + "dw_out", s["y"], dres_b, TN).reshape(N_DEV, D // N_DEV, D))
    dypool, dyattn, dglu, dpg, dag, dsg, dbg, dbglu = branch_bwd(
        t + "branch_bwd", dy, s["ypool"], s["yattn"], s["glu_pre"], proj, p["b_glu"], p["branch_g"])
    dhg = mm_nt_gathered(t + "dhg", dglu, gw("w_glu", l))
    big["w_glu"] = by_target(mm_tn_scattered(t + "dw_glu", s["hg"], dglu))
    w = s["ssm_w"]
    site = (l, "ssm_bwd")
    (du, dwcr, dwci, dwbr, dwbi, dar, dai, dds), landed = ssm_bwd(
        t + "ssm_bwd", dhg, s["ypre"], proj, s["xr"], s["xi"], w["wbr"], w["wbi"], w["zr"], w["zi"],
        w["wcr"], w["wci"], p["d_skip"], D, copies=carrier.make(site, big))
    carrier.store(site, landed)
    G, _, P = p["lr"].shape
    dlr, dli, dld, dbr, dbi = ssm_prep_bwd(
        t + "ssm_prep_bwd", p["lr"], p["li"], p["ld"], p["br"], p["bi"],
        dar.reshape(G, 1, P), dai.reshape(G, 1, P), _diag_blocks(dwbr, False), _diag_blocks(dwbi, False))
    site = (l, "attn_bwd")
    (dq, dk, dv), landed = attn_bwd(t + "attn_bwd", proj, s["yattn"], dyattn, D, copies=carrier.make(site, big))
    carrier.store(site, landed)
    dxp, dwp, dps = pool_bwd(t + "pool_bwd", dypool, proj, s["wp"], p["pool_scale"])
    dproj = jnp.concatenate([dxp, dpg, dq, dk, dv, dag, du, dsg], axis=1)
    PG = dwp.shape[1]
    big["w_pool"] = by_target(dwp.reshape(N_POOL_GROUPS, N_DEV, PG // N_DEV, PG).transpose(1, 0, 2, 3))
    if split_w_in:
        site = (l, "dw_in_a")
        to_sibling = carrier.split(site, mm_tn_half(t + "dw_in_a", s["h"], dproj, pos, False,
                                                    copies=carrier.make(site, big)))
        site = (l, "dw_in_b")
        mine = carrier.split(site, mm_tn_half(t + "dw_in_b", s["h"], dproj, pos, True,
                                              copies=carrier.make(site, dict(big, to_sibling=to_sibling[:, None]))))
        big["w_in"] = mine[:, None]
    else:
        big["w_in"] = by_target(mm_tn_scattered(t + "dw_in", s["h"], dproj))
    site = (l, "dh")
    dh = carrier.split(site, mm_nt_gathered(t + "dh", dproj, gw("w_in", l), copies=carrier.make(site, big)))
    dx, dx_b, dlng = rms_bwd(t + "rms_bwd", s["x"], dh, dres, p["ln_g"])
    small = dict(ln_g=dlng[0], pool_scale=dps[0], lam_re=dlr.reshape(G, P), lam_im=dli.reshape(G, P),
                 log_dt=dld.reshape(G), b_re=dbr.transpose(0, 2, 1), b_im=dbi.transpose(0, 2, 1),
                 c_re=_diag_blocks(dwcr, True), c_im=_diag_blocks(dwci, True),
                 d_skip=dds[0], b_glu=dbglu[0], branch_g=dbg[0])
    return dx, dx_b, small


SMALL_NAMES = ("ln_g", "pool_scale", "lam_re", "lam_im", "log_dt", "b_re", "b_im", "c_re", "c_im",
               "d_skip", "b_glu", "branch_g", "final_g")
BIG_NAMES = ("w_in", "w_pool", "w_glu", "w_out")
WEIGHT_ORDER = ("ln_g", "w_in", "w_pool", "pool_scale", "lam_re", "lam_im", "log_dt", "b_re", "b_im",
                "c_re", "c_im", "d_skip", "w_glu", "b_glu", "branch_g", "w_out", "final_g")


PACK_ROWS = 512


def _pack(arrs):
    flat = jnp.concatenate([a.reshape(-1) for a in arrs])
    pad = (-flat.shape[0]) % (PACK_ROWS * LANE)
    return jnp.pad(flat, (0, pad)).reshape(-1, LANE)


def _unpack(packed, like):
    flat = packed.reshape(-1)
    out, off = [], 0
    for a in like:
        out.append(flat[off:off + a.size].reshape(a.shape))
        off += a.size
    return out


def kernel(x, ln_g, w_in, w_pool, pool_scale, lam_re, lam_im, log_dt, b_re, b_im, c_re, c_im, d_skip, w_glu, b_glu, branch_g, w_out, final_g, loss_target, m_ln_g, m_w_in, m_w_pool, m_pool_scale, m_lam_re, m_lam_im, m_log_dt, m_b_re, m_b_im, m_c_re, m_c_im, m_d_skip, m_w_glu, m_b_glu, m_branch_g, m_w_out, m_final_g, v_ln_g, v_w_in, v_w_pool, v_pool_scale, v_lam_re, v_lam_im, v_log_dt, v_b_re, v_b_im, v_c_re, v_c_im, v_d_skip, v_w_glu, v_b_glu, v_branch_g, v_w_out, v_final_g):
    W = dict(ln_g=ln_g, w_in=w_in, w_pool=w_pool, pool_scale=pool_scale, lam_re=lam_re, lam_im=lam_im,
             log_dt=log_dt, b_re=b_re, b_im=b_im, c_re=c_re, c_im=c_im, d_skip=d_skip, w_glu=w_glu,
             b_glu=b_glu, branch_g=branch_g, w_out=w_out, final_g=final_g)
    Mo = dict(ln_g=m_ln_g, w_in=m_w_in, w_pool=m_w_pool, pool_scale=m_pool_scale, lam_re=m_lam_re,
              lam_im=m_lam_im, log_dt=m_log_dt, b_re=m_b_re, b_im=m_b_im, c_re=m_c_re, c_im=m_c_im,
              d_skip=m_d_skip, w_glu=m_w_glu, b_glu=m_b_glu, branch_g=m_branch_g, w_out=m_w_out,
              final_g=m_final_g)
    Vo = dict(ln_g=v_ln_g, w_in=v_w_in, w_pool=v_w_pool, pool_scale=v_pool_scale, lam_re=v_lam_re,
              lam_im=v_lam_im, log_dt=v_log_dt, b_re=v_b_re, b_im=v_b_im, c_re=v_c_re, c_im=v_c_im,
              d_skip=v_d_skip, w_glu=v_w_glu, b_glu=v_b_glu, branch_g=v_branch_g, w_out=v_w_out,
              final_g=v_final_g)
    depth = ln_g.shape[0]
    _, L, D = x.shape
    xc, yc, cc = _position()
    pos = jnp.stack([cc, 2 * xc + yc, 4 * xc + 2 * yc + cc]).astype(jnp.int32)

    def n_parts(n, l):
        return (W_IN_PARTS if l > 0 else W_IN0_PARTS) if n == "w_in" else 1

    shards, landed = {}, {}
    for n in BIG_NAMES:
        for l in range(depth):
            for g, part in enumerate(cast_bf16(f"cast_{n}_{l}", W[n], l, n_parts(n, l))):
                shards[n, l, g] = part
    carrier = Carrier()

    def gw(n, l, g=None):
        return landed[n, l, g] if g is not None else [landed[n, l, i] for i in range(n_parts(n, l))]

    def gather_plan(keys):
        return (lambda ctx: gather_copies([shards[k] for k in keys])), (lambda outs: landed.update(zip(keys, outs)))

    first = [("w_in", 0, 0)] + [("w_pool", l, 0) for l in range(depth)]
    landed.update(zip(first, copies_call("gather_first", gather_copies([shards[k] for k in first]))))
    for g in range(1, W_IN0_PARTS):
        carrier.ride((0, f"proj{g - 1}"), *gather_plan([("w_in", 0, g)]))
    carrier.ride((0, f"proj{W_IN0_PARTS - 1}"), *gather_plan([("w_out", 0, 0), ("w_glu", 0, 0)]))
    for l in range(1, depth):
        for g, call in enumerate(("attn_fwd", "ssm_fwd")):
            carrier.ride((l - 1, call), *gather_plan([("w_in", l, g)]))
        carrier.ride((l, "proj"), *gather_plan([("w_out", l, 0), ("w_glu", l, 0)]))

    own, recv_a, recv_b = {}, {}, {}

    def sibling_plan(l, names, keep, pick):
        def make(ctx):
            own.update({(n, l): ctx[n] for n in keep})
            return sibling_copies(pick(ctx))
        return make, (lambda outs: recv_a.update(zip([(n, l) for n in names], outs)))

    def chip_plan(l, names):
        def make(ctx):
            parts = [chip_partial(f"chip_partial_{n}_{l}", pos, own[n, l], recv_a[n, l])[:, None] for n in names]
            return chip_copies(parts)
        return make, (lambda outs: recv_b.update(zip([(n, l) for n in names], outs)))

    early, late = ("w_out", "w_glu"), ("w_in", "w_pool")
    for l in range(1, depth):
        carrier.ride((l, "dh"), *sibling_plan(l, BIG_NAMES, BIG_NAMES, lambda big: [big[n] for n in BIG_NAMES]))
        carrier.ride((l - 1, "ssm_bwd"), *chip_plan(l, ("w_out", "w_glu", "w_pool")))
        carrier.ride((l - 1, "attn_bwd"), *chip_plan(l, ("w_in",)))
    carrier.ride((0, "ssm_bwd"), *sibling_plan(0, early, early, lambda big: [big[n] for n in early]))
    carrier.ride((0, "dw_in_a"), *chip_plan(0, early))
    carrier.ride((0, "dw_in_b"), *sibling_plan(0, late, ("w_pool",), lambda ctx: [ctx["to_sibling"], ctx["w_pool"]]))

    def last_chip_make(big):
        own["w_in", 0] = big["w_in"]
        return chip_plan(0, late)[0](big)

    carrier.ride((0, "dh"), last_chip_make, chip_plan(0, late)[1])

    params = [_layer_params(l, ln_g, pool_scale, lam_re, lam_im, log_dt, b_re, b_im, c_re, c_im,
                            d_skip, b_glu, branch_g) for l in range(depth)]
    h = x[0]
    saved = []
    for l in range(depth):
        h, s = layer_fwd(l, h, params[l], gw, D, carrier, staged_parts=W_IN0_PARTS if l == 0 else 0)
        saved.append(s)
    loss_part, dres, dres_b, dfinal = loss_head("loss_head", h, final_g[None, :], loss_target[0])
    loss = lax.psum(loss_part[0, 0], ("x", "y", "c"))

    small = [None] * depth
    for l in reversed(range(depth)):
        dres, dres_b, small[l] = layer_bwd(l, dres, dres_b, saved[l], params[l], gw, D, carrier, pos,
                                           split_w_in=(l == 0))
    grad_x = dres[None]

    results = {}
    for n in BIG_NAMES:
        shape = W[n].shape
        R, C = int(math.prod(shape[1:-1])), shape[-1]
        w3, m3, v3 = (t.reshape(depth, R, C) for t in (W[n], Mo[n], Vo[n]))
        prev = None
        for l in range(depth):
            prev = adamw_shard(f"adamw_{n}_{l}", pos, l, own[n, l], recv_a[n, l], recv_b[n, l], w3, m3, v3, prev)
        results[n] = [t.reshape(shape) for t in prev]

    small_like = [W[n] for n in SMALL_NAMES]
    small_grads = [jnp.stack([small[l][n] for l in range(depth)]) for n in SMALL_NAMES[:-1]] + [dfinal[0]]
    gathered = copies_call("gather_small_grads", gather_copies([_pack(small_grads)]))[0]
    packed = adamw_small("adamw_small", gathered, _pack(small_like), _pack([Mo[n] for n in SMALL_NAMES]),
                         _pack([Vo[n] for n in SMALL_NAMES]))
    unpacked = [_unpack(t, small_like) for t in packed]
    for i, n in enumerate(SMALL_NAMES):
        results[n] = [unpacked[j][i] for j in range(4)]

    out = [loss, grad_x]
    for j in range(4):
        out += [results[n][j] for n in WEIGHT_ORDER]
    return tuple(out)
```

```python
import functools
import math

import jax
import jax.numpy as jnp
from jax import lax
from jax.experimental import pallas as pl
from jax.experimental.pallas import tpu as pltpu

F32 = jnp.float32
BF16 = jnp.bfloat16
MESH = pl.DeviceIdType.MESH

EPS = 1e-6
HEAD_DIM = 128
SSM_GROUP = 16
SSM_STATE = 64
GROUPS_PER_CHUNK = 8
CHUNK_U = GROUPS_PER_CHUNK * SSM_GROUP
CHUNK_X = GROUPS_PER_CHUNK * SSM_STATE
N_POOL_GROUPS = 4
POOL_HALO = 16
N_DEV = 8
LANE = 128
FULL_K = 4096
ATTN_TILE = 256
ATTN_DECAY_CUTOFF = 100.0
ROW_TILE = 128
VMEM_BIG = 58 * 1024 * 1024
VMEM_MID = 40 * 1024 * 1024

ADAM_LR = 0.001
ADAM_B1 = 0.9
ADAM_B2 = 0.999
ADAM_EPS = 1e-08
ADAM_WD = 0.01
ADAM_STEP = 10
ADAM_C1 = 1.0 / (1.0 - ADAM_B1 ** ADAM_STEP)
ADAM_C2 = 1.0 / (1.0 - ADAM_B2 ** ADAM_STEP)

NN = (((1,), (0,)), ((), ()))
NT = (((1,), (1,)), ((), ()))
TN = (((0,), (0,)), ((), ()))


def _pick(n, cap):
    if n <= cap:
        return n
    step = LANE if cap >= LANE else 8
    t = (cap // step) * step
    while t > step and n % t:
        t -= step
    assert n % t == 0, (n, cap)
    return t


def _cparams(sem, vmem=None):
    return pltpu.CompilerParams(dimension_semantics=sem, vmem_limit_bytes=vmem)


def _dot(a, b, dn=NN):
    return lax.dot_general(a, b, dn, preferred_element_type=F32)


def _sigmoid(x):
    e = jnp.exp(-jnp.abs(x))
    r = 1.0 / (1.0 + e)
    return jnp.where(x >= 0, r, e * r)


HBM = pl.BlockSpec(memory_space=pl.ANY)


class HostedCopies:
    def __init__(self, inputs, out_shape, scratch, phases):
        self.inputs, self.out_shape, self.scratch, self.phases = inputs, out_shape, scratch, phases

    def emit(self, ins, outs, sems, step, total):
        plan = {}
        for frac, fn in self.phases:
            plan.setdefault(min(total - 1, int(frac * total)), []).append(fn)
        for s in sorted(plan):
            def run(fns=plan[s]):
                for fn in fns:
                    fn(ins, outs, sems)
            if total == 1:
                run()
            else:
                pl.when(step == s)(run)


def copies_call(name, copies):
    n_i, n_o = len(copies.inputs), len(copies.out_shape)

    def body(*refs):
        copies.emit(refs[:n_i], refs[n_i:n_i + n_o], refs[n_i + n_o:], 0, 1)

    return pl.pallas_call(
        body, name=name, in_specs=[HBM] * n_i, out_specs=[HBM] * n_o, out_shape=copies.out_shape,
        scratch_shapes=copies.scratch, compiler_params=pltpu.CompilerParams(has_side_effects=True),
    )(*copies.inputs)


def _matmul(name, a, b, *, grid, a_spec, b_spec, o_spec, out_shape, dn,
            res=None, res_spec=None, pos=None, copies=None, product=None, into=None):
    ni, nj, nk = grid
    bs, b_specs = (list(b), list(b_spec)) if isinstance(b, (list, tuple)) else ([b], [b_spec])
    n_b = len(bs)
    n_pos = 0 if pos is None else 1
    n_res = 0 if res is None else 1
    n_into = 0 if into is None else 1
    n_ci = 0 if copies is None else len(copies.inputs)
    n_co = 0 if copies is None else len(copies.out_shape)

    def body(*refs):
        refs = refs[n_pos:]
        a_ref, b_refs = refs[0], refs[1:1 + n_b]
        r_ref = refs[1 + n_b] if n_res else None
        base = 1 + n_b + n_res + n_into
        cin = refs[base:base + n_ci]
        o_ref = refs[base + n_ci]
        cout = refs[base + n_ci + 1:base + n_ci + 1 + n_co]
        sems = refs[base + n_ci + 1 + n_co:]
        k = pl.program_id(2)
        if copies is not None:
            step = (pl.program_id(0) * nj + pl.program_id(1)) * nk + k
            copies.emit(cin, cout, sems, step, ni * nj * nk)

        if product is None:
            part = _dot(a_ref[...].astype(BF16), b_refs[0][...].astype(BF16), dn)
        else:
            part = product(a_ref, b_refs)
        if nk == 1:
            if r_ref is not None:
                part = part + r_ref[...]
            o_ref[...] = part.astype(o_ref.dtype)
        else:
            @pl.when(k == 0)
            def _():
                o_ref[...] = part if r_ref is None else part + r_ref[...]

            @pl.when(k > 0)
            def _():
                o_ref[...] += part

    assert nk == 1 or out_shape.dtype == F32
    in_specs = [a_spec] + b_specs + ([res_spec] if n_res else []) + [HBM] * (n_into + n_ci)
    args = (((pos,) if n_pos else ()) + (a, *bs) + ((res,) if n_res else ()) + ((into,) if n_into else ())
            + tuple(copies.inputs if copies else ()))
    aliases = {n_pos + 1 + n_b + n_res: 0} if n_into else {}
    out_specs = [o_spec] + [HBM] * n_co
    out_shapes = [out_shape] + list(copies.out_shape if copies else [])
    scratch = list(copies.scratch if copies else [])
    params = pltpu.CompilerParams(
        dimension_semantics=("arbitrary",) * 3 if copies else ("parallel", "parallel", "arbitrary"),
        vmem_limit_bytes=VMEM_BIG, has_side_effects=copies is not None)
    out = pl.pallas_call(
        body, name=name,
        grid_spec=pltpu.PrefetchScalarGridSpec(
            num_scalar_prefetch=n_pos, grid=grid, in_specs=in_specs, out_specs=out_specs, scratch_shapes=scratch),
        out_shape=out_shapes, input_output_aliases=aliases, compiler_params=params)(*args)
    return out[0] if copies is None else (out[0], list(out[1:]))


def mm_nn_gathered(name, a, parts, out_dtype=F32, copies=None):
    M, K = a.shape
    P, w = len(parts), parts[0].shape[2]
    nper = P * w
    tm, tk, tn = _pick(M, 1024), _pick(K, FULL_K), _pick(w, 768)
    r = w // tn
    per_part = N_DEV * r

    def b_spec(g):
        def index(i, j, k, *_):
            t = jnp.clip(j - g * per_part, 0, per_part - 1)
            return (t // r, k, t % r)
        return pl.BlockSpec((None, tk, tn), index)

    def o_index(i, j, k, *_):
        t = j % per_part
        return (i, (t // r) * (nper // tn) + (j // per_part) * r + t % r)

    def product(a_ref, b_refs):
        j = pl.program_id(1)
        b = b_refs[0][...]
        for g in range(1, P):
            b = jnp.where(j >= g * per_part, b_refs[g][...], b)
        return _dot(a_ref[...].astype(BF16), b.astype(BF16))

    return _matmul(
        name, a, list(parts), grid=(M // tm, P * per_part, K // tk),
        a_spec=pl.BlockSpec((tm, tk), lambda i, j, k, *_: (i, k)),
        b_spec=[b_spec(g) for g in range(P)], o_spec=pl.BlockSpec((tm, tn), o_index),
        out_shape=jax.ShapeDtypeStruct((M, N_DEV * nper), out_dtype), dn=NN, copies=copies,
        product=product if P > 1 else None)


def mm_nn_part(name, a, part, g, P, into=None, out_dtype=F32, copies=None):
    M, K = a.shape
    w = part.shape[2]
    tm, tk, tn = _pick(M, 1024), _pick(K, FULL_K), _pick(w, 768)
    r = w // tn
    return _matmul(
        name, a, part, grid=(M // tm, N_DEV * r, K // tk),
        a_spec=pl.BlockSpec((tm, tk), lambda i, j, k, *_: (i, k)),
        b_spec=pl.BlockSpec((None, tk, tn), lambda i, j, k, *_: (j // r, k, j % r)),
        o_spec=pl.BlockSpec((tm, tn), lambda i, j, k, *_: (i, (j // r) * (P * r) + g * r + j % r)),
        out_shape=jax.ShapeDtypeStruct((M, N_DEV * P * w), out_dtype), dn=NN, copies=copies, into=into)


NT_SLICES = 2
W_IN_PARTS = 2
W_IN0_PARTS = 3


def mm_nt_gathered(name, a, parts, out_dtype=F32, copies=None):
    M, _ = a.shape
    P, (_, N, w) = len(parts), parts[0].shape
    nper = P * w
    tm, tn = _pick(M, 1024), _pick(N, 1024)
    S = NT_SLICES

    def product(a_ref, b_refs):
        total = None
        for s in range(S):
            for g in range(P):
                off = (s * P + g) * w
                term = _dot(a_ref[:, off:off + w].astype(BF16), b_refs[g][s], NT)
                total = term if total is None else total + term
        return total

    return _matmul(
        name, a, list(parts), grid=(M // tm, N // tn, N_DEV // S),
        a_spec=pl.BlockSpec((tm, S * nper), lambda i, j, k, *_: (i, k)),
        b_spec=[pl.BlockSpec((S, tn, w), lambda i, j, k, *_: (k, j, 0)) for _ in range(P)],
        o_spec=pl.BlockSpec((tm, tn), lambda i, j, k, *_: (i, j)),
        out_shape=jax.ShapeDtypeStruct((M, N), out_dtype), dn=NT, copies=copies, product=product)


def mm_tn_scattered(name, a, b, copies=None):
    L, M = a.shape
    nper = b.shape[1] // N_DEV
    tm, tn, tk = _pick(M, 1024), _pick(nper, 768), _pick(L, FULL_K)
    r = nper // tn
    return _matmul(
        name, a, b, grid=(M // tm, N_DEV * r, L // tk),
        a_spec=pl.BlockSpec((tk, tm), lambda i, j, k, *_: (k, i)),
        b_spec=pl.BlockSpec((tk, tn), lambda i, j, k, *_: (k, j)),
        o_spec=pl.BlockSpec((None, tm, tn), lambda i, j, k, *_: (j // r, i, j % r)),
        out_shape=jax.ShapeDtypeStruct((N_DEV, M, nper), F32), dn=TN, copies=copies)


def mm_tn_half(name, a, b, pos, own, copies=None):
    L, M = a.shape
    nper = b.shape[1] // N_DEV
    tm, tn, tk = _pick(M, 1024), _pick(nper, 768), _pick(L, FULL_K)
    r = nper // tn

    def b_map(i, j, k, p):
        core = p[0] if own else 1 - p[0]
        return (k, (2 * (j // r) + core) * r + j % r)

    return _matmul(
        name, a, b, grid=(M // tm, 4 * r, L // tk),
        a_spec=pl.BlockSpec((tk, tm), lambda i, j, k, *_: (k, i)),
        b_spec=pl.BlockSpec((tk, tn), b_map),
        o_spec=pl.BlockSpec((None, tm, tn), lambda i, j, k, *_: (j // r, i, j % r)),
        out_shape=jax.ShapeDtypeStruct((4, M, nper), F32), dn=TN, pos=pos, copies=copies)


def mm_plain(name, a, b, dn, out_dtype=F32, res=None, copies=None):
    if dn == NN:
        (M, K), N = a.shape, b.shape[1]
    elif dn == NT:
        (M, K), N = a.shape, b.shape[0]
    else:
        (K, M), N = a.shape, b.shape[1]
    tm, tn, tk = _pick(M, 1024), _pick(N, 512), _pick(K, FULL_K)
    a_spec =(pl.BlockSpec((tk, tm), lambda i, j, k, *_: (k, i)) if dn == TN
              else pl.BlockSpec((tm, tk), lambda i, j, k, *_: (i, k)))
    b_spec = (pl.BlockSpec((tn, tk), lambda i, j, k, *_: (j, k)) if dn == NT
              else pl.BlockSpec((tk, tn), lambda i, j, k, *_: (k, j)))
    o_spec = pl.BlockSpec((tm, tn), lambda i, j, k, *_: (i, j))
    return _matmul(
        name, a, b, grid=(M // tm, N // tn, K // tk), a_spec=a_spec, b_spec=b_spec, o_spec=o_spec,
        out_shape=jax.ShapeDtypeStruct((M, N), out_dtype), dn=dn,
        res=res, res_spec=o_spec if res is not None else None, copies=copies)


def rms_fwd(name, x, g):
    L, D = x.shape
    tr = _pick(L, ROW_TILE)

    def body(x_ref, g_ref, h_ref):
        xv = x_ref[...]
        r = lax.rsqrt(jnp.mean(xv * xv, axis=-1, keepdims=True) + EPS)
        h_ref[...] = (xv * r * g_ref[...]).astype(BF16)

    return pl.pallas_call(
        body, name=name, grid=(L // tr,),
        in_specs=[pl.BlockSpec((tr, D), lambda i: (i, 0)), pl.BlockSpec((1, D), lambda i: (0, 0))],
        out_specs=pl.BlockSpec((tr, D), lambda i: (i, 0)),
        out_shape=jax.ShapeDtypeStruct((L, D), BF16),
        compiler_params=_cparams(("parallel",), VMEM_MID))(x, g)


def rms_bwd(name, x, dh, dres, g):
    L, D = x.shape
    tr = _pick(L, ROW_TILE)

    def body(x_ref, dh_ref, dr_ref, g_ref, dx_ref, dxb_ref, dg_ref):
        xv = x_ref[...]
        r = lax.rsqrt(jnp.mean(xv * xv, axis=-1, keepdims=True) + EPS)
        xh = xv * r
        dhv = dh_ref[...]
        dn = dhv * g_ref[...]
        dxv = dr_ref[...] + r * (dn - xh * jnp.mean(dn * xh, axis=-1, keepdims=True))
        dx_ref[...] = dxv
        dxb_ref[...] = dxv.astype(BF16)

        @pl.when(pl.program_id(0) == 0)
        def _():
            dg_ref[...] = jnp.zeros_like(dg_ref)

        dg_ref[...] += jnp.sum(dhv * xh, axis=0, keepdims=True)

    row = pl.BlockSpec((tr, D), lambda i: (i, 0))
    vec = pl.BlockSpec((1, D), lambda i: (0, 0))
    return pl.pallas_call(
        body, name=name, grid=(L // tr,), in_specs=[row, row, row, vec], out_specs=[row, row, vec],
        out_shape=[jax.ShapeDtypeStruct((L, D), F32), jax.ShapeDtypeStruct((L, D), BF16),
                   jax.ShapeDtypeStruct((1, D), F32)],
        compiler_params=_cparams(("arbitrary",), VMEM_MID))(x, dh, dres, g)


def loss_head(name, x, g, target):
    L, D = x.shape
    tr = _pick(L, ROW_TILE)

    def body(x_ref, g_ref, t_ref, loss_ref, dx_ref, dxb_ref, dg_ref):
        xv = x_ref[...]
        gv = g_ref[...]
        r = lax.rsqrt(jnp.mean(xv * xv, axis=-1, keepdims=True) + EPS)
        xh = xv * r
        err = xh * gv - t_ref[...]
        dy = err * (1.0 / D)
        dn = dy * gv
        dxv = r * (dn - xh * jnp.mean(dn * xh, axis=-1, keepdims=True))
        dx_ref[...] = dxv
        dxb_ref[...] = dxv.astype(BF16)

        @pl.when(pl.program_id(0) == 0)
        def _():
            dg_ref[...] = jnp.zeros_like(dg_ref)
            loss_ref[...] = jnp.zeros_like(loss_ref)

        dg_ref[...] += jnp.sum(dy * xh, axis=0, keepdims=True)
        row_loss = jnp.sum(err * err, axis=-1, keepdims=True) * (0.5 / D)
        loss_ref[...] += jnp.sum(row_loss, axis=0, keepdims=True)

    row = pl.BlockSpec((tr, D), lambda i: (i, 0))
    vec = pl.BlockSpec((1, D), lambda i: (0, 0))
    one = pl.BlockSpec((1, 1), lambda i: (0, 0))
    return pl.pallas_call(
        body, name=name, grid=(L // tr,), in_specs=[row, vec, row], out_specs=[one, row, row, vec],
        out_shape=[jax.ShapeDtypeStruct((1, 1), F32), jax.ShapeDtypeStruct((L, D), F32),
                   jax.ShapeDtypeStruct((L, D), BF16), jax.ShapeDtypeStruct((1, D), F32)],
        compiler_params=_cparams(("arbitrary",), VMEM_MID))(x, g, target)


def _branch_specs(D, tr):
    DP, DA, DS = D // 4, D // 2, D // 4
    return dict(
        pool=pl.BlockSpec((tr, DP), lambda i: (i, 0)),
        attn=pl.BlockSpec((tr, DA), lambda i: (i, 0)),
        glu=pl.BlockSpec((tr, 2 * DS), lambda i: (i, 0)),
        p_gate=pl.BlockSpec((tr, DP), lambda i: (i, 1)),
        a_gate=pl.BlockSpec((tr, DA), lambda i: (i, 4)),
        s_gate=pl.BlockSpec((tr, DS), lambda i: (i, 11)),
        bglu=pl.BlockSpec((1, 2 * DS), lambda i: (0, 0)),
        bg=pl.BlockSpec((1, D), lambda i: (0, 0)),
        row=pl.BlockSpec((tr, D), lambda i: (i, 0)),
    )


def branch_fwd(name, ypool, yattn, glu_pre, proj, b_glu, branch_g):
    L, DP = ypool.shape
    D = 4 * DP
    DA, DS = D // 2, D // 4
    tr = _pick(L, ROW_TILE)
    s = _branch_specs(D, tr)

    def body(yp_ref, ya_ref, gl_ref, pg_ref, ag_ref, sg_ref, bgl_ref, bg_ref, y_ref):
        pre = gl_ref[...] + bgl_ref[...]
        ys = pre[:, :DS] * _sigmoid(pre[:, DS:])
        bg = bg_ref[...]

        def one(raw, gate, g):
            gate = gate.astype(F32)
            r = lax.rsqrt(jnp.mean(raw * raw, axis=-1, keepdims=True) + EPS)
            return raw * r * g * (gate * _sigmoid(gate))

        y_ref[:, :DP] = one(yp_ref[...], pg_ref[...], bg[:, :DP]).astype(BF16)
        y_ref[:, DP:DP + DA] = one(ya_ref[...], ag_ref[...], bg[:, DP:DP + DA]).astype(BF16)
        y_ref[:, DP + DA:] = one(ys, sg_ref[...], bg[:, DP + DA:]).astype(BF16)

    return pl.pallas_call(
        body, name=name, grid=(L // tr,),
        in_specs=[s["pool"], s["attn"], s["glu"], s["p_gate"], s["a_gate"], s["s_gate"], s["bglu"], s["bg"]],
        out_specs=s["row"], out_shape=jax.ShapeDtypeStruct((L, D), BF16),
        compiler_params=_cparams(("parallel",), VMEM_MID))(ypool, yattn, glu_pre, proj, proj, proj, b_glu, branch_g)


def branch_bwd(name, dy, ypool, yattn, glu_pre, proj, b_glu, branch_g):
    L, DP = ypool.shape
    D = 4 * DP
    DA, DS = D // 2, D // 4
    tr = _pick(L, ROW_TILE // 2)
    s = _branch_specs(D, tr)

    def body(dy_ref, yp_ref, ya_ref, gl_ref, pg_ref, ag_ref, sg_ref, bgl_ref, bg_ref,
             dyp_ref, dya_ref, dgl_ref, dpg_ref, dag_ref, dsg_ref, dbg_ref, dbgl_ref):
        @pl.when(pl.program_id(0) == 0)
        def _():
            dbg_ref[...] = jnp.zeros_like(dbg_ref)
            dbgl_ref[...] = jnp.zeros_like(dbgl_ref)

        bg = bg_ref[...]

        def one(raw, gate, g, dyb):
            gate = gate.astype(F32)
            r = lax.rsqrt(jnp.mean(raw * raw, axis=-1, keepdims=True) + EPS)
            n = raw * r
            sg = _sigmoid(gate)
            sl = gate * sg
            dgate = dyb * n * g * (sg * (1.0 + gate * (1.0 - sg)))
            dbg = jnp.sum(dyb * n * sl, axis=0, keepdims=True)
            dn = dyb * g * sl
            draw = r * (dn - n * jnp.mean(dn * n, axis=-1, keepdims=True))
            return draw, dgate, dbg

        draw, dgate, dbg = one(yp_ref[...], pg_ref[...], bg[:, :DP], dy_ref[:, :DP])
        dyp_ref[...] = draw
        dpg_ref[...] = dgate.astype(BF16)
        dbg_ref[:, :DP] += dbg

        draw, dgate, dbg = one(ya_ref[...], ag_ref[...], bg[:, DP:DP + DA], dy_ref[:, DP:DP + DA])
        dya_ref[...] = draw
        dag_ref[...] = dgate.astype(BF16)
        dbg_ref[:, DP:DP + DA] += dbg

        pre = gl_ref[...] + bgl_ref[...]
        val = pre[:, :DS]
        sgt = _sigmoid(pre[:, DS:])
        draw, dgate, dbg = one(val * sgt, sg_ref[...], bg[:, DP + DA:], dy_ref[:, DP + DA:])
        dsg_ref[...] = dgate.astype(BF16)
        dbg_ref[:, DP + DA:] += dbg
        dval = draw * sgt
        dgt = draw * val * sgt * (1.0 - sgt)
        dgl_ref[:, :DS] = dval.astype(BF16)
        dgl_ref[:, DS:] = dgt.astype(BF16)
        dbgl_ref[:, :DS] += jnp.sum(dval, axis=0, keepdims=True)
        dbgl_ref[:, DS:] += jnp.sum(dgt, axis=0, keepdims=True)

    loc = lambda w: pl.BlockSpec((tr, w), lambda i: (i, 0))
    return pl.pallas_call(
        body, name=name, grid=(L // tr,),
        in_specs=[s["row"], s["pool"], s["attn"], s["glu"], s["p_gate"], s["a_gate"], s["s_gate"], s["bglu"], s["bg"]],
        out_specs=[loc(DP), loc(DA), loc(2 * DS), loc(DP), loc(DA), loc(DS), s["bg"], s["bglu"]],
        out_shape=[jax.ShapeDtypeStruct((L, DP), F32), jax.ShapeDtypeStruct((L, DA), F32),
                   jax.ShapeDtypeStruct((L, 2 * DS), BF16), jax.ShapeDtypeStruct((L, DP), BF16),
                   jax.ShapeDtypeStruct((L, DA), BF16), jax.ShapeDtypeStruct((L, DS), BF16),
                   jax.ShapeDtypeStruct((1, D), F32), jax.ShapeDtypeStruct((1, 2 * DS), F32)],
        compiler_params=_cparams(("arbitrary",), VMEM_BIG),
    )(dy, ypool, yattn, glu_pre, proj, proj, proj, b_glu, branch_g)


def _pool_select(g, s2, s4, s8, s16):
    return jnp.where(g == 0, s2, jnp.where(g == 1, s4, jnp.where(g == 2, s8, s16)))


def _pool_window(g):
    return jnp.where(g == 0, 2.0, jnp.where(g == 1, 4.0, jnp.where(g == 2, 8.0, 16.0))).astype(F32)


def _pooled_chunk(pad, g, r0, ch):
    xh = pad[pl.ds(r0, ch + POOL_HALO), :]
    s2 = xh + pltpu.roll(xh, 1, 0)
    s4 = s2 + pltpu.roll(s2, 2, 0)
    s8 = s4 + pltpu.roll(s4, 4, 0)
    s16 = s8 + pltpu.roll(s8, 8, 0)
    win = _pool_select(g, s2, s4, s8, s16)[POOL_HALO:]
    pos = (r0 + 1 + lax.broadcasted_iota(jnp.int32, (ch, 1), 0)).astype(F32)
    return win / jnp.minimum(pos, _pool_window(g)) - xh[POOL_HALO:]


def pool_fwd(name, proj, wp, scale):
    L = proj.shape[0]
    DP = scale.shape[1]
    PG = DP // N_POOL_GROUPS
    ch = _pick(L, 256)

    def body(x_ref, w_ref, s_ref, o_ref, pad):
        g = pl.program_id(0)
        pad[0:POOL_HALO, :] = jnp.zeros((POOL_HALO, PG), F32)
        pad[POOL_HALO:, :] = x_ref[...].astype(F32)

        def chunk(ci, carry):
            r0 = pl.multiple_of(ci * ch, ch)
            pooled = _pooled_chunk(pad, g, r0, ch)
            o_ref[pl.ds(r0, ch), :] = _dot(pooled.astype(BF16), w_ref[...]) * s_ref[...]
            return carry

        lax.fori_loop(0, L // ch, chunk, 0)

    return pl.pallas_call(
        body, name=name, grid=(N_POOL_GROUPS,),
        in_specs=[pl.BlockSpec((L, PG), lambda g: (0, g)), pl.BlockSpec((None, PG, PG), lambda g: (g, 0, 0)),
                  pl.BlockSpec((1, PG), lambda g: (0, g))],
        out_specs=pl.BlockSpec((L, PG), lambda g: (0, g)),
        out_shape=jax.ShapeDtypeStruct((L, DP), F32),
        scratch_shapes=[pltpu.VMEM((L + POOL_HALO, PG), F32)],
        compiler_params=_cparams(("parallel",), VMEM_MID))(proj, wp, scale)


def pool_bwd(name, dyraw, proj, wp, scale):
    L = proj.shape[0]
    DP = scale.shape[1]
    PG = DP // N_POOL_GROUPS
    ch = _pick(L, 256)

    def body(dy_ref, x_ref, w_ref, s_ref, dx_ref, dw_ref, ds_ref, pad, dpad, dpo):
        g = pl.program_id(0)
        pad[0:POOL_HALO, :] = jnp.zeros((POOL_HALO, PG), F32)
        pad[POOL_HALO:, :] = x_ref[...].astype(F32)
        dpad[L:, :] = jnp.zeros((POOL_HALO, PG), F32)
        dw_ref[...] = jnp.zeros_like(dw_ref)
        ds_ref[...] = jnp.zeros_like(ds_ref)
        wv = w_ref[...]
        win_f = _pool_window(g)

        def chunk(ci, carry):
            r0 = pl.multiple_of(ci * ch, ch)
            pooled = _pooled_chunk(pad, g, r0, ch).astype(BF16)
            dyv = dy_ref[pl.ds(r0, ch), :]
            ds_ref[...] += jnp.sum(dyv * _dot(pooled, wv), axis=0, keepdims=True)
            dmixed = (dyv * s_ref[...]).astype(BF16)
            dw_ref[...] += _dot(pooled, dmixed, TN)
            dpooled = _dot(dmixed, wv, NT)
            pos = (r0 + 1 + lax.broadcasted_iota(jnp.int32, (ch, 1), 0)).astype(F32)
            dpad[pl.ds(r0, ch), :] = dpooled / jnp.minimum(pos, win_f)
            dpo[pl.ds(r0, ch), :] = dpooled
            return carry

        lax.fori_loop(0, L // ch, chunk, 0)

        def chunk2(ci, carry):
            r0 = pl.multiple_of(ci * ch, ch)
            n = ch + POOL_HALO
            dm = dpad[pl.ds(r0, n), :]
            s2 = dm + pltpu.roll(dm, n - 1, 0)
            s4 = s2 + pltpu.roll(s2, n - 2, 0)
            s8 = s4 + pltpu.roll(s4, n - 4, 0)
            s16 = s8 + pltpu.roll(s8, n - 8, 0)
            win = _pool_select(g, s2, s4, s8, s16)[:ch]
            dx_ref[pl.ds(r0, ch), :] = (win - dpo[pl.ds(r0, ch), :]).astype(BF16)
            return carry

        lax.fori_loop(0, L // ch, chunk2, 0)

    col = pl.BlockSpec((L, PG), lambda g: (0, g))
    return pl.pallas_call(
        body, name=name, grid=(N_POOL_GROUPS,),
        in_specs=[col, col, pl.BlockSpec((None, PG, PG), lambda g: (g, 0, 0)), pl.BlockSpec((1, PG), lambda g: (0, g))],
        out_specs=[col, pl.BlockSpec((None, PG, PG), lambda g: (g, 0, 0)), pl.BlockSpec((1, PG), lambda g: (0, g))],
        out_shape=[jax.ShapeDtypeStruct((L, DP), BF16), jax.ShapeDtypeStruct((N_POOL_GROUPS, PG, PG), F32),
                   jax.ShapeDtypeStruct((1, DP), F32)],
        scratch_shapes=[pltpu.VMEM((L + POOL_HALO, PG), F32), pltpu.VMEM((L + POOL_HALO, PG), F32),
                        pltpu.VMEM((L, PG), F32)],
        compiler_params=_cparams(("parallel",), VMEM_MID))(dyraw, proj, wp, scale)


def _attn_tile(L):
    return _pick(L, ATTN_TILE)


def _tri(t, strict):
    j = lax.broadcasted_iota(jnp.int32, (t, t), 0)
    s = lax.broadcasted_iota(jnp.int32, (t, t), 1)
    return ((j > s) if strict else (j >= s)).astype(BF16)


def _attn_block(q, kt, rb, after, diagonal):
    tq, tk = q.shape[0], kt.shape[0]
    z = _dot(q, kt, NT)
    e = jnp.exp(-jnp.abs(z))
    l1p = jnp.log(1.0 + e)
    log_sig = jnp.minimum(z, 0.0) - l1p
    b = -jnp.maximum(z, 0.0) - l1p
    causal = None
    if diagonal:
        causal = lax.broadcasted_iota(jnp.int32, (tq, tk), 1) < lax.broadcasted_iota(jnp.int32, (tq, tk), 0)
        b = jnp.where(causal, b, 0.0)
    b_hi = b.astype(BF16)
    b_lo = (b - b_hi.astype(F32)).astype(BF16)
    suffix = _dot(b_hi, after) + _dot(b_lo, after) + rb
    w = jnp.exp(log_sig + suffix)
    if diagonal:
        w = jnp.where(causal, w, 0.0)
    return z, e, causal, b, w


def _attn_sweep(i, visit):
    go = visit(i, True)
    lax.while_loop(lambda c: jnp.logical_and(c[0] >= 0, c[1]),
                   lambda c: (c[0] - 1, visit(c[0], False)), (i - 1, go))


def attn_fwd(name, proj, D, copies=None):
    L = proj.shape[0]
    DA = D // 2
    H = DA // HEAD_DIM
    tq = tk = _attn_tile(L)
    qo, ko, vo = (D // 2) // HEAD_DIM, D // HEAD_DIM, (3 * D // 2) // HEAD_DIM
    scale = HEAD_DIM ** -0.5

    def body(q_ref, k_ref, v_ref, tri_ref, o_ref, kb_s, vb_s, acc, rb):
        i = pl.program_id(1)

        @pl.when(i == 0)
        def _():
            kb_s[...] = k_ref[...].astype(BF16)
            vb_s[...] = v_ref[...].astype(BF16)

        q = (q_ref[...].astype(F32) * scale).astype(BF16)
        acc[...] = jnp.zeros_like(acc)
        rb[...] = jnp.zeros_like(rb)

        def visit(kb, diagonal):
            k0 = pl.multiple_of(kb * tk, tk)
            kt = kb_s[pl.ds(k0, tk), :]
            vt = vb_s[pl.ds(k0, tk), :]
            _, _, _, b, w = _attn_block(q, kt, rb[...], tri_ref[...], diagonal)
            acc[...] += _dot(w.astype(BF16), vt)
            rbn = rb[...] + jnp.sum(b, axis=1, keepdims=True)
            rb[...] = rbn
            return jnp.max(rbn) > -ATTN_DECAY_CUTOFF

        _attn_sweep(i, visit)
        o_ref[...] = acc[...]

    (out,), landed = _call(
        body, name=name, grid=(H, L // tq),
        in_specs=[pl.BlockSpec((tq, HEAD_DIM), lambda h, i: (i, qo + h)),
                  pl.BlockSpec((L, HEAD_DIM), lambda h, i: (0, ko + h)),
                  pl.BlockSpec((L, HEAD_DIM), lambda h, i: (0, vo + h)),
                  pl.BlockSpec((tk, tk), lambda h, i: (0, 0))],
        out_specs=[pl.BlockSpec((tq, HEAD_DIM), lambda h, i: (i, h))],
        out_shape=[jax.ShapeDtypeStruct((L, DA), F32)],
        scratch_shapes=[pltpu.VMEM((L, HEAD_DIM), BF16), pltpu.VMEM((L, HEAD_DIM), BF16),
                        pltpu.VMEM((tq, HEAD_DIM), F32), pltpu.VMEM((tq, 1), F32)],
        vmem=VMEM_MID, args=(proj, proj, proj, _tri(tk, True)), semantics=("arbitrary", "arbitrary"),
        copies=copies)
    return out, landed


def attn_bwd(name, proj, o, do, D, copies=None):
    L = proj.shape[0]
    DA = D // 2
    H = DA // HEAD_DIM
    tq = tk = _attn_tile(L)
    qo, ko, vo = (D // 2) // HEAD_DIM, D // HEAD_DIM, (3 * D // 2) // HEAD_DIM
    scale = HEAD_DIM ** -0.5

    def body(q_ref, k_ref, v_ref, o_ref, do_ref, after_ref, from_ref, dq_ref, dk_ref, dv_ref,
             kb_s, vb_s, dk_s, dv_s, dq_acc, rb, rg):
        i = pl.program_id(1)
        nq = pl.num_programs(1)

        @pl.when(i == 0)
        def _():
            kb_s[...] = k_ref[...].astype(BF16)
            vb_s[...] = v_ref[...].astype(BF16)
            dk_s[...] = jnp.zeros_like(dk_s)
            dv_s[...] = jnp.zeros_like(dv_s)

        q = (q_ref[...].astype(F32) * scale).astype(BF16)
        dob = do_ref[...].astype(BF16)
        delta = jnp.sum(dob.astype(F32) * o_ref[...], axis=1, keepdims=True)
        dq_acc[...] = jnp.zeros_like(dq_acc)
        rb[...] = jnp.zeros_like(rb)
        rg[...] = jnp.zeros_like(rg)

        def visit(kb, diagonal):
            k0 = pl.multiple_of(kb * tk, tk)
            kt = kb_s[pl.ds(k0, tk), :]
            vt = vb_s[pl.ds(k0, tk), :]
            z, e, causal, b, w = _attn_block(q, kt, rb[...], after_ref[...], diagonal)
            wq = w.astype(BF16)
            dw = _dot(dob, vt, NT)
            g = wq.astype(F32) * dw
            g_hi = g.astype(BF16)
            g_lo = (g - g_hi.astype(F32)).astype(BF16)
            from_s = from_ref[...]
            suffix_g = _dot(g_hi, from_s) + _dot(g_lo, from_s) + rg[...]
            before = delta - suffix_g
            r = 1.0 / (1.0 + e)
            sig = jnp.where(z >= 0, r, e * r)
            sig_neg = jnp.where(z >= 0, e * r, r)
            dz = g * sig_neg - before * sig
            if diagonal:
                dz = jnp.where(causal, dz, 0.0)
            dz = dz.astype(BF16)
            dq_acc[...] += _dot(dz, kt)
            dk_s[pl.ds(k0, tk), :] += _dot(dz, q, TN)
            dv_s[pl.ds(k0, tk), :] += _dot(wq, dob, TN)
            rbn = rb[...] + jnp.sum(b, axis=1, keepdims=True)
            rb[...] = rbn
            rg[...] += jnp.sum(g, axis=1, keepdims=True)
            return jnp.max(rbn) > -ATTN_DECAY_CUTOFF

        _attn_sweep(i, visit)
        dq_ref[...] = (dq_acc[...] * scale).astype(BF16)

        @pl.when(i == nq - 1)
        def _():
            dk_ref[...] = dk_s[...].astype(BF16)
            dv_ref[...] = dv_s[...].astype(BF16)

    blk = pl.BlockSpec((tq, HEAD_DIM), lambda h, i: (i, h))
    full = pl.BlockSpec((L, HEAD_DIM), lambda h, i: (0, h))
    return _call(
        body, name=name, grid=(H, L // tq),
        in_specs=[pl.BlockSpec((tq, HEAD_DIM), lambda h, i: (i, qo + h)),
                  pl.BlockSpec((L, HEAD_DIM), lambda h, i: (0, ko + h)),
                  pl.BlockSpec((L, HEAD_DIM), lambda h, i: (0, vo + h)), blk, blk,
                  pl.BlockSpec((tk, tk), lambda h, i: (0, 0)), pl.BlockSpec((tk, tk), lambda h, i: (0, 0))],
        out_specs=[blk, full, full],
        out_shape=[jax.ShapeDtypeStruct((L, DA), BF16)] * 3,
        scratch_shapes=[pltpu.VMEM((L, HEAD_DIM), BF16), pltpu.VMEM((L, HEAD_DIM), BF16),
                        pltpu.VMEM((L, HEAD_DIM), F32), pltpu.VMEM((L, HEAD_DIM), F32),
                        pltpu.VMEM((tq, HEAD_DIM), F32), pltpu.VMEM((tq, 1), F32), pltpu.VMEM((tq, 1), F32)],
        vmem=VMEM_MID, args=(proj, proj, proj, o, do, _tri(tk, True), _tri(tk, False)),
        semantics=("arbitrary", "arbitrary"), copies=copies)


def _cmul(ar, ai, br, bi):
    return ar * br - ai * bi, ar * bi + ai * br


def _cmul_conj(ar, ai, br, bi):
    return ar * br + ai * bi, ar * bi - ai * br


def _ssm_disc(lr, li, ld):
    dt = jnp.exp(ld)
    m = jnp.exp(lr * dt)
    ar, ai = m * jnp.cos(li * dt), m * jnp.sin(li * dt)
    inv = 1.0 / (lr * lr + li * li)
    fr, fi = _cmul(ar - 1.0, ai, lr * inv, -li * inv)
    return dt, ar, ai, fr, fi, inv


def ssm_prep(name, lr, li, ld, br, bi):
    def body(lr_ref, li_ref, ld_ref, br_ref, bi_ref, zr_ref, zi_ref, bbr_ref, bbi_ref):
        dt, _, _, fr, fi, _ = _ssm_disc(lr_ref[...], li_ref[...], ld_ref[...])
        zr_ref[...] = lr_ref[...] * dt
        zi_ref[...] = li_ref[...] * dt
        bbr, bbi = _cmul(fr, fi, br_ref[...], bi_ref[...])
        bbr_ref[...] = bbr
        bbi_ref[...] = bbi

    sd = jax.ShapeDtypeStruct
    return pl.pallas_call(
        body, name=name,
        out_shape=[sd(lr.shape, F32), sd(lr.shape, F32), sd(br.shape, F32), sd(br.shape, F32)],
    )(lr, li, ld, br, bi)


def ssm_prep_bwd(name, lr, li, ld, br, bi, gar, gai, gbr, gbi):
    def body(lr_ref, li_ref, ld_ref, br_ref, bi_ref, gar_ref, gai_ref, gbr_ref, gbi_ref,
             dlr_ref, dli_ref, dld_ref, dbr_ref, dbi_ref):
        lr_, li_ = lr_ref[...], li_ref[...]
        dt, ar, ai, fr, fi, inv = _ssm_disc(lr_, li_, ld_ref[...])
        gbr_, gbi_ = gbr_ref[...], gbi_ref[...]
        dbr, dbi = _cmul_conj(fr, fi, gbr_, gbi_)
        dbr_ref[...] = dbr
        dbi_ref[...] = dbi
        pr, pi = _cmul_conj(br_ref[...], bi_ref[...], gbr_, gbi_)
        gfr = jnp.sum(pr, axis=1, keepdims=True)
        gfi = jnp.sum(pi, axis=1, keepdims=True)
        ilr, ili = lr_ * inv, -li_ * inv
        tr_, ti_ = _cmul_conj(ilr, ili, gfr, gfi)
        gatr, gati = gar_ref[...] + tr_, gai_ref[...] + ti_
        hr, hi = _cmul(fr, fi, ilr, ili)
        t1r, t1i = _cmul_conj(ar * dt, ai * dt, gatr, gati)
        t2r, t2i = _cmul_conj(hr, hi, gfr, gfi)
        dlr_ref[...] = t1r - t2r
        dli_ref[...] = t1i - t2i
        lar, lai = _cmul(lr_, li_, ar, ai)
        gdt, _ = _cmul_conj(lar, lai, gatr, gati)
        dld_ref[...] = jnp.sum(gdt, axis=2, keepdims=True) * dt

    sd = jax.ShapeDtypeStruct
    return pl.pallas_call(
        body, name=name,
        out_shape=[sd(lr.shape, F32), sd(lr.shape, F32), sd(ld.shape, F32), sd(br.shape, F32), sd(br.shape, F32)],
    )(lr, li, ld, br, bi, gar, gai, gbr, gbi)


SCAN_ROWS = 64


def _scan_rows(L):
    return min(SCAN_ROWS, L)


def _power_table(pr_s, pi_s, zr, zi, L, reverse):
    R = _scan_rows(L)
    row = lax.broadcasted_iota(jnp.int32, (R, 1), 0).astype(F32)
    dist = (R - row) if reverse else (row + 1.0)
    mag = jnp.exp(dist * zr)
    pr_s[...] = mag * jnp.cos(dist * zi)
    pi_s[...] = mag * jnp.sin(dist * zi)


def _scan(xr, xi, pr_s, pi_s, L, reverse):
    R = _scan_rows(L)
    nt = L // R
    assert L % R == 0 and R & (R - 1) == 0
    ns = CHUNK_X // LANE
    ridx = lax.broadcasted_iota(jnp.int32, (R, LANE), 0)

    def power(ref, d, cs):
        at = R - d if reverse else d - 1
        return ref[at:at + 1, cs]

    def shift(v, d):
        if d < 8:
            if reverse:
                return jnp.where(ridx < R - d, pltpu.roll(v, R - d, 0), 0.0)
            return jnp.where(ridx >= d, pltpu.roll(v, d, 0), 0.0)
        zeros = jnp.zeros((d, LANE), F32)
        return jnp.concatenate([v[d:], zeros], 0) if reverse else jnp.concatenate([zeros, v[:R - d]], 0)

    def tile(n, carry):
        t = nt - 1 - n if reverse else n
        rows = pl.ds(pl.multiple_of(t * R, R), R)
        edges = []
        for c in range(ns):
            cs = slice(c * LANE, (c + 1) * LANE)
            vr, vi = xr[rows, cs], xi[rows, cs]
            d = 1
            while d < R:
                ar, ai = power(pr_s, d, cs), power(pi_s, d, cs)
                sr, si = shift(vr, d), shift(vi, d)
                vr, vi = vr + ar * sr - ai * si, vi + ar * si + ai * sr
                d *= 2
            cr, ci = carry[2 * c], carry[2 * c + 1]
            pr, pi = pr_s[:, cs], pi_s[:, cs]
            vr, vi = vr + pr * cr - pi * ci, vi + pr * ci + pi * cr
            xr[rows, cs] = vr
            xi[rows, cs] = vi
            edge = slice(0, 1) if reverse else slice(R - 1, R)
            edges += [vr[edge], vi[edge]]
        return tuple(edges)

    lax.fori_loop(0, nt, tile, tuple(jnp.zeros((1, LANE), F32) for _ in range(2 * ns)))


def _gelu(x):
    t = jnp.tanh(0.7978845608028654 * (x + 0.044715 * x * x * x))
    return 0.5 * x * (1.0 + t)


def _gelu_grad(x):
    t = jnp.tanh(0.7978845608028654 * (x + 0.044715 * x * x * x))
    return 0.5 * (1.0 + t) + 0.5 * x * (1.0 - t * t) * 0.7978845608028654 * (1.0 + 0.134145 * x * x)


def _call(body, *, name, grid, in_specs, out_specs, out_shape, scratch_shapes, vmem, args, semantics,
          copies=None):
    n_i, n_o, n_s = len(in_specs), len(out_specs), len(scratch_shapes)
    if copies is None:
        out = pl.pallas_call(
            body, name=name, grid=grid, in_specs=in_specs, out_specs=out_specs, out_shape=out_shape,
            scratch_shapes=scratch_shapes, compiler_params=_cparams(semantics, vmem))(*args)
        return list(out), []
    n_ci, n_co = len(copies.inputs), len(copies.out_shape)

    def hosted(*refs):
        ins, cin = refs[:n_i], refs[n_i:n_i + n_ci]
        outs = refs[n_i + n_ci:n_i + n_ci + n_o]
        cout = refs[n_i + n_ci + n_o:n_i + n_ci + n_o + n_co]
        scr = refs[n_i + n_ci + n_o + n_co:n_i + n_ci + n_o + n_co + n_s]
        sems = refs[n_i + n_ci + n_o + n_co + n_s:]
        step = pl.program_id(0)
        for axis in range(1, len(grid)):
            step = step * grid[axis] + pl.program_id(axis)
        copies.emit(cin, cout, sems, step, math.prod(grid))
        body(*ins, *outs, *scr)

    out = pl.pallas_call(
        hosted, name=name, grid=grid, in_specs=list(in_specs) + [HBM] * n_ci,
        out_specs=list(out_specs) + [HBM] * n_co, out_shape=list(out_shape) + list(copies.out_shape),
        scratch_shapes=list(scratch_shapes) + list(copies.scratch),
        compiler_params=pltpu.CompilerParams(dimension_semantics=("arbitrary",) * len(grid),
                                             vmem_limit_bytes=vmem, has_side_effects=True))(*args, *copies.inputs)
    return list(out[:n_o]), list(out[n_o:])


def merge_copies(group):
    group = [c for c in group if c is not None]
    if len(group) <= 1:
        return group[0] if group else None
    bounds, i0, o0, s0 = [], 0, 0, 0
    for c in group:
        bounds.append((i0, o0, s0))
        i0, o0, s0 = i0 + len(c.inputs), o0 + len(c.out_shape), s0 + len(c.scratch)
    phases = []
    for c, (i, o, s) in zip(group, bounds):
        for frac, fn in c.phases:
            def shifted(ins, outs, sems, fn=fn, c=c, i=i, o=o, s=s):
                fn(ins[i:i + len(c.inputs)], outs[o:o + len(c.out_shape)], sems[s:s + len(c.scratch)])
            phases.append((frac, shifted))
    return HostedCopies([a for c in group for a in c.inputs], [a for c in group for a in c.out_shape],
                        [a for c in group for a in c.scratch], phases)


def ssm_fwd(name, proj, wbr, wbi, zr, zi, wcr, wci, dskip, D, copies=None):
    L = proj.shape[0]
    DS = D // 4
    NC = DS // CHUNK_U
    uo = (5 * D // 2) // CHUNK_U
    ch = _pick(L, 256)

    def body(u_ref, wbr_ref, wbi_ref, zr_ref, zi_ref, wcr_ref, wci_ref, ds_ref,
             y_ref, hg_ref, xr_ref, xi_ref, sr, si, pr_s, pi_s):
        def fill(ci, carry):
            rows = pl.ds(pl.multiple_of(ci * ch, ch), ch)
            ub = u_ref[rows, :].astype(BF16)
            sr[rows, :] = _dot(ub, wbr_ref[...])
            si[rows, :] = _dot(ub, wbi_ref[...])
            return carry

        lax.fori_loop(0, L // ch, fill, 0)
        _power_table(pr_s, pi_s, zr_ref[...], zi_ref[...], L, reverse=False)
        _scan(sr, si, pr_s, pi_s, L, reverse=False)

        def emit(ci, carry):
            rows = pl.ds(pl.multiple_of(ci * ch, ch), ch)
            xrb, xib = sr[rows, :].astype(BF16), si[rows, :].astype(BF16)
            xr_ref[rows, :] = xrb
            xi_ref[rows, :] = xib
            y = _dot(xrb, wcr_ref[...]) - _dot(xib, wci_ref[...]) + ds_ref[...] * u_ref[rows, :].astype(F32)
            y_ref[rows, :] = y
            hg_ref[rows, :] = _gelu(y).astype(BF16)
            return carry

        lax.fori_loop(0, L // ch, emit, 0)

    ucol = pl.BlockSpec((L, CHUNK_U), lambda k: (0, k))
    xcol = pl.BlockSpec((L, CHUNK_X), lambda k: (0, k))
    sd = jax.ShapeDtypeStruct
    return _call(
        body, name=name, grid=(NC,),
        in_specs=[pl.BlockSpec((L, CHUNK_U), lambda k: (0, uo + k)),
                  pl.BlockSpec((None, CHUNK_U, CHUNK_X), lambda k: (k, 0, 0)),
                  pl.BlockSpec((None, CHUNK_U, CHUNK_X), lambda k: (k, 0, 0)),
                  pl.BlockSpec((1, CHUNK_X), lambda k: (0, k)), pl.BlockSpec((1, CHUNK_X), lambda k: (0, k)),
                  pl.BlockSpec((None, CHUNK_X, CHUNK_U), lambda k: (k, 0, 0)),
                  pl.BlockSpec((None, CHUNK_X, CHUNK_U), lambda k: (k, 0, 0)),
                  pl.BlockSpec((1, CHUNK_U), lambda k: (0, k))],
        out_specs=[ucol, ucol, xcol, xcol],
        out_shape=[sd((L, DS), F32), sd((L, DS), BF16), sd((L, 4 * DS), BF16), sd((L, 4 * DS), BF16)],
        scratch_shapes=[pltpu.VMEM((L, CHUNK_X), F32), pltpu.VMEM((L, CHUNK_X), F32),
                        pltpu.VMEM((_scan_rows(L), CHUNK_X), F32), pltpu.VMEM((_scan_rows(L), CHUNK_X), F32)],
        vmem=VMEM_BIG, args=(proj, wbr, wbi, zr, zi, wcr, wci, dskip), semantics=("parallel",), copies=copies)


def ssm_bwd(name, dhg, ypre, proj, xr, xi, wbr, wbi, zr, zi, wcr, wci, dskip, D, copies=None):
    L = proj.shape[0]
    DS = D // 4
    NC = DS // CHUNK_U
    uo = (5 * D // 2) // CHUNK_U
    ch = _pick(L, 256)
    nch = L // ch
    halo = 16

    def body(dhg_ref, y_ref, u_ref, xr_ref, xi_ref, wbr_ref, wbi_ref, zr_ref, zi_ref, wcr_ref, wci_ref,
             ds_ref, du_ref, dwcr_ref, dwci_ref, dwbr_ref, dwbi_ref, dar_ref, dai_ref, dds_ref,
             gr, gi, duf, pr_s, pi_s):
        dwcr_ref[...] = jnp.zeros_like(dwcr_ref)
        dwci_ref[...] = jnp.zeros_like(dwci_ref)
        dwbr_ref[...] = jnp.zeros_like(dwbr_ref)
        dwbi_ref[...] = jnp.zeros_like(dwbi_ref)
        dar_ref[...] = jnp.zeros_like(dar_ref)
        dai_ref[...] = jnp.zeros_like(dai_ref)
        dds_ref[...] = jnp.zeros_like(dds_ref)

        def first(ci, carry):
            rows = pl.ds(pl.multiple_of(ci * ch, ch), ch)
            dy = dhg_ref[rows, :] * _gelu_grad(y_ref[rows, :])
            dyb = dy.astype(BF16)
            dds_ref[...] += jnp.sum(dy * u_ref[rows, :].astype(F32), axis=0, keepdims=True)
            duf[rows, :] = ds_ref[...] * dy
            gr[rows, :] = _dot(dyb, wcr_ref[...], NT)
            gi[rows, :] = -_dot(dyb, wci_ref[...], NT)
            dwcr_ref[...] += _dot(xr_ref[rows, :], dyb, TN)
            dwci_ref[...] -= _dot(xi_ref[rows, :], dyb, TN)
            return carry

        lax.fori_loop(0, nch, first, 0)
        _power_table(pr_s, pi_s, zr_ref[...], -zi_ref[...], L, reverse=True)
        _scan(gr, gi, pr_s, pi_s, L, reverse=True)

        def lam_grad(gxr, gxi, xpr, xpi):
            pr, pi = _cmul_conj(xpr, xpi, gxr, gxi)
            dar_ref[...] += jnp.sum(pr, axis=0, keepdims=True)
            dai_ref[...] += jnp.sum(pi, axis=0, keepdims=True)

        def second(ci, carry):
            r0 = pl.multiple_of(ci * ch, ch)
            rows = pl.ds(r0, ch)
            gxr, gxi = gr[rows, :], gi[rows, :]
            gxrb, gxib = gxr.astype(BF16), gxi.astype(BF16)
            du_ref[rows, :] = (duf[rows, :] + _dot(gxrb, wbr_ref[...], NT) + _dot(gxib, wbi_ref[...], NT)).astype(BF16)
            ub = u_ref[rows, :].astype(BF16)
            dwbr_ref[...] += _dot(ub, gxrb, TN)
            dwbi_ref[...] += _dot(ub, gxib, TN)
            return carry

        lax.fori_loop(0, nch, second, 0)

        ridx = lax.broadcasted_iota(jnp.int32, (ch, CHUNK_X), 0)
        xpr = jnp.where(ridx >= 1, pltpu.roll(xr_ref[0:ch, :].astype(F32), 1, 0), 0.0)
        xpi = jnp.where(ridx >= 1, pltpu.roll(xi_ref[0:ch, :].astype(F32), 1, 0), 0.0)
        lam_grad(gr[0:ch, :], gi[0:ch, :], xpr, xpi)

        def third(ci, carry):
            r0 = pl.multiple_of(ci * ch, ch)
            ext = pl.ds(pl.multiple_of(r0 - halo, halo), ch + halo)
            xpr = pltpu.roll(xr_ref[ext, :].astype(F32), 1, 0)[halo:]
            xpi = pltpu.roll(xi_ref[ext, :].astype(F32), 1, 0)[halo:]
            lam_grad(gr[pl.ds(r0, ch), :], gi[pl.ds(r0, ch), :], xpr, xpi)
            return carry

        if nch > 1:
            lax.fori_loop(1, nch, third, 0)

    ucol = pl.BlockSpec((L, CHUNK_U), lambda k: (0, k))
    xcol = pl.BlockSpec((L, CHUNK_X), lambda k: (0, k))
    wb_spec = pl.BlockSpec((None, CHUNK_U, CHUNK_X), lambda k: (k, 0, 0))
    wc_spec = pl.BlockSpec((None, CHUNK_X, CHUNK_U), lambda k: (k, 0, 0))
    avec = pl.BlockSpec((1, CHUNK_X), lambda k: (0, k))
    uvec = pl.BlockSpec((1, CHUNK_U), lambda k: (0, k))
    sd = jax.ShapeDtypeStruct
    return _call(
        body, name=name, grid=(NC,),
        in_specs=[ucol, ucol, pl.BlockSpec((L, CHUNK_U), lambda k: (0, uo + k)), xcol, xcol,
                  wb_spec, wb_spec, avec, avec, wc_spec, wc_spec, uvec],
        out_specs=[ucol, wc_spec, wc_spec, wb_spec, wb_spec, avec, avec, uvec],
        out_shape=[sd((L, DS), BF16), sd((NC, CHUNK_X, CHUNK_U), F32), sd((NC, CHUNK_X, CHUNK_U), F32),
                   sd((NC, CHUNK_U, CHUNK_X), F32), sd((NC, CHUNK_U, CHUNK_X), F32),
                   sd((1, 4 * DS), F32), sd((1, 4 * DS), F32), sd((1, DS), F32)],
        scratch_shapes=[pltpu.VMEM((L, CHUNK_X), F32), pltpu.VMEM((L, CHUNK_X), F32), pltpu.VMEM((L, CHUNK_U), F32),
                        pltpu.VMEM((_scan_rows(L), CHUNK_X), F32), pltpu.VMEM((_scan_rows(L), CHUNK_X), F32)],
        vmem=VMEM_BIG, args=(dhg, ypre, proj, xr, xi, wbr, wbi, zr, zi, wcr, wci, dskip),
        semantics=("parallel",), copies=copies)


def _block_diag(w, transpose):
    G = w.shape[0]
    nc = G // GROUPS_PER_CHUNK
    w4 = w.reshape(nc, GROUPS_PER_CHUNK, SSM_GROUP, SSM_STATE)
    eye = jnp.eye(GROUPS_PER_CHUNK, dtype=w.dtype)
    if transpose:
        return (w4[:, None, :, :, :].transpose(0, 1, 4, 2, 3) * eye[None, :, None, :, None]).reshape(
            nc, CHUNK_X, CHUNK_U).astype(BF16)
    return (w4[:, :, :, None, :] * eye[None, :, None, :, None]).reshape(nc, CHUNK_U, CHUNK_X).astype(BF16)


def _diag_blocks(dw, transpose):
    nc = dw.shape[0]
    gpc = GROUPS_PER_CHUNK
    eye = jnp.eye(gpc, dtype=dw.dtype)
    if transpose:
        d5 = dw.reshape(nc, gpc, SSM_STATE, gpc, SSM_GROUP)
        kept = jnp.sum(d5 * eye[None, :, None, :, None], axis=1)
        return kept.transpose(0, 2, 3, 1).reshape(nc * gpc, SSM_GROUP, SSM_STATE)
    d5 = dw.reshape(nc, gpc, SSM_GROUP, gpc, SSM_STATE)
    kept = jnp.sum(d5 * eye[None, :, None, :, None], axis=3)
    return kept.reshape(nc * gpc, SSM_GROUP, SSM_STATE)


SHARD_BLOCK_ELEMS = 128 * 1024


def _shard_rows(R, C, scale):
    return _pick(R, max(8, scale * SHARD_BLOCK_ELEMS // C))


def cast_bf16(name, w, layer, parts=1):
    shape = w.shape[1:]
    w3 = w.reshape(w.shape[0], -1, shape[-1])
    _, R, C = w3.shape
    tr = _shard_rows(R, C, 4)
    cw = C // parts

    def body(w_ref, *o_refs):
        for g, o_ref in enumerate(o_refs):
            o_ref[...] = w_ref[:, g * cw:(g + 1) * cw].astype(BF16)

    out = pl.pallas_call(body, name=name, grid=(R // tr,),
                         in_specs=[pl.BlockSpec((None, tr, C), lambda i: (layer, i, 0))],
                         out_specs=[pl.BlockSpec((tr, cw), lambda i: (i, 0))] * parts,
                         out_shape=[jax.ShapeDtypeStruct((R, cw), BF16)] * parts,
                         compiler_params=_cparams(("parallel",), VMEM_MID))(w3)
    return [o.reshape(shape[:-1] + (cw,)) for o in out]


def _adamw(w, g, m, v):
    m = ADAM_B1 * m + (1.0 - ADAM_B1) * g
    v = ADAM_B2 * v + (1.0 - ADAM_B2) * (g * g)
    delta = -ADAM_LR * ((m * ADAM_C1) / (jnp.sqrt(v * ADAM_C2) + ADAM_EPS) + ADAM_WD * w)
    return delta, m, v


def _own_core(g4):
    return (lambda p: p[0]) if g4.shape[1] == 2 else (lambda p: 0)


def chip_partial(name, pos, g4, recv_a):
    _, _, R, C = g4.shape
    tr = _shard_rows(R, C, 4)
    core = _own_core(g4)

    def body(pos_ref, g_ref, a_ref, o_ref):
        o_ref[...] = (g_ref[...] + a_ref[...]).astype(BF16)

    return pl.pallas_call(
        body, name=name,
        grid_spec=pltpu.PrefetchScalarGridSpec(
            num_scalar_prefetch=1, grid=(4, R // tr),
            in_specs=[pl.BlockSpec((None, None, tr, C), lambda q, i, p: (q, core(p), i, 0)),
                      pl.BlockSpec((None, tr, C), lambda q, i, p: (q, i, 0))],
            out_specs=pl.BlockSpec((None, tr, C), lambda q, i, p: (q, i, 0))),
        out_shape=jax.ShapeDtypeStruct((4, R, C), BF16),
        compiler_params=_cparams(("parallel", "parallel"), VMEM_MID))(pos, g4, recv_a)


def adamw_shard(name, pos, layer, g4, recv_a, recv_b, w, m, v, prev):
    _, _, R, C = g4.shape
    tr = _shard_rows(R, C, 1)
    n_prev = 0 if prev is None else 4
    core = _own_core(g4)

    def body(pos_ref, g_ref, a_ref, b_ref, w_ref, m_ref, v_ref, *rest):
        go_ref, d_ref, mo_ref, vo_ref = rest[n_prev:]
        gs = g_ref[...] + a_ref[...]
        for j in range(3):
            gs = gs + b_ref[j].astype(F32)
        delta, mn, vn = _adamw(w_ref[...], gs, m_ref[...], v_ref[...])
        go_ref[...] = gs
        d_ref[...] = delta
        mo_ref[...] = mn
        vo_ref[...] = vn

    lay = pl.BlockSpec((None, tr, C), lambda i, p: (layer, i, 0))
    in_specs = [pl.BlockSpec((None, None, tr, C), lambda i, p: (p[1], core(p), i, 0)),
                pl.BlockSpec((None, tr, C), lambda i, p: (p[1], i, 0)),
                pl.BlockSpec((3, tr, C), lambda i, p: (0, i, 0)), lay, lay, lay]
    args = [g4, recv_a, recv_b, w, m, v]
    aliases = {}
    if prev is not None:
        in_specs += [pl.BlockSpec(memory_space=pl.ANY)] * 4
        args += list(prev)
        aliases = {7 + j: j for j in range(4)}
    return pl.pallas_call(
        body, name=name,
        grid_spec=pltpu.PrefetchScalarGridSpec(
            num_scalar_prefetch=1, grid=(R // tr,), in_specs=in_specs, out_specs=[lay] * 4),
        out_shape=[jax.ShapeDtypeStruct(w.shape, F32)] * 4,
        input_output_aliases=aliases,
        compiler_params=_cparams(("parallel",), VMEM_MID))(pos, *args)


def adamw_small(name, gathered, w, m, v):
    _, R, C = gathered.shape
    tr = _pick(R, 512)

    def body(g_ref, w_ref, m_ref, v_ref, go_ref, d_ref, mo_ref, vo_ref):
        gs = g_ref[0]
        for j in range(1, N_DEV):
            gs = gs + g_ref[j]
        delta, mn, vn = _adamw(w_ref[...], gs, m_ref[...], v_ref[...])
        go_ref[...] = gs
        d_ref[...] = delta
        mo_ref[...] = mn
        vo_ref[...] = vn

    spec = pl.BlockSpec((tr, C), lambda i: (i, 0))
    return pl.pallas_call(
        body, name=name, grid=(R // tr,),
        in_specs=[pl.BlockSpec((N_DEV, tr, C), lambda i: (0, i, 0)), spec, spec, spec], out_specs=[spec] * 4,
        out_shape=[jax.ShapeDtypeStruct((R, C), F32)] * 4,
        compiler_params=_cparams(("parallel",), VMEM_MID))(gathered, w, m, v)


def _position():
    return lax.axis_index("x"), lax.axis_index("y"), lax.axis_index("c")


FORWARD_AT = 0.88


def gather_copies(shards):
    n = len(shards)

    def parts(ins, outs, sems):
        send_sems, recv_sems, local_sems = sems
        x, y, c = _position()
        me, sibling = (x, y, c), (x, y, 1 - c)
        chips = [(1 - x, y), (x, 1 - y), (1 - x, 1 - y)]

        def copy(a, k, block, to, src=None):
            blk = outs[a].at[4 * block[0] + 2 * block[1] + block[2]]
            return pltpu.make_async_remote_copy(
                src_ref=blk if src is None else src, dst_ref=blk,
                send_sem=send_sems.at[a, k], recv_sem=recv_sems.at[a, k], device_id=to, device_id_type=MESH)

        mine = [pltpu.make_async_copy(ins[a], outs[a].at[4 * x + 2 * y + c], local_sems.at[a]) for a in range(n)]
        first = [[copy(a, 0, me, sibling, src=ins[a])] +
                 [copy(a, 1 + j, me, (*chip, c), src=ins[a]) for j, chip in enumerate(chips)] for a in range(n)]
        landed = [[copy(a, 1 + j, (*chip, c), me) for j, chip in enumerate(chips)] for a in range(n)]
        passed = [[copy(a, 4 + j, (*chip, c), sibling) for j, chip in enumerate(chips)] for a in range(n)]
        from_sibling = [[copy(a, 0, sibling, me)] +
                        [copy(a, 4 + j, (*chip, 1 - c), me) for j, chip in enumerate(chips)] for a in range(n)]
        return mine, first, landed, passed, from_sibling

    def start(ins, outs, sems):
        mine, first, _, _, _ = parts(ins, outs, sems)
        for a in range(n):
            mine[a].start()
            for cp in first[a]:
                cp.start()

    def forward(ins, outs, sems):
        _, _, landed, passed, _ = parts(ins, outs, sems)
        for a in range(n):
            for j in range(3):
                landed[a][j].wait_recv()
                passed[a][j].start()

    def finish(ins, outs, sems):
        mine, first, _, passed, from_sibling = parts(ins, outs, sems)
        for a in range(n):
            for cp in from_sibling[a]:
                cp.wait_recv()
        for a in range(n):
            for cp in first[a] + passed[a]:
                cp.wait_send()
            mine[a].wait()

    return HostedCopies(
        list(shards), [jax.ShapeDtypeStruct((N_DEV,) + s.shape, s.dtype) for s in shards],
        [pltpu.SemaphoreType.DMA((n, 7)), pltpu.SemaphoreType.DMA((n, 7)), pltpu.SemaphoreType.DMA((n,))],
        [(0.0, start), (FORWARD_AT, forward), (1.0, finish)])


def _exchange_copies(arrays, out_lead, make):
    n = len(arrays)

    def all_copies(ins, outs, sems):
        send_sems, recv_sems = sems
        return [make(ins[a], outs[a], send_sems.at[a, k], recv_sems.at[a, k], k)
                for a in range(n) for k in range(out_lead)]

    def start(ins, outs, sems):
        for cp in all_copies(ins, outs, sems):
            cp.start()

    def finish(ins, outs, sems):
        for cp in all_copies(ins, outs, sems):
            cp.wait()

    return HostedCopies(
        list(arrays), [jax.ShapeDtypeStruct((out_lead,) + a.shape[2:], a.dtype) for a in arrays],
        [pltpu.SemaphoreType.DMA((n, out_lead)), pltpu.SemaphoreType.DMA((n, out_lead))],
        [(0.0, start), (1.0, finish)])


def sibling_copies(grads):
    def make(src, dst, send_sem, recv_sem, q):
        x, y, c = _position()
        core = 1 - c if src.shape[1] == 2 else 0
        return pltpu.make_async_remote_copy(
            src_ref=src.at[q, core], dst_ref=dst.at[q], send_sem=send_sem, recv_sem=recv_sem,
            device_id=(x, y, 1 - c), device_id_type=MESH)

    return _exchange_copies(grads, 4, make)


def chip_copies(parts):
    def make(src, dst, send_sem, recv_sem, j):
        x, y, c = _position()
        chip = [(1 - x, y), (x, 1 - y), (1 - x, 1 - y)][j]
        return pltpu.make_async_remote_copy(
            src_ref=src.at[2 * chip[0] + chip[1], 0], dst_ref=dst.at[j], send_sem=send_sem, recv_sem=recv_sem,
            device_id=(*chip, c), device_id_type=MESH)

    return _exchange_copies(parts, 3, make)


class Carrier:
    def __init__(self):
        self.plan = {}
        self.counts = {}

    def ride(self, site, make, store):
        self.plan.setdefault(site, []).append((make, store))

    def make(self, site, ctx=None):
        if site not in self.plan:
            return None
        group = [make(ctx) for make, _ in self.plan[site]]
        self.counts[site] = [len(c.out_shape) for c in group]
        return merge_copies(group)

    def store(self, site, results):
        if site in self.plan:
            at = 0
            for (_, store), n in zip(self.plan[site], self.counts[site]):
                store(results[at:at + n])
                at += n

    def split(self, site, out):
        if site not in self.plan:
            return out
        self.store(site, out[1])
        return out[0]


def _pool_weight(gathered):
    PG = gathered.shape[-1]
    return gathered.transpose(1, 0, 2, 3).reshape(N_POOL_GROUPS, PG, PG)


def _layer_params(l, ln_g, pool_scale, lam_re, lam_im, log_dt, b_re, b_im, c_re, c_im,
                  d_skip, b_glu, branch_g):
    G, P = lam_re.shape[1:]
    p = dict(
        ln_g=ln_g[l][None, :], pool_scale=pool_scale[l][None, :], d_skip=d_skip[l][None, :],
        b_glu=b_glu[l][None, :], branch_g=branch_g[l][None, :],
        lr=lam_re[l].reshape(G, 1, P), li=lam_im[l].reshape(G, 1, P), ld=log_dt[l].reshape(G, 1, 1),
        br=b_re[l].transpose(0, 2, 1), bi=b_im[l].transpose(0, 2, 1), cr=c_re[l], ci=c_im[l])
    return p


def layer_fwd(l, x, p, gw, D, carrier, staged_parts=0):
    t = f"l{l}_"
    h = rms_fwd(t + "rms_fwd", x, p["ln_g"])
    if staged_parts:
        proj = None
        for g in range(staged_parts):
            site = (l, f"proj{g}")
            proj = carrier.split(site, mm_nn_part(t + f"proj{g}", h, gw("w_in", l, g), g, staged_parts,
                                                  into=proj, out_dtype=BF16, copies=carrier.make(site)))
    else:
        site = (l, "proj")
        proj = carrier.split(site, mm_nn_gathered(t + "proj", h, gw("w_in", l), out_dtype=BF16,
                                                  copies=carrier.make(site)))
    wp = _pool_weight(gw("w_pool", l)[0])
    ypool = pool_fwd(t + "pool_fwd", proj, wp, p["pool_scale"])
    site = (l, "attn_fwd")
    yattn, landed = attn_fwd(t + "attn_fwd", proj, D, copies=carrier.make(site))
    carrier.store(site, landed)
    zr, zi, bbr, bbi = ssm_prep(t + "ssm_prep", p["lr"], p["li"], p["ld"], p["br"], p["bi"])
    ssm_w = dict(wbr=_block_diag(bbr, False), wbi=_block_diag(bbi, False),
                 zr=zr.reshape(1, -1), zi=zi.reshape(1, -1),
                 wcr=_block_diag(p["cr"], True), wci=_block_diag(p["ci"], True))
    site = (l, "ssm_fwd")
    (ypre, hg, xr, xi), landed = ssm_fwd(
        t + "ssm_fwd", proj, ssm_w["wbr"], ssm_w["wbi"], ssm_w["zr"], ssm_w["zi"],
        ssm_w["wcr"], ssm_w["wci"], p["d_skip"], D, copies=carrier.make(site))
    carrier.store(site, landed)
    glu_pre = mm_nn_gathered(t + "glu", hg, gw("w_glu", l))
    y = branch_fwd(t + "branch_fwd", ypool, yattn, glu_pre, proj, p["b_glu"], p["branch_g"])
    out = mm_plain(t + "out", y, gw("w_out", l)[0].reshape(D, D), NN, res=x)
    saved = dict(x=x, h=h, proj=proj, ypool=ypool, yattn=yattn, ypre=ypre, hg=hg, xr=xr, xi=xi,
                 glu_pre=glu_pre, y=y, ssm_w=ssm_w, wp=wp)
    return out, saved


def layer_bwd(l, dres, dres_b, s, p, gw, D, carrier, pos, split_w_in):
    t = f"l{l}_"
    proj = s["proj"]

    def by_target(g):
        return g.reshape(4, 2, -1, g.shape[-1])

    big = {}
    w_out_g = gw("w_out", l)[0].reshape(D, D)
    site = (l, "dy")
    dy = carrier.split(site, mm_plain(t + "dy", dres_b, w_out_g, NT, copies=carrier.make(site)))
    big["w_out"] = by_target(mm_plain(t + "dw_out", s["y"], dres_b, TN).reshape(N_DEV, D // N_DEV, D))
    dypool, dyattn, dglu, dpg, dag, dsg, dbg, dbglu = branch_bwd(
        t + "branch_bwd", dy, s["ypool"], s["yattn"], s["glu_pre"], proj, p["b_glu"], p["branch_g"])
    dhg = mm_nt_gathered(t + "dhg", dglu, gw("w_glu", l))
    big["w_glu"] = by_target(mm_tn_scattered(t + "dw_glu", s["hg"], dglu))
    w = s["ssm_w"]
    site = (l, "ssm_bwd")
    (du, dwcr, dwci, dwbr, dwbi, dar, dai, dds), landed = ssm_bwd(
        t + "ssm_bwd", dhg, s["ypre"], proj, s["xr"], s["xi"], w["wbr"], w["wbi"], w["zr"], w["zi"],
        w["wcr"], w["wci"], p["d_skip"], D, copies=carrier.make(site, big))
    carrier.store(site, landed)
    G, _, P = p["lr"].shape
    dlr, dli, dld, dbr, dbi = ssm_prep_bwd(
        t + "ssm_prep_bwd", p["lr"], p["li"], p["ld"], p["br"], p["bi"],
        dar.reshape(G, 1, P), dai.reshape(G, 1, P), _diag_blocks(dwbr, False), _diag_blocks(dwbi, False))
    site = (l, "attn_bwd")
    (dq, dk, dv), landed = attn_bwd(t + "attn_bwd", proj, s["yattn"], dyattn, D, copies=carrier.make(site, big))
    carrier.store(site, landed)
    dxp, dwp, dps = pool_bwd(t + "pool_bwd", dypool, proj, s["wp"], p["pool_scale"])
    dproj = jnp.concatenate([dxp, dpg, dq, dk, dv, dag, du, dsg], axis=1)
    PG = dwp.shape[1]
    big["w_pool"] = by_target(dwp.reshape(N_POOL_GROUPS, N_DEV, PG // N_DEV, PG).transpose(1, 0, 2, 3))
    if split_w_in:
        site = (l, "dw_in_a")
        to_sibling = carrier.split(site, mm_tn_half(t + "dw_in_a", s["h"], dproj, pos, False,
                                                    copies=carrier.make(site, big)))
        site = (l, "dw_in_b")
        mine = carrier.split(site, mm_tn_half(t + "dw_in_b", s["h"], dproj, pos, True,
                                              copies=carrier.make(site, dict(big, to_sibling=to_sibling[:, None]))))
        big["w_in"] = mine[:, None]
    else:
        big["w_in"] = by_target(mm_tn_scattered(t + "dw_in", s["h"], dproj))
    site = (l, "dh")
    dh = carrier.split(site, mm_nt_gathered(t + "dh", dproj, gw("w_in", l), copies=carrier.make(site, big)))
    dx, dx_b, dlng = rms_bwd(t + "rms_bwd", s["x"], dh, dres, p["ln_g"])
    small = dict(ln_g=dlng[0], pool_scale=dps[0], lam_re=dlr.reshape(G, P), lam_im=dli.reshape(G, P),
                 log_dt=dld.reshape(G), b_re=dbr.transpose(0, 2, 1), b_im=dbi.transpose(0, 2, 1),
                 c_re=_diag_blocks(dwcr, True), c_im=_diag_blocks(dwci, True),
                 d_skip=dds[0], b_glu=dbglu[0], branch_g=dbg[0])
    return dx, dx_b, small


SMALL_NAMES = ("ln_g", "pool_scale", "lam_re", "lam_im", "log_dt", "b_re", "b_im", "c_re", "c_im",
               "d_skip", "b_glu", "branch_g", "final_g")
BIG_NAMES = ("w_in", "w_pool", "w_glu", "w_out")
WEIGHT_ORDER = ("ln_g", "w_in", "w_pool", "pool_scale", "lam_re", "lam_im", "log_dt", "b_re", "b_im",
                "c_re", "c_im", "d_skip", "w_glu", "b_glu", "branch_g", "w_out", "final_g")


PACK_ROWS = 512


def _pack(arrs):
    flat = jnp.concatenate([a.reshape(-1) for a in arrs])
    pad = (-flat.shape[0]) % (PACK_ROWS * LANE)
    return jnp.pad(flat, (0, pad)).reshape(-1, LANE)


def _unpack(packed, like):
    flat = packed.reshape(-1)
    out, off = [], 0
    for a in like:
        out.append(flat[off:off + a.size].reshape(a.shape))
        off += a.size
    return out


def kernel(x, ln_g, w_in, w_pool, pool_scale, lam_re, lam_im, log_dt, b_re, b_im, c_re, c_im, d_skip, w_glu, b_glu, branch_g, w_out, final_g, loss_target, m_ln_g, m_w_in, m_w_pool, m_pool_scale, m_lam_re, m_lam_im, m_log_dt, m_b_re, m_b_im, m_c_re, m_c_im, m_d_skip, m_w_glu, m_b_glu, m_branch_g, m_w_out, m_final_g, v_ln_g, v_w_in, v_w_pool, v_pool_scale, v_lam_re, v_lam_im, v_log_dt, v_b_re, v_b_im, v_c_re, v_c_im, v_d_skip, v_w_glu, v_b_glu, v_branch_g, v_w_out, v_final_g):
    W = dict(ln_g=ln_g, w_in=w_in, w_pool=w_pool, pool_scale=pool_scale, lam_re=lam_re, lam_im=lam_im,
             log_dt=log_dt, b_re=b_re, b_im=b_im, c_re=c_re, c_im=c_im, d_skip=d_skip, w_glu=w_glu,
             b_glu=b_glu, branch_g=branch_g, w_out=w_out, final_g=final_g)
    Mo = dict(ln_g=m_ln_g, w_in=m_w_in, w_pool=m_w_pool, pool_scale=m_pool_scale, lam_re=m_lam_re,
              lam_im=m_lam_im, log_dt=m_log_dt, b_re=m_b_re, b_im=m_b_im, c_re=m_c_re, c_im=m_c_im,
              d_skip=m_d_skip, w_glu=m_w_glu, b_glu=m_b_glu, branch_g=m_branch_g, w_out=m_w_out,
              final_g=m_final_g)
    Vo = dict(ln_g=v_ln_g, w_in=v_w_in, w_pool=v_w_pool, pool_scale=v_pool_scale, lam_re=v_lam_re,
              lam_im=v_lam_im, log_dt=v_log_dt, b_re=v_b_re, b_im=v_b_im, c_re=v_c_re, c_im=v_c_im,
              d_skip=v_d_skip, w_glu=v_w_glu, b_glu=v_b_glu, branch_g=v_branch_g, w_out=v_w_out,
              final_g=v_final_g)
    depth = ln_g.shape[0]
    _, L, D = x.shape
    xc, yc, cc = _position()
    pos = jnp.stack([cc, 2 * xc + yc, 4 * xc + 2 * yc + cc]).astype(jnp.int32)

    def n_parts(n, l):
        return (W_IN_PARTS if l > 0 else W_IN0_PARTS) if n == "w_in" else 1

    shards, landed = {}, {}
    for n in BIG_NAMES:
        for l in range(depth):
            for g, part in enumerate(cast_bf16(f"cast_{n}_{l}", W[n], l, n_parts(n, l))):
                shards[n, l, g] = part
    carrier = Carrier()

    def gw(n, l, g=None):
        return landed[n, l, g] if g is not None else [landed[n, l, i] for i in range(n_parts(n, l))]

    def gather_plan(keys):
        return (lambda ctx: gather_copies([shards[k] for k in keys])), (lambda outs: landed.update(zip(keys, outs)))

    first = [("w_in", 0, 0)] + [("w_pool", l, 0) for l in range(depth)]
    landed.update(zip(first, copies_call("gather_first", gather_copies([shards[k] for k in first]))))
    for g in range(1, W_IN0_PARTS):
        carrier.ride((0, f"proj{g - 1}"), *gather_plan([("w_in", 0, g)]))
    carrier.ride((0, f"proj{W_IN0_PARTS - 1}"), *gather_plan([("w_out", 0, 0), ("w_glu", 0, 0)]))
    for l in range(1, depth):
        for g, call in enumerate(("attn_fwd", "ssm_fwd")):
            carrier.ride((l - 1, call), *gather_plan([("w_in", l, g)]))
        carrier.ride((l, "proj"), *gather_plan([("w_out", l, 0), ("w_glu", l, 0)]))

    own, recv_a, recv_b = {}, {}, {}

    def sibling_plan(l, names, keep, pick):
        def make(ctx):
            own.update({(n, l): ctx[n] for n in keep})
            return sibling_copies(pick(ctx))
        return make, (lambda outs: recv_a.update(zip([(n, l) for n in names], outs)))

    def chip_plan(l, names):
        def make(ctx):
            parts = [chip_partial(f"chip_partial_{n}_{l}", pos, own[n, l], recv_a[n, l])[:, None] for n in names]
            return chip_copies(parts)
        return make, (lambda outs: recv_b.update(zip([(n, l) for n in names], outs)))

    early, late = ("w_out", "w_glu"), ("w_in", "w_pool")
    for l in range(1, depth):
        carrier.ride((l, "dh"), *sibling_plan(l, BIG_NAMES, BIG_NAMES, lambda big: [big[n] for n in BIG_NAMES]))
        carrier.ride((l - 1, "ssm_bwd"), *chip_plan(l, ("w_out", "w_glu", "w_pool")))
        carrier.ride((l - 1, "attn_bwd"), *chip_plan(l, ("w_in",)))
    carrier.ride((0, "ssm_bwd"), *sibling_plan(0, early, early, lambda big: [big[n] for n in early]))
    carrier.ride((0, "dw_in_a"), *chip_plan(0, early))
    carrier.ride((0, "dw_in_b"), *sibling_plan(0, late, ("w_pool",), lambda ctx: [ctx["to_sibling"], ctx["w_pool"]]))

    def last_chip_make(big):
        own["w_in", 0] = big["w_in"]
        return chip_plan(0, late)[0](big)

    carrier.ride((0, "dh"), last_chip_make, chip_plan(0, late)[1])

    params = [_layer_params(l, ln_g, pool_scale, lam_re, lam_im, log_dt, b_re, b_im, c_re, c_im,
                            d_skip, b_glu, branch_g) for l in range(depth)]
    h = x[0]
    saved = []
    for l in range(depth):
        h, s = layer_fwd(l, h, params[l], gw, D, carrier, staged_parts=W_IN0_PARTS if l == 0 else 0)
        saved.append(s)
    loss_part, dres, dres_b, dfinal = loss_head("loss_head", h, final_g[None, :], loss_target[0])
    loss = lax.psum(loss_part[0, 0], ("x", "y", "c"))

    small = [None] * depth
    for l in reversed(range(depth)):
        dres, dres_b, small[l] = layer_bwd(l, dres, dres_b, saved[l], params[l], gw, D, carrier, pos,
                                           split_w_in=(l == 0))
    grad_x = dres[None]

    results = {}
    for n in BIG_NAMES:
        shape = W[n].shape
        R, C = int(math.prod(shape[1:-1])), shape[-1]
        w3, m3, v3 = (t.reshape(depth, R, C) for t in (W[n], Mo[n], Vo[n]))
        prev = None
        for l in range(depth):
            prev = adamw_shard(f"adamw_{n}_{l}", pos, l, own[n, l], recv_a[n, l], recv_b[n, l], w3, m3, v3, prev)
        results[n] = [t.reshape(shape) for t in prev]

    small_like = [W[n] for n in SMALL_NAMES]
    small_grads = [jnp.stack([small[l][n] for l in range(depth)]) for n in SMALL_NAMES[:-1]] + [dfinal[0]]
    gathered = copies_call("gather_small_grads", gather_copies([_pack(small_grads)]))[0]
    packed = adamw_small("adamw_small", gathered, _pack(small_like), _pack([Mo[n] for n in SMALL_NAMES]),
                         _pack([Vo[n] for n in SMALL_NAMES]))
    unpacked = [_unpack(t, small_like) for t in packed]
    for i, n in enumerate(SMALL_NAMES):
        results[n] = [unpacked[j][i] for j in range(4)]

    out = [loss, grad_x]
    for j in range(4):
        out += [results[n][j] for n in WEIGHT_ORDER]
    return tuple(out)
```

```python
import functools
import math

import jax
import jax.numpy as jnp
from jax import lax
from jax.experimental import pallas as pl
from jax.experimental.pallas import tpu as pltpu

F32 = jnp.float32
BF16 = jnp.bfloat16
MESH = pl.DeviceIdType.MESH

EPS = 1e-6
HEAD_DIM = 128
SSM_GROUP = 16
SSM_STATE = 64
GROUPS_PER_CHUNK = 8
CHUNK_U = GROUPS_PER_CHUNK * SSM_GROUP
CHUNK_X = GROUPS_PER_CHUNK * SSM_STATE
N_POOL_GROUPS = 4
POOL_HALO = 16
N_DEV = 8
LANE = 128
FULL_K = 4096
ATTN_TILE = 256
ATTN_DECAY_CUTOFF = 100.0
ROW_TILE = 128
VMEM_BIG = 58 * 1024 * 1024
VMEM_MID = 40 * 1024 * 1024

ADAM_LR = 0.001
ADAM_B1 = 0.9
ADAM_B2 = 0.999
ADAM_EPS = 1e-08
ADAM_WD = 0.01
ADAM_STEP = 10
ADAM_C1 = 1.0 / (1.0 - ADAM_B1 ** ADAM_STEP)
ADAM_C2 = 1.0 / (1.0 - ADAM_B2 ** ADAM_STEP)

NN = (((1,), (0,)), ((), ()))
NT = (((1,), (1,)), ((), ()))
TN = (((0,), (0,)), ((), ()))


def _pick(n, cap):
    if n <= cap:
        return n
    step = LANE if cap >= LANE else 8
    t = (cap // step) * step
    while t > step and n % t:
        t -= step
    assert n % t == 0, (n, cap)
    return t


def _cparams(sem, vmem=None):
    return pltpu.CompilerParams(dimension_semantics=sem, vmem_limit_bytes=vmem)


def _dot(a, b, dn=NN):
    return lax.dot_general(a, b, dn, preferred_element_type=F32)


def _sigmoid(x):
    e = jnp.exp(-jnp.abs(x))
    r = 1.0 / (1.0 + e)
    return jnp.where(x >= 0, r, e * r)


HBM = pl.BlockSpec(memory_space=pl.ANY)


class HostedCopies:
    def __init__(self, inputs, out_shape, scratch, phases):
        self.inputs, self.out_shape, self.scratch, self.phases = inputs, out_shape, scratch, phases

    def emit(self, ins, outs, sems, step, total):
        plan = {}
        for frac, fn in self.phases:
            plan.setdefault(min(total - 1, int(frac * total)), []).append(fn)
        for s in sorted(plan):
            def run(fns=plan[s]):
                for fn in fns:
                    fn(ins, outs, sems)
            if total == 1:
                run()
            else:
                pl.when(step == s)(run)


def copies_call(name, copies):
    n_i, n_o = len(copies.inputs), len(copies.out_shape)

    def body(*refs):
        copies.emit(refs[:n_i], refs[n_i:n_i + n_o], refs[n_i + n_o:], 0, 1)

    return pl.pallas_call(
        body, name=name, in_specs=[HBM] * n_i, out_specs=[HBM] * n_o, out_shape=copies.out_shape,
        scratch_shapes=copies.scratch, compiler_params=pltpu.CompilerParams(has_side_effects=True),
    )(*copies.inputs)


def _matmul(name, a, b, *, grid, a_spec, b_spec, o_spec, out_shape, dn,
            res=None, res_spec=None, pos=None, copies=None, product=None, into=None):
    ni, nj, nk = grid
    bs, b_specs = (list(b), list(b_spec)) if isinstance(b, (list, tuple)) else ([b], [b_spec])
    n_b = len(bs)
    n_pos = 0 if pos is None else 1
    n_res = 0 if res is None else 1
    n_into = 0 if into is None else 1
    n_ci = 0 if copies is None else len(copies.inputs)
    n_co = 0 if copies is None else len(copies.out_shape)

    def body(*refs):
        refs = refs[n_pos:]
        a_ref, b_refs = refs[0], refs[1:1 + n_b]
        r_ref = refs[1 + n_b] if n_res else None
        base = 1 + n_b + n_res + n_into
        cin = refs[base:base + n_ci]
        o_ref = refs[base + n_ci]
        cout = refs[base + n_ci + 1:base + n_ci + 1 + n_co]
        sems = refs[base + n_ci + 1 + n_co:]
        k = pl.program_id(2)
        if copies is not None:
            step = (pl.program_id(0) * nj + pl.program_id(1)) * nk + k
            copies.emit(cin, cout, sems, step, ni * nj * nk)

        if product is None:
            part = _dot(a_ref[...].astype(BF16), b_refs[0][...].astype(BF16), dn)
        else:
            part = product(a_ref, b_refs)
        if nk == 1:
            if r_ref is not None:
                part = part + r_ref[...]
            o_ref[...] = part.astype(o_ref.dtype)
        else:
            @pl.when(k == 0)
            def _():
                o_ref[...] = part if r_ref is None else part + r_ref[...]

            @pl.when(k > 0)
            def _():
                o_ref[...] += part

    assert nk == 1 or out_shape.dtype == F32
    in_specs = [a_spec] + b_specs + ([res_spec] if n_res else []) + [HBM] * (n_into + n_ci)
    args = (((pos,) if n_pos else ()) + (a, *bs) + ((res,) if n_res else ()) + ((into,) if n_into else ())
            + tuple(copies.inputs if copies else ()))
    aliases = {n_pos + 1 + n_b + n_res: 0} if n_into else {}
    out_specs = [o_spec] + [HBM] * n_co
    out_shapes = [out_shape] + list(copies.out_shape if copies else [])
    scratch = list(copies.scratch if copies else [])
    params = pltpu.CompilerParams(
        dimension_semantics=("arbitrary",) * 3 if copies else ("parallel", "parallel", "arbitrary"),
        vmem_limit_bytes=VMEM_BIG, has_side_effects=copies is not None)
    out = pl.pallas_call(
        body, name=name,
        grid_spec=pltpu.PrefetchScalarGridSpec(
            num_scalar_prefetch=n_pos, grid=grid, in_specs=in_specs, out_specs=out_specs, scratch_shapes=scratch),
        out_shape=out_shapes, input_output_aliases=aliases, compiler_params=params)(*args)
    return out[0] if copies is None else (out[0], list(out[1:]))


def mm_nn_gathered(name, a, parts, out_dtype=F32, copies=None):
    M, K = a.shape
    P, w = len(parts), parts[0].shape[2]
    nper = P * w
    tm, tk, tn = _pick(M, 1024), _pick(K, FULL_K), _pick(w, 768)
    r = w // tn
    per_part = N_DEV * r

    def b_spec(g):
        def index(i, j, k, *_):
            t = jnp.clip(j - g * per_part, 0, per_part - 1)
            return (t // r, k, t % r)
        return pl.BlockSpec((None, tk, tn), index)

    def o_index(i, j, k, *_):
        t = j % per_part
        return (i, (t // r) * (nper // tn) + (j // per_part) * r + t % r)

    def product(a_ref, b_refs):
        j = pl.program_id(1)
        b = b_refs[0][...]
        for g in range(1, P):
            b = jnp.where(j >= g * per_part, b_refs[g][...], b)
        return _dot(a_ref[...].astype(BF16), b.astype(BF16))

    return _matmul(
        name, a, list(parts), grid=(M // tm, P * per_part, K // tk),
        a_spec=pl.BlockSpec((tm, tk), lambda i, j, k, *_: (i, k)),
        b_spec=[b_spec(g) for g in range(P)], o_spec=pl.BlockSpec((tm, tn), o_index),
        out_shape=jax.ShapeDtypeStruct((M, N_DEV * nper), out_dtype), dn=NN, copies=copies,
        product=product if P > 1 else None)


def mm_nn_part(name, a, part, g, P, into=None, out_dtype=F32, copies=None):
    M, K = a.shape
    w = part.shape[2]
    tm, tk, tn = _pick(M, 1024), _pick(K, FULL_K), _pick(w, 768)
    r = w // tn
    return _matmul(
        name, a, part, grid=(M // tm, N_DEV * r, K // tk),
        a_spec=pl.BlockSpec((tm, tk), lambda i, j, k, *_: (i, k)),
        b_spec=pl.BlockSpec((None, tk, tn), lambda i, j, k, *_: (j // r, k, j % r)),
        o_spec=pl.BlockSpec((tm, tn), lambda i, j, k, *_: (i, (j // r) * (P * r) + g * r + j % r)),
        out_shape=jax.ShapeDtypeStruct((M, N_DEV * P * w), out_dtype), dn=NN, copies=copies, into=into)


NT_SLICES = 2
W_IN_PARTS = 3
W_IN0_PARTS = 3


def mm_nt_gathered(name, a, parts, out_dtype=F32, copies=None):
    M, _ = a.shape
    P, (_, N, w) = len(parts), parts[0].shape
    nper = P * w
    tm, tn = _pick(M, 1024), _pick(N, 1024)
    S = NT_SLICES

    def product(a_ref, b_refs):
        total = None
        for s in range(S):
            for g in range(P):
                off = (s * P + g) * w
                term = _dot(a_ref[:, off:off + w].astype(BF16), b_refs[g][s], NT)
                total = term if total is None else total + term
        return total

    return _matmul(
        name, a, list(parts), grid=(M // tm, N // tn, N_DEV // S),
        a_spec=pl.BlockSpec((tm, S * nper), lambda i, j, k, *_: (i, k)),
        b_spec=[pl.BlockSpec((S, tn, w), lambda i, j, k, *_: (k, j, 0)) for _ in range(P)],
        o_spec=pl.BlockSpec((tm, tn), lambda i, j, k, *_: (i, j)),
        out_shape=jax.ShapeDtypeStruct((M, N), out_dtype), dn=NT, copies=copies, product=product)


def mm_tn_scattered(name, a, b, copies=None):
    L, M = a.shape
    nper = b.shape[1] // N_DEV
    tm, tn, tk = _pick(M, 1024), _pick(nper, 768), _pick(L, FULL_K)
    r = nper // tn
    return _matmul(
        name, a, b, grid=(M // tm, N_DEV * r, L // tk),
        a_spec=pl.BlockSpec((tk, tm), lambda i, j, k, *_: (k, i)),
        b_spec=pl.BlockSpec((tk, tn), lambda i, j, k, *_: (k, j)),
        o_spec=pl.BlockSpec((None, tm, tn), lambda i, j, k, *_: (j // r, i, j % r)),
        out_shape=jax.ShapeDtypeStruct((N_DEV, M, nper), F32), dn=TN, copies=copies)


def mm_tn_half(name, a, b, pos, own, copies=None):
    L, M = a.shape
    nper = b.shape[1] // N_DEV
    tm, tn, tk = _pick(M, 1024), _pick(nper, 768), _pick(L, FULL_K)
    r = nper // tn

    def b_map(i, j, k, p):
        core = p[0] if own else 1 - p[0]
        return (k, (2 * (j // r) + core) * r + j % r)

    return _matmul(
        name, a, b, grid=(M // tm, 4 * r, L // tk),
        a_spec=pl.BlockSpec((tk, tm), lambda i, j, k, *_: (k, i)),
        b_spec=pl.BlockSpec((tk, tn), b_map),
        o_spec=pl.BlockSpec((None, tm, tn), lambda i, j, k, *_: (j // r, i, j % r)),
        out_shape=jax.ShapeDtypeStruct((4, M, nper), F32), dn=TN, pos=pos, copies=copies)


def mm_plain(name, a, b, dn, out_dtype=F32, res=None, copies=None):
    if dn == NN:
        (M, K), N = a.shape, b.shape[1]
    elif dn == NT:
        (M, K), N = a.shape, b.shape[0]
    else:
        (K, M), N = a.shape, b.shape[1]
    tm, tn, tk = _pick(M, 1024), _pick(N, 512), _pick(K, FULL_K)
    a_spec =(pl.BlockSpec((tk, tm), lambda i, j, k, *_: (k, i)) if dn == TN
              else pl.BlockSpec((tm, tk), lambda i, j, k, *_: (i, k)))
    b_spec = (pl.BlockSpec((tn, tk), lambda i, j, k, *_: (j, k)) if dn == NT
              else pl.BlockSpec((tk, tn), lambda i, j, k, *_: (k, j)))
    o_spec = pl.BlockSpec((tm, tn), lambda i, j, k, *_: (i, j))
    return _matmul(
        name, a, b, grid=(M // tm, N // tn, K // tk), a_spec=a_spec, b_spec=b_spec, o_spec=o_spec,
        out_shape=jax.ShapeDtypeStruct((M, N), out_dtype), dn=dn,
        res=res, res_spec=o_spec if res is not None else None, copies=copies)


def rms_fwd(name, x, g):
    L, D = x.shape
    tr = _pick(L, ROW_TILE)

    def body(x_ref, g_ref, h_ref):
        xv = x_ref[...]
        r = lax.rsqrt(jnp.mean(xv * xv, axis=-1, keepdims=True) + EPS)
        h_ref[...] = (xv * r * g_ref[...]).astype(BF16)

    return pl.pallas_call(
        body, name=name, grid=(L // tr,),
        in_specs=[pl.BlockSpec((tr, D), lambda i: (i, 0)), pl.BlockSpec((1, D), lambda i: (0, 0))],
        out_specs=pl.BlockSpec((tr, D), lambda i: (i, 0)),
        out_shape=jax.ShapeDtypeStruct((L, D), BF16),
        compiler_params=_cparams(("parallel",), VMEM_MID))(x, g)


def rms_bwd(name, x, dh, dres, g):
    L, D = x.shape
    tr = _pick(L, ROW_TILE)

    def body(x_ref, dh_ref, dr_ref, g_ref, dx_ref, dxb_ref, dg_ref):
        xv = x_ref[...]
        r = lax.rsqrt(jnp.mean(xv * xv, axis=-1, keepdims=True) + EPS)
        xh = xv * r
        dhv = dh_ref[...]
        dn = dhv * g_ref[...]
        dxv = dr_ref[...] + r * (dn - xh * jnp.mean(dn * xh, axis=-1, keepdims=True))
        dx_ref[...] = dxv
        dxb_ref[...] = dxv.astype(BF16)

        @pl.when(pl.program_id(0) == 0)
        def _():
            dg_ref[...] = jnp.zeros_like(dg_ref)

        dg_ref[...] += jnp.sum(dhv * xh, axis=0, keepdims=True)

    row = pl.BlockSpec((tr, D), lambda i: (i, 0))
    vec = pl.BlockSpec((1, D), lambda i: (0, 0))
    return pl.pallas_call(
        body, name=name, grid=(L // tr,), in_specs=[row, row, row, vec], out_specs=[row, row, vec],
        out_shape=[jax.ShapeDtypeStruct((L, D), F32), jax.ShapeDtypeStruct((L, D), BF16),
                   jax.ShapeDtypeStruct((1, D), F32)],
        compiler_params=_cparams(("arbitrary",), VMEM_MID))(x, dh, dres, g)


def loss_head(name, x, g, target):
    L, D = x.shape
    tr = _pick(L, ROW_TILE)

    def body(x_ref, g_ref, t_ref, loss_ref, dx_ref, dxb_ref, dg_ref):
        xv = x_ref[...]
        gv = g_ref[...]
        r = lax.rsqrt(jnp.mean(xv * xv, axis=-1, keepdims=True) + EPS)
        xh = xv * r
        err = xh * gv - t_ref[...]
        dy = err * (1.0 / D)
        dn = dy * gv
        dxv = r * (dn - xh * jnp.mean(dn * xh, axis=-1, keepdims=True))
        dx_ref[...] = dxv
        dxb_ref[...] = dxv.astype(BF16)

        @pl.when(pl.program_id(0) == 0)
        def _():
            dg_ref[...] = jnp.zeros_like(dg_ref)
            loss_ref[...] = jnp.zeros_like(loss_ref)

        dg_ref[...] += jnp.sum(dy * xh, axis=0, keepdims=True)
        row_loss = jnp.sum(err * err, axis=-1, keepdims=True) * (0.5 / D)
        loss_ref[...] += jnp.sum(row_loss, axis=0, keepdims=True)

    row = pl.BlockSpec((tr, D), lambda i: (i, 0))
    vec = pl.BlockSpec((1, D), lambda i: (0, 0))
    one = pl.BlockSpec((1, 1), lambda i: (0, 0))
    return pl.pallas_call(
        body, name=name, grid=(L // tr,), in_specs=[row, vec, row], out_specs=[one, row, row, vec],
        out_shape=[jax.ShapeDtypeStruct((1, 1), F32), jax.ShapeDtypeStruct((L, D), F32),
                   jax.ShapeDtypeStruct((L, D), BF16), jax.ShapeDtypeStruct((1, D), F32)],
        compiler_params=_cparams(("arbitrary",), VMEM_MID))(x, g, target)


def _branch_specs(D, tr):
    DP, DA, DS = D // 4, D // 2, D // 4
    return dict(
        pool=pl.BlockSpec((tr, DP), lambda i: (i, 0)),
        attn=pl.BlockSpec((tr, DA), lambda i: (i, 0)),
        glu=pl.BlockSpec((tr, 2 * DS), lambda i: (i, 0)),
        p_gate=pl.BlockSpec((tr, DP), lambda i: (i, 1)),
        a_gate=pl.BlockSpec((tr, DA), lambda i: (i, 4)),
        s_gate=pl.BlockSpec((tr, DS), lambda i: (i, 11)),
        bglu=pl.BlockSpec((1, 2 * DS), lambda i: (0, 0)),
        bg=pl.BlockSpec((1, D), lambda i: (0, 0)),
        row=pl.BlockSpec((tr, D), lambda i: (i, 0)),
    )


def branch_fwd(name, ypool, yattn, glu_pre, proj, b_glu, branch_g):
    L, DP = ypool.shape
    D = 4 * DP
    DA, DS = D // 2, D // 4
    tr = _pick(L, ROW_TILE)
    s = _branch_specs(D, tr)

    def body(yp_ref, ya_ref, gl_ref, pg_ref, ag_ref, sg_ref, bgl_ref, bg_ref, y_ref):
        pre = gl_ref[...] + bgl_ref[...]
        ys = pre[:, :DS] * _sigmoid(pre[:, DS:])
        bg = bg_ref[...]

        def one(raw, gate, g):
            gate = gate.astype(F32)
            r = lax.rsqrt(jnp.mean(raw * raw, axis=-1, keepdims=True) + EPS)
            return raw * r * g * (gate * _sigmoid(gate))

        y_ref[:, :DP] = one(yp_ref[...], pg_ref[...], bg[:, :DP]).astype(BF16)
        y_ref[:, DP:DP + DA] = one(ya_ref[...], ag_ref[...], bg[:, DP:DP + DA]).astype(BF16)
        y_ref[:, DP + DA:] = one(ys, sg_ref[...], bg[:, DP + DA:]).astype(BF16)

    return pl.pallas_call(
        body, name=name, grid=(L // tr,),
        in_specs=[s["pool"], s["attn"], s["glu"], s["p_gate"], s["a_gate"], s["s_gate"], s["bglu"], s["bg"]],
        out_specs=s["row"], out_shape=jax.ShapeDtypeStruct((L, D), BF16),
        compiler_params=_cparams(("parallel",), VMEM_MID))(ypool, yattn, glu_pre, proj, proj, proj, b_glu, branch_g)


def branch_bwd(name, dy, ypool, yattn, glu_pre, proj, b_glu, branch_g):
    L, DP = ypool.shape
    D = 4 * DP
    DA, DS = D // 2, D // 4
    tr = _pick(L, ROW_TILE // 2)
    s = _branch_specs(D, tr)

    def body(dy_ref, yp_ref, ya_ref, gl_ref, pg_ref, ag_ref, sg_ref, bgl_ref, bg_ref,
             dyp_ref, dya_ref, dgl_ref, dpg_ref, dag_ref, dsg_ref, dbg_ref, dbgl_ref):
        @pl.when(pl.program_id(0) == 0)
        def _():
            dbg_ref[...] = jnp.zeros_like(dbg_ref)
            dbgl_ref[...] = jnp.zeros_like(dbgl_ref)

        bg = bg_ref[...]

        def one(raw, gate, g, dyb):
            gate = gate.astype(F32)
            r = lax.rsqrt(jnp.mean(raw * raw, axis=-1, keepdims=True) + EPS)
            n = raw * r
            sg = _sigmoid(gate)
            sl = gate * sg
            dgate = dyb * n * g * (sg * (1.0 + gate * (1.0 - sg)))
            dbg = jnp.sum(dyb * n * sl, axis=0, keepdims=True)
            dn = dyb * g * sl
            draw = r * (dn - n * jnp.mean(dn * n, axis=-1, keepdims=True))
            return draw, dgate, dbg

        draw, dgate, dbg = one(yp_ref[...], pg_ref[...], bg[:, :DP], dy_ref[:, :DP])
        dyp_ref[...] = draw
        dpg_ref[...] = dgate.astype(BF16)
        dbg_ref[:, :DP] += dbg

        draw, dgate, dbg = one(ya_ref[...], ag_ref[...], bg[:, DP:DP + DA], dy_ref[:, DP:DP + DA])
        dya_ref[...] = draw
        dag_ref[...] = dgate.astype(BF16)
        dbg_ref[:, DP:DP + DA] += dbg

        pre = gl_ref[...] + bgl_ref[...]
        val = pre[:, :DS]
        sgt = _sigmoid(pre[:, DS:])
        draw, dgate, dbg = one(val * sgt, sg_ref[...], bg[:, DP + DA:], dy_ref[:, DP + DA:])
        dsg_ref[...] = dgate.astype(BF16)
        dbg_ref[:, DP + DA:] += dbg
        dval = draw * sgt
        dgt = draw * val * sgt * (1.0 - sgt)
        dgl_ref[:, :DS] = dval.astype(BF16)
        dgl_ref[:, DS:] = dgt.astype(BF16)
        dbgl_ref[:, :DS] += jnp.sum(dval, axis=0, keepdims=True)
        dbgl_ref[:, DS:] += jnp.sum(dgt, axis=0, keepdims=True)

    loc = lambda w: pl.BlockSpec((tr, w), lambda i: (i, 0))
    return pl.pallas_call(
        body, name=name, grid=(L // tr,),
        in_specs=[s["row"], s["pool"], s["attn"], s["glu"], s["p_gate"], s["a_gate"], s["s_gate"], s["bglu"], s["bg"]],
        out_specs=[loc(DP), loc(DA), loc(2 * DS), loc(DP), loc(DA), loc(DS), s["bg"], s["bglu"]],
        out_shape=[jax.ShapeDtypeStruct((L, DP), F32), jax.ShapeDtypeStruct((L, DA), F32),
                   jax.ShapeDtypeStruct((L, 2 * DS), BF16), jax.ShapeDtypeStruct((L, DP), BF16),
                   jax.ShapeDtypeStruct((L, DA), BF16), jax.ShapeDtypeStruct((L, DS), BF16),
                   jax.ShapeDtypeStruct((1, D), F32), jax.ShapeDtypeStruct((1, 2 * DS), F32)],
        compiler_params=_cparams(("arbitrary",), VMEM_BIG),
    )(dy, ypool, yattn, glu_pre, proj, proj, proj, b_glu, branch_g)


def _pool_select(g, s2, s4, s8, s16):
    return jnp.where(g == 0, s2, jnp.where(g == 1, s4, jnp.where(g == 2, s8, s16)))


def _pool_window(g):
    return jnp.where(g == 0, 2.0, jnp.where(g == 1, 4.0, jnp.where(g == 2, 8.0, 16.0))).astype(F32)


def _pooled_chunk(pad, g, r0, ch):
    xh = pad[pl.ds(r0, ch + POOL_HALO), :]
    s2 = xh + pltpu.roll(xh, 1, 0)
    s4 = s2 + pltpu.roll(s2, 2, 0)
    s8 = s4 + pltpu.roll(s4, 4, 0)
    s16 = s8 + pltpu.roll(s8, 8, 0)
    win = _pool_select(g, s2, s4, s8, s16)[POOL_HALO:]
    pos = (r0 + 1 + lax.broadcasted_iota(jnp.int32, (ch, 1), 0)).astype(F32)
    return win / jnp.minimum(pos, _pool_window(g)) - xh[POOL_HALO:]


def pool_fwd(name, proj, wp, scale):
    L = proj.shape[0]
    DP = scale.shape[1]
    PG = DP // N_POOL_GROUPS
    ch = _pick(L, 256)

    def body(x_ref, w_ref, s_ref, o_ref, pad):
        g = pl.program_id(0)
        pad[0:POOL_HALO, :] = jnp.zeros((POOL_HALO, PG), F32)
        pad[POOL_HALO:, :] = x_ref[...].astype(F32)

        def chunk(ci, carry):
            r0 = pl.multiple_of(ci * ch, ch)
            pooled = _pooled_chunk(pad, g, r0, ch)
            o_ref[pl.ds(r0, ch), :] = _dot(pooled.astype(BF16), w_ref[...]) * s_ref[...]
            return carry

        lax.fori_loop(0, L // ch, chunk, 0)

    return pl.pallas_call(
        body, name=name, grid=(N_POOL_GROUPS,),
        in_specs=[pl.BlockSpec((L, PG), lambda g: (0, g)), pl.BlockSpec((None, PG, PG), lambda g: (g, 0, 0)),
                  pl.BlockSpec((1, PG), lambda g: (0, g))],
        out_specs=pl.BlockSpec((L, PG), lambda g: (0, g)),
        out_shape=jax.ShapeDtypeStruct((L, DP), F32),
        scratch_shapes=[pltpu.VMEM((L + POOL_HALO, PG), F32)],
        compiler_params=_cparams(("parallel",), VMEM_MID))(proj, wp, scale)


def pool_bwd(name, dyraw, proj, wp, scale):
    L = proj.shape[0]
    DP = scale.shape[1]
    PG = DP // N_POOL_GROUPS
    ch = _pick(L, 256)

    def body(dy_ref, x_ref, w_ref, s_ref, dx_ref, dw_ref, ds_ref, pad, dpad, dpo):
        g = pl.program_id(0)
        pad[0:POOL_HALO, :] = jnp.zeros((POOL_HALO, PG), F32)
        pad[POOL_HALO:, :] = x_ref[...].astype(F32)
        dpad[L:, :] = jnp.zeros((POOL_HALO, PG), F32)
        dw_ref[...] = jnp.zeros_like(dw_ref)
        ds_ref[...] = jnp.zeros_like(ds_ref)
        wv = w_ref[...]
        win_f = _pool_window(g)

        def chunk(ci, carry):
            r0 = pl.multiple_of(ci * ch, ch)
            pooled = _pooled_chunk(pad, g, r0, ch).astype(BF16)
            dyv = dy_ref[pl.ds(r0, ch), :]
            ds_ref[...] += jnp.sum(dyv * _dot(pooled, wv), axis=0, keepdims=True)
            dmixed = (dyv * s_ref[...]).astype(BF16)
            dw_ref[...] += _dot(pooled, dmixed, TN)
            dpooled = _dot(dmixed, wv, NT)
            pos = (r0 + 1 + lax.broadcasted_iota(jnp.int32, (ch, 1), 0)).astype(F32)
            dpad[pl.ds(r0, ch), :] = dpooled / jnp.minimum(pos, win_f)
            dpo[pl.ds(r0, ch), :] = dpooled
            return carry

        lax.fori_loop(0, L // ch, chunk, 0)

        def chunk2(ci, carry):
            r0 = pl.multiple_of(ci * ch, ch)
            n = ch + POOL_HALO
            dm = dpad[pl.ds(r0, n), :]
            s2 = dm + pltpu.roll(dm, n - 1, 0)
            s4 = s2 + pltpu.roll(s2, n - 2, 0)
            s8 = s4 + pltpu.roll(s4, n - 4, 0)
            s16 = s8 + pltpu.roll(s8, n - 8, 0)
            win = _pool_select(g, s2, s4, s8, s16)[:ch]
            dx_ref[pl.ds(r0, ch), :] = (win - dpo[pl.ds(r0, ch), :]).astype(BF16)
            return carry

        lax.fori_loop(0, L // ch, chunk2, 0)

    col = pl.BlockSpec((L, PG), lambda g: (0, g))
    return pl.pallas_call(
        body, name=name, grid=(N_POOL_GROUPS,),
        in_specs=[col, col, pl.BlockSpec((None, PG, PG), lambda g: (g, 0, 0)), pl.BlockSpec((1, PG), lambda g: (0, g))],
        out_specs=[col, pl.BlockSpec((None, PG, PG), lambda g: (g, 0, 0)), pl.BlockSpec((1, PG), lambda g: (0, g))],
        out_shape=[jax.ShapeDtypeStruct((L, DP), BF16), jax.ShapeDtypeStruct((N_POOL_GROUPS, PG, PG), F32),
                   jax.ShapeDtypeStruct((1, DP), F32)],
        scratch_shapes=[pltpu.VMEM((L + POOL_HALO, PG), F32), pltpu.VMEM((L + POOL_HALO, PG), F32),
                        pltpu.VMEM((L, PG), F32)],
        compiler_params=_cparams(("parallel",), VMEM_MID))(dyraw, proj, wp, scale)


def _attn_tile(L):
    return _pick(L, ATTN_TILE)


def _tri(t, strict):
    j = lax.broadcasted_iota(jnp.int32, (t, t), 0)
    s = lax.broadcasted_iota(jnp.int32, (t, t), 1)
    return ((j > s) if strict else (j >= s)).astype(BF16)


def _attn_block(q, kt, rb, after, diagonal):
    tq, tk = q.shape[0], kt.shape[0]
    z = _dot(q, kt, NT)
    e = jnp.exp(-jnp.abs(z))
    l1p = jnp.log(1.0 + e)
    log_sig = jnp.minimum(z, 0.0) - l1p
    b = -jnp.maximum(z, 0.0) - l1p
    causal = None
    if diagonal:
        causal = lax.broadcasted_iota(jnp.int32, (tq, tk), 1) < lax.broadcasted_iota(jnp.int32, (tq, tk), 0)
        b = jnp.where(causal, b, 0.0)
    b_hi = b.astype(BF16)
    b_lo = (b - b_hi.astype(F32)).astype(BF16)
    suffix = _dot(b_hi, after) + _dot(b_lo, after) + rb
    w = jnp.exp(log_sig + suffix)
    if diagonal:
        w = jnp.where(causal, w, 0.0)
    return z, e, causal, b, w


def _attn_sweep(i, visit):
    go = visit(i, True)
    lax.while_loop(lambda c: jnp.logical_and(c[0] >= 0, c[1]),
                   lambda c: (c[0] - 1, visit(c[0], False)), (i - 1, go))


def attn_fwd(name, proj, D, copies=None):
    L = proj.shape[0]
    DA = D // 2
    H = DA // HEAD_DIM
    tq = tk = _attn_tile(L)
    qo, ko, vo = (D // 2) // HEAD_DIM, D // HEAD_DIM, (3 * D // 2) // HEAD_DIM
    scale = HEAD_DIM ** -0.5

    def body(q_ref, k_ref, v_ref, tri_ref, o_ref, kb_s, vb_s, acc, rb):
        i = pl.program_id(1)

        @pl.when(i == 0)
        def _():
            kb_s[...] = k_ref[...].astype(BF16)
            vb_s[...] = v_ref[...].astype(BF16)

        q = (q_ref[...].astype(F32) * scale).astype(BF16)
        acc[...] = jnp.zeros_like(acc)
        rb[...] = jnp.zeros_like(rb)

        def visit(kb, diagonal):
            k0 = pl.multiple_of(kb * tk, tk)
            kt = kb_s[pl.ds(k0, tk), :]
            vt = vb_s[pl.ds(k0, tk), :]
            _, _, _, b, w = _attn_block(q, kt, rb[...], tri_ref[...], diagonal)
            acc[...] += _dot(w.astype(BF16), vt)
            rbn = rb[...] + jnp.sum(b, axis=1, keepdims=True)
            rb[...] = rbn
            return jnp.max(rbn) > -ATTN_DECAY_CUTOFF

        _attn_sweep(i, visit)
        o_ref[...] = acc[...]

    (out,), landed = _call(
        body, name=name, grid=(H, L // tq),
        in_specs=[pl.BlockSpec((tq, HEAD_DIM), lambda h, i: (i, qo + h)),
                  pl.BlockSpec((L, HEAD_DIM), lambda h, i: (0, ko + h)),
                  pl.BlockSpec((L, HEAD_DIM), lambda h, i: (0, vo + h)),
                  pl.BlockSpec((tk, tk), lambda h, i: (0, 0))],
        out_specs=[pl.BlockSpec((tq, HEAD_DIM), lambda h, i: (i, h))],
        out_shape=[jax.ShapeDtypeStruct((L, DA), F32)],
        scratch_shapes=[pltpu.VMEM((L, HEAD_DIM), BF16), pltpu.VMEM((L, HEAD_DIM), BF16),
                        pltpu.VMEM((tq, HEAD_DIM), F32), pltpu.VMEM((tq, 1), F32)],
        vmem=VMEM_MID, args=(proj, proj, proj, _tri(tk, True)), semantics=("arbitrary", "arbitrary"),
        copies=copies)
    return out, landed


def attn_bwd(name, proj, o, do, D, copies=None):
    L = proj.shape[0]
    DA = D // 2
    H = DA // HEAD_DIM
    tq = tk = _attn_tile(L)
    qo, ko, vo = (D // 2) // HEAD_DIM, D // HEAD_DIM, (3 * D // 2) // HEAD_DIM
    scale = HEAD_DIM ** -0.5

    def body(q_ref, k_ref, v_ref, o_ref, do_ref, after_ref, from_ref, dq_ref, dk_ref, dv_ref,
             kb_s, vb_s, dk_s, dv_s, dq_acc, rb, rg):
        i = pl.program_id(1)
        nq = pl.num_programs(1)

        @pl.when(i == 0)
        def _():
            kb_s[...] = k_ref[...].astype(BF16)
            vb_s[...] = v_ref[...].astype(BF16)
            dk_s[...] = jnp.zeros_like(dk_s)
            dv_s[...] = jnp.zeros_like(dv_s)

        q = (q_ref[...].astype(F32) * scale).astype(BF16)
        dob = do_ref[...].astype(BF16)
        delta = jnp.sum(dob.astype(F32) * o_ref[...], axis=1, keepdims=True)
        dq_acc[...] = jnp.zeros_like(dq_acc)
        rb[...] = jnp.zeros_like(rb)
        rg[...] = jnp.zeros_like(rg)

        def visit(kb, diagonal):
            k0 = pl.multiple_of(kb * tk, tk)
            kt = kb_s[pl.ds(k0, tk), :]
            vt = vb_s[pl.ds(k0, tk), :]
            z, e, causal, b, w = _attn_block(q, kt, rb[...], after_ref[...], diagonal)
            wq = w.astype(BF16)
            dw = _dot(dob, vt, NT)
            g = wq.astype(F32) * dw
            g_hi = g.astype(BF16)
            g_lo = (g - g_hi.astype(F32)).astype(BF16)
            from_s = from_ref[...]
            suffix_g = _dot(g_hi, from_s) + _dot(g_lo, from_s) + rg[...]
            before = delta - suffix_g
            r = 1.0 / (1.0 + e)
            sig = jnp.where(z >= 0, r, e * r)
            sig_neg = jnp.where(z >= 0, e * r, r)
            dz = g * sig_neg - before * sig
            if diagonal:
                dz = jnp.where(causal, dz, 0.0)
            dz = dz.astype(BF16)
            dq_acc[...] += _dot(dz, kt)
            dk_s[pl.ds(k0, tk), :] += _dot(dz, q, TN)
            dv_s[pl.ds(k0, tk), :] += _dot(wq, dob, TN)
            rbn = rb[...] + jnp.sum(b, axis=1, keepdims=True)
            rb[...] = rbn
            rg[...] += jnp.sum(g, axis=1, keepdims=True)
            return jnp.max(rbn) > -ATTN_DECAY_CUTOFF

        _attn_sweep(i, visit)
        dq_ref[...] = (dq_acc[...] * scale).astype(BF16)

        @pl.when(i == nq - 1)
        def _():
            dk_ref[...] = dk_s[...].astype(BF16)
            dv_ref[...] = dv_s[...].astype(BF16)

    blk = pl.BlockSpec((tq, HEAD_DIM), lambda h, i: (i, h))
    full = pl.BlockSpec((L, HEAD_DIM), lambda h, i: (0, h))
    return _call(
        body, name=name, grid=(H, L // tq),
        in_specs=[pl.BlockSpec((tq, HEAD_DIM), lambda h, i: (i, qo + h)),
                  pl.BlockSpec((L, HEAD_DIM), lambda h, i: (0, ko + h)),
                  pl.BlockSpec((L, HEAD_DIM), lambda h, i: (0, vo + h)), blk, blk,
                  pl.BlockSpec((tk, tk), lambda h, i: (0, 0)), pl.BlockSpec((tk, tk), lambda h, i: (0, 0))],
        out_specs=[blk, full, full],
        out_shape=[jax.ShapeDtypeStruct((L, DA), BF16)] * 3,
        scratch_shapes=[pltpu.VMEM((L, HEAD_DIM), BF16), pltpu.VMEM((L, HEAD_DIM), BF16),
                        pltpu.VMEM((L, HEAD_DIM), F32), pltpu.VMEM((L, HEAD_DIM), F32),
                        pltpu.VMEM((tq, HEAD_DIM), F32), pltpu.VMEM((tq, 1), F32), pltpu.VMEM((tq, 1), F32)],
        vmem=VMEM_MID, args=(proj, proj, proj, o, do, _tri(tk, True), _tri(tk, False)),
        semantics=("arbitrary", "arbitrary"), copies=copies)


def _cmul(ar, ai, br, bi):
    return ar * br - ai * bi, ar * bi + ai * br


def _cmul_conj(ar, ai, br, bi):
    return ar * br + ai * bi, ar * bi - ai * br


def _ssm_disc(lr, li, ld):
    dt = jnp.exp(ld)
    m = jnp.exp(lr * dt)
    ar, ai = m * jnp.cos(li * dt), m * jnp.sin(li * dt)
    inv = 1.0 / (lr * lr + li * li)
    fr, fi = _cmul(ar - 1.0, ai, lr * inv, -li * inv)
    return dt, ar, ai, fr, fi, inv


def ssm_prep(name, lr, li, ld, br, bi):
    def body(lr_ref, li_ref, ld_ref, br_ref, bi_ref, zr_ref, zi_ref, bbr_ref, bbi_ref):
        dt, _, _, fr, fi, _ = _ssm_disc(lr_ref[...], li_ref[...], ld_ref[...])
        zr_ref[...] = lr_ref[...] * dt
        zi_ref[...] = li_ref[...] * dt
        bbr, bbi = _cmul(fr, fi, br_ref[...], bi_ref[...])
        bbr_ref[...] = bbr
        bbi_ref[...] = bbi

    sd = jax.ShapeDtypeStruct
    return pl.pallas_call(
        body, name=name,
        out_shape=[sd(lr.shape, F32), sd(lr.shape, F32), sd(br.shape, F32), sd(br.shape, F32)],
    )(lr, li, ld, br, bi)


def ssm_prep_bwd(name, lr, li, ld, br, bi, gar, gai, gbr, gbi):
    def body(lr_ref, li_ref, ld_ref, br_ref, bi_ref, gar_ref, gai_ref, gbr_ref, gbi_ref,
             dlr_ref, dli_ref, dld_ref, dbr_ref, dbi_ref):
        lr_, li_ = lr_ref[...], li_ref[...]
        dt, ar, ai, fr, fi, inv = _ssm_disc(lr_, li_, ld_ref[...])
        gbr_, gbi_ = gbr_ref[...], gbi_ref[...]
        dbr, dbi = _cmul_conj(fr, fi, gbr_, gbi_)
        dbr_ref[...] = dbr
        dbi_ref[...] = dbi
        pr, pi = _cmul_conj(br_ref[...], bi_ref[...], gbr_, gbi_)
        gfr = jnp.sum(pr, axis=1, keepdims=True)
        gfi = jnp.sum(pi, axis=1, keepdims=True)
        ilr, ili = lr_ * inv, -li_ * inv
        tr_, ti_ = _cmul_conj(ilr, ili, gfr, gfi)
        gatr, gati = gar_ref[...] + tr_, gai_ref[...] + ti_
        hr, hi = _cmul(fr, fi, ilr, ili)
        t1r, t1i = _cmul_conj(ar * dt, ai * dt, gatr, gati)
        t2r, t2i = _cmul_conj(hr, hi, gfr, gfi)
        dlr_ref[...] = t1r - t2r
        dli_ref[...] = t1i - t2i
        lar, lai = _cmul(lr_, li_, ar, ai)
        gdt, _ = _cmul_conj(lar, lai, gatr, gati)
        dld_ref[...] = jnp.sum(gdt, axis=2, keepdims=True) * dt

    sd = jax.ShapeDtypeStruct
    return pl.pallas_call(
        body, name=name,
        out_shape=[sd(lr.shape, F32), sd(lr.shape, F32), sd(ld.shape, F32), sd(br.shape, F32), sd(br.shape, F32)],
    )(lr, li, ld, br, bi, gar, gai, gbr, gbi)


SCAN_ROWS = 64


def _scan_rows(L):
    return min(SCAN_ROWS, L)


def _power_table(pr_s, pi_s, zr, zi, L, reverse):
    R = _scan_rows(L)
    row = lax.broadcasted_iota(jnp.int32, (R, 1), 0).astype(F32)
    dist = (R - row) if reverse else (row + 1.0)
    mag = jnp.exp(dist * zr)
    pr_s[...] = mag * jnp.cos(dist * zi)
    pi_s[...] = mag * jnp.sin(dist * zi)


def _scan(xr, xi, pr_s, pi_s, L, reverse):
    R = _scan_rows(L)
    nt = L // R
    assert L % R == 0 and R & (R - 1) == 0
    ns = CHUNK_X // LANE
    ridx = lax.broadcasted_iota(jnp.int32, (R, LANE), 0)

    def power(ref, d, cs):
        at = R - d if reverse else d - 1
        return ref[at:at + 1, cs]

    def shift(v, d):
        if d < 8:
            if reverse:
                return jnp.where(ridx < R - d, pltpu.roll(v, R - d, 0), 0.0)
            return jnp.where(ridx >= d, pltpu.roll(v, d, 0), 0.0)
        zeros = jnp.zeros((d, LANE), F32)
        return jnp.concatenate([v[d:], zeros], 0) if reverse else jnp.concatenate([zeros, v[:R - d]], 0)

    def tile(n, carry):
        t = nt - 1 - n if reverse else n
        rows = pl.ds(pl.multiple_of(t * R, R), R)
        edges = []
        for c in range(ns):
            cs = slice(c * LANE, (c + 1) * LANE)
            vr, vi = xr[rows, cs], xi[rows, cs]
            d = 1
            while d < R:
                ar, ai = power(pr_s, d, cs), power(pi_s, d, cs)
                sr, si = shift(vr, d), shift(vi, d)
                vr, vi = vr + ar * sr - ai * si, vi + ar * si + ai * sr
                d *= 2
            cr, ci = carry[2 * c], carry[2 * c + 1]
            pr, pi = pr_s[:, cs], pi_s[:, cs]
            vr, vi = vr + pr * cr - pi * ci, vi + pr * ci + pi * cr
            xr[rows, cs] = vr
            xi[rows, cs] = vi
            edge = slice(0, 1) if reverse else slice(R - 1, R)
            edges += [vr[edge], vi[edge]]
        return tuple(edges)

    lax.fori_loop(0, nt, tile, tuple(jnp.zeros((1, LANE), F32) for _ in range(2 * ns)))


def _gelu(x):
    t = jnp.tanh(0.7978845608028654 * (x + 0.044715 * x * x * x))
    return 0.5 * x * (1.0 + t)


def _gelu_grad(x):
    t = jnp.tanh(0.7978845608028654 * (x + 0.044715 * x * x * x))
    return 0.5 * (1.0 + t) + 0.5 * x * (1.0 - t * t) * 0.7978845608028654 * (1.0 + 0.134145 * x * x)


def _call(body, *, name, grid, in_specs, out_specs, out_shape, scratch_shapes, vmem, args, semantics,
          copies=None):
    n_i, n_o, n_s = len(in_specs), len(out_specs), len(scratch_shapes)
    if copies is None:
        out = pl.pallas_call(
            body, name=name, grid=grid, in_specs=in_specs, out_specs=out_specs, out_shape=out_shape,
            scratch_shapes=scratch_shapes, compiler_params=_cparams(semantics, vmem))(*args)
        return list(out), []
    n_ci, n_co = len(copies.inputs), len(copies.out_shape)

    def hosted(*refs):
        ins, cin = refs[:n_i], refs[n_i:n_i + n_ci]
        outs = refs[n_i + n_ci:n_i + n_ci + n_o]
        cout = refs[n_i + n_ci + n_o:n_i + n_ci + n_o + n_co]
        scr = refs[n_i + n_ci + n_o + n_co:n_i + n_ci + n_o + n_co + n_s]
        sems = refs[n_i + n_ci + n_o + n_co + n_s:]
        step = pl.program_id(0)
        for axis in range(1, len(grid)):
            step = step * grid[axis] + pl.program_id(axis)
        copies.emit(cin, cout, sems, step, math.prod(grid))
        body(*ins, *outs, *scr)

    out = pl.pallas_call(
        hosted, name=name, grid=grid, in_specs=list(in_specs) + [HBM] * n_ci,
        out_specs=list(out_specs) + [HBM] * n_co, out_shape=list(out_shape) + list(copies.out_shape),
        scratch_shapes=list(scratch_shapes) + list(copies.scratch),
        compiler_params=pltpu.CompilerParams(dimension_semantics=("arbitrary",) * len(grid),
                                             vmem_limit_bytes=vmem, has_side_effects=True))(*args, *copies.inputs)
    return list(out[:n_o]), list(out[n_o:])


def merge_copies(group):
    group = [c for c in group if c is not None]
    if len(group) <= 1:
        return group[0] if group else None
    bounds, i0, o0, s0 = [], 0, 0, 0
    for c in group:
        bounds.append((i0, o0, s0))
        i0, o0, s0 = i0 + len(c.inputs), o0 + len(c.out_shape), s0 + len(c.scratch)
    phases = []
    for c, (i, o, s) in zip(group, bounds):
        for frac, fn in c.phases:
            def shifted(ins, outs, sems, fn=fn, c=c, i=i, o=o, s=s):
                fn(ins[i:i + len(c.inputs)], outs[o:o + len(c.out_shape)], sems[s:s + len(c.scratch)])
            phases.append((frac, shifted))
    return HostedCopies([a for c in group for a in c.inputs], [a for c in group for a in c.out_shape],
                        [a for c in group for a in c.scratch], phases)


def ssm_fwd(name, proj, wbr, wbi, zr, zi, wcr, wci, dskip, D, copies=None):
    L = proj.shape[0]
    DS = D // 4
    NC = DS // CHUNK_U
    uo = (5 * D // 2) // CHUNK_U
    ch = _pick(L, 256)

    def body(u_ref, wbr_ref, wbi_ref, zr_ref, zi_ref, wcr_ref, wci_ref, ds_ref,
             y_ref, hg_ref, xr_ref, xi_ref, sr, si, pr_s, pi_s):
        def fill(ci, carry):
            rows = pl.ds(pl.multiple_of(ci * ch, ch), ch)
            ub = u_ref[rows, :].astype(BF16)
            sr[rows, :] = _dot(ub, wbr_ref[...])
            si[rows, :] = _dot(ub, wbi_ref[...])
            return carry

        lax.fori_loop(0, L // ch, fill, 0)
        _power_table(pr_s, pi_s, zr_ref[...], zi_ref[...], L, reverse=False)
        _scan(sr, si, pr_s, pi_s, L, reverse=False)

        def emit(ci, carry):
            rows = pl.ds(pl.multiple_of(ci * ch, ch), ch)
            xrb, xib = sr[rows, :].astype(BF16), si[rows, :].astype(BF16)
            xr_ref[rows, :] = xrb
            xi_ref[rows, :] = xib
            y = _dot(xrb, wcr_ref[...]) - _dot(xib, wci_ref[...]) + ds_ref[...] * u_ref[rows, :].astype(F32)
            y_ref[rows, :] = y
            hg_ref[rows, :] = _gelu(y).astype(BF16)
            return carry

        lax.fori_loop(0, L // ch, emit, 0)

    ucol = pl.BlockSpec((L, CHUNK_U), lambda k: (0, k))
    xcol = pl.BlockSpec((L, CHUNK_X), lambda k: (0, k))
    sd = jax.ShapeDtypeStruct
    return _call(
        body, name=name, grid=(NC,),
        in_specs=[pl.BlockSpec((L, CHUNK_U), lambda k: (0, uo + k)),
                  pl.BlockSpec((None, CHUNK_U, CHUNK_X), lambda k: (k, 0, 0)),
                  pl.BlockSpec((None, CHUNK_U, CHUNK_X), lambda k: (k, 0, 0)),
                  pl.BlockSpec((1, CHUNK_X), lambda k: (0, k)), pl.BlockSpec((1, CHUNK_X), lambda k: (0, k)),
                  pl.BlockSpec((None, CHUNK_X, CHUNK_U), lambda k: (k, 0, 0)),
                  pl.BlockSpec((None, CHUNK_X, CHUNK_U), lambda k: (k, 0, 0)),
                  pl.BlockSpec((1, CHUNK_U), lambda k: (0, k))],
        out_specs=[ucol, ucol, xcol, xcol],
        out_shape=[sd((L, DS), F32), sd((L, DS), BF16), sd((L, 4 * DS), BF16), sd((L, 4 * DS), BF16)],
        scratch_shapes=[pltpu.VMEM((L, CHUNK_X), F32), pltpu.VMEM((L, CHUNK_X), F32),
                        pltpu.VMEM((_scan_rows(L), CHUNK_X), F32), pltpu.VMEM((_scan_rows(L), CHUNK_X), F32)],
        vmem=VMEM_BIG, args=(proj, wbr, wbi, zr, zi, wcr, wci, dskip), semantics=("parallel",), copies=copies)


def ssm_bwd(name, dhg, ypre, proj, xr, xi, wbr, wbi, zr, zi, wcr, wci, dskip, D, copies=None):
    L = proj.shape[0]
    DS = D // 4
    NC = DS // CHUNK_U
    uo = (5 * D // 2) // CHUNK_U
    ch = _pick(L, 256)
    nch = L // ch
    halo = 16

    def body(dhg_ref, y_ref, u_ref, xr_ref, xi_ref, wbr_ref, wbi_ref, zr_ref, zi_ref, wcr_ref, wci_ref,
             ds_ref, du_ref, dwcr_ref, dwci_ref, dwbr_ref, dwbi_ref, dar_ref, dai_ref, dds_ref,
             gr, gi, duf, pr_s, pi_s):
        dwcr_ref[...] = jnp.zeros_like(dwcr_ref)
        dwci_ref[...] = jnp.zeros_like(dwci_ref)
        dwbr_ref[...] = jnp.zeros_like(dwbr_ref)
        dwbi_ref[...] = jnp.zeros_like(dwbi_ref)
        dar_ref[...] = jnp.zeros_like(dar_ref)
        dai_ref[...] = jnp.zeros_like(dai_ref)
        dds_ref[...] = jnp.zeros_like(dds_ref)

        def first(ci, carry):
            rows = pl.ds(pl.multiple_of(ci * ch, ch), ch)
            dy = dhg_ref[rows, :] * _gelu_grad(y_ref[rows, :])
            dyb = dy.astype(BF16)
            dds_ref[...] += jnp.sum(dy * u_ref[rows, :].astype(F32), axis=0, keepdims=True)
            duf[rows, :] = ds_ref[...] * dy
            gr[rows, :] = _dot(dyb, wcr_ref[...], NT)
            gi[rows, :] = -_dot(dyb, wci_ref[...], NT)
            dwcr_ref[...] += _dot(xr_ref[rows, :], dyb, TN)
            dwci_ref[...] -= _dot(xi_ref[rows, :], dyb, TN)
            return carry

        lax.fori_loop(0, nch, first, 0)
        _power_table(pr_s, pi_s, zr_ref[...], -zi_ref[...], L, reverse=True)
        _scan(gr, gi, pr_s, pi_s, L, reverse=True)

        def lam_grad(gxr, gxi, xpr, xpi):
            pr, pi = _cmul_conj(xpr, xpi, gxr, gxi)
            dar_ref[...] += jnp.sum(pr, axis=0, keepdims=True)
            dai_ref[...] += jnp.sum(pi, axis=0, keepdims=True)

        def second(ci, carry):
            r0 = pl.multiple_of(ci * ch, ch)
            rows = pl.ds(r0, ch)
            gxr, gxi = gr[rows, :], gi[rows, :]
            gxrb, gxib = gxr.astype(BF16), gxi.astype(BF16)
            du_ref[rows, :] = (duf[rows, :] + _dot(gxrb, wbr_ref[...], NT) + _dot(gxib, wbi_ref[...], NT)).astype(BF16)
            ub = u_ref[rows, :].astype(BF16)
            dwbr_ref[...] += _dot(ub, gxrb, TN)
            dwbi_ref[...] += _dot(ub, gxib, TN)
            return carry

        lax.fori_loop(0, nch, second, 0)

        ridx = lax.broadcasted_iota(jnp.int32, (ch, CHUNK_X), 0)
        xpr = jnp.where(ridx >= 1, pltpu.roll(xr_ref[0:ch, :].astype(F32), 1, 0), 0.0)
        xpi = jnp.where(ridx >= 1, pltpu.roll(xi_ref[0:ch, :].astype(F32), 1, 0), 0.0)
        lam_grad(gr[0:ch, :], gi[0:ch, :], xpr, xpi)

        def third(ci, carry):
            r0 = pl.multiple_of(ci * ch, ch)
            ext = pl.ds(pl.multiple_of(r0 - halo, halo), ch + halo)
            xpr = pltpu.roll(xr_ref[ext, :].astype(F32), 1, 0)[halo:]
            xpi = pltpu.roll(xi_ref[ext, :].astype(F32), 1, 0)[halo:]
            lam_grad(gr[pl.ds(r0, ch), :], gi[pl.ds(r0, ch), :], xpr, xpi)
            return carry

        if nch > 1:
            lax.fori_loop(1, nch, third, 0)

    ucol = pl.BlockSpec((L, CHUNK_U), lambda k: (0, k))
    xcol = pl.BlockSpec((L, CHUNK_X), lambda k: (0, k))
    wb_spec = pl.BlockSpec((None, CHUNK_U, CHUNK_X), lambda k: (k, 0, 0))
    wc_spec = pl.BlockSpec((None, CHUNK_X, CHUNK_U), lambda k: (k, 0, 0))
    avec = pl.BlockSpec((1, CHUNK_X), lambda k: (0, k))
    uvec = pl.BlockSpec((1, CHUNK_U), lambda k: (0, k))
    sd = jax.ShapeDtypeStruct
    return _call(
        body, name=name, grid=(NC,),
        in_specs=[ucol, ucol, pl.BlockSpec((L, CHUNK_U), lambda k: (0, uo + k)), xcol, xcol,
                  wb_spec, wb_spec, avec, avec, wc_spec, wc_spec, uvec],
        out_specs=[ucol, wc_spec, wc_spec, wb_spec, wb_spec, avec, avec, uvec],
        out_shape=[sd((L, DS), BF16), sd((NC, CHUNK_X, CHUNK_U), F32), sd((NC, CHUNK_X, CHUNK_U), F32),
                   sd((NC, CHUNK_U, CHUNK_X), F32), sd((NC, CHUNK_U, CHUNK_X), F32),
                   sd((1, 4 * DS), F32), sd((1, 4 * DS), F32), sd((1, DS), F32)],
        scratch_shapes=[pltpu.VMEM((L, CHUNK_X), F32), pltpu.VMEM((L, CHUNK_X), F32), pltpu.VMEM((L, CHUNK_U), F32),
                        pltpu.VMEM((_scan_rows(L), CHUNK_X), F32), pltpu.VMEM((_scan_rows(L), CHUNK_X), F32)],
        vmem=VMEM_BIG, args=(dhg, ypre, proj, xr, xi, wbr, wbi, zr, zi, wcr, wci, dskip),
        semantics=("parallel",), copies=copies)


def _block_diag(w, transpose):
    G = w.shape[0]
    nc = G // GROUPS_PER_CHUNK
    w4 = w.reshape(nc, GROUPS_PER_CHUNK, SSM_GROUP, SSM_STATE)
    eye = jnp.eye(GROUPS_PER_CHUNK, dtype=w.dtype)
    if transpose:
        return (w4[:, None, :, :, :].transpose(0, 1, 4, 2, 3) * eye[None, :, None, :, None]).reshape(
            nc, CHUNK_X, CHUNK_U).astype(BF16)
    return (w4[:, :, :, None, :] * eye[None, :, None, :, None]).reshape(nc, CHUNK_U, CHUNK_X).astype(BF16)


def _diag_blocks(dw, transpose):
    nc = dw.shape[0]
    gpc = GROUPS_PER_CHUNK
    eye = jnp.eye(gpc, dtype=dw.dtype)
    if transpose:
        d5 = dw.reshape(nc, gpc, SSM_STATE, gpc, SSM_GROUP)
        kept = jnp.sum(d5 * eye[None, :, None, :, None], axis=1)
        return kept.transpose(0, 2, 3, 1).reshape(nc * gpc, SSM_GROUP, SSM_STATE)
    d5 = dw.reshape(nc, gpc, SSM_GROUP, gpc, SSM_STATE)
    kept = jnp.sum(d5 * eye[None, :, None, :, None], axis=3)
    return kept.reshape(nc * gpc, SSM_GROUP, SSM_STATE)


SHARD_BLOCK_ELEMS = 128 * 1024


def _shard_rows(R, C, scale):
    return _pick(R, max(8, scale * SHARD_BLOCK_ELEMS // C))


def cast_bf16(name, w, layer, parts=1):
    shape = w.shape[1:]
    w3 = w.reshape(w.shape[0], -1, shape[-1])
    _, R, C = w3.shape
    tr = _shard_rows(R, C, 4)
    cw = C // parts

    def body(w_ref, *o_refs):
        for g, o_ref in enumerate(o_refs):
            o_ref[...] = w_ref[:, g * cw:(g + 1) * cw].astype(BF16)

    out = pl.pallas_call(body, name=name, grid=(R // tr,),
                         in_specs=[pl.BlockSpec((None, tr, C), lambda i: (layer, i, 0))],
                         out_specs=[pl.BlockSpec((tr, cw), lambda i: (i, 0))] * parts,
                         out_shape=[jax.ShapeDtypeStruct((R, cw), BF16)] * parts,
                         compiler_params=_cparams(("parallel",), VMEM_MID))(w3)
    return [o.reshape(shape[:-1] + (cw,)) for o in out]


def _adamw(w, g, m, v):
    m = ADAM_B1 * m + (1.0 - ADAM_B1) * g
    v = ADAM_B2 * v + (1.0 - ADAM_B2) * (g * g)
    delta = -ADAM_LR * ((m * ADAM_C1) / (jnp.sqrt(v * ADAM_C2) + ADAM_EPS) + ADAM_WD * w)
    return delta, m, v


def _own_core(g4):
    return (lambda p: p[0]) if g4.shape[1] == 2 else (lambda p: 0)


def chip_partial(name, pos, g4, recv_a):
    _, _, R, C = g4.shape
    tr = _shard_rows(R, C, 4)
    core = _own_core(g4)

    def body(pos_ref, g_ref, a_ref, o_ref):
        o_ref[...] = (g_ref[...] + a_ref[...]).astype(BF16)

    return pl.pallas_call(
        body, name=name,
        grid_spec=pltpu.PrefetchScalarGridSpec(
            num_scalar_prefetch=1, grid=(4, R // tr),
            in_specs=[pl.BlockSpec((None, None, tr, C), lambda q, i, p: (q, core(p), i, 0)),
                      pl.BlockSpec((None, tr, C), lambda q, i, p: (q, i, 0))],
            out_specs=pl.BlockSpec((None, tr, C), lambda q, i, p: (q, i, 0))),
        out_shape=jax.ShapeDtypeStruct((4, R, C), BF16),
        compiler_params=_cparams(("parallel", "parallel"), VMEM_MID))(pos, g4, recv_a)


def adamw_shard(name, pos, layer, g4, recv_a, recv_b, w, m, v, prev, copies=None):
    _, _, R, C = g4.shape
    tr = _shard_rows(R, C, 1)
    n_prev = 0 if prev is None else 4
    n_ci, n_co = (len(copies.inputs), len(copies.out_shape)) if copies else (0, 0)
    core = _own_core(g4)

    def body(pos_ref, g_ref, a_ref, b_ref, w_ref, m_ref, v_ref, *rest):
        rest = rest[n_prev:]
        go_ref, d_ref, mo_ref, vo_ref = rest[n_ci:n_ci + 4]
        if copies is not None:
            copies.emit(rest[:n_ci], rest[n_ci + 4:n_ci + 4 + n_co], rest[n_ci + 4 + n_co:],
                        pl.program_id(0), R // tr)
        gs = g_ref[...] + a_ref[...]
        for j in range(3):
            gs = gs + b_ref[j].astype(F32)
        delta, mn, vn = _adamw(w_ref[...], gs, m_ref[...], v_ref[...])
        go_ref[...] = gs
        d_ref[...] = delta
        mo_ref[...] = mn
        vo_ref[...] = vn

    lay = pl.BlockSpec((None, tr, C), lambda i, p: (layer, i, 0))
    in_specs = [pl.BlockSpec((None, None, tr, C), lambda i, p: (p[1], core(p), i, 0)),
                pl.BlockSpec((None, tr, C), lambda i, p: (p[1], i, 0)),
                pl.BlockSpec((3, tr, C), lambda i, p: (0, i, 0)), lay, lay, lay]
    args = [g4, recv_a, recv_b, w, m, v]
    aliases = {}
    if prev is not None:
        in_specs += [pl.BlockSpec(memory_space=pl.ANY)] * 4
        args += list(prev)
        aliases = {7 + j: j for j in range(4)}
    out = pl.pallas_call(
        body, name=name,
        grid_spec=pltpu.PrefetchScalarGridSpec(
            num_scalar_prefetch=1, grid=(R // tr,), in_specs=in_specs + [HBM] * n_ci,
            out_specs=[lay] * 4 + [HBM] * n_co, scratch_shapes=list(copies.scratch) if copies else []),
        out_shape=[jax.ShapeDtypeStruct(w.shape, F32)] * 4 + list(copies.out_shape if copies else []),
        input_output_aliases=aliases,
        compiler_params=pltpu.CompilerParams(
            dimension_semantics=("arbitrary",) if copies else ("parallel",), vmem_limit_bytes=VMEM_MID,
            has_side_effects=copies is not None))(pos, *args, *(copies.inputs if copies else []))
    return list(out[:4]), list(out[4:])


def adamw_small(name, gathered, w, m, v):
    _, R, C = gathered.shape
    tr = _pick(R, 512)

    def body(g_ref, w_ref, m_ref, v_ref, go_ref, d_ref, mo_ref, vo_ref):
        gs = g_ref[0]
        for j in range(1, N_DEV):
            gs = gs + g_ref[j]
        delta, mn, vn = _adamw(w_ref[...], gs, m_ref[...], v_ref[...])
        go_ref[...] = gs
        d_ref[...] = delta
        mo_ref[...] = mn
        vo_ref[...] = vn

    spec = pl.BlockSpec((tr, C), lambda i: (i, 0))
    return pl.pallas_call(
        body, name=name, grid=(R // tr,),
        in_specs=[pl.BlockSpec((N_DEV, tr, C), lambda i: (0, i, 0)), spec, spec, spec], out_specs=[spec] * 4,
        out_shape=[jax.ShapeDtypeStruct((R, C), F32)] * 4,
        compiler_params=_cparams(("parallel",), VMEM_MID))(gathered, w, m, v)


def _position():
    return lax.axis_index("x"), lax.axis_index("y"), lax.axis_index("c")


FORWARD_AT = 0.88


def gather_copies(shards):
    n = len(shards)

    def parts(ins, outs, sems):
        send_sems, recv_sems, local_sems = sems
        x, y, c = _position()
        me, sibling = (x, y, c), (x, y, 1 - c)
        chips = [(1 - x, y), (x, 1 - y), (1 - x, 1 - y)]

        def copy(a, k, block, to, src=None):
            blk = outs[a].at[4 * block[0] + 2 * block[1] + block[2]]
            return pltpu.make_async_remote_copy(
                src_ref=blk if src is None else src, dst_ref=blk,
                send_sem=send_sems.at[a, k], recv_sem=recv_sems.at[a, k], device_id=to, device_id_type=MESH)

        mine = [pltpu.make_async_copy(ins[a], outs[a].at[4 * x + 2 * y + c], local_sems.at[a]) for a in range(n)]
        first = [[copy(a, 0, me, sibling, src=ins[a])] +
                 [copy(a, 1 + j, me, (*chip, c), src=ins[a]) for j, chip in enumerate(chips)] for a in range(n)]
        landed = [[copy(a, 1 + j, (*chip, c), me) for j, chip in enumerate(chips)] for a in range(n)]
        passed = [[copy(a, 4 + j, (*chip, c), sibling) for j, chip in enumerate(chips)] for a in range(n)]
        from_sibling = [[copy(a, 0, sibling, me)] +
                        [copy(a, 4 + j, (*chip, 1 - c), me) for j, chip in enumerate(chips)] for a in range(n)]
        return mine, first, landed, passed, from_sibling

    def start(ins, outs, sems):
        mine, first, _, _, _ = parts(ins, outs, sems)
        for a in range(n):
            mine[a].start()
            for cp in first[a]:
                cp.start()

    def forward(ins, outs, sems):
        _, _, landed, passed, _ = parts(ins, outs, sems)
        for a in range(n):
            for j in range(3):
                landed[a][j].wait_recv()
                passed[a][j].start()

    def finish(ins, outs, sems):
        mine, first, _, passed, from_sibling = parts(ins, outs, sems)
        for a in range(n):
            for cp in from_sibling[a]:
                cp.wait_recv()
        for a in range(n):
            for cp in first[a] + passed[a]:
                cp.wait_send()
            mine[a].wait()

    return HostedCopies(
        list(shards), [jax.ShapeDtypeStruct((N_DEV,) + s.shape, s.dtype) for s in shards],
        [pltpu.SemaphoreType.DMA((n, 7)), pltpu.SemaphoreType.DMA((n, 7)), pltpu.SemaphoreType.DMA((n,))],
        [(0.0, start), (FORWARD_AT, forward), (1.0, finish)])


def _exchange_copies(arrays, out_lead, make):
    n = len(arrays)

    def all_copies(ins, outs, sems):
        send_sems, recv_sems = sems
        return [make(ins[a], outs[a], send_sems.at[a, k], recv_sems.at[a, k], k)
                for a in range(n) for k in range(out_lead)]

    def start(ins, outs, sems):
        for cp in all_copies(ins, outs, sems):
            cp.start()

    def finish(ins, outs, sems):
        for cp in all_copies(ins, outs, sems):
            cp.wait()

    return HostedCopies(
        list(arrays), [jax.ShapeDtypeStruct((out_lead,) + a.shape[2:], a.dtype) for a in arrays],
        [pltpu.SemaphoreType.DMA((n, out_lead)), pltpu.SemaphoreType.DMA((n, out_lead))],
        [(0.0, start), (1.0, finish)])


def sibling_copies(grads):
    def make(src, dst, send_sem, recv_sem, q):
        x, y, c = _position()
        core = 1 - c if src.shape[1] == 2 else 0
        return pltpu.make_async_remote_copy(
            src_ref=src.at[q, core], dst_ref=dst.at[q], send_sem=send_sem, recv_sem=recv_sem,
            device_id=(x, y, 1 - c), device_id_type=MESH)

    return _exchange_copies(grads, 4, make)


def chip_copies(parts):
    def make(src, dst, send_sem, recv_sem, j):
        x, y, c = _position()
        chip = [(1 - x, y), (x, 1 - y), (1 - x, 1 - y)][j]
        return pltpu.make_async_remote_copy(
            src_ref=src.at[2 * chip[0] + chip[1], 0], dst_ref=dst.at[j], send_sem=send_sem, recv_sem=recv_sem,
            device_id=(*chip, c), device_id_type=MESH)

    return _exchange_copies(parts, 3, make)


class Carrier:
    def __init__(self):
        self.plan = {}
        self.counts = {}

    def ride(self, site, make, store):
        self.plan.setdefault(site, []).append((make, store))

    def make(self, site, ctx=None):
        if site not in self.plan:
            return None
        group = [make(ctx) for make, _ in self.plan[site]]
        self.counts[site] = [len(c.out_shape) for c in group]
        return merge_copies(group)

    def store(self, site, results):
        if site in self.plan:
            at = 0
            for (_, store), n in zip(self.plan[site], self.counts[site]):
                store(results[at:at + n])
                at += n

    def split(self, site, out):
        if site not in self.plan:
            return out
        self.store(site, out[1])
        return out[0]


def _pool_weight(gathered):
    PG = gathered.shape[-1]
    return gathered.transpose(1, 0, 2, 3).reshape(N_POOL_GROUPS, PG, PG)


def _layer_params(l, ln_g, pool_scale, lam_re, lam_im, log_dt, b_re, b_im, c_re, c_im,
                  d_skip, b_glu, branch_g):
    G, P = lam_re.shape[1:]
    p = dict(
        ln_g=ln_g[l][None, :], pool_scale=pool_scale[l][None, :], d_skip=d_skip[l][None, :],
        b_glu=b_glu[l][None, :], branch_g=branch_g[l][None, :],
        lr=lam_re[l].reshape(G, 1, P), li=lam_im[l].reshape(G, 1, P), ld=log_dt[l].reshape(G, 1, 1),
        br=b_re[l].transpose(0, 2, 1), bi=b_im[l].transpose(0, 2, 1), cr=c_re[l], ci=c_im[l])
    return p


def layer_fwd(l, x, p, gw, D, carrier, staged_parts=0):
    t = f"l{l}_"
    h = rms_fwd(t + "rms_fwd", x, p["ln_g"])
    if staged_parts:
        proj = None
        for g in range(staged_parts):
            site = (l, f"proj{g}")
            proj = carrier.split(site, mm_nn_part(t + f"proj{g}", h, gw("w_in", l, g), g, staged_parts,
                                                  into=proj, out_dtype=BF16, copies=carrier.make(site)))
    else:
        site = (l, "proj")
        proj = carrier.split(site, mm_nn_gathered(t + "proj", h, gw("w_in", l), out_dtype=BF16,
                                                  copies=carrier.make(site)))
    wp = _pool_weight(gw("w_pool", l)[0])
    ypool = pool_fwd(t + "pool_fwd", proj, wp, p["pool_scale"])
    site = (l, "attn_fwd")
    yattn, landed = attn_fwd(t + "attn_fwd", proj, D, copies=carrier.make(site))
    carrier.store(site, landed)
    zr, zi, bbr, bbi = ssm_prep(t + "ssm_prep", p["lr"], p["li"], p["ld"], p["br"], p["bi"])
    ssm_w = dict(wbr=_block_diag(bbr, False), wbi=_block_diag(bbi, False),
                 zr=zr.reshape(1, -1), zi=zi.reshape(1, -1),
                 wcr=_block_diag(p["cr"], True), wci=_block_diag(p["ci"], True))
    site = (l, "ssm_fwd")
    (ypre, hg, xr, xi), landed = ssm_fwd(
        t + "ssm_fwd", proj, ssm_w["wbr"], ssm_w["wbi"], ssm_w["zr"], ssm_w["zi"],
        ssm_w["wcr"], ssm_w["wci"], p["d_skip"], D, copies=carrier.make(site))
    carrier.store(site, landed)
    glu_pre = mm_nn_gathered(t + "glu", hg, gw("w_glu", l))
    y = branch_fwd(t + "branch_fwd", ypool, yattn, glu_pre, proj, p["b_glu"], p["branch_g"])
    out = mm_plain(t + "out", y, gw("w_out", l)[0].reshape(D, D), NN, res=x)
    saved = dict(x=x, h=h, proj=proj, ypool=ypool, yattn=yattn, ypre=ypre, hg=hg, xr=xr, xi=xi,
                 glu_pre=glu_pre, y=y, ssm_w=ssm_w, wp=wp)
    return out, saved


def layer_bwd(l, dres, dres_b, s, p, gw, D, carrier, pos, split_w_in):
    t = f"l{l}_"
    proj = s["proj"]

    def by_target(g):
        return g.reshape(4, 2, -1, g.shape[-1])

    big = {}
    w_out_g = gw("w_out", l)[0].reshape(D, D)
    site = (l, "dy")
    dy = carrier.split(site, mm_plain(t + "dy", dres_b, w_out_g, NT, copies=carrier.make(site)))
    big["w_out"] = by_target(mm_plain(t + "dw_out", s["y"], dres_b, TN).reshape(N_DEV, D // N_DEV, D))
    dypool, dyattn, dglu, dpg, dag, dsg, dbg, dbglu = branch_bwd(
        t + "branch_bwd", dy, s["ypool"], s["yattn"], s["glu_pre"], proj, p["b_glu"], p["branch_g"])
    dhg = mm_nt_gathered(t + "dhg", dglu, gw("w_glu", l))
    big["w_glu"] = by_target(mm_tn_scattered(t + "dw_glu", s["hg"], dglu))
    w = s["ssm_w"]
    site = (l, "ssm_bwd")
    (du, dwcr, dwci, dwbr, dwbi, dar, dai, dds), landed = ssm_bwd(
        t + "ssm_bwd", dhg, s["ypre"], proj, s["xr"], s["xi"], w["wbr"], w["wbi"], w["zr"], w["zi"],
        w["wcr"], w["wci"], p["d_skip"], D, copies=carrier.make(site, big))
    carrier.store(site, landed)
    G, _, P = p["lr"].shape
    dlr, dli, dld, dbr, dbi = ssm_prep_bwd(
        t + "ssm_prep_bwd", p["lr"], p["li"], p["ld"], p["br"], p["bi"],
        dar.reshape(G, 1, P), dai.reshape(G, 1, P), _diag_blocks(dwbr, False), _diag_blocks(dwbi, False))
    site = (l, "attn_bwd")
    (dq, dk, dv), landed = attn_bwd(t + "attn_bwd", proj, s["yattn"], dyattn, D, copies=carrier.make(site, big))
    carrier.store(site, landed)
    dxp, dwp, dps = pool_bwd(t + "pool_bwd", dypool, proj, s["wp"], p["pool_scale"])
    dproj = jnp.concatenate([dxp, dpg, dq, dk, dv, dag, du, dsg], axis=1)
    PG = dwp.shape[1]
    big["w_pool"] = by_target(dwp.reshape(N_POOL_GROUPS, N_DEV, PG // N_DEV, PG).transpose(1, 0, 2, 3))
    if split_w_in:
        site = (l, "dw_in_a")
        to_sibling = carrier.split(site, mm_tn_half(t + "dw_in_a", s["h"], dproj, pos, False,
                                                    copies=carrier.make(site, big)))
        site = (l, "dw_in_b")
        mine = carrier.split(site, mm_tn_half(t + "dw_in_b", s["h"], dproj, pos, True,
                                              copies=carrier.make(site, dict(big, to_sibling=to_sibling[:, None]))))
        big["w_in"] = mine[:, None]
    else:
        big["w_in"] = by_target(mm_tn_scattered(t + "dw_in", s["h"], dproj))
    site = (l, "dh")
    dh = carrier.split(site, mm_nt_gathered(t + "dh", dproj, gw("w_in", l), copies=carrier.make(site, big)))
    dx, dx_b, dlng = rms_bwd(t + "rms_bwd", s["x"], dh, dres, p["ln_g"])
    small = dict(ln_g=dlng[0], pool_scale=dps[0], lam_re=dlr.reshape(G, P), lam_im=dli.reshape(G, P),
                 log_dt=dld.reshape(G), b_re=dbr.transpose(0, 2, 1), b_im=dbi.transpose(0, 2, 1),
                 c_re=_diag_blocks(dwcr, True), c_im=_diag_blocks(dwci, True),
                 d_skip=dds[0], b_glu=dbglu[0], branch_g=dbg[0])
    return dx, dx_b, small


SMALL_NAMES = ("ln_g", "pool_scale", "lam_re", "lam_im", "log_dt", "b_re", "b_im", "c_re", "c_im",
               "d_skip", "b_glu", "branch_g", "final_g")
BIG_NAMES = ("w_in", "w_pool", "w_glu", "w_out")
WEIGHT_ORDER = ("ln_g", "w_in", "w_pool", "pool_scale", "lam_re", "lam_im", "log_dt", "b_re", "b_im",
                "c_re", "c_im", "d_skip", "w_glu", "b_glu", "branch_g", "w_out", "final_g")


PACK_ROWS = 512


def _pack(arrs):
    flat = jnp.concatenate([a.reshape(-1) for a in arrs])
    pad = (-flat.shape[0]) % (PACK_ROWS * LANE)
    return jnp.pad(flat, (0, pad)).reshape(-1, LANE)


def _unpack(packed, like):
    flat = packed.reshape(-1)
    out, off = [], 0
    for a in like:
        out.append(flat[off:off + a.size].reshape(a.shape))
        off += a.size
    return out


def kernel(x, ln_g, w_in, w_pool, pool_scale, lam_re, lam_im, log_dt, b_re, b_im, c_re, c_im, d_skip, w_glu, b_glu, branch_g, w_out, final_g, loss_target, m_ln_g, m_w_in, m_w_pool, m_pool_scale, m_lam_re, m_lam_im, m_log_dt, m_b_re, m_b_im, m_c_re, m_c_im, m_d_skip, m_w_glu, m_b_glu, m_branch_g, m_w_out, m_final_g, v_ln_g, v_w_in, v_w_pool, v_pool_scale, v_lam_re, v_lam_im, v_log_dt, v_b_re, v_b_im, v_c_re, v_c_im, v_d_skip, v_w_glu, v_b_glu, v_branch_g, v_w_out, v_final_g):
    W = dict(ln_g=ln_g, w_in=w_in, w_pool=w_pool, pool_scale=pool_scale, lam_re=lam_re, lam_im=lam_im,
             log_dt=log_dt, b_re=b_re, b_im=b_im, c_re=c_re, c_im=c_im, d_skip=d_skip, w_glu=w_glu,
             b_glu=b_glu, branch_g=branch_g, w_out=w_out, final_g=final_g)
    Mo = dict(ln_g=m_ln_g, w_in=m_w_in, w_pool=m_w_pool, pool_scale=m_pool_scale, lam_re=m_lam_re,
              lam_im=m_lam_im, log_dt=m_log_dt, b_re=m_b_re, b_im=m_b_im, c_re=m_c_re, c_im=m_c_im,
              d_skip=m_d_skip, w_glu=m_w_glu, b_glu=m_b_glu, branch_g=m_branch_g, w_out=m_w_out,
              final_g=m_final_g)
    Vo = dict(ln_g=v_ln_g, w_in=v_w_in, w_pool=v_w_pool, pool_scale=v_pool_scale, lam_re=v_lam_re,
              lam_im=v_lam_im, log_dt=v_log_dt, b_re=v_b_re, b_im=v_b_im, c_re=v_c_re, c_im=v_c_im,
              d_skip=v_d_skip, w_glu=v_w_glu, b_glu=v_b_glu, branch_g=v_branch_g, w_out=v_w_out,
              final_g=v_final_g)
    depth = ln_g.shape[0]
    _, L, D = x.shape
    xc, yc, cc = _position()
    pos = jnp.stack([cc, 2 * xc + yc, 4 * xc + 2 * yc + cc]).astype(jnp.int32)

    def n_parts(n, l):
        return (W_IN_PARTS if l > 0 else W_IN0_PARTS) if n == "w_in" else 1

    shards, landed = {}, {}
    for n in BIG_NAMES:
        for l in range(depth):
            for g, part in enumerate(cast_bf16(f"cast_{n}_{l}", W[n], l, n_parts(n, l))):
                shards[n, l, g] = part
    carrier = Carrier()

    def gw(n, l, g=None):
        return landed[n, l, g] if g is not None else [landed[n, l, i] for i in range(n_parts(n, l))]

    def gather_plan(keys):
        return (lambda ctx: gather_copies([shards[k] for k in keys])), (lambda outs: landed.update(zip(keys, outs)))

    first = [("w_in", 0, 0)] + [("w_pool", l, 0) for l in range(depth)]
    landed.update(zip(first, copies_call("gather_first", gather_copies([shards[k] for k in first]))))
    for g in range(1, W_IN0_PARTS):
        carrier.ride((0, f"proj{g - 1}"), *gather_plan([("w_in", 0, g)]))
    carrier.ride((0, f"proj{W_IN0_PARTS - 1}"), *gather_plan([("w_out", 0, 0), ("w_glu", 0, 0)]))
    for l in range(1, depth):
        for g in range(W_IN_PARTS):
            call = "ssm_fwd" if g == W_IN_PARTS - 1 else "attn_fwd"
            carrier.ride((l - 1, call), *gather_plan([("w_in", l, g)]))
        carrier.ride((l, "proj"), *gather_plan([("w_out", l, 0), ("w_glu", l, 0)]))

    own, recv_a, recv_b = {}, {}, {}

    def sibling_plan(l, names, keep, pick):
        def make(ctx):
            own.update({(n, l): ctx[n] for n in keep})
            return sibling_copies(pick(ctx))
        return make, (lambda outs: recv_a.update(zip([(n, l) for n in names], outs)))

    def chip_plan(l, names):
        def make(ctx):
            parts = [chip_partial(f"chip_partial_{n}_{l}", pos, own[n, l], recv_a[n, l])[:, None] for n in names]
            return chip_copies(parts)
        return make, (lambda outs: recv_b.update(zip([(n, l) for n in names], outs)))

    early, late = ("w_out", "w_glu"), ("w_in", "w_pool")
    for l in range(1, depth):
        carrier.ride((l, "dh"), *sibling_plan(l, BIG_NAMES, BIG_NAMES, lambda big: [big[n] for n in BIG_NAMES]))
        carrier.ride((l - 1, "ssm_bwd"), *chip_plan(l, ("w_out", "w_glu", "w_pool")))
        carrier.ride((l - 1, "attn_bwd"), *chip_plan(l, ("w_in",)))
    carrier.ride((0, "ssm_bwd"), *sibling_plan(0, early, early, lambda big: [big[n] for n in early]))
    carrier.ride((0, "dw_in_a"), *chip_plan(0, early))
    carrier.ride((0, "dw_in_b"), *sibling_plan(0, late, ("w_pool",), lambda ctx: [ctx["to_sibling"], ctx["w_pool"]]))

    def last_chip_make(big):
        own["w_in", 0] = big["w_in"]
        return chip_plan(0, late)[0](big)

    carrier.ride((0, "dh"), last_chip_make, chip_plan(0, late)[1])

    params = [_layer_params(l, ln_g, pool_scale, lam_re, lam_im, log_dt, b_re, b_im, c_re, c_im,
                            d_skip, b_glu, branch_g) for l in range(depth)]
    h = x[0]
    saved = []
    for l in range(depth):
        h, s = layer_fwd(l, h, params[l], gw, D, carrier, staged_parts=W_IN0_PARTS if l == 0 else 0)
        saved.append(s)
    loss_part, dres, dres_b, dfinal = loss_head("loss_head", h, final_g[None, :], loss_target[0])
    loss = lax.psum(loss_part[0, 0], ("x", "y", "c"))

    small = [None] * depth
    for l in reversed(range(depth)):
        dres, dres_b, small[l] = layer_bwd(l, dres, dres_b, saved[l], params[l], gw, D, carrier, pos,
                                           split_w_in=(l == 0))
    grad_x = dres[None]

    small_like = [W[n] for n in SMALL_NAMES]
    small_grads = [jnp.stack([small[l][n] for l in range(depth)]) for n in SMALL_NAMES[:-1]] + [dfinal[0]]
    small_gather = gather_copies([_pack(small_grads)])
    gathered = None
    results = {}
    for n in BIG_NAMES:
        shape = W[n].shape
        R, C = int(math.prod(shape[1:-1])), shape[-1]
        w3, m3, v3 = (t.reshape(depth, R, C) for t in (W[n], Mo[n], Vo[n]))
        prev = None
        for l in range(depth):
            prev, landed_small = adamw_shard(
                f"adamw_{n}_{l}", pos, l, own[n, l], recv_a[n, l], recv_b[n, l], w3, m3, v3, prev,
                copies=small_gather if gathered is None else None)
            gathered = landed_small[0] if gathered is None else gathered
        results[n] = [t.reshape(shape) for t in prev]

    packed = adamw_small("adamw_small", gathered, _pack(small_like), _pack([Mo[n] for n in SMALL_NAMES]),
                         _pack([Vo[n] for n in SMALL_NAMES]))
    unpacked = [_unpack(t, small_like) for t in packed]
    for i, n in enumerate(SMALL_NAMES):
        results[n] = [unpacked[j][i] for j in range(4)]

    out = [loss, grad_x]
    for j in range(4):
        out += [results[n][j] for n in WEIGHT_ORDER]
    return tuple(out)
```

```python
import functools
import math

import jax
import jax.numpy as jnp
from jax import lax
from jax.experimental import pallas as pl
from jax.experimental.pallas import tpu as pltpu

F32 = jnp.float32
BF16 = jnp.bfloat16
MESH = pl.DeviceIdType.MESH

EPS = 1e-6
HEAD_DIM = 128
SSM_GROUP = 16
SSM_STATE = 64
GROUPS_PER_CHUNK = 8
CHUNK_U = GROUPS_PER_CHUNK * SSM_GROUP
CHUNK_X = GROUPS_PER_CHUNK * SSM_STATE
N_POOL_GROUPS = 4
POOL_HALO = 16
N_DEV = 8
LANE = 128
FULL_K = 4096
ATTN_TILE = 256
ATTN_DECAY_CUTOFF = 100.0
ROW_TILE = 128
VMEM_BIG = 58 * 1024 * 1024
VMEM_MID = 40 * 1024 * 1024

ADAM_LR = 0.001
ADAM_B1 = 0.9
ADAM_B2 = 0.999
ADAM_EPS = 1e-08
ADAM_WD = 0.01
ADAM_STEP = 10
ADAM_C1 = 1.0 / (1.0 - ADAM_B1 ** ADAM_STEP)
ADAM_C2 = 1.0 / (1.0 - ADAM_B2 ** ADAM_STEP)

NN = (((1,), (0,)), ((), ()))
NT = (((1,), (1,)), ((), ()))
TN = (((0,), (0,)), ((), ()))


def _pick(n, cap):
    if n <= cap:
        return n
    step = LANE if cap >= LANE else 8
    t = (cap // step) * step
    while t > step and n % t:
        t -= step
    assert n % t == 0, (n, cap)
    return t


def _cparams(sem, vmem=None):
    return pltpu.CompilerParams(dimension_semantics=sem, vmem_limit_bytes=vmem)


def _dot(a, b, dn=NN):
    return lax.dot_general(a, b, dn, preferred_element_type=F32)


def _sigmoid(x):
    e = jnp.exp(-jnp.abs(x))
    r = 1.0 / (1.0 + e)
    return jnp.where(x >= 0, r, e * r)


HBM = pl.BlockSpec(memory_space=pl.ANY)


class HostedCopies:
    def __init__(self, inputs, out_shape, scratch, phases):
        self.inputs, self.out_shape, self.scratch, self.phases = inputs, out_shape, scratch, phases

    def emit(self, ins, outs, sems, step, total):
        plan = {}
        for frac, fn in self.phases:
            plan.setdefault(min(total - 1, int(frac * total)), []).append(fn)
        for s in sorted(plan):
            def run(fns=plan[s]):
                for fn in fns:
                    fn(ins, outs, sems)
            if total == 1:
                run()
            else:
                pl.when(step == s)(run)


def copies_call(name, copies):
    n_i, n_o = len(copies.inputs), len(copies.out_shape)

    def body(*refs):
        copies.emit(refs[:n_i], refs[n_i:n_i + n_o], refs[n_i + n_o:], 0, 1)

    return pl.pallas_call(
        body, name=name, in_specs=[HBM] * n_i, out_specs=[HBM] * n_o, out_shape=copies.out_shape,
        scratch_shapes=copies.scratch, compiler_params=pltpu.CompilerParams(has_side_effects=True),
    )(*copies.inputs)


def _matmul(name, a, b, *, grid, a_spec, b_spec, o_spec, out_shape, dn,
            res=None, res_spec=None, pos=None, copies=None, product=None, into=None):
    ni, nj, nk = grid
    bs, b_specs = (list(b), list(b_spec)) if isinstance(b, (list, tuple)) else ([b], [b_spec])
    n_b = len(bs)
    n_pos = 0 if pos is None else 1
    n_res = 0 if res is None else 1
    n_into = 0 if into is None else 1
    n_ci = 0 if copies is None else len(copies.inputs)
    n_co = 0 if copies is None else len(copies.out_shape)

    def body(*refs):
        refs = refs[n_pos:]
        a_ref, b_refs = refs[0], refs[1:1 + n_b]
        r_ref = refs[1 + n_b] if n_res else None
        base = 1 + n_b + n_res + n_into
        cin = refs[base:base + n_ci]
        o_ref = refs[base + n_ci]
        cout = refs[base + n_ci + 1:base + n_ci + 1 + n_co]
        sems = refs[base + n_ci + 1 + n_co:]
        k = pl.program_id(2)
        if copies is not None:
            step = (pl.program_id(0) * nj + pl.program_id(1)) * nk + k
            copies.emit(cin, cout, sems, step, ni * nj * nk)

        if product is None:
            part = _dot(a_ref[...].astype(BF16), b_refs[0][...].astype(BF16), dn)
        else:
            part = product(a_ref, b_refs)
        if nk == 1:
            if r_ref is not None:
                part = part + r_ref[...]
            o_ref[...] = part.astype(o_ref.dtype)
        else:
            @pl.when(k == 0)
            def _():
                o_ref[...] = part if r_ref is None else part + r_ref[...]

            @pl.when(k > 0)
            def _():
                o_ref[...] += part

    assert nk == 1 or out_shape.dtype == F32
    in_specs = [a_spec] + b_specs + ([res_spec] if n_res else []) + [HBM] * (n_into + n_ci)
    args = (((pos,) if n_pos else ()) + (a, *bs) + ((res,) if n_res else ()) + ((into,) if n_into else ())
            + tuple(copies.inputs if copies else ()))
    aliases = {n_pos + 1 + n_b + n_res: 0} if n_into else {}
    out_specs = [o_spec] + [HBM] * n_co
    out_shapes = [out_shape] + list(copies.out_shape if copies else [])
    scratch = list(copies.scratch if copies else [])
    params = pltpu.CompilerParams(
        dimension_semantics=("arbitrary",) * 3 if copies else ("parallel", "parallel", "arbitrary"),
        vmem_limit_bytes=VMEM_BIG, has_side_effects=copies is not None)
    out = pl.pallas_call(
        body, name=name,
        grid_spec=pltpu.PrefetchScalarGridSpec(
            num_scalar_prefetch=n_pos, grid=grid, in_specs=in_specs, out_specs=out_specs, scratch_shapes=scratch),
        out_shape=out_shapes, input_output_aliases=aliases, compiler_params=params)(*args)
    return out[0] if copies is None else (out[0], list(out[1:]))


def mm_nn_gathered(name, a, parts, out_dtype=F32, copies=None):
    M, K = a.shape
    P, w = len(parts), parts[0].shape[2]
    nper = P * w
    tm, tk, tn = _pick(M, 1024), _pick(K, FULL_K), _pick(w, 768)
    r = w // tn
    per_part = N_DEV * r

    def b_spec(g):
        def index(i, j, k, *_):
            t = jnp.clip(j - g * per_part, 0, per_part - 1)
            return (t // r, k, t % r)
        return pl.BlockSpec((None, tk, tn), index)

    def o_index(i, j, k, *_):
        t = j % per_part
        return (i, (t // r) * (nper // tn) + (j // per_part) * r + t % r)

    def product(a_ref, b_refs):
        j = pl.program_id(1)
        b = b_refs[0][...]
        for g in range(1, P):
            b = jnp.where(j >= g * per_part, b_refs[g][...], b)
        return _dot(a_ref[...].astype(BF16), b.astype(BF16))

    return _matmul(
        name, a, list(parts), grid=(M // tm, P * per_part, K // tk),
        a_spec=pl.BlockSpec((tm, tk), lambda i, j, k, *_: (i, k)),
        b_spec=[b_spec(g) for g in range(P)], o_spec=pl.BlockSpec((tm, tn), o_index),
        out_shape=jax.ShapeDtypeStruct((M, N_DEV * nper), out_dtype), dn=NN, copies=copies,
        product=product if P > 1 else None)


def mm_nn_part(name, a, part, g, P, into=None, out_dtype=F32, copies=None):
    M, K = a.shape
    w = part.shape[2]
    tm, tk, tn = _pick(M, 1024), _pick(K, FULL_K), _pick(w, 768)
    r = w // tn
    return _matmul(
        name, a, part, grid=(M // tm, N_DEV * r, K // tk),
        a_spec=pl.BlockSpec((tm, tk), lambda i, j, k, *_: (i, k)),
        b_spec=pl.BlockSpec((None, tk, tn), lambda i, j, k, *_: (j // r, k, j % r)),
        o_spec=pl.BlockSpec((tm, tn), lambda i, j, k, *_: (i, (j // r) * (P * r) + g * r + j % r)),
        out_shape=jax.ShapeDtypeStruct((M, N_DEV * P * w), out_dtype), dn=NN, copies=copies, into=into)


NT_SLICES = 2
W_IN_PARTS = 2
W_IN0_PARTS = 3


def mm_nt_gathered(name, a, parts, out_dtype=F32, copies=None):
    M, _ = a.shape
    P, (_, N, w) = len(parts), parts[0].shape
    nper = P * w
    tm, tn = _pick(M, 1024), _pick(N, 1024)
    S = NT_SLICES

    def product(a_ref, b_refs):
        total = None
        for s in range(S):
            for g in range(P):
                off = (s * P + g) * w
                term = _dot(a_ref[:, off:off + w].astype(BF16), b_refs[g][s], NT)
                total = term if total is None else total + term
        return total

    return _matmul(
        name, a, list(parts), grid=(M // tm, N // tn, N_DEV // S),
        a_spec=pl.BlockSpec((tm, S * nper), lambda i, j, k, *_: (i, k)),
        b_spec=[pl.BlockSpec((S, tn, w), lambda i, j, k, *_: (k, j, 0)) for _ in range(P)],
        o_spec=pl.BlockSpec((tm, tn), lambda i, j, k, *_: (i, j)),
        out_shape=jax.ShapeDtypeStruct((M, N), out_dtype), dn=NT, copies=copies, product=product)


def mm_tn_scattered(name, a, b, copies=None):
    L, M = a.shape
    nper = b.shape[1] // N_DEV
    tm, tn, tk = _pick(M, 1024), _pick(nper, 768), _pick(L, FULL_K)
    r = nper // tn
    return _matmul(
        name, a, b, grid=(M // tm, N_DEV * r, L // tk),
        a_spec=pl.BlockSpec((tk, tm), lambda i, j, k, *_: (k, i)),
        b_spec=pl.BlockSpec((tk, tn), lambda i, j, k, *_: (k, j)),
        o_spec=pl.BlockSpec((None, tm, tn), lambda i, j, k, *_: (j // r, i, j % r)),
        out_shape=jax.ShapeDtypeStruct((N_DEV, M, nper), F32), dn=TN, copies=copies)


def mm_tn_half(name, a, b, pos, own, copies=None):
    L, M = a.shape
    nper = b.shape[1] // N_DEV
    tm, tn, tk = _pick(M, 1024), _pick(nper, 768), _pick(L, FULL_K)
    r = nper // tn

    def b_map(i, j, k, p):
        core = p[0] if own else 1 - p[0]
        return (k, (2 * (j // r) + core) * r + j % r)

    return _matmul(
        name, a, b, grid=(M // tm, 4 * r, L // tk),
        a_spec=pl.BlockSpec((tk, tm), lambda i, j, k, *_: (k, i)),
        b_spec=pl.BlockSpec((tk, tn), b_map),
        o_spec=pl.BlockSpec((None, tm, tn), lambda i, j, k, *_: (j // r, i, j % r)),
        out_shape=jax.ShapeDtypeStruct((4, M, nper), F32), dn=TN, pos=pos, copies=copies)


def mm_plain(name, a, b, dn, out_dtype=F32, res=None, copies=None):
    if dn == NN:
        (M, K), N = a.shape, b.shape[1]
    elif dn == NT:
        (M, K), N = a.shape, b.shape[0]
    else:
        (K, M), N = a.shape, b.shape[1]
    tm, tn, tk = _pick(M, 1024), _pick(N, 512), _pick(K, FULL_K)
    a_spec =(pl.BlockSpec((tk, tm), lambda i, j, k, *_: (k, i)) if dn == TN
              else pl.BlockSpec((tm, tk), lambda i, j, k, *_: (i, k)))
    b_spec = (pl.BlockSpec((tn, tk), lambda i, j, k, *_: (j, k)) if dn == NT
              else pl.BlockSpec((tk, tn), lambda i, j, k, *_: (k, j)))
    o_spec = pl.BlockSpec((tm, tn), lambda i, j, k, *_: (i, j))
    return _matmul(
        name, a, b, grid=(M // tm, N // tn, K // tk), a_spec=a_spec, b_spec=b_spec, o_spec=o_spec,
        out_shape=jax.ShapeDtypeStruct((M, N), out_dtype), dn=dn,
        res=res, res_spec=o_spec if res is not None else None, copies=copies)


def rms_fwd(name, x, g):
    L, D = x.shape
    tr = _pick(L, ROW_TILE)

    def body(x_ref, g_ref, h_ref):
        xv = x_ref[...]
        r = lax.rsqrt(jnp.mean(xv * xv, axis=-1, keepdims=True) + EPS)
        h_ref[...] = (xv * r * g_ref[...]).astype(BF16)

    return pl.pallas_call(
        body, name=name, grid=(L // tr,),
        in_specs=[pl.BlockSpec((tr, D), lambda i: (i, 0)), pl.BlockSpec((1, D), lambda i: (0, 0))],
        out_specs=pl.BlockSpec((tr, D), lambda i: (i, 0)),
        out_shape=jax.ShapeDtypeStruct((L, D), BF16),
        compiler_params=_cparams(("parallel",), VMEM_MID))(x, g)


def rms_bwd(name, x, dh, dres, g):
    L, D = x.shape
    tr = _pick(L, ROW_TILE)

    def body(x_ref, dh_ref, dr_ref, g_ref, dx_ref, dxb_ref, dg_ref):
        xv = x_ref[...]
        r = lax.rsqrt(jnp.mean(xv * xv, axis=-1, keepdims=True) + EPS)
        xh = xv * r
        dhv = dh_ref[...]
        dn = dhv * g_ref[...]
        dxv = dr_ref[...] + r * (dn - xh * jnp.mean(dn * xh, axis=-1, keepdims=True))
        dx_ref[...] = dxv
        dxb_ref[...] = dxv.astype(BF16)

        @pl.when(pl.program_id(0) == 0)
        def _():
            dg_ref[...] = jnp.zeros_like(dg_ref)

        dg_ref[...] += jnp.sum(dhv * xh, axis=0, keepdims=True)

    row = pl.BlockSpec((tr, D), lambda i: (i, 0))
    vec = pl.BlockSpec((1, D), lambda i: (0, 0))
    return pl.pallas_call(
        body, name=name, grid=(L // tr,), in_specs=[row, row, row, vec], out_specs=[row, row, vec],
        out_shape=[jax.ShapeDtypeStruct((L, D), F32), jax.ShapeDtypeStruct((L, D), BF16),
                   jax.ShapeDtypeStruct((1, D), F32)],
        compiler_params=_cparams(("arbitrary",), VMEM_MID))(x, dh, dres, g)


def loss_head(name, x, g, target):
    L, D = x.shape
    tr = _pick(L, ROW_TILE)

    def body(x_ref, g_ref, t_ref, loss_ref, dx_ref, dxb_ref, dg_ref):
        xv = x_ref[...]
        gv = g_ref[...]
        r = lax.rsqrt(jnp.mean(xv * xv, axis=-1, keepdims=True) + EPS)
        xh = xv * r
        err = xh * gv - t_ref[...]
        dy = err * (1.0 / D)
        dn = dy * gv
        dxv = r * (dn - xh * jnp.mean(dn * xh, axis=-1, keepdims=True))
        dx_ref[...] = dxv
        dxb_ref[...] = dxv.astype(BF16)

        @pl.when(pl.program_id(0) == 0)
        def _():
            dg_ref[...] = jnp.zeros_like(dg_ref)
            loss_ref[...] = jnp.zeros_like(loss_ref)

        dg_ref[...] += jnp.sum(dy * xh, axis=0, keepdims=True)
        row_loss = jnp.sum(err * err, axis=-1, keepdims=True) * (0.5 / D)
        loss_ref[...] += jnp.sum(row_loss, axis=0, keepdims=True)

    row = pl.BlockSpec((tr, D), lambda i: (i, 0))
    vec = pl.BlockSpec((1, D), lambda i: (0, 0))
    one = pl.BlockSpec((1, 1), lambda i: (0, 0))
    return pl.pallas_call(
        body, name=name, grid=(L // tr,), in_specs=[row, vec, row], out_specs=[one, row, row, vec],
        out_shape=[jax.ShapeDtypeStruct((1, 1), F32), jax.ShapeDtypeStruct((L, D), F32),
                   jax.ShapeDtypeStruct((L, D), BF16), jax.ShapeDtypeStruct((1, D), F32)],
        compiler_params=_cparams(("arbitrary",), VMEM_MID))(x, g, target)


def _branch_specs(D, tr):
    DP, DA, DS = D // 4, D // 2, D // 4
    return dict(
        pool=pl.BlockSpec((tr, DP), lambda i: (i, 0)),
        attn=pl.BlockSpec((tr, DA), lambda i: (i, 0)),
        glu=pl.BlockSpec((tr, 2 * DS), lambda i: (i, 0)),
        p_gate=pl.BlockSpec((tr, DP), lambda i: (i, 1)),
        a_gate=pl.BlockSpec((tr, DA), lambda i: (i, 4)),
        s_gate=pl.BlockSpec((tr, DS), lambda i: (i, 11)),
        bglu=pl.BlockSpec((1, 2 * DS), lambda i: (0, 0)),
        bg=pl.BlockSpec((1, D), lambda i: (0, 0)),
        row=pl.BlockSpec((tr, D), lambda i: (i, 0)),
    )


def branch_fwd(name, ypool, yattn, glu_pre, proj, b_glu, branch_g):
    L, DP = ypool.shape
    D = 4 * DP
    DA, DS = D // 2, D // 4
    tr = _pick(L, ROW_TILE)
    s = _branch_specs(D, tr)

    def body(yp_ref, ya_ref, gl_ref, pg_ref, ag_ref, sg_ref, bgl_ref, bg_ref, y_ref):
        pre = gl_ref[...] + bgl_ref[...]
        ys = pre[:, :DS] * _sigmoid(pre[:, DS:])
        bg = bg_ref[...]

        def one(raw, gate, g):
            gate = gate.astype(F32)
            r = lax.rsqrt(jnp.mean(raw * raw, axis=-1, keepdims=True) + EPS)
            return raw * r * g * (gate * _sigmoid(gate))

        y_ref[:, :DP] = one(yp_ref[...], pg_ref[...], bg[:, :DP]).astype(BF16)
        y_ref[:, DP:DP + DA] = one(ya_ref[...], ag_ref[...], bg[:, DP:DP + DA]).astype(BF16)
        y_ref[:, DP + DA:] = one(ys, sg_ref[...], bg[:, DP + DA:]).astype(BF16)

    return pl.pallas_call(
        body, name=name, grid=(L // tr,),
        in_specs=[s["pool"], s["attn"], s["glu"], s["p_gate"], s["a_gate"], s["s_gate"], s["bglu"], s["bg"]],
        out_specs=s["row"], out_shape=jax.ShapeDtypeStruct((L, D), BF16),
        compiler_params=_cparams(("parallel",), VMEM_MID))(ypool, yattn, glu_pre, proj, proj, proj, b_glu, branch_g)


def branch_bwd(name, dy, ypool, yattn, glu_pre, proj, b_glu, branch_g):
    L, DP = ypool.shape
    D = 4 * DP
    DA, DS = D // 2, D // 4
    tr = _pick(L, ROW_TILE // 2)
    s = _branch_specs(D, tr)

    def body(dy_ref, yp_ref, ya_ref, gl_ref, pg_ref, ag_ref, sg_ref, bgl_ref, bg_ref,
             dyp_ref, dya_ref, dgl_ref, dpg_ref, dag_ref, dsg_ref, dbg_ref, dbgl_ref):
        @pl.when(pl.program_id(0) == 0)
        def _():
            dbg_ref[...] = jnp.zeros_like(dbg_ref)
            dbgl_ref[...] = jnp.zeros_like(dbgl_ref)

        bg = bg_ref[...]

        def one(raw, gate, g, dyb):
            gate = gate.astype(F32)
            r = lax.rsqrt(jnp.mean(raw * raw, axis=-1, keepdims=True) + EPS)
            n = raw * r
            sg = _sigmoid(gate)
            sl = gate * sg
            dgate = dyb * n * g * (sg * (1.0 + gate * (1.0 - sg)))
            dbg = jnp.sum(dyb * n * sl, axis=0, keepdims=True)
            dn = dyb * g * sl
            draw = r * (dn - n * jnp.mean(dn * n, axis=-1, keepdims=True))
            return draw, dgate, dbg

        draw, dgate, dbg = one(yp_ref[...], pg_ref[...], bg[:, :DP], dy_ref[:, :DP])
        dyp_ref[...] = draw
        dpg_ref[...] = dgate.astype(BF16)
        dbg_ref[:, :DP] += dbg

        draw, dgate, dbg = one(ya_ref[...], ag_ref[...], bg[:, DP:DP + DA], dy_ref[:, DP:DP + DA])
        dya_ref[...] = draw
        dag_ref[...] = dgate.astype(BF16)
        dbg_ref[:, DP:DP + DA] += dbg

        pre = gl_ref[...] + bgl_ref[...]
        val = pre[:, :DS]
        sgt = _sigmoid(pre[:, DS:])
        draw, dgate, dbg = one(val * sgt, sg_ref[...], bg[:, DP + DA:], dy_ref[:, DP + DA:])
        dsg_ref[...] = dgate.astype(BF16)
        dbg_ref[:, DP + DA:] += dbg
        dval = draw * sgt
        dgt = draw * val * sgt * (1.0 - sgt)
        dgl_ref[:, :DS] = dval.astype(BF16)
        dgl_ref[:, DS:] = dgt.astype(BF16)
        dbgl_ref[:, :DS] += jnp.sum(dval, axis=0, keepdims=True)
        dbgl_ref[:, DS:] += jnp.sum(dgt, axis=0, keepdims=True)

    loc = lambda w: pl.BlockSpec((tr, w), lambda i: (i, 0))
    return pl.pallas_call(
        body, name=name, grid=(L // tr,),
        in_specs=[s["row"], s["pool"], s["attn"], s["glu"], s["p_gate"], s["a_gate"], s["s_gate"], s["bglu"], s["bg"]],
        out_specs=[loc(DP), loc(DA), loc(2 * DS), loc(DP), loc(DA), loc(DS), s["bg"], s["bglu"]],
        out_shape=[jax.ShapeDtypeStruct((L, DP), F32), jax.ShapeDtypeStruct((L, DA), F32),
                   jax.ShapeDtypeStruct((L, 2 * DS), BF16), jax.ShapeDtypeStruct((L, DP), BF16),
                   jax.ShapeDtypeStruct((L, DA), BF16), jax.ShapeDtypeStruct((L, DS), BF16),
                   jax.ShapeDtypeStruct((1, D), F32), jax.ShapeDtypeStruct((1, 2 * DS), F32)],
        compiler_params=_cparams(("arbitrary",), VMEM_BIG),
    )(dy, ypool, yattn, glu_pre, proj, proj, proj, b_glu, branch_g)


def _pool_select(g, s2, s4, s8, s16):
    return jnp.where(g == 0, s2, jnp.where(g == 1, s4, jnp.where(g == 2, s8, s16)))


def _pool_window(g):
    return jnp.where(g == 0, 2.0, jnp.where(g == 1, 4.0, jnp.where(g == 2, 8.0, 16.0))).astype(F32)


def _pooled_chunk(pad, g, r0, ch):
    xh = pad[pl.ds(r0, ch + POOL_HALO), :]
    s2 = xh + pltpu.roll(xh, 1, 0)
    s4 = s2 + pltpu.roll(s2, 2, 0)
    s8 = s4 + pltpu.roll(s4, 4, 0)
    s16 = s8 + pltpu.roll(s8, 8, 0)
    win = _pool_select(g, s2, s4, s8, s16)[POOL_HALO:]
    pos = (r0 + 1 + lax.broadcasted_iota(jnp.int32, (ch, 1), 0)).astype(F32)
    return win / jnp.minimum(pos, _pool_window(g)) - xh[POOL_HALO:]


def pool_fwd(name, proj, wp, scale):
    L = proj.shape[0]
    DP = scale.shape[1]
    PG = DP // N_POOL_GROUPS
    ch = _pick(L, 256)

    def body(x_ref, w_ref, s_ref, o_ref, pad):
        g = pl.program_id(0)
        pad[0:POOL_HALO, :] = jnp.zeros((POOL_HALO, PG), F32)
        pad[POOL_HALO:, :] = x_ref[...].astype(F32)

        def chunk(ci, carry):
            r0 = pl.multiple_of(ci * ch, ch)
            pooled = _pooled_chunk(pad, g, r0, ch)
            o_ref[pl.ds(r0, ch), :] = _dot(pooled.astype(BF16), w_ref[...]) * s_ref[...]
            return carry

        lax.fori_loop(0, L // ch, chunk, 0)

    return pl.pallas_call(
        body, name=name, grid=(N_POOL_GROUPS,),
        in_specs=[pl.BlockSpec((L, PG), lambda g: (0, g)), pl.BlockSpec((None, PG, PG), lambda g: (g, 0, 0)),
                  pl.BlockSpec((1, PG), lambda g: (0, g))],
        out_specs=pl.BlockSpec((L, PG), lambda g: (0, g)),
        out_shape=jax.ShapeDtypeStruct((L, DP), F32),
        scratch_shapes=[pltpu.VMEM((L + POOL_HALO, PG), F32)],
        compiler_params=_cparams(("parallel",), VMEM_MID))(proj, wp, scale)


def pool_bwd(name, dyraw, proj, wp, scale):
    L = proj.shape[0]
    DP = scale.shape[1]
    PG = DP // N_POOL_GROUPS
    ch = _pick(L, 256)

    def body(dy_ref, x_ref, w_ref, s_ref, dx_ref, dw_ref, ds_ref, pad, dpad, dpo):
        g = pl.program_id(0)
        pad[0:POOL_HALO, :] = jnp.zeros((POOL_HALO, PG), F32)
        pad[POOL_HALO:, :] = x_ref[...].astype(F32)
        dpad[L:, :] = jnp.zeros((POOL_HALO, PG), F32)
        dw_ref[...] = jnp.zeros_like(dw_ref)
        ds_ref[...] = jnp.zeros_like(ds_ref)
        wv = w_ref[...]
        win_f = _pool_window(g)

        def chunk(ci, carry):
            r0 = pl.multiple_of(ci * ch, ch)
            pooled = _pooled_chunk(pad, g, r0, ch).astype(BF16)
            dyv = dy_ref[pl.ds(r0, ch), :]
            ds_ref[...] += jnp.sum(dyv * _dot(pooled, wv), axis=0, keepdims=True)
            dmixed = (dyv * s_ref[...]).astype(BF16)
            dw_ref[...] += _dot(pooled, dmixed, TN)
            dpooled = _dot(dmixed, wv, NT)
            pos = (r0 + 1 + lax.broadcasted_iota(jnp.int32, (ch, 1), 0)).astype(F32)
            dpad[pl.ds(r0, ch), :] = dpooled / jnp.minimum(pos, win_f)
            dpo[pl.ds(r0, ch), :] = dpooled
            return carry

        lax.fori_loop(0, L // ch, chunk, 0)

        def chunk2(ci, carry):
            r0 = pl.multiple_of(ci * ch, ch)
            n = ch + POOL_HALO
            dm = dpad[pl.ds(r0, n), :]
            s2 = dm + pltpu.roll(dm, n - 1, 0)
            s4 = s2 + pltpu.roll(s2, n - 2, 0)
            s8 = s4 + pltpu.roll(s4, n - 4, 0)
            s16 = s8 + pltpu.roll(s8, n - 8, 0)
            win = _pool_select(g, s2, s4, s8, s16)[:ch]
            dx_ref[pl.ds(r0, ch), :] = (win - dpo[pl.ds(r0, ch), :]).astype(BF16)
            return carry

        lax.fori_loop(0, L // ch, chunk2, 0)

    col = pl.BlockSpec((L, PG), lambda g: (0, g))
    return pl.pallas_call(
        body, name=name, grid=(N_POOL_GROUPS,),
        in_specs=[col, col, pl.BlockSpec((None, PG, PG), lambda g: (g, 0, 0)), pl.BlockSpec((1, PG), lambda g: (0, g))],
        out_specs=[col, pl.BlockSpec((None, PG, PG), lambda g: (g, 0, 0)), pl.BlockSpec((1, PG), lambda g: (0, g))],
        out_shape=[jax.ShapeDtypeStruct((L, DP), BF16), jax.ShapeDtypeStruct((N_POOL_GROUPS, PG, PG), F32),
                   jax.ShapeDtypeStruct((1, DP), F32)],
        scratch_shapes=[pltpu.VMEM((L + POOL_HALO, PG), F32), pltpu.VMEM((L + POOL_HALO, PG), F32),
                        pltpu.VMEM((L, PG), F32)],
        compiler_params=_cparams(("parallel",), VMEM_MID))(dyraw, proj, wp, scale)


def _attn_tile(L):
    return _pick(L, ATTN_TILE)


def _tri(t, strict):
    j = lax.broadcasted_iota(jnp.int32, (t, t), 0)
    s = lax.broadcasted_iota(jnp.int32, (t, t), 1)
    return ((j > s) if strict else (j >= s)).astype(BF16)


def _attn_block(q, kt, rb, after, diagonal):
    tq, tk = q.shape[0], kt.shape[0]
    z = _dot(q, kt, NT)
    e = jnp.exp(-jnp.abs(z))
    l1p = jnp.log(1.0 + e)
    log_sig = jnp.minimum(z, 0.0) - l1p
    b = -jnp.maximum(z, 0.0) - l1p
    causal = None
    if diagonal:
        causal = lax.broadcasted_iota(jnp.int32, (tq, tk), 1) < lax.broadcasted_iota(jnp.int32, (tq, tk), 0)
        b = jnp.where(causal, b, 0.0)
    b_hi = b.astype(BF16)
    b_lo = (b - b_hi.astype(F32)).astype(BF16)
    suffix = _dot(b_hi, after) + _dot(b_lo, after) + rb
    w = jnp.exp(log_sig + suffix)
    if diagonal:
        w = jnp.where(causal, w, 0.0)
    return z, e, causal, b, w


def _attn_sweep(i, visit):
    go = visit(i, True)
    lax.while_loop(lambda c: jnp.logical_and(c[0] >= 0, c[1]),
                   lambda c: (c[0] - 1, visit(c[0], False)), (i - 1, go))


def attn_fwd(name, proj, D, copies=None):
    L = proj.shape[0]
    DA = D // 2
    H = DA // HEAD_DIM
    tq = tk = _attn_tile(L)
    qo, ko, vo = (D // 2) // HEAD_DIM, D // HEAD_DIM, (3 * D // 2) // HEAD_DIM
    scale = HEAD_DIM ** -0.5

    def body(q_ref, k_ref, v_ref, tri_ref, o_ref, kb_s, vb_s, acc, rb):
        i = pl.program_id(1)

        @pl.when(i == 0)
        def _():
            kb_s[...] = k_ref[...].astype(BF16)
            vb_s[...] = v_ref[...].astype(BF16)

        q = (q_ref[...].astype(F32) * scale).astype(BF16)
        acc[...] = jnp.zeros_like(acc)
        rb[...] = jnp.zeros_like(rb)

        def visit(kb, diagonal):
            k0 = pl.multiple_of(kb * tk, tk)
            kt = kb_s[pl.ds(k0, tk), :]
            vt = vb_s[pl.ds(k0, tk), :]
            _, _, _, b, w = _attn_block(q, kt, rb[...], tri_ref[...], diagonal)
            acc[...] += _dot(w.astype(BF16), vt)
            rbn = rb[...] + jnp.sum(b, axis=1, keepdims=True)
            rb[...] = rbn
            return jnp.max(rbn) > -ATTN_DECAY_CUTOFF

        _attn_sweep(i, visit)
        o_ref[...] = acc[...]

    (out,), landed = _call(
        body, name=name, grid=(H, L // tq),
        in_specs=[pl.BlockSpec((tq, HEAD_DIM), lambda h, i: (i, qo + h)),
                  pl.BlockSpec((L, HEAD_DIM), lambda h, i: (0, ko + h)),
                  pl.BlockSpec((L, HEAD_DIM), lambda h, i: (0, vo + h)),
                  pl.BlockSpec((tk, tk), lambda h, i: (0, 0))],
        out_specs=[pl.BlockSpec((tq, HEAD_DIM), lambda h, i: (i, h))],
        out_shape=[jax.ShapeDtypeStruct((L, DA), F32)],
        scratch_shapes=[pltpu.VMEM((L, HEAD_DIM), BF16), pltpu.VMEM((L, HEAD_DIM), BF16),
                        pltpu.VMEM((tq, HEAD_DIM), F32), pltpu.VMEM((tq, 1), F32)],
        vmem=VMEM_MID, args=(proj, proj, proj, _tri(tk, True)), semantics=("arbitrary", "arbitrary"),
        copies=copies)
    return out, landed


def attn_bwd(name, proj, o, do, D, copies=None):
    L = proj.shape[0]
    DA = D // 2
    H = DA // HEAD_DIM
    tq = tk = _attn_tile(L)
    qo, ko, vo = (D // 2) // HEAD_DIM, D // HEAD_DIM, (3 * D // 2) // HEAD_DIM
    scale = HEAD_DIM ** -0.5

    def body(q_ref, k_ref, v_ref, o_ref, do_ref, after_ref, from_ref, dq_ref, dk_ref, dv_ref,
             kb_s, vb_s, dk_s, dv_s, dq_acc, rb, rg):
        i = pl.program_id(1)
        nq = pl.num_programs(1)

        @pl.when(i == 0)
        def _():
            kb_s[...] = k_ref[...].astype(BF16)
            vb_s[...] = v_ref[...].astype(BF16)
            dk_s[...] = jnp.zeros_like(dk_s)
            dv_s[...] = jnp.zeros_like(dv_s)

        q = (q_ref[...].astype(F32) * scale).astype(BF16)
        dob = do_ref[...].astype(BF16)
        delta = jnp.sum(dob.astype(F32) * o_ref[...], axis=1, keepdims=True)
        dq_acc[...] = jnp.zeros_like(dq_acc)
        rb[...] = jnp.zeros_like(rb)
        rg[...] = jnp.zeros_like(rg)

        def visit(kb, diagonal):
            k0 = pl.multiple_of(kb * tk, tk)
            kt = kb_s[pl.ds(k0, tk), :]
            vt = vb_s[pl.ds(k0, tk), :]
            z, e, causal, b, w = _attn_block(q, kt, rb[...], after_ref[...], diagonal)
            wq = w.astype(BF16)
            dw = _dot(dob, vt, NT)
            g = wq.astype(F32) * dw
            g_hi = g.astype(BF16)
            g_lo = (g - g_hi.astype(F32)).astype(BF16)
            from_s = from_ref[...]
            suffix_g = _dot(g_hi, from_s) + _dot(g_lo, from_s) + rg[...]
            before = delta - suffix_g
            r = 1.0 / (1.0 + e)
            sig = jnp.where(z >= 0, r, e * r)
            sig_neg = jnp.where(z >= 0, e * r, r)
            dz = g * sig_neg - before * sig
            if diagonal:
                dz = jnp.where(causal, dz, 0.0)
            dz = dz.astype(BF16)
            dq_acc[...] += _dot(dz, kt)
            dk_s[pl.ds(k0, tk), :] += _dot(dz, q, TN)
            dv_s[pl.ds(k0, tk), :] += _dot(wq, dob, TN)
            rbn = rb[...] + jnp.sum(b, axis=1, keepdims=True)
            rb[...] = rbn
            rg[...] += jnp.sum(g, axis=1, keepdims=True)
            return jnp.max(rbn) > -ATTN_DECAY_CUTOFF

        _attn_sweep(i, visit)
        dq_ref[...] = (dq_acc[...] * scale).astype(BF16)

        @pl.when(i == nq - 1)
        def _():
            dk_ref[...] = dk_s[...].astype(BF16)
            dv_ref[...] = dv_s[...].astype(BF16)

    blk = pl.BlockSpec((tq, HEAD_DIM), lambda h, i: (i, h))
    full = pl.BlockSpec((L, HEAD_DIM), lambda h, i: (0, h))
    return _call(
        body, name=name, grid=(H, L // tq),
        in_specs=[pl.BlockSpec((tq, HEAD_DIM), lambda h, i: (i, qo + h)),
                  pl.BlockSpec((L, HEAD_DIM), lambda h, i: (0, ko + h)),
                  pl.BlockSpec((L, HEAD_DIM), lambda h, i: (0, vo + h)), blk, blk,
                  pl.BlockSpec((tk, tk), lambda h, i: (0, 0)), pl.BlockSpec((tk, tk), lambda h, i: (0, 0))],
        out_specs=[blk, full, full],
        out_shape=[jax.ShapeDtypeStruct((L, DA), BF16)] * 3,
        scratch_shapes=[pltpu.VMEM((L, HEAD_DIM), BF16), pltpu.VMEM((L, HEAD_DIM), BF16),
                        pltpu.VMEM((L, HEAD_DIM), F32), pltpu.VMEM((L, HEAD_DIM), F32),
                        pltpu.VMEM((tq, HEAD_DIM), F32), pltpu.VMEM((tq, 1), F32), pltpu.VMEM((tq, 1), F32)],
        vmem=VMEM_MID, args=(proj, proj, proj, o, do, _tri(tk, True), _tri(tk, False)),
        semantics=("arbitrary", "arbitrary"), copies=copies)


def _cmul(ar, ai, br, bi):
    return ar * br - ai * bi, ar * bi + ai * br


def _cmul_conj(ar, ai, br, bi):
    return ar * br + ai * bi, ar * bi - ai * br


def _ssm_disc(lr, li, ld):
    dt = jnp.exp(ld)
    m = jnp.exp(lr * dt)
    ar, ai = m * jnp.cos(li * dt), m * jnp.sin(li * dt)
    inv = 1.0 / (lr * lr + li * li)
    fr, fi = _cmul(ar - 1.0, ai, lr * inv, -li * inv)
    return dt, ar, ai, fr, fi, inv


def ssm_prep(name, lr, li, ld, br, bi):
    def body(lr_ref, li_ref, ld_ref, br_ref, bi_ref, zr_ref, zi_ref, bbr_ref, bbi_ref):
        dt, _, _, fr, fi, _ = _ssm_disc(lr_ref[...], li_ref[...], ld_ref[...])
        zr_ref[...] = lr_ref[...] * dt
        zi_ref[...] = li_ref[...] * dt
        bbr, bbi = _cmul(fr, fi, br_ref[...], bi_ref[...])
        bbr_ref[...] = bbr
        bbi_ref[...] = bbi

    sd = jax.ShapeDtypeStruct
    return pl.pallas_call(
        body, name=name,
        out_shape=[sd(lr.shape, F32), sd(lr.shape, F32), sd(br.shape, F32), sd(br.shape, F32)],
    )(lr, li, ld, br, bi)


def ssm_prep_bwd(name, lr, li, ld, br, bi, gar, gai, gbr, gbi):
    def body(lr_ref, li_ref, ld_ref, br_ref, bi_ref, gar_ref, gai_ref, gbr_ref, gbi_ref,
             dlr_ref, dli_ref, dld_ref, dbr_ref, dbi_ref):
        lr_, li_ = lr_ref[...], li_ref[...]
        dt, ar, ai, fr, fi, inv = _ssm_disc(lr_, li_, ld_ref[...])
        gbr_, gbi_ = gbr_ref[...], gbi_ref[...]
        dbr, dbi = _cmul_conj(fr, fi, gbr_, gbi_)
        dbr_ref[...] = dbr
        dbi_ref[...] = dbi
        pr, pi = _cmul_conj(br_ref[...], bi_ref[...], gbr_, gbi_)
        gfr = jnp.sum(pr, axis=1, keepdims=True)
        gfi = jnp.sum(pi, axis=1, keepdims=True)
        ilr, ili = lr_ * inv, -li_ * inv
        tr_, ti_ = _cmul_conj(ilr, ili, gfr, gfi)
        gatr, gati = gar_ref[...] + tr_, gai_ref[...] + ti_
        hr, hi = _cmul(fr, fi, ilr, ili)
        t1r, t1i = _cmul_conj(ar * dt, ai * dt, gatr, gati)
        t2r, t2i = _cmul_conj(hr, hi, gfr, gfi)
        dlr_ref[...] = t1r - t2r
        dli_ref[...] = t1i - t2i
        lar, lai = _cmul(lr_, li_, ar, ai)
        gdt, _ = _cmul_conj(lar, lai, gatr, gati)
        dld_ref[...] = jnp.sum(gdt, axis=2, keepdims=True) * dt

    sd = jax.ShapeDtypeStruct
    return pl.pallas_call(
        body, name=name,
        out_shape=[sd(lr.shape, F32), sd(lr.shape, F32), sd(ld.shape, F32), sd(br.shape, F32), sd(br.shape, F32)],
    )(lr, li, ld, br, bi, gar, gai, gbr, gbi)


SCAN_ROWS = 64


def _scan_rows(L):
    return min(SCAN_ROWS, L)


def _power_table(pr_s, pi_s, zr, zi, L, reverse):
    R = _scan_rows(L)
    row = lax.broadcasted_iota(jnp.int32, (R, 1), 0).astype(F32)
    dist = (R - row) if reverse else (row + 1.0)
    mag = jnp.exp(dist * zr)
    pr_s[...] = mag * jnp.cos(dist * zi)
    pi_s[...] = mag * jnp.sin(dist * zi)


def _scan(xr, xi, pr_s, pi_s, L, reverse):
    R = _scan_rows(L)
    nt = L // R
    assert L % R == 0 and R & (R - 1) == 0
    ns = CHUNK_X // LANE
    ridx = lax.broadcasted_iota(jnp.int32, (R, LANE), 0)

    def power(ref, d, cs):
        at = R - d if reverse else d - 1
        return ref[at:at + 1, cs]

    def shift(v, d):
        if d < 8:
            if reverse:
                return jnp.where(ridx < R - d, pltpu.roll(v, R - d, 0), 0.0)
            return jnp.where(ridx >= d, pltpu.roll(v, d, 0), 0.0)
        zeros = jnp.zeros((d, LANE), F32)
        return jnp.concatenate([v[d:], zeros], 0) if reverse else jnp.concatenate([zeros, v[:R - d]], 0)

    def tile(n, carry):
        t = nt - 1 - n if reverse else n
        rows = pl.ds(pl.multiple_of(t * R, R), R)
        edges = []
        for c in range(ns):
            cs = slice(c * LANE, (c + 1) * LANE)
            vr, vi = xr[rows, cs], xi[rows, cs]
            d = 1
            while d < R:
                ar, ai = power(pr_s, d, cs), power(pi_s, d, cs)
                sr, si = shift(vr, d), shift(vi, d)
                vr, vi = vr + ar * sr - ai * si, vi + ar * si + ai * sr
                d *= 2
            cr, ci = carry[2 * c], carry[2 * c + 1]
            pr, pi = pr_s[:, cs], pi_s[:, cs]
            vr, vi = vr + pr * cr - pi * ci, vi + pr * ci + pi * cr
            xr[rows, cs] = vr
            xi[rows, cs] = vi
            edge = slice(0, 1) if reverse else slice(R - 1, R)
            edges += [vr[edge], vi[edge]]
        return tuple(edges)

    lax.fori_loop(0, nt, tile, tuple(jnp.zeros((1, LANE), F32) for _ in range(2 * ns)))


def _gelu(x):
    t = jnp.tanh(0.7978845608028654 * (x + 0.044715 * x * x * x))
    return 0.5 * x * (1.0 + t)


def _gelu_grad(x):
    t = jnp.tanh(0.7978845608028654 * (x + 0.044715 * x * x * x))
    return 0.5 * (1.0 + t) + 0.5 * x * (1.0 - t * t) * 0.7978845608028654 * (1.0 + 0.134145 * x * x)


def _call(body, *, name, grid, in_specs, out_specs, out_shape, scratch_shapes, vmem, args, semantics,
          copies=None):
    n_i, n_o, n_s = len(in_specs), len(out_specs), len(scratch_shapes)
    if copies is None:
        out = pl.pallas_call(
            body, name=name, grid=grid, in_specs=in_specs, out_specs=out_specs, out_shape=out_shape,
            scratch_shapes=scratch_shapes, compiler_params=_cparams(semantics, vmem))(*args)
        return list(out), []
    n_ci, n_co = len(copies.inputs), len(copies.out_shape)

    def hosted(*refs):
        ins, cin = refs[:n_i], refs[n_i:n_i + n_ci]
        outs = refs[n_i + n_ci:n_i + n_ci + n_o]
        cout = refs[n_i + n_ci + n_o:n_i + n_ci + n_o + n_co]
        scr = refs[n_i + n_ci + n_o + n_co:n_i + n_ci + n_o + n_co + n_s]
        sems = refs[n_i + n_ci + n_o + n_co + n_s:]
        step = pl.program_id(0)
        for axis in range(1, len(grid)):
            step = step * grid[axis] + pl.program_id(axis)
        copies.emit(cin, cout, sems, step, math.prod(grid))
        body(*ins, *outs, *scr)

    out = pl.pallas_call(
        hosted, name=name, grid=grid, in_specs=list(in_specs) + [HBM] * n_ci,
        out_specs=list(out_specs) + [HBM] * n_co, out_shape=list(out_shape) + list(copies.out_shape),
        scratch_shapes=list(scratch_shapes) + list(copies.scratch),
        compiler_params=pltpu.CompilerParams(dimension_semantics=("arbitrary",) * len(grid),
                                             vmem_limit_bytes=vmem, has_side_effects=True))(*args, *copies.inputs)
    return list(out[:n_o]), list(out[n_o:])


def merge_copies(group):
    group = [c for c in group if c is not None]
    if len(group) <= 1:
        return group[0] if group else None
    bounds, i0, o0, s0 = [], 0, 0, 0
    for c in group:
        bounds.append((i0, o0, s0))
        i0, o0, s0 = i0 + len(c.inputs), o0 + len(c.out_shape), s0 + len(c.scratch)
    phases = []
    for c, (i, o, s) in zip(group, bounds):
        for frac, fn in c.phases:
            def shifted(ins, outs, sems, fn=fn, c=c, i=i, o=o, s=s):
                fn(ins[i:i + len(c.inputs)], outs[o:o + len(c.out_shape)], sems[s:s + len(c.scratch)])
            phases.append((frac, shifted))
    return HostedCopies([a for c in group for a in c.inputs], [a for c in group for a in c.out_shape],
                        [a for c in group for a in c.scratch], phases)


def ssm_fwd(name, proj, wbr, wbi, zr, zi, wcr, wci, dskip, D, copies=None):
    L = proj.shape[0]
    DS = D // 4
    NC = DS // CHUNK_U
    uo = (5 * D // 2) // CHUNK_U
    ch = _pick(L, 256)

    def body(u_ref, wbr_ref, wbi_ref, zr_ref, zi_ref, wcr_ref, wci_ref, ds_ref,
             y_ref, hg_ref, xr_ref, xi_ref, sr, si, pr_s, pi_s):
        def fill(ci, carry):
            rows = pl.ds(pl.multiple_of(ci * ch, ch), ch)
            ub = u_ref[rows, :].astype(BF16)
            sr[rows, :] = _dot(ub, wbr_ref[...])
            si[rows, :] = _dot(ub, wbi_ref[...])
            return carry

        lax.fori_loop(0, L // ch, fill, 0)
        _power_table(pr_s, pi_s, zr_ref[...], zi_ref[...], L, reverse=False)
        _scan(sr, si, pr_s, pi_s, L, reverse=False)

        def emit(ci, carry):
            rows = pl.ds(pl.multiple_of(ci * ch, ch), ch)
            xrb, xib = sr[rows, :].astype(BF16), si[rows, :].astype(BF16)
            xr_ref[rows, :] = xrb
            xi_ref[rows, :] = xib
            y = _dot(xrb, wcr_ref[...]) - _dot(xib, wci_ref[...]) + ds_ref[...] * u_ref[rows, :].astype(F32)
            y_ref[rows, :] = y
            hg_ref[rows, :] = _gelu(y).astype(BF16)
            return carry

        lax.fori_loop(0, L // ch, emit, 0)

    ucol = pl.BlockSpec((L, CHUNK_U), lambda k: (0, k))
    xcol = pl.BlockSpec((L, CHUNK_X), lambda k: (0, k))
    sd = jax.ShapeDtypeStruct
    return _call(
        body, name=name, grid=(NC,),
        in_specs=[pl.BlockSpec((L, CHUNK_U), lambda k: (0, uo + k)),
                  pl.BlockSpec((None, CHUNK_U, CHUNK_X), lambda k: (k, 0, 0)),
                  pl.BlockSpec((None, CHUNK_U, CHUNK_X), lambda k: (k, 0, 0)),
                  pl.BlockSpec((1, CHUNK_X), lambda k: (0, k)), pl.BlockSpec((1, CHUNK_X), lambda k: (0, k)),
                  pl.BlockSpec((None, CHUNK_X, CHUNK_U), lambda k: (k, 0, 0)),
                  pl.BlockSpec((None, CHUNK_X, CHUNK_U), lambda k: (k, 0, 0)),
                  pl.BlockSpec((1, CHUNK_U), lambda k: (0, k))],
        out_specs=[ucol, ucol, xcol, xcol],
        out_shape=[sd((L, DS), F32), sd((L, DS), BF16), sd((L, 4 * DS), BF16), sd((L, 4 * DS), BF16)],
        scratch_shapes=[pltpu.VMEM((L, CHUNK_X), F32), pltpu.VMEM((L, CHUNK_X), F32),
                        pltpu.VMEM((_scan_rows(L), CHUNK_X), F32), pltpu.VMEM((_scan_rows(L), CHUNK_X), F32)],
        vmem=VMEM_BIG, args=(proj, wbr, wbi, zr, zi, wcr, wci, dskip), semantics=("parallel",), copies=copies)


def ssm_bwd(name, dhg, ypre, proj, xr, xi, wbr, wbi, zr, zi, wcr, wci, dskip, D, copies=None):
    L = proj.shape[0]
    DS = D // 4
    NC = DS // CHUNK_U
    uo = (5 * D // 2) // CHUNK_U
    ch = _pick(L, 256)
    nch = L // ch
    halo = 16

    def body(dhg_ref, y_ref, u_ref, xr_ref, xi_ref, wbr_ref, wbi_ref, zr_ref, zi_ref, wcr_ref, wci_ref,
             ds_ref, du_ref, dwcr_ref, dwci_ref, dwbr_ref, dwbi_ref, dar_ref, dai_ref, dds_ref,
             gr, gi, duf, pr_s, pi_s):
        dwcr_ref[...] = jnp.zeros_like(dwcr_ref)
        dwci_ref[...] = jnp.zeros_like(dwci_ref)
        dwbr_ref[...] = jnp.zeros_like(dwbr_ref)
        dwbi_ref[...] = jnp.zeros_like(dwbi_ref)
        dar_ref[...] = jnp.zeros_like(dar_ref)
        dai_ref[...] = jnp.zeros_like(dai_ref)
        dds_ref[...] = jnp.zeros_like(dds_ref)

        def first(ci, carry):
            rows = pl.ds(pl.multiple_of(ci * ch, ch), ch)
            dy = dhg_ref[rows, :] * _gelu_grad(y_ref[rows, :])
            dyb = dy.astype(BF16)
            dds_ref[...] += jnp.sum(dy * u_ref[rows, :].astype(F32), axis=0, keepdims=True)
            duf[rows, :] = ds_ref[...] * dy
            gr[rows, :] = _dot(dyb, wcr_ref[...], NT)
            gi[rows, :] = -_dot(dyb, wci_ref[...], NT)
            dwcr_ref[...] += _dot(xr_ref[rows, :], dyb, TN)
            dwci_ref[...] -= _dot(xi_ref[rows, :], dyb, TN)
            return carry

        lax.fori_loop(0, nch, first, 0)
        _power_table(pr_s, pi_s, zr_ref[...], -zi_ref[...], L, reverse=True)
        _scan(gr, gi, pr_s, pi_s, L, reverse=True)

        def lam_grad(gxr, gxi, xpr, xpi):
            pr, pi = _cmul_conj(xpr, xpi, gxr, gxi)
            dar_ref[...] += jnp.sum(pr, axis=0, keepdims=True)
            dai_ref[...] += jnp.sum(pi, axis=0, keepdims=True)

        def second(ci, carry):
            r0 = pl.multiple_of(ci * ch, ch)
            rows = pl.ds(r0, ch)
            gxr, gxi = gr[rows, :], gi[rows, :]
            gxrb, gxib = gxr.astype(BF16), gxi.astype(BF16)
            du_ref[rows, :] = (duf[rows, :] + _dot(gxrb, wbr_ref[...], NT) + _dot(gxib, wbi_ref[...], NT)).astype(BF16)
            ub = u_ref[rows, :].astype(BF16)
            dwbr_ref[...] += _dot(ub, gxrb, TN)
            dwbi_ref[...] += _dot(ub, gxib, TN)
            return carry

        lax.fori_loop(0, nch, second, 0)

        ridx = lax.broadcasted_iota(jnp.int32, (ch, CHUNK_X), 0)
        xpr = jnp.where(ridx >= 1, pltpu.roll(xr_ref[0:ch, :].astype(F32), 1, 0), 0.0)
        xpi = jnp.where(ridx >= 1, pltpu.roll(xi_ref[0:ch, :].astype(F32), 1, 0), 0.0)
        lam_grad(gr[0:ch, :], gi[0:ch, :], xpr, xpi)

        def third(ci, carry):
            r0 = pl.multiple_of(ci * ch, ch)
            ext = pl.ds(pl.multiple_of(r0 - halo, halo), ch + halo)
            xpr = pltpu.roll(xr_ref[ext, :].astype(F32), 1, 0)[halo:]
            xpi = pltpu.roll(xi_ref[ext, :].astype(F32), 1, 0)[halo:]
            lam_grad(gr[pl.ds(r0, ch), :], gi[pl.ds(r0, ch), :], xpr, xpi)
            return carry

        if nch > 1:
            lax.fori_loop(1, nch, third, 0)

    ucol = pl.BlockSpec((L, CHUNK_U), lambda k: (0, k))
    xcol = pl.BlockSpec((L, CHUNK_X), lambda k: (0, k))
    wb_spec = pl.BlockSpec((None, CHUNK_U, CHUNK_X), lambda k: (k, 0, 0))
    wc_spec = pl.BlockSpec((None, CHUNK_X, CHUNK_U), lambda k: (k, 0, 0))
    avec = pl.BlockSpec((1, CHUNK_X), lambda k: (0, k))
    uvec = pl.BlockSpec((1, CHUNK_U), lambda k: (0, k))
    sd = jax.ShapeDtypeStruct
    return _call(
        body, name=name, grid=(NC,),
        in_specs=[ucol, ucol, pl.BlockSpec((L, CHUNK_U), lambda k: (0, uo + k)), xcol, xcol,
                  wb_spec, wb_spec, avec, avec, wc_spec, wc_spec, uvec],
        out_specs=[ucol, wc_spec, wc_spec, wb_spec, wb_spec, avec, avec, uvec],
        out_shape=[sd((L, DS), BF16), sd((NC, CHUNK_X, CHUNK_U), F32), sd((NC, CHUNK_X, CHUNK_U), F32),
                   sd((NC, CHUNK_U, CHUNK_X), F32), sd((NC, CHUNK_U, CHUNK_X), F32),
                   sd((1, 4 * DS), F32), sd((1, 4 * DS), F32), sd((1, DS), F32)],
        scratch_shapes=[pltpu.VMEM((L, CHUNK_X), F32), pltpu.VMEM((L, CHUNK_X), F32), pltpu.VMEM((L, CHUNK_U), F32),
                        pltpu.VMEM((_scan_rows(L), CHUNK_X), F32), pltpu.VMEM((_scan_rows(L), CHUNK_X), F32)],
        vmem=VMEM_BIG, args=(dhg, ypre, proj, xr, xi, wbr, wbi, zr, zi, wcr, wci, dskip),
        semantics=("parallel",), copies=copies)


def _block_diag(w, transpose):
    G = w.shape[0]
    nc = G // GROUPS_PER_CHUNK
    w4 = w.reshape(nc, GROUPS_PER_CHUNK, SSM_GROUP, SSM_STATE)
    eye = jnp.eye(GROUPS_PER_CHUNK, dtype=w.dtype)
    if transpose:
        return (w4[:, None, :, :, :].transpose(0, 1, 4, 2, 3) * eye[None, :, None, :, None]).reshape(
            nc, CHUNK_X, CHUNK_U).astype(BF16)
    return (w4[:, :, :, None, :] * eye[None, :, None, :, None]).reshape(nc, CHUNK_U, CHUNK_X).astype(BF16)


def _diag_blocks(dw, transpose):
    nc = dw.shape[0]
    gpc = GROUPS_PER_CHUNK
    eye = jnp.eye(gpc, dtype=dw.dtype)
    if transpose:
        d5 = dw.reshape(nc, gpc, SSM_STATE, gpc, SSM_GROUP)
        kept = jnp.sum(d5 * eye[None, :, None, :, None], axis=1)
        return kept.transpose(0, 2, 3, 1).reshape(nc * gpc, SSM_GROUP, SSM_STATE)
    d5 = dw.reshape(nc, gpc, SSM_GROUP, gpc, SSM_STATE)
    kept = jnp.sum(d5 * eye[None, :, None, :, None], axis=3)
    return kept.reshape(nc * gpc, SSM_GROUP, SSM_STATE)


SHARD_BLOCK_ELEMS = 128 * 1024


def _shard_rows(R, C, scale):
    return _pick(R, max(8, scale * SHARD_BLOCK_ELEMS // C))


def cast_bf16(name, w, layer, parts=1):
    shape = w.shape[1:]
    w3 = w.reshape(w.shape[0], -1, shape[-1])
    _, R, C = w3.shape
    tr = _shard_rows(R, C, 4)
    cw = C // parts

    def body(w_ref, *o_refs):
        for g, o_ref in enumerate(o_refs):
            o_ref[...] = w_ref[:, g * cw:(g + 1) * cw].astype(BF16)

    out = pl.pallas_call(body, name=name, grid=(R // tr,),
                         in_specs=[pl.BlockSpec((None, tr, C), lambda i: (layer, i, 0))],
                         out_specs=[pl.BlockSpec((tr, cw), lambda i: (i, 0))] * parts,
                         out_shape=[jax.ShapeDtypeStruct((R, cw), BF16)] * parts,
                         compiler_params=_cparams(("parallel",), VMEM_MID))(w3)
    return [o.reshape(shape[:-1] + (cw,)) for o in out]


def _adamw(w, g, m, v):
    m = ADAM_B1 * m + (1.0 - ADAM_B1) * g
    v = ADAM_B2 * v + (1.0 - ADAM_B2) * (g * g)
    delta = -ADAM_LR * ((m * ADAM_C1) / (jnp.sqrt(v * ADAM_C2) + ADAM_EPS) + ADAM_WD * w)
    return delta, m, v


def _own_core(g4):
    return (lambda p: p[0]) if g4.shape[1] == 2 else (lambda p: 0)


def chip_partial(name, pos, g4, recv_a):
    _, _, R, C = g4.shape
    tr = _shard_rows(R, C, 4)
    core = _own_core(g4)

    def body(pos_ref, g_ref, a_ref, o_ref):
        o_ref[...] = (g_ref[...] + a_ref[...]).astype(BF16)

    return pl.pallas_call(
        body, name=name,
        grid_spec=pltpu.PrefetchScalarGridSpec(
            num_scalar_prefetch=1, grid=(4, R // tr),
            in_specs=[pl.BlockSpec((None, None, tr, C), lambda q, i, p: (q, core(p), i, 0)),
                      pl.BlockSpec((None, tr, C), lambda q, i, p: (q, i, 0))],
            out_specs=pl.BlockSpec((None, tr, C), lambda q, i, p: (q, i, 0))),
        out_shape=jax.ShapeDtypeStruct((4, R, C), BF16),
        compiler_params=_cparams(("parallel", "parallel"), VMEM_MID))(pos, g4, recv_a)


def adamw_shard(name, pos, layer, g4, recv_a, recv_b, w, m, v, prev, copies=None):
    _, _, R, C = g4.shape
    tr = _shard_rows(R, C, 1)
    n_prev = 0 if prev is None else 4
    n_ci, n_co = (len(copies.inputs), len(copies.out_shape)) if copies else (0, 0)
    core = _own_core(g4)

    def body(pos_ref, g_ref, a_ref, b_ref, w_ref, m_ref, v_ref, *rest):
        rest = rest[n_prev:]
        go_ref, d_ref, mo_ref, vo_ref = rest[n_ci:n_ci + 4]
        if copies is not None:
            copies.emit(rest[:n_ci], rest[n_ci + 4:n_ci + 4 + n_co], rest[n_ci + 4 + n_co:],
                        pl.program_id(0), R // tr)
        gs = g_ref[...] + a_ref[...]
        for j in range(3):
            gs = gs + b_ref[j].astype(F32)
        delta, mn, vn = _adamw(w_ref[...], gs, m_ref[...], v_ref[...])
        go_ref[...] = gs
        d_ref[...] = delta
        mo_ref[...] = mn
        vo_ref[...] = vn

    lay = pl.BlockSpec((None, tr, C), lambda i, p: (layer, i, 0))
    in_specs = [pl.BlockSpec((None, None, tr, C), lambda i, p: (p[1], core(p), i, 0)),
                pl.BlockSpec((None, tr, C), lambda i, p: (p[1], i, 0)),
                pl.BlockSpec((3, tr, C), lambda i, p: (0, i, 0)), lay, lay, lay]
    args = [g4, recv_a, recv_b, w, m, v]
    aliases = {}
    if prev is not None:
        in_specs += [pl.BlockSpec(memory_space=pl.ANY)] * 4
        args += list(prev)
        aliases = {7 + j: j for j in range(4)}
    out = pl.pallas_call(
        body, name=name,
        grid_spec=pltpu.PrefetchScalarGridSpec(
            num_scalar_prefetch=1, grid=(R // tr,), in_specs=in_specs + [HBM] * n_ci,
            out_specs=[lay] * 4 + [HBM] * n_co, scratch_shapes=list(copies.scratch) if copies else []),
        out_shape=[jax.ShapeDtypeStruct(w.shape, F32)] * 4 + list(copies.out_shape if copies else []),
        input_output_aliases=aliases,
        compiler_params=pltpu.CompilerParams(
            dimension_semantics=("arbitrary",) if copies else ("parallel",), vmem_limit_bytes=VMEM_MID,
            has_side_effects=copies is not None))(pos, *args, *(copies.inputs if copies else []))
    return list(out[:4]), list(out[4:])


def adamw_small(name, gathered, head, w, m, v):
    _, R, C = gathered.shape
    E = head.shape[1]
    tr = _pick(R, 512)
    assert E % 8 == 0 and E <= tr

    def body(g_ref, h_ref, w_ref, m_ref, v_ref, go_ref, d_ref, mo_ref, vo_ref):
        gs, hs = g_ref[0], h_ref[0]
        for j in range(1, N_DEV):
            gs = gs + g_ref[j]
            hs = hs + h_ref[j]
        hs = jnp.where(pl.program_id(0) == 0, hs, 0.0)
        gs = jnp.concatenate([gs[:E] + hs, gs[E:]], axis=0)
        delta, mn, vn = _adamw(w_ref[...], gs, m_ref[...], v_ref[...])
        go_ref[...] = gs
        d_ref[...] = delta
        mo_ref[...] = mn
        vo_ref[...] = vn

    spec = pl.BlockSpec((tr, C), lambda i: (i, 0))
    return pl.pallas_call(
        body, name=name, grid=(R // tr,),
        in_specs=[pl.BlockSpec((N_DEV, tr, C), lambda i: (0, i, 0)),
                  pl.BlockSpec((N_DEV, E, C), lambda i: (0, 0, 0)), spec, spec, spec], out_specs=[spec] * 4,
        out_shape=[jax.ShapeDtypeStruct((R, C), F32)] * 4,
        compiler_params=_cparams(("parallel",), VMEM_MID))(gathered, head, w, m, v)


def _position():
    return lax.axis_index("x"), lax.axis_index("y"), lax.axis_index("c")


FORWARD_AT = 0.88


def gather_copies(shards):
    n = len(shards)

    def parts(ins, outs, sems):
        send_sems, recv_sems, local_sems = sems
        x, y, c = _position()
        me, sibling = (x, y, c), (x, y, 1 - c)
        chips = [(1 - x, y), (x, 1 - y), (1 - x, 1 - y)]

        def copy(a, k, block, to, src=None):
            blk = outs[a].at[4 * block[0] + 2 * block[1] + block[2]]
            return pltpu.make_async_remote_copy(
                src_ref=blk if src is None else src, dst_ref=blk,
                send_sem=send_sems.at[a, k], recv_sem=recv_sems.at[a, k], device_id=to, device_id_type=MESH)

        mine = [pltpu.make_async_copy(ins[a], outs[a].at[4 * x + 2 * y + c], local_sems.at[a]) for a in range(n)]
        first = [[copy(a, 0, me, sibling, src=ins[a])] +
                 [copy(a, 1 + j, me, (*chip, c), src=ins[a]) for j, chip in enumerate(chips)] for a in range(n)]
        landed = [[copy(a, 1 + j, (*chip, c), me) for j, chip in enumerate(chips)] for a in range(n)]
        passed = [[copy(a, 4 + j, (*chip, c), sibling) for j, chip in enumerate(chips)] for a in range(n)]
        from_sibling = [[copy(a, 0, sibling, me)] +
                        [copy(a, 4 + j, (*chip, 1 - c), me) for j, chip in enumerate(chips)] for a in range(n)]
        return mine, first, landed, passed, from_sibling

    def start(ins, outs, sems):
        mine, first, _, _, _ = parts(ins, outs, sems)
        for a in range(n):
            mine[a].start()
            for cp in first[a]:
                cp.start()

    def forward(ins, outs, sems):
        _, _, landed, passed, _ = parts(ins, outs, sems)
        for a in range(n):
            for j in range(3):
                landed[a][j].wait_recv()
                passed[a][j].start()

    def finish(ins, outs, sems):
        mine, first, _, passed, from_sibling = parts(ins, outs, sems)
        for a in range(n):
            for cp in from_sibling[a]:
                cp.wait_recv()
        for a in range(n):
            for cp in first[a] + passed[a]:
                cp.wait_send()
            mine[a].wait()

    return HostedCopies(
        list(shards), [jax.ShapeDtypeStruct((N_DEV,) + s.shape, s.dtype) for s in shards],
        [pltpu.SemaphoreType.DMA((n, 7)), pltpu.SemaphoreType.DMA((n, 7)), pltpu.SemaphoreType.DMA((n,))],
        [(0.0, start), (FORWARD_AT, forward), (1.0, finish)])


def _exchange_copies(arrays, out_lead, make):
    n = len(arrays)

    def all_copies(ins, outs, sems):
        send_sems, recv_sems = sems
        return [make(ins[a], outs[a], send_sems.at[a, k], recv_sems.at[a, k], k)
                for a in range(n) for k in range(out_lead)]

    def start(ins, outs, sems):
        for cp in all_copies(ins, outs, sems):
            cp.start()

    def finish(ins, outs, sems):
        for cp in all_copies(ins, outs, sems):
            cp.wait()

    return HostedCopies(
        list(arrays), [jax.ShapeDtypeStruct((out_lead,) + a.shape[2:], a.dtype) for a in arrays],
        [pltpu.SemaphoreType.DMA((n, out_lead)), pltpu.SemaphoreType.DMA((n, out_lead))],
        [(0.0, start), (1.0, finish)])


def sibling_copies(grads):
    def make(src, dst, send_sem, recv_sem, q):
        x, y, c = _position()
        core = 1 - c if src.shape[1] == 2 else 0
        return pltpu.make_async_remote_copy(
            src_ref=src.at[q, core], dst_ref=dst.at[q], send_sem=send_sem, recv_sem=recv_sem,
            device_id=(x, y, 1 - c), device_id_type=MESH)

    return _exchange_copies(grads, 4, make)


def chip_copies(parts):
    def make(src, dst, send_sem, recv_sem, j):
        x, y, c = _position()
        chip = [(1 - x, y), (x, 1 - y), (1 - x, 1 - y)][j]
        return pltpu.make_async_remote_copy(
            src_ref=src.at[2 * chip[0] + chip[1], 0], dst_ref=dst.at[j], send_sem=send_sem, recv_sem=recv_sem,
            device_id=(*chip, c), device_id_type=MESH)

    return _exchange_copies(parts, 3, make)


class Carrier:
    def __init__(self):
        self.plan = {}
        self.counts = {}

    def ride(self, site, make, store):
        self.plan.setdefault(site, []).append((make, store))

    def make(self, site, ctx=None):
        if site not in self.plan:
            return None
        group = [make(ctx) for make, _ in self.plan[site]]
        self.counts[site] = [len(c.out_shape) for c in group]
        return merge_copies(group)

    def store(self, site, results):
        if site in self.plan:
            at = 0
            for (_, store), n in zip(self.plan[site], self.counts[site]):
                store(results[at:at + n])
                at += n

    def split(self, site, out):
        if site not in self.plan:
            return out
        self.store(site, out[1])
        return out[0]


def _pool_weight(gathered):
    PG = gathered.shape[-1]
    return gathered.transpose(1, 0, 2, 3).reshape(N_POOL_GROUPS, PG, PG)


def _layer_params(l, ln_g, pool_scale, lam_re, lam_im, log_dt, b_re, b_im, c_re, c_im,
                  d_skip, b_glu, branch_g):
    G, P = lam_re.shape[1:]
    p = dict(
        ln_g=ln_g[l][None, :], pool_scale=pool_scale[l][None, :], d_skip=d_skip[l][None, :],
        b_glu=b_glu[l][None, :], branch_g=branch_g[l][None, :],
        lr=lam_re[l].reshape(G, 1, P), li=lam_im[l].reshape(G, 1, P), ld=log_dt[l].reshape(G, 1, 1),
        br=b_re[l].transpose(0, 2, 1), bi=b_im[l].transpose(0, 2, 1), cr=c_re[l], ci=c_im[l])
    return p


def layer_fwd(l, x, p, gw, D, carrier, staged_parts=0):
    t = f"l{l}_"
    h = rms_fwd(t + "rms_fwd", x, p["ln_g"])
    if staged_parts:
        proj = None
        for g in range(staged_parts):
            site = (l, f"proj{g}")
            proj = carrier.split(site, mm_nn_part(t + f"proj{g}", h, gw("w_in", l, g), g, staged_parts,
                                                  into=proj, out_dtype=BF16, copies=carrier.make(site)))
    else:
        site = (l, "proj")
        proj = carrier.split(site, mm_nn_gathered(t + "proj", h, gw("w_in", l), out_dtype=BF16,
                                                  copies=carrier.make(site)))
    wp = _pool_weight(gw("w_pool", l)[0])
    ypool = pool_fwd(t + "pool_fwd", proj, wp, p["pool_scale"])
    site = (l, "attn_fwd")
    yattn, landed = attn_fwd(t + "attn_fwd", proj, D, copies=carrier.make(site))
    carrier.store(site, landed)
    zr, zi, bbr, bbi = ssm_prep(t + "ssm_prep", p["lr"], p["li"], p["ld"], p["br"], p["bi"])
    ssm_w = dict(wbr=_block_diag(bbr, False), wbi=_block_diag(bbi, False),
                 zr=zr.reshape(1, -1), zi=zi.reshape(1, -1),
                 wcr=_block_diag(p["cr"], True), wci=_block_diag(p["ci"], True))
    site = (l, "ssm_fwd")
    (ypre, hg, xr, xi), landed = ssm_fwd(
        t + "ssm_fwd", proj, ssm_w["wbr"], ssm_w["wbi"], ssm_w["zr"], ssm_w["zi"],
        ssm_w["wcr"], ssm_w["wci"], p["d_skip"], D, copies=carrier.make(site))
    carrier.store(site, landed)
    glu_pre = mm_nn_gathered(t + "glu", hg, gw("w_glu", l))
    y = branch_fwd(t + "branch_fwd", ypool, yattn, glu_pre, proj, p["b_glu"], p["branch_g"])
    out = mm_plain(t + "out", y, gw("w_out", l)[0].reshape(D, D), NN, res=x)
    saved = dict(x=x, h=h, proj=proj, ypool=ypool, yattn=yattn, ypre=ypre, hg=hg, xr=xr, xi=xi,
                 glu_pre=glu_pre, y=y, ssm_w=ssm_w, wp=wp)
    return out, saved


def layer_bwd(l, dres, dres_b, s, p, gw, D, carrier, pos, split_w_in):
    t = f"l{l}_"
    proj = s["proj"]

    def by_target(g):
        return g.reshape(4, 2, -1, g.shape[-1])

    big = {}
    w_out_g = gw("w_out", l)[0].reshape(D, D)
    site = (l, "dy")
    dy = carrier.split(site, mm_plain(t + "dy", dres_b, w_out_g, NT, copies=carrier.make(site)))
    big["w_out"] = by_target(mm_plain(t + "dw_out", s["y"], dres_b, TN).reshape(N_DEV, D // N_DEV, D))
    dypool, dyattn, dglu, dpg, dag, dsg, dbg, dbglu = branch_bwd(
        t + "branch_bwd", dy, s["ypool"], s["yattn"], s["glu_pre"], proj, p["b_glu"], p["branch_g"])
    dhg = mm_nt_gathered(t + "dhg", dglu, gw("w_glu", l))
    big["w_glu"] = by_target(mm_tn_scattered(t + "dw_glu", s["hg"], dglu))
    w = s["ssm_w"]
    site = (l, "ssm_bwd")
    (du, dwcr, dwci, dwbr, dwbi, dar, dai, dds), landed = ssm_bwd(
        t + "ssm_bwd", dhg, s["ypre"], proj, s["xr"], s["xi"], w["wbr"], w["wbi"], w["zr"], w["zi"],
        w["wcr"], w["wci"], p["d_skip"], D, copies=carrier.make(site, big))
    carrier.store(site, landed)
    G, _, P = p["lr"].shape
    dlr, dli, dld, dbr, dbi = ssm_prep_bwd(
        t + "ssm_prep_bwd", p["lr"], p["li"], p["ld"], p["br"], p["bi"],
        dar.reshape(G, 1, P), dai.reshape(G, 1, P), _diag_blocks(dwbr, False), _diag_blocks(dwbi, False))
    site = (l, "attn_bwd")
    (dq, dk, dv), landed = attn_bwd(t + "attn_bwd", proj, s["yattn"], dyattn, D, copies=carrier.make(site, big))
    carrier.store(site, landed)
    dxp, dwp, dps = pool_bwd(t + "pool_bwd", dypool, proj, s["wp"], p["pool_scale"])
    dproj = jnp.concatenate([dxp, dpg, dq, dk, dv, dag, du, dsg], axis=1)
    PG = dwp.shape[1]
    big["w_pool"] = by_target(dwp.reshape(N_POOL_GROUPS, N_DEV, PG // N_DEV, PG).transpose(1, 0, 2, 3))
    small = dict(pool_scale=dps[0], lam_re=dlr.reshape(G, P), lam_im=dli.reshape(G, P),
                 log_dt=dld.reshape(G), b_re=dbr.transpose(0, 2, 1), b_im=dbi.transpose(0, 2, 1),
                 c_re=_diag_blocks(dwcr, True), c_im=_diag_blocks(dwci, True),
                 d_skip=dds[0], b_glu=dbglu[0], branch_g=dbg[0])
    if split_w_in:
        site = (l, "dw_in_a")
        to_sibling = carrier.split(site, mm_tn_half(t + "dw_in_a", s["h"], dproj, pos, False,
                                                    copies=carrier.make(site, dict(big, small=small))))
        site = (l, "dw_in_b")
        mine = carrier.split(site, mm_tn_half(t + "dw_in_b", s["h"], dproj, pos, True,
                                              copies=carrier.make(site, dict(big, to_sibling=to_sibling[:, None]))))
        big["w_in"] = mine[:, None]
    else:
        big["w_in"] = by_target(mm_tn_scattered(t + "dw_in", s["h"], dproj))
    site = (l, "dh")
    dh = carrier.split(site, mm_nt_gathered(t + "dh", dproj, gw("w_in", l), copies=carrier.make(site, big)))
    dx, dx_b, dlng = rms_bwd(t + "rms_bwd", s["x"], dh, dres, p["ln_g"])
    small["ln_g"] = dlng[0]
    return dx, dx_b, small


SMALL_NAMES = ("ln_g", "pool_scale", "lam_re", "lam_im", "log_dt", "b_re", "b_im", "c_re", "c_im",
               "d_skip", "b_glu", "branch_g", "final_g")
BIG_NAMES = ("w_in", "w_pool", "w_glu", "w_out")
WEIGHT_ORDER = ("ln_g", "w_in", "w_pool", "pool_scale", "lam_re", "lam_im", "log_dt", "b_re", "b_im",
                "c_re", "c_im", "d_skip", "w_glu", "b_glu", "branch_g", "w_out", "final_g")


PACK_ROWS = 512


def _pack(arrs):
    flat = jnp.concatenate([a.reshape(-1) for a in arrs])
    pad = (-flat.shape[0]) % (PACK_ROWS * LANE)
    return jnp.pad(flat, (0, pad)).reshape(-1, LANE)


def _unpack(packed, like):
    flat = packed.reshape(-1)
    out, off = [], 0
    for a in like:
        out.append(flat[off:off + a.size].reshape(a.shape))
        off += a.size
    return out


def kernel(x, ln_g, w_in, w_pool, pool_scale, lam_re, lam_im, log_dt, b_re, b_im, c_re, c_im, d_skip, w_glu, b_glu, branch_g, w_out, final_g, loss_target, m_ln_g, m_w_in, m_w_pool, m_pool_scale, m_lam_re, m_lam_im, m_log_dt, m_b_re, m_b_im, m_c_re, m_c_im, m_d_skip, m_w_glu, m_b_glu, m_branch_g, m_w_out, m_final_g, v_ln_g, v_w_in, v_w_pool, v_pool_scale, v_lam_re, v_lam_im, v_log_dt, v_b_re, v_b_im, v_c_re, v_c_im, v_d_skip, v_w_glu, v_b_glu, v_branch_g, v_w_out, v_final_g):
    W = dict(ln_g=ln_g, w_in=w_in, w_pool=w_pool, pool_scale=pool_scale, lam_re=lam_re, lam_im=lam_im,
             log_dt=log_dt, b_re=b_re, b_im=b_im, c_re=c_re, c_im=c_im, d_skip=d_skip, w_glu=w_glu,
             b_glu=b_glu, branch_g=branch_g, w_out=w_out, final_g=final_g)
    Mo = dict(ln_g=m_ln_g, w_in=m_w_in, w_pool=m_w_pool, pool_scale=m_pool_scale, lam_re=m_lam_re,
              lam_im=m_lam_im, log_dt=m_log_dt, b_re=m_b_re, b_im=m_b_im, c_re=m_c_re, c_im=m_c_im,
              d_skip=m_d_skip, w_glu=m_w_glu, b_glu=m_b_glu, branch_g=m_branch_g, w_out=m_w_out,
              final_g=m_final_g)
    Vo = dict(ln_g=v_ln_g, w_in=v_w_in, w_pool=v_w_pool, pool_scale=v_pool_scale, lam_re=v_lam_re,
              lam_im=v_lam_im, log_dt=v_log_dt, b_re=v_b_re, b_im=v_b_im, c_re=v_c_re, c_im=v_c_im,
              d_skip=v_d_skip, w_glu=v_w_glu, b_glu=v_b_glu, branch_g=v_branch_g, w_out=v_w_out,
              final_g=v_final_g)
    depth = ln_g.shape[0]
    _, L, D = x.shape
    xc, yc, cc = _position()
    pos = jnp.stack([cc, 2 * xc + yc, 4 * xc + 2 * yc + cc]).astype(jnp.int32)

    def n_parts(n, l):
        return (W_IN_PARTS if l > 0 else W_IN0_PARTS) if n == "w_in" else 1

    shards, landed = {}, {}
    for n in BIG_NAMES:
        for l in range(depth):
            for g, part in enumerate(cast_bf16(f"cast_{n}_{l}", W[n], l, n_parts(n, l))):
                shards[n, l, g] = part
    carrier = Carrier()

    def gw(n, l, g=None):
        return landed[n, l, g] if g is not None else [landed[n, l, i] for i in range(n_parts(n, l))]

    def gather_plan(keys):
        return (lambda ctx: gather_copies([shards[k] for k in keys])), (lambda outs: landed.update(zip(keys, outs)))

    first = [("w_in", 0, 0)] + [("w_pool", l, 0) for l in range(depth)]
    landed.update(zip(first, copies_call("gather_first", gather_copies([shards[k] for k in first]))))
    for g in range(1, W_IN0_PARTS):
        carrier.ride((0, f"proj{g - 1}"), *gather_plan([("w_in", 0, g)]))
    carrier.ride((0, f"proj{W_IN0_PARTS - 1}"), *gather_plan([("w_out", 0, 0), ("w_glu", 0, 0)]))
    for l in range(1, depth):
        for g in range(W_IN_PARTS):
            call = "ssm_fwd" if g == W_IN_PARTS - 1 else "attn_fwd"
            carrier.ride((l - 1, call), *gather_plan([("w_in", l, g)]))
        carrier.ride((l, "proj"), *gather_plan([("w_out", l, 0), ("w_glu", l, 0)]))

    own, recv_a, recv_b = {}, {}, {}

    def sibling_plan(l, names, keep, pick):
        def make(ctx):
            own.update({(n, l): ctx[n] for n in keep})
            return sibling_copies(pick(ctx))
        return make, (lambda outs: recv_a.update(zip([(n, l) for n in names], outs)))

    def chip_plan(l, names):
        def make(ctx):
            parts = [chip_partial(f"chip_partial_{n}_{l}", pos, own[n, l], recv_a[n, l])[:, None] for n in names]
            return chip_copies(parts)
        return make, (lambda outs: recv_b.update(zip([(n, l) for n in names], outs)))

    early, late = ("w_out", "w_glu"), ("w_in", "w_pool")
    for l in range(1, depth):
        carrier.ride((l, "dh"), *sibling_plan(l, BIG_NAMES, BIG_NAMES, lambda big: [big[n] for n in BIG_NAMES]))
        carrier.ride((l - 1, "ssm_bwd"), *chip_plan(l, ("w_out", "w_glu", "w_pool")))
        carrier.ride((l - 1, "attn_bwd"), *chip_plan(l, ("w_in",)))
    carrier.ride((0, "ssm_bwd"), *sibling_plan(0, early, early, lambda big: [big[n] for n in early]))
    carrier.ride((0, "dw_in_a"), *chip_plan(0, early))
    carrier.ride((0, "dw_in_b"), *sibling_plan(0, late, ("w_pool",), lambda ctx: [ctx["to_sibling"], ctx["w_pool"]]))

    def last_chip_make(big):
        own["w_in", 0] = big["w_in"]
        return chip_plan(0, late)[0](big)

    carrier.ride((0, "dh"), last_chip_make, chip_plan(0, late)[1])

    small = [None] * depth
    small_like = [W[n] for n in SMALL_NAMES]
    small_landed = []

    def small_make(ctx):
        layer0 = dict(ctx["small"], ln_g=jnp.zeros((D,), F32))
        grads = [jnp.stack([layer0[n]] + [small[l][n] for l in range(1, depth)]) for n in SMALL_NAMES[:-1]]
        return gather_copies([_pack(grads + [dfinal[0]])])

    carrier.ride((0, "dw_in_a"), small_make, small_landed.extend)

    params = [_layer_params(l, ln_g, pool_scale, lam_re, lam_im, log_dt, b_re, b_im, c_re, c_im,
                            d_skip, b_glu, branch_g) for l in range(depth)]
    h = x[0]
    saved = []
    for l in range(depth):
        h, s = layer_fwd(l, h, params[l], gw, D, carrier, staged_parts=W_IN0_PARTS if l == 0 else 0)
        saved.append(s)
    loss_part, dres, dres_b, dfinal = loss_head("loss_head", h, final_g[None, :], loss_target[0])
    loss = lax.psum(loss_part[0, 0], ("x", "y", "c"))

    for l in reversed(range(depth)):
        dres, dres_b, small[l] = layer_bwd(l, dres, dres_b, saved[l], params[l], gw, D, carrier, pos,
                                           split_w_in=(l == 0))
    grad_x = dres[None]

    results = {}
    for n in BIG_NAMES:
        shape = W[n].shape
        R, C = int(math.prod(shape[1:-1])), shape[-1]
        w3, m3, v3 = (t.reshape(depth, R, C) for t in (W[n], Mo[n], Vo[n]))
        prev = None
        for l in range(depth):
            prev, _ = adamw_shard(f"adamw_{n}_{l}", pos, l, own[n, l], recv_a[n, l], recv_b[n, l], w3, m3, v3, prev)
        results[n] = [t.reshape(shape) for t in prev]

    head = copies_call("gather_ln_g", gather_copies([small[0]["ln_g"].reshape(-1, LANE)]))[0]
    packed = adamw_small("adamw_small", small_landed[0], head, _pack(small_like),
                         _pack([Mo[n] for n in SMALL_NAMES]), _pack([Vo[n] for n in SMALL_NAMES]))
    unpacked = [_unpack(t, small_like) for t in packed]
    for i, n in enumerate(SMALL_NAMES):
        results[n] = [unpacked[j][i] for j in range(4)]

    out = [loss, grad_x]
    for j in range(4):
        out += [results[n][j] for n in WEIGHT_ORDER]
    return tuple(out)
```

```python
import functools
import math

import jax
import jax.numpy as jnp
from jax import lax
from jax.experimental import pallas as pl
from jax.experimental.pallas import tpu as pltpu

F32 = jnp.float32
BF16 = jnp.bfloat16
MESH = pl.DeviceIdType.MESH

EPS = 1e-6
HEAD_DIM = 128
SSM_GROUP = 16
SSM_STATE = 64
GROUPS_PER_CHUNK = 8
CHUNK_U = GROUPS_PER_CHUNK * SSM_GROUP
CHUNK_X = GROUPS_PER_CHUNK * SSM_STATE
N_POOL_GROUPS = 4
POOL_HALO = 16
N_DEV = 8
LANE = 128
FULL_K = 4096
ATTN_TILE = 256
ATTN_DECAY_CUTOFF = 100.0
ROW_TILE = 128
VMEM_BIG = 58 * 1024 * 1024
VMEM_MID = 40 * 1024 * 1024

ADAM_LR = 0.001
ADAM_B1 = 0.9
ADAM_B2 = 0.999
ADAM_EPS = 1e-08
ADAM_WD = 0.01
ADAM_STEP = 10
ADAM_C1 = 1.0 / (1.0 - ADAM_B1 ** ADAM_STEP)
ADAM_C2 = 1.0 / (1.0 - ADAM_B2 ** ADAM_STEP)

NN = (((1,), (0,)), ((), ()))
NT = (((1,), (1,)), ((), ()))
TN = (((0,), (0,)), ((), ()))


def _pick(n, cap):
    if n <= cap:
        return n
    step = LANE if cap >= LANE else 8
    t = (cap // step) * step
    while t > step and n % t:
        t -= step
    assert n % t == 0, (n, cap)
    return t


def _cparams(sem, vmem=None):
    return pltpu.CompilerParams(dimension_semantics=sem, vmem_limit_bytes=vmem)


def _dot(a, b, dn=NN):
    return lax.dot_general(a, b, dn, preferred_element_type=F32)


def _sigmoid(x):
    e = jnp.exp(-jnp.abs(x))
    r = 1.0 / (1.0 + e)
    return jnp.where(x >= 0, r, e * r)


HBM = pl.BlockSpec(memory_space=pl.ANY)


class HostedCopies:
    def __init__(self, inputs, out_shape, scratch, phases):
        self.inputs, self.out_shape, self.scratch, self.phases = inputs, out_shape, scratch, phases

    def emit(self, ins, outs, sems, step, total):
        plan = {}
        for frac, fn in self.phases:
            plan.setdefault(min(total - 1, int(frac * total)), []).append(fn)
        for s in sorted(plan):
            def run(fns=plan[s]):
                for fn in fns:
                    fn(ins, outs, sems)
            if total == 1:
                run()
            else:
                pl.when(step == s)(run)


def copies_call(name, copies):
    n_i, n_o = len(copies.inputs), len(copies.out_shape)

    def body(*refs):
        copies.emit(refs[:n_i], refs[n_i:n_i + n_o], refs[n_i + n_o:], 0, 1)

    return pl.pallas_call(
        body, name=name, in_specs=[HBM] * n_i, out_specs=[HBM] * n_o, out_shape=copies.out_shape,
        scratch_shapes=copies.scratch, compiler_params=pltpu.CompilerParams(has_side_effects=True),
    )(*copies.inputs)


def _matmul(name, a, b, *, grid, a_spec, b_spec, o_spec, out_shape, dn,
            res=None, res_spec=None, pos=None, copies=None, product=None, into=None):
    ni, nj, nk = grid
    bs, b_specs = (list(b), list(b_spec)) if isinstance(b, (list, tuple)) else ([b], [b_spec])
    n_b = len(bs)
    n_pos = 0 if pos is None else 1
    n_res = 0 if res is None else 1
    n_into = 0 if into is None else 1
    n_ci = 0 if copies is None else len(copies.inputs)
    n_co = 0 if copies is None else len(copies.out_shape)

    def body(*refs):
        refs = refs[n_pos:]
        a_ref, b_refs = refs[0], refs[1:1 + n_b]
        r_ref = refs[1 + n_b] if n_res else None
        base = 1 + n_b + n_res + n_into
        cin = refs[base:base + n_ci]
        o_ref = refs[base + n_ci]
        cout = refs[base + n_ci + 1:base + n_ci + 1 + n_co]
        sems = refs[base + n_ci + 1 + n_co:]
        k = pl.program_id(2)
        if copies is not None:
            step = (pl.program_id(0) * nj + pl.program_id(1)) * nk + k
            copies.emit(cin, cout, sems, step, ni * nj * nk)

        if product is None:
            part = _dot(a_ref[...].astype(BF16), b_refs[0][...].astype(BF16), dn)
        else:
            part = product(a_ref, b_refs)
        if nk == 1:
            if r_ref is not None:
                part = part + r_ref[...]
            o_ref[...] = part.astype(o_ref.dtype)
        else:
            @pl.when(k == 0)
            def _():
                o_ref[...] = part if r_ref is None else part + r_ref[...]

            @pl.when(k > 0)
            def _():
                o_ref[...] += part

    assert nk == 1 or out_shape.dtype == F32
    in_specs = [a_spec] + b_specs + ([res_spec] if n_res else []) + [HBM] * (n_into + n_ci)
    args = (((pos,) if n_pos else ()) + (a, *bs) + ((res,) if n_res else ()) + ((into,) if n_into else ())
            + tuple(copies.inputs if copies else ()))
    aliases = {n_pos + 1 + n_b + n_res: 0} if n_into else {}
    out_specs = [o_spec] + [HBM] * n_co
    out_shapes = [out_shape] + list(copies.out_shape if copies else [])
    scratch = list(copies.scratch if copies else [])
    params = pltpu.CompilerParams(
        dimension_semantics=("arbitrary",) * 3 if copies else ("parallel", "parallel", "arbitrary"),
        vmem_limit_bytes=VMEM_BIG, has_side_effects=copies is not None)
    out = pl.pallas_call(
        body, name=name,
        grid_spec=pltpu.PrefetchScalarGridSpec(
            num_scalar_prefetch=n_pos, grid=grid, in_specs=in_specs, out_specs=out_specs, scratch_shapes=scratch),
        out_shape=out_shapes, input_output_aliases=aliases, compiler_params=params)(*args)
    return out[0] if copies is None else (out[0], list(out[1:]))


def mm_nn_gathered(name, a, parts, out_dtype=F32, copies=None):
    M, K = a.shape
    P, w = len(parts), parts[0].shape[2]
    nper = P * w
    tm, tk, tn = _pick(M, 1024), _pick(K, FULL_K), _pick(w, 768)
    r = w // tn
    per_part = N_DEV * r

    def b_spec(g):
        def index(i, j, k, *_):
            t = jnp.clip(j - g * per_part, 0, per_part - 1)
            return (t // r, k, t % r)
        return pl.BlockSpec((None, tk, tn), index)

    def o_index(i, j, k, *_):
        t = j % per_part
        return (i, (t // r) * (nper // tn) + (j // per_part) * r + t % r)

    def product(a_ref, b_refs):
        j = pl.program_id(1)
        b = b_refs[0][...]
        for g in range(1, P):
            b = jnp.where(j >= g * per_part, b_refs[g][...], b)
        return _dot(a_ref[...].astype(BF16), b.astype(BF16))

    return _matmul(
        name, a, list(parts), grid=(M // tm, P * per_part, K // tk),
        a_spec=pl.BlockSpec((tm, tk), lambda i, j, k, *_: (i, k)),
        b_spec=[b_spec(g) for g in range(P)], o_spec=pl.BlockSpec((tm, tn), o_index),
        out_shape=jax.ShapeDtypeStruct((M, N_DEV * nper), out_dtype), dn=NN, copies=copies,
        product=product if P > 1 else None)


def mm_nn_part(name, a, part, g, P, into=None, out_dtype=F32, copies=None):
    M, K = a.shape
    w = part.shape[2]
    tm, tk, tn = _pick(M, 1024), _pick(K, FULL_K), _pick(w, 768)
    r = w // tn
    return _matmul(
        name, a, part, grid=(M // tm, N_DEV * r, K // tk),
        a_spec=pl.BlockSpec((tm, tk), lambda i, j, k, *_: (i, k)),
        b_spec=pl.BlockSpec((None, tk, tn), lambda i, j, k, *_: (j // r, k, j % r)),
        o_spec=pl.BlockSpec((tm, tn), lambda i, j, k, *_: (i, (j // r) * (P * r) + g * r + j % r)),
        out_shape=jax.ShapeDtypeStruct((M, N_DEV * P * w), out_dtype), dn=NN, copies=copies, into=into)


NT_SLICES = 2
W_IN_PARTS = 2
W_IN0_PARTS = 3


def mm_nt_gathered(name, a, parts, out_dtype=F32, copies=None):
    M, _ = a.shape
    P, (_, N, w) = len(parts), parts[0].shape
    nper = P * w
    tm, tn = _pick(M, 1024), _pick(N, 1024)
    S = NT_SLICES

    def product(a_ref, b_refs):
        total = None
        for s in range(S):
            for g in range(P):
                off = (s * P + g) * w
                term = _dot(a_ref[:, off:off + w].astype(BF16), b_refs[g][s], NT)
                total = term if total is None else total + term
        return total

    return _matmul(
        name, a, list(parts), grid=(M // tm, N // tn, N_DEV // S),
        a_spec=pl.BlockSpec((tm, S * nper), lambda i, j, k, *_: (i, k)),
        b_spec=[pl.BlockSpec((S, tn, w), lambda i, j, k, *_: (k, j, 0)) for _ in range(P)],
        o_spec=pl.BlockSpec((tm, tn), lambda i, j, k, *_: (i, j)),
        out_shape=jax.ShapeDtypeStruct((M, N), out_dtype), dn=NT, copies=copies, product=product)


def mm_tn_scattered(name, a, b, copies=None):
    L, M = a.shape
    nper = b.shape[1] // N_DEV
    tm, tn, tk = _pick(M, 1024), _pick(nper, 768), _pick(L, FULL_K)
    r = nper // tn
    return _matmul(
        name, a, b, grid=(M // tm, N_DEV * r, L // tk),
        a_spec=pl.BlockSpec((tk, tm), lambda i, j, k, *_: (k, i)),
        b_spec=pl.BlockSpec((tk, tn), lambda i, j, k, *_: (k, j)),
        o_spec=pl.BlockSpec((None, tm, tn), lambda i, j, k, *_: (j // r, i, j % r)),
        out_shape=jax.ShapeDtypeStruct((N_DEV, M, nper), F32), dn=TN, copies=copies)


def mm_tn_half(name, a, b, pos, own, copies=None):
    L, M = a.shape
    nper = b.shape[1] // N_DEV
    tm, tn, tk = _pick(M, 1024), _pick(nper, 768), _pick(L, FULL_K)
    r = nper // tn

    def b_map(i, j, k, p):
        core = p[0] if own else 1 - p[0]
        return (k, (2 * (j // r) + core) * r + j % r)

    return _matmul(
        name, a, b, grid=(M // tm, 4 * r, L // tk),
        a_spec=pl.BlockSpec((tk, tm), lambda i, j, k, *_: (k, i)),
        b_spec=pl.BlockSpec((tk, tn), b_map),
        o_spec=pl.BlockSpec((None, tm, tn), lambda i, j, k, *_: (j // r, i, j % r)),
        out_shape=jax.ShapeDtypeStruct((4, M, nper), F32), dn=TN, pos=pos, copies=copies)


def mm_plain(name, a, b, dn, out_dtype=F32, res=None, copies=None):
    if dn == NN:
        (M, K), N = a.shape, b.shape[1]
    elif dn == NT:
        (M, K), N = a.shape, b.shape[0]
    else:
        (K, M), N = a.shape, b.shape[1]
    tm, tn, tk = _pick(M, 1024), _pick(N, 512), _pick(K, FULL_K)
    a_spec =(pl.BlockSpec((tk, tm), lambda i, j, k, *_: (k, i)) if dn == TN
              else pl.BlockSpec((tm, tk), lambda i, j, k, *_: (i, k)))
    b_spec = (pl.BlockSpec((tn, tk), lambda i, j, k, *_: (j, k)) if dn == NT
              else pl.BlockSpec((tk, tn), lambda i, j, k, *_: (k, j)))
    o_spec = pl.BlockSpec((tm, tn), lambda i, j, k, *_: (i, j))
    return _matmul(
        name, a, b, grid=(M // tm, N // tn, K // tk), a_spec=a_spec, b_spec=b_spec, o_spec=o_spec,
        out_shape=jax.ShapeDtypeStruct((M, N), out_dtype), dn=dn,
        res=res, res_spec=o_spec if res is not None else None, copies=copies)


def rms_fwd(name, x, g):
    L, D = x.shape
    tr = _pick(L, ROW_TILE)

    def body(x_ref, g_ref, h_ref):
        xv = x_ref[...]
        r = lax.rsqrt(jnp.mean(xv * xv, axis=-1, keepdims=True) + EPS)
        h_ref[...] = (xv * r * g_ref[...]).astype(BF16)

    return pl.pallas_call(
        body, name=name, grid=(L // tr,),
        in_specs=[pl.BlockSpec((tr, D), lambda i: (i, 0)), pl.BlockSpec((1, D), lambda i: (0, 0))],
        out_specs=pl.BlockSpec((tr, D), lambda i: (i, 0)),
        out_shape=jax.ShapeDtypeStruct((L, D), BF16),
        compiler_params=_cparams(("parallel",), VMEM_MID))(x, g)


def rms_bwd(name, x, dh, dres, g):
    L, D = x.shape
    tr = _pick(L, ROW_TILE)

    def body(x_ref, dh_ref, dr_ref, g_ref, dx_ref, dxb_ref, dg_ref):
        xv = x_ref[...]
        r = lax.rsqrt(jnp.mean(xv * xv, axis=-1, keepdims=True) + EPS)
        xh = xv * r
        dhv = dh_ref[...]
        dn = dhv * g_ref[...]
        dxv = dr_ref[...] + r * (dn - xh * jnp.mean(dn * xh, axis=-1, keepdims=True))
        dx_ref[...] = dxv
        dxb_ref[...] = dxv.astype(BF16)

        @pl.when(pl.program_id(0) == 0)
        def _():
            dg_ref[...] = jnp.zeros_like(dg_ref)

        dg_ref[...] += jnp.sum(dhv * xh, axis=0, keepdims=True)

    row = pl.BlockSpec((tr, D), lambda i: (i, 0))
    vec = pl.BlockSpec((1, D), lambda i: (0, 0))
    return pl.pallas_call(
        body, name=name, grid=(L // tr,), in_specs=[row, row, row, vec], out_specs=[row, row, vec],
        out_shape=[jax.ShapeDtypeStruct((L, D), F32), jax.ShapeDtypeStruct((L, D), BF16),
                   jax.ShapeDtypeStruct((1, D), F32)],
        compiler_params=_cparams(("arbitrary",), VMEM_MID))(x, dh, dres, g)


def loss_head(name, x, g, target):
    L, D = x.shape
    tr = _pick(L, ROW_TILE)

    def body(x_ref, g_ref, t_ref, loss_ref, dx_ref, dxb_ref, dg_ref):
        xv = x_ref[...]
        gv = g_ref[...]
        r = lax.rsqrt(jnp.mean(xv * xv, axis=-1, keepdims=True) + EPS)
        xh = xv * r
        err = xh * gv - t_ref[...]
        dy = err * (1.0 / D)
        dn = dy * gv
        dxv = r * (dn - xh * jnp.mean(dn * xh, axis=-1, keepdims=True))
        dx_ref[...] = dxv
        dxb_ref[...] = dxv.astype(BF16)

        @pl.when(pl.program_id(0) == 0)
        def _():
            dg_ref[...] = jnp.zeros_like(dg_ref)
            loss_ref[...] = jnp.zeros_like(loss_ref)

        dg_ref[...] += jnp.sum(dy * xh, axis=0, keepdims=True)
        row_loss = jnp.sum(err * err, axis=-1, keepdims=True) * (0.5 / D)
        loss_ref[...] += jnp.sum(row_loss, axis=0, keepdims=True)

    row = pl.BlockSpec((tr, D), lambda i: (i, 0))
    vec = pl.BlockSpec((1, D), lambda i: (0, 0))
    one = pl.BlockSpec((1, 1), lambda i: (0, 0))
    return pl.pallas_call(
        body, name=name, grid=(L // tr,), in_specs=[row, vec, row], out_specs=[one, row, row, vec],
        out_shape=[jax.ShapeDtypeStruct((1, 1), F32), jax.ShapeDtypeStruct((L, D), F32),
                   jax.ShapeDtypeStruct((L, D), BF16), jax.ShapeDtypeStruct((1, D), F32)],
        compiler_params=_cparams(("arbitrary",), VMEM_MID))(x, g, target)


def _branch_specs(D, tr):
    DP, DA, DS = D // 4, D // 2, D // 4
    return dict(
        pool=pl.BlockSpec((tr, DP), lambda i: (i, 0)),
        attn=pl.BlockSpec((tr, DA), lambda i: (i, 0)),
        glu=pl.BlockSpec((tr, 2 * DS), lambda i: (i, 0)),
        p_gate=pl.BlockSpec((tr, DP), lambda i: (i, 1)),
        a_gate=pl.BlockSpec((tr, DA), lambda i: (i, 4)),
        s_gate=pl.BlockSpec((tr, DS), lambda i: (i, 11)),
        bglu=pl.BlockSpec((1, 2 * DS), lambda i: (0, 0)),
        bg=pl.BlockSpec((1, D), lambda i: (0, 0)),
        row=pl.BlockSpec((tr, D), lambda i: (i, 0)),
    )


def branch_fwd(name, ypool, yattn, glu_pre, proj, b_glu, branch_g):
    L, DP = ypool.shape
    D = 4 * DP
    DA, DS = D // 2, D // 4
    tr = _pick(L, ROW_TILE)
    s = _branch_specs(D, tr)

    def body(yp_ref, ya_ref, gl_ref, pg_ref, ag_ref, sg_ref, bgl_ref, bg_ref, y_ref):
        pre = gl_ref[...] + bgl_ref[...]
        ys = pre[:, :DS] * _sigmoid(pre[:, DS:])
        bg = bg_ref[...]

        def one(raw, gate, g):
            gate = gate.astype(F32)
            r = lax.rsqrt(jnp.mean(raw * raw, axis=-1, keepdims=True) + EPS)
            return raw * r * g * (gate * _sigmoid(gate))

        y_ref[:, :DP] = one(yp_ref[...], pg_ref[...], bg[:, :DP]).astype(BF16)
        y_ref[:, DP:DP + DA] = one(ya_ref[...], ag_ref[...], bg[:, DP:DP + DA]).astype(BF16)
        y_ref[:, DP + DA:] = one(ys, sg_ref[...], bg[:, DP + DA:]).astype(BF16)

    return pl.pallas_call(
        body, name=name, grid=(L // tr,),
        in_specs=[s["pool"], s["attn"], s["glu"], s["p_gate"], s["a_gate"], s["s_gate"], s["bglu"], s["bg"]],
        out_specs=s["row"], out_shape=jax.ShapeDtypeStruct((L, D), BF16),
        compiler_params=_cparams(("parallel",), VMEM_MID))(ypool, yattn, glu_pre, proj, proj, proj, b_glu, branch_g)


def branch_bwd(name, dy, ypool, yattn, glu_pre, proj, b_glu, branch_g):
    L, DP = ypool.shape
    D = 4 * DP
    DA, DS = D // 2, D // 4
    tr = _pick(L, ROW_TILE // 2)
    s = _branch_specs(D, tr)

    def body(dy_ref, yp_ref, ya_ref, gl_ref, pg_ref, ag_ref, sg_ref, bgl_ref, bg_ref,
             dyp_ref, dya_ref, dgl_ref, dpg_ref, dag_ref, dsg_ref, dbg_ref, dbgl_ref):
        @pl.when(pl.program_id(0) == 0)
        def _():
            dbg_ref[...] = jnp.zeros_like(dbg_ref)
            dbgl_ref[...] = jnp.zeros_like(dbgl_ref)

        bg = bg_ref[...]

        def one(raw, gate, g, dyb):
            gate = gate.astype(F32)
            r = lax.rsqrt(jnp.mean(raw * raw, axis=-1, keepdims=True) + EPS)
            n = raw * r
            sg = _sigmoid(gate)
            sl = gate * sg
            dgate = dyb * n * g * (sg * (1.0 + gate * (1.0 - sg)))
            dbg = jnp.sum(dyb * n * sl, axis=0, keepdims=True)
            dn = dyb * g * sl
            draw = r * (dn - n * jnp.mean(dn * n, axis=-1, keepdims=True))
            return draw, dgate, dbg

        draw, dgate, dbg = one(yp_ref[...], pg_ref[...], bg[:, :DP], dy_ref[:, :DP])
        dyp_ref[...] = draw
        dpg_ref[...] = dgate.astype(BF16)
        dbg_ref[:, :DP] += dbg

        draw, dgate, dbg = one(ya_ref[...], ag_ref[...], bg[:, DP:DP + DA], dy_ref[:, DP:DP + DA])
        dya_ref[...] = draw
        dag_ref[...] = dgate.astype(BF16)
        dbg_ref[:, DP:DP + DA] += dbg

        pre = gl_ref[...] + bgl_ref[...]
        val = pre[:, :DS]
        sgt = _sigmoid(pre[:, DS:])
        draw, dgate, dbg = one(val * sgt, sg_ref[...], bg[:, DP + DA:], dy_ref[:, DP + DA:])
        dsg_ref[...] = dgate.astype(BF16)
        dbg_ref[:, DP + DA:] += dbg
        dval = draw * sgt
        dgt = draw * val * sgt * (1.0 - sgt)
        dgl_ref[:, :DS] = dval.astype(BF16)
        dgl_ref[:, DS:] = dgt.astype(BF16)
        dbgl_ref[:, :DS] += jnp.sum(dval, axis=0, keepdims=True)
        dbgl_ref[:, DS:] += jnp.sum(dgt, axis=0, keepdims=True)

    loc = lambda w: pl.BlockSpec((tr, w), lambda i: (i, 0))
    return pl.pallas_call(
        body, name=name, grid=(L // tr,),
        in_specs=[s["row"], s["pool"], s["attn"], s["glu"], s["p_gate"], s["a_gate"], s["s_gate"], s["bglu"], s["bg"]],
        out_specs=[loc(DP), loc(DA), loc(2 * DS), loc(DP), loc(DA), loc(DS), s["bg"], s["bglu"]],
        out_shape=[jax.ShapeDtypeStruct((L, DP), F32), jax.ShapeDtypeStruct((L, DA), F32),
                   jax.ShapeDtypeStruct((L, 2 * DS), BF16), jax.ShapeDtypeStruct((L, DP), BF16),
                   jax.ShapeDtypeStruct((L, DA), BF16), jax.ShapeDtypeStruct((L, DS), BF16),
                   jax.ShapeDtypeStruct((1, D), F32), jax.ShapeDtypeStruct((1, 2 * DS), F32)],
        compiler_params=_cparams(("arbitrary",), VMEM_BIG),
    )(dy, ypool, yattn, glu_pre, proj, proj, proj, b_glu, branch_g)


def _pool_select(g, s2, s4, s8, s16):
    return jnp.where(g == 0, s2, jnp.where(g == 1, s4, jnp.where(g == 2, s8, s16)))


def _pool_window(g):
    return jnp.where(g == 0, 2.0, jnp.where(g == 1, 4.0, jnp.where(g == 2, 8.0, 16.0))).astype(F32)


def _pooled_chunk(pad, g, r0, ch):
    xh = pad[pl.ds(r0, ch + POOL_HALO), :]
    s2 = xh + pltpu.roll(xh, 1, 0)
    s4 = s2 + pltpu.roll(s2, 2, 0)
    s8 = s4 + pltpu.roll(s4, 4, 0)
    s16 = s8 + pltpu.roll(s8, 8, 0)
    win = _pool_select(g, s2, s4, s8, s16)[POOL_HALO:]
    pos = (r0 + 1 + lax.broadcasted_iota(jnp.int32, (ch, 1), 0)).astype(F32)
    return win / jnp.minimum(pos, _pool_window(g)) - xh[POOL_HALO:]


def pool_fwd(name, proj, wp, scale):
    L = proj.shape[0]
    DP = scale.shape[1]
    PG = DP // N_POOL_GROUPS
    ch = _pick(L, 256)

    def body(x_ref, w_ref, s_ref, o_ref, pad):
        g = pl.program_id(0)
        pad[0:POOL_HALO, :] = jnp.zeros((POOL_HALO, PG), F32)
        pad[POOL_HALO:, :] = x_ref[...].astype(F32)

        def chunk(ci, carry):
            r0 = pl.multiple_of(ci * ch, ch)
            pooled = _pooled_chunk(pad, g, r0, ch)
            o_ref[pl.ds(r0, ch), :] = _dot(pooled.astype(BF16), w_ref[...]) * s_ref[...]
            return carry

        lax.fori_loop(0, L // ch, chunk, 0)

    return pl.pallas_call(
        body, name=name, grid=(N_POOL_GROUPS,),
        in_specs=[pl.BlockSpec((L, PG), lambda g: (0, g)), pl.BlockSpec((None, PG, PG), lambda g: (g, 0, 0)),
                  pl.BlockSpec((1, PG), lambda g: (0, g))],
        out_specs=pl.BlockSpec((L, PG), lambda g: (0, g)),
        out_shape=jax.ShapeDtypeStruct((L, DP), F32),
        scratch_shapes=[pltpu.VMEM((L + POOL_HALO, PG), F32)],
        compiler_params=_cparams(("parallel",), VMEM_MID))(proj, wp, scale)


def pool_bwd(name, dyraw, proj, wp, scale):
    L = proj.shape[0]
    DP = scale.shape[1]
    PG = DP // N_POOL_GROUPS
    ch = _pick(L, 256)

    def body(dy_ref, x_ref, w_ref, s_ref, dx_ref, dw_ref, ds_ref, pad, dpad, dpo):
        g = pl.program_id(0)
        pad[0:POOL_HALO, :] = jnp.zeros((POOL_HALO, PG), F32)
        pad[POOL_HALO:, :] = x_ref[...].astype(F32)
        dpad[L:, :] = jnp.zeros((POOL_HALO, PG), F32)
        dw_ref[...] = jnp.zeros_like(dw_ref)
        ds_ref[...] = jnp.zeros_like(ds_ref)
        wv = w_ref[...]
        win_f = _pool_window(g)

        def chunk(ci, carry):
            r0 = pl.multiple_of(ci * ch, ch)
            pooled = _pooled_chunk(pad, g, r0, ch).astype(BF16)
            dyv = dy_ref[pl.ds(r0, ch), :]
            ds_ref[...] += jnp.sum(dyv * _dot(pooled, wv), axis=0, keepdims=True)
            dmixed = (dyv * s_ref[...]).astype(BF16)
            dw_ref[...] += _dot(pooled, dmixed, TN)
            dpooled = _dot(dmixed, wv, NT)
            pos = (r0 + 1 + lax.broadcasted_iota(jnp.int32, (ch, 1), 0)).astype(F32)
            dpad[pl.ds(r0, ch), :] = dpooled / jnp.minimum(pos, win_f)
            dpo[pl.ds(r0, ch), :] = dpooled
            return carry

        lax.fori_loop(0, L // ch, chunk, 0)

        def chunk2(ci, carry):
            r0 = pl.multiple_of(ci * ch, ch)
            n = ch + POOL_HALO
            dm = dpad[pl.ds(r0, n), :]
            s2 = dm + pltpu.roll(dm, n - 1, 0)
            s4 = s2 + pltpu.roll(s2, n - 2, 0)
            s8 = s4 + pltpu.roll(s4, n - 4, 0)
            s16 = s8 + pltpu.roll(s8, n - 8, 0)
            win = _pool_select(g, s2, s4, s8, s16)[:ch]
            dx_ref[pl.ds(r0, ch), :] = (win - dpo[pl.ds(r0, ch), :]).astype(BF16)
            return carry

        lax.fori_loop(0, L // ch, chunk2, 0)

    col = pl.BlockSpec((L, PG), lambda g: (0, g))
    return pl.pallas_call(
        body, name=name, grid=(N_POOL_GROUPS,),
        in_specs=[col, col, pl.BlockSpec((None, PG, PG), lambda g: (g, 0, 0)), pl.BlockSpec((1, PG), lambda g: (0, g))],
        out_specs=[col, pl.BlockSpec((None, PG, PG), lambda g: (g, 0, 0)), pl.BlockSpec((1, PG), lambda g: (0, g))],
        out_shape=[jax.ShapeDtypeStruct((L, DP), BF16), jax.ShapeDtypeStruct((N_POOL_GROUPS, PG, PG), F32),
                   jax.ShapeDtypeStruct((1, DP), F32)],
        scratch_shapes=[pltpu.VMEM((L + POOL_HALO, PG), F32), pltpu.VMEM((L + POOL_HALO, PG), F32),
                        pltpu.VMEM((L, PG), F32)],
        compiler_params=_cparams(("parallel",), VMEM_MID))(dyraw, proj, wp, scale)


def _attn_tile(L):
    return _pick(L, ATTN_TILE)


def _tri(t, strict):
    j = lax.broadcasted_iota(jnp.int32, (t, t), 0)
    s = lax.broadcasted_iota(jnp.int32, (t, t), 1)
    return ((j > s) if strict else (j >= s)).astype(BF16)


def _attn_block(q, kt, rb, after, diagonal):
    tq, tk = q.shape[0], kt.shape[0]
    z = _dot(q, kt, NT)
    e = jnp.exp(-jnp.abs(z))
    l1p = jnp.log(1.0 + e)
    log_sig = jnp.minimum(z, 0.0) - l1p
    b = -jnp.maximum(z, 0.0) - l1p
    causal = None
    if diagonal:
        causal = lax.broadcasted_iota(jnp.int32, (tq, tk), 1) < lax.broadcasted_iota(jnp.int32, (tq, tk), 0)
        b = jnp.where(causal, b, 0.0)
    b_hi = b.astype(BF16)
    b_lo = (b - b_hi.astype(F32)).astype(BF16)
    suffix = _dot(b_hi, after) + _dot(b_lo, after) + rb
    w = jnp.exp(log_sig + suffix)
    if diagonal:
        w = jnp.where(causal, w, 0.0)
    return z, e, causal, b, w


def _attn_sweep(i, visit):
    go = visit(i, True)
    lax.while_loop(lambda c: jnp.logical_and(c[0] >= 0, c[1]),
                   lambda c: (c[0] - 1, visit(c[0], False)), (i - 1, go))


def attn_fwd(name, proj, D, copies=None):
    L = proj.shape[0]
    DA = D // 2
    H = DA // HEAD_DIM
    tq = tk = _attn_tile(L)
    qo, ko, vo = (D // 2) // HEAD_DIM, D // HEAD_DIM, (3 * D // 2) // HEAD_DIM
    scale = HEAD_DIM ** -0.5

    def body(q_ref, k_ref, v_ref, tri_ref, o_ref, kb_s, vb_s, acc, rb):
        i = pl.program_id(1)

        @pl.when(i == 0)
        def _():
            kb_s[...] = k_ref[...].astype(BF16)
            vb_s[...] = v_ref[...].astype(BF16)

        q = (q_ref[...].astype(F32) * scale).astype(BF16)
        acc[...] = jnp.zeros_like(acc)
        rb[...] = jnp.zeros_like(rb)

        def visit(kb, diagonal):
            k0 = pl.multiple_of(kb * tk, tk)
            kt = kb_s[pl.ds(k0, tk), :]
            vt = vb_s[pl.ds(k0, tk), :]
            _, _, _, b, w = _attn_block(q, kt, rb[...], tri_ref[...], diagonal)
            acc[...] += _dot(w.astype(BF16), vt)
            rbn = rb[...] + jnp.sum(b, axis=1, keepdims=True)
            rb[...] = rbn
            return jnp.max(rbn) > -ATTN_DECAY_CUTOFF

        _attn_sweep(i, visit)
        o_ref[...] = acc[...]

    (out,), landed = _call(
        body, name=name, grid=(H, L // tq),
        in_specs=[pl.BlockSpec((tq, HEAD_DIM), lambda h, i: (i, qo + h)),
                  pl.BlockSpec((L, HEAD_DIM), lambda h, i: (0, ko + h)),
                  pl.BlockSpec((L, HEAD_DIM), lambda h, i: (0, vo + h)),
                  pl.BlockSpec((tk, tk), lambda h, i: (0, 0))],
        out_specs=[pl.BlockSpec((tq, HEAD_DIM), lambda h, i: (i, h))],
        out_shape=[jax.ShapeDtypeStruct((L, DA), F32)],
        scratch_shapes=[pltpu.VMEM((L, HEAD_DIM), BF16), pltpu.VMEM((L, HEAD_DIM), BF16),
                        pltpu.VMEM((tq, HEAD_DIM), F32), pltpu.VMEM((tq, 1), F32)],
        vmem=VMEM_MID, args=(proj, proj, proj, _tri(tk, True)), semantics=("arbitrary", "arbitrary"),
        copies=copies)
    return out, landed


def attn_bwd(name, proj, o, do, D, copies=None):
    L = proj.shape[0]
    DA = D // 2
    H = DA // HEAD_DIM
    tq = tk = _attn_tile(L)
    qo, ko, vo = (D // 2) // HEAD_DIM, D // HEAD_DIM, (3 * D // 2) // HEAD_DIM
    scale = HEAD_DIM ** -0.5

    def body(q_ref, k_ref, v_ref, o_ref, do_ref, after_ref, from_ref, dq_ref, dk_ref, dv_ref,
             kb_s, vb_s, dk_s, dv_s, dq_acc, rb, rg):
        i = pl.program_id(1)
        nq = pl.num_programs(1)

        @pl.when(i == 0)
        def _():
            kb_s[...] = k_ref[...].astype(BF16)
            vb_s[...] = v_ref[...].astype(BF16)
            dk_s[...] = jnp.zeros_like(dk_s)
            dv_s[...] = jnp.zeros_like(dv_s)

        q = (q_ref[...].astype(F32) * scale).astype(BF16)
        dob = do_ref[...].astype(BF16)
        delta = jnp.sum(dob.astype(F32) * o_ref[...], axis=1, keepdims=True)
        dq_acc[...] = jnp.zeros_like(dq_acc)
        rb[...] = jnp.zeros_like(rb)
        rg[...] = jnp.zeros_like(rg)

        def visit(kb, diagonal):
            k0 = pl.multiple_of(kb * tk, tk)
            kt = kb_s[pl.ds(k0, tk), :]
            vt = vb_s[pl.ds(k0, tk), :]
            z, e, causal, b, w = _attn_block(q, kt, rb[...], after_ref[...], diagonal)
            wq = w.astype(BF16)
            dw = _dot(dob, vt, NT)
            g = wq.astype(F32) * dw
            g_hi = g.astype(BF16)
            g_lo = (g - g_hi.astype(F32)).astype(BF16)
            from_s = from_ref[...]
            suffix_g = _dot(g_hi, from_s) + _dot(g_lo, from_s) + rg[...]
            before = delta - suffix_g
            r = 1.0 / (1.0 + e)
            sig = jnp.where(z >= 0, r, e * r)
            sig_neg = jnp.where(z >= 0, e * r, r)
            dz = g * sig_neg - before * sig
            if diagonal:
                dz = jnp.where(causal, dz, 0.0)
            dz = dz.astype(BF16)
            dq_acc[...] += _dot(dz, kt)
            dk_s[pl.ds(k0, tk), :] += _dot(dz, q, TN)
            dv_s[pl.ds(k0, tk), :] += _dot(wq, dob, TN)
            rbn = rb[...] + jnp.sum(b, axis=1, keepdims=True)
            rb[...] = rbn
            rg[...] += jnp.sum(g, axis=1, keepdims=True)
            return jnp.max(rbn) > -ATTN_DECAY_CUTOFF

        _attn_sweep(i, visit)
        dq_ref[...] = (dq_acc[...] * scale).astype(BF16)

        @pl.when(i == nq - 1)
        def _():
            dk_ref[...] = dk_s[...].astype(BF16)
            dv_ref[...] = dv_s[...].astype(BF16)

    blk = pl.BlockSpec((tq, HEAD_DIM), lambda h, i: (i, h))
    full = pl.BlockSpec((L, HEAD_DIM), lambda h, i: (0, h))
    return _call(
        body, name=name, grid=(H, L // tq),
        in_specs=[pl.BlockSpec((tq, HEAD_DIM), lambda h, i: (i, qo + h)),
                  pl.BlockSpec((L, HEAD_DIM), lambda h, i: (0, ko + h)),
                  pl.BlockSpec((L, HEAD_DIM), lambda h, i: (0, vo + h)), blk, blk,
                  pl.BlockSpec((tk, tk), lambda h, i: (0, 0)), pl.BlockSpec((tk, tk), lambda h, i: (0, 0))],
        out_specs=[blk, full, full],
        out_shape=[jax.ShapeDtypeStruct((L, DA), BF16)] * 3,
        scratch_shapes=[pltpu.VMEM((L, HEAD_DIM), BF16), pltpu.VMEM((L, HEAD_DIM), BF16),
                        pltpu.VMEM((L, HEAD_DIM), F32), pltpu.VMEM((L, HEAD_DIM), F32),
                        pltpu.VMEM((tq, HEAD_DIM), F32), pltpu.VMEM((tq, 1), F32), pltpu.VMEM((tq, 1), F32)],
        vmem=VMEM_MID, args=(proj, proj, proj, o, do, _tri(tk, True), _tri(tk, False)),
        semantics=("arbitrary", "arbitrary"), copies=copies)


def _cmul(ar, ai, br, bi):
    return ar * br - ai * bi, ar * bi + ai * br


def _cmul_conj(ar, ai, br, bi):
    return ar * br + ai * bi, ar * bi - ai * br


def _ssm_disc(lr, li, ld):
    dt = jnp.exp(ld)
    m = jnp.exp(lr * dt)
    ar, ai = m * jnp.cos(li * dt), m * jnp.sin(li * dt)
    inv = 1.0 / (lr * lr + li * li)
    fr, fi = _cmul(ar - 1.0, ai, lr * inv, -li * inv)
    return dt, ar, ai, fr, fi, inv


def ssm_prep(name, lr, li, ld, br, bi):
    def body(lr_ref, li_ref, ld_ref, br_ref, bi_ref, zr_ref, zi_ref, bbr_ref, bbi_ref):
        dt, _, _, fr, fi, _ = _ssm_disc(lr_ref[...], li_ref[...], ld_ref[...])
        zr_ref[...] = lr_ref[...] * dt
        zi_ref[...] = li_ref[...] * dt
        bbr, bbi = _cmul(fr, fi, br_ref[...], bi_ref[...])
        bbr_ref[...] = bbr
        bbi_ref[...] = bbi

    sd = jax.ShapeDtypeStruct
    return pl.pallas_call(
        body, name=name,
        out_shape=[sd(lr.shape, F32), sd(lr.shape, F32), sd(br.shape, F32), sd(br.shape, F32)],
    )(lr, li, ld, br, bi)


def ssm_prep_bwd(name, lr, li, ld, br, bi, gar, gai, gbr, gbi):
    def body(lr_ref, li_ref, ld_ref, br_ref, bi_ref, gar_ref, gai_ref, gbr_ref, gbi_ref,
             dlr_ref, dli_ref, dld_ref, dbr_ref, dbi_ref):
        lr_, li_ = lr_ref[...], li_ref[...]
        dt, ar, ai, fr, fi, inv = _ssm_disc(lr_, li_, ld_ref[...])
        gbr_, gbi_ = gbr_ref[...], gbi_ref[...]
        dbr, dbi = _cmul_conj(fr, fi, gbr_, gbi_)
        dbr_ref[...] = dbr
        dbi_ref[...] = dbi
        pr, pi = _cmul_conj(br_ref[...], bi_ref[...], gbr_, gbi_)
        gfr = jnp.sum(pr, axis=1, keepdims=True)
        gfi = jnp.sum(pi, axis=1, keepdims=True)
        ilr, ili = lr_ * inv, -li_ * inv
        tr_, ti_ = _cmul_conj(ilr, ili, gfr, gfi)
        gatr, gati = gar_ref[...] + tr_, gai_ref[...] + ti_
        hr, hi = _cmul(fr, fi, ilr, ili)
        t1r, t1i = _cmul_conj(ar * dt, ai * dt, gatr, gati)
        t2r, t2i = _cmul_conj(hr, hi, gfr, gfi)
        dlr_ref[...] = t1r - t2r
        dli_ref[...] = t1i - t2i
        lar, lai = _cmul(lr_, li_, ar, ai)
        gdt, _ = _cmul_conj(lar, lai, gatr, gati)
        dld_ref[...] = jnp.sum(gdt, axis=2, keepdims=True) * dt

    sd = jax.ShapeDtypeStruct
    return pl.pallas_call(
        body, name=name,
        out_shape=[sd(lr.shape, F32), sd(lr.shape, F32), sd(ld.shape, F32), sd(br.shape, F32), sd(br.shape, F32)],
    )(lr, li, ld, br, bi, gar, gai, gbr, gbi)


SCAN_ROWS = 64


def _scan_rows(L):
    return min(SCAN_ROWS, L)


def _power_table(pr_s, pi_s, zr, zi, L, reverse):
    R = _scan_rows(L)
    row = lax.broadcasted_iota(jnp.int32, (R, 1), 0).astype(F32)
    dist = (R - row) if reverse else (row + 1.0)
    mag = jnp.exp(dist * zr)
    pr_s[...] = mag * jnp.cos(dist * zi)
    pi_s[...] = mag * jnp.sin(dist * zi)


def _scan(xr, xi, pr_s, pi_s, L, reverse):
    R = _scan_rows(L)
    nt = L // R
    assert L % R == 0 and R & (R - 1) == 0
    ns = CHUNK_X // LANE
    ridx = lax.broadcasted_iota(jnp.int32, (R, LANE), 0)

    def power(ref, d, cs):
        at = R - d if reverse else d - 1
        return ref[at:at + 1, cs]

    def shift(v, d):
        if d < 8:
            if reverse:
                return jnp.where(ridx < R - d, pltpu.roll(v, R - d, 0), 0.0)
            return jnp.where(ridx >= d, pltpu.roll(v, d, 0), 0.0)
        zeros = jnp.zeros((d, LANE), F32)
        return jnp.concatenate([v[d:], zeros], 0) if reverse else jnp.concatenate([zeros, v[:R - d]], 0)

    def tile(n, carry):
        t = nt - 1 - n if reverse else n
        rows = pl.ds(pl.multiple_of(t * R, R), R)
        edges = []
        for c in range(ns):
            cs = slice(c * LANE, (c + 1) * LANE)
            vr, vi = xr[rows, cs], xi[rows, cs]
            d = 1
            while d < R:
                ar, ai = power(pr_s, d, cs), power(pi_s, d, cs)
                sr, si = shift(vr, d), shift(vi, d)
                vr, vi = vr + ar * sr - ai * si, vi + ar * si + ai * sr
                d *= 2
            cr, ci = carry[2 * c], carry[2 * c + 1]
            pr, pi = pr_s[:, cs], pi_s[:, cs]
            vr, vi = vr + pr * cr - pi * ci, vi + pr * ci + pi * cr
            xr[rows, cs] = vr
            xi[rows, cs] = vi
            edge = slice(0, 1) if reverse else slice(R - 1, R)
            edges += [vr[edge], vi[edge]]
        return tuple(edges)

    lax.fori_loop(0, nt, tile, tuple(jnp.zeros((1, LANE), F32) for _ in range(2 * ns)))


def _gelu(x):
    t = jnp.tanh(0.7978845608028654 * (x + 0.044715 * x * x * x))
    return 0.5 * x * (1.0 + t)


def _gelu_grad(x):
    t = jnp.tanh(0.7978845608028654 * (x + 0.044715 * x * x * x))
    return 0.5 * (1.0 + t) + 0.5 * x * (1.0 - t * t) * 0.7978845608028654 * (1.0 + 0.134145 * x * x)


def _call(body, *, name, grid, in_specs, out_specs, out_shape, scratch_shapes, vmem, args, semantics,
          copies=None):
    n_i, n_o, n_s = len(in_specs), len(out_specs), len(scratch_shapes)
    if copies is None:
        out = pl.pallas_call(
            body, name=name, grid=grid, in_specs=in_specs, out_specs=out_specs, out_shape=out_shape,
            scratch_shapes=scratch_shapes, compiler_params=_cparams(semantics, vmem))(*args)
        return list(out), []
    n_ci, n_co = len(copies.inputs), len(copies.out_shape)

    def hosted(*refs):
        ins, cin = refs[:n_i], refs[n_i:n_i + n_ci]
        outs = refs[n_i + n_ci:n_i + n_ci + n_o]
        cout = refs[n_i + n_ci + n_o:n_i + n_ci + n_o + n_co]
        scr = refs[n_i + n_ci + n_o + n_co:n_i + n_ci + n_o + n_co + n_s]
        sems = refs[n_i + n_ci + n_o + n_co + n_s:]
        step = pl.program_id(0)
        for axis in range(1, len(grid)):
            step = step * grid[axis] + pl.program_id(axis)
        copies.emit(cin, cout, sems, step, math.prod(grid))
        body(*ins, *outs, *scr)

    out = pl.pallas_call(
        hosted, name=name, grid=grid, in_specs=list(in_specs) + [HBM] * n_ci,
        out_specs=list(out_specs) + [HBM] * n_co, out_shape=list(out_shape) + list(copies.out_shape),
        scratch_shapes=list(scratch_shapes) + list(copies.scratch),
        compiler_params=pltpu.CompilerParams(dimension_semantics=("arbitrary",) * len(grid),
                                             vmem_limit_bytes=vmem, has_side_effects=True))(*args, *copies.inputs)
    return list(out[:n_o]), list(out[n_o:])


def merge_copies(group):
    group = [c for c in group if c is not None]
    if len(group) <= 1:
        return group[0] if group else None
    bounds, i0, o0, s0 = [], 0, 0, 0
    for c in group:
        bounds.append((i0, o0, s0))
        i0, o0, s0 = i0 + len(c.inputs), o0 + len(c.out_shape), s0 + len(c.scratch)
    phases = []
    for c, (i, o, s) in zip(group, bounds):
        for frac, fn in c.phases:
            def shifted(ins, outs, sems, fn=fn, c=c, i=i, o=o, s=s):
                fn(ins[i:i + len(c.inputs)], outs[o:o + len(c.out_shape)], sems[s:s + len(c.scratch)])
            phases.append((frac, shifted))
    return HostedCopies([a for c in group for a in c.inputs], [a for c in group for a in c.out_shape],
                        [a for c in group for a in c.scratch], phases)


def ssm_fwd(name, proj, wbr, wbi, zr, zi, wcr, wci, dskip, D, copies=None):
    L = proj.shape[0]
    DS = D // 4
    NC = DS // CHUNK_U
    uo = (5 * D // 2) // CHUNK_U
    ch = _pick(L, 256)

    def body(u_ref, wbr_ref, wbi_ref, zr_ref, zi_ref, wcr_ref, wci_ref, ds_ref,
             y_ref, hg_ref, xr_ref, xi_ref, sr, si, pr_s, pi_s):
        def fill(ci, carry):
            rows = pl.ds(pl.multiple_of(ci * ch, ch), ch)
            ub = u_ref[rows, :].astype(BF16)
            sr[rows, :] = _dot(ub, wbr_ref[...])
            si[rows, :] = _dot(ub, wbi_ref[...])
            return carry

        lax.fori_loop(0, L // ch, fill, 0)
        _power_table(pr_s, pi_s, zr_ref[...], zi_ref[...], L, reverse=False)
        _scan(sr, si, pr_s, pi_s, L, reverse=False)

        def emit(ci, carry):
            rows = pl.ds(pl.multiple_of(ci * ch, ch), ch)
            xrb, xib = sr[rows, :].astype(BF16), si[rows, :].astype(BF16)
            xr_ref[rows, :] = xrb
            xi_ref[rows, :] = xib
            y = _dot(xrb, wcr_ref[...]) - _dot(xib, wci_ref[...]) + ds_ref[...] * u_ref[rows, :].astype(F32)
            y_ref[rows, :] = y
            hg_ref[rows, :] = _gelu(y).astype(BF16)
            return carry

        lax.fori_loop(0, L // ch, emit, 0)

    ucol = pl.BlockSpec((L, CHUNK_U), lambda k: (0, k))
    xcol = pl.BlockSpec((L, CHUNK_X), lambda k: (0, k))
    sd = jax.ShapeDtypeStruct
    return _call(
        body, name=name, grid=(NC,),
        in_specs=[pl.BlockSpec((L, CHUNK_U), lambda k: (0, uo + k)),
                  pl.BlockSpec((None, CHUNK_U, CHUNK_X), lambda k: (k, 0, 0)),
                  pl.BlockSpec((None, CHUNK_U, CHUNK_X), lambda k: (k, 0, 0)),
                  pl.BlockSpec((1, CHUNK_X), lambda k: (0, k)), pl.BlockSpec((1, CHUNK_X), lambda k: (0, k)),
                  pl.BlockSpec((None, CHUNK_X, CHUNK_U), lambda k: (k, 0, 0)),
                  pl.BlockSpec((None, CHUNK_X, CHUNK_U), lambda k: (k, 0, 0)),
                  pl.BlockSpec((1, CHUNK_U), lambda k: (0, k))],
        out_specs=[ucol, ucol, xcol, xcol],
        out_shape=[sd((L, DS), F32), sd((L, DS), BF16), sd((L, 4 * DS), BF16), sd((L, 4 * DS), BF16)],
        scratch_shapes=[pltpu.VMEM((L, CHUNK_X), F32), pltpu.VMEM((L, CHUNK_X), F32),
                        pltpu.VMEM((_scan_rows(L), CHUNK_X), F32), pltpu.VMEM((_scan_rows(L), CHUNK_X), F32)],
        vmem=VMEM_BIG, args=(proj, wbr, wbi, zr, zi, wcr, wci, dskip), semantics=("parallel",), copies=copies)


def ssm_bwd(name, dhg, ypre, proj, xr, xi, wbr, wbi, zr, zi, wcr, wci, dskip, D, copies=None):
    L = proj.shape[0]
    DS = D // 4
    NC = DS // CHUNK_U
    uo = (5 * D // 2) // CHUNK_U
    ch = _pick(L, 256)
    nch = L // ch
    halo = 16

    def body(dhg_ref, y_ref, u_ref, xr_ref, xi_ref, wbr_ref, wbi_ref, zr_ref, zi_ref, wcr_ref, wci_ref,
             ds_ref, du_ref, dwcr_ref, dwci_ref, dwbr_ref, dwbi_ref, dar_ref, dai_ref, dds_ref,
             gr, gi, duf, pr_s, pi_s):
        dwcr_ref[...] = jnp.zeros_like(dwcr_ref)
        dwci_ref[...] = jnp.zeros_like(dwci_ref)
        dwbr_ref[...] = jnp.zeros_like(dwbr_ref)
        dwbi_ref[...] = jnp.zeros_like(dwbi_ref)
        dar_ref[...] = jnp.zeros_like(dar_ref)
        dai_ref[...] = jnp.zeros_like(dai_ref)
        dds_ref[...] = jnp.zeros_like(dds_ref)

        def first(ci, carry):
            rows = pl.ds(pl.multiple_of(ci * ch, ch), ch)
            dy = dhg_ref[rows, :] * _gelu_grad(y_ref[rows, :])
            dyb = dy.astype(BF16)
            dds_ref[...] += jnp.sum(dy * u_ref[rows, :].astype(F32), axis=0, keepdims=True)
            duf[rows, :] = ds_ref[...] * dy
            gr[rows, :] = _dot(dyb, wcr_ref[...], NT)
            gi[rows, :] = -_dot(dyb, wci_ref[...], NT)
            dwcr_ref[...] += _dot(xr_ref[rows, :], dyb, TN)
            dwci_ref[...] -= _dot(xi_ref[rows, :], dyb, TN)
            return carry

        lax.fori_loop(0, nch, first, 0)
        _power_table(pr_s, pi_s, zr_ref[...], -zi_ref[...], L, reverse=True)
        _scan(gr, gi, pr_s, pi_s, L, reverse=True)

        def lam_grad(gxr, gxi, xpr, xpi):
            pr, pi = _cmul_conj(xpr, xpi, gxr, gxi)
            dar_ref[...] += jnp.sum(pr, axis=0, keepdims=True)
            dai_ref[...] += jnp.sum(pi, axis=0, keepdims=True)

        def second(ci, carry):
            r0 = pl.multiple_of(ci * ch, ch)
            rows = pl.ds(r0, ch)
            gxr, gxi = gr[rows, :], gi[rows, :]
            gxrb, gxib = gxr.astype(BF16), gxi.astype(BF16)
            du_ref[rows, :] = (duf[rows, :] + _dot(gxrb, wbr_ref[...], NT) + _dot(gxib, wbi_ref[...], NT)).astype(BF16)
            ub = u_ref[rows, :].astype(BF16)
            dwbr_ref[...] += _dot(ub, gxrb, TN)
            dwbi_ref[...] += _dot(ub, gxib, TN)
            return carry

        lax.fori_loop(0, nch, second, 0)

        ridx = lax.broadcasted_iota(jnp.int32, (ch, CHUNK_X), 0)
        xpr = jnp.where(ridx >= 1, pltpu.roll(xr_ref[0:ch, :].astype(F32), 1, 0), 0.0)
        xpi = jnp.where(ridx >= 1, pltpu.roll(xi_ref[0:ch, :].astype(F32), 1, 0), 0.0)
        lam_grad(gr[0:ch, :], gi[0:ch, :], xpr, xpi)

        def third(ci, carry):
            r0 = pl.multiple_of(ci * ch, ch)
            ext = pl.ds(pl.multiple_of(r0 - halo, halo), ch + halo)
            xpr = pltpu.roll(xr_ref[ext, :].astype(F32), 1, 0)[halo:]
            xpi = pltpu.roll(xi_ref[ext, :].astype(F32), 1, 0)[halo:]
            lam_grad(gr[pl.ds(r0, ch), :], gi[pl.ds(r0, ch), :], xpr, xpi)
            return carry

        if nch > 1:
            lax.fori_loop(1, nch, third, 0)

    ucol = pl.BlockSpec((L, CHUNK_U), lambda k: (0, k))
    xcol = pl.BlockSpec((L, CHUNK_X), lambda k: (0, k))
    wb_spec = pl.BlockSpec((None, CHUNK_U, CHUNK_X), lambda k: (k, 0, 0))
    wc_spec = pl.BlockSpec((None, CHUNK_X, CHUNK_U), lambda k: (k, 0, 0))
    avec = pl.BlockSpec((1, CHUNK_X), lambda k: (0, k))
    uvec = pl.BlockSpec((1, CHUNK_U), lambda k: (0, k))
    sd = jax.ShapeDtypeStruct
    return _call(
        body, name=name, grid=(NC,),
        in_specs=[ucol, ucol, pl.BlockSpec((L, CHUNK_U), lambda k: (0, uo + k)), xcol, xcol,
                  wb_spec, wb_spec, avec, avec, wc_spec, wc_spec, uvec],
        out_specs=[ucol, wc_spec, wc_spec, wb_spec, wb_spec, avec, avec, uvec],
        out_shape=[sd((L, DS), BF16), sd((NC, CHUNK_X, CHUNK_U), F32), sd((NC, CHUNK_X, CHUNK_U), F32),
                   sd((NC, CHUNK_U, CHUNK_X), F32), sd((NC, CHUNK_U, CHUNK_X), F32),
                   sd((1, 4 * DS), F32), sd((1, 4 * DS), F32), sd((1, DS), F32)],
        scratch_shapes=[pltpu.VMEM((L, CHUNK_X), F32), pltpu.VMEM((L, CHUNK_X), F32), pltpu.VMEM((L, CHUNK_U), F32),
                        pltpu.VMEM((_scan_rows(L), CHUNK_X), F32), pltpu.VMEM((_scan_rows(L), CHUNK_X), F32)],
        vmem=VMEM_BIG, args=(dhg, ypre, proj, xr, xi, wbr, wbi, zr, zi, wcr, wci, dskip),
        semantics=("parallel",), copies=copies)


def _block_diag(w, transpose):
    G = w.shape[0]
    nc = G // GROUPS_PER_CHUNK
    w4 = w.reshape(nc, GROUPS_PER_CHUNK, SSM_GROUP, SSM_STATE)
    eye = jnp.eye(GROUPS_PER_CHUNK, dtype=w.dtype)
    if transpose:
        return (w4[:, None, :, :, :].transpose(0, 1, 4, 2, 3) * eye[None, :, None, :, None]).reshape(
            nc, CHUNK_X, CHUNK_U).astype(BF16)
    return (w4[:, :, :, None, :] * eye[None, :, None, :, None]).reshape(nc, CHUNK_U, CHUNK_X).astype(BF16)


def _diag_blocks(dw, transpose):
    nc = dw.shape[0]
    gpc = GROUPS_PER_CHUNK
    eye = jnp.eye(gpc, dtype=dw.dtype)
    if transpose:
        d5 = dw.reshape(nc, gpc, SSM_STATE, gpc, SSM_GROUP)
        kept = jnp.sum(d5 * eye[None, :, None, :, None], axis=1)
        return kept.transpose(0, 2, 3, 1).reshape(nc * gpc, SSM_GROUP, SSM_STATE)
    d5 = dw.reshape(nc, gpc, SSM_GROUP, gpc, SSM_STATE)
    kept = jnp.sum(d5 * eye[None, :, None, :, None], axis=3)
    return kept.reshape(nc * gpc, SSM_GROUP, SSM_STATE)


SHARD_BLOCK_ELEMS = 128 * 1024


def _shard_rows(R, C, scale):
    return _pick(R, max(8, scale * SHARD_BLOCK_ELEMS // C))


def cast_bf16(name, w, layer, parts=1):
    shape = w.shape[1:]
    w3 = w.reshape(w.shape[0], -1, shape[-1])
    _, R, C = w3.shape
    tr = _shard_rows(R, C, 4)
    cw = C // parts

    def body(w_ref, *o_refs):
        for g, o_ref in enumerate(o_refs):
            o_ref[...] = w_ref[:, g * cw:(g + 1) * cw].astype(BF16)

    out = pl.pallas_call(body, name=name, grid=(R // tr,),
                         in_specs=[pl.BlockSpec((None, tr, C), lambda i: (layer, i, 0))],
                         out_specs=[pl.BlockSpec((tr, cw), lambda i: (i, 0))] * parts,
                         out_shape=[jax.ShapeDtypeStruct((R, cw), BF16)] * parts,
                         compiler_params=_cparams(("parallel",), VMEM_MID))(w3)
    return [o.reshape(shape[:-1] + (cw,)) for o in out]


def _adamw(w, g, m, v):
    m = ADAM_B1 * m + (1.0 - ADAM_B1) * g
    v = ADAM_B2 * v + (1.0 - ADAM_B2) * (g * g)
    delta = -ADAM_LR * ((m * ADAM_C1) / (jnp.sqrt(v * ADAM_C2) + ADAM_EPS) + ADAM_WD * w)
    return delta, m, v


def _own_core(g4):
    return (lambda p: p[0]) if g4.shape[1] == 2 else (lambda p: 0)


def chip_partial(name, pos, g4, recv_a):
    _, _, R, C = g4.shape
    tr = _shard_rows(R, C, 4)
    core = _own_core(g4)

    def body(pos_ref, g_ref, a_ref, o_ref):
        o_ref[...] = (g_ref[...] + a_ref[...]).astype(BF16)

    return pl.pallas_call(
        body, name=name,
        grid_spec=pltpu.PrefetchScalarGridSpec(
            num_scalar_prefetch=1, grid=(4, R // tr),
            in_specs=[pl.BlockSpec((None, None, tr, C), lambda q, i, p: (q, core(p), i, 0)),
                      pl.BlockSpec((None, tr, C), lambda q, i, p: (q, i, 0))],
            out_specs=pl.BlockSpec((None, tr, C), lambda q, i, p: (q, i, 0))),
        out_shape=jax.ShapeDtypeStruct((4, R, C), BF16),
        compiler_params=_cparams(("parallel", "parallel"), VMEM_MID))(pos, g4, recv_a)


def adamw_shard(name, pos, layer, g4, recv_a, recv_b, w, m, v, prev, copies=None):
    _, _, R, C = g4.shape
    tr = _shard_rows(R, C, 1)
    n_prev = 0 if prev is None else 4
    n_ci, n_co = (len(copies.inputs), len(copies.out_shape)) if copies else (0, 0)
    core = _own_core(g4)

    def body(pos_ref, g_ref, a_ref, b_ref, w_ref, m_ref, v_ref, *rest):
        rest = rest[n_prev:]
        go_ref, d_ref, mo_ref, vo_ref = rest[n_ci:n_ci + 4]
        if copies is not None:
            copies.emit(rest[:n_ci], rest[n_ci + 4:n_ci + 4 + n_co], rest[n_ci + 4 + n_co:],
                        pl.program_id(0), R // tr)
        gs = g_ref[...] + a_ref[...]
        for j in range(3):
            gs = gs + b_ref[j].astype(F32)
        delta, mn, vn = _adamw(w_ref[...], gs, m_ref[...], v_ref[...])
        go_ref[...] = gs
        d_ref[...] = delta
        mo_ref[...] = mn
        vo_ref[...] = vn

    lay = pl.BlockSpec((None, tr, C), lambda i, p: (layer, i, 0))
    in_specs = [pl.BlockSpec((None, None, tr, C), lambda i, p: (p[1], core(p), i, 0)),
                pl.BlockSpec((None, tr, C), lambda i, p: (p[1], i, 0)),
                pl.BlockSpec((3, tr, C), lambda i, p: (0, i, 0)), lay, lay, lay]
    args = [g4, recv_a, recv_b, w, m, v]
    aliases = {}
    if prev is not None:
        in_specs += [pl.BlockSpec(memory_space=pl.ANY)] * 4
        args += list(prev)
        aliases = {7 + j: j for j in range(4)}
    out = pl.pallas_call(
        body, name=name,
        grid_spec=pltpu.PrefetchScalarGridSpec(
            num_scalar_prefetch=1, grid=(R // tr,), in_specs=in_specs + [HBM] * n_ci,
            out_specs=[lay] * 4 + [HBM] * n_co, scratch_shapes=list(copies.scratch) if copies else []),
        out_shape=[jax.ShapeDtypeStruct(w.shape, F32)] * 4 + list(copies.out_shape if copies else []),
        input_output_aliases=aliases,
        compiler_params=pltpu.CompilerParams(
            dimension_semantics=("arbitrary",) if copies else ("parallel",), vmem_limit_bytes=VMEM_MID,
            has_side_effects=copies is not None))(pos, *args, *(copies.inputs if copies else []))
    return list(out[:4]), list(out[4:])


def adamw_small(name, gathered, head, w, m, v):
    _, R, C = gathered.shape
    E = head.shape[1]
    tr = _pick(R, 512)
    assert E % 8 == 0 and E <= tr

    def body(g_ref, h_ref, w_ref, m_ref, v_ref, go_ref, d_ref, mo_ref, vo_ref):
        gs, hs = g_ref[0], h_ref[0]
        for j in range(1, N_DEV):
            gs = gs + g_ref[j]
            hs = hs + h_ref[j]
        hs = jnp.where(pl.program_id(0) == 0, hs, 0.0)
        gs = jnp.concatenate([gs[:E] + hs, gs[E:]], axis=0)
        delta, mn, vn = _adamw(w_ref[...], gs, m_ref[...], v_ref[...])
        go_ref[...] = gs
        d_ref[...] = delta
        mo_ref[...] = mn
        vo_ref[...] = vn

    spec = pl.BlockSpec((tr, C), lambda i: (i, 0))
    return pl.pallas_call(
        body, name=name, grid=(R // tr,),
        in_specs=[pl.BlockSpec((N_DEV, tr, C), lambda i: (0, i, 0)),
                  pl.BlockSpec((N_DEV, E, C), lambda i: (0, 0, 0)), spec, spec, spec], out_specs=[spec] * 4,
        out_shape=[jax.ShapeDtypeStruct((R, C), F32)] * 4,
        compiler_params=_cparams(("parallel",), VMEM_MID))(gathered, head, w, m, v)


def _position():
    return lax.axis_index("x"), lax.axis_index("y"), lax.axis_index("c")


FORWARD_AT = 0.88


def gather_copies(shards):
    n = len(shards)

    def parts(ins, outs, sems):
        send_sems, recv_sems, local_sems = sems
        x, y, c = _position()
        me, sibling = (x, y, c), (x, y, 1 - c)
        chips = [(1 - x, y), (x, 1 - y), (1 - x, 1 - y)]

        def copy(a, k, block, to, src=None):
            blk = outs[a].at[4 * block[0] + 2 * block[1] + block[2]]
            return pltpu.make_async_remote_copy(
                src_ref=blk if src is None else src, dst_ref=blk,
                send_sem=send_sems.at[a, k], recv_sem=recv_sems.at[a, k], device_id=to, device_id_type=MESH)

        mine = [pltpu.make_async_copy(ins[a], outs[a].at[4 * x + 2 * y + c], local_sems.at[a]) for a in range(n)]
        first = [[copy(a, 0, me, sibling, src=ins[a])] +
                 [copy(a, 1 + j, me, (*chip, c), src=ins[a]) for j, chip in enumerate(chips)] for a in range(n)]
        landed = [[copy(a, 1 + j, (*chip, c), me) for j, chip in enumerate(chips)] for a in range(n)]
        passed = [[copy(a, 4 + j, (*chip, c), sibling) for j, chip in enumerate(chips)] for a in range(n)]
        from_sibling = [[copy(a, 0, sibling, me)] +
                        [copy(a, 4 + j, (*chip, 1 - c), me) for j, chip in enumerate(chips)] for a in range(n)]
        return mine, first, landed, passed, from_sibling

    def start(ins, outs, sems):
        mine, first, _, _, _ = parts(ins, outs, sems)
        for a in range(n):
            mine[a].start()
            for cp in first[a]:
                cp.start()

    def forward(ins, outs, sems):
        _, _, landed, passed, _ = parts(ins, outs, sems)
        for a in range(n):
            for j in range(3):
                landed[a][j].wait_recv()
                passed[a][j].start()

    def finish(ins, outs, sems):
        mine, first, _, passed, from_sibling = parts(ins, outs, sems)
        for a in range(n):
            for cp in from_sibling[a]:
                cp.wait_recv()
        for a in range(n):
            for cp in first[a] + passed[a]:
                cp.wait_send()
            mine[a].wait()

    return HostedCopies(
        list(shards), [jax.ShapeDtypeStruct((N_DEV,) + s.shape, s.dtype) for s in shards],
        [pltpu.SemaphoreType.DMA((n, 7)), pltpu.SemaphoreType.DMA((n, 7)), pltpu.SemaphoreType.DMA((n,))],
        [(0.0, start), (FORWARD_AT, forward), (1.0, finish)])


def _exchange_copies(arrays, out_lead, make):
    n = len(arrays)

    def all_copies(ins, outs, sems):
        send_sems, recv_sems = sems
        return [make(ins[a], outs[a], send_sems.at[a, k], recv_sems.at[a, k], k)
                for a in range(n) for k in range(out_lead)]

    def start(ins, outs, sems):
        for cp in all_copies(ins, outs, sems):
            cp.start()

    def finish(ins, outs, sems):
        for cp in all_copies(ins, outs, sems):
            cp.wait()

    return HostedCopies(
        list(arrays), [jax.ShapeDtypeStruct((out_lead,) + a.shape[2:], a.dtype) for a in arrays],
        [pltpu.SemaphoreType.DMA((n, out_lead)), pltpu.SemaphoreType.DMA((n, out_lead))],
        [(0.0, start), (1.0, finish)])


def sibling_copies(grads):
    def make(src, dst, send_sem, recv_sem, q):
        x, y, c = _position()
        core = 1 - c if src.shape[1] == 2 else 0
        return pltpu.make_async_remote_copy(
            src_ref=src.at[q, core], dst_ref=dst.at[q], send_sem=send_sem, recv_sem=recv_sem,
            device_id=(x, y, 1 - c), device_id_type=MESH)

    return _exchange_copies(grads, 4, make)


def chip_copies(parts):
    def make(src, dst, send_sem, recv_sem, j):
        x, y, c = _position()
        chip = [(1 - x, y), (x, 1 - y), (1 - x, 1 - y)][j]
        return pltpu.make_async_remote_copy(
            src_ref=src.at[2 * chip[0] + chip[1], 0], dst_ref=dst.at[j], send_sem=send_sem, recv_sem=recv_sem,
            device_id=(*chip, c), device_id_type=MESH)

    return _exchange_copies(parts, 3, make)


class Carrier:
    def __init__(self):
        self.plan = {}
        self.counts = {}

    def ride(self, site, make, store):
        self.plan.setdefault(site, []).append((make, store))

    def make(self, site, ctx=None):
        if site not in self.plan:
            return None
        group = [make(ctx) for make, _ in self.plan[site]]
        self.counts[site] = [len(c.out_shape) for c in group]
        return merge_copies(group)

    def store(self, site, results):
        if site in self.plan:
            at = 0
            for (_, store), n in zip(self.plan[site], self.counts[site]):
                store(results[at:at + n])
                at += n

    def split(self, site, out):
        if site not in self.plan:
            return out
        self.store(site, out[1])
        return out[0]


def _pool_weight(gathered):
    PG = gathered.shape[-1]
    return gathered.transpose(1, 0, 2, 3).reshape(N_POOL_GROUPS, PG, PG)


def _layer_params(l, ln_g, pool_scale, lam_re, lam_im, log_dt, b_re, b_im, c_re, c_im,
                  d_skip, b_glu, branch_g):
    G, P = lam_re.shape[1:]
    p = dict(
        ln_g=ln_g[l][None, :], pool_scale=pool_scale[l][None, :], d_skip=d_skip[l][None, :],
        b_glu=b_glu[l][None, :], branch_g=branch_g[l][None, :],
        lr=lam_re[l].reshape(G, 1, P), li=lam_im[l].reshape(G, 1, P), ld=log_dt[l].reshape(G, 1, 1),
        br=b_re[l].transpose(0, 2, 1), bi=b_im[l].transpose(0, 2, 1), cr=c_re[l], ci=c_im[l])
    return p


def layer_fwd(l, x, p, gw, D, carrier, staged_parts=0):
    t = f"l{l}_"
    h = rms_fwd(t + "rms_fwd", x, p["ln_g"])
    if staged_parts:
        proj = None
        for g in range(staged_parts):
            site = (l, f"proj{g}")
            proj = carrier.split(site, mm_nn_part(t + f"proj{g}", h, gw("w_in", l, g), g, staged_parts,
                                                  into=proj, out_dtype=BF16, copies=carrier.make(site)))
    else:
        site = (l, "proj")
        proj = carrier.split(site, mm_nn_gathered(t + "proj", h, gw("w_in", l), out_dtype=BF16,
                                                  copies=carrier.make(site)))
    wp = _pool_weight(gw("w_pool", l)[0])
    ypool = pool_fwd(t + "pool_fwd", proj, wp, p["pool_scale"])
    site = (l, "attn_fwd")
    yattn, landed = attn_fwd(t + "attn_fwd", proj, D, copies=carrier.make(site))
    carrier.store(site, landed)
    zr, zi, bbr, bbi = ssm_prep(t + "ssm_prep", p["lr"], p["li"], p["ld"], p["br"], p["bi"])
    ssm_w = dict(wbr=_block_diag(bbr, False), wbi=_block_diag(bbi, False),
                 zr=zr.reshape(1, -1), zi=zi.reshape(1, -1),
                 wcr=_block_diag(p["cr"], True), wci=_block_diag(p["ci"], True))
    site = (l, "ssm_fwd")
    (ypre, hg, xr, xi), landed = ssm_fwd(
        t + "ssm_fwd", proj, ssm_w["wbr"], ssm_w["wbi"], ssm_w["zr"], ssm_w["zi"],
        ssm_w["wcr"], ssm_w["wci"], p["d_skip"], D, copies=carrier.make(site))
    carrier.store(site, landed)
    glu_pre = mm_nn_gathered(t + "glu", hg, gw("w_glu", l))
    y = branch_fwd(t + "branch_fwd", ypool, yattn, glu_pre, proj, p["b_glu"], p["branch_g"])
    out = mm_plain(t + "out", y, gw("w_out", l)[0].reshape(D, D), NN, res=x)
    saved = dict(x=x, h=h, proj=proj, ypool=ypool, yattn=yattn, ypre=ypre, hg=hg, xr=xr, xi=xi,
                 glu_pre=glu_pre, y=y, ssm_w=ssm_w, wp=wp)
    return out, saved


def layer_bwd(l, dres, dres_b, s, p, gw, D, carrier, pos, split_w_in):
    t = f"l{l}_"
    proj = s["proj"]

    def by_target(g):
        return g.reshape(4, 2, -1, g.shape[-1])

    big = {}
    w_out_g = gw("w_out", l)[0].reshape(D, D)
    site = (l, "dy")
    dy = carrier.split(site, mm_plain(t + "dy", dres_b, w_out_g, NT, copies=carrier.make(site)))
    big["w_out"] = by_target(mm_plain(t + "dw_out", s["y"], dres_b, TN).reshape(N_DEV, D // N_DEV, D))
    dypool, dyattn, dglu, dpg, dag, dsg, dbg, dbglu = branch_bwd(
        t + "branch_bwd", dy, s["ypool"], s["yattn"], s["glu_pre"], proj, p["b_glu"], p["branch_g"])
    dhg = mm_nt_gathered(t + "dhg", dglu, gw("w_glu", l))
    big["w_glu"] = by_target(mm_tn_scattered(t + "dw_glu", s["hg"], dglu))
    w = s["ssm_w"]
    site = (l, "ssm_bwd")
    (du, dwcr, dwci, dwbr, dwbi, dar, dai, dds), landed = ssm_bwd(
        t + "ssm_bwd", dhg, s["ypre"], proj, s["xr"], s["xi"], w["wbr"], w["wbi"], w["zr"], w["zi"],
        w["wcr"], w["wci"], p["d_skip"], D, copies=carrier.make(site, big))
    carrier.store(site, landed)
    G, _, P = p["lr"].shape
    dlr, dli, dld, dbr, dbi = ssm_prep_bwd(
        t + "ssm_prep_bwd", p["lr"], p["li"], p["ld"], p["br"], p["bi"],
        dar.reshape(G, 1, P), dai.reshape(G, 1, P), _diag_blocks(dwbr, False), _diag_blocks(dwbi, False))
    site = (l, "attn_bwd")
    (dq, dk, dv), landed = attn_bwd(t + "attn_bwd", proj, s["yattn"], dyattn, D, copies=carrier.make(site, big))
    carrier.store(site, landed)
    dxp, dwp, dps = pool_bwd(t + "pool_bwd", dypool, proj, s["wp"], p["pool_scale"])
    dproj = jnp.concatenate([dxp, dpg, dq, dk, dv, dag, du, dsg], axis=1)
    PG = dwp.shape[1]
    big["w_pool"] = by_target(dwp.reshape(N_POOL_GROUPS, N_DEV, PG // N_DEV, PG).transpose(1, 0, 2, 3))
    small = dict(pool_scale=dps[0], lam_re=dlr.reshape(G, P), lam_im=dli.reshape(G, P),
                 log_dt=dld.reshape(G), b_re=dbr.transpose(0, 2, 1), b_im=dbi.transpose(0, 2, 1),
                 c_re=_diag_blocks(dwcr, True), c_im=_diag_blocks(dwci, True),
                 d_skip=dds[0], b_glu=dbglu[0], branch_g=dbg[0])
    if split_w_in:
        site = (l, "dw_in_a")
        to_sibling = carrier.split(site, mm_tn_half(t + "dw_in_a", s["h"], dproj, pos, False,
                                                    copies=carrier.make(site, big)))
        site = (l, "dw_in_b")
        mine = carrier.split(site, mm_tn_half(t + "dw_in_b", s["h"], dproj, pos, True,
                                              copies=carrier.make(site, dict(big, to_sibling=to_sibling[:, None], small=small))))
        big["w_in"] = mine[:, None]
    else:
        big["w_in"] = by_target(mm_tn_scattered(t + "dw_in", s["h"], dproj))
    site = (l, "dh")
    dh = carrier.split(site, mm_nt_gathered(t + "dh", dproj, gw("w_in", l), copies=carrier.make(site, big)))
    dx, dx_b, dlng = rms_bwd(t + "rms_bwd", s["x"], dh, dres, p["ln_g"])
    small["ln_g"] = dlng[0]
    return dx, dx_b, small


SMALL_NAMES = ("ln_g", "pool_scale", "lam_re", "lam_im", "log_dt", "b_re", "b_im", "c_re", "c_im",
               "d_skip", "b_glu", "branch_g", "final_g")
BIG_NAMES = ("w_in", "w_pool", "w_glu", "w_out")
WEIGHT_ORDER = ("ln_g", "w_in", "w_pool", "pool_scale", "lam_re", "lam_im", "log_dt", "b_re", "b_im",
                "c_re", "c_im", "d_skip", "w_glu", "b_glu", "branch_g", "w_out", "final_g")


PACK_ROWS = 512


def _pack(arrs):
    flat = jnp.concatenate([a.reshape(-1) for a in arrs])
    pad = (-flat.shape[0]) % (PACK_ROWS * LANE)
    return jnp.pad(flat, (0, pad)).reshape(-1, LANE)


def _unpack(packed, like):
    flat = packed.reshape(-1)
    out, off = [], 0
    for a in like:
        out.append(flat[off:off + a.size].reshape(a.shape))
        off += a.size
    return out


def kernel(x, ln_g, w_in, w_pool, pool_scale, lam_re, lam_im, log_dt, b_re, b_im, c_re, c_im, d_skip, w_glu, b_glu, branch_g, w_out, final_g, loss_target, m_ln_g, m_w_in, m_w_pool, m_pool_scale, m_lam_re, m_lam_im, m_log_dt, m_b_re, m_b_im, m_c_re, m_c_im, m_d_skip, m_w_glu, m_b_glu, m_branch_g, m_w_out, m_final_g, v_ln_g, v_w_in, v_w_pool, v_pool_scale, v_lam_re, v_lam_im, v_log_dt, v_b_re, v_b_im, v_c_re, v_c_im, v_d_skip, v_w_glu, v_b_glu, v_branch_g, v_w_out, v_final_g):
    W = dict(ln_g=ln_g, w_in=w_in, w_pool=w_pool, pool_scale=pool_scale, lam_re=lam_re, lam_im=lam_im,
             log_dt=log_dt, b_re=b_re, b_im=b_im, c_re=c_re, c_im=c_im, d_skip=d_skip, w_glu=w_glu,
             b_glu=b_glu, branch_g=branch_g, w_out=w_out, final_g=final_g)
    Mo = dict(ln_g=m_ln_g, w_in=m_w_in, w_pool=m_w_pool, pool_scale=m_pool_scale, lam_re=m_lam_re,
              lam_im=m_lam_im, log_dt=m_log_dt, b_re=m_b_re, b_im=m_b_im, c_re=m_c_re, c_im=m_c_im,
              d_skip=m_d_skip, w_glu=m_w_glu, b_glu=m_b_glu, branch_g=m_branch_g, w_out=m_w_out,
              final_g=m_final_g)
    Vo = dict(ln_g=v_ln_g, w_in=v_w_in, w_pool=v_w_pool, pool_scale=v_pool_scale, lam_re=v_lam_re,
              lam_im=v_lam_im, log_dt=v_log_dt, b_re=v_b_re, b_im=v_b_im, c_re=v_c_re, c_im=v_c_im,
              d_skip=v_d_skip, w_glu=v_w_glu, b_glu=v_b_glu, branch_g=v_branch_g, w_out=v_w_out,
              final_g=v_final_g)
    depth = ln_g.shape[0]
    _, L, D = x.shape
    xc, yc, cc = _position()
    pos = jnp.stack([cc, 2 * xc + yc, 4 * xc + 2 * yc + cc]).astype(jnp.int32)

    def n_parts(n, l):
        return (W_IN_PARTS if l > 0 else W_IN0_PARTS) if n == "w_in" else 1

    shards, landed = {}, {}
    for n in BIG_NAMES:
        for l in range(depth):
            for g, part in enumerate(cast_bf16(f"cast_{n}_{l}", W[n], l, n_parts(n, l))):
                shards[n, l, g] = part
    carrier = Carrier()

    def gw(n, l, g=None):
        return landed[n, l, g] if g is not None else [landed[n, l, i] for i in range(n_parts(n, l))]

    def gather_plan(keys):
        return (lambda ctx: gather_copies([shards[k] for k in keys])), (lambda outs: landed.update(zip(keys, outs)))

    first = [("w_in", 0, 0)] + [("w_pool", l, 0) for l in range(depth)]
    landed.update(zip(first, copies_call("gather_first", gather_copies([shards[k] for k in first]))))
    for g in range(1, W_IN0_PARTS):
        carrier.ride((0, f"proj{g - 1}"), *gather_plan([("w_in", 0, g)]))
    carrier.ride((0, f"proj{W_IN0_PARTS - 1}"), *gather_plan([("w_out", 0, 0), ("w_glu", 0, 0)]))
    for l in range(1, depth):
        for g in range(W_IN_PARTS):
            call = "ssm_fwd" if g == W_IN_PARTS - 1 else "attn_fwd"
            carrier.ride((l - 1, call), *gather_plan([("w_in", l, g)]))
        carrier.ride((l, "proj"), *gather_plan([("w_out", l, 0), ("w_glu", l, 0)]))

    own, recv_a, recv_b = {}, {}, {}

    def sibling_plan(l, names, keep, pick):
        def make(ctx):
            own.update({(n, l): ctx[n] for n in keep})
            return sibling_copies(pick(ctx))
        return make, (lambda outs: recv_a.update(zip([(n, l) for n in names], outs)))

    def chip_plan(l, names):
        def make(ctx):
            parts = [chip_partial(f"chip_partial_{n}_{l}", pos, own[n, l], recv_a[n, l])[:, None] for n in names]
            return chip_copies(parts)
        return make, (lambda outs: recv_b.update(zip([(n, l) for n in names], outs)))

    early, late = ("w_out", "w_glu"), ("w_in", "w_pool")
    for l in range(1, depth):
        carrier.ride((l, "dh"), *sibling_plan(l, BIG_NAMES, BIG_NAMES, lambda big: [big[n] for n in BIG_NAMES]))
        carrier.ride((l - 1, "ssm_bwd"), *chip_plan(l, ("w_out", "w_glu", "w_pool")))
        carrier.ride((l - 1, "attn_bwd"), *chip_plan(l, ("w_in",)))
    carrier.ride((0, "ssm_bwd"), *sibling_plan(0, early, early, lambda big: [big[n] for n in early]))
    carrier.ride((0, "dw_in_a"), *chip_plan(0, early))
    carrier.ride((0, "dw_in_b"), *sibling_plan(0, late, ("w_pool",), lambda ctx: [ctx["to_sibling"], ctx["w_pool"]]))

    def last_chip_make(big):
        own["w_in", 0] = big["w_in"]
        return chip_plan(0, late)[0](big)

    carrier.ride((0, "dh"), last_chip_make, chip_plan(0, late)[1])

    small = [None] * depth
    small_like = [W[n] for n in SMALL_NAMES]
    small_landed = []

    def small_make(ctx):
        layer0 = dict(ctx["small"], ln_g=jnp.zeros((D,), F32))
        grads = [jnp.stack([layer0[n]] + [small[l][n] for l in range(1, depth)]) for n in SMALL_NAMES[:-1]]
        return gather_copies([_pack(grads + [dfinal[0]])])

    carrier.ride((0, "dw_in_b"), small_make, small_landed.extend)

    params = [_layer_params(l, ln_g, pool_scale, lam_re, lam_im, log_dt, b_re, b_im, c_re, c_im,
                            d_skip, b_glu, branch_g) for l in range(depth)]
    h = x[0]
    saved = []
    for l in range(depth):
        h, s = layer_fwd(l, h, params[l], gw, D, carrier, staged_parts=W_IN0_PARTS if l == 0 else 0)
        saved.append(s)
    loss_part, dres, dres_b, dfinal = loss_head("loss_head", h, final_g[None, :], loss_target[0])
    loss = lax.psum(loss_part[0, 0], ("x", "y", "c"))

    for l in reversed(range(depth)):
        dres, dres_b, small[l] = layer_bwd(l, dres, dres_b, saved[l], params[l], gw, D, carrier, pos,
                                           split_w_in=(l == 0))
    grad_x = dres[None]

    results = {}
    for n in BIG_NAMES:
        shape = W[n].shape
        R, C = int(math.prod(shape[1:-1])), shape[-1]
        w3, m3, v3 = (t.reshape(depth, R, C) for t in (W[n], Mo[n], Vo[n]))
        prev = None
        for l in range(depth):
            prev, _ = adamw_shard(f"adamw_{n}_{l}", pos, l, own[n, l], recv_a[n, l], recv_b[n, l], w3, m3, v3, prev)
        results[n] = [t.reshape(shape) for t in prev]

    head = copies_call("gather_ln_g", gather_copies([small[0]["ln_g"].reshape(-1, LANE)]))[0]
    packed = adamw_small("adamw_small", small_landed[0], head, _pack(small_like),
                         _pack([Mo[n] for n in SMALL_NAMES]), _pack([Vo[n] for n in SMALL_NAMES]))
    unpacked = [_unpack(t, small_like) for t in packed]
    for i, n in enumerate(SMALL_NAMES):
        results[n] = [unpacked[j][i] for j in range(4)]

    out = [loss, grad_x]
    for j in range(4):
        out += [results[n][j] for n in WEIGHT_ORDER]
    return tuple(out)
```

```python
import functools
import math

import jax
import jax.numpy as jnp
from jax import lax
from jax.experimental import pallas as pl
from jax.experimental.pallas import tpu as pltpu

F32 = jnp.float32
BF16 = jnp.bfloat16
MESH = pl.DeviceIdType.MESH

EPS = 1e-6
HEAD_DIM = 128
SSM_GROUP = 16
SSM_STATE = 64
GROUPS_PER_CHUNK = 8
CHUNK_U = GROUPS_PER_CHUNK * SSM_GROUP
CHUNK_X = GROUPS_PER_CHUNK * SSM_STATE
N_POOL_GROUPS = 4
POOL_HALO = 16
N_DEV = 8
LANE = 128
FULL_K = 4096
ATTN_TILE = 256
ATTN_DECAY_CUTOFF = 100.0
ROW_TILE = 128
VMEM_BIG = 58 * 1024 * 1024
VMEM_MID = 40 * 1024 * 1024

ADAM_LR = 0.001
ADAM_B1 = 0.9
ADAM_B2 = 0.999
ADAM_EPS = 1e-08
ADAM_WD = 0.01
ADAM_STEP = 10
ADAM_C1 = 1.0 / (1.0 - ADAM_B1 ** ADAM_STEP)
ADAM_C2 = 1.0 / (1.0 - ADAM_B2 ** ADAM_STEP)

NN = (((1,), (0,)), ((), ()))
NT = (((1,), (1,)), ((), ()))
TN = (((0,), (0,)), ((), ()))


def _pick(n, cap):
    if n <= cap:
        return n
    step = LANE if cap >= LANE else 8
    t = (cap // step) * step
    while t > step and n % t:
        t -= step
    assert n % t == 0, (n, cap)
    return t


def _cparams(sem, vmem=None):
    return pltpu.CompilerParams(dimension_semantics=sem, vmem_limit_bytes=vmem)


def _dot(a, b, dn=NN):
    return lax.dot_general(a, b, dn, preferred_element_type=F32)


def _sigmoid(x):
    e = jnp.exp(-jnp.abs(x))
    r = 1.0 / (1.0 + e)
    return jnp.where(x >= 0, r, e * r)


HBM = pl.BlockSpec(memory_space=pl.ANY)


class HostedCopies:
    def __init__(self, inputs, out_shape, scratch, phases):
        self.inputs, self.out_shape, self.scratch, self.phases = inputs, out_shape, scratch, phases

    def emit(self, ins, outs, sems, step, total):
        plan = {}
        for frac, fn in self.phases:
            plan.setdefault(min(total - 1, int(frac * total)), []).append(fn)
        for s in sorted(plan):
            def run(fns=plan[s]):
                for fn in fns:
                    fn(ins, outs, sems)
            if total == 1:
                run()
            else:
                pl.when(step == s)(run)


def copies_call(name, copies):
    n_i, n_o = len(copies.inputs), len(copies.out_shape)

    def body(*refs):
        copies.emit(refs[:n_i], refs[n_i:n_i + n_o], refs[n_i + n_o:], 0, 1)

    return pl.pallas_call(
        body, name=name, in_specs=[HBM] * n_i, out_specs=[HBM] * n_o, out_shape=copies.out_shape,
        scratch_shapes=copies.scratch, compiler_params=pltpu.CompilerParams(has_side_effects=True),
    )(*copies.inputs)


def _matmul(name, a, b, *, grid, a_spec, b_spec, o_spec, out_shape, dn,
            res=None, res_spec=None, pos=None, copies=None, product=None, into=None):
    ni, nj, nk = grid
    bs, b_specs = (list(b), list(b_spec)) if isinstance(b, (list, tuple)) else ([b], [b_spec])
    n_b = len(bs)
    n_pos = 0 if pos is None else 1
    n_res = 0 if res is None else 1
    n_into = 0 if into is None else 1
    n_ci = 0 if copies is None else len(copies.inputs)
    n_co = 0 if copies is None else len(copies.out_shape)

    def body(*refs):
        refs = refs[n_pos:]
        a_ref, b_refs = refs[0], refs[1:1 + n_b]
        r_ref = refs[1 + n_b] if n_res else None
        base = 1 + n_b + n_res + n_into
        cin = refs[base:base + n_ci]
        o_ref = refs[base + n_ci]
        cout = refs[base + n_ci + 1:base + n_ci + 1 + n_co]
        sems = refs[base + n_ci + 1 + n_co:]
        k = pl.program_id(2)
        if copies is not None:
            step = (pl.program_id(0) * nj + pl.program_id(1)) * nk + k
            copies.emit(cin, cout, sems, step, ni * nj * nk)

        if product is None:
            part = _dot(a_ref[...].astype(BF16), b_refs[0][...].astype(BF16), dn)
        else:
            part = product(a_ref, b_refs)
        if nk == 1:
            if r_ref is not None:
                part = part + r_ref[...]
            o_ref[...] = part.astype(o_ref.dtype)
        else:
            @pl.when(k == 0)
            def _():
                o_ref[...] = part if r_ref is None else part + r_ref[...]

            @pl.when(k > 0)
            def _():
                o_ref[...] += part

    assert nk == 1 or out_shape.dtype == F32
    in_specs = [a_spec] + b_specs + ([res_spec] if n_res else []) + [HBM] * (n_into + n_ci)
    args = (((pos,) if n_pos else ()) + (a, *bs) + ((res,) if n_res else ()) + ((into,) if n_into else ())
            + tuple(copies.inputs if copies else ()))
    aliases = {n_pos + 1 + n_b + n_res: 0} if n_into else {}
    out_specs = [o_spec] + [HBM] * n_co
    out_shapes = [out_shape] + list(copies.out_shape if copies else [])
    scratch = list(copies.scratch if copies else [])
    params = pltpu.CompilerParams(
        dimension_semantics=("arbitrary",) * 3 if copies else ("parallel", "parallel", "arbitrary"),
        vmem_limit_bytes=VMEM_BIG, has_side_effects=copies is not None)
    out = pl.pallas_call(
        body, name=name,
        grid_spec=pltpu.PrefetchScalarGridSpec(
            num_scalar_prefetch=n_pos, grid=grid, in_specs=in_specs, out_specs=out_specs, scratch_shapes=scratch),
        out_shape=out_shapes, input_output_aliases=aliases, compiler_params=params)(*args)
    return out[0] if copies is None else (out[0], list(out[1:]))


def mm_nn_gathered(name, a, parts, out_dtype=F32, copies=None):
    M, K = a.shape
    P, w = len(parts), parts[0].shape[2]
    nper = P * w
    tm, tk, tn = _pick(M, 1024), _pick(K, FULL_K), _pick(w, 768)
    r = w // tn
    per_part = N_DEV * r

    def b_spec(g):
        def index(i, j, k, *_):
            t = jnp.clip(j - g * per_part, 0, per_part - 1)
            return (t // r, k, t % r)
        return pl.BlockSpec((None, tk, tn), index)

    def o_index(i, j, k, *_):
        t = j % per_part
        return (i, (t // r) * (nper // tn) + (j // per_part) * r + t % r)

    def product(a_ref, b_refs):
        j = pl.program_id(1)
        b = b_refs[0][...]
        for g in range(1, P):
            b = jnp.where(j >= g * per_part, b_refs[g][...], b)
        return _dot(a_ref[...].astype(BF16), b.astype(BF16))

    return _matmul(
        name, a, list(parts), grid=(M // tm, P * per_part, K // tk),
        a_spec=pl.BlockSpec((tm, tk), lambda i, j, k, *_: (i, k)),
        b_spec=[b_spec(g) for g in range(P)], o_spec=pl.BlockSpec((tm, tn), o_index),
        out_shape=jax.ShapeDtypeStruct((M, N_DEV * nper), out_dtype), dn=NN, copies=copies,
        product=product if P > 1 else None)


def mm_nn_part(name, a, part, g, P, into=None, out_dtype=F32, copies=None):
    M, K = a.shape
    w = part.shape[2]
    tm, tk, tn = _pick(M, 1024), _pick(K, FULL_K), _pick(w, 768)
    r = w // tn
    return _matmul(
        name, a, part, grid=(M // tm, N_DEV * r, K // tk),
        a_spec=pl.BlockSpec((tm, tk), lambda i, j, k, *_: (i, k)),
        b_spec=pl.BlockSpec((None, tk, tn), lambda i, j, k, *_: (j // r, k, j % r)),
        o_spec=pl.BlockSpec((tm, tn), lambda i, j, k, *_: (i, (j // r) * (P * r) + g * r + j % r)),
        out_shape=jax.ShapeDtypeStruct((M, N_DEV * P * w), out_dtype), dn=NN, copies=copies, into=into)


NT_SLICES = 2
W_IN_PARTS = 2
W_IN0_PARTS = 3


def mm_nt_gathered(name, a, parts, out_dtype=F32, copies=None):
    M, _ = a.shape
    P, (_, N, w) = len(parts), parts[0].shape
    nper = P * w
    tm, tn = _pick(M, 1024), _pick(N, 1024)
    S = NT_SLICES

    def product(a_ref, b_refs):
        total = None
        for s in range(S):
            for g in range(P):
                off = (s * P + g) * w
                term = _dot(a_ref[:, off:off + w].astype(BF16), b_refs[g][s], NT)
                total = term if total is None else total + term
        return total

    return _matmul(
        name, a, list(parts), grid=(M // tm, N // tn, N_DEV // S),
        a_spec=pl.BlockSpec((tm, S * nper), lambda i, j, k, *_: (i, k)),
        b_spec=[pl.BlockSpec((S, tn, w), lambda i, j, k, *_: (k, j, 0)) for _ in range(P)],
        o_spec=pl.BlockSpec((tm, tn), lambda i, j, k, *_: (i, j)),
        out_shape=jax.ShapeDtypeStruct((M, N), out_dtype), dn=NT, copies=copies, product=product)


def mm_tn_scattered(name, a, b, copies=None):
    L, M = a.shape
    nper = b.shape[1] // N_DEV
    tm, tn, tk = _pick(M, 1024), _pick(nper, 768), _pick(L, FULL_K)
    r = nper // tn
    return _matmul(
        name, a, b, grid=(M // tm, N_DEV * r, L // tk),
        a_spec=pl.BlockSpec((tk, tm), lambda i, j, k, *_: (k, i)),
        b_spec=pl.BlockSpec((tk, tn), lambda i, j, k, *_: (k, j)),
        o_spec=pl.BlockSpec((None, tm, tn), lambda i, j, k, *_: (j // r, i, j % r)),
        out_shape=jax.ShapeDtypeStruct((N_DEV, M, nper), F32), dn=TN, copies=copies)


def mm_tn_half(name, a, b, pos, own, copies=None):
    L, M = a.shape
    nper = b.shape[1] // N_DEV
    tm, tn, tk = _pick(M, 1024), _pick(nper, 768), _pick(L, FULL_K)
    r = nper // tn

    def b_map(i, j, k, p):
        core = p[0] if own else 1 - p[0]
        return (k, (2 * (j // r) + core) * r + j % r)

    return _matmul(
        name, a, b, grid=(M // tm, 4 * r, L // tk),
        a_spec=pl.BlockSpec((tk, tm), lambda i, j, k, *_: (k, i)),
        b_spec=pl.BlockSpec((tk, tn), b_map),
        o_spec=pl.BlockSpec((None, tm, tn), lambda i, j, k, *_: (j // r, i, j % r)),
        out_shape=jax.ShapeDtypeStruct((4, M, nper), F32), dn=TN, pos=pos, copies=copies)


def mm_plain(name, a, b, dn, out_dtype=F32, res=None, copies=None):
    if dn == NN:
        (M, K), N = a.shape, b.shape[1]
    elif dn == NT:
        (M, K), N = a.shape, b.shape[0]
    else:
        (K, M), N = a.shape, b.shape[1]
    tm, tn, tk = _pick(M, 1024), _pick(N, 512), _pick(K, FULL_K)
    a_spec =(pl.BlockSpec((tk, tm), lambda i, j, k, *_: (k, i)) if dn == TN
              else pl.BlockSpec((tm, tk), lambda i, j, k, *_: (i, k)))
    b_spec = (pl.BlockSpec((tn, tk), lambda i, j, k, *_: (j, k)) if dn == NT
              else pl.BlockSpec((tk, tn), lambda i, j, k, *_: (k, j)))
    o_spec = pl.BlockSpec((tm, tn), lambda i, j, k, *_: (i, j))
    return _matmul(
        name, a, b, grid=(M // tm, N // tn, K // tk), a_spec=a_spec, b_spec=b_spec, o_spec=o_spec,
        out_shape=jax.ShapeDtypeStruct((M, N), out_dtype), dn=dn,
        res=res, res_spec=o_spec if res is not None else None, copies=copies)


def rms_fwd(name, x, g):
    L, D = x.shape
    tr = _pick(L, ROW_TILE)

    def body(x_ref, g_ref, h_ref):
        xv = x_ref[...]
        r = lax.rsqrt(jnp.mean(xv * xv, axis=-1, keepdims=True) + EPS)
        h_ref[...] = (xv * r * g_ref[...]).astype(BF16)

    return pl.pallas_call(
        body, name=name, grid=(L // tr,),
        in_specs=[pl.BlockSpec((tr, D), lambda i: (i, 0)), pl.BlockSpec((1, D), lambda i: (0, 0))],
        out_specs=pl.BlockSpec((tr, D), lambda i: (i, 0)),
        out_shape=jax.ShapeDtypeStruct((L, D), BF16),
        compiler_params=_cparams(("parallel",), VMEM_MID))(x, g)


def rms_bwd(name, x, dh, dres, g):
    L, D = x.shape
    tr = _pick(L, ROW_TILE)

    def body(x_ref, dh_ref, dr_ref, g_ref, dx_ref, dxb_ref, dg_ref):
        xv = x_ref[...]
        r = lax.rsqrt(jnp.mean(xv * xv, axis=-1, keepdims=True) + EPS)
        xh = xv * r
        dhv = dh_ref[...]
        dn = dhv * g_ref[...]
        dxv = dr_ref[...] + r * (dn - xh * jnp.mean(dn * xh, axis=-1, keepdims=True))
        dx_ref[...] = dxv
        dxb_ref[...] = dxv.astype(BF16)

        @pl.when(pl.program_id(0) == 0)
        def _():
            dg_ref[...] = jnp.zeros_like(dg_ref)

        dg_ref[...] += jnp.sum(dhv * xh, axis=0, keepdims=True)

    row = pl.BlockSpec((tr, D), lambda i: (i, 0))
    vec = pl.BlockSpec((1, D), lambda i: (0, 0))
    return pl.pallas_call(
        body, name=name, grid=(L // tr,), in_specs=[row, row, row, vec], out_specs=[row, row, vec],
        out_shape=[jax.ShapeDtypeStruct((L, D), F32), jax.ShapeDtypeStruct((L, D), BF16),
                   jax.ShapeDtypeStruct((1, D), F32)],
        compiler_params=_cparams(("arbitrary",), VMEM_MID))(x, dh, dres, g)


def loss_head(name, x, g, target):
    L, D = x.shape
    tr = _pick(L, ROW_TILE)

    def body(x_ref, g_ref, t_ref, loss_ref, dx_ref, dxb_ref, dg_ref):
        xv = x_ref[...]
        gv = g_ref[...]
        r = lax.rsqrt(jnp.mean(xv * xv, axis=-1, keepdims=True) + EPS)
        xh = xv * r
        err = xh * gv - t_ref[...]
        dy = err * (1.0 / D)
        dn = dy * gv
        dxv = r * (dn - xh * jnp.mean(dn * xh, axis=-1, keepdims=True))
        dx_ref[...] = dxv
        dxb_ref[...] = dxv.astype(BF16)

        @pl.when(pl.program_id(0) == 0)
        def _():
            dg_ref[...] = jnp.zeros_like(dg_ref)
            loss_ref[...] = jnp.zeros_like(loss_ref)

        dg_ref[...] += jnp.sum(dy * xh, axis=0, keepdims=True)
        row_loss = jnp.sum(err * err, axis=-1, keepdims=True) * (0.5 / D)
        loss_ref[...] += jnp.sum(row_loss, axis=0, keepdims=True)

    row = pl.BlockSpec((tr, D), lambda i: (i, 0))
    vec = pl.BlockSpec((1, D), lambda i: (0, 0))
    one = pl.BlockSpec((1, 1), lambda i: (0, 0))
    return pl.pallas_call(
        body, name=name, grid=(L // tr,), in_specs=[row, vec, row], out_specs=[one, row, row, vec],
        out_shape=[jax.ShapeDtypeStruct((1, 1), F32), jax.ShapeDtypeStruct((L, D), F32),
                   jax.ShapeDtypeStruct((L, D), BF16), jax.ShapeDtypeStruct((1, D), F32)],
        compiler_params=_cparams(("arbitrary",), VMEM_MID))(x, g, target)


def _branch_specs(D, tr):
    DP, DA, DS = D // 4, D // 2, D // 4
    return dict(
        pool=pl.BlockSpec((tr, DP), lambda i: (i, 0)),
        attn=pl.BlockSpec((tr, DA), lambda i: (i, 0)),
        glu=pl.BlockSpec((tr, 2 * DS), lambda i: (i, 0)),
        p_gate=pl.BlockSpec((tr, DP), lambda i: (i, 1)),
        a_gate=pl.BlockSpec((tr, DA), lambda i: (i, 4)),
        s_gate=pl.BlockSpec((tr, DS), lambda i: (i, 11)),
        bglu=pl.BlockSpec((1, 2 * DS), lambda i: (0, 0)),
        bg=pl.BlockSpec((1, D), lambda i: (0, 0)),
        row=pl.BlockSpec((tr, D), lambda i: (i, 0)),
    )


def branch_fwd(name, ypool, yattn, glu_pre, proj, b_glu, branch_g):
    L, DP = ypool.shape
    D = 4 * DP
    DA, DS = D // 2, D // 4
    tr = _pick(L, ROW_TILE)
    s = _branch_specs(D, tr)

    def body(yp_ref, ya_ref, gl_ref, pg_ref, ag_ref, sg_ref, bgl_ref, bg_ref, y_ref):
        pre = gl_ref[...] + bgl_ref[...]
        ys = pre[:, :DS] * _sigmoid(pre[:, DS:])
        bg = bg_ref[...]

        def one(raw, gate, g):
            gate = gate.astype(F32)
            r = lax.rsqrt(jnp.mean(raw * raw, axis=-1, keepdims=True) + EPS)
            return raw * r * g * (gate * _sigmoid(gate))

        y_ref[:, :DP] = one(yp_ref[...], pg_ref[...], bg[:, :DP]).astype(BF16)
        y_ref[:, DP:DP + DA] = one(ya_ref[...], ag_ref[...], bg[:, DP:DP + DA]).astype(BF16)
        y_ref[:, DP + DA:] = one(ys, sg_ref[...], bg[:, DP + DA:]).astype(BF16)

    return pl.pallas_call(
        body, name=name, grid=(L // tr,),
        in_specs=[s["pool"], s["attn"], s["glu"], s["p_gate"], s["a_gate"], s["s_gate"], s["bglu"], s["bg"]],
        out_specs=s["row"], out_shape=jax.ShapeDtypeStruct((L, D), BF16),
        compiler_params=_cparams(("parallel",), VMEM_MID))(ypool, yattn, glu_pre, proj, proj, proj, b_glu, branch_g)


def branch_bwd(name, dy, ypool, yattn, glu_pre, proj, b_glu, branch_g):
    L, DP = ypool.shape
    D = 4 * DP
    DA, DS = D // 2, D // 4
    tr = _pick(L, ROW_TILE // 2)
    s = _branch_specs(D, tr)

    def body(dy_ref, yp_ref, ya_ref, gl_ref, pg_ref, ag_ref, sg_ref, bgl_ref, bg_ref,
             dyp_ref, dya_ref, dgl_ref, dpg_ref, dag_ref, dsg_ref, dbg_ref, dbgl_ref):
        @pl.when(pl.program_id(0) == 0)
        def _():
            dbg_ref[...] = jnp.zeros_like(dbg_ref)
            dbgl_ref[...] = jnp.zeros_like(dbgl_ref)

        bg = bg_ref[...]

        def one(raw, gate, g, dyb):
            gate = gate.astype(F32)
            r = lax.rsqrt(jnp.mean(raw * raw, axis=-1, keepdims=True) + EPS)
            n = raw * r
            sg = _sigmoid(gate)
            sl = gate * sg
            dgate = dyb * n * g * (sg * (1.0 + gate * (1.0 - sg)))
            dbg = jnp.sum(dyb * n * sl, axis=0, keepdims=True)
            dn = dyb * g * sl
            draw = r * (dn - n * jnp.mean(dn * n, axis=-1, keepdims=True))
            return draw, dgate, dbg

        draw, dgate, dbg = one(yp_ref[...], pg_ref[...], bg[:, :DP], dy_ref[:, :DP])
        dyp_ref[...] = draw
        dpg_ref[...] = dgate.astype(BF16)
        dbg_ref[:, :DP] += dbg

        draw, dgate, dbg = one(ya_ref[...], ag_ref[...], bg[:, DP:DP + DA], dy_ref[:, DP:DP + DA])
        dya_ref[...] = draw
        dag_ref[...] = dgate.astype(BF16)
        dbg_ref[:, DP:DP + DA] += dbg

        pre = gl_ref[...] + bgl_ref[...]
        val = pre[:, :DS]
        sgt = _sigmoid(pre[:, DS:])
        draw, dgate, dbg = one(val * sgt, sg_ref[...], bg[:, DP + DA:], dy_ref[:, DP + DA:])
        dsg_ref[...] = dgate.astype(BF16)
        dbg_ref[:, DP + DA:] += dbg
        dval = draw * sgt
        dgt = draw * val * sgt * (1.0 - sgt)
        dgl_ref[:, :DS] = dval.astype(BF16)
        dgl_ref[:, DS:] = dgt.astype(BF16)
        dbgl_ref[:, :DS] += jnp.sum(dval, axis=0, keepdims=True)
        dbgl_ref[:, DS:] += jnp.sum(dgt, axis=0, keepdims=True)

    loc = lambda w: pl.BlockSpec((tr, w), lambda i: (i, 0))
    return pl.pallas_call(
        body, name=name, grid=(L // tr,),
        in_specs=[s["row"], s["pool"], s["attn"], s["glu"], s["p_gate"], s["a_gate"], s["s_gate"], s["bglu"], s["bg"]],
        out_specs=[loc(DP), loc(DA), loc(2 * DS), loc(DP), loc(DA), loc(DS), s["bg"], s["bglu"]],
        out_shape=[jax.ShapeDtypeStruct((L, DP), F32), jax.ShapeDtypeStruct((L, DA), F32),
                   jax.ShapeDtypeStruct((L, 2 * DS), BF16), jax.ShapeDtypeStruct((L, DP), BF16),
                   jax.ShapeDtypeStruct((L, DA), BF16), jax.ShapeDtypeStruct((L, DS), BF16),
                   jax.ShapeDtypeStruct((1, D), F32), jax.ShapeDtypeStruct((1, 2 * DS), F32)],
        compiler_params=_cparams(("arbitrary",), VMEM_BIG),
    )(dy, ypool, yattn, glu_pre, proj, proj, proj, b_glu, branch_g)


def _pool_select(g, s2, s4, s8, s16):
    return jnp.where(g == 0, s2, jnp.where(g == 1, s4, jnp.where(g == 2, s8, s16)))


def _pool_window(g):
    return jnp.where(g == 0, 2.0, jnp.where(g == 1, 4.0, jnp.where(g == 2, 8.0, 16.0))).astype(F32)


def _pooled_chunk(pad, g, r0, ch):
    xh = pad[pl.ds(r0, ch + POOL_HALO), :]
    s2 = xh + pltpu.roll(xh, 1, 0)
    s4 = s2 + pltpu.roll(s2, 2, 0)
    s8 = s4 + pltpu.roll(s4, 4, 0)
    s16 = s8 + pltpu.roll(s8, 8, 0)
    win = _pool_select(g, s2, s4, s8, s16)[POOL_HALO:]
    pos = (r0 + 1 + lax.broadcasted_iota(jnp.int32, (ch, 1), 0)).astype(F32)
    return win / jnp.minimum(pos, _pool_window(g)) - xh[POOL_HALO:]


def pool_fwd(name, proj, wp, scale):
    L = proj.shape[0]
    DP = scale.shape[1]
    PG = DP // N_POOL_GROUPS
    ch = _pick(L, 256)

    def body(x_ref, w_ref, s_ref, o_ref, pad):
        g = pl.program_id(0)
        pad[0:POOL_HALO, :] = jnp.zeros((POOL_HALO, PG), F32)
        pad[POOL_HALO:, :] = x_ref[...].astype(F32)

        def chunk(ci, carry):
            r0 = pl.multiple_of(ci * ch, ch)
            pooled = _pooled_chunk(pad, g, r0, ch)
            o_ref[pl.ds(r0, ch), :] = _dot(pooled.astype(BF16), w_ref[...]) * s_ref[...]
            return carry

        lax.fori_loop(0, L // ch, chunk, 0)

    return pl.pallas_call(
        body, name=name, grid=(N_POOL_GROUPS,),
        in_specs=[pl.BlockSpec((L, PG), lambda g: (0, g)), pl.BlockSpec((None, PG, PG), lambda g: (g, 0, 0)),
                  pl.BlockSpec((1, PG), lambda g: (0, g))],
        out_specs=pl.BlockSpec((L, PG), lambda g: (0, g)),
        out_shape=jax.ShapeDtypeStruct((L, DP), F32),
        scratch_shapes=[pltpu.VMEM((L + POOL_HALO, PG), F32)],
        compiler_params=_cparams(("parallel",), VMEM_MID))(proj, wp, scale)


def pool_bwd(name, dyraw, proj, wp, scale):
    L = proj.shape[0]
    DP = scale.shape[1]
    PG = DP // N_POOL_GROUPS
    ch = _pick(L, 256)

    def body(dy_ref, x_ref, w_ref, s_ref, dx_ref, dw_ref, ds_ref, pad, dpad, dpo):
        g = pl.program_id(0)
        pad[0:POOL_HALO, :] = jnp.zeros((POOL_HALO, PG), F32)
        pad[POOL_HALO:, :] = x_ref[...].astype(F32)
        dpad[L:, :] = jnp.zeros((POOL_HALO, PG), F32)
        dw_ref[...] = jnp.zeros_like(dw_ref)
        ds_ref[...] = jnp.zeros_like(ds_ref)
        wv = w_ref[...]
        win_f = _pool_window(g)

        def chunk(ci, carry):
            r0 = pl.multiple_of(ci * ch, ch)
            pooled = _pooled_chunk(pad, g, r0, ch).astype(BF16)
            dyv = dy_ref[pl.ds(r0, ch), :]
            ds_ref[...] += jnp.sum(dyv * _dot(pooled, wv), axis=0, keepdims=True)
            dmixed = (dyv * s_ref[...]).astype(BF16)
            dw_ref[...] += _dot(pooled, dmixed, TN)
            dpooled = _dot(dmixed, wv, NT)
            pos = (r0 + 1 + lax.broadcasted_iota(jnp.int32, (ch, 1), 0)).astype(F32)
            dpad[pl.ds(r0, ch), :] = dpooled / jnp.minimum(pos, win_f)
            dpo[pl.ds(r0, ch), :] = dpooled
            return carry

        lax.fori_loop(0, L // ch, chunk, 0)

        def chunk2(ci, carry):
            r0 = pl.multiple_of(ci * ch, ch)
            n = ch + POOL_HALO
            dm = dpad[pl.ds(r0, n), :]
            s2 = dm + pltpu.roll(dm, n - 1, 0)
            s4 = s2 + pltpu.roll(s2, n - 2, 0)
            s8 = s4 + pltpu.roll(s4, n - 4, 0)
            s16 = s8 + pltpu.roll(s8, n - 8, 0)
            win = _pool_select(g, s2, s4, s8, s16)[:ch]
            dx_ref[pl.ds(r0, ch), :] = (win - dpo[pl.ds(r0, ch), :]).astype(BF16)
            return carry

        lax.fori_loop(0, L // ch, chunk2, 0)

    col = pl.BlockSpec((L, PG), lambda g: (0, g))
    return pl.pallas_call(
        body, name=name, grid=(N_POOL_GROUPS,),
        in_specs=[col, col, pl.BlockSpec((None, PG, PG), lambda g: (g, 0, 0)), pl.BlockSpec((1, PG), lambda g: (0, g))],
        out_specs=[col, pl.BlockSpec((None, PG, PG), lambda g: (g, 0, 0)), pl.BlockSpec((1, PG), lambda g: (0, g))],
        out_shape=[jax.ShapeDtypeStruct((L, DP), BF16), jax.ShapeDtypeStruct((N_POOL_GROUPS, PG, PG), F32),
                   jax.ShapeDtypeStruct((1, DP), F32)],
        scratch_shapes=[pltpu.VMEM((L + POOL_HALO, PG), F32), pltpu.VMEM((L + POOL_HALO, PG), F32),
                        pltpu.VMEM((L, PG), F32)],
        compiler_params=_cparams(("parallel",), VMEM_MID))(dyraw, proj, wp, scale)


def _attn_tile(L):
    return _pick(L, ATTN_TILE)


def _tri(t, strict):
    j = lax.broadcasted_iota(jnp.int32, (t, t), 0)
    s = lax.broadcasted_iota(jnp.int32, (t, t), 1)
    return ((j > s) if strict else (j >= s)).astype(BF16)


def _attn_block(q, kt, rb, after, diagonal):
    tq, tk = q.shape[0], kt.shape[0]
    z = _dot(q, kt, NT)
    e = jnp.exp(-jnp.abs(z))
    l1p = jnp.log(1.0 + e)
    log_sig = jnp.minimum(z, 0.0) - l1p
    b = -jnp.maximum(z, 0.0) - l1p
    causal = None
    if diagonal:
        causal = lax.broadcasted_iota(jnp.int32, (tq, tk), 1) < lax.broadcasted_iota(jnp.int32, (tq, tk), 0)
        b = jnp.where(causal, b, 0.0)
    b_hi = b.astype(BF16)
    b_lo = (b - b_hi.astype(F32)).astype(BF16)
    suffix = _dot(b_hi, after) + _dot(b_lo, after) + rb
    w = jnp.exp(log_sig + suffix)
    if diagonal:
        w = jnp.where(causal, w, 0.0)
    return z, e, causal, b, w


def _attn_sweep(i, visit):
    go = visit(i, True)
    lax.while_loop(lambda c: jnp.logical_and(c[0] >= 0, c[1]),
                   lambda c: (c[0] - 1, visit(c[0], False)), (i - 1, go))


def attn_fwd(name, proj, D, copies=None):
    L = proj.shape[0]
    DA = D // 2
    H = DA // HEAD_DIM
    tq = tk = _attn_tile(L)
    qo, ko, vo = (D // 2) // HEAD_DIM, D // HEAD_DIM, (3 * D // 2) // HEAD_DIM
    scale = HEAD_DIM ** -0.5

    def body(q_ref, k_ref, v_ref, tri_ref, o_ref, kb_s, vb_s, acc, rb):
        i = pl.program_id(1)

        @pl.when(i == 0)
        def _():
            kb_s[...] = k_ref[...].astype(BF16)
            vb_s[...] = v_ref[...].astype(BF16)

        q = (q_ref[...].astype(F32) * scale).astype(BF16)
        acc[...] = jnp.zeros_like(acc)
        rb[...] = jnp.zeros_like(rb)

        def visit(kb, diagonal):
            k0 = pl.multiple_of(kb * tk, tk)
            kt = kb_s[pl.ds(k0, tk), :]
            vt = vb_s[pl.ds(k0, tk), :]
            _, _, _, b, w = _attn_block(q, kt, rb[...], tri_ref[...], diagonal)
            acc[...] += _dot(w.astype(BF16), vt)
            rbn = rb[...] + jnp.sum(b, axis=1, keepdims=True)
            rb[...] = rbn
            return jnp.max(rbn) > -ATTN_DECAY_CUTOFF

        _attn_sweep(i, visit)
        o_ref[...] = acc[...]

    (out,), landed = _call(
        body, name=name, grid=(H, L // tq),
        in_specs=[pl.BlockSpec((tq, HEAD_DIM), lambda h, i: (i, qo + h)),
                  pl.BlockSpec((L, HEAD_DIM), lambda h, i: (0, ko + h)),
                  pl.BlockSpec((L, HEAD_DIM), lambda h, i: (0, vo + h)),
                  pl.BlockSpec((tk, tk), lambda h, i: (0, 0))],
        out_specs=[pl.BlockSpec((tq, HEAD_DIM), lambda h, i: (i, h))],
        out_shape=[jax.ShapeDtypeStruct((L, DA), F32)],
        scratch_shapes=[pltpu.VMEM((L, HEAD_DIM), BF16), pltpu.VMEM((L, HEAD_DIM), BF16),
                        pltpu.VMEM((tq, HEAD_DIM), F32), pltpu.VMEM((tq, 1), F32)],
        vmem=VMEM_MID, args=(proj, proj, proj, _tri(tk, True)), semantics=("arbitrary", "arbitrary"),
        copies=copies)
    return out, landed


def attn_bwd(name, proj, o, do, D, copies=None):
    L = proj.shape[0]
    DA = D // 2
    H = DA // HEAD_DIM
    tq = tk = _attn_tile(L)
    qo, ko, vo = (D // 2) // HEAD_DIM, D // HEAD_DIM, (3 * D // 2) // HEAD_DIM
    scale = HEAD_DIM ** -0.5

    def body(q_ref, k_ref, v_ref, o_ref, do_ref, after_ref, from_ref, dq_ref, dk_ref, dv_ref,
             kb_s, vb_s, dk_s, dv_s, dq_acc, rb, rg):
        i = pl.program_id(1)
        nq = pl.num_programs(1)

        @pl.when(i == 0)
        def _():
            kb_s[...] = k_ref[...].astype(BF16)
            vb_s[...] = v_ref[...].astype(BF16)
            dk_s[...] = jnp.zeros_like(dk_s)
            dv_s[...] = jnp.zeros_like(dv_s)

        q = (q_ref[...].astype(F32) * scale).astype(BF16)
        dob = do_ref[...].astype(BF16)
        delta = jnp.sum(dob.astype(F32) * o_ref[...], axis=1, keepdims=True)
        dq_acc[...] = jnp.zeros_like(dq_acc)
        rb[...] = jnp.zeros_like(rb)
        rg[...] = jnp.zeros_like(rg)

        def visit(kb, diagonal):
            k0 = pl.multiple_of(kb * tk, tk)
            kt = kb_s[pl.ds(k0, tk), :]
            vt = vb_s[pl.ds(k0, tk), :]
            z, e, causal, b, w = _attn_block(q, kt, rb[...], after_ref[...], diagonal)
            wq = w.astype(BF16)
            dw = _dot(dob, vt, NT)
            g = wq.astype(F32) * dw
            g_hi = g.astype(BF16)
            g_lo = (g - g_hi.astype(F32)).astype(BF16)
            from_s = from_ref[...]
            suffix_g = _dot(g_hi, from_s) + _dot(g_lo, from_s) + rg[...]
            before = delta - suffix_g
            r = 1.0 / (1.0 + e)
            sig = jnp.where(z >= 0, r, e * r)
            sig_neg = jnp.where(z >= 0, e * r, r)
            dz = g * sig_neg - before * sig
            if diagonal:
                dz = jnp.where(causal, dz, 0.0)
            dz = dz.astype(BF16)
            dq_acc[...] += _dot(dz, kt)
            dk_s[pl.ds(k0, tk), :] += _dot(dz, q, TN)
            dv_s[pl.ds(k0, tk), :] += _dot(wq, dob, TN)
            rbn = rb[...] + jnp.sum(b, axis=1, keepdims=True)
            rb[...] = rbn
            rg[...] += jnp.sum(g, axis=1, keepdims=True)
            return jnp.max(rbn) > -ATTN_DECAY_CUTOFF

        _attn_sweep(i, visit)
        dq_ref[...] = (dq_acc[...] * scale).astype(BF16)

        @pl.when(i == nq - 1)
        def _():
            dk_ref[...] = dk_s[...].astype(BF16)
            dv_ref[...] = dv_s[...].astype(BF16)

    blk = pl.BlockSpec((tq, HEAD_DIM), lambda h, i: (i, h))
    full = pl.BlockSpec((L, HEAD_DIM), lambda h, i: (0, h))
    return _call(
        body, name=name, grid=(H, L // tq),
        in_specs=[pl.BlockSpec((tq, HEAD_DIM), lambda h, i: (i, qo + h)),
                  pl.BlockSpec((L, HEAD_DIM), lambda h, i: (0, ko + h)),
                  pl.BlockSpec((L, HEAD_DIM), lambda h, i: (0, vo + h)), blk, blk,
                  pl.BlockSpec((tk, tk), lambda h, i: (0, 0)), pl.BlockSpec((tk, tk), lambda h, i: (0, 0))],
        out_specs=[blk, full, full],
        out_shape=[jax.ShapeDtypeStruct((L, DA), BF16)] * 3,
        scratch_shapes=[pltpu.VMEM((L, HEAD_DIM), BF16), pltpu.VMEM((L, HEAD_DIM), BF16),
                        pltpu.VMEM((L, HEAD_DIM), F32), pltpu.VMEM((L, HEAD_DIM), F32),
                        pltpu.VMEM((tq, HEAD_DIM), F32), pltpu.VMEM((tq, 1), F32), pltpu.VMEM((tq, 1), F32)],
        vmem=VMEM_MID, args=(proj, proj, proj, o, do, _tri(tk, True), _tri(tk, False)),
        semantics=("arbitrary", "arbitrary"), copies=copies)


def _cmul(ar, ai, br, bi):
    return ar * br - ai * bi, ar * bi + ai * br


def _cmul_conj(ar, ai, br, bi):
    return ar * br + ai * bi, ar * bi - ai * br


def _ssm_disc(lr, li, ld):
    dt = jnp.exp(ld)
    m = jnp.exp(lr * dt)
    ar, ai = m * jnp.cos(li * dt), m * jnp.sin(li * dt)
    inv = 1.0 / (lr * lr + li * li)
    fr, fi = _cmul(ar - 1.0, ai, lr * inv, -li * inv)
    return dt, ar, ai, fr, fi, inv


def ssm_prep(name, lr, li, ld, br, bi):
    def body(lr_ref, li_ref, ld_ref, br_ref, bi_ref, zr_ref, zi_ref, bbr_ref, bbi_ref):
        dt, _, _, fr, fi, _ = _ssm_disc(lr_ref[...], li_ref[...], ld_ref[...])
        zr_ref[...] = lr_ref[...] * dt
        zi_ref[...] = li_ref[...] * dt
        bbr, bbi = _cmul(fr, fi, br_ref[...], bi_ref[...])
        bbr_ref[...] = bbr
        bbi_ref[...] = bbi

    sd = jax.ShapeDtypeStruct
    return pl.pallas_call(
        body, name=name,
        out_shape=[sd(lr.shape, F32), sd(lr.shape, F32), sd(br.shape, F32), sd(br.shape, F32)],
    )(lr, li, ld, br, bi)


def ssm_prep_bwd(name, lr, li, ld, br, bi, gar, gai, gbr, gbi):
    def body(lr_ref, li_ref, ld_ref, br_ref, bi_ref, gar_ref, gai_ref, gbr_ref, gbi_ref,
             dlr_ref, dli_ref, dld_ref, dbr_ref, dbi_ref):
        lr_, li_ = lr_ref[...], li_ref[...]
        dt, ar, ai, fr, fi, inv = _ssm_disc(lr_, li_, ld_ref[...])
        gbr_, gbi_ = gbr_ref[...], gbi_ref[...]
        dbr, dbi = _cmul_conj(fr, fi, gbr_, gbi_)
        dbr_ref[...] = dbr
        dbi_ref[...] = dbi
        pr, pi = _cmul_conj(br_ref[...], bi_ref[...], gbr_, gbi_)
        gfr = jnp.sum(pr, axis=1, keepdims=True)
        gfi = jnp.sum(pi, axis=1, keepdims=True)
        ilr, ili = lr_ * inv, -li_ * inv
        tr_, ti_ = _cmul_conj(ilr, ili, gfr, gfi)
        gatr, gati = gar_ref[...] + tr_, gai_ref[...] + ti_
        hr, hi = _cmul(fr, fi, ilr, ili)
        t1r, t1i = _cmul_conj(ar * dt, ai * dt, gatr, gati)
        t2r, t2i = _cmul_conj(hr, hi, gfr, gfi)
        dlr_ref[...] = t1r - t2r
        dli_ref[...] = t1i - t2i
        lar, lai = _cmul(lr_, li_, ar, ai)
        gdt, _ = _cmul_conj(lar, lai, gatr, gati)
        dld_ref[...] = jnp.sum(gdt, axis=2, keepdims=True) * dt

    sd = jax.ShapeDtypeStruct
    return pl.pallas_call(
        body, name=name,
        out_shape=[sd(lr.shape, F32), sd(lr.shape, F32), sd(ld.shape, F32), sd(br.shape, F32), sd(br.shape, F32)],
    )(lr, li, ld, br, bi, gar, gai, gbr, gbi)


SCAN_ROWS = 32


def _scan_rows(L):
    return min(SCAN_ROWS, L)


def _power_table(pr_s, pi_s, zr, zi, L, reverse):
    R = _scan_rows(L)
    row = lax.broadcasted_iota(jnp.int32, (R, 1), 0).astype(F32)
    dist = (R - row) if reverse else (row + 1.0)
    mag = jnp.exp(dist * zr)
    pr_s[...] = mag * jnp.cos(dist * zi)
    pi_s[...] = mag * jnp.sin(dist * zi)


def _scan(xr, xi, pr_s, pi_s, L, reverse):
    R = _scan_rows(L)
    nt = L // R
    assert L % R == 0 and R & (R - 1) == 0
    ns = CHUNK_X // LANE
    ridx = lax.broadcasted_iota(jnp.int32, (R, LANE), 0)

    def power(ref, d, cs):
        at = R - d if reverse else d - 1
        return ref[at:at + 1, cs]

    def shift(v, d):
        if d < 8:
            if reverse:
                return jnp.where(ridx < R - d, pltpu.roll(v, R - d, 0), 0.0)
            return jnp.where(ridx >= d, pltpu.roll(v, d, 0), 0.0)
        zeros = jnp.zeros((d, LANE), F32)
        return jnp.concatenate([v[d:], zeros], 0) if reverse else jnp.concatenate([zeros, v[:R - d]], 0)

    def tile(n, carry):
        t = nt - 1 - n if reverse else n
        rows = pl.ds(pl.multiple_of(t * R, R), R)
        edges = []
        for c in range(ns):
            cs = slice(c * LANE, (c + 1) * LANE)
            vr, vi = xr[rows, cs], xi[rows, cs]
            d = 1
            while d < R:
                ar, ai = power(pr_s, d, cs), power(pi_s, d, cs)
                sr, si = shift(vr, d), shift(vi, d)
                vr, vi = vr + ar * sr - ai * si, vi + ar * si + ai * sr
                d *= 2
            cr, ci = carry[2 * c], carry[2 * c + 1]
            pr, pi = pr_s[:, cs], pi_s[:, cs]
            vr, vi = vr + pr * cr - pi * ci, vi + pr * ci + pi * cr
            xr[rows, cs] = vr
            xi[rows, cs] = vi
            edge = slice(0, 1) if reverse else slice(R - 1, R)
            edges += [vr[edge], vi[edge]]
        return tuple(edges)

    lax.fori_loop(0, nt, tile, tuple(jnp.zeros((1, LANE), F32) for _ in range(2 * ns)))


def _gelu(x):
    t = jnp.tanh(0.7978845608028654 * (x + 0.044715 * x * x * x))
    return 0.5 * x * (1.0 + t)


def _gelu_grad(x):
    t = jnp.tanh(0.7978845608028654 * (x + 0.044715 * x * x * x))
    return 0.5 * (1.0 + t) + 0.5 * x * (1.0 - t * t) * 0.7978845608028654 * (1.0 + 0.134145 * x * x)


def _call(body, *, name, grid, in_specs, out_specs, out_shape, scratch_shapes, vmem, args, semantics,
          copies=None):
    n_i, n_o, n_s = len(in_specs), len(out_specs), len(scratch_shapes)
    if copies is None:
        out = pl.pallas_call(
            body, name=name, grid=grid, in_specs=in_specs, out_specs=out_specs, out_shape=out_shape,
            scratch_shapes=scratch_shapes, compiler_params=_cparams(semantics, vmem))(*args)
        return list(out), []
    n_ci, n_co = len(copies.inputs), len(copies.out_shape)

    def hosted(*refs):
        ins, cin = refs[:n_i], refs[n_i:n_i + n_ci]
        outs = refs[n_i + n_ci:n_i + n_ci + n_o]
        cout = refs[n_i + n_ci + n_o:n_i + n_ci + n_o + n_co]
        scr = refs[n_i + n_ci + n_o + n_co:n_i + n_ci + n_o + n_co + n_s]
        sems = refs[n_i + n_ci + n_o + n_co + n_s:]
        step = pl.program_id(0)
        for axis in range(1, len(grid)):
            step = step * grid[axis] + pl.program_id(axis)
        copies.emit(cin, cout, sems, step, math.prod(grid))
        body(*ins, *outs, *scr)

    out = pl.pallas_call(
        hosted, name=name, grid=grid, in_specs=list(in_specs) + [HBM] * n_ci,
        out_specs=list(out_specs) + [HBM] * n_co, out_shape=list(out_shape) + list(copies.out_shape),
        scratch_shapes=list(scratch_shapes) + list(copies.scratch),
        compiler_params=pltpu.CompilerParams(dimension_semantics=("arbitrary",) * len(grid),
                                             vmem_limit_bytes=vmem, has_side_effects=True))(*args, *copies.inputs)
    return list(out[:n_o]), list(out[n_o:])


def merge_copies(group):
    group = [c for c in group if c is not None]
    if len(group) <= 1:
        return group[0] if group else None
    bounds, i0, o0, s0 = [], 0, 0, 0
    for c in group:
        bounds.append((i0, o0, s0))
        i0, o0, s0 = i0 + len(c.inputs), o0 + len(c.out_shape), s0 + len(c.scratch)
    phases = []
    for c, (i, o, s) in zip(group, bounds):
        for frac, fn in c.phases:
            def shifted(ins, outs, sems, fn=fn, c=c, i=i, o=o, s=s):
                fn(ins[i:i + len(c.inputs)], outs[o:o + len(c.out_shape)], sems[s:s + len(c.scratch)])
            phases.append((frac, shifted))
    return HostedCopies([a for c in group for a in c.inputs], [a for c in group for a in c.out_shape],
                        [a for c in group for a in c.scratch], phases)


def ssm_fwd(name, proj, wbr, wbi, zr, zi, wcr, wci, dskip, D, copies=None):
    L = proj.shape[0]
    DS = D // 4
    NC = DS // CHUNK_U
    uo = (5 * D // 2) // CHUNK_U
    ch = _pick(L, 256)

    def body(u_ref, wbr_ref, wbi_ref, zr_ref, zi_ref, wcr_ref, wci_ref, ds_ref,
             y_ref, hg_ref, xr_ref, xi_ref, sr, si, pr_s, pi_s):
        def fill(ci, carry):
            rows = pl.ds(pl.multiple_of(ci * ch, ch), ch)
            ub = u_ref[rows, :].astype(BF16)
            sr[rows, :] = _dot(ub, wbr_ref[...])
            si[rows, :] = _dot(ub, wbi_ref[...])
            return carry

        lax.fori_loop(0, L // ch, fill, 0)
        _power_table(pr_s, pi_s, zr_ref[...], zi_ref[...], L, reverse=False)
        _scan(sr, si, pr_s, pi_s, L, reverse=False)

        def emit(ci, carry):
            rows = pl.ds(pl.multiple_of(ci * ch, ch), ch)
            xrb, xib = sr[rows, :].astype(BF16), si[rows, :].astype(BF16)
            xr_ref[rows, :] = xrb
            xi_ref[rows, :] = xib
            y = _dot(xrb, wcr_ref[...]) - _dot(xib, wci_ref[...]) + ds_ref[...] * u_ref[rows, :].astype(F32)
            y_ref[rows, :] = y
            hg_ref[rows, :] = _gelu(y).astype(BF16)
            return carry

        lax.fori_loop(0, L // ch, emit, 0)

    ucol = pl.BlockSpec((L, CHUNK_U), lambda k: (0, k))
    xcol = pl.BlockSpec((L, CHUNK_X), lambda k: (0, k))
    sd = jax.ShapeDtypeStruct
    return _call(
        body, name=name, grid=(NC,),
        in_specs=[pl.BlockSpec((L, CHUNK_U), lambda k: (0, uo + k)),
                  pl.BlockSpec((None, CHUNK_U, CHUNK_X), lambda k: (k, 0, 0)),
                  pl.BlockSpec((None, CHUNK_U, CHUNK_X), lambda k: (k, 0, 0)),
                  pl.BlockSpec((1, CHUNK_X), lambda k: (0, k)), pl.BlockSpec((1, CHUNK_X), lambda k: (0, k)),
                  pl.BlockSpec((None, CHUNK_X, CHUNK_U), lambda k: (k, 0, 0)),
                  pl.BlockSpec((None, CHUNK_X, CHUNK_U), lambda k: (k, 0, 0)),
                  pl.BlockSpec((1, CHUNK_U), lambda k: (0, k))],
        out_specs=[ucol, ucol, xcol, xcol],
        out_shape=[sd((L, DS), F32), sd((L, DS), BF16), sd((L, 4 * DS), BF16), sd((L, 4 * DS), BF16)],
        scratch_shapes=[pltpu.VMEM((L, CHUNK_X), F32), pltpu.VMEM((L, CHUNK_X), F32),
                        pltpu.VMEM((_scan_rows(L), CHUNK_X), F32), pltpu.VMEM((_scan_rows(L), CHUNK_X), F32)],
        vmem=VMEM_BIG, args=(proj, wbr, wbi, zr, zi, wcr, wci, dskip), semantics=("parallel",), copies=copies)


def ssm_bwd(name, dhg, ypre, proj, xr, xi, wbr, wbi, zr, zi, wcr, wci, dskip, D, copies=None):
    L = proj.shape[0]
    DS = D // 4
    NC = DS // CHUNK_U
    uo = (5 * D // 2) // CHUNK_U
    ch = _pick(L, 256)
    nch = L // ch
    halo = 16

    def body(dhg_ref, y_ref, u_ref, xr_ref, xi_ref, wbr_ref, wbi_ref, zr_ref, zi_ref, wcr_ref, wci_ref,
             ds_ref, du_ref, dwcr_ref, dwci_ref, dwbr_ref, dwbi_ref, dar_ref, dai_ref, dds_ref,
             gr, gi, duf, pr_s, pi_s):
        dwcr_ref[...] = jnp.zeros_like(dwcr_ref)
        dwci_ref[...] = jnp.zeros_like(dwci_ref)
        dwbr_ref[...] = jnp.zeros_like(dwbr_ref)
        dwbi_ref[...] = jnp.zeros_like(dwbi_ref)
        dar_ref[...] = jnp.zeros_like(dar_ref)
        dai_ref[...] = jnp.zeros_like(dai_ref)
        dds_ref[...] = jnp.zeros_like(dds_ref)

        def first(ci, carry):
            rows = pl.ds(pl.multiple_of(ci * ch, ch), ch)
            dy = dhg_ref[rows, :] * _gelu_grad(y_ref[rows, :])
            dyb = dy.astype(BF16)
            dds_ref[...] += jnp.sum(dy * u_ref[rows, :].astype(F32), axis=0, keepdims=True)
            duf[rows, :] = ds_ref[...] * dy
            gr[rows, :] = _dot(dyb, wcr_ref[...], NT)
            gi[rows, :] = -_dot(dyb, wci_ref[...], NT)
            dwcr_ref[...] += _dot(xr_ref[rows, :], dyb, TN)
            dwci_ref[...] -= _dot(xi_ref[rows, :], dyb, TN)
            return carry

        lax.fori_loop(0, nch, first, 0)
        _power_table(pr_s, pi_s, zr_ref[...], -zi_ref[...], L, reverse=True)
        _scan(gr, gi, pr_s, pi_s, L, reverse=True)

        def lam_grad(gxr, gxi, xpr, xpi):
            pr, pi = _cmul_conj(xpr, xpi, gxr, gxi)
            dar_ref[...] += jnp.sum(pr, axis=0, keepdims=True)
            dai_ref[...] += jnp.sum(pi, axis=0, keepdims=True)

        def second(ci, carry):
            r0 = pl.multiple_of(ci * ch, ch)
            rows = pl.ds(r0, ch)
            gxr, gxi = gr[rows, :], gi[rows, :]
            gxrb, gxib = gxr.astype(BF16), gxi.astype(BF16)
            du_ref[rows, :] = (duf[rows, :] + _dot(gxrb, wbr_ref[...], NT) + _dot(gxib, wbi_ref[...], NT)).astype(BF16)
            ub = u_ref[rows, :].astype(BF16)
            dwbr_ref[...] += _dot(ub, gxrb, TN)
            dwbi_ref[...] += _dot(ub, gxib, TN)
            return carry

        lax.fori_loop(0, nch, second, 0)

        ridx = lax.broadcasted_iota(jnp.int32, (ch, CHUNK_X), 0)
        xpr = jnp.where(ridx >= 1, pltpu.roll(xr_ref[0:ch, :].astype(F32), 1, 0), 0.0)
        xpi = jnp.where(ridx >= 1, pltpu.roll(xi_ref[0:ch, :].astype(F32), 1, 0), 0.0)
        lam_grad(gr[0:ch, :], gi[0:ch, :], xpr, xpi)

        def third(ci, carry):
            r0 = pl.multiple_of(ci * ch, ch)
            ext = pl.ds(pl.multiple_of(r0 - halo, halo), ch + halo)
            xpr = pltpu.roll(xr_ref[ext, :].astype(F32), 1, 0)[halo:]
            xpi = pltpu.roll(xi_ref[ext, :].astype(F32), 1, 0)[halo:]
            lam_grad(gr[pl.ds(r0, ch), :], gi[pl.ds(r0, ch), :], xpr, xpi)
            return carry

        if nch > 1:
            lax.fori_loop(1, nch, third, 0)

    ucol = pl.BlockSpec((L, CHUNK_U), lambda k: (0, k))
    xcol = pl.BlockSpec((L, CHUNK_X), lambda k: (0, k))
    wb_spec = pl.BlockSpec((None, CHUNK_U, CHUNK_X), lambda k: (k, 0, 0))
    wc_spec = pl.BlockSpec((None, CHUNK_X, CHUNK_U), lambda k: (k, 0, 0))
    avec = pl.BlockSpec((1, CHUNK_X), lambda k: (0, k))
    uvec = pl.BlockSpec((1, CHUNK_U), lambda k: (0, k))
    sd = jax.ShapeDtypeStruct
    return _call(
        body, name=name, grid=(NC,),
        in_specs=[ucol, ucol, pl.BlockSpec((L, CHUNK_U), lambda k: (0, uo + k)), xcol, xcol,
                  wb_spec, wb_spec, avec, avec, wc_spec, wc_spec, uvec],
        out_specs=[ucol, wc_spec, wc_spec, wb_spec, wb_spec, avec, avec, uvec],
        out_shape=[sd((L, DS), BF16), sd((NC, CHUNK_X, CHUNK_U), F32), sd((NC, CHUNK_X, CHUNK_U), F32),
                   sd((NC, CHUNK_U, CHUNK_X), F32), sd((NC, CHUNK_U, CHUNK_X), F32),
                   sd((1, 4 * DS), F32), sd((1, 4 * DS), F32), sd((1, DS), F32)],
        scratch_shapes=[pltpu.VMEM((L, CHUNK_X), F32), pltpu.VMEM((L, CHUNK_X), F32), pltpu.VMEM((L, CHUNK_U), F32),
                        pltpu.VMEM((_scan_rows(L), CHUNK_X), F32), pltpu.VMEM((_scan_rows(L), CHUNK_X), F32)],
        vmem=VMEM_BIG, args=(dhg, ypre, proj, xr, xi, wbr, wbi, zr, zi, wcr, wci, dskip),
        semantics=("parallel",), copies=copies)


def _block_diag(w, transpose):
    G = w.shape[0]
    nc = G // GROUPS_PER_CHUNK
    w4 = w.reshape(nc, GROUPS_PER_CHUNK, SSM_GROUP, SSM_STATE)
    eye = jnp.eye(GROUPS_PER_CHUNK, dtype=w.dtype)
    if transpose:
        return (w4[:, None, :, :, :].transpose(0, 1, 4, 2, 3) * eye[None, :, None, :, None]).reshape(
            nc, CHUNK_X, CHUNK_U).astype(BF16)
    return (w4[:, :, :, None, :] * eye[None, :, None, :, None]).reshape(nc, CHUNK_U, CHUNK_X).astype(BF16)


def _diag_blocks(dw, transpose):
    nc = dw.shape[0]
    gpc = GROUPS_PER_CHUNK
    eye = jnp.eye(gpc, dtype=dw.dtype)
    if transpose:
        d5 = dw.reshape(nc, gpc, SSM_STATE, gpc, SSM_GROUP)
        kept = jnp.sum(d5 * eye[None, :, None, :, None], axis=1)
        return kept.transpose(0, 2, 3, 1).reshape(nc * gpc, SSM_GROUP, SSM_STATE)
    d5 = dw.reshape(nc, gpc, SSM_GROUP, gpc, SSM_STATE)
    kept = jnp.sum(d5 * eye[None, :, None, :, None], axis=3)
    return kept.reshape(nc * gpc, SSM_GROUP, SSM_STATE)


SHARD_BLOCK_ELEMS = 128 * 1024


def _shard_rows(R, C, scale):
    return _pick(R, max(8, scale * SHARD_BLOCK_ELEMS // C))


def cast_bf16(name, w, layer, parts=1):
    shape = w.shape[1:]
    w3 = w.reshape(w.shape[0], -1, shape[-1])
    _, R, C = w3.shape
    tr = _shard_rows(R, C, 4)
    cw = C // parts

    def body(w_ref, *o_refs):
        for g, o_ref in enumerate(o_refs):
            o_ref[...] = w_ref[:, g * cw:(g + 1) * cw].astype(BF16)

    out = pl.pallas_call(body, name=name, grid=(R // tr,),
                         in_specs=[pl.BlockSpec((None, tr, C), lambda i: (layer, i, 0))],
                         out_specs=[pl.BlockSpec((tr, cw), lambda i: (i, 0))] * parts,
                         out_shape=[jax.ShapeDtypeStruct((R, cw), BF16)] * parts,
                         compiler_params=_cparams(("parallel",), VMEM_MID))(w3)
    return [o.reshape(shape[:-1] + (cw,)) for o in out]


def _adamw(w, g, m, v):
    m = ADAM_B1 * m + (1.0 - ADAM_B1) * g
    v = ADAM_B2 * v + (1.0 - ADAM_B2) * (g * g)
    delta = -ADAM_LR * ((m * ADAM_C1) / (jnp.sqrt(v * ADAM_C2) + ADAM_EPS) + ADAM_WD * w)
    return delta, m, v


def _own_core(g4):
    return (lambda p: p[0]) if g4.shape[1] == 2 else (lambda p: 0)


def chip_partial(name, pos, g4, recv_a):
    _, _, R, C = g4.shape
    tr = _shard_rows(R, C, 4)
    core = _own_core(g4)

    def body(pos_ref, g_ref, a_ref, o_ref):
        o_ref[...] = (g_ref[...] + a_ref[...]).astype(BF16)

    return pl.pallas_call(
        body, name=name,
        grid_spec=pltpu.PrefetchScalarGridSpec(
            num_scalar_prefetch=1, grid=(4, R // tr),
            in_specs=[pl.BlockSpec((None, None, tr, C), lambda q, i, p: (q, core(p), i, 0)),
                      pl.BlockSpec((None, tr, C), lambda q, i, p: (q, i, 0))],
            out_specs=pl.BlockSpec((None, tr, C), lambda q, i, p: (q, i, 0))),
        out_shape=jax.ShapeDtypeStruct((4, R, C), BF16),
        compiler_params=_cparams(("parallel", "parallel"), VMEM_MID))(pos, g4, recv_a)


def adamw_shard(name, pos, layer, g4, recv_a, recv_b, w, m, v, prev, copies=None):
    _, _, R, C = g4.shape
    tr = _shard_rows(R, C, 1)
    n_prev = 0 if prev is None else 4
    n_ci, n_co = (len(copies.inputs), len(copies.out_shape)) if copies else (0, 0)
    core = _own_core(g4)

    def body(pos_ref, g_ref, a_ref, b_ref, w_ref, m_ref, v_ref, *rest):
        rest = rest[n_prev:]
        go_ref, d_ref, mo_ref, vo_ref = rest[n_ci:n_ci + 4]
        if copies is not None:
            copies.emit(rest[:n_ci], rest[n_ci + 4:n_ci + 4 + n_co], rest[n_ci + 4 + n_co:],
                        pl.program_id(0), R // tr)
        gs = g_ref[...] + a_ref[...]
        for j in range(3):
            gs = gs + b_ref[j].astype(F32)
        delta, mn, vn = _adamw(w_ref[...], gs, m_ref[...], v_ref[...])
        go_ref[...] = gs
        d_ref[...] = delta
        mo_ref[...] = mn
        vo_ref[...] = vn

    lay = pl.BlockSpec((None, tr, C), lambda i, p: (layer, i, 0))
    in_specs = [pl.BlockSpec((None, None, tr, C), lambda i, p: (p[1], core(p), i, 0)),
                pl.BlockSpec((None, tr, C), lambda i, p: (p[1], i, 0)),
                pl.BlockSpec((3, tr, C), lambda i, p: (0, i, 0)), lay, lay, lay]
    args = [g4, recv_a, recv_b, w, m, v]
    aliases = {}
    if prev is not None:
        in_specs += [pl.BlockSpec(memory_space=pl.ANY)] * 4
        args += list(prev)
        aliases = {7 + j: j for j in range(4)}
    out = pl.pallas_call(
        body, name=name,
        grid_spec=pltpu.PrefetchScalarGridSpec(
            num_scalar_prefetch=1, grid=(R // tr,), in_specs=in_specs + [HBM] * n_ci,
            out_specs=[lay] * 4 + [HBM] * n_co, scratch_shapes=list(copies.scratch) if copies else []),
        out_shape=[jax.ShapeDtypeStruct(w.shape, F32)] * 4 + list(copies.out_shape if copies else []),
        input_output_aliases=aliases,
        compiler_params=pltpu.CompilerParams(
            dimension_semantics=("arbitrary",) if copies else ("parallel",), vmem_limit_bytes=VMEM_MID,
            has_side_effects=copies is not None))(pos, *args, *(copies.inputs if copies else []))
    return list(out[:4]), list(out[4:])


def adamw_small(name, gathered, head, w, m, v):
    _, R, C = gathered.shape
    E = head.shape[1]
    tr = _pick(R, 512)
    assert E % 8 == 0 and E <= tr

    def body(g_ref, h_ref, w_ref, m_ref, v_ref, go_ref, d_ref, mo_ref, vo_ref):
        gs, hs = g_ref[0], h_ref[0]
        for j in range(1, N_DEV):
            gs = gs + g_ref[j]
            hs = hs + h_ref[j]
        hs = jnp.where(pl.program_id(0) == 0, hs, 0.0)
        gs = jnp.concatenate([gs[:E] + hs, gs[E:]], axis=0)
        delta, mn, vn = _adamw(w_ref[...], gs, m_ref[...], v_ref[...])
        go_ref[...] = gs
        d_ref[...] = delta
        mo_ref[...] = mn
        vo_ref[...] = vn

    spec = pl.BlockSpec((tr, C), lambda i: (i, 0))
    return pl.pallas_call(
        body, name=name, grid=(R // tr,),
        in_specs=[pl.BlockSpec((N_DEV, tr, C), lambda i: (0, i, 0)),
                  pl.BlockSpec((N_DEV, E, C), lambda i: (0, 0, 0)), spec, spec, spec], out_specs=[spec] * 4,
        out_shape=[jax.ShapeDtypeStruct((R, C), F32)] * 4,
        compiler_params=_cparams(("parallel",), VMEM_MID))(gathered, head, w, m, v)


def _position():
    return lax.axis_index("x"), lax.axis_index("y"), lax.axis_index("c")


FORWARD_AT = 0.88


def gather_copies(shards):
    n = len(shards)

    def parts(ins, outs, sems):
        send_sems, recv_sems, local_sems = sems
        x, y, c = _position()
        me, sibling = (x, y, c), (x, y, 1 - c)
        chips = [(1 - x, y), (x, 1 - y), (1 - x, 1 - y)]

        def copy(a, k, block, to, src=None):
            blk = outs[a].at[4 * block[0] + 2 * block[1] + block[2]]
            return pltpu.make_async_remote_copy(
                src_ref=blk if src is None else src, dst_ref=blk,
                send_sem=send_sems.at[a, k], recv_sem=recv_sems.at[a, k], device_id=to, device_id_type=MESH)

        mine = [pltpu.make_async_copy(ins[a], outs[a].at[4 * x + 2 * y + c], local_sems.at[a]) for a in range(n)]
        first = [[copy(a, 0, me, sibling, src=ins[a])] +
                 [copy(a, 1 + j, me, (*chip, c), src=ins[a]) for j, chip in enumerate(chips)] for a in range(n)]
        landed = [[copy(a, 1 + j, (*chip, c), me) for j, chip in enumerate(chips)] for a in range(n)]
        passed = [[copy(a, 4 + j, (*chip, c), sibling) for j, chip in enumerate(chips)] for a in range(n)]
        from_sibling = [[copy(a, 0, sibling, me)] +
                        [copy(a, 4 + j, (*chip, 1 - c), me) for j, chip in enumerate(chips)] for a in range(n)]
        return mine, first, landed, passed, from_sibling

    def start(ins, outs, sems):
        mine, first, _, _, _ = parts(ins, outs, sems)
        for a in range(n):
            mine[a].start()
            for cp in first[a]:
                cp.start()

    def forward(ins, outs, sems):
        _, _, landed, passed, _ = parts(ins, outs, sems)
        for a in range(n):
            for j in range(3):
                landed[a][j].wait_recv()
                passed[a][j].start()

    def finish(ins, outs, sems):
        mine, first, _, passed, from_sibling = parts(ins, outs, sems)
        for a in range(n):
            for cp in from_sibling[a]:
                cp.wait_recv()
        for a in range(n):
            for cp in first[a] + passed[a]:
                cp.wait_send()
            mine[a].wait()

    return HostedCopies(
        list(shards), [jax.ShapeDtypeStruct((N_DEV,) + s.shape, s.dtype) for s in shards],
        [pltpu.SemaphoreType.DMA((n, 7)), pltpu.SemaphoreType.DMA((n, 7)), pltpu.SemaphoreType.DMA((n,))],
        [(0.0, start), (FORWARD_AT, forward), (1.0, finish)])


def _exchange_copies(arrays, out_lead, make):
    n = len(arrays)

    def all_copies(ins, outs, sems):
        send_sems, recv_sems = sems
        return [make(ins[a], outs[a], send_sems.at[a, k], recv_sems.at[a, k], k)
                for a in range(n) for k in range(out_lead)]

    def start(ins, outs, sems):
        for cp in all_copies(ins, outs, sems):
            cp.start()

    def finish(ins, outs, sems):
        for cp in all_copies(ins, outs, sems):
            cp.wait()

    return HostedCopies(
        list(arrays), [jax.ShapeDtypeStruct((out_lead,) + a.shape[2:], a.dtype) for a in arrays],
        [pltpu.SemaphoreType.DMA((n, out_lead)), pltpu.SemaphoreType.DMA((n, out_lead))],
        [(0.0, start), (1.0, finish)])


def sibling_copies(grads):
    def make(src, dst, send_sem, recv_sem, q):
        x, y, c = _position()
        core = 1 - c if src.shape[1] == 2 else 0
        return pltpu.make_async_remote_copy(
            src_ref=src.at[q, core], dst_ref=dst.at[q], send_sem=send_sem, recv_sem=recv_sem,
            device_id=(x, y, 1 - c), device_id_type=MESH)

    return _exchange_copies(grads, 4, make)


def chip_copies(parts):
    def make(src, dst, send_sem, recv_sem, j):
        x, y, c = _position()
        chip = [(1 - x, y), (x, 1 - y), (1 - x, 1 - y)][j]
        return pltpu.make_async_remote_copy(
            src_ref=src.at[2 * chip[0] + chip[1], 0], dst_ref=dst.at[j], send_sem=send_sem, recv_sem=recv_sem,
            device_id=(*chip, c), device_id_type=MESH)

    return _exchange_copies(parts, 3, make)


class Carrier:
    def __init__(self):
        self.plan = {}
        self.counts = {}

    def ride(self, site, make, store):
        self.plan.setdefault(site, []).append((make, store))

    def make(self, site, ctx=None):
        if site not in self.plan:
            return None
        group = [make(ctx) for make, _ in self.plan[site]]
        self.counts[site] = [len(c.out_shape) for c in group]
        return merge_copies(group)

    def store(self, site, results):
        if site in self.plan:
            at = 0
            for (_, store), n in zip(self.plan[site], self.counts[site]):
                store(results[at:at + n])
                at += n

    def split(self, site, out):
        if site not in self.plan:
            return out
        self.store(site, out[1])
        return out[0]


def _pool_weight(gathered):
    PG = gathered.shape[-1]
    return gathered.transpose(1, 0, 2, 3).reshape(N_POOL_GROUPS, PG, PG)


def _layer_params(l, ln_g, pool_scale, lam_re, lam_im, log_dt, b_re, b_im, c_re, c_im,
                  d_skip, b_glu, branch_g):
    G, P = lam_re.shape[1:]
    p = dict(
        ln_g=ln_g[l][None, :], pool_scale=pool_scale[l][None, :], d_skip=d_skip[l][None, :],
        b_glu=b_glu[l][None, :], branch_g=branch_g[l][None, :],
        lr=lam_re[l].reshape(G, 1, P), li=lam_im[l].reshape(G, 1, P), ld=log_dt[l].reshape(G, 1, 1),
        br=b_re[l].transpose(0, 2, 1), bi=b_im[l].transpose(0, 2, 1), cr=c_re[l], ci=c_im[l])
    return p


def layer_fwd(l, x, p, gw, D, carrier, staged_parts=0):
    t = f"l{l}_"
    h = rms_fwd(t + "rms_fwd", x, p["ln_g"])
    if staged_parts:
        proj = None
        for g in range(staged_parts):
            site = (l, f"proj{g}")
            proj = carrier.split(site, mm_nn_part(t + f"proj{g}", h, gw("w_in", l, g), g, staged_parts,
                                                  into=proj, out_dtype=BF16, copies=carrier.make(site)))
    else:
        site = (l, "proj")
        proj = carrier.split(site, mm_nn_gathered(t + "proj", h, gw("w_in", l), out_dtype=BF16,
                                                  copies=carrier.make(site)))
    wp = _pool_weight(gw("w_pool", l)[0])
    ypool = pool_fwd(t + "pool_fwd", proj, wp, p["pool_scale"])
    site = (l, "attn_fwd")
    yattn, landed = attn_fwd(t + "attn_fwd", proj, D, copies=carrier.make(site))
    carrier.store(site, landed)
    zr, zi, bbr, bbi = ssm_prep(t + "ssm_prep", p["lr"], p["li"], p["ld"], p["br"], p["bi"])
    ssm_w = dict(wbr=_block_diag(bbr, False), wbi=_block_diag(bbi, False),
                 zr=zr.reshape(1, -1), zi=zi.reshape(1, -1),
                 wcr=_block_diag(p["cr"], True), wci=_block_diag(p["ci"], True))
    site = (l, "ssm_fwd")
    (ypre, hg, xr, xi), landed = ssm_fwd(
        t + "ssm_fwd", proj, ssm_w["wbr"], ssm_w["wbi"], ssm_w["zr"], ssm_w["zi"],
        ssm_w["wcr"], ssm_w["wci"], p["d_skip"], D, copies=carrier.make(site))
    carrier.store(site, landed)
    glu_pre = mm_nn_gathered(t + "glu", hg, gw("w_glu", l))
    y = branch_fwd(t + "branch_fwd", ypool, yattn, glu_pre, proj, p["b_glu"], p["branch_g"])
    out = mm_plain(t + "out", y, gw("w_out", l)[0].reshape(D, D), NN, res=x)
    saved = dict(x=x, h=h, proj=proj, ypool=ypool, yattn=yattn, ypre=ypre, hg=hg, xr=xr, xi=xi,
                 glu_pre=glu_pre, y=y, ssm_w=ssm_w, wp=wp)
    return out, saved


def layer_bwd(l, dres, dres_b, s, p, gw, D, carrier, pos, split_w_in):
    t = f"l{l}_"
    proj = s["proj"]

    def by_target(g):
        return g.reshape(4, 2, -1, g.shape[-1])

    big = {}
    w_out_g = gw("w_out", l)[0].reshape(D, D)
    site = (l, "dy")
    dy = carrier.split(site, mm_plain(t + "dy", dres_b, w_out_g, NT, copies=carrier.make(site)))
    big["w_out"] = by_target(mm_plain(t + "dw_out", s["y"], dres_b, TN).reshape(N_DEV, D // N_DEV, D))
    dypool, dyattn, dglu, dpg, dag, dsg, dbg, dbglu = branch_bwd(
        t + "branch_bwd", dy, s["ypool"], s["yattn"], s["glu_pre"], proj, p["b_glu"], p["branch_g"])
    dhg = mm_nt_gathered(t + "dhg", dglu, gw("w_glu", l))
    big["w_glu"] = by_target(mm_tn_scattered(t + "dw_glu", s["hg"], dglu))
    w = s["ssm_w"]
    site = (l, "ssm_bwd")
    (du, dwcr, dwci, dwbr, dwbi, dar, dai, dds), landed = ssm_bwd(
        t + "ssm_bwd", dhg, s["ypre"], proj, s["xr"], s["xi"], w["wbr"], w["wbi"], w["zr"], w["zi"],
        w["wcr"], w["wci"], p["d_skip"], D, copies=carrier.make(site, big))
    carrier.store(site, landed)
    G, _, P = p["lr"].shape
    dlr, dli, dld, dbr, dbi = ssm_prep_bwd(
        t + "ssm_prep_bwd", p["lr"], p["li"], p["ld"], p["br"], p["bi"],
        dar.reshape(G, 1, P), dai.reshape(G, 1, P), _diag_blocks(dwbr, False), _diag_blocks(dwbi, False))
    site = (l, "attn_bwd")
    (dq, dk, dv), landed = attn_bwd(t + "attn_bwd", proj, s["yattn"], dyattn, D, copies=carrier.make(site, big))
    carrier.store(site, landed)
    dxp, dwp, dps = pool_bwd(t + "pool_bwd", dypool, proj, s["wp"], p["pool_scale"])
    dproj = jnp.concatenate([dxp, dpg, dq, dk, dv, dag, du, dsg], axis=1)
    PG = dwp.shape[1]
    big["w_pool"] = by_target(dwp.reshape(N_POOL_GROUPS, N_DEV, PG // N_DEV, PG).transpose(1, 0, 2, 3))
    small = dict(pool_scale=dps[0], lam_re=dlr.reshape(G, P), lam_im=dli.reshape(G, P),
                 log_dt=dld.reshape(G), b_re=dbr.transpose(0, 2, 1), b_im=dbi.transpose(0, 2, 1),
                 c_re=_diag_blocks(dwcr, True), c_im=_diag_blocks(dwci, True),
                 d_skip=dds[0], b_glu=dbglu[0], branch_g=dbg[0])
    if split_w_in:
        site = (l, "dw_in_a")
        to_sibling = carrier.split(site, mm_tn_half(t + "dw_in_a", s["h"], dproj, pos, False,
                                                    copies=carrier.make(site, big)))
        site = (l, "dw_in_b")
        mine = carrier.split(site, mm_tn_half(t + "dw_in_b", s["h"], dproj, pos, True,
                                              copies=carrier.make(site, dict(big, to_sibling=to_sibling[:, None], small=small))))
        big["w_in"] = mine[:, None]
    else:
        big["w_in"] = by_target(mm_tn_scattered(t + "dw_in", s["h"], dproj))
    site = (l, "dh")
    dh = carrier.split(site, mm_nt_gathered(t + "dh", dproj, gw("w_in", l), copies=carrier.make(site, big)))
    dx, dx_b, dlng = rms_bwd(t + "rms_bwd", s["x"], dh, dres, p["ln_g"])
    small["ln_g"] = dlng[0]
    return dx, dx_b, small


SMALL_NAMES = ("ln_g", "pool_scale", "lam_re", "lam_im", "log_dt", "b_re", "b_im", "c_re", "c_im",
               "d_skip", "b_glu", "branch_g", "final_g")
BIG_NAMES = ("w_in", "w_pool", "w_glu", "w_out")
WEIGHT_ORDER = ("ln_g", "w_in", "w_pool", "pool_scale", "lam_re", "lam_im", "log_dt", "b_re", "b_im",
                "c_re", "c_im", "d_skip", "w_glu", "b_glu", "branch_g", "w_out", "final_g")


PACK_ROWS = 512


def _pack(arrs):
    flat = jnp.concatenate([a.reshape(-1) for a in arrs])
    pad = (-flat.shape[0]) % (PACK_ROWS * LANE)
    return jnp.pad(flat, (0, pad)).reshape(-1, LANE)


def _unpack(packed, like):
    flat = packed.reshape(-1)
    out, off = [], 0
    for a in like:
        out.append(flat[off:off + a.size].reshape(a.shape))
        off += a.size
    return out


def kernel(x, ln_g, w_in, w_pool, pool_scale, lam_re, lam_im, log_dt, b_re, b_im, c_re, c_im, d_skip, w_glu, b_glu, branch_g, w_out, final_g, loss_target, m_ln_g, m_w_in, m_w_pool, m_pool_scale, m_lam_re, m_lam_im, m_log_dt, m_b_re, m_b_im, m_c_re, m_c_im, m_d_skip, m_w_glu, m_b_glu, m_branch_g, m_w_out, m_final_g, v_ln_g, v_w_in, v_w_pool, v_pool_scale, v_lam_re, v_lam_im, v_log_dt, v_b_re, v_b_im, v_c_re, v_c_im, v_d_skip, v_w_glu, v_b_glu, v_branch_g, v_w_out, v_final_g):
    W = dict(ln_g=ln_g, w_in=w_in, w_pool=w_pool, pool_scale=pool_scale, lam_re=lam_re, lam_im=lam_im,
             log_dt=log_dt, b_re=b_re, b_im=b_im, c_re=c_re, c_im=c_im, d_skip=d_skip, w_glu=w_glu,
             b_glu=b_glu, branch_g=branch_g, w_out=w_out, final_g=final_g)
    Mo = dict(ln_g=m_ln_g, w_in=m_w_in, w_pool=m_w_pool, pool_scale=m_pool_scale, lam_re=m_lam_re,
              lam_im=m_lam_im, log_dt=m_log_dt, b_re=m_b_re, b_im=m_b_im, c_re=m_c_re, c_im=m_c_im,
              d_skip=m_d_skip, w_glu=m_w_glu, b_glu=m_b_glu, branch_g=m_branch_g, w_out=m_w_out,
              final_g=m_final_g)
    Vo = dict(ln_g=v_ln_g, w_in=v_w_in, w_pool=v_w_pool, pool_scale=v_pool_scale, lam_re=v_lam_re,
              lam_im=v_lam_im, log_dt=v_log_dt, b_re=v_b_re, b_im=v_b_im, c_re=v_c_re, c_im=v_c_im,
              d_skip=v_d_skip, w_glu=v_w_glu, b_glu=v_b_glu, branch_g=v_branch_g, w_out=v_w_out,
              final_g=v_final_g)
    depth = ln_g.shape[0]
    _, L, D = x.shape
    xc, yc, cc = _position()
    pos = jnp.stack([cc, 2 * xc + yc, 4 * xc + 2 * yc + cc]).astype(jnp.int32)

    def n_parts(n, l):
        return (W_IN_PARTS if l > 0 else W_IN0_PARTS) if n == "w_in" else 1

    shards, landed = {}, {}
    for n in BIG_NAMES:
        for l in range(depth):
            for g, part in enumerate(cast_bf16(f"cast_{n}_{l}", W[n], l, n_parts(n, l))):
                shards[n, l, g] = part
    carrier = Carrier()

    def gw(n, l, g=None):
        return landed[n, l, g] if g is not None else [landed[n, l, i] for i in range(n_parts(n, l))]

    def gather_plan(keys):
        return (lambda ctx: gather_copies([shards[k] for k in keys])), (lambda outs: landed.update(zip(keys, outs)))

    first = [("w_in", 0, 0)] + [("w_pool", l, 0) for l in range(depth)]
    landed.update(zip(first, copies_call("gather_first", gather_copies([shards[k] for k in first]))))
    for g in range(1, W_IN0_PARTS):
        carrier.ride((0, f"proj{g - 1}"), *gather_plan([("w_in", 0, g)]))
    carrier.ride((0, f"proj{W_IN0_PARTS - 1}"), *gather_plan([("w_out", 0, 0), ("w_glu", 0, 0)]))
    for l in range(1, depth):
        for g in range(W_IN_PARTS):
            call = "ssm_fwd" if g == W_IN_PARTS - 1 else "attn_fwd"
            carrier.ride((l - 1, call), *gather_plan([("w_in", l, g)]))
        carrier.ride((l, "proj"), *gather_plan([("w_out", l, 0), ("w_glu", l, 0)]))

    own, recv_a, recv_b = {}, {}, {}

    def sibling_plan(l, names, keep, pick):
        def make(ctx):
            own.update({(n, l): ctx[n] for n in keep})
            return sibling_copies(pick(ctx))
        return make, (lambda outs: recv_a.update(zip([(n, l) for n in names], outs)))

    def chip_plan(l, names):
        def make(ctx):
            parts = [chip_partial(f"chip_partial_{n}_{l}", pos, own[n, l], recv_a[n, l])[:, None] for n in names]
            return chip_copies(parts)
        return make, (lambda outs: recv_b.update(zip([(n, l) for n in names], outs)))

    early, late = ("w_out", "w_glu"), ("w_in", "w_pool")
    for l in range(1, depth):
        carrier.ride((l, "dh"), *sibling_plan(l, BIG_NAMES, BIG_NAMES, lambda big: [big[n] for n in BIG_NAMES]))
        carrier.ride((l - 1, "ssm_bwd"), *chip_plan(l, ("w_out", "w_glu", "w_pool")))
        carrier.ride((l - 1, "attn_bwd"), *chip_plan(l, ("w_in",)))
    carrier.ride((0, "ssm_bwd"), *sibling_plan(0, early, early, lambda big: [big[n] for n in early]))
    carrier.ride((0, "dw_in_a"), *chip_plan(0, early))
    carrier.ride((0, "dw_in_b"), *sibling_plan(0, late, ("w_pool",), lambda ctx: [ctx["to_sibling"], ctx["w_pool"]]))

    def last_chip_make(big):
        own["w_in", 0] = big["w_in"]
        return chip_plan(0, late)[0](big)

    carrier.ride((0, "dh"), last_chip_make, chip_plan(0, late)[1])

    small = [None] * depth
    small_like = [W[n] for n in SMALL_NAMES]
    small_landed = []

    def small_make(ctx):
        layer0 = dict(ctx["small"], ln_g=jnp.zeros((D,), F32))
        grads = [jnp.stack([layer0[n]] + [small[l][n] for l in range(1, depth)]) for n in SMALL_NAMES[:-1]]
        return gather_copies([_pack(grads + [dfinal[0]])])

    carrier.ride((0, "dw_in_b"), small_make, small_landed.extend)

    params = [_layer_params(l, ln_g, pool_scale, lam_re, lam_im, log_dt, b_re, b_im, c_re, c_im,
                            d_skip, b_glu, branch_g) for l in range(depth)]
    h = x[0]
    saved = []
    for l in range(depth):
        h, s = layer_fwd(l, h, params[l], gw, D, carrier, staged_parts=W_IN0_PARTS if l == 0 else 0)
        saved.append(s)
    loss_part, dres, dres_b, dfinal = loss_head("loss_head", h, final_g[None, :], loss_target[0])
    loss = lax.psum(loss_part[0, 0], ("x", "y", "c"))

    for l in reversed(range(depth)):
        dres, dres_b, small[l] = layer_bwd(l, dres, dres_b, saved[l], params[l], gw, D, carrier, pos,
                                           split_w_in=(l == 0))
    grad_x = dres[None]

    results = {}
    for n in BIG_NAMES:
        shape = W[n].shape
        R, C = int(math.prod(shape[1:-1])), shape[-1]
        w3, m3, v3 = (t.reshape(depth, R, C) for t in (W[n], Mo[n], Vo[n]))
        prev = None
        for l in range(depth):
            prev, _ = adamw_shard(f"adamw_{n}_{l}", pos, l, own[n, l], recv_a[n, l], recv_b[n, l], w3, m3, v3, prev)
        results[n] = [t.reshape(shape) for t in prev]

    head = copies_call("gather_ln_g", gather_copies([small[0]["ln_g"].reshape(-1, LANE)]))[0]
    packed = adamw_small("adamw_small", small_landed[0], head, _pack(small_like),
                         _pack([Mo[n] for n in SMALL_NAMES]), _pack([Vo[n] for n in SMALL_NAMES]))
    unpacked = [_unpack(t, small_like) for t in packed]
    for i, n in enumerate(SMALL_NAMES):
        results[n] = [unpacked[j][i] for j in range(4)]

    out = [loss, grad_x]
    for j in range(4):
        out += [results[n][j] for n in WEIGHT_ORDER]
    return tuple(out)
```
